```python
import jax, jax.numpy as jnp
from jax import lax
import numpy as np

D_MODEL = 2048
BATCH = 8
SEQ = 2048
DEPTH = 2

CHUNK = 64
Q_BLOCK = 128
EPS = 1e-6
MIX_WIDTH = D_MODEL
N_GROUPS = 4
GROUP_WIDTH = MIX_WIDTH // N_GROUPS
CONV_WIDTH = 3
POOL_WINDOWS = (2, 4, 8, 16)
POOL_GROUP_DIM = GROUP_WIDTH // len(POOL_WINDOWS)
GLA_HEADS = 4
GLA_VALUE_DIM = GROUP_WIDTH // GLA_HEADS
GLA_KEY_DIM = GLA_VALUE_DIM // 2
GLA_RANK = 16
GLA_TAU = 16.0
FOX_HEADS = 4
FOX_HEAD_DIM = GROUP_WIDTH // FOX_HEADS
FOX_GATE_BIAS = 3.0
D_FF = -(-8 * D_MODEL // (3 * 256)) * 256

IN_SPLITS = (
    GROUP_WIDTH, GROUP_WIDTH, GROUP_WIDTH,
    GROUP_WIDTH,
    GLA_HEADS * GLA_KEY_DIM, GLA_HEADS * GLA_KEY_DIM, GROUP_WIDTH, GROUP_WIDTH, GLA_RANK,
    GROUP_WIDTH, GROUP_WIDTH, GROUP_WIDTH, FOX_HEADS,
)
IN_COLS = sum(IN_SPLITS)

kernel_name = 'hybrid_conv_pool_gla_fox_trunk'


def _split_points():
    pts, acc = [], 0
    for s in IN_SPLITS[:-1]:
        acc += s
        pts.append(acc)
    return pts


def rmsnorm(x, g):
    xf = x.astype(jnp.float32)
    y = xf * lax.rsqrt(jnp.mean(xf * xf, axis=-1, keepdims=True) + EPS)
    return (y * g.astype(jnp.float32)).astype(x.dtype)


def short_gated_conv(b_gate, c_gate, h, conv_w):
    u = c_gate * h
    y = lax.conv_general_dilated(u, conv_w.astype(u.dtype), window_strides=(1,),
                                 padding=[(CONV_WIDTH - 1, 0)],
                                 dimension_numbers=('NWC', 'WIO', 'NWC'),
                                 feature_group_count=u.shape[-1])
    return b_gate * y


def multiscale_pool(p, pool_w, pool_scale):
    Bn, S, G = p.shape
    P = POOL_WINDOWS[-1]
    pg = p.astype(jnp.float32).reshape(Bn, S, len(POOL_WINDOWS), POOL_GROUP_DIM)
    cs = jnp.cumsum(jnp.pad(pg, ((0, 0), (P + 1, 0), (0, 0), (0, 0))), axis=1)
    t = jnp.arange(S, dtype=jnp.float32)
    diffs = []
    for gi, w in enumerate(POOL_WINDOWS):
        wsum = cs[:, P + 1:P + 1 + S, gi] - cs[:, P + 1 - w:P + 1 - w + S, gi]
        count = jnp.minimum(t + 1.0, float(w))[:, None]
        diffs.append(wsum / count - pg[:, :, gi])
    d = jnp.stack(diffs, axis=2)
    y = jnp.einsum('bsgc,gcd->bsgd', d, pool_w.astype(jnp.float32))
    return (y.reshape(Bn, S, G) * pool_scale.astype(jnp.float32)).astype(p.dtype)


def gated_linear_attention(q, k, v, g, a_low, w_decay, b_decay, out_g):
    Bn, S, _ = q.shape
    H, K, V = GLA_HEADS, GLA_KEY_DIM, GLA_VALUE_DIM
    N = S // CHUNK
    f32 = jnp.float32

    def chunks(t, d):
        return t.astype(f32).reshape(Bn, N, CHUNK, H, d).transpose(1, 0, 2, 3, 4)

    log_a = jax.nn.log_sigmoid((a_low @ w_decay + b_decay).astype(f32)) / GLA_TAU
    cum = jnp.cumsum(chunks(log_a, K), axis=2)
    qs = chunks(q, K) * (K ** -0.5)
    ks = chunks(k, K)
    vs = chunks(v, V)

    def step(state, inp):
        qc, kc, vc, cc = inp
        decay = jnp.exp(-jnp.abs(cc[:, :, None] - cc[:, None, :]))
        scores = jnp.einsum('bthk,btshk,bshk->bhts', qc, decay, kc)
        o_intra = jnp.einsum('bhts,bshv->bthv', scores, vc)
        o_inter = jnp.einsum('bthk,bhkv->bthv', qc * jnp.exp(cc), state)
        last = cc[:, -1]
        k_dec = kc * jnp.exp(last[:, None] - cc)
        state = state * jnp.exp(last)[..., None] + jnp.einsum('bshk,bshv->bhkv', k_dec, vc)
        return state, o_intra + o_inter

    state0 = jnp.zeros((Bn, H, K, V), f32)
    _, o = lax.scan(step, state0, (qs, ks, vs, cum))
    o = o.transpose(1, 0, 2, 3, 4).reshape(Bn, S, H, V)
    o = rmsnorm(o, out_g) * jax.nn.silu(g.astype(f32)).reshape(Bn, S, H, V)
    return o.reshape(Bn, S, H * V).astype(q.dtype)


def forgetting_attention(q, k, v, f_logit, q_g, k_g, b_f):
    Bn, S, _ = q.shape
    H, Dh = FOX_HEADS, FOX_HEAD_DIM
    q = rmsnorm(q.reshape(Bn, S, H, Dh), q_g)
    k = rmsnorm(k.reshape(Bn, S, H, Dh), k_g)
    v = v.reshape(Bn, S, H, Dh)
    log_f = jax.nn.log_sigmoid((f_logit + b_f).astype(jnp.float32))
    F = jnp.cumsum(log_f, axis=1).transpose(0, 2, 1)
    n_blk = S // Q_BLOCK
    q_blocks = q.reshape(Bn, n_blk, Q_BLOCK, H, Dh).transpose(1, 0, 2, 3, 4)
    F_blocks = F.reshape(Bn, H, n_blk, Q_BLOCK).transpose(2, 0, 1, 3)
    key_pos = jnp.arange(S)
    scale = Dh ** -0.5

    def attend_block(args):
        q_blk, F_blk, blk = args
        logits = jnp.einsum('bqhd,bshd->bhqs', q_blk, k).astype(jnp.float32) * scale
        logits = logits + F_blk[..., :, None] - F[..., None, :]
        q_pos = blk * Q_BLOCK + jnp.arange(Q_BLOCK)
        logits = jnp.where(key_pos[None, :] <= q_pos[:, None], logits, -jnp.inf)
        p = jax.nn.softmax(logits, axis=-1).astype(v.dtype)
        return jnp.einsum('bhqs,bshd->bqhd', p, v)

    out = lax.map(attend_block, (q_blocks, F_blocks, jnp.arange(n_blk)))
    return out.transpose(1, 0, 2, 3, 4).reshape(Bn, S, H * Dh)


def hybrid_mixer(u, w_in, conv_w, pool_w, pool_scale, gla_w_decay, gla_b_decay, gla_out_g,
                 fox_q_g, fox_k_g, fox_b_f, w_out):
    z = u @ w_in
    (cb, cc, ch, pu, gq, gk, gv, gg, ga, fq, fk, fv, ff) = jnp.split(z, _split_points(), axis=-1)
    y_a = short_gated_conv(cb, cc, ch, conv_w)
    y_b = multiscale_pool(pu, pool_w, pool_scale)
    y_c = gated_linear_attention(gq, gk, gv, gg, ga, gla_w_decay, gla_b_decay, gla_out_g)
    y_d = forgetting_attention(fq, fk, fv, ff, fox_q_g, fox_k_g, fox_b_f)
    y = jnp.concatenate([y_a.astype(u.dtype), y_b.astype(u.dtype),
                         y_c.astype(u.dtype), y_d.astype(u.dtype)], axis=-1)
    return y @ w_out


def swiglu_ffn(u, w_gate, w_up, w_down):
    return (jax.nn.silu(u @ w_gate) * (u @ w_up)) @ w_down


def _fwd_setup_inputs(seed: int = 0) -> dict:
    key = jax.random.key(seed)
    ks = jax.random.split(key, 17)
    f32 = jnp.float32

    def nrm(k, shape, scale):
        return jax.random.normal(k, shape, f32) * scale

    G = GROUP_WIDTH
    return {
        'x': nrm(ks[0], (BATCH, SEQ, D_MODEL), 1.0),
        'norm_mix_g': 1.0 + nrm(ks[1], (DEPTH, D_MODEL), 0.02),
        'w_in': nrm(ks[2], (DEPTH, D_MODEL, IN_COLS), D_MODEL ** -0.5),
        'conv_w': nrm(ks[3], (DEPTH, CONV_WIDTH, 1, G), CONV_WIDTH ** -0.5),
        'pool_w': nrm(ks[4], (DEPTH, len(POOL_WINDOWS), POOL_GROUP_DIM, POOL_GROUP_DIM), POOL_GROUP_DIM ** -0.5),
        'pool_scale': 1.0 + nrm(ks[5], (DEPTH, G), 0.02),
        'gla_w_decay': nrm(ks[6], (DEPTH, GLA_RANK, GLA_HEADS * GLA_KEY_DIM), GLA_RANK ** -0.5),
        'gla_b_decay': nrm(ks[7], (DEPTH, GLA_HEADS * GLA_KEY_DIM), 0.1),
        'gla_out_g': 1.0 + nrm(ks[8], (DEPTH, GLA_VALUE_DIM), 0.02),
        'fox_q_g': 1.0 + nrm(ks[9], (DEPTH, FOX_HEAD_DIM), 0.02),
        'fox_k_g': 1.0 + nrm(ks[10], (DEPTH, FOX_HEAD_DIM), 0.02),
        'fox_b_f': FOX_GATE_BIAS + nrm(ks[11], (DEPTH, FOX_HEADS), 0.1),
        'w_out': nrm(ks[12], (DEPTH, MIX_WIDTH, D_MODEL), MIX_WIDTH ** -0.5),
        'norm_ffn_g': 1.0 + nrm(ks[13], (DEPTH, D_MODEL), 0.02),
        'w_gate': nrm(ks[14], (DEPTH, D_MODEL, D_FF), D_MODEL ** -0.5),
        'w_up': nrm(ks[15], (DEPTH, D_MODEL, D_FF), D_MODEL ** -0.5),
        'w_down': nrm(ks[16], (DEPTH, D_FF, D_MODEL), D_FF ** -0.5),
    }


def _fwd_reference(x, norm_mix_g, w_in, conv_w, pool_w, pool_scale, gla_w_decay, gla_b_decay, gla_out_g,
              fox_q_g, fox_k_g, fox_b_f, w_out, norm_ffn_g, w_gate, w_up, w_down):
    h = x
    for l in range(DEPTH):
        u = rmsnorm(h, norm_mix_g[l])
        h = h + hybrid_mixer(u, w_in[l], conv_w[l], pool_w[l], pool_scale[l], gla_w_decay[l],
                             gla_b_decay[l], gla_out_g[l], fox_q_g[l], fox_k_g[l], fox_b_f[l], w_out[l])
        u = rmsnorm(h, norm_ffn_g[l])
        h = h + swiglu_ffn(u, w_gate[l], w_up[l], w_down[l])
    return h


import jax as _jax
import jax.numpy as _jnp

TWIN_FORMAT = 'train_step'
FWD_PARAMS = ['x', 'norm_mix_g', 'w_in', 'conv_w', 'pool_w', 'pool_scale', 'gla_w_decay', 'gla_b_decay', 'gla_out_g', 'fox_q_g', 'fox_k_g', 'fox_b_f', 'w_out', 'norm_ffn_g', 'w_gate', 'w_up', 'w_down']
TWIN_WEIGHTS = ['norm_mix_g', 'w_in', 'conv_w', 'pool_w', 'pool_scale', 'gla_w_decay', 'gla_b_decay', 'gla_out_g', 'fox_q_g', 'fox_k_g', 'fox_b_f', 'w_out', 'norm_ffn_g', 'w_gate', 'w_up', 'w_down']
TWIN_DIFF_INPUT = 'x'
TWIN_INPUTS = ['x', 'norm_mix_g', 'w_in', 'conv_w', 'pool_w', 'pool_scale', 'gla_w_decay', 'gla_b_decay', 'gla_out_g', 'fox_q_g', 'fox_k_g', 'fox_b_f', 'w_out', 'norm_ffn_g', 'w_gate', 'w_up', 'w_down', 'loss_target', 'm_norm_mix_g', 'm_w_in', 'm_conv_w', 'm_pool_w', 'm_pool_scale', 'm_gla_w_decay', 'm_gla_b_decay', 'm_gla_out_g', 'm_fox_q_g', 'm_fox_k_g', 'm_fox_b_f', 'm_w_out', 'm_norm_ffn_g', 'm_w_gate', 'm_w_up', 'm_w_down', 'v_norm_mix_g', 'v_w_in', 'v_conv_w', 'v_pool_w', 'v_pool_scale', 'v_gla_w_decay', 'v_gla_b_decay', 'v_gla_out_g', 'v_fox_q_g', 'v_fox_k_g', 'v_fox_b_f', 'v_w_out', 'v_norm_ffn_g', 'v_w_gate', 'v_w_up', 'v_w_down']
TWIN_OUTPUTS = ['loss', 'grad_x', 'grad_norm_mix_g', 'grad_w_in', 'grad_conv_w', 'grad_pool_w', 'grad_pool_scale', 'grad_gla_w_decay', 'grad_gla_b_decay', 'grad_gla_out_g', 'grad_fox_q_g', 'grad_fox_k_g', 'grad_fox_b_f', 'grad_w_out', 'grad_norm_ffn_g', 'grad_w_gate', 'grad_w_up', 'grad_w_down', 'delta_norm_mix_g', 'delta_w_in', 'delta_conv_w', 'delta_pool_w', 'delta_pool_scale', 'delta_gla_w_decay', 'delta_gla_b_decay', 'delta_gla_out_g', 'delta_fox_q_g', 'delta_fox_k_g', 'delta_fox_b_f', 'delta_w_out', 'delta_norm_ffn_g', 'delta_w_gate', 'delta_w_up', 'delta_w_down', 'new_m_norm_mix_g', 'new_m_w_in', 'new_m_conv_w', 'new_m_pool_w', 'new_m_pool_scale', 'new_m_gla_w_decay', 'new_m_gla_b_decay', 'new_m_gla_out_g', 'new_m_fox_q_g', 'new_m_fox_k_g', 'new_m_fox_b_f', 'new_m_w_out', 'new_m_norm_ffn_g', 'new_m_w_gate', 'new_m_w_up', 'new_m_w_down', 'new_v_norm_mix_g', 'new_v_w_in', 'new_v_conv_w', 'new_v_pool_w', 'new_v_pool_scale', 'new_v_gla_w_decay', 'new_v_gla_b_decay', 'new_v_gla_out_g', 'new_v_fox_q_g', 'new_v_fox_k_g', 'new_v_fox_b_f', 'new_v_w_out', 'new_v_norm_ffn_g', 'new_v_w_gate', 'new_v_w_up', 'new_v_w_down']
TWIN_LEAF_KINDS = {'loss': 'loss', 'grad_x': 'grad_x', 'grad_norm_mix_g': 'grad_w', 'grad_w_in': 'grad_w', 'grad_conv_w': 'grad_w', 'grad_pool_w': 'grad_w', 'grad_pool_scale': 'grad_w', 'grad_gla_w_decay': 'grad_w', 'grad_gla_b_decay': 'grad_w', 'grad_gla_out_g': 'grad_w', 'grad_fox_q_g': 'grad_w', 'grad_fox_k_g': 'grad_w', 'grad_fox_b_f': 'grad_w', 'grad_w_out': 'grad_w', 'grad_norm_ffn_g': 'grad_w', 'grad_w_gate': 'grad_w', 'grad_w_up': 'grad_w', 'grad_w_down': 'grad_w', 'delta_norm_mix_g': 'delta_w', 'delta_w_in': 'delta_w', 'delta_conv_w': 'delta_w', 'delta_pool_w': 'delta_w', 'delta_pool_scale': 'delta_w', 'delta_gla_w_decay': 'delta_w', 'delta_gla_b_decay': 'delta_w', 'delta_gla_out_g': 'delta_w', 'delta_fox_q_g': 'delta_w', 'delta_fox_k_g': 'delta_w', 'delta_fox_b_f': 'delta_w', 'delta_w_out': 'delta_w', 'delta_norm_ffn_g': 'delta_w', 'delta_w_gate': 'delta_w', 'delta_w_up': 'delta_w', 'delta_w_down': 'delta_w', 'new_m_norm_mix_g': 'new_m', 'new_m_w_in': 'new_m', 'new_m_conv_w': 'new_m', 'new_m_pool_w': 'new_m', 'new_m_pool_scale': 'new_m', 'new_m_gla_w_decay': 'new_m', 'new_m_gla_b_decay': 'new_m', 'new_m_gla_out_g': 'new_m', 'new_m_fox_q_g': 'new_m', 'new_m_fox_k_g': 'new_m', 'new_m_fox_b_f': 'new_m', 'new_m_w_out': 'new_m', 'new_m_norm_ffn_g': 'new_m', 'new_m_w_gate': 'new_m', 'new_m_w_up': 'new_m', 'new_m_w_down': 'new_m', 'new_v_norm_mix_g': 'new_v', 'new_v_w_in': 'new_v', 'new_v_conv_w': 'new_v', 'new_v_pool_w': 'new_v', 'new_v_pool_scale': 'new_v', 'new_v_gla_w_decay': 'new_v', 'new_v_gla_b_decay': 'new_v', 'new_v_gla_out_g': 'new_v', 'new_v_fox_q_g': 'new_v', 'new_v_fox_k_g': 'new_v', 'new_v_fox_b_f': 'new_v', 'new_v_w_out': 'new_v', 'new_v_norm_ffn_g': 'new_v', 'new_v_w_gate': 'new_v', 'new_v_w_up': 'new_v', 'new_v_w_down': 'new_v'}


def _forward(args):
    return _fwd_reference(*[args[k] for k in FWD_PARAMS])


def _output_shape():
    out = _jax.eval_shape(lambda: _forward(_fwd_setup_inputs(0)))
    return out.shape, out.dtype

N_MICROBATCH = 1
ADAM_LR = 0.001
ADAM_B1 = 0.9
ADAM_B2 = 0.999
ADAM_EPS = 1e-08
ADAM_WD = 0.01
ADAM_STEP = 10
PER_EXAMPLE_BATCH_AXIS = {'x': 0, 'loss_target': 0}
SHARED_INPUTS = []
_WEIGHT_DTYPES = {'norm_mix_g': _jnp.float32, 'w_in': _jnp.float32, 'conv_w': _jnp.float32, 'pool_w': _jnp.float32, 'pool_scale': _jnp.float32, 'gla_w_decay': _jnp.float32, 'gla_b_decay': _jnp.float32, 'gla_out_g': _jnp.float32, 'fox_q_g': _jnp.float32, 'fox_k_g': _jnp.float32, 'fox_b_f': _jnp.float32, 'w_out': _jnp.float32, 'norm_ffn_g': _jnp.float32, 'w_gate': _jnp.float32, 'w_up': _jnp.float32, 'w_down': _jnp.float32}
MOMENT_SCALE = {'norm_mix_g': 8.345517e+00, 'w_in': 2.373656e-01, 'conv_w': 4.477427e+00, 'pool_w': 5.575538e-01, 'pool_scale': 6.323018e+00, 'gla_w_decay': 2.309143e-02, 'gla_b_decay': 9.618189e-02, 'gla_out_g': 1.138667e+01, 'fox_q_g': 1.403731e+00, 'fox_k_g': 1.405946e+00, 'fox_b_f': 2.734829e+01, 'w_out': 2.260188e-01, 'norm_ffn_g': 6.181885e+00, 'w_gate': 7.775409e-02, 'w_up': 7.979733e-02, 'w_down': 1.268221e-01}


def _to_microbatches(a, axis):
    t = _jnp.moveaxis(a, axis, 0)
    t = t.reshape((N_MICROBATCH, t.shape[0] // N_MICROBATCH) + t.shape[1:])
    return _jnp.moveaxis(t, 1, axis + 1)


def setup_inputs(seed: int = 0) -> dict:
    inp = _fwd_setup_inputs(seed)
    key = _jax.random.fold_in(_jax.random.key(seed), 7919)
    shape, _ = _output_shape()
    out = dict(inp)
    out["loss_target"] = _jax.random.normal(_jax.random.fold_in(key, 0), shape, _jnp.float32)
    for i, name in enumerate(TWIN_WEIGHTS):
        w = inp[name].astype(_jnp.float32)
        if MOMENT_SCALE is None:
            s = _jnp.sqrt(_jnp.mean(_jnp.square(w)) + 1e-30)
        else:
            s = MOMENT_SCALE[name]
        km, kv = _jax.random.split(_jax.random.fold_in(key, i + 1))
        out[name] = w
        out["m_" + name] = s * _jax.random.normal(km, w.shape, _jnp.float32)
        out["v_" + name] = (s * s) * _jax.random.uniform(kv, w.shape, _jnp.float32, 0.5, 1.5)
    if N_MICROBATCH > 1:
        for name, axis in PER_EXAMPLE_BATCH_AXIS.items():
            out[name] = _to_microbatches(out[name], axis)
    return {'x': out['x'], 'norm_mix_g': out['norm_mix_g'], 'w_in': out['w_in'], 'conv_w': out['conv_w'], 'pool_w': out['pool_w'], 'pool_scale': out['pool_scale'], 'gla_w_decay': out['gla_w_decay'], 'gla_b_decay': out['gla_b_decay'], 'gla_out_g': out['gla_out_g'], 'fox_q_g': out['fox_q_g'], 'fox_k_g': out['fox_k_g'], 'fox_b_f': out['fox_b_f'], 'w_out': out['w_out'], 'norm_ffn_g': out['norm_ffn_g'], 'w_gate': out['w_gate'], 'w_up': out['w_up'], 'w_down': out['w_down'], 'loss_target': out['loss_target'], 'm_norm_mix_g': out['m_norm_mix_g'], 'm_w_in': out['m_w_in'], 'm_conv_w': out['m_conv_w'], 'm_pool_w': out['m_pool_w'], 'm_pool_scale': out['m_pool_scale'], 'm_gla_w_decay': out['m_gla_w_decay'], 'm_gla_b_decay': out['m_gla_b_decay'], 'm_gla_out_g': out['m_gla_out_g'], 'm_fox_q_g': out['m_fox_q_g'], 'm_fox_k_g': out['m_fox_k_g'], 'm_fox_b_f': out['m_fox_b_f'], 'm_w_out': out['m_w_out'], 'm_norm_ffn_g': out['m_norm_ffn_g'], 'm_w_gate': out['m_w_gate'], 'm_w_up': out['m_w_up'], 'm_w_down': out['m_w_down'], 'v_norm_mix_g': out['v_norm_mix_g'], 'v_w_in': out['v_w_in'], 'v_conv_w': out['v_conv_w'], 'v_pool_w': out['v_pool_w'], 'v_pool_scale': out['v_pool_scale'], 'v_gla_w_decay': out['v_gla_w_decay'], 'v_gla_b_decay': out['v_gla_b_decay'], 'v_gla_out_g': out['v_gla_out_g'], 'v_fox_q_g': out['v_fox_q_g'], 'v_fox_k_g': out['v_fox_k_g'], 'v_fox_b_f': out['v_fox_b_f'], 'v_w_out': out['v_w_out'], 'v_norm_ffn_g': out['v_norm_ffn_g'], 'v_w_gate': out['v_w_gate'], 'v_w_up': out['v_w_up'], 'v_w_down': out['v_w_down']}


def _loss(weights, diff, rest, loss_target):
    with _jax.named_scope("forward"):
        args = {**rest, TWIN_DIFF_INPUT: diff, **{k: w.astype(_WEIGHT_DTYPES[k]) for k, w in weights.items()}}
        y = _forward(args)
    with _jax.named_scope("loss_head"):
        err = _jnp.square(y.astype(_jnp.float32) - loss_target)
        return 0.5 * _jnp.sum(_jnp.mean(err, axis=-1)) if err.ndim else 0.5 * err


def _adamw(w, g, m, v):
    m = ADAM_B1 * m + (1.0 - ADAM_B1) * g
    v = ADAM_B2 * v + (1.0 - ADAM_B2) * _jnp.square(g)
    m_hat = m / (1.0 - ADAM_B1 ** ADAM_STEP)
    v_hat = v / (1.0 - ADAM_B2 ** ADAM_STEP)
    delta = -ADAM_LR * (m_hat / (_jnp.sqrt(v_hat) + ADAM_EPS) + ADAM_WD * w)
    return delta, m, v


def reference(x, norm_mix_g, w_in, conv_w, pool_w, pool_scale, gla_w_decay, gla_b_decay, gla_out_g, fox_q_g, fox_k_g, fox_b_f, w_out, norm_ffn_g, w_gate, w_up, w_down, loss_target, m_norm_mix_g, m_w_in, m_conv_w, m_pool_w, m_pool_scale, m_gla_w_decay, m_gla_b_decay, m_gla_out_g, m_fox_q_g, m_fox_k_g, m_fox_b_f, m_w_out, m_norm_ffn_g, m_w_gate, m_w_up, m_w_down, v_norm_mix_g, v_w_in, v_conv_w, v_pool_w, v_pool_scale, v_gla_w_decay, v_gla_b_decay, v_gla_out_g, v_fox_q_g, v_fox_k_g, v_fox_b_f, v_w_out, v_norm_ffn_g, v_w_gate, v_w_up, v_w_down):
    given = dict(x=x, norm_mix_g=norm_mix_g, w_in=w_in, conv_w=conv_w, pool_w=pool_w, pool_scale=pool_scale, gla_w_decay=gla_w_decay, gla_b_decay=gla_b_decay, gla_out_g=gla_out_g, fox_q_g=fox_q_g, fox_k_g=fox_k_g, fox_b_f=fox_b_f, w_out=w_out, norm_ffn_g=norm_ffn_g, w_gate=w_gate, w_up=w_up, w_down=w_down, loss_target=loss_target, m_norm_mix_g=m_norm_mix_g, m_w_in=m_w_in, m_conv_w=m_conv_w, m_pool_w=m_pool_w, m_pool_scale=m_pool_scale, m_gla_w_decay=m_gla_w_decay, m_gla_b_decay=m_gla_b_decay, m_gla_out_g=m_gla_out_g, m_fox_q_g=m_fox_q_g, m_fox_k_g=m_fox_k_g, m_fox_b_f=m_fox_b_f, m_w_out=m_w_out, m_norm_ffn_g=m_norm_ffn_g, m_w_gate=m_w_gate, m_w_up=m_w_up, m_w_down=m_w_down, v_norm_mix_g=v_norm_mix_g, v_w_in=v_w_in, v_conv_w=v_conv_w, v_pool_w=v_pool_w, v_pool_scale=v_pool_scale, v_gla_w_decay=v_gla_w_decay, v_gla_b_decay=v_gla_b_decay, v_gla_out_g=v_gla_out_g, v_fox_q_g=v_fox_q_g, v_fox_k_g=v_fox_k_g, v_fox_b_f=v_fox_b_f, v_w_out=v_w_out, v_norm_ffn_g=v_norm_ffn_g, v_w_gate=v_w_gate, v_w_up=v_w_up, v_w_down=v_w_down)
    weights = {n: given[n] for n in TWIN_WEIGHTS}
    shared = {n: given[n] for n in SHARED_INPUTS}
    per_example = {n: given[n] for n in ['x']}
    grad_fn = _jax.value_and_grad(_loss, argnums=(0, 1))

    def one_microbatch(ex, loss_target):
        ex = dict(ex)
        diff = ex.pop(TWIN_DIFF_INPUT)
        return grad_fn(weights, diff, {**shared, **ex}, loss_target)

    if N_MICROBATCH == 1:
        loss, (grad_w, grad_x) = one_microbatch(per_example, given["loss_target"])
    else:
        def body(carry, xs):
            loss_sum, grad_sum = carry
            l_k, (gw_k, gx_k) = one_microbatch(xs[0], xs[1])
            with _jax.named_scope("update"):
                return (loss_sum + l_k, _jax.tree.map(_jnp.add, grad_sum, gw_k)), gx_k

        init = (_jnp.zeros((), _jnp.float32), _jax.tree.map(_jnp.zeros_like, weights))
        (loss, grad_w), grad_x = _jax.lax.scan(body, init, (per_example, given["loss_target"]))
    with _jax.named_scope("update"):
        delta_w, new_m, new_v = {}, {}, {}
        for n in TWIN_WEIGHTS:
            delta_w[n], new_m[n], new_v[n] = _adamw(weights[n], grad_w[n], given["m_" + n], given["v_" + n])
    return (loss, grad_x, *[grad_w[n] for n in TWIN_WEIGHTS], *[delta_w[n] for n in TWIN_WEIGHTS],
            *[new_m[n] for n in TWIN_WEIGHTS], *[new_v[n] for n in TWIN_WEIGHTS])
```

```python
import functools

import jax
import jax.numpy as jnp
from jax import lax
from jax.experimental import pallas as pl
from jax.experimental.pallas import tpu as pltpu

f32 = jnp.float32
bf16 = jnp.bfloat16

D = 2048
G = 512
DFF = 5632
NCHIP = 4
FB = DFF // NCHIP
WIN = 5140
WINB = WIN // NCHIP
EPS = 1e-6
CHUNK = 64
LANE = 128

CB, CC, CH, PU, GQ, GK, GV, GG, FQ, FK, FV, MISC = 0, 4, 8, 12, 16, 20, 24, 28, 32, 36, 40, 44
ZC = 45 * LANE
FF_LANE = 0
GA_LANE = 8

ADAM_LR, ADAM_B1, ADAM_B2, ADAM_EPS, ADAM_WD, ADAM_STEP = 0.001, 0.9, 0.999, 1e-08, 0.01, 10

VMEM_LIMIT = 60 * 1024 * 1024
MESH = pl.DeviceIdType.MESH


def _cp(sem=None):
    return pltpu.CompilerParams(dimension_semantics=sem, vmem_limit_bytes=VMEM_LIMIT)


def _dot(a, b, dims=((1,), (0,))):
    return lax.dot_general(a.astype(bf16), b.astype(bf16), (dims, ((), ())), preferred_element_type=f32)


def _bdot(a, b, ca, cb):
    return lax.dot_general(a.astype(bf16), b.astype(bf16), (((ca,), (cb,)), ((0,), (0,))),
                           preferred_element_type=f32)


def _log_sigmoid(x):
    return jnp.minimum(x, 0.0) - jnp.log(1.0 + jnp.exp(-jnp.abs(x)))


@jax.custom_vjp
def _sigmoid(x):
    return 1.0 / (1.0 + jnp.exp(-x))


def _sigmoid_fwd(x):
    s = _sigmoid(x)
    return s, s


def _sigmoid_bwd(s, g):
    return (g * s * (1.0 - s),)


_sigmoid.defvjp(_sigmoid_fwd, _sigmoid_bwd)


def _rms(x, g):
    return x * lax.rsqrt(jnp.mean(x * x, axis=-1, keepdims=True) + EPS) * g


def _shift_impl(x, n, period, transpose):
    rows = x.shape[0]
    t = lax.broadcasted_iota(jnp.int32, x.shape, 0)
    if period is not None:
        t = t & (period - 1)
    keep = t >= n
    if not transpose:
        return jnp.where(keep, pltpu.roll(x, n, 0), 0.0)
    return pltpu.roll(jnp.where(keep, x, 0.0), rows - n, 0)


def _shift(x, n, period=None):
    @jax.custom_vjp
    def f(v):
        return _shift_impl(v, n, period, False)

    def fwd(v):
        return f(v), None

    def bwd(_, g):
        return (_shift_impl(g, n, period, True),)

    f.defvjp(fwd, bwd)
    return f(x)


def _cumsum_rows(x, length, period=None):
    n = 1
    while n < length:
        x = x + _shift(x, n, period)
        n *= 2
    return x


def _convpool_fn(cb, cc, ch, pu, w0, w1, w2, pw, ps, j):
    u = cc * ch
    y = w2 * u + w1 * _shift(u, 1) + w0 * _shift(u, 2)
    ya = cb * y
    s2 = pu + _shift(pu, 1)
    s4 = s2 + _shift(s2, 2)
    s8 = s4 + _shift(s4, 4)
    s16 = s8 + _shift(s8, 8)
    wsum = jnp.where(j == 0, s2, jnp.where(j == 1, s4, jnp.where(j == 2, s8, s16)))
    width = (2 << j).astype(f32)
    t = lax.broadcasted_iota(jnp.int32, pu.shape, 0).astype(f32)
    count = jnp.minimum(t + 1.0, width)
    d = wsum / count - pu
    yb = _dot(d, pw) * ps
    return ya, yb


def _foxprep_fn(misc, bf):
    lf = _log_sigmoid(misc + bf)
    fc = _cumsum_rows(lf, lf.shape[0])
    return fc, jnp.transpose(fc)


def _fox_fn(q, k, v, fcol, frow8, qg, kg, h, i):
    tq, s = q.shape[0], k.shape[0]
    qn = _rms(q, qg)
    kn = _rms(k, kg)
    lg = _dot(qn, kn, ((1,), (1,))) * (LANE ** -0.5)
    lane = lax.broadcasted_iota(jnp.int32, fcol.shape, 1)
    fq = jnp.sum(jnp.where(lane == h, fcol, 0.0), axis=1, keepdims=True)
    row = lax.broadcasted_iota(jnp.int32, frow8.shape, 0)
    fk = jnp.sum(jnp.where(row == h, frow8, 0.0), axis=0, keepdims=True)
    lg = lg + fq - fk
    qpos = i * tq + lax.broadcasted_iota(jnp.int32, (tq, s), 0)
    kpos = lax.broadcasted_iota(jnp.int32, (tq, s), 1)
    lg = jnp.where(kpos <= qpos, lg, -jnp.inf)
    m = lax.stop_gradient(jnp.max(lg, axis=1, keepdims=True))
    e = jnp.exp(lg - m)
    p = e / jnp.sum(e, axis=1, keepdims=True)
    return _dot(p, v)


def _gla1_fn(q, k, v, misc, wd, bd):
    ts = q.shape[0]
    nb = ts // CHUNK
    x = _dot(misc, wd) + bd
    la = _log_sigmoid(x) * (1.0 / 16.0)
    cc = _cumsum_rows(la, CHUNK, CHUNK)
    la3 = la.reshape(nb, CHUNK, LANE)
    last3 = jnp.sum(la3, axis=1, keepdims=True)
    last2 = jnp.sum(la3, axis=1)
    cc3 = cc.reshape(nb, CHUNK, LANE)
    q3 = (q * 0.125).reshape(nb, CHUNK, LANE)
    k3 = k.reshape(nb, CHUNK, LANE)
    v3 = v.reshape(nb, CHUNK, LANE)
    ep = jnp.exp(cc3)
    en = jnp.exp(-cc3)
    qe = q3 * ep
    a1 = _bdot(qe, k3 * en, 2, 2)
    a2 = _bdot(q3 * en, k3 * ep, 2, 2)
    ti = lax.broadcasted_iota(jnp.int32, a1.shape, 1)
    si = lax.broadcasted_iota(jnp.int32, a1.shape, 2)
    sc = jnp.where(si <= ti, a1, a2)
    oi = _bdot(sc, v3, 2, 1)
    kd = k3 * jnp.exp(last3 - cc3)
    el = jnp.exp(last2)
    return qe.reshape(ts, LANE), kd.reshape(ts, LANE), el, oi.reshape(ts, LANE)


def _gla3_fn(o, gg, og):
    return _rms(o, og) * (gg * _sigmoid(gg))


def _ffn_fn(gate, up):
    return gate * _sigmoid(gate) * up


def _mm(name, a, b, a_spec, b_spec, out_shape, out_spec, grid, dims, nk, res=None, res_spec=None):
    has_res = res is not None
    nax = len(grid)

    def body(*refs):
        a_ref, b_ref = refs[0], refs[1]
        res_ref = refs[2] if has_res else None
        out_ref = refs[2 + has_res]
        part = _dot(a_ref[...], b_ref[...], dims)
        if nk == 1:
            if has_res:
                part = part + res_ref[...]
            out_ref[...] = part.astype(out_ref.dtype)
            return
        acc_ref = refs[3 + has_res]
        k = pl.program_id(nax - 1)

        @pl.when(k == 0)
        def _():
            acc_ref[...] = part

        @pl.when(k > 0)
        def _():
            acc_ref[...] += part

        @pl.when(k == nk - 1)
        def _():
            tot = acc_ref[...]
            if has_res:
                tot = tot + res_ref[...]
            out_ref[...] = tot.astype(out_ref.dtype)

    ops = [a, b] + ([res] if has_res else [])
    specs = [a_spec, b_spec] + ([res_spec] if has_res else [])
    blk = tuple(d for d in out_spec.block_shape if d is not None)
    scratch = [pltpu.VMEM(blk, f32)] if nk > 1 else []
    return pl.pallas_call(
        body, name=name, grid=grid, in_specs=specs, out_specs=out_spec, out_shape=out_shape,
        scratch_shapes=scratch,
        compiler_params=_cp(("parallel",) * (nax - 1) + ("arbitrary",)),
    )(*ops)


def _tm(s):
    return min(s, 512)


def _rmsnorm_fwd(name, x, g, l):
    s = x.shape[0]
    tm = min(s, 256)

    def body(x_ref, g_ref, u_ref):
        u_ref[...] = _rms(x_ref[...], g_ref[...]).astype(bf16)

    return pl.pallas_call(
        body, name=name, grid=(s // tm,),
        in_specs=[pl.BlockSpec((tm, D), lambda i: (i, 0)), pl.BlockSpec((None, 1, D), lambda i: (l, 0, 0))],
        out_specs=pl.BlockSpec((tm, D), lambda i: (i, 0)),
        out_shape=jax.ShapeDtypeStruct((s, D), bf16), compiler_params=_cp(("parallel",)),
    )(x, g)


def _rmsnorm_bwd(name, x, g, du, dres, l):
    s = x.shape[0]
    tm = min(s, 256)

    def body(x_ref, g_ref, du_ref, dres_ref, dx_ref, dxb_ref, dg_ref):
        _, vjp = jax.vjp(_rms, x_ref[...], g_ref[...])
        dx, dg = vjp(du_ref[...])
        tot = dx + dres_ref[...]
        dx_ref[...] = tot
        dxb_ref[...] = tot.astype(bf16)

        @pl.when(pl.program_id(0) == 0)
        def _():
            dg_ref[...] = dg

        @pl.when(pl.program_id(0) > 0)
        def _():
            dg_ref[...] += dg

    row = pl.BlockSpec((tm, D), lambda i: (i, 0))
    return pl.pallas_call(
        body, name=name, grid=(s // tm,),
        in_specs=[row, pl.BlockSpec((None, 1, D), lambda i: (l, 0, 0)), row, row],
        out_specs=[row, row, pl.BlockSpec((1, D), lambda i: (0, 0))],
        out_shape=[jax.ShapeDtypeStruct((s, D), f32), jax.ShapeDtypeStruct((s, D), bf16),
                   jax.ShapeDtypeStruct((1, D), f32)],
        compiler_params=_cp(("arbitrary",)),
    )(x, g, du, dres)


def _ffn_act(name, gate, up):
    s = gate.shape[0]
    tm = min(s, 256)
    spec = pl.BlockSpec((tm, FB), lambda i, j: (i, j))

    def body(g_ref, u_ref, a_ref):
        a_ref[...] = _ffn_fn(g_ref[...], u_ref[...]).astype(bf16)

    return pl.pallas_call(
        body, name=name, grid=(s // tm, NCHIP), in_specs=[spec, spec], out_specs=spec,
        out_shape=jax.ShapeDtypeStruct((s, DFF), bf16), compiler_params=_cp(("parallel", "parallel")),
    )(gate, up)


def _ffn_act_bwd(name, gate, up, dact):
    s = gate.shape[0]
    tm = min(s, 256)
    spec = pl.BlockSpec((tm, FB), lambda i, j: (i, j))

    def body(g_ref, u_ref, d_ref, dg_ref, du_ref):
        _, vjp = jax.vjp(_ffn_fn, g_ref[...], u_ref[...])
        dg, du = vjp(d_ref[...])
        dg_ref[...] = dg.astype(bf16)
        du_ref[...] = du.astype(bf16)

    return pl.pallas_call(
        body, name=name, grid=(s // tm, NCHIP), in_specs=[spec, spec, spec], out_specs=[spec, spec],
        out_shape=[jax.ShapeDtypeStruct((s, DFF), bf16)] * 2, compiler_params=_cp(("parallel", "parallel")),
    )(gate, up, dact)


def _loss(name, y, t):
    s = y.shape[0]
    tm = min(s, 256)
    row = pl.BlockSpec((tm, D), lambda i: (i, 0))

    def body(y_ref, t_ref, l_ref, d_ref, db_ref):
        e = y_ref[...] - t_ref[...]
        d = e * (1.0 / D)
        d_ref[...] = d
        db_ref[...] = d.astype(bf16)
        part = jnp.zeros((8, LANE), f32) + jnp.sum(e * e)

        @pl.when(pl.program_id(0) == 0)
        def _():
            l_ref[...] = part

        @pl.when(pl.program_id(0) > 0)
        def _():
            l_ref[...] += part

    return pl.pallas_call(
        body, name=name, grid=(s // tm,), in_specs=[row, row],
        out_specs=[pl.BlockSpec((8, LANE), lambda i: (0, 0)), row, row],
        out_shape=[jax.ShapeDtypeStruct((8, LANE), f32), jax.ShapeDtypeStruct((s, D), f32),
                   jax.ShapeDtypeStruct((s, D), bf16)],
        compiler_params=_cp(("arbitrary",)),
    )(y, t)


def _zspec(s, blk):
    return pl.BlockSpec((s, LANE), lambda j: (0, blk + j))


def _convpool_specs(s, l):
    return [_zspec(s, CB), _zspec(s, CC), _zspec(s, CH), _zspec(s, PU),
            pl.BlockSpec((None, 3, LANE), lambda j: (l, 0, j)),
            pl.BlockSpec((None, None, LANE, LANE), lambda j: (l, j, 0, 0)),
            pl.BlockSpec((None, 1, LANE), lambda j: (l, 0, j))]


def _convpool_fwd(name, z, conv_w, pool_w, pool_scale, l):
    s = z.shape[0]

    def body(cb, cc, ch, pu, cw, pw, ps, ya_ref, yb_ref):
        ya, yb = _convpool_fn(cb[...], cc[...], ch[...], pu[...], cw[0:1, :], cw[1:2, :], cw[2:3, :], pw[...], ps[...],
                              pl.program_id(0))
        ya_ref[...] = ya.astype(bf16)
        yb_ref[...] = yb.astype(bf16)

    col = pl.BlockSpec((s, LANE), lambda j: (0, j))
    return pl.pallas_call(
        body, name=name, grid=(4,), in_specs=_convpool_specs(s, l), out_specs=[col, col],
        out_shape=[jax.ShapeDtypeStruct((s, G), bf16)] * 2, compiler_params=_cp(("parallel",)),
    )(z, z, z, z, conv_w, pool_w, pool_scale)


def _convpool_bwd(name, z, conv_w, pool_w, pool_scale, dy, l):
    s = z.shape[0]

    def body(cb, cc, ch, pu, cw, pw, ps, dya, dyb, dcb, dcc, dch, dpu, dcw, dpw, dps):
        j = pl.program_id(0)
        fn = functools.partial(_convpool_fn, j=j)
        _, vjp = jax.vjp(fn, cb[...], cc[...], ch[...], pu[...], cw[0:1, :], cw[1:2, :], cw[2:3, :], pw[...], ps[...])
        g = vjp((dya[...], dyb[...]))
        dcb[...] = g[0].astype(bf16)
        dcc[...] = g[1].astype(bf16)
        dch[...] = g[2].astype(bf16)
        dpu[...] = g[3].astype(bf16)
        dcw[0:1, :] = g[4]
        dcw[1:2, :] = g[5]
        dcw[2:3, :] = g[6]
        dpw[...] = g[7]
        dps[...] = g[8]

    col = pl.BlockSpec((s, LANE), lambda j: (0, j))
    specs = _convpool_specs(s, l) + [pl.BlockSpec((s, LANE), lambda j: (0, j)),
                                     pl.BlockSpec((s, LANE), lambda j: (0, 4 + j))]
    return pl.pallas_call(
        body, name=name, grid=(4,), in_specs=specs,
        out_specs=[col, col, col, col, pl.BlockSpec((3, LANE), lambda j: (0, j)),
                   pl.BlockSpec((None, LANE, LANE), lambda j: (j, 0, 0)), pl.BlockSpec((1, LANE), lambda j: (0, j))],
        out_shape=[jax.ShapeDtypeStruct((s, G), bf16)] * 4 + [
            jax.ShapeDtypeStruct((3, G), f32), jax.ShapeDtypeStruct((4, LANE, LANE), f32),
            jax.ShapeDtypeStruct((1, G), f32)],
        compiler_params=_cp(("parallel",)),
    )(z, z, z, z, conv_w, pool_w, pool_scale, dy, dy)


def _foxprep_fwd(name, z, bf, l):
    s = z.shape[0]

    def body(m_ref, b_ref, fc_ref, fr_ref):
        fc, fr = _foxprep_fn(m_ref[...], b_ref[...])
        fc_ref[...] = fc
        fr_ref[...] = fr

    return pl.pallas_call(
        body, name=name, grid=(1,),
        in_specs=[pl.BlockSpec((s, LANE), lambda i: (0, MISC)), pl.BlockSpec((None, 1, LANE), lambda i: (l, 0, 0))],
        out_specs=[pl.BlockSpec((s, LANE), lambda i: (0, 0)), pl.BlockSpec((LANE, s), lambda i: (0, 0))],
        out_shape=[jax.ShapeDtypeStruct((s, LANE), f32), jax.ShapeDtypeStruct((LANE, s), f32)],
        compiler_params=_cp(("arbitrary",)),
    )(z, bf)


def _foxprep_bwd(name, z, bf, dfc4, dfr4, dmisc4, l):
    s = z.shape[0]

    def body(m_ref, b_ref, dfc_ref, dfr_ref, dm4_ref, dm_ref, db_ref):
        _, vjp = jax.vjp(_foxprep_fn, m_ref[...], b_ref[...])
        dfc = dfc_ref[0] + dfc_ref[1] + dfc_ref[2] + dfc_ref[3]
        dfr = dfr_ref[0] + dfr_ref[1] + dfr_ref[2] + dfr_ref[3]
        dfr = jnp.concatenate([dfr, jnp.zeros((LANE - 8, s), f32)], axis=0)
        dm, db = vjp((dfc, dfr))
        dm = dm + (dm4_ref[0] + dm4_ref[1] + dm4_ref[2] + dm4_ref[3])
        dm_ref[...] = dm.astype(bf16)
        db_ref[...] = db

    whole = lambda shape: pl.BlockSpec(shape, lambda i: (0,) * len(shape))
    return pl.pallas_call(
        body, name=name, grid=(1,),
        in_specs=[pl.BlockSpec((s, LANE), lambda i: (0, MISC)), pl.BlockSpec((None, 1, LANE), lambda i: (l, 0, 0)),
                  whole((4, s, LANE)), whole((4, 8, s)), whole((4, s, LANE))],
        out_specs=[whole((s, LANE)), whole((1, LANE))],
        out_shape=[jax.ShapeDtypeStruct((s, LANE), bf16), jax.ShapeDtypeStruct((1, LANE), f32)],
        compiler_params=_cp(("arbitrary",)),
    )(z, bf, dfc4, dfr4, dmisc4)


def _fox_specs(s, tq, l):
    return [pl.BlockSpec((tq, LANE), lambda h, i: (i, FQ + h)),
            pl.BlockSpec((s, LANE), lambda h, i: (0, FK + h)),
            pl.BlockSpec((s, LANE), lambda h, i: (0, FV + h)),
            pl.BlockSpec((tq, LANE), lambda h, i: (i, 0)),
            pl.BlockSpec((8, s), lambda h, i: (0, 0)),
            pl.BlockSpec((None, 1, LANE), lambda h, i: (l, 0, 0)),
            pl.BlockSpec((None, 1, LANE), lambda h, i: (l, 0, 0))]


def _fox_fwd(name, z, fc, fr, qg, kg, l):
    s = z.shape[0]
    tq = min(s, 256)

    def body(q, k, v, fc_ref, fr_ref, qg_ref, kg_ref, y_ref):
        y = _fox_fn(q[...], k[...], v[...], fc_ref[...], fr_ref[...], qg_ref[...], kg_ref[...],
                    pl.program_id(0), pl.program_id(1))
        y_ref[...] = y.astype(bf16)

    return pl.pallas_call(
        body, name=name, grid=(4, s // tq), in_specs=_fox_specs(s, tq, l),
        out_specs=pl.BlockSpec((tq, LANE), lambda h, i: (i, h)),
        out_shape=jax.ShapeDtypeStruct((s, G), bf16), compiler_params=_cp(("parallel", "parallel")),
    )(z, z, z, fc, fr, qg, kg)


def _fox_bwd(name, z, fc, fr, qg, kg, dy, l):
    s = z.shape[0]
    tq = min(s, 256)

    def body(q, k, v, fc_ref, fr_ref, qg_ref, kg_ref, dy_ref, dq, dk, dv, dfc, dfr, dqg, dkg):
        h, i = pl.program_id(0), pl.program_id(1)
        fn = functools.partial(_fox_fn, h=h, i=i)
        _, vjp = jax.vjp(fn, q[...], k[...], v[...], fc_ref[...], fr_ref[...], qg_ref[...], kg_ref[...])
        g = vjp(dy_ref[...])
        dq[...] = g[0].astype(bf16)
        dfc[...] = g[3]

        @pl.when(i == 0)
        def _():
            dk[...] = g[1]
            dv[...] = g[2]
            dfr[...] = g[4]
            dqg[...] = g[5]
            dkg[...] = g[6]

        @pl.when(i > 0)
        def _():
            dk[...] += g[1]
            dv[...] += g[2]
            dfr[...] += g[4]
            dqg[...] += g[5]
            dkg[...] += g[6]

    specs = _fox_specs(s, tq, l) + [pl.BlockSpec((tq, LANE), lambda h, i: (i, 12 + h))]
    head = pl.BlockSpec((s, LANE), lambda h, i: (0, h))
    gain = pl.BlockSpec((None, 1, LANE), lambda h, i: (h, 0, 0))
    return pl.pallas_call(
        body, name=name, grid=(4, s // tq), in_specs=specs,
        out_specs=[pl.BlockSpec((tq, LANE), lambda h, i: (i, h)), head, head,
                   pl.BlockSpec((None, tq, LANE), lambda h, i: (h, i, 0)),
                   pl.BlockSpec((None, 8, s), lambda h, i: (h, 0, 0)), gain, gain],
        out_shape=[jax.ShapeDtypeStruct((s, G), bf16), jax.ShapeDtypeStruct((s, G), f32),
                   jax.ShapeDtypeStruct((s, G), f32), jax.ShapeDtypeStruct((4, s, LANE), f32),
                   jax.ShapeDtypeStruct((4, 8, s), f32), jax.ShapeDtypeStruct((4, 1, LANE), f32),
                   jax.ShapeDtypeStruct((4, 1, LANE), f32)],
        compiler_params=_cp(("parallel", "arbitrary")),
    )(z, z, z, fc, fr, qg, kg, dy)


def _gla_ts(s):
    return min(s, 512)


def _gla1_specs(s, ts, l):
    return [pl.BlockSpec((ts, LANE), lambda h, i: (i, GQ + h)),
            pl.BlockSpec((ts, LANE), lambda h, i: (i, GK + h)),
            pl.BlockSpec((ts, LANE), lambda h, i: (i, GV + h)),
            pl.BlockSpec((ts, LANE), lambda h, i: (i, MISC)),
            pl.BlockSpec((None, LANE, LANE), lambda h, i: (l, 0, h)),
            pl.BlockSpec((None, 1, LANE), lambda h, i: (l, 0, h))]


def _gla1_fwd(name, z, wd, bd, l):
    s = z.shape[0]
    ts = _gla_ts(s)
    nb = ts // CHUNK

    def body(q, k, v, m, wd_ref, bd_ref, qe_ref, kd_ref, el_ref, oi_ref):
        qe, kd, el, oi = _gla1_fn(q[...], k[...], v[...], m[...], wd_ref[...], bd_ref[...])
        qe_ref[...] = qe.astype(bf16)
        kd_ref[...] = kd.astype(bf16)
        el_ref[...] = el
        oi_ref[...] = oi

    blk = pl.BlockSpec((ts, LANE), lambda h, i: (i, h))
    return pl.pallas_call(
        body, name=name, grid=(4, s // ts), in_specs=_gla1_specs(s, ts, l),
        out_specs=[blk, blk, pl.BlockSpec((nb, LANE), lambda h, i: (i, h)), blk],
        out_shape=[jax.ShapeDtypeStruct((s, G), bf16), jax.ShapeDtypeStruct((s, G), bf16),
                   jax.ShapeDtypeStruct((s // CHUNK, G), f32), jax.ShapeDtypeStruct((s, G), f32)],
        compiler_params=_cp(("parallel", "parallel")),
    )(z, z, z, z, wd, bd)


def _gla1_bwd(name, z, wd, bd, dqe, dkd, del_, do, dvi, l):
    s = z.shape[0]
    ts = _gla_ts(s)
    nb = ts // CHUNK

    def body(q, k, v, m, wd_ref, bd_ref, dqe_ref, dkd_ref, del_ref, do_ref, dvi_ref, dq, dk, dv, dm, dwd, dbd):
        i = pl.program_id(1)
        _, vjp = jax.vjp(_gla1_fn, q[...], k[...], v[...], m[...], wd_ref[...], bd_ref[...])
        g = vjp((dqe_ref[...], dkd_ref[...], del_ref[...], do_ref[...]))
        dq[...] = g[0].astype(bf16)
        dk[...] = g[1].astype(bf16)
        dv[...] = (g[2] + dvi_ref[...]).astype(bf16)
        dm[...] = g[3]

        @pl.when(i == 0)
        def _():
            dwd[...] = g[4]
            dbd[...] = g[5]

        @pl.when(i > 0)
        def _():
            dwd[...] += g[4]
            dbd[...] += g[5]

    blk = pl.BlockSpec((ts, LANE), lambda h, i: (i, h))
    specs = _gla1_specs(s, ts, l) + [blk, blk, pl.BlockSpec((nb, LANE), lambda h, i: (i, h)), blk, blk]
    return pl.pallas_call(
        body, name=name, grid=(4, s // ts), in_specs=specs,
        out_specs=[blk, blk, blk, pl.BlockSpec((None, ts, LANE), lambda h, i: (h, i, 0)),
                   pl.BlockSpec((None, LANE, LANE), lambda h, i: (h, 0, 0)),
                   pl.BlockSpec((None, 1, LANE), lambda h, i: (h, 0, 0))],
        out_shape=[jax.ShapeDtypeStruct((s, G), bf16)] * 3 + [
            jax.ShapeDtypeStruct((4, s, LANE), f32), jax.ShapeDtypeStruct((4, LANE, LANE), f32),
            jax.ShapeDtypeStruct((4, 1, LANE), f32)],
        compiler_params=_cp(("parallel", "arbitrary")),
    )(z, z, z, z, wd, bd, dqe, dkd, del_, do, dvi)


def _gla2_fwd(name, z, qe, kd, el, oi):
    s = z.shape[0]
    n = s // CHUNK

    def body(v_ref, qe_ref, kd_ref, el_ref, oi_ref, o_ref, st_ref, cur):
        cur[...] = jnp.zeros_like(cur)

        def step(c, carry):
            rows = pl.ds(pl.multiple_of(c * CHUNK, CHUNK), CHUNK)
            st = cur[...]
            st_ref[c] = st
            o_ref[rows, :] = oi_ref[rows, :] + _dot(qe_ref[rows, :], st, ((1,), (1,)))
            cur[...] = st * el_ref[pl.ds(c, 1), :] + _dot(v_ref[rows, :], kd_ref[rows, :], ((0,), (0,)))
            return carry

        lax.fori_loop(0, n, step, 0)

    head = pl.BlockSpec((s, LANE), lambda h: (0, h))
    return pl.pallas_call(
        body, name=name, grid=(4,),
        in_specs=[pl.BlockSpec((s, LANE), lambda h: (0, GV + h)), head, head,
                  pl.BlockSpec((n, LANE), lambda h: (0, h)), head],
        out_specs=[head, pl.BlockSpec((None, n, LANE, LANE), lambda h: (h, 0, 0, 0))],
        out_shape=[jax.ShapeDtypeStruct((s, G), f32), jax.ShapeDtypeStruct((4, n, LANE, LANE), f32)],
        scratch_shapes=[pltpu.VMEM((LANE, LANE), f32)],
        compiler_params=_cp(("parallel",)),
    )(z, qe, kd, el, oi)


def _gla2_bwd(name, z, qe, kd, el, st, do):
    s = z.shape[0]
    n = s // CHUNK

    def body(v_ref, qe_ref, kd_ref, el_ref, st_ref, do_ref, dqe_ref, dkd_ref, dv_ref, del_ref, dcur):
        dcur[...] = jnp.zeros_like(dcur)

        def step(t, carry):
            c = n - 1 - t
            rows = pl.ds(pl.multiple_of(c * CHUNK, CHUNK), CHUNK)
            dn = dcur[...]
            stc = st_ref[c]
            doc = do_ref[rows, :]
            dqe_ref[rows, :] = _dot(doc, stc)
            dv_ref[rows, :] = _dot(kd_ref[rows, :], dn, ((1,), (1,)))
            dkd_ref[rows, :] = _dot(v_ref[rows, :], dn)
            del_ref[pl.ds(c, 1), :] = jnp.sum(stc * dn, axis=0, keepdims=True)
            dcur[...] = dn * el_ref[pl.ds(c, 1), :] + _dot(doc, qe_ref[rows, :], ((0,), (0,)))
            return carry

        lax.fori_loop(0, n, step, 0)

    head = pl.BlockSpec((s, LANE), lambda h: (0, h))
    chunk = pl.BlockSpec((n, LANE), lambda h: (0, h))
    return pl.pallas_call(
        body, name=name, grid=(4,),
        in_specs=[pl.BlockSpec((s, LANE), lambda h: (0, GV + h)), head, head, chunk,
                  pl.BlockSpec((None, n, LANE, LANE), lambda h: (h, 0, 0, 0)), head],
        out_specs=[head, head, head, chunk],
        out_shape=[jax.ShapeDtypeStruct((s, G), f32)] * 3 + [jax.ShapeDtypeStruct((n, G), f32)],
        scratch_shapes=[pltpu.VMEM((LANE, LANE), f32)],
        compiler_params=_cp(("parallel",)),
    )(z, qe, kd, el, st, do)


def _gla3_specs(ts, l):
    return [pl.BlockSpec((ts, LANE), lambda h, i: (i, h)),
            pl.BlockSpec((ts, LANE), lambda h, i: (i, GG + h)),
            pl.BlockSpec((None, 1, LANE), lambda h, i: (l, 0, 0))]


def _gla3_fwd(name, o, z, og, l):
    s = z.shape[0]
    ts = _gla_ts(s)

    def body(o_ref, g_ref, og_ref, y_ref):
        y_ref[...] = _gla3_fn(o_ref[...], g_ref[...], og_ref[...]).astype(bf16)

    return pl.pallas_call(
        body, name=name, grid=(4, s // ts), in_specs=_gla3_specs(ts, l),
        out_specs=pl.BlockSpec((ts, LANE), lambda h, i: (i, h)),
        out_shape=jax.ShapeDtypeStruct((s, G), bf16), compiler_params=_cp(("parallel", "parallel")),
    )(o, z, og)


def _gla3_bwd(name, o, z, og, dy, l):
    s = z.shape[0]
    ts = _gla_ts(s)

    def body(o_ref, g_ref, og_ref, dy_ref, do_ref, dg_ref, dog_ref):
        i = pl.program_id(1)
        _, vjp = jax.vjp(_gla3_fn, o_ref[...], g_ref[...], og_ref[...])
        g = vjp(dy_ref[...])
        do_ref[...] = g[0]
        dg_ref[...] = g[1].astype(bf16)

        @pl.when(i == 0)
        def _():
            dog_ref[...] = g[2]

        @pl.when(i > 0)
        def _():
            dog_ref[...] += g[2]

    blk = pl.BlockSpec((ts, LANE), lambda h, i: (i, h))
    return pl.pallas_call(
        body, name=name, grid=(4, s // ts),
        in_specs=_gla3_specs(ts, l) + [pl.BlockSpec((ts, LANE), lambda h, i: (i, 8 + h))],
        out_specs=[blk, blk, pl.BlockSpec((None, 1, LANE), lambda h, i: (h, 0, 0))],
        out_shape=[jax.ShapeDtypeStruct((s, G), f32), jax.ShapeDtypeStruct((s, G), bf16),
                   jax.ShapeDtypeStruct((4, 1, LANE), f32)],
        compiler_params=_cp(("parallel", "arbitrary")),
    )(o, z, og, dy)


def _layer_fwd(l, h, p):
    s = h.shape[0]
    tm = _tm(s)
    n = f"l{l}_"
    u = _rmsnorm_fwd(n + "norm_mix", h, p["g_mix"], l)
    z = _mm(n + "mm_in", u, p["w_in"],
            pl.BlockSpec((tm, D), lambda j, i, k: (i, 0)), pl.BlockSpec((None, D, 1152), lambda j, i, k: (l, 0, j)),
            jax.ShapeDtypeStruct((s, ZC), f32), pl.BlockSpec((tm, 1152), lambda j, i, k: (i, j)),
            (ZC // 1152, s // tm, 1), ((1,), (0,)), 1)
    ya, yb = _convpool_fwd(n + "convpool", z, p["conv_w"], p["pool_w"], p["pool_scale"], l)
    qe, kd, el, oi = _gla1_fwd(n + "gla_chunk", z, p["wdec"], p["bdec"], l)
    o, st = _gla2_fwd(n + "gla_scan", z, qe, kd, el, oi)
    yc = _gla3_fwd(n + "gla_out", o, z, p["gla_og"], l)
    fc, fr = _foxprep_fwd(n + "fox_prep", z, p["fox_bf"], l)
    yd = _fox_fwd(n + "fox_attn", z, fc, fr, p["fox_qg"], p["fox_kg"], l)
    y = jnp.concatenate([ya, yb, yc, yd], axis=1)
    h1 = _mm(n + "mm_out", y, p["w_out"],
             pl.BlockSpec((tm, G), lambda j, i, k: (i, k)),
             pl.BlockSpec((None, None, G, 1024), lambda j, i, k: (k, l, 0, j)),
             jax.ShapeDtypeStruct((s, D), f32), pl.BlockSpec((tm, 1024), lambda j, i, k: (i, j)),
             (2, s // tm, NCHIP), ((1,), (0,)), NCHIP, res=h, res_spec=pl.BlockSpec((tm, 1024), lambda j, i, k: (i, j)))
    u2 = _rmsnorm_fwd(n + "norm_ffn", h1, p["g_ffn"], l)
    ffn_in = (pl.BlockSpec((tm, D), lambda j, i, k: (i, 0)),
              pl.BlockSpec((None, None, D, FB), lambda j, i, k: (j, l, 0, 0)))
    ffn_out = (jax.ShapeDtypeStruct((s, DFF), f32), pl.BlockSpec((tm, FB), lambda j, i, k: (i, j)))
    gate = _mm(n + "mm_gate", u2, p["w_gate"], *ffn_in, *ffn_out, (NCHIP, s // tm, 1), ((1,), (0,)), 1)
    up = _mm(n + "mm_up", u2, p["w_up"], *ffn_in, *ffn_out, (NCHIP, s // tm, 1), ((1,), (0,)), 1)
    act = _ffn_act(n + "ffn_act", gate, up)
    h2 = _mm(n + "mm_down", act, p["w_down"],
             pl.BlockSpec((tm, FB), lambda j, i, k: (i, k)),
             pl.BlockSpec((None, None, FB, 1024), lambda j, i, k: (k, l, 0, j)),
             jax.ShapeDtypeStruct((s, D), f32), pl.BlockSpec((tm, 1024), lambda j, i, k: (i, j)),
             (2, s // tm, NCHIP), ((1,), (0,)), NCHIP, res=h1, res_spec=pl.BlockSpec((tm, 1024), lambda j, i, k: (i, j)))
    saved = dict(h=h, u=u, z=z, qe=qe, kd=kd, el=el, st=st, o=o, fc=fc, fr=fr, y=y, h1=h1, u2=u2,
                 gate=gate, up=up, act=act)
    return h2, saved


def _mm_tn(name, a, b, ta, tb, out_shape, out_spec, grid):
    s = a.shape[0]
    return _mm(name, a, b, pl.BlockSpec((s, ta), lambda i, j, k: (0, i)), pl.BlockSpec((s, tb), lambda i, j, k: (0, j)),
               out_shape, out_spec, grid, ((0,), (0,)), 1)


def _layer_bwd(l, dh2, dh2b, p, sv):
    s = dh2.shape[0]
    tm = _tm(s)
    n = f"l{l}_bwd_"
    dact = _mm(n + "mm_dact", dh2b, p["w_down"],
               pl.BlockSpec((tm, D), lambda j, i, k: (i, 0)),
               pl.BlockSpec((None, None, FB, D), lambda j, i, k: (j, l, 0, 0)),
               jax.ShapeDtypeStruct((s, DFF), f32), pl.BlockSpec((tm, FB), lambda j, i, k: (i, j)),
               (NCHIP, s // tm, 1), ((1,), (1,)), 1)
    g_wd = _mm_tn(n + "mm_dwd", sv["act"], dh2b, FB, 1024, jax.ShapeDtypeStruct((NCHIP, FB, D), bf16),
                  pl.BlockSpec((None, FB, 1024), lambda i, j, k: (i, 0, j)), (NCHIP, 2, 1))
    dgate, dup = _ffn_act_bwd(n + "ffn_act", sv["gate"], sv["up"], dact)
    nt_in = lambda: (pl.BlockSpec((tm, FB), lambda j, i, k: (i, k)),
                     pl.BlockSpec((None, None, 1024, FB), lambda j, i, k: (k, l, j, 0)))
    nt_out = lambda: (jax.ShapeDtypeStruct((s, D), f32), pl.BlockSpec((tm, 1024), lambda j, i, k: (i, j)))
    du2 = _mm(n + "mm_du2_gate", dgate, p["w_gate"], *nt_in(), *nt_out(), (2, s // tm, NCHIP), ((1,), (1,)), NCHIP)
    du2 = _mm(n + "mm_du2_up", dup, p["w_up"], *nt_in(), *nt_out(), (2, s // tm, NCHIP), ((1,), (1,)), NCHIP,
              res=du2, res_spec=pl.BlockSpec((tm, 1024), lambda j, i, k: (i, j)))
    wg_shape = jax.ShapeDtypeStruct((NCHIP, D, FB), bf16)
    wg_spec = lambda: pl.BlockSpec((None, 1024, FB), lambda i, j, k: (j, i, 0))
    g_wg = _mm_tn(n + "mm_dwg", sv["u2"], dgate, 1024, FB, wg_shape, wg_spec(), (2, NCHIP, 1))
    g_wu = _mm_tn(n + "mm_dwu", sv["u2"], dup, 1024, FB, wg_shape, wg_spec(), (2, NCHIP, 1))
    dh1, dh1b, dg_ffn = _rmsnorm_bwd(n + "norm_ffn", sv["h1"], p["g_ffn"], du2, dh2, l)
    dy = _mm(n + "mm_dy", dh1b, p["w_out"],
             pl.BlockSpec((tm, D), lambda j, i, k: (i, 0)),
             pl.BlockSpec((None, None, G, D), lambda j, i, k: (j, l, 0, 0)),
             jax.ShapeDtypeStruct((s, D), f32), pl.BlockSpec((tm, G), lambda j, i, k: (i, j)),
             (NCHIP, s // tm, 1), ((1,), (1,)), 1)
    g_wo = _mm_tn(n + "mm_dwo", sv["y"], dh1b, G, 1024, jax.ShapeDtypeStruct((NCHIP, G, D), bf16),
                  pl.BlockSpec((None, G, 1024), lambda i, j, k: (i, 0, j)), (NCHIP, 2, 1))
    z = sv["z"]
    dcb, dcc, dch, dpu, dconv, dpoolw, dpools = _convpool_bwd(
        n + "convpool", z, p["conv_w"], p["pool_w"], p["pool_scale"], dy, l)
    do, dgg, dog = _gla3_bwd(n + "gla_out", sv["o"], z, p["gla_og"], dy, l)
    dqe, dkd, dvi, del_ = _gla2_bwd(n + "gla_scan", z, sv["qe"], sv["kd"], sv["el"], sv["st"], do)
    dgq, dgk, dgv, dmisc4, dwd, dbd = _gla1_bwd(n + "gla_chunk", z, p["wdec"], p["bdec"], dqe, dkd, del_, do, dvi, l)
    dfq, dfk, dfv, dfc4, dfr4, dqg, dkg = _fox_bwd(n + "fox_attn", z, sv["fc"], sv["fr"], p["fox_qg"], p["fox_kg"], dy, l)
    dmisc, dbf = _foxprep_bwd(n + "fox_prep", z, p["fox_bf"], dfc4, dfr4, dmisc4, l)
    dz = jnp.concatenate([dcb, dcc, dch, dpu, dgq, dgk, dgv, dgg, dfq, dfk.astype(bf16), dfv.astype(bf16), dmisc],
                         axis=1)
    du = _mm(n + "mm_du", dz, p["w_in"],
             pl.BlockSpec((tm, 1152), lambda j, i, k: (i, k)), pl.BlockSpec((None, 1024, 1152), lambda j, i, k: (l, j, k)),
             jax.ShapeDtypeStruct((s, D), f32), pl.BlockSpec((tm, 1024), lambda j, i, k: (i, j)),
             (2, s // tm, ZC // 1152), ((1,), (1,)), ZC // 1152)
    g_wi = _mm_tn(n + "mm_dwi", sv["u"], dz, 1024, 1152, jax.ShapeDtypeStruct((D, ZC), bf16),
                  pl.BlockSpec((1024, 1152), lambda i, j, k: (i, j)), (2, ZC // 1152, 1))
    dh, dhb, dg_mix = _rmsnorm_bwd(n + "norm_mix", sv["h"], p["g_mix"], du, dh1, l)
    big = dict(w_in=_win_to_blocks(g_wi), w_out=g_wo, w_gate=g_wg, w_up=g_wu, w_down=g_wd)
    small = dict(
        norm_mix_g=dg_mix[0], norm_ffn_g=dg_ffn[0], conv_w=dconv, pool_w=dpoolw, pool_scale=dpools[0],
        gla_w_decay=jnp.concatenate([dwd[hh, GA_LANE:GA_LANE + 16, :64] for hh in range(4)], axis=1),
        gla_b_decay=jnp.concatenate([dbd[hh, 0, :64] for hh in range(4)]),
        gla_out_g=jnp.sum(dog[:, 0, :], axis=0), fox_q_g=jnp.sum(dqg[:, 0, :], axis=0),
        fox_k_g=jnp.sum(dkg[:, 0, :], axis=0), fox_b_f=dbf[0, FF_LANE:FF_LANE + 4])
    return dh, dhb, big, small


def _win_from_blocks(wb):
    def cols(a, b):
        parts = []
        for kk in range(NCHIP):
            lo, hi = max(a, kk * WINB), min(b, (kk + 1) * WINB)
            if lo < hi:
                parts.append(wb[kk, :, :, lo - kk * WINB:hi - kk * WINB])
        return parts

    zeros = lambda w: [jnp.zeros(wb.shape[1:3] + (w,), wb.dtype)]
    segs = cols(0, 2048)
    for hh in range(4):
        segs += cols(2048 + 64 * hh, 2112 + 64 * hh) + zeros(64)
    for hh in range(4):
        segs += cols(2304 + 64 * hh, 2368 + 64 * hh) + zeros(64)
    segs += cols(2560, 3584) + cols(3600, 5136)
    segs += cols(5136, 5140) + zeros(GA_LANE - 4) + cols(3584, 3600) + zeros(LANE - GA_LANE - 16)
    return jnp.concatenate(segs, axis=-1)


def _win_to_blocks(g):
    mb = MISC * LANE
    parts = [g[:, 0:2048]]
    parts += [g[:, GQ * LANE + LANE * hh:GQ * LANE + LANE * hh + 64] for hh in range(4)]
    parts += [g[:, GK * LANE + LANE * hh:GK * LANE + LANE * hh + 64] for hh in range(4)]
    parts += [g[:, GV * LANE:FQ * LANE], g[:, mb + GA_LANE:mb + GA_LANE + 16], g[:, FQ * LANE:mb],
              g[:, mb + FF_LANE:mb + FF_LANE + 4]]
    full = jnp.concatenate(parts, axis=1)
    return full.reshape(D, NCHIP, WINB).transpose(1, 0, 2)


def _place():
    x, y, c = lax.axis_index("x"), lax.axis_index("y"), lax.axis_index("c")
    chips = [(1 - x, y), (x, 1 - y), (1 - x, 1 - y)]
    return x, y, c, chips


def _allgather_small(name, v):
    m_per, n = v.shape

    def body(x_ref, out_ref, send_sems, recv_sems, local_sem):
        x, y, c, chips = _place()
        me, sibling = (x, y, c), (x, y, 1 - c)

        def rows(px, py, pc):
            return out_ref.at[pl.ds((4 * px + 2 * py + pc) * m_per, m_per), :]

        def copy(k, block, to, src=None):
            return pltpu.make_async_remote_copy(
                src_ref=rows(*block) if src is None else src, dst_ref=rows(*block),
                send_sem=send_sems.at[k], recv_sem=recv_sems.at[k], device_id=to, device_id_type=MESH)

        mine = pltpu.make_async_copy(x_ref, rows(*me), local_sem)
        mine.start()
        first = [copy(0, me, sibling, src=x_ref)]
        first += [copy(1 + j, me, (*chip, c), src=x_ref) for j, chip in enumerate(chips)]
        for cp in first:
            cp.start()
        passed = [copy(4 + j, (*chip, c), sibling) for j, chip in enumerate(chips)]
        for j, chip in enumerate(chips):
            copy(1 + j, (*chip, c), me).wait_recv()
            passed[j].start()
        copy(0, sibling, me).wait_recv()
        for j, chip in enumerate(chips):
            copy(4 + j, (*chip, 1 - c), me).wait_recv()
        for cp in first + passed:
            cp.wait_send()
        mine.wait()

    return pl.pallas_call(
        body, name=name, out_shape=jax.ShapeDtypeStruct((8 * m_per, n), v.dtype),
        in_specs=[pl.BlockSpec(memory_space=pltpu.VMEM)], out_specs=pl.BlockSpec(memory_space=pltpu.VMEM),
        scratch_shapes=[pltpu.SemaphoreType.DMA((7,)), pltpu.SemaphoreType.DMA((7,)), pltpu.SemaphoreType.DMA],
    )(v)


def _hbm_specs(n):
    return [pl.BlockSpec(memory_space=pl.ANY)] * n


def _allgather_weights(name, shards):
    nt = len(shards)

    def body(*refs):
        ins, outs = refs[:nt], refs[nt:2 * nt]
        send_sems, recv_sems, local_sems = refs[2 * nt:]
        x, y, c, chips = _place()
        me = 2 * x + y
        sibling = (x, y, 1 - c)

        def half(t, chip_idx, cc):
            r = ins[t].shape[1] // 2
            return outs[t].at[chip_idx, :, pl.ds(cc * r, r), :]

        def src_half(t):
            r = ins[t].shape[1] // 2
            return ins[t].at[:, pl.ds(c * r, r), :]

        def copy(t, k, chip_idx, cc, to, src=None):
            return pltpu.make_async_remote_copy(
                src_ref=half(t, chip_idx, cc) if src is None else src, dst_ref=half(t, chip_idx, cc),
                send_sem=send_sems.at[t, k], recv_sem=recv_sems.at[t, k], device_id=to, device_id_type=MESH)

        local = [pltpu.make_async_copy(ins[t], outs[t].at[me], local_sems.at[t]) for t in range(nt)]
        for cp in local:
            cp.start()
        sent = []
        for t in range(nt):
            for j, chip in enumerate(chips):
                cp = copy(t, j, me, c, (*chip, c), src=src_half(t))
                cp.start()
                sent.append(cp)
        for t in range(nt):
            for j, (px, py) in enumerate(chips):
                copy(t, j, 2 * px + py, c, (x, y, c)).wait_recv()
                cp = copy(t, 3 + j, 2 * px + py, c, sibling)
                cp.start()
                sent.append(cp)
        for t in range(nt):
            for j, (px, py) in enumerate(chips):
                copy(t, 3 + j, 2 * px + py, 1 - c, (x, y, c)).wait_recv()
        for cp in sent:
            cp.wait_send()
        for cp in local:
            cp.wait()

    return pl.pallas_call(
        body, name=name, out_shape=[jax.ShapeDtypeStruct((NCHIP,) + v.shape, v.dtype) for v in shards],
        in_specs=_hbm_specs(nt), out_specs=_hbm_specs(nt),
        scratch_shapes=[pltpu.SemaphoreType.DMA((nt, 6)), pltpu.SemaphoreType.DMA((nt, 6)),
                        pltpu.SemaphoreType.DMA((nt,))],
    )(*shards)


def _rs_to_sibling(name, grads):
    nt = len(grads)

    def body(*refs):
        ins, outs = refs[:nt], refs[nt:2 * nt]
        send_sems, recv_sems = refs[2 * nt:]
        x, y, c, _ = _place()
        cps = []
        for t in range(nt):
            r = ins[t].shape[1] // 2
            cp = pltpu.make_async_remote_copy(
                src_ref=ins[t].at[:, pl.ds((1 - c) * r, r), :], dst_ref=outs[t],
                send_sem=send_sems.at[t], recv_sem=recv_sems.at[t], device_id=(x, y, 1 - c), device_id_type=MESH)
            cp.start()
            cps.append(cp)
        for cp in cps:
            cp.wait()

    return pl.pallas_call(
        body, name=name,
        out_shape=[jax.ShapeDtypeStruct((NCHIP, g.shape[1] // 2, g.shape[2]), g.dtype) for g in grads],
        in_specs=_hbm_specs(nt), out_specs=_hbm_specs(nt),
        scratch_shapes=[pltpu.SemaphoreType.DMA((nt,)), pltpu.SemaphoreType.DMA((nt,))],
    )(*grads)


def _rs_pair_sum(name, mine, other):
    r, cdim = other.shape[1], other.shape[2]
    tr = r // 4 if (r // 4) % 16 == 0 else r // 2

    def body(g_ref, o_ref, s_ref):
        s_ref[...] = (g_ref[...].astype(f32) + o_ref[...].astype(f32)).astype(bf16)

    blk = pl.BlockSpec((None, tr, cdim), lambda q, i: (q, i, 0))
    return pl.pallas_call(
        body, name=name, grid=(NCHIP, r // tr), in_specs=[blk, blk], out_specs=blk,
        out_shape=jax.ShapeDtypeStruct(other.shape, bf16), compiler_params=_cp(("parallel", "parallel")),
    )(mine, other)


def _rs_to_owner(name, sums):
    nt = len(sums)

    def body(*refs):
        ins, outs = refs[:nt], refs[nt:2 * nt]
        send_sems, recv_sems, local_sems = refs[2 * nt:]
        x, y, c, chips = _place()
        me = 2 * x + y
        local = [pltpu.make_async_copy(ins[t].at[me], outs[t].at[me], local_sems.at[t]) for t in range(nt)]
        for cp in local:
            cp.start()
        cps = []
        for t in range(nt):
            for j, (px, py) in enumerate(chips):
                cp = pltpu.make_async_remote_copy(
                    src_ref=ins[t].at[2 * px + py], dst_ref=outs[t].at[me],
                    send_sem=send_sems.at[t, j], recv_sem=recv_sems.at[t, j], device_id=(px, py, c),
                    device_id_type=MESH)
                cp.start()
                cps.append(cp)
        for t in range(nt):
            for j, (px, py) in enumerate(chips):
                pltpu.make_async_remote_copy(
                    src_ref=ins[t].at[me], dst_ref=outs[t].at[2 * px + py],
                    send_sem=send_sems.at[t, j], recv_sem=recv_sems.at[t, j], device_id=(px, py, c),
                    device_id_type=MESH).wait_recv()
        for cp in cps:
            cp.wait_send()
        for cp in local:
            cp.wait()

    return pl.pallas_call(
        body, name=name, out_shape=[jax.ShapeDtypeStruct(v.shape, v.dtype) for v in sums],
        in_specs=_hbm_specs(nt), out_specs=_hbm_specs(nt),
        scratch_shapes=[pltpu.SemaphoreType.DMA((nt, 3)), pltpu.SemaphoreType.DMA((nt, 3)),
                        pltpu.SemaphoreType.DMA((nt,))],
    )(*sums)


def _rs_chip_sum(name, parts):
    r, cdim = parts.shape[1], parts.shape[2]
    tr = r // 4 if (r // 4) % 16 == 0 else r // 2

    def body(p_ref, o_ref):
        o_ref[...] = ((p_ref[0].astype(f32) + p_ref[1].astype(f32)) + p_ref[2].astype(f32)) + p_ref[3].astype(f32)

    return pl.pallas_call(
        body, name=name, grid=(r // tr,), in_specs=[pl.BlockSpec((NCHIP, tr, cdim), lambda i: (0, i, 0))],
        out_specs=pl.BlockSpec((tr, cdim), lambda i: (i, 0)),
        out_shape=jax.ShapeDtypeStruct((r, cdim), f32), compiler_params=_cp(("parallel",)),
    )(parts)


def _rs_share_halves(name, halves):
    nt = len(halves)

    def body(*refs):
        ins, outs = refs[:nt], refs[nt:2 * nt]
        send_sems, recv_sems, local_sems = refs[2 * nt:]
        x, y, c, _ = _place()
        cps = []
        for t in range(nt):
            r = ins[t].shape[0]
            mine = outs[t].at[pl.ds(c * r, r), :]
            loc = pltpu.make_async_copy(ins[t], mine, local_sems.at[t])
            loc.start()
            cp = pltpu.make_async_remote_copy(
                src_ref=ins[t], dst_ref=mine, send_sem=send_sems.at[t], recv_sem=recv_sems.at[t],
                device_id=(x, y, 1 - c), device_id_type=MESH)
            cp.start()
            cps.append((loc, cp, outs[t].at[pl.ds((1 - c) * r, r), :]))
        for t, (loc, cp, theirs) in enumerate(cps):
            pltpu.make_async_remote_copy(
                src_ref=ins[t], dst_ref=theirs, send_sem=send_sems.at[t], recv_sem=recv_sems.at[t],
                device_id=(x, y, 1 - c), device_id_type=MESH).wait_recv()
            cp.wait_send()
            loc.wait()

    return pl.pallas_call(
        body, name=name, out_shape=[jax.ShapeDtypeStruct((2 * v.shape[0], v.shape[1]), v.dtype) for v in halves],
        in_specs=_hbm_specs(nt), out_specs=_hbm_specs(nt),
        scratch_shapes=[pltpu.SemaphoreType.DMA((nt,)), pltpu.SemaphoreType.DMA((nt,)),
                        pltpu.SemaphoreType.DMA((nt,))],
    )(*halves)


BIG = ("w_in", "w_out", "w_gate", "w_up", "w_down")


def _reduce_scatter(l, big):
    n = f"l{l}_rs_"
    grads = [big[k] for k in BIG]
    got = _rs_to_sibling(n + "to_sibling", grads)
    c = lax.axis_index("c")
    mine = [lax.dynamic_slice_in_dim(g, c * (g.shape[1] // 2), g.shape[1] // 2, axis=1) for g in grads]
    sums = [_rs_pair_sum(n + "pair_sum_" + k, g, o) for k, g, o in zip(BIG, mine, got)]
    parts = _rs_to_owner(n + "to_owner", sums)
    halves = [_rs_chip_sum(n + "chip_sum_" + k, v) for k, v in zip(BIG, parts)]
    full = _rs_share_halves(n + "share", halves)
    return dict(zip(BIG, full))


def _adam_math(w, g, m, v):
    m = ADAM_B1 * m + (1.0 - ADAM_B1) * g
    v = ADAM_B2 * v + (1.0 - ADAM_B2) * (g * g)
    m_hat = m / (1.0 - ADAM_B1 ** ADAM_STEP)
    v_hat = v / (1.0 - ADAM_B2 ** ADAM_STEP)
    delta = -ADAM_LR * (m_hat / (jnp.sqrt(v_hat) + ADAM_EPS) + ADAM_WD * w)
    return delta, m, v


def _adam_big(name, g0, g1, w, m, v):
    _, r, cdim = w.shape
    tr = 128 if r % 128 == 0 else 64
    nb = r // tr

    def body(g0_ref, g1_ref, w_ref, m_ref, v_ref, go_ref, d_ref, mo_ref, vo_ref):
        l = pl.program_id(0)
        g = jnp.where(l == 0, g0_ref[...], g1_ref[...])
        delta, mn, vn = _adam_math(w_ref[...], g, m_ref[...], v_ref[...])
        go_ref[...] = g
        d_ref[...] = delta
        mo_ref[...] = mn
        vo_ref[...] = vn

    lay = pl.BlockSpec((None, tr, cdim), lambda l, i: (l, i, 0))
    return pl.pallas_call(
        body, name=name, grid=(2, nb),
        in_specs=[pl.BlockSpec((tr, cdim), lambda l, i: (i * (1 - l) + (nb - 1) * l, 0)),
                  pl.BlockSpec((tr, cdim), lambda l, i: (i * l, 0)), lay, lay, lay],
        out_specs=[lay] * 4, out_shape=[jax.ShapeDtypeStruct(w.shape, f32)] * 4,
        compiler_params=_cp(("arbitrary", "arbitrary")),
    )(g0, g1, w, m, v)


def _sum8(name, gathered):
    m_per = gathered.shape[0] // 8

    def body(g_ref, o_ref):
        tot = g_ref[pl.ds(0, m_per), :]
        for d in range(1, 8):
            tot = tot + g_ref[pl.ds(d * m_per, m_per), :]
        o_ref[...] = tot

    return pl.pallas_call(body, name=name, out_shape=jax.ShapeDtypeStruct((m_per, LANE), f32))(gathered)


def _adam_small(name, g, w, m, v):
    def body(g_ref, w_ref, m_ref, v_ref, d_ref, mo_ref, vo_ref):
        delta, mn, vn = _adam_math(w_ref[...], g_ref[...], m_ref[...], v_ref[...])
        d_ref[...] = delta
        mo_ref[...] = mn
        vo_ref[...] = vn

    return pl.pallas_call(body, name=name, out_shape=[jax.ShapeDtypeStruct(g.shape, f32)] * 3)(g, w, m, v)


def _pack(vals):
    rows, offs, at = [], [], 0
    for a in vals:
        a = a.reshape(-1)
        n = -(-a.shape[0] // LANE)
        rows.append(jnp.pad(a, (0, n * LANE - a.shape[0])).reshape(n, LANE))
        offs.append(at)
        at += n
    pad = -at % 8
    if pad:
        rows.append(jnp.zeros((pad, LANE), f32))
    return jnp.concatenate(rows, axis=0), offs


def _unpack(packed, offs, shapes):
    out = []
    for o, shp in zip(offs, shapes):
        size = 1
        for d in shp:
            size *= d
        n = -(-size // LANE)
        out.append(packed[o:o + n].reshape(-1)[:size].reshape(shp))
    return out


SMALL = ("norm_mix_g", "conv_w", "pool_w", "pool_scale", "gla_w_decay", "gla_b_decay", "gla_out_g",
         "fox_q_g", "fox_k_g", "fox_b_f", "norm_ffn_g")
ALL = ("norm_mix_g", "w_in", "conv_w", "pool_w", "pool_scale", "gla_w_decay", "gla_b_decay", "gla_out_g",
       "fox_q_g", "fox_k_g", "fox_b_f", "w_out", "norm_ffn_g", "w_gate", "w_up", "w_down")


def kernel(x, norm_mix_g, w_in, conv_w, pool_w, pool_scale, gla_w_decay, gla_b_decay, gla_out_g, fox_q_g, fox_k_g, fox_b_f, w_out, norm_ffn_g, w_gate, w_up, w_down, loss_target, m_norm_mix_g, m_w_in, m_conv_w, m_pool_w, m_pool_scale, m_gla_w_decay, m_gla_b_decay, m_gla_out_g, m_fox_q_g, m_fox_k_g, m_fox_b_f, m_w_out, m_norm_ffn_g, m_w_gate, m_w_up, m_w_down, v_norm_mix_g, v_w_in, v_conv_w, v_pool_w, v_pool_scale, v_gla_w_decay, v_gla_b_decay, v_gla_out_g, v_fox_q_g, v_fox_k_g, v_fox_b_f, v_w_out, v_norm_ffn_g, v_w_gate, v_w_up, v_w_down):
    w = dict(norm_mix_g=norm_mix_g, w_in=w_in, conv_w=conv_w, pool_w=pool_w, pool_scale=pool_scale,
             gla_w_decay=gla_w_decay, gla_b_decay=gla_b_decay, gla_out_g=gla_out_g, fox_q_g=fox_q_g, fox_k_g=fox_k_g,
             fox_b_f=fox_b_f, w_out=w_out, norm_ffn_g=norm_ffn_g, w_gate=w_gate, w_up=w_up, w_down=w_down)
    m = dict(norm_mix_g=m_norm_mix_g, w_in=m_w_in, conv_w=m_conv_w, pool_w=m_pool_w, pool_scale=m_pool_scale,
             gla_w_decay=m_gla_w_decay, gla_b_decay=m_gla_b_decay, gla_out_g=m_gla_out_g, fox_q_g=m_fox_q_g,
             fox_k_g=m_fox_k_g, fox_b_f=m_fox_b_f, w_out=m_w_out, norm_ffn_g=m_norm_ffn_g, w_gate=m_w_gate,
             w_up=m_w_up, w_down=m_w_down)
    v = dict(norm_mix_g=v_norm_mix_g, w_in=v_w_in, conv_w=v_conv_w, pool_w=v_pool_w, pool_scale=v_pool_scale,
             gla_w_decay=v_gla_w_decay, gla_b_decay=v_gla_b_decay, gla_out_g=v_gla_out_g, fox_q_g=v_fox_q_g,
             fox_k_g=v_fox_k_g, fox_b_f=v_fox_b_f, w_out=v_w_out, norm_ffn_g=v_norm_ffn_g, w_gate=v_w_gate,
             w_up=v_w_up, w_down=v_w_down)
    chip = 2 * lax.axis_index("x") + lax.axis_index("y")

    mine, offs = _pack([conv_w, gla_w_decay])
    rows = mine.shape[0]
    every = _allgather_small("gather_small_params", mine).reshape(NCHIP, 2, rows, LANE)[:, 0]
    per_chip = [_unpack(every[kk], offs, [conv_w.shape, gla_w_decay.shape]) for kk in range(NCHIP)]
    conv_full = jnp.concatenate([pc[0] for pc in per_chip], axis=-1)[:, :, 0, :]
    wdec_full = jnp.concatenate([pc[1] for pc in per_chip], axis=-1)

    gathered = _allgather_weights("gather_weights", [w[k].astype(bf16) for k in BIG])
    wg = dict(zip(BIG, gathered))

    wdec = jnp.pad(wdec_full.reshape(2, 16, 4, 64), ((0, 0), (GA_LANE, LANE - GA_LANE - 16), (0, 0), (0, 64)))
    p = dict(
        g_mix=norm_mix_g[:, None, :], g_ffn=norm_ffn_g[:, None, :], w_in=_win_from_blocks(wg["w_in"]),
        w_out=wg["w_out"], w_gate=wg["w_gate"], w_up=wg["w_up"], w_down=wg["w_down"],
        conv_w=conv_full, pool_w=pool_w, pool_scale=pool_scale[:, None, :],
        wdec=wdec.reshape(2, LANE, G),
        bdec=jnp.pad(gla_b_decay.reshape(2, 4, 64), ((0, 0), (0, 0), (0, 64))).reshape(2, 1, G),
        gla_og=gla_out_g[:, None, :], fox_qg=fox_q_g[:, None, :], fox_kg=fox_k_g[:, None, :],
        fox_bf=jnp.pad(fox_b_f, ((0, 0), (FF_LANE, LANE - FF_LANE - 4)))[:, None, :])

    h0 = x[0]
    h1, sv0 = _layer_fwd(0, h0, p)
    h2, sv1 = _layer_fwd(1, h1, p)
    sq, dh, dhb = _loss("loss", h2, loss_target[0])
    loss = lax.psum(sq[0, 0] * (0.5 / D), ("x", "y", "c"))

    dh, dhb, big1, small1 = _layer_bwd(1, dh, dhb, p, sv1)
    red1 = _reduce_scatter(1, big1)
    dh, dhb, big0, small0 = _layer_bwd(0, dh, dhb, p, sv0)
    red0 = _reduce_scatter(0, big0)

    grads, deltas, new_m, new_v = {}, {}, {}, {}
    for k in BIG:
        grads[k], deltas[k], new_m[k], new_v[k] = _adam_big("adam_" + k, red0[k], red1[k], w[k], m[k], v[k])

    packed, goffs = _pack([jnp.stack([small0[k], small1[k]]) for k in SMALL])
    total = _sum8("sum_small_grads", _allgather_small("gather_small_grads", packed))
    full_shapes = [(2,) + small0[k].shape for k in SMALL]
    gsmall = dict(zip(SMALL, _unpack(total, goffs, full_shapes)))
    gsmall["conv_w"] = lax.dynamic_slice_in_dim(gsmall["conv_w"], chip * LANE, LANE, axis=2)[:, :, None, :]
    gsmall["gla_w_decay"] = lax.dynamic_slice_in_dim(gsmall["gla_w_decay"], chip * 64, 64, axis=2)
    gp, loffs = _pack([gsmall[k] for k in SMALL])
    wp, _ = _pack([w[k] for k in SMALL])
    mp, _ = _pack([m[k] for k in SMALL])
    vp, _ = _pack([v[k] for k in SMALL])
    dp, mnp, vnp = _adam_small("adam_small", gp, wp, mp, vp)
    shapes = [w[k].shape for k in SMALL]
    for k, a, b, c_, d_ in zip(SMALL, _unpack(gp, loffs, shapes), _unpack(dp, loffs, shapes),
                               _unpack(mnp, loffs, shapes), _unpack(vnp, loffs, shapes)):
        grads[k], deltas[k], new_m[k], new_v[k] = a, b, c_, d_

    return (loss, dh[None], *[grads[k] for k in ALL], *[deltas[k] for k in ALL],
            *[new_m[k] for k in ALL], *[new_v[k] for k in ALL])
```

```python
import functools

import jax
import jax.numpy as jnp
from jax import lax
from jax.experimental import pallas as pl
from jax.experimental.pallas import tpu as pltpu

f32 = jnp.float32
bf16 = jnp.bfloat16

D = 2048
G = 512
DFF = 5632
NCHIP = 4
FB = DFF // NCHIP
WIN = 5140
WINB = WIN // NCHIP
EPS = 1e-6
CHUNK = 64
LANE = 128

CB, CC, CH, PU, GQ, GK, GV, GG, FQ, FK, FV, MISC = 0, 4, 8, 12, 16, 20, 24, 28, 32, 36, 40, 44
ZC = 45 * LANE
FF_LANE = 0
GA_LANE = 8

ADAM_LR, ADAM_B1, ADAM_B2, ADAM_EPS, ADAM_WD, ADAM_STEP = 0.001, 0.9, 0.999, 1e-08, 0.01, 10

VMEM_LIMIT = 60 * 1024 * 1024
MESH = pl.DeviceIdType.MESH


def _cp(sem=None):
    return pltpu.CompilerParams(dimension_semantics=sem, vmem_limit_bytes=VMEM_LIMIT)


def _dot(a, b, dims=((1,), (0,))):
    return lax.dot_general(a.astype(bf16), b.astype(bf16), (dims, ((), ())), preferred_element_type=f32)


def _bdot(a, b, ca, cb):
    return lax.dot_general(a.astype(bf16), b.astype(bf16), (((ca,), (cb,)), ((0,), (0,))),
                           preferred_element_type=f32)


def _log_sigmoid(x):
    return jnp.minimum(x, 0.0) - jnp.log(1.0 + jnp.exp(-jnp.abs(x)))


@jax.custom_vjp
def _sigmoid(x):
    return 1.0 / (1.0 + jnp.exp(-x))


def _sigmoid_fwd(x):
    s = _sigmoid(x)
    return s, s


def _sigmoid_bwd(s, g):
    return (g * s * (1.0 - s),)


_sigmoid.defvjp(_sigmoid_fwd, _sigmoid_bwd)


def _rms(x, g):
    return x * lax.rsqrt(jnp.mean(x * x, axis=-1, keepdims=True) + EPS) * g


def _shift_impl(x, n, period, transpose):
    rows = x.shape[0]
    t = lax.broadcasted_iota(jnp.int32, x.shape, 0)
    if period is not None:
        t = t & (period - 1)
    keep = t >= n
    if not transpose:
        return jnp.where(keep, pltpu.roll(x, n, 0), 0.0)
    return pltpu.roll(jnp.where(keep, x, 0.0), rows - n, 0)


def _shift(x, n, period=None):
    @jax.custom_vjp
    def f(v):
        return _shift_impl(v, n, period, False)

    def fwd(v):
        return f(v), None

    def bwd(_, g):
        return (_shift_impl(g, n, period, True),)

    f.defvjp(fwd, bwd)
    return f(x)


def _cumsum_rows(x, length, period=None):
    n = 1
    while n < length:
        x = x + _shift(x, n, period)
        n *= 2
    return x


def _convpool_fn(cb, cc, ch, pu, w0, w1, w2, pw, ps, j):
    u = cc * ch
    y = w2 * u + w1 * _shift(u, 1) + w0 * _shift(u, 2)
    ya = cb * y
    s2 = pu + _shift(pu, 1)
    s4 = s2 + _shift(s2, 2)
    s8 = s4 + _shift(s4, 4)
    s16 = s8 + _shift(s8, 8)
    wsum = jnp.where(j == 0, s2, jnp.where(j == 1, s4, jnp.where(j == 2, s8, s16)))
    width = (2 << j).astype(f32)
    t = lax.broadcasted_iota(jnp.int32, pu.shape, 0).astype(f32)
    count = jnp.minimum(t + 1.0, width)
    d = wsum / count - pu
    yb = _dot(d, pw) * ps
    return ya, yb


def _foxprep_fn(misc, bf):
    lf = _log_sigmoid(misc + bf)
    fc = _cumsum_rows(lf, lf.shape[0])
    return fc, jnp.transpose(fc)


def _fox_fn(q, k, v, fcol, frow8, qg, kg, h, i):
    tq, s = q.shape[0], k.shape[0]
    qn = _rms(q, qg)
    kn = _rms(k, kg)
    lg = _dot(qn, kn, ((1,), (1,))) * (LANE ** -0.5)
    lane = lax.broadcasted_iota(jnp.int32, fcol.shape, 1)
    fq = jnp.sum(jnp.where(lane == h, fcol, 0.0), axis=1, keepdims=True)
    row = lax.broadcasted_iota(jnp.int32, frow8.shape, 0)
    fk = jnp.sum(jnp.where(row == h, frow8, 0.0), axis=0, keepdims=True)
    lg = lg + fq - fk
    qpos = i * tq + lax.broadcasted_iota(jnp.int32, (tq, s), 0)
    kpos = lax.broadcasted_iota(jnp.int32, (tq, s), 1)
    lg = jnp.where(kpos <= qpos, lg, -jnp.inf)
    m = lax.stop_gradient(jnp.max(lg, axis=1, keepdims=True))
    e = jnp.exp(lg - m)
    p = e / jnp.sum(e, axis=1, keepdims=True)
    return _dot(p, v)


def _gla1_fn(q, k, v, misc, wd, bd):
    ts = q.shape[0]
    nb = ts // CHUNK
    x = _dot(misc, wd) + bd
    la = _log_sigmoid(x) * (1.0 / 16.0)
    cc = _cumsum_rows(la, CHUNK, CHUNK)
    la3 = la.reshape(nb, CHUNK, LANE)
    last3 = jnp.sum(la3, axis=1, keepdims=True)
    last2 = jnp.sum(la3, axis=1)
    cc3 = cc.reshape(nb, CHUNK, LANE)
    q3 = (q * 0.125).reshape(nb, CHUNK, LANE)
    k3 = k.reshape(nb, CHUNK, LANE)
    v3 = v.reshape(nb, CHUNK, LANE)
    ep = jnp.exp(cc3)
    en = jnp.exp(-cc3)
    qe = q3 * ep
    a1 = _bdot(qe, k3 * en, 2, 2)
    a2 = _bdot(q3 * en, k3 * ep, 2, 2)
    ti = lax.broadcasted_iota(jnp.int32, a1.shape, 1)
    si = lax.broadcasted_iota(jnp.int32, a1.shape, 2)
    sc = jnp.where(si <= ti, a1, a2)
    oi = _bdot(sc, v3, 2, 1)
    kd = k3 * jnp.exp(last3 - cc3)
    el = jnp.exp(last2)
    return qe.reshape(ts, LANE), kd.reshape(ts, LANE), el, oi.reshape(ts, LANE)


def _gla3_fn(o, gg, og):
    return _rms(o, og) * (gg * _sigmoid(gg))


def _ffn_fn(gate, up):
    return gate * _sigmoid(gate) * up


def _mm(name, a, b, a_spec, b_spec, out_shape, out_spec, grid, dims, nk, res=None, res_spec=None):
    has_res = res is not None
    nax = len(grid)

    def body(*refs):
        a_ref, b_ref = refs[0], refs[1]
        res_ref = refs[2] if has_res else None
        out_ref = refs[2 + has_res]
        part = _dot(a_ref[...], b_ref[...], dims)
        if nk == 1:
            if has_res:
                part = part + res_ref[...]
            out_ref[...] = part.astype(out_ref.dtype)
            return
        acc_ref = refs[3 + has_res]
        k = pl.program_id(nax - 1)

        @pl.when(k == 0)
        def _():
            acc_ref[...] = part

        @pl.when(k > 0)
        def _():
            acc_ref[...] += part

        @pl.when(k == nk - 1)
        def _():
            tot = acc_ref[...]
            if has_res:
                tot = tot + res_ref[...]
            out_ref[...] = tot.astype(out_ref.dtype)

    ops = [a, b] + ([res] if has_res else [])
    specs = [a_spec, b_spec] + ([res_spec] if has_res else [])
    blk = tuple(d for d in out_spec.block_shape if d is not None)
    scratch = [pltpu.VMEM(blk, f32)] if nk > 1 else []
    return pl.pallas_call(
        body, name=name, grid=grid, in_specs=specs, out_specs=out_spec, out_shape=out_shape,
        scratch_shapes=scratch,
        compiler_params=_cp(("parallel",) * (nax - 1) + ("arbitrary",)),
    )(*ops)


def _tm(s):
    return min(s, 512)


def _rmsnorm_fwd(name, x, g, l):
    s = x.shape[0]
    tm = min(s, 256)

    def body(x_ref, g_ref, u_ref):
        u_ref[...] = _rms(x_ref[...], g_ref[...]).astype(bf16)

    return pl.pallas_call(
        body, name=name, grid=(s // tm,),
        in_specs=[pl.BlockSpec((tm, D), lambda i: (i, 0)), pl.BlockSpec((None, 1, D), lambda i: (l, 0, 0))],
        out_specs=pl.BlockSpec((tm, D), lambda i: (i, 0)),
        out_shape=jax.ShapeDtypeStruct((s, D), bf16), compiler_params=_cp(("parallel",)),
    )(x, g)


def _rmsnorm_bwd(name, x, g, du, dres, l):
    s = x.shape[0]
    tm = min(s, 256)

    def body(x_ref, g_ref, du_ref, dres_ref, dx_ref, dxb_ref, dg_ref):
        _, vjp = jax.vjp(_rms, x_ref[...], g_ref[...])
        dx, dg = vjp(du_ref[...])
        tot = dx + dres_ref[...]
        dx_ref[...] = tot
        dxb_ref[...] = tot.astype(bf16)

        @pl.when(pl.program_id(0) == 0)
        def _():
            dg_ref[...] = dg

        @pl.when(pl.program_id(0) > 0)
        def _():
            dg_ref[...] += dg

    row = pl.BlockSpec((tm, D), lambda i: (i, 0))
    return pl.pallas_call(
        body, name=name, grid=(s // tm,),
        in_specs=[row, pl.BlockSpec((None, 1, D), lambda i: (l, 0, 0)), row, row],
        out_specs=[row, row, pl.BlockSpec((1, D), lambda i: (0, 0))],
        out_shape=[jax.ShapeDtypeStruct((s, D), f32), jax.ShapeDtypeStruct((s, D), bf16),
                   jax.ShapeDtypeStruct((1, D), f32)],
        compiler_params=_cp(("arbitrary",)),
    )(x, g, du, dres)


def _ffn_act(name, gate, up):
    s = gate.shape[0]
    tm = min(s, 256)
    spec = pl.BlockSpec((tm, FB), lambda i, j: (i, j))

    def body(g_ref, u_ref, a_ref):
        a_ref[...] = _ffn_fn(g_ref[...], u_ref[...]).astype(bf16)

    return pl.pallas_call(
        body, name=name, grid=(s // tm, NCHIP), in_specs=[spec, spec], out_specs=spec,
        out_shape=jax.ShapeDtypeStruct((s, DFF), bf16), compiler_params=_cp(("parallel", "parallel")),
    )(gate, up)


def _ffn_act_bwd(name, gate, up, dact):
    s = gate.shape[0]
    tm = min(s, 256)
    spec = pl.BlockSpec((tm, FB), lambda i, j: (i, j))

    def body(g_ref, u_ref, d_ref, dg_ref, du_ref):
        _, vjp = jax.vjp(_ffn_fn, g_ref[...], u_ref[...])
        dg, du = vjp(d_ref[...])
        dg_ref[...] = dg.astype(bf16)
        du_ref[...] = du.astype(bf16)

    return pl.pallas_call(
        body, name=name, grid=(s // tm, NCHIP), in_specs=[spec, spec, spec], out_specs=[spec, spec],
        out_shape=[jax.ShapeDtypeStruct((s, DFF), bf16)] * 2, compiler_params=_cp(("parallel", "parallel")),
    )(gate, up, dact)


def _loss(name, y, t):
    s = y.shape[0]
    tm = min(s, 256)
    row = pl.BlockSpec((tm, D), lambda i: (i, 0))

    def body(y_ref, t_ref, l_ref, d_ref, db_ref):
        e = y_ref[...] - t_ref[...]
        d = e * (1.0 / D)
        d_ref[...] = d
        db_ref[...] = d.astype(bf16)
        part = jnp.zeros((8, LANE), f32) + jnp.sum(e * e)

        @pl.when(pl.program_id(0) == 0)
        def _():
            l_ref[...] = part

        @pl.when(pl.program_id(0) > 0)
        def _():
            l_ref[...] += part

    return pl.pallas_call(
        body, name=name, grid=(s // tm,), in_specs=[row, row],
        out_specs=[pl.BlockSpec((8, LANE), lambda i: (0, 0)), row, row],
        out_shape=[jax.ShapeDtypeStruct((8, LANE), f32), jax.ShapeDtypeStruct((s, D), f32),
                   jax.ShapeDtypeStruct((s, D), bf16)],
        compiler_params=_cp(("arbitrary",)),
    )(y, t)


def _zspec(s, blk):
    return pl.BlockSpec((s, LANE), lambda j: (0, blk + j))


def _convpool_specs(s, l):
    return [_zspec(s, CB), _zspec(s, CC), _zspec(s, CH), _zspec(s, PU),
            pl.BlockSpec((None, 3, LANE), lambda j: (l, 0, j)),
            pl.BlockSpec((None, None, LANE, LANE), lambda j: (l, j, 0, 0)),
            pl.BlockSpec((None, 1, LANE), lambda j: (l, 0, j))]


def _convpool_fwd(name, z, conv_w, pool_w, pool_scale, l):
    s = z.shape[0]

    def body(cb, cc, ch, pu, cw, pw, ps, ya_ref, yb_ref):
        ya, yb = _convpool_fn(cb[...], cc[...], ch[...], pu[...], cw[0:1, :], cw[1:2, :], cw[2:3, :], pw[...], ps[...],
                              pl.program_id(0))
        ya_ref[...] = ya.astype(bf16)
        yb_ref[...] = yb.astype(bf16)

    col = pl.BlockSpec((s, LANE), lambda j: (0, j))
    return pl.pallas_call(
        body, name=name, grid=(4,), in_specs=_convpool_specs(s, l), out_specs=[col, col],
        out_shape=[jax.ShapeDtypeStruct((s, G), bf16)] * 2, compiler_params=_cp(("parallel",)),
    )(z, z, z, z, conv_w, pool_w, pool_scale)


def _convpool_bwd(name, z, conv_w, pool_w, pool_scale, dy, l):
    s = z.shape[0]

    def body(cb, cc, ch, pu, cw, pw, ps, dya, dyb, dcb, dcc, dch, dpu, dcw, dpw, dps):
        j = pl.program_id(0)
        fn = functools.partial(_convpool_fn, j=j)
        _, vjp = jax.vjp(fn, cb[...], cc[...], ch[...], pu[...], cw[0:1, :], cw[1:2, :], cw[2:3, :], pw[...], ps[...])
        g = vjp((dya[...], dyb[...]))
        dcb[...] = g[0].astype(bf16)
        dcc[...] = g[1].astype(bf16)
        dch[...] = g[2].astype(bf16)
        dpu[...] = g[3].astype(bf16)
        dcw[0:1, :] = g[4]
        dcw[1:2, :] = g[5]
        dcw[2:3, :] = g[6]
        dpw[...] = g[7]
        dps[...] = g[8]

    col = pl.BlockSpec((s, LANE), lambda j: (0, j))
    specs = _convpool_specs(s, l) + [pl.BlockSpec((s, LANE), lambda j: (0, j)),
                                     pl.BlockSpec((s, LANE), lambda j: (0, 4 + j))]
    return pl.pallas_call(
        body, name=name, grid=(4,), in_specs=specs,
        out_specs=[col, col, col, col, pl.BlockSpec((3, LANE), lambda j: (0, j)),
                   pl.BlockSpec((None, LANE, LANE), lambda j: (j, 0, 0)), pl.BlockSpec((1, LANE), lambda j: (0, j))],
        out_shape=[jax.ShapeDtypeStruct((s, G), bf16)] * 4 + [
            jax.ShapeDtypeStruct((3, G), f32), jax.ShapeDtypeStruct((4, LANE, LANE), f32),
            jax.ShapeDtypeStruct((1, G), f32)],
        compiler_params=_cp(("parallel",)),
    )(z, z, z, z, conv_w, pool_w, pool_scale, dy, dy)


def _foxprep_fwd(name, z, bf, l):
    s = z.shape[0]

    def body(m_ref, b_ref, fc_ref, fr_ref):
        fc, fr = _foxprep_fn(m_ref[...], b_ref[...])
        fc_ref[...] = fc
        fr_ref[...] = fr

    return pl.pallas_call(
        body, name=name, grid=(1,),
        in_specs=[pl.BlockSpec((s, LANE), lambda i: (0, MISC)), pl.BlockSpec((None, 1, LANE), lambda i: (l, 0, 0))],
        out_specs=[pl.BlockSpec((s, LANE), lambda i: (0, 0)), pl.BlockSpec((LANE, s), lambda i: (0, 0))],
        out_shape=[jax.ShapeDtypeStruct((s, LANE), f32), jax.ShapeDtypeStruct((LANE, s), f32)],
        compiler_params=_cp(("arbitrary",)),
    )(z, bf)


def _foxprep_bwd(name, z, bf, dfc4, dfr4, dmisc4, l):
    s = z.shape[0]

    def body(m_ref, b_ref, dfc_ref, dfr_ref, dm4_ref, dm_ref, db_ref):
        _, vjp = jax.vjp(_foxprep_fn, m_ref[...], b_ref[...])
        dfc = dfc_ref[0] + dfc_ref[1] + dfc_ref[2] + dfc_ref[3]
        dfr = dfr_ref[0] + dfr_ref[1] + dfr_ref[2] + dfr_ref[3]
        dfr = jnp.concatenate([dfr, jnp.zeros((LANE - 8, s), f32)], axis=0)
        dm, db = vjp((dfc, dfr))
        dm = dm + (dm4_ref[0] + dm4_ref[1] + dm4_ref[2] + dm4_ref[3])
        dm_ref[...] = dm.astype(bf16)
        db_ref[...] = db

    whole = lambda shape: pl.BlockSpec(shape, lambda i: (0,) * len(shape))
    return pl.pallas_call(
        body, name=name, grid=(1,),
        in_specs=[pl.BlockSpec((s, LANE), lambda i: (0, MISC)), pl.BlockSpec((None, 1, LANE), lambda i: (l, 0, 0)),
                  whole((4, s, LANE)), whole((4, 8, s)), whole((4, s, LANE))],
        out_specs=[whole((s, LANE)), whole((1, LANE))],
        out_shape=[jax.ShapeDtypeStruct((s, LANE), bf16), jax.ShapeDtypeStruct((1, LANE), f32)],
        compiler_params=_cp(("arbitrary",)),
    )(z, bf, dfc4, dfr4, dmisc4)


def _fox_specs(s, tq, l):
    return [pl.BlockSpec((tq, LANE), lambda h, i: (i, FQ + h)),
            pl.BlockSpec((s, LANE), lambda h, i: (0, FK + h)),
            pl.BlockSpec((s, LANE), lambda h, i: (0, FV + h)),
            pl.BlockSpec((tq, LANE), lambda h, i: (i, 0)),
            pl.BlockSpec((8, s), lambda h, i: (0, 0)),
            pl.BlockSpec((None, 1, LANE), lambda h, i: (l, 0, 0)),
            pl.BlockSpec((None, 1, LANE), lambda h, i: (l, 0, 0))]


def _fox_fwd(name, z, fc, fr, qg, kg, l):
    s = z.shape[0]
    tq = min(s, 256)

    def body(q, k, v, fc_ref, fr_ref, qg_ref, kg_ref, y_ref):
        y = _fox_fn(q[...], k[...], v[...], fc_ref[...], fr_ref[...], qg_ref[...], kg_ref[...],
                    pl.program_id(0), pl.program_id(1))
        y_ref[...] = y.astype(bf16)

    return pl.pallas_call(
        body, name=name, grid=(4, s // tq), in_specs=_fox_specs(s, tq, l),
        out_specs=pl.BlockSpec((tq, LANE), lambda h, i: (i, h)),
        out_shape=jax.ShapeDtypeStruct((s, G), bf16), compiler_params=_cp(("parallel", "parallel")),
    )(z, z, z, fc, fr, qg, kg)


def _fox_bwd(name, z, fc, fr, qg, kg, dy, l):
    s = z.shape[0]
    tq = min(s, 256)

    def body(q, k, v, fc_ref, fr_ref, qg_ref, kg_ref, dy_ref, dq, dk, dv, dfc, dfr, dqg, dkg):
        h, i = pl.program_id(0), pl.program_id(1)
        fn = functools.partial(_fox_fn, h=h, i=i)
        _, vjp = jax.vjp(fn, q[...], k[...], v[...], fc_ref[...], fr_ref[...], qg_ref[...], kg_ref[...])
        g = vjp(dy_ref[...])
        dq[...] = g[0].astype(bf16)
        dfc[...] = g[3]

        @pl.when(i == 0)
        def _():
            dk[...] = g[1]
            dv[...] = g[2]
            dfr[...] = g[4]
            dqg[...] = g[5]
            dkg[...] = g[6]

        @pl.when(i > 0)
        def _():
            dk[...] += g[1]
            dv[...] += g[2]
            dfr[...] += g[4]
            dqg[...] += g[5]
            dkg[...] += g[6]

    specs = _fox_specs(s, tq, l) + [pl.BlockSpec((tq, LANE), lambda h, i: (i, 12 + h))]
    head = pl.BlockSpec((s, LANE), lambda h, i: (0, h))
    gain = pl.BlockSpec((None, 1, LANE), lambda h, i: (h, 0, 0))
    return pl.pallas_call(
        body, name=name, grid=(4, s // tq), in_specs=specs,
        out_specs=[pl.BlockSpec((tq, LANE), lambda h, i: (i, h)), head, head,
                   pl.BlockSpec((None, tq, LANE), lambda h, i: (h, i, 0)),
                   pl.BlockSpec((None, 8, s), lambda h, i: (h, 0, 0)), gain, gain],
        out_shape=[jax.ShapeDtypeStruct((s, G), bf16), jax.ShapeDtypeStruct((s, G), f32),
                   jax.ShapeDtypeStruct((s, G), f32), jax.ShapeDtypeStruct((4, s, LANE), f32),
                   jax.ShapeDtypeStruct((4, 8, s), f32), jax.ShapeDtypeStruct((4, 1, LANE), f32),
                   jax.ShapeDtypeStruct((4, 1, LANE), f32)],
        compiler_params=_cp(("parallel", "arbitrary")),
    )(z, z, z, fc, fr, qg, kg, dy)


def _gla_ts(s):
    return min(s, 512)


def _gla1_specs(s, ts, l):
    return [pl.BlockSpec((ts, LANE), lambda h, i: (i, GQ + h)),
            pl.BlockSpec((ts, LANE), lambda h, i: (i, GK + h)),
            pl.BlockSpec((ts, LANE), lambda h, i: (i, GV + h)),
            pl.BlockSpec((ts, LANE), lambda h, i: (i, MISC)),
            pl.BlockSpec((None, LANE, LANE), lambda h, i: (l, 0, h)),
            pl.BlockSpec((None, 1, LANE), lambda h, i: (l, 0, h))]


def _gla1_fwd(name, z, wd, bd, l):
    s = z.shape[0]
    ts = _gla_ts(s)
    nb = ts // CHUNK

    def body(q, k, v, m, wd_ref, bd_ref, qe_ref, kd_ref, el_ref, oi_ref):
        qe, kd, el, oi = _gla1_fn(q[...], k[...], v[...], m[...], wd_ref[...], bd_ref[...])
        qe_ref[...] = qe.astype(bf16)
        kd_ref[...] = kd.astype(bf16)
        el_ref[...] = el
        oi_ref[...] = oi

    blk = pl.BlockSpec((ts, LANE), lambda h, i: (i, h))
    return pl.pallas_call(
        body, name=name, grid=(4, s // ts), in_specs=_gla1_specs(s, ts, l),
        out_specs=[blk, blk, pl.BlockSpec((nb, LANE), lambda h, i: (i, h)), blk],
        out_shape=[jax.ShapeDtypeStruct((s, G), bf16), jax.ShapeDtypeStruct((s, G), bf16),
                   jax.ShapeDtypeStruct((s // CHUNK, G), f32), jax.ShapeDtypeStruct((s, G), f32)],
        compiler_params=_cp(("parallel", "parallel")),
    )(z, z, z, z, wd, bd)


def _gla1_bwd(name, z, wd, bd, dqe, dkd, del_, do, dvi, l):
    s = z.shape[0]
    ts = _gla_ts(s)
    nb = ts // CHUNK

    def body(q, k, v, m, wd_ref, bd_ref, dqe_ref, dkd_ref, del_ref, do_ref, dvi_ref, dq, dk, dv, dm, dwd, dbd):
        i = pl.program_id(1)
        _, vjp = jax.vjp(_gla1_fn, q[...], k[...], v[...], m[...], wd_ref[...], bd_ref[...])
        g = vjp((dqe_ref[...], dkd_ref[...], del_ref[...], do_ref[...]))
        dq[...] = g[0].astype(bf16)
        dk[...] = g[1].astype(bf16)
        dv[...] = (g[2] + dvi_ref[...]).astype(bf16)
        dm[...] = g[3]

        @pl.when(i == 0)
        def _():
            dwd[...] = g[4]
            dbd[...] = g[5]

        @pl.when(i > 0)
        def _():
            dwd[...] += g[4]
            dbd[...] += g[5]

    blk = pl.BlockSpec((ts, LANE), lambda h, i: (i, h))
    specs = _gla1_specs(s, ts, l) + [blk, blk, pl.BlockSpec((nb, LANE), lambda h, i: (i, h)), blk, blk]
    return pl.pallas_call(
        body, name=name, grid=(4, s // ts), in_specs=specs,
        out_specs=[blk, blk, blk, pl.BlockSpec((None, ts, LANE), lambda h, i: (h, i, 0)),
                   pl.BlockSpec((None, LANE, LANE), lambda h, i: (h, 0, 0)),
                   pl.BlockSpec((None, 1, LANE), lambda h, i: (h, 0, 0))],
        out_shape=[jax.ShapeDtypeStruct((s, G), bf16)] * 3 + [
            jax.ShapeDtypeStruct((4, s, LANE), f32), jax.ShapeDtypeStruct((4, LANE, LANE), f32),
            jax.ShapeDtypeStruct((4, 1, LANE), f32)],
        compiler_params=_cp(("parallel", "arbitrary")),
    )(z, z, z, z, wd, bd, dqe, dkd, del_, do, dvi)


def _gla2_fwd(name, z, qe, kd, el, oi):
    s = z.shape[0]
    n = s // CHUNK

    def body(v_ref, qe_ref, kd_ref, el_ref, oi_ref, o_ref, st_ref, cur):
        cur[...] = jnp.zeros_like(cur)

        def step(c, carry):
            rows = pl.ds(pl.multiple_of(c * CHUNK, CHUNK), CHUNK)
            st = cur[...]
            st_ref[c] = st
            o_ref[rows, :] = oi_ref[rows, :] + _dot(qe_ref[rows, :], st, ((1,), (1,)))
            cur[...] = st * el_ref[pl.ds(c, 1), :] + _dot(v_ref[rows, :], kd_ref[rows, :], ((0,), (0,)))
            return carry

        lax.fori_loop(0, n, step, 0)

    head = pl.BlockSpec((s, LANE), lambda h: (0, h))
    return pl.pallas_call(
        body, name=name, grid=(4,),
        in_specs=[pl.BlockSpec((s, LANE), lambda h: (0, GV + h)), head, head,
                  pl.BlockSpec((n, LANE), lambda h: (0, h)), head],
        out_specs=[head, pl.BlockSpec((None, n, LANE, LANE), lambda h: (h, 0, 0, 0))],
        out_shape=[jax.ShapeDtypeStruct((s, G), f32), jax.ShapeDtypeStruct((4, n, LANE, LANE), f32)],
        scratch_shapes=[pltpu.VMEM((LANE, LANE), f32)],
        compiler_params=_cp(("parallel",)),
    )(z, qe, kd, el, oi)


def _gla2_bwd(name, z, qe, kd, el, st, do):
    s = z.shape[0]
    n = s // CHUNK

    def body(v_ref, qe_ref, kd_ref, el_ref, st_ref, do_ref, dqe_ref, dkd_ref, dv_ref, del_ref, dcur):
        dcur[...] = jnp.zeros_like(dcur)

        def step(t, carry):
            c = n - 1 - t
            rows = pl.ds(pl.multiple_of(c * CHUNK, CHUNK), CHUNK)
            dn = dcur[...]
            stc = st_ref[c]
            doc = do_ref[rows, :]
            dqe_ref[rows, :] = _dot(doc, stc)
            dv_ref[rows, :] = _dot(kd_ref[rows, :], dn, ((1,), (1,)))
            dkd_ref[rows, :] = _dot(v_ref[rows, :], dn)
            del_ref[pl.ds(c, 1), :] = jnp.sum(stc * dn, axis=0, keepdims=True)
            dcur[...] = dn * el_ref[pl.ds(c, 1), :] + _dot(doc, qe_ref[rows, :], ((0,), (0,)))
            return carry

        lax.fori_loop(0, n, step, 0)

    head = pl.BlockSpec((s, LANE), lambda h: (0, h))
    chunk = pl.BlockSpec((n, LANE), lambda h: (0, h))
    return pl.pallas_call(
        body, name=name, grid=(4,),
        in_specs=[pl.BlockSpec((s, LANE), lambda h: (0, GV + h)), head, head, chunk,
                  pl.BlockSpec((None, n, LANE, LANE), lambda h: (h, 0, 0, 0)), head],
        out_specs=[head, head, head, chunk],
        out_shape=[jax.ShapeDtypeStruct((s, G), f32)] * 3 + [jax.ShapeDtypeStruct((n, G), f32)],
        scratch_shapes=[pltpu.VMEM((LANE, LANE), f32)],
        compiler_params=_cp(("parallel",)),
    )(z, qe, kd, el, st, do)


def _gla3_specs(ts, l):
    return [pl.BlockSpec((ts, LANE), lambda h, i: (i, h)),
            pl.BlockSpec((ts, LANE), lambda h, i: (i, GG + h)),
            pl.BlockSpec((None, 1, LANE), lambda h, i: (l, 0, 0))]


def _gla3_fwd(name, o, z, og, l):
    s = z.shape[0]
    ts = _gla_ts(s)

    def body(o_ref, g_ref, og_ref, y_ref):
        y_ref[...] = _gla3_fn(o_ref[...], g_ref[...], og_ref[...]).astype(bf16)

    return pl.pallas_call(
        body, name=name, grid=(4, s // ts), in_specs=_gla3_specs(ts, l),
        out_specs=pl.BlockSpec((ts, LANE), lambda h, i: (i, h)),
        out_shape=jax.ShapeDtypeStruct((s, G), bf16), compiler_params=_cp(("parallel", "parallel")),
    )(o, z, og)


def _gla3_bwd(name, o, z, og, dy, l):
    s = z.shape[0]
    ts = _gla_ts(s)

    def body(o_ref, g_ref, og_ref, dy_ref, do_ref, dg_ref, dog_ref):
        i = pl.program_id(1)
        _, vjp = jax.vjp(_gla3_fn, o_ref[...], g_ref[...], og_ref[...])
        g = vjp(dy_ref[...])
        do_ref[...] = g[0]
        dg_ref[...] = g[1].astype(bf16)

        @pl.when(i == 0)
        def _():
            dog_ref[...] = g[2]

        @pl.when(i > 0)
        def _():
            dog_ref[...] += g[2]

    blk = pl.BlockSpec((ts, LANE), lambda h, i: (i, h))
    return pl.pallas_call(
        body, name=name, grid=(4, s // ts),
        in_specs=_gla3_specs(ts, l) + [pl.BlockSpec((ts, LANE), lambda h, i: (i, 8 + h))],
        out_specs=[blk, blk, pl.BlockSpec((None, 1, LANE), lambda h, i: (h, 0, 0))],
        out_shape=[jax.ShapeDtypeStruct((s, G), f32), jax.ShapeDtypeStruct((s, G), bf16),
                   jax.ShapeDtypeStruct((4, 1, LANE), f32)],
        compiler_params=_cp(("parallel", "arbitrary")),
    )(o, z, og, dy)


def _layer_fwd(l, h, p):
    s = h.shape[0]
    tm = _tm(s)
    n = f"l{l}_"
    u = _rmsnorm_fwd(n + "norm_mix", h, p["g_mix"], l)
    z = _mm(n + "mm_in", u, p["w_in"],
            pl.BlockSpec((tm, D), lambda j, i, k: (i, 0)), pl.BlockSpec((None, D, 1152), lambda j, i, k: (l, 0, j)),
            jax.ShapeDtypeStruct((s, ZC), f32), pl.BlockSpec((tm, 1152), lambda j, i, k: (i, j)),
            (ZC // 1152, s // tm, 1), ((1,), (0,)), 1)
    ya, yb = _convpool_fwd(n + "convpool", z, p["conv_w"], p["pool_w"], p["pool_scale"], l)
    qe, kd, el, oi = _gla1_fwd(n + "gla_chunk", z, p["wdec"], p["bdec"], l)
    o, st = _gla2_fwd(n + "gla_scan", z, qe, kd, el, oi)
    yc = _gla3_fwd(n + "gla_out", o, z, p["gla_og"], l)
    fc, fr = _foxprep_fwd(n + "fox_prep", z, p["fox_bf"], l)
    yd = _fox_fwd(n + "fox_attn", z, fc, fr, p["fox_qg"], p["fox_kg"], l)
    y = jnp.concatenate([ya, yb, yc, yd], axis=1)
    h1 = _mm(n + "mm_out", y, p["w_out"],
             pl.BlockSpec((tm, G), lambda j, i, k: (i, k)),
             pl.BlockSpec((None, None, G, 1024), lambda j, i, k: (k, l, 0, j)),
             jax.ShapeDtypeStruct((s, D), f32), pl.BlockSpec((tm, 1024), lambda j, i, k: (i, j)),
             (2, s // tm, NCHIP), ((1,), (0,)), NCHIP, res=h, res_spec=pl.BlockSpec((tm, 1024), lambda j, i, k: (i, j)))
    u2 = _rmsnorm_fwd(n + "norm_ffn", h1, p["g_ffn"], l)
    ffn_in = (pl.BlockSpec((tm, D), lambda j, i, k: (i, 0)),
              pl.BlockSpec((None, None, D, FB), lambda j, i, k: (j, l, 0, 0)))
    ffn_out = (jax.ShapeDtypeStruct((s, DFF), f32), pl.BlockSpec((tm, FB), lambda j, i, k: (i, j)))
    gate = _mm(n + "mm_gate", u2, p["w_gate"], *ffn_in, *ffn_out, (NCHIP, s // tm, 1), ((1,), (0,)), 1)
    up = _mm(n + "mm_up", u2, p["w_up"], *ffn_in, *ffn_out, (NCHIP, s // tm, 1), ((1,), (0,)), 1)
    act = _ffn_act(n + "ffn_act", gate, up)
    h2 = _mm(n + "mm_down", act, p["w_down"],
             pl.BlockSpec((tm, FB), lambda j, i, k: (i, k)),
             pl.BlockSpec((None, None, FB, 1024), lambda j, i, k: (k, l, 0, j)),
             jax.ShapeDtypeStruct((s, D), f32), pl.BlockSpec((tm, 1024), lambda j, i, k: (i, j)),
             (2, s // tm, NCHIP), ((1,), (0,)), NCHIP, res=h1, res_spec=pl.BlockSpec((tm, 1024), lambda j, i, k: (i, j)))
    saved = dict(h=h, u=u, z=z, qe=qe, kd=kd, el=el, st=st, o=o, fc=fc, fr=fr, y=y, h1=h1, u2=u2,
                 gate=gate, up=up, act=act)
    return h2, saved


def _mm_tn(name, a, b, ta, tb, out_shape, out_spec, grid):
    s = a.shape[0]
    return _mm(name, a, b, pl.BlockSpec((s, ta), lambda i, j, k: (0, i)), pl.BlockSpec((s, tb), lambda i, j, k: (0, j)),
               out_shape, out_spec, grid, ((0,), (0,)), 1)


def _layer_bwd(l, dh2, dh2b, p, sv):
    s = dh2.shape[0]
    tm = _tm(s)
    n = f"l{l}_bwd_"
    dact = _mm(n + "mm_dact", dh2b, p["w_down"],
               pl.BlockSpec((tm, D), lambda j, i, k: (i, 0)),
               pl.BlockSpec((None, None, FB, D), lambda j, i, k: (j, l, 0, 0)),
               jax.ShapeDtypeStruct((s, DFF), f32), pl.BlockSpec((tm, FB), lambda j, i, k: (i, j)),
               (NCHIP, s // tm, 1), ((1,), (1,)), 1)
    g_wd = _mm_tn(n + "mm_dwd", sv["act"], dh2b, FB, 1024, jax.ShapeDtypeStruct((NCHIP, FB, D), bf16),
                  pl.BlockSpec((None, FB, 1024), lambda i, j, k: (i, 0, j)), (NCHIP, 2, 1))
    dgate, dup = _ffn_act_bwd(n + "ffn_act", sv["gate"], sv["up"], dact)
    nt_in = lambda: (pl.BlockSpec((tm, FB), lambda j, i, k: (i, k)),
                     pl.BlockSpec((None, None, 1024, FB), lambda j, i, k: (k, l, j, 0)))
    nt_out = lambda: (jax.ShapeDtypeStruct((s, D), f32), pl.BlockSpec((tm, 1024), lambda j, i, k: (i, j)))
    du2 = _mm(n + "mm_du2_gate", dgate, p["w_gate"], *nt_in(), *nt_out(), (2, s // tm, NCHIP), ((1,), (1,)), NCHIP)
    du2 = _mm(n + "mm_du2_up", dup, p["w_up"], *nt_in(), *nt_out(), (2, s // tm, NCHIP), ((1,), (1,)), NCHIP,
              res=du2, res_spec=pl.BlockSpec((tm, 1024), lambda j, i, k: (i, j)))
    wg_shape = jax.ShapeDtypeStruct((NCHIP, D, FB), bf16)
    wg_spec = lambda: pl.BlockSpec((None, 1024, FB), lambda i, j, k: (j, i, 0))
    g_wg = _mm_tn(n + "mm_dwg", sv["u2"], dgate, 1024, FB, wg_shape, wg_spec(), (2, NCHIP, 1))
    g_wu = _mm_tn(n + "mm_dwu", sv["u2"], dup, 1024, FB, wg_shape, wg_spec(), (2, NCHIP, 1))
    dh1, dh1b, dg_ffn = _rmsnorm_bwd(n + "norm_ffn", sv["h1"], p["g_ffn"], du2, dh2, l)
    dy = _mm(n + "mm_dy", dh1b, p["w_out"],
             pl.BlockSpec((tm, D), lambda j, i, k: (i, 0)),
             pl.BlockSpec((None, None, G, D), lambda j, i, k: (j, l, 0, 0)),
             jax.ShapeDtypeStruct((s, D), f32), pl.BlockSpec((tm, G), lambda j, i, k: (i, j)),
             (NCHIP, s // tm, 1), ((1,), (1,)), 1)
    g_wo = _mm_tn(n + "mm_dwo", sv["y"], dh1b, G, 1024, jax.ShapeDtypeStruct((NCHIP, G, D), bf16),
                  pl.BlockSpec((None, G, 1024), lambda i, j, k: (i, 0, j)), (NCHIP, 2, 1))
    z = sv["z"]
    dcb, dcc, dch, dpu, dconv, dpoolw, dpools = _convpool_bwd(
        n + "convpool", z, p["conv_w"], p["pool_w"], p["pool_scale"], dy, l)
    do, dgg, dog = _gla3_bwd(n + "gla_out", sv["o"], z, p["gla_og"], dy, l)
    dqe, dkd, dvi, del_ = _gla2_bwd(n + "gla_scan", z, sv["qe"], sv["kd"], sv["el"], sv["st"], do)
    dgq, dgk, dgv, dmisc4, dwd, dbd = _gla1_bwd(n + "gla_chunk", z, p["wdec"], p["bdec"], dqe, dkd, del_, do, dvi, l)
    dfq, dfk, dfv, dfc4, dfr4, dqg, dkg = _fox_bwd(n + "fox_attn", z, sv["fc"], sv["fr"], p["fox_qg"], p["fox_kg"], dy, l)
    dmisc, dbf = _foxprep_bwd(n + "fox_prep", z, p["fox_bf"], dfc4, dfr4, dmisc4, l)
    dz = jnp.concatenate([dcb, dcc, dch, dpu, dgq, dgk, dgv, dgg, dfq, dfk.astype(bf16), dfv.astype(bf16), dmisc],
                         axis=1)
    du = _mm(n + "mm_du", dz, p["w_in"],
             pl.BlockSpec((tm, 1152), lambda j, i, k: (i, k)), pl.BlockSpec((None, 1024, 1152), lambda j, i, k: (l, j, k)),
             jax.ShapeDtypeStruct((s, D), f32), pl.BlockSpec((tm, 1024), lambda j, i, k: (i, j)),
             (2, s // tm, ZC // 1152), ((1,), (1,)), ZC // 1152)
    g_wi = _mm_tn(n + "mm_dwi", sv["u"], dz, 1024, 1152, jax.ShapeDtypeStruct((D, ZC), bf16),
                  pl.BlockSpec((1024, 1152), lambda i, j, k: (i, j)), (2, ZC // 1152, 1))
    dh, dhb, dg_mix = _rmsnorm_bwd(n + "norm_mix", sv["h"], p["g_mix"], du, dh1, l)
    big = dict(w_in=_win_to_blocks(g_wi), w_out=g_wo, w_gate=g_wg, w_up=g_wu, w_down=g_wd)
    small = dict(
        norm_mix_g=dg_mix[0], norm_ffn_g=dg_ffn[0], conv_w=dconv, pool_w=dpoolw, pool_scale=dpools[0],
        gla_w_decay=jnp.concatenate([dwd[hh, GA_LANE:GA_LANE + 16, :64] for hh in range(4)], axis=1),
        gla_b_decay=jnp.concatenate([dbd[hh, 0, :64] for hh in range(4)]),
        gla_out_g=jnp.sum(dog[:, 0, :], axis=0), fox_q_g=jnp.sum(dqg[:, 0, :], axis=0),
        fox_k_g=jnp.sum(dkg[:, 0, :], axis=0), fox_b_f=dbf[0, FF_LANE:FF_LANE + 4])
    return dh, dhb, big, small


def _win_from_blocks(wb):
    def cols(a, b):
        parts = []
        for kk in range(NCHIP):
            lo, hi = max(a, kk * WINB), min(b, (kk + 1) * WINB)
            if lo < hi:
                parts.append(wb[kk, :, :, lo - kk * WINB:hi - kk * WINB])
        return parts

    zeros = lambda w: [jnp.zeros(wb.shape[1:3] + (w,), wb.dtype)]
    segs = cols(0, 2048)
    for hh in range(4):
        segs += cols(2048 + 64 * hh, 2112 + 64 * hh) + zeros(64)
    for hh in range(4):
        segs += cols(2304 + 64 * hh, 2368 + 64 * hh) + zeros(64)
    segs += cols(2560, 3584) + cols(3600, 5136)
    segs += cols(5136, 5140) + zeros(GA_LANE - 4) + cols(3584, 3600) + zeros(LANE - GA_LANE - 16)
    return jnp.concatenate(segs, axis=-1)


def _win_to_blocks(g):
    mb = MISC * LANE
    parts = [g[:, 0:2048]]
    parts += [g[:, GQ * LANE + LANE * hh:GQ * LANE + LANE * hh + 64] for hh in range(4)]
    parts += [g[:, GK * LANE + LANE * hh:GK * LANE + LANE * hh + 64] for hh in range(4)]
    parts += [g[:, GV * LANE:FQ * LANE], g[:, mb + GA_LANE:mb + GA_LANE + 16], g[:, FQ * LANE:mb],
              g[:, mb + FF_LANE:mb + FF_LANE + 4]]
    full = jnp.concatenate(parts, axis=1)
    return full.reshape(D, NCHIP, WINB).transpose(1, 0, 2)


def _place():
    x, y, c = lax.axis_index("x"), lax.axis_index("y"), lax.axis_index("c")
    chips = [(1 - x, y), (x, 1 - y), (1 - x, 1 - y)]
    return x, y, c, chips


def _allgather_small(name, v):
    m_per, n = v.shape

    def body(x_ref, out_ref, send_sems, recv_sems, local_sem):
        x, y, c, chips = _place()
        me, sibling = (x, y, c), (x, y, 1 - c)

        def rows(px, py, pc):
            return out_ref.at[pl.ds((4 * px + 2 * py + pc) * m_per, m_per), :]

        def copy(k, block, to, src=None):
            return pltpu.make_async_remote_copy(
                src_ref=rows(*block) if src is None else src, dst_ref=rows(*block),
                send_sem=send_sems.at[k], recv_sem=recv_sems.at[k], device_id=to, device_id_type=MESH)

        mine = pltpu.make_async_copy(x_ref, rows(*me), local_sem)
        mine.start()
        first = [copy(0, me, sibling, src=x_ref)]
        first += [copy(1 + j, me, (*chip, c), src=x_ref) for j, chip in enumerate(chips)]
        for cp in first:
            cp.start()
        passed = [copy(4 + j, (*chip, c), sibling) for j, chip in enumerate(chips)]
        for j, chip in enumerate(chips):
            copy(1 + j, (*chip, c), me).wait_recv()
            passed[j].start()
        copy(0, sibling, me).wait_recv()
        for j, chip in enumerate(chips):
            copy(4 + j, (*chip, 1 - c), me).wait_recv()
        for cp in first + passed:
            cp.wait_send()
        mine.wait()

    return pl.pallas_call(
        body, name=name, out_shape=jax.ShapeDtypeStruct((8 * m_per, n), v.dtype),
        in_specs=[pl.BlockSpec(memory_space=pltpu.VMEM)], out_specs=pl.BlockSpec(memory_space=pltpu.VMEM),
        scratch_shapes=[pltpu.SemaphoreType.DMA((7,)), pltpu.SemaphoreType.DMA((7,)), pltpu.SemaphoreType.DMA],
    )(v)


def _hbm_specs(n):
    return [pl.BlockSpec(memory_space=pl.ANY)] * n


def _own_slot(shard, chip):
    return lax.dynamic_update_index_in_dim(lax.empty((NCHIP,) + shard.shape, shard.dtype), shard, chip, 0)


def _allgather_weights(name, bufs):
    nt = len(bufs)

    def body(*refs):
        outs = refs[nt:2 * nt]
        send_sems, recv_sems = refs[2 * nt:]
        x, y, c, chips = _place()
        me = 2 * x + y
        sibling = (x, y, 1 - c)

        def half(t, chip_idx, cc):
            r = outs[t].shape[2] // 2
            return outs[t].at[chip_idx, :, pl.ds(cc * r, r), :]

        def copy(t, k, chip_idx, cc, to):
            return pltpu.make_async_remote_copy(
                src_ref=half(t, chip_idx, cc), dst_ref=half(t, chip_idx, cc),
                send_sem=send_sems.at[t, k], recv_sem=recv_sems.at[t, k], device_id=to, device_id_type=MESH)

        sent = []
        for t in range(nt):
            for j, chip in enumerate(chips):
                cp = copy(t, j, me, c, (*chip, c))
                cp.start()
                sent.append(cp)
        for t in range(nt):
            for j, (px, py) in enumerate(chips):
                copy(t, j, 2 * px + py, c, (x, y, c)).wait_recv()
                cp = copy(t, 3 + j, 2 * px + py, c, sibling)
                cp.start()
                sent.append(cp)
        for t in range(nt):
            for j, (px, py) in enumerate(chips):
                copy(t, 3 + j, 2 * px + py, 1 - c, (x, y, c)).wait_recv()
        for cp in sent:
            cp.wait_send()

    return pl.pallas_call(
        body, name=name, out_shape=[jax.ShapeDtypeStruct(v.shape, v.dtype) for v in bufs],
        in_specs=_hbm_specs(nt), out_specs=_hbm_specs(nt), input_output_aliases={t: t for t in range(nt)},
        scratch_shapes=[pltpu.SemaphoreType.DMA((nt, 6)), pltpu.SemaphoreType.DMA((nt, 6))],
    )(*bufs)


def _rs_to_sibling(name, grads):
    nt = len(grads)

    def body(*refs):
        ins, outs = refs[:nt], refs[nt:2 * nt]
        send_sems, recv_sems = refs[2 * nt:]
        x, y, c, _ = _place()
        cps = []
        for t in range(nt):
            r = ins[t].shape[1] // 2
            cp = pltpu.make_async_remote_copy(
                src_ref=ins[t].at[:, pl.ds((1 - c) * r, r), :], dst_ref=outs[t],
                send_sem=send_sems.at[t], recv_sem=recv_sems.at[t], device_id=(x, y, 1 - c), device_id_type=MESH)
            cp.start()
            cps.append(cp)
        for cp in cps:
            cp.wait()

    return pl.pallas_call(
        body, name=name,
        out_shape=[jax.ShapeDtypeStruct((NCHIP, g.shape[1] // 2, g.shape[2]), g.dtype) for g in grads],
        in_specs=_hbm_specs(nt), out_specs=_hbm_specs(nt),
        scratch_shapes=[pltpu.SemaphoreType.DMA((nt,)), pltpu.SemaphoreType.DMA((nt,))],
    )(*grads)


def _rs_pair_sum(name, pos, g, other):
    r, cdim = other.shape[1], other.shape[2]
    tr = r // 4 if (r // 4) % 16 == 0 else r // 2
    nblk = r // tr

    def body(pos_ref, g_ref, o_ref, s_ref):
        s_ref[...] = (g_ref[...].astype(f32) + o_ref[...].astype(f32)).astype(bf16)

    blk = pl.BlockSpec((None, tr, cdim), lambda q, i, p: (q, i, 0))
    return pl.pallas_call(
        body, name=name, out_shape=jax.ShapeDtypeStruct(other.shape, bf16),
        grid_spec=pltpu.PrefetchScalarGridSpec(
            num_scalar_prefetch=1, grid=(NCHIP, nblk),
            in_specs=[pl.BlockSpec((None, tr, cdim), lambda q, i, p: (q, p[1] * nblk + i, 0)), blk], out_specs=blk),
        compiler_params=_cp(("parallel", "parallel")),
    )(pos, g, other)


def _rs_to_owner(name, sums):
    nt = len(sums)

    def body(*refs):
        ins, outs = refs[:nt], refs[nt:2 * nt]
        send_sems, recv_sems = refs[2 * nt:]
        x, y, c, chips = _place()
        me = 2 * x + y
        cps = []
        for t in range(nt):
            for j, (px, py) in enumerate(chips):
                cp = pltpu.make_async_remote_copy(
                    src_ref=ins[t].at[2 * px + py], dst_ref=outs[t].at[me],
                    send_sem=send_sems.at[t, j], recv_sem=recv_sems.at[t, j], device_id=(px, py, c),
                    device_id_type=MESH)
                cp.start()
                cps.append(cp)
        for t in range(nt):
            for j, (px, py) in enumerate(chips):
                pltpu.make_async_remote_copy(
                    src_ref=ins[t].at[me], dst_ref=outs[t].at[2 * px + py],
                    send_sem=send_sems.at[t, j], recv_sem=recv_sems.at[t, j], device_id=(px, py, c),
                    device_id_type=MESH).wait_recv()
        for cp in cps:
            cp.wait_send()

    return pl.pallas_call(
        body, name=name, out_shape=[jax.ShapeDtypeStruct(v.shape, v.dtype) for v in sums],
        in_specs=_hbm_specs(nt), out_specs=_hbm_specs(nt),
        scratch_shapes=[pltpu.SemaphoreType.DMA((nt, 3)), pltpu.SemaphoreType.DMA((nt, 3))],
    )(*sums)


def _rs_chip_sum(name, pos, sums, parts):
    r, cdim = parts.shape[1], parts.shape[2]
    tr = r // 4 if (r // 4) % 16 == 0 else r // 2
    nblk = r // tr

    def body(pos_ref, own_ref, a_ref, b_ref, c_ref, o_ref):
        o_ref[...] = ((own_ref[...].astype(f32) + a_ref[...].astype(f32)) + b_ref[...].astype(f32)) \
            + c_ref[...].astype(f32)

    def slot(k):
        return pl.BlockSpec((None, tr, cdim), lambda i, p: ((p[0] + k) % NCHIP, i, 0))

    return pl.pallas_call(
        body, name=name, out_shape=jax.ShapeDtypeStruct((2 * r, cdim), f32),
        grid_spec=pltpu.PrefetchScalarGridSpec(
            num_scalar_prefetch=1, grid=(nblk,), in_specs=[slot(0), slot(1), slot(2), slot(3)],
            out_specs=pl.BlockSpec((tr, cdim), lambda i, p: (p[1] * nblk + i, 0))),
        compiler_params=_cp(("parallel",)),
    )(pos, sums, parts, parts, parts)


def _rs_share_halves(name, bufs):
    nt = len(bufs)

    def body(*refs):
        outs = refs[nt:2 * nt]
        send_sems, recv_sems = refs[2 * nt:]
        x, y, c, _ = _place()
        cps = []
        for t in range(nt):
            r = outs[t].shape[0] // 2
            mine = outs[t].at[pl.ds(c * r, r), :]
            theirs = outs[t].at[pl.ds((1 - c) * r, r), :]
            cp = pltpu.make_async_remote_copy(
                src_ref=mine, dst_ref=mine, send_sem=send_sems.at[t], recv_sem=recv_sems.at[t],
                device_id=(x, y, 1 - c), device_id_type=MESH)
            cp.start()
            cps.append((cp, theirs))
        for t, (cp, theirs) in enumerate(cps):
            pltpu.make_async_remote_copy(
                src_ref=theirs, dst_ref=theirs, send_sem=send_sems.at[t], recv_sem=recv_sems.at[t],
                device_id=(x, y, 1 - c), device_id_type=MESH).wait_recv()
            cp.wait_send()

    return pl.pallas_call(
        body, name=name, out_shape=[jax.ShapeDtypeStruct(v.shape, v.dtype) for v in bufs],
        in_specs=_hbm_specs(nt), out_specs=_hbm_specs(nt), input_output_aliases={t: t for t in range(nt)},
        scratch_shapes=[pltpu.SemaphoreType.DMA((nt,)), pltpu.SemaphoreType.DMA((nt,))],
    )(*bufs)


BIG = ("w_in", "w_out", "w_gate", "w_up", "w_down")


def _reduce_scatter(l, big, pos):
    n = f"l{l}_rs_"
    grads = [big[k] for k in BIG]
    got = _rs_to_sibling(n + "to_sibling", grads)
    sums = [_rs_pair_sum(n + "pair_sum_" + k, pos, g, o) for k, g, o in zip(BIG, grads, got)]
    parts = _rs_to_owner(n + "to_owner", sums)
    halves = [_rs_chip_sum(n + "chip_sum_" + k, pos, s, v) for k, s, v in zip(BIG, sums, parts)]
    full = _rs_share_halves(n + "share", halves)
    return dict(zip(BIG, full))


def _adam_math(w, g, m, v):
    m = ADAM_B1 * m + (1.0 - ADAM_B1) * g
    v = ADAM_B2 * v + (1.0 - ADAM_B2) * (g * g)
    m_hat = m / (1.0 - ADAM_B1 ** ADAM_STEP)
    v_hat = v / (1.0 - ADAM_B2 ** ADAM_STEP)
    delta = -ADAM_LR * (m_hat / (jnp.sqrt(v_hat) + ADAM_EPS) + ADAM_WD * w)
    return delta, m, v


def _adam_big(name, g0, g1, w, m, v):
    _, r, cdim = w.shape
    tr = 128 if r % 128 == 0 else 64
    nb = r // tr

    def body(g0_ref, g1_ref, w_ref, m_ref, v_ref, go_ref, d_ref, mo_ref, vo_ref):
        l = pl.program_id(0)
        g = jnp.where(l == 0, g0_ref[...], g1_ref[...])
        delta, mn, vn = _adam_math(w_ref[...], g, m_ref[...], v_ref[...])
        go_ref[...] = g
        d_ref[...] = delta
        mo_ref[...] = mn
        vo_ref[...] = vn

    lay = pl.BlockSpec((None, tr, cdim), lambda l, i: (l, i, 0))
    return pl.pallas_call(
        body, name=name, grid=(2, nb),
        in_specs=[pl.BlockSpec((tr, cdim), lambda l, i: (i * (1 - l) + (nb - 1) * l, 0)),
                  pl.BlockSpec((tr, cdim), lambda l, i: (i * l, 0)), lay, lay, lay],
        out_specs=[lay] * 4, out_shape=[jax.ShapeDtypeStruct(w.shape, f32)] * 4,
        compiler_params=_cp(("arbitrary", "arbitrary")),
    )(g0, g1, w, m, v)


def _sum8(name, gathered):
    m_per = gathered.shape[0] // 8

    def body(g_ref, o_ref):
        tot = g_ref[pl.ds(0, m_per), :]
        for d in range(1, 8):
            tot = tot + g_ref[pl.ds(d * m_per, m_per), :]
        o_ref[...] = tot

    return pl.pallas_call(body, name=name, out_shape=jax.ShapeDtypeStruct((m_per, LANE), f32))(gathered)


def _adam_small(name, g, w, m, v):
    def body(g_ref, w_ref, m_ref, v_ref, d_ref, mo_ref, vo_ref):
        delta, mn, vn = _adam_math(w_ref[...], g_ref[...], m_ref[...], v_ref[...])
        d_ref[...] = delta
        mo_ref[...] = mn
        vo_ref[...] = vn

    return pl.pallas_call(body, name=name, out_shape=[jax.ShapeDtypeStruct(g.shape, f32)] * 3)(g, w, m, v)


def _pack(vals):
    rows, offs, at = [], [], 0
    for a in vals:
        a = a.reshape(-1)
        n = -(-a.shape[0] // (8 * LANE)) * 8
        rows.append(jnp.pad(a, (0, n * LANE - a.shape[0])).reshape(n, LANE))
        offs.append(at)
        at += n
    return jnp.concatenate(rows, axis=0), offs


def _unpack(packed, offs, shapes):
    out = []
    for o, shp in zip(offs, shapes):
        size = 1
        for d in shp:
            size *= d
        n = -(-size // LANE)
        out.append(packed[o:o + n].reshape(-1)[:size].reshape(shp))
    return out


SMALL = ("norm_mix_g", "conv_w", "pool_w", "pool_scale", "gla_w_decay", "gla_b_decay", "gla_out_g",
         "fox_q_g", "fox_k_g", "fox_b_f", "norm_ffn_g")
ALL = ("norm_mix_g", "w_in", "conv_w", "pool_w", "pool_scale", "gla_w_decay", "gla_b_decay", "gla_out_g",
       "fox_q_g", "fox_k_g", "fox_b_f", "w_out", "norm_ffn_g", "w_gate", "w_up", "w_down")


def kernel(x, norm_mix_g, w_in, conv_w, pool_w, pool_scale, gla_w_decay, gla_b_decay, gla_out_g, fox_q_g, fox_k_g, fox_b_f, w_out, norm_ffn_g, w_gate, w_up, w_down, loss_target, m_norm_mix_g, m_w_in, m_conv_w, m_pool_w, m_pool_scale, m_gla_w_decay, m_gla_b_decay, m_gla_out_g, m_fox_q_g, m_fox_k_g, m_fox_b_f, m_w_out, m_norm_ffn_g, m_w_gate, m_w_up, m_w_down, v_norm_mix_g, v_w_in, v_conv_w, v_pool_w, v_pool_scale, v_gla_w_decay, v_gla_b_decay, v_gla_out_g, v_fox_q_g, v_fox_k_g, v_fox_b_f, v_w_out, v_norm_ffn_g, v_w_gate, v_w_up, v_w_down):
    w = dict(norm_mix_g=norm_mix_g, w_in=w_in, conv_w=conv_w, pool_w=pool_w, pool_scale=pool_scale,
             gla_w_decay=gla_w_decay, gla_b_decay=gla_b_decay, gla_out_g=gla_out_g, fox_q_g=fox_q_g, fox_k_g=fox_k_g,
             fox_b_f=fox_b_f, w_out=w_out, norm_ffn_g=norm_ffn_g, w_gate=w_gate, w_up=w_up, w_down=w_down)
    m = dict(norm_mix_g=m_norm_mix_g, w_in=m_w_in, conv_w=m_conv_w, pool_w=m_pool_w, pool_scale=m_pool_scale,
             gla_w_decay=m_gla_w_decay, gla_b_decay=m_gla_b_decay, gla_out_g=m_gla_out_g, fox_q_g=m_fox_q_g,
             fox_k_g=m_fox_k_g, fox_b_f=m_fox_b_f, w_out=m_w_out, norm_ffn_g=m_norm_ffn_g, w_gate=m_w_gate,
             w_up=m_w_up, w_down=m_w_down)
    v = dict(norm_mix_g=v_norm_mix_g, w_in=v_w_in, conv_w=v_conv_w, pool_w=v_pool_w, pool_scale=v_pool_scale,
             gla_w_decay=v_gla_w_decay, gla_b_decay=v_gla_b_decay, gla_out_g=v_gla_out_g, fox_q_g=v_fox_q_g,
             fox_k_g=v_fox_k_g, fox_b_f=v_fox_b_f, w_out=v_w_out, norm_ffn_g=v_norm_ffn_g, w_gate=v_w_gate,
             w_up=v_w_up, w_down=v_w_down)
    chip = 2 * lax.axis_index("x") + lax.axis_index("y")

    mine, offs = _pack([conv_w, gla_w_decay])
    rows = mine.shape[0]
    every = _allgather_small("gather_small_params", mine).reshape(NCHIP, 2, rows, LANE)[:, 0]
    per_chip = [_unpack(every[kk], offs, [conv_w.shape, gla_w_decay.shape]) for kk in range(NCHIP)]
    conv_full = jnp.concatenate([pc[0] for pc in per_chip], axis=-1)[:, :, 0, :]
    wdec_full = jnp.concatenate([pc[1] for pc in per_chip], axis=-1)

    pos = jnp.stack([chip, lax.axis_index("c")]).astype(jnp.int32)
    gathered = _allgather_weights("gather_weights", [_own_slot(w[k].astype(bf16), chip) for k in BIG])
    wg = dict(zip(BIG, gathered))

    wdec = jnp.pad(wdec_full.reshape(2, 16, 4, 64), ((0, 0), (GA_LANE, LANE - GA_LANE - 16), (0, 0), (0, 64)))
    p = dict(
        g_mix=norm_mix_g[:, None, :], g_ffn=norm_ffn_g[:, None, :], w_in=_win_from_blocks(wg["w_in"]),
        w_out=wg["w_out"], w_gate=wg["w_gate"], w_up=wg["w_up"], w_down=wg["w_down"],
        conv_w=conv_full, pool_w=pool_w, pool_scale=pool_scale[:, None, :],
        wdec=wdec.reshape(2, LANE, G),
        bdec=jnp.pad(gla_b_decay.reshape(2, 4, 64), ((0, 0), (0, 0), (0, 64))).reshape(2, 1, G),
        gla_og=gla_out_g[:, None, :], fox_qg=fox_q_g[:, None, :], fox_kg=fox_k_g[:, None, :],
        fox_bf=jnp.pad(fox_b_f, ((0, 0), (FF_LANE, LANE - FF_LANE - 4)))[:, None, :])

    h0 = x[0]
    h1, sv0 = _layer_fwd(0, h0, p)
    h2, sv1 = _layer_fwd(1, h1, p)
    sq, dh, dhb = _loss("loss", h2, loss_target[0])
    loss = lax.psum(sq[0, 0] * (0.5 / D), ("x", "y", "c"))

    dh, dhb, big1, small1 = _layer_bwd(1, dh, dhb, p, sv1)
    red1 = _reduce_scatter(1, big1, pos)
    dh, dhb, big0, small0 = _layer_bwd(0, dh, dhb, p, sv0)
    red0 = _reduce_scatter(0, big0, pos)

    grads, deltas, new_m, new_v = {}, {}, {}, {}
    for k in BIG:
        grads[k], deltas[k], new_m[k], new_v[k] = _adam_big("adam_" + k, red0[k], red1[k], w[k], m[k], v[k])

    packed, goffs = _pack([jnp.stack([small0[k], small1[k]]) for k in SMALL])
    total = _sum8("sum_small_grads", _allgather_small("gather_small_grads", packed))
    full_shapes = [(2,) + small0[k].shape for k in SMALL]
    gsmall = dict(zip(SMALL, _unpack(total, goffs, full_shapes)))
    gsmall["conv_w"] = lax.dynamic_slice_in_dim(gsmall["conv_w"], chip * LANE, LANE, axis=2)[:, :, None, :]
    gsmall["gla_w_decay"] = lax.dynamic_slice_in_dim(gsmall["gla_w_decay"], chip * 64, 64, axis=2)
    gp, loffs = _pack([gsmall[k] for k in SMALL])
    wp, _ = _pack([w[k] for k in SMALL])
    mp, _ = _pack([m[k] for k in SMALL])
    vp, _ = _pack([v[k] for k in SMALL])
    dp, mnp, vnp = _adam_small("adam_small", gp, wp, mp, vp)
    shapes = [w[k].shape for k in SMALL]
    for k, a, b, c_, d_ in zip(SMALL, _unpack(gp, loffs, shapes), _unpack(dp, loffs, shapes),
                               _unpack(mnp, loffs, shapes), _unpack(vnp, loffs, shapes)):
        grads[k], deltas[k], new_m[k], new_v[k] = a, b, c_, d_

    return (loss, dh[None], *[grads[k] for k in ALL], *[deltas[k] for k in ALL],
            *[new_m[k] for k in ALL], *[new_v[k] for k in ALL])
```

```python
import functools

import jax
import jax.numpy as jnp
from jax import lax
from jax.experimental import pallas as pl
from jax.experimental.pallas import tpu as pltpu

f32 = jnp.float32
bf16 = jnp.bfloat16

D = 2048
G = 512
DFF = 5632
NCHIP = 4
FB = DFF // NCHIP
WIN = 5140
WINB = WIN // NCHIP
EPS = 1e-6
CHUNK = 64
LANE = 128

CB, CC, CH, PU, GQ, GK, GV, GG, FQ, FK, FV, MISC = 0, 4, 8, 12, 16, 20, 24, 28, 32, 36, 40, 44
ZC = 45 * LANE
FF_LANE = 0
GA_LANE = 8

ADAM_LR, ADAM_B1, ADAM_B2, ADAM_EPS, ADAM_WD, ADAM_STEP = 0.001, 0.9, 0.999, 1e-08, 0.01, 10

VMEM_LIMIT = 60 * 1024 * 1024
MESH = pl.DeviceIdType.MESH


def _cp(sem=None):
    return pltpu.CompilerParams(dimension_semantics=sem, vmem_limit_bytes=VMEM_LIMIT)


def _dot(a, b, dims=((1,), (0,))):
    return lax.dot_general(a.astype(bf16), b.astype(bf16), (dims, ((), ())), preferred_element_type=f32)


def _bdot(a, b, ca, cb):
    return lax.dot_general(a.astype(bf16), b.astype(bf16), (((ca,), (cb,)), ((0,), (0,))),
                           preferred_element_type=f32)


def _log_sigmoid(x):
    return jnp.minimum(x, 0.0) - jnp.log(1.0 + jnp.exp(-jnp.abs(x)))


@jax.custom_vjp
def _sigmoid(x):
    return 1.0 / (1.0 + jnp.exp(-x))


def _sigmoid_fwd(x):
    s = _sigmoid(x)
    return s, s


def _sigmoid_bwd(s, g):
    return (g * s * (1.0 - s),)


_sigmoid.defvjp(_sigmoid_fwd, _sigmoid_bwd)


def _rms(x, g):
    return x * lax.rsqrt(jnp.mean(x * x, axis=-1, keepdims=True) + EPS) * g


def _shift_impl(x, n, period, transpose):
    rows = x.shape[0]
    t = lax.broadcasted_iota(jnp.int32, x.shape, 0)
    if period is not None:
        t = t & (period - 1)
    keep = t >= n
    if not transpose:
        return jnp.where(keep, pltpu.roll(x, n, 0), 0.0)
    return pltpu.roll(jnp.where(keep, x, 0.0), rows - n, 0)


def _shift(x, n, period=None):
    @jax.custom_vjp
    def f(v):
        return _shift_impl(v, n, period, False)

    def fwd(v):
        return f(v), None

    def bwd(_, g):
        return (_shift_impl(g, n, period, True),)

    f.defvjp(fwd, bwd)
    return f(x)


def _cumsum_rows(x, length, period=None):
    n = 1
    while n < length:
        x = x + _shift(x, n, period)
        n *= 2
    return x


def _convpool_fn(cb, cc, ch, pu, w0, w1, w2, pw, ps, j):
    u = cc * ch
    y = w2 * u + w1 * _shift(u, 1) + w0 * _shift(u, 2)
    ya = cb * y
    s2 = pu + _shift(pu, 1)
    s4 = s2 + _shift(s2, 2)
    s8 = s4 + _shift(s4, 4)
    s16 = s8 + _shift(s8, 8)
    wsum = jnp.where(j == 0, s2, jnp.where(j == 1, s4, jnp.where(j == 2, s8, s16)))
    width = (2 << j).astype(f32)
    t = lax.broadcasted_iota(jnp.int32, pu.shape, 0).astype(f32)
    count = jnp.minimum(t + 1.0, width)
    d = wsum / count - pu
    yb = _dot(d, pw) * ps
    return ya, yb


def _foxprep_fn(misc, bf):
    lf = _log_sigmoid(misc + bf)
    fc = _cumsum_rows(lf, lf.shape[0])
    return fc, jnp.transpose(fc)


def _fox_fn(q, k, v, fcol, frow8, qg, kg, h, i):
    tq, s = q.shape[0], k.shape[0]
    qn = _rms(q, qg)
    kn = _rms(k, kg)
    lg = _dot(qn, kn, ((1,), (1,))) * (LANE ** -0.5)
    lane = lax.broadcasted_iota(jnp.int32, fcol.shape, 1)
    fq = jnp.sum(jnp.where(lane == h, fcol, 0.0), axis=1, keepdims=True)
    row = lax.broadcasted_iota(jnp.int32, frow8.shape, 0)
    fk = jnp.sum(jnp.where(row == h, frow8, 0.0), axis=0, keepdims=True)
    lg = lg + fq - fk
    qpos = i * tq + lax.broadcasted_iota(jnp.int32, (tq, s), 0)
    kpos = lax.broadcasted_iota(jnp.int32, (tq, s), 1)
    lg = jnp.where(kpos <= qpos, lg, -jnp.inf)
    m = lax.stop_gradient(jnp.max(lg, axis=1, keepdims=True))
    e = jnp.exp(lg - m)
    p = e / jnp.sum(e, axis=1, keepdims=True)
    return _dot(p, v)


def _gla1_fn(q, k, v, misc, wd, bd):
    ts = q.shape[0]
    nb = ts // CHUNK
    x = _dot(misc, wd) + bd
    la = _log_sigmoid(x) * (1.0 / 16.0)
    cc = _cumsum_rows(la, CHUNK, CHUNK)
    la3 = la.reshape(nb, CHUNK, LANE)
    last3 = jnp.sum(la3, axis=1, keepdims=True)
    last2 = jnp.sum(la3, axis=1)
    cc3 = cc.reshape(nb, CHUNK, LANE)
    q3 = (q * 0.125).reshape(nb, CHUNK, LANE)
    k3 = k.reshape(nb, CHUNK, LANE)
    v3 = v.reshape(nb, CHUNK, LANE)
    ep = jnp.exp(cc3)
    en = jnp.exp(-cc3)
    qe = q3 * ep
    a1 = _bdot(qe, k3 * en, 2, 2)
    a2 = _bdot(q3 * en, k3 * ep, 2, 2)
    ti = lax.broadcasted_iota(jnp.int32, a1.shape, 1)
    si = lax.broadcasted_iota(jnp.int32, a1.shape, 2)
    sc = jnp.where(si <= ti, a1, a2)
    oi = _bdot(sc, v3, 2, 1)
    kd = k3 * jnp.exp(last3 - cc3)
    el = jnp.exp(last2)
    return qe.reshape(ts, LANE), kd.reshape(ts, LANE), el, oi.reshape(ts, LANE)


def _gla3_fn(o, gg, og):
    return _rms(o, og) * (gg * _sigmoid(gg))


def _ffn_fn(gate, up):
    return gate * _sigmoid(gate) * up


def _mm(name, a, b, a_spec, b_spec, out_shape, out_spec, grid, dims, nk, res=None, res_spec=None):
    has_res = res is not None
    nax = len(grid)

    def body(*refs):
        a_ref, b_ref = refs[0], refs[1]
        res_ref = refs[2] if has_res else None
        out_ref = refs[2 + has_res]
        part = _dot(a_ref[...], b_ref[...], dims)
        if nk == 1:
            if has_res:
                part = part + res_ref[...]
            out_ref[...] = part.astype(out_ref.dtype)
            return
        acc_ref = refs[3 + has_res]
        k = pl.program_id(nax - 1)

        @pl.when(k == 0)
        def _():
            acc_ref[...] = part

        @pl.when(k > 0)
        def _():
            acc_ref[...] += part

        @pl.when(k == nk - 1)
        def _():
            tot = acc_ref[...]
            if has_res:
                tot = tot + res_ref[...]
            out_ref[...] = tot.astype(out_ref.dtype)

    ops = [a, b] + ([res] if has_res else [])
    specs = [a_spec, b_spec] + ([res_spec] if has_res else [])
    blk = tuple(d for d in out_spec.block_shape if d is not None)
    scratch = [pltpu.VMEM(blk, f32)] if nk > 1 else []
    return pl.pallas_call(
        body, name=name, grid=grid, in_specs=specs, out_specs=out_spec, out_shape=out_shape,
        scratch_shapes=scratch,
        compiler_params=_cp(("parallel",) * (nax - 1) + ("arbitrary",)),
    )(*ops)


def _tm(s):
    return min(s, 512)


def _rmsnorm_fwd(name, x, g, l):
    s = x.shape[0]
    tm = min(s, 256)

    def body(x_ref, g_ref, u_ref):
        u_ref[...] = _rms(x_ref[...], g_ref[...]).astype(bf16)

    return pl.pallas_call(
        body, name=name, grid=(s // tm,),
        in_specs=[pl.BlockSpec((tm, D), lambda i: (i, 0)), pl.BlockSpec((None, 1, D), lambda i: (l, 0, 0))],
        out_specs=pl.BlockSpec((tm, D), lambda i: (i, 0)),
        out_shape=jax.ShapeDtypeStruct((s, D), bf16), compiler_params=_cp(("parallel",)),
    )(x, g)


def _rmsnorm_bwd(name, x, g, du, dres, l):
    s = x.shape[0]
    tm = min(s, 256)

    def body(x_ref, g_ref, du_ref, dres_ref, dx_ref, dxb_ref, dg_ref):
        _, vjp = jax.vjp(_rms, x_ref[...], g_ref[...])
        dx, dg = vjp(du_ref[...])
        tot = dx + dres_ref[...]
        dx_ref[...] = tot
        dxb_ref[...] = tot.astype(bf16)

        @pl.when(pl.program_id(0) == 0)
        def _():
            dg_ref[...] = dg

        @pl.when(pl.program_id(0) > 0)
        def _():
            dg_ref[...] += dg

    row = pl.BlockSpec((tm, D), lambda i: (i, 0))
    return pl.pallas_call(
        body, name=name, grid=(s // tm,),
        in_specs=[row, pl.BlockSpec((None, 1, D), lambda i: (l, 0, 0)), row, row],
        out_specs=[row, row, pl.BlockSpec((1, D), lambda i: (0, 0))],
        out_shape=[jax.ShapeDtypeStruct((s, D), f32), jax.ShapeDtypeStruct((s, D), bf16),
                   jax.ShapeDtypeStruct((1, D), f32)],
        compiler_params=_cp(("arbitrary",)),
    )(x, g, du, dres)


def _ffn_act(name, gate, up):
    s = gate.shape[0]
    tm = min(s, 256)
    spec = pl.BlockSpec((tm, FB), lambda i, j: (i, j))

    def body(g_ref, u_ref, a_ref):
        a_ref[...] = _ffn_fn(g_ref[...], u_ref[...]).astype(bf16)

    return pl.pallas_call(
        body, name=name, grid=(s // tm, NCHIP), in_specs=[spec, spec], out_specs=spec,
        out_shape=jax.ShapeDtypeStruct((s, DFF), bf16), compiler_params=_cp(("parallel", "parallel")),
    )(gate, up)


def _ffn_act_bwd(name, gate, up, dact):
    s = gate.shape[0]
    tm = min(s, 256)
    spec = pl.BlockSpec((tm, FB), lambda i, j: (i, j))

    def body(g_ref, u_ref, d_ref, dg_ref, du_ref):
        _, vjp = jax.vjp(_ffn_fn, g_ref[...], u_ref[...])
        dg, du = vjp(d_ref[...])
        dg_ref[...] = dg.astype(bf16)
        du_ref[...] = du.astype(bf16)

    return pl.pallas_call(
        body, name=name, grid=(s // tm, NCHIP), in_specs=[spec, spec, spec], out_specs=[spec, spec],
        out_shape=[jax.ShapeDtypeStruct((s, DFF), bf16)] * 2, compiler_params=_cp(("parallel", "parallel")),
    )(gate, up, dact)


def _loss(name, y, t):
    s = y.shape[0]
    tm = min(s, 256)
    row = pl.BlockSpec((tm, D), lambda i: (i, 0))

    def body(y_ref, t_ref, l_ref, d_ref, db_ref):
        e = y_ref[...] - t_ref[...]
        d = e * (1.0 / D)
        d_ref[...] = d
        db_ref[...] = d.astype(bf16)
        part = jnp.zeros((8, LANE), f32) + jnp.sum(e * e)

        @pl.when(pl.program_id(0) == 0)
        def _():
            l_ref[...] = part

        @pl.when(pl.program_id(0) > 0)
        def _():
            l_ref[...] += part

    return pl.pallas_call(
        body, name=name, grid=(s // tm,), in_specs=[row, row],
        out_specs=[pl.BlockSpec((8, LANE), lambda i: (0, 0)), row, row],
        out_shape=[jax.ShapeDtypeStruct((8, LANE), f32), jax.ShapeDtypeStruct((s, D), f32),
                   jax.ShapeDtypeStruct((s, D), bf16)],
        compiler_params=_cp(("arbitrary",)),
    )(y, t)


def _zspec(s, blk):
    return pl.BlockSpec((s, LANE), lambda j: (0, blk + j))


def _convpool_specs(s, l):
    return [_zspec(s, CB), _zspec(s, CC), _zspec(s, CH), _zspec(s, PU),
            pl.BlockSpec((None, 3, LANE), lambda j: (l, 0, j)),
            pl.BlockSpec((None, None, LANE, LANE), lambda j: (l, j, 0, 0)),
            pl.BlockSpec((None, 1, LANE), lambda j: (l, 0, j))]


def _convpool_fwd(name, z, conv_w, pool_w, pool_scale, l):
    s = z.shape[0]

    def body(cb, cc, ch, pu, cw, pw, ps, ya_ref, yb_ref):
        ya, yb = _convpool_fn(cb[...], cc[...], ch[...], pu[...], cw[0:1, :], cw[1:2, :], cw[2:3, :], pw[...], ps[...],
                              pl.program_id(0))
        ya_ref[...] = ya.astype(bf16)
        yb_ref[...] = yb.astype(bf16)

    col = pl.BlockSpec((s, LANE), lambda j: (0, j))
    return pl.pallas_call(
        body, name=name, grid=(4,), in_specs=_convpool_specs(s, l), out_specs=[col, col],
        out_shape=[jax.ShapeDtypeStruct((s, G), bf16)] * 2, compiler_params=_cp(("parallel",)),
    )(z, z, z, z, conv_w, pool_w, pool_scale)


def _convpool_bwd(name, z, conv_w, pool_w, pool_scale, dy, l):
    s = z.shape[0]

    def body(cb, cc, ch, pu, cw, pw, ps, dya, dyb, dcb, dcc, dch, dpu, dcw, dpw, dps):
        j = pl.program_id(0)
        fn = functools.partial(_convpool_fn, j=j)
        _, vjp = jax.vjp(fn, cb[...], cc[...], ch[...], pu[...], cw[0:1, :], cw[1:2, :], cw[2:3, :], pw[...], ps[...])
        g = vjp((dya[...], dyb[...]))
        dcb[...] = g[0].astype(bf16)
        dcc[...] = g[1].astype(bf16)
        dch[...] = g[2].astype(bf16)
        dpu[...] = g[3].astype(bf16)
        dcw[0:1, :] = g[4]
        dcw[1:2, :] = g[5]
        dcw[2:3, :] = g[6]
        dpw[...] = g[7]
        dps[...] = g[8]

    col = pl.BlockSpec((s, LANE), lambda j: (0, j))
    specs = _convpool_specs(s, l) + [pl.BlockSpec((s, LANE), lambda j: (0, j)),
                                     pl.BlockSpec((s, LANE), lambda j: (0, 4 + j))]
    return pl.pallas_call(
        body, name=name, grid=(4,), in_specs=specs,
        out_specs=[col, col, col, col, pl.BlockSpec((3, LANE), lambda j: (0, j)),
                   pl.BlockSpec((None, LANE, LANE), lambda j: (j, 0, 0)), pl.BlockSpec((1, LANE), lambda j: (0, j))],
        out_shape=[jax.ShapeDtypeStruct((s, G), bf16)] * 4 + [
            jax.ShapeDtypeStruct((3, G), f32), jax.ShapeDtypeStruct((4, LANE, LANE), f32),
            jax.ShapeDtypeStruct((1, G), f32)],
        compiler_params=_cp(("parallel",)),
    )(z, z, z, z, conv_w, pool_w, pool_scale, dy, dy)


def _foxprep_fwd(name, z, bf, l):
    s = z.shape[0]

    def body(m_ref, b_ref, fc_ref, fr_ref):
        fc, fr = _foxprep_fn(m_ref[...], b_ref[...])
        fc_ref[...] = fc
        fr_ref[...] = fr

    return pl.pallas_call(
        body, name=name, grid=(1,),
        in_specs=[pl.BlockSpec((s, LANE), lambda i: (0, MISC)), pl.BlockSpec((None, 1, LANE), lambda i: (l, 0, 0))],
        out_specs=[pl.BlockSpec((s, LANE), lambda i: (0, 0)), pl.BlockSpec((LANE, s), lambda i: (0, 0))],
        out_shape=[jax.ShapeDtypeStruct((s, LANE), f32), jax.ShapeDtypeStruct((LANE, s), f32)],
        compiler_params=_cp(("arbitrary",)),
    )(z, bf)


def _foxprep_bwd(name, z, bf, dfc4, dfr4, dmisc4, l):
    s = z.shape[0]

    def body(m_ref, b_ref, dfc_ref, dfr_ref, dm4_ref, dm_ref, db_ref):
        _, vjp = jax.vjp(_foxprep_fn, m_ref[...], b_ref[...])
        dfc = dfc_ref[0] + dfc_ref[1] + dfc_ref[2] + dfc_ref[3]
        dfr = dfr_ref[0] + dfr_ref[1] + dfr_ref[2] + dfr_ref[3]
        dfr = jnp.concatenate([dfr, jnp.zeros((LANE - 8, s), f32)], axis=0)
        dm, db = vjp((dfc, dfr))
        dm = dm + (dm4_ref[0] + dm4_ref[1] + dm4_ref[2] + dm4_ref[3])
        dm_ref[...] = dm.astype(bf16)
        db_ref[...] = db

    whole = lambda shape: pl.BlockSpec(shape, lambda i: (0,) * len(shape))
    return pl.pallas_call(
        body, name=name, grid=(1,),
        in_specs=[pl.BlockSpec((s, LANE), lambda i: (0, MISC)), pl.BlockSpec((None, 1, LANE), lambda i: (l, 0, 0)),
                  whole((4, s, LANE)), whole((4, 8, s)), whole((4, s, LANE))],
        out_specs=[whole((s, LANE)), whole((1, LANE))],
        out_shape=[jax.ShapeDtypeStruct((s, LANE), bf16), jax.ShapeDtypeStruct((1, LANE), f32)],
        compiler_params=_cp(("arbitrary",)),
    )(z, bf, dfc4, dfr4, dmisc4)


def _fox_specs(s, tq, l):
    return [pl.BlockSpec((tq, LANE), lambda h, i: (i, FQ + h)),
            pl.BlockSpec((s, LANE), lambda h, i: (0, FK + h)),
            pl.BlockSpec((s, LANE), lambda h, i: (0, FV + h)),
            pl.BlockSpec((tq, LANE), lambda h, i: (i, 0)),
            pl.BlockSpec((8, s), lambda h, i: (0, 0)),
            pl.BlockSpec((None, 1, LANE), lambda h, i: (l, 0, 0)),
            pl.BlockSpec((None, 1, LANE), lambda h, i: (l, 0, 0))]


def _fox_fwd(name, z, fc, fr, qg, kg, l):
    s = z.shape[0]
    tq = min(s, 256)

    def body(q, k, v, fc_ref, fr_ref, qg_ref, kg_ref, y_ref):
        y = _fox_fn(q[...], k[...], v[...], fc_ref[...], fr_ref[...], qg_ref[...], kg_ref[...],
                    pl.program_id(0), pl.program_id(1))
        y_ref[...] = y.astype(bf16)

    return pl.pallas_call(
        body, name=name, grid=(4, s // tq), in_specs=_fox_specs(s, tq, l),
        out_specs=pl.BlockSpec((tq, LANE), lambda h, i: (i, h)),
        out_shape=jax.ShapeDtypeStruct((s, G), bf16), compiler_params=_cp(("parallel", "parallel")),
    )(z, z, z, fc, fr, qg, kg)


def _fox_bwd(name, z, fc, fr, qg, kg, dy, l):
    s = z.shape[0]
    tq = min(s, 256)

    def body(q, k, v, fc_ref, fr_ref, qg_ref, kg_ref, dy_ref, dq, dk, dv, dfc, dfr, dqg, dkg):
        h, i = pl.program_id(0), pl.program_id(1)
        fn = functools.partial(_fox_fn, h=h, i=i)
        _, vjp = jax.vjp(fn, q[...], k[...], v[...], fc_ref[...], fr_ref[...], qg_ref[...], kg_ref[...])
        g = vjp(dy_ref[...])
        dq[...] = g[0].astype(bf16)
        dfc[...] = g[3]

        @pl.when(i == 0)
        def _():
            dk[...] = g[1]
            dv[...] = g[2]
            dfr[...] = g[4]
            dqg[...] = g[5]
            dkg[...] = g[6]

        @pl.when(i > 0)
        def _():
            dk[...] += g[1]
            dv[...] += g[2]
            dfr[...] += g[4]
            dqg[...] += g[5]
            dkg[...] += g[6]

    specs = _fox_specs(s, tq, l) + [pl.BlockSpec((tq, LANE), lambda h, i: (i, 12 + h))]
    head = pl.BlockSpec((s, LANE), lambda h, i: (0, h))
    gain = pl.BlockSpec((None, 1, LANE), lambda h, i: (h, 0, 0))
    return pl.pallas_call(
        body, name=name, grid=(4, s // tq), in_specs=specs,
        out_specs=[pl.BlockSpec((tq, LANE), lambda h, i: (i, h)), head, head,
                   pl.BlockSpec((None, tq, LANE), lambda h, i: (h, i, 0)),
                   pl.BlockSpec((None, 8, s), lambda h, i: (h, 0, 0)), gain, gain],
        out_shape=[jax.ShapeDtypeStruct((s, G), bf16), jax.ShapeDtypeStruct((s, G), f32),
                   jax.ShapeDtypeStruct((s, G), f32), jax.ShapeDtypeStruct((4, s, LANE), f32),
                   jax.ShapeDtypeStruct((4, 8, s), f32), jax.ShapeDtypeStruct((4, 1, LANE), f32),
                   jax.ShapeDtypeStruct((4, 1, LANE), f32)],
        compiler_params=_cp(("parallel", "arbitrary")),
    )(z, z, z, fc, fr, qg, kg, dy)


def _gla_ts(s):
    return min(s, 512)


def _gla1_specs(s, ts, l):
    return [pl.BlockSpec((ts, LANE), lambda h, i: (i, GQ + h)),
            pl.BlockSpec((ts, LANE), lambda h, i: (i, GK + h)),
            pl.BlockSpec((ts, LANE), lambda h, i: (i, GV + h)),
            pl.BlockSpec((ts, LANE), lambda h, i: (i, MISC)),
            pl.BlockSpec((None, LANE, LANE), lambda h, i: (l, 0, h)),
            pl.BlockSpec((None, 1, LANE), lambda h, i: (l, 0, h))]


def _gla1_fwd(name, z, wd, bd, l):
    s = z.shape[0]
    ts = _gla_ts(s)
    nb = ts // CHUNK

    def body(q, k, v, m, wd_ref, bd_ref, qe_ref, kd_ref, el_ref, oi_ref):
        qe, kd, el, oi = _gla1_fn(q[...], k[...], v[...], m[...], wd_ref[...], bd_ref[...])
        qe_ref[...] = qe.astype(bf16)
        kd_ref[...] = kd.astype(bf16)
        el_ref[...] = el
        oi_ref[...] = oi

    blk = pl.BlockSpec((ts, LANE), lambda h, i: (i, h))
    return pl.pallas_call(
        body, name=name, grid=(4, s // ts), in_specs=_gla1_specs(s, ts, l),
        out_specs=[blk, blk, pl.BlockSpec((nb, LANE), lambda h, i: (i, h)), blk],
        out_shape=[jax.ShapeDtypeStruct((s, G), bf16), jax.ShapeDtypeStruct((s, G), bf16),
                   jax.ShapeDtypeStruct((s // CHUNK, G), f32), jax.ShapeDtypeStruct((s, G), f32)],
        compiler_params=_cp(("parallel", "parallel")),
    )(z, z, z, z, wd, bd)


def _gla1_bwd(name, z, wd, bd, dqe, dkd, del_, do, dvi, l):
    s = z.shape[0]
    ts = _gla_ts(s)
    nb = ts // CHUNK

    def body(q, k, v, m, wd_ref, bd_ref, dqe_ref, dkd_ref, del_ref, do_ref, dvi_ref, dq, dk, dv, dm, dwd, dbd):
        i = pl.program_id(1)
        _, vjp = jax.vjp(_gla1_fn, q[...], k[...], v[...], m[...], wd_ref[...], bd_ref[...])
        g = vjp((dqe_ref[...], dkd_ref[...], del_ref[...], do_ref[...]))
        dq[...] = g[0].astype(bf16)
        dk[...] = g[1].astype(bf16)
        dv[...] = (g[2] + dvi_ref[...]).astype(bf16)
        dm[...] = g[3]

        @pl.when(i == 0)
        def _():
            dwd[...] = g[4]
            dbd[...] = g[5]

        @pl.when(i > 0)
        def _():
            dwd[...] += g[4]
            dbd[...] += g[5]

    blk = pl.BlockSpec((ts, LANE), lambda h, i: (i, h))
    specs = _gla1_specs(s, ts, l) + [blk, blk, pl.BlockSpec((nb, LANE), lambda h, i: (i, h)), blk, blk]
    return pl.pallas_call(
        body, name=name, grid=(4, s // ts), in_specs=specs,
        out_specs=[blk, blk, blk, pl.BlockSpec((None, ts, LANE), lambda h, i: (h, i, 0)),
                   pl.BlockSpec((None, LANE, LANE), lambda h, i: (h, 0, 0)),
                   pl.BlockSpec((None, 1, LANE), lambda h, i: (h, 0, 0))],
        out_shape=[jax.ShapeDtypeStruct((s, G), bf16)] * 3 + [
            jax.ShapeDtypeStruct((4, s, LANE), f32), jax.ShapeDtypeStruct((4, LANE, LANE), f32),
            jax.ShapeDtypeStruct((4, 1, LANE), f32)],
        compiler_params=_cp(("parallel", "arbitrary")),
    )(z, z, z, z, wd, bd, dqe, dkd, del_, do, dvi)


def _gla2_fwd(name, z, qe, kd, el, oi):
    s = z.shape[0]
    n = s // CHUNK

    def body(v_ref, qe_ref, kd_ref, el_ref, oi_ref, o_ref, st_ref, cur):
        cur[...] = jnp.zeros_like(cur)

        def step(c, carry):
            rows = pl.ds(pl.multiple_of(c * CHUNK, CHUNK), CHUNK)
            st = cur[...]
            st_ref[c] = st
            o_ref[rows, :] = oi_ref[rows, :] + _dot(qe_ref[rows, :], st, ((1,), (1,)))
            cur[...] = st * el_ref[pl.ds(c, 1), :] + _dot(v_ref[rows, :], kd_ref[rows, :], ((0,), (0,)))
            return carry

        lax.fori_loop(0, n, step, 0)

    head = pl.BlockSpec((s, LANE), lambda h: (0, h))
    return pl.pallas_call(
        body, name=name, grid=(4,),
        in_specs=[pl.BlockSpec((s, LANE), lambda h: (0, GV + h)), head, head,
                  pl.BlockSpec((n, LANE), lambda h: (0, h)), head],
        out_specs=[head, pl.BlockSpec((None, n, LANE, LANE), lambda h: (h, 0, 0, 0))],
        out_shape=[jax.ShapeDtypeStruct((s, G), f32), jax.ShapeDtypeStruct((4, n, LANE, LANE), f32)],
        scratch_shapes=[pltpu.VMEM((LANE, LANE), f32)],
        compiler_params=_cp(("parallel",)),
    )(z, qe, kd, el, oi)


def _gla2_bwd(name, z, qe, kd, el, st, do):
    s = z.shape[0]
    n = s // CHUNK

    def body(v_ref, qe_ref, kd_ref, el_ref, st_ref, do_ref, dqe_ref, dkd_ref, dv_ref, del_ref, dcur):
        dcur[...] = jnp.zeros_like(dcur)

        def step(t, carry):
            c = n - 1 - t
            rows = pl.ds(pl.multiple_of(c * CHUNK, CHUNK), CHUNK)
            dn = dcur[...]
            stc = st_ref[c]
            doc = do_ref[rows, :]
            dqe_ref[rows, :] = _dot(doc, stc)
            dv_ref[rows, :] = _dot(kd_ref[rows, :], dn, ((1,), (1,)))
            dkd_ref[rows, :] = _dot(v_ref[rows, :], dn)
            del_ref[pl.ds(c, 1), :] = jnp.sum(stc * dn, axis=0, keepdims=True)
            dcur[...] = dn * el_ref[pl.ds(c, 1), :] + _dot(doc, qe_ref[rows, :], ((0,), (0,)))
            return carry

        lax.fori_loop(0, n, step, 0)

    head = pl.BlockSpec((s, LANE), lambda h: (0, h))
    chunk = pl.BlockSpec((n, LANE), lambda h: (0, h))
    return pl.pallas_call(
        body, name=name, grid=(4,),
        in_specs=[pl.BlockSpec((s, LANE), lambda h: (0, GV + h)), head, head, chunk,
                  pl.BlockSpec((None, n, LANE, LANE), lambda h: (h, 0, 0, 0)), head],
        out_specs=[head, head, head, chunk],
        out_shape=[jax.ShapeDtypeStruct((s, G), f32)] * 3 + [jax.ShapeDtypeStruct((n, G), f32)],
        scratch_shapes=[pltpu.VMEM((LANE, LANE), f32)],
        compiler_params=_cp(("parallel",)),
    )(z, qe, kd, el, st, do)


def _gla3_specs(ts, l):
    return [pl.BlockSpec((ts, LANE), lambda h, i: (i, h)),
            pl.BlockSpec((ts, LANE), lambda h, i: (i, GG + h)),
            pl.BlockSpec((None, 1, LANE), lambda h, i: (l, 0, 0))]


def _gla3_fwd(name, o, z, og, l):
    s = z.shape[0]
    ts = _gla_ts(s)

    def body(o_ref, g_ref, og_ref, y_ref):
        y_ref[...] = _gla3_fn(o_ref[...], g_ref[...], og_ref[...]).astype(bf16)

    return pl.pallas_call(
        body, name=name, grid=(4, s // ts), in_specs=_gla3_specs(ts, l),
        out_specs=pl.BlockSpec((ts, LANE), lambda h, i: (i, h)),
        out_shape=jax.ShapeDtypeStruct((s, G), bf16), compiler_params=_cp(("parallel", "parallel")),
    )(o, z, og)


def _gla3_bwd(name, o, z, og, dy, l):
    s = z.shape[0]
    ts = _gla_ts(s)

    def body(o_ref, g_ref, og_ref, dy_ref, do_ref, dg_ref, dog_ref):
        i = pl.program_id(1)
        _, vjp = jax.vjp(_gla3_fn, o_ref[...], g_ref[...], og_ref[...])
        g = vjp(dy_ref[...])
        do_ref[...] = g[0]
        dg_ref[...] = g[1].astype(bf16)

        @pl.when(i == 0)
        def _():
            dog_ref[...] = g[2]

        @pl.when(i > 0)
        def _():
            dog_ref[...] += g[2]

    blk = pl.BlockSpec((ts, LANE), lambda h, i: (i, h))
    return pl.pallas_call(
        body, name=name, grid=(4, s // ts),
        in_specs=_gla3_specs(ts, l) + [pl.BlockSpec((ts, LANE), lambda h, i: (i, 8 + h))],
        out_specs=[blk, blk, pl.BlockSpec((None, 1, LANE), lambda h, i: (h, 0, 0))],
        out_shape=[jax.ShapeDtypeStruct((s, G), f32), jax.ShapeDtypeStruct((s, G), bf16),
                   jax.ShapeDtypeStruct((4, 1, LANE), f32)],
        compiler_params=_cp(("parallel", "arbitrary")),
    )(o, z, og, dy)


def _layer_fwd(l, h, p, weights):
    s = h.shape[0]
    tm = _tm(s)
    n = f"l{l}_"
    w_in = weights(l, "w_in", h)
    u = _rmsnorm_fwd(n + "norm_mix", h, p["g_mix"], l)
    z = _mm(n + "mm_in", u, w_in,
            pl.BlockSpec((tm, D), lambda j, i, k: (i, 0)), pl.BlockSpec((D, 1152), lambda j, i, k: (0, j)),
            jax.ShapeDtypeStruct((s, ZC), f32), pl.BlockSpec((tm, 1152), lambda j, i, k: (i, j)),
            (ZC // 1152, s // tm, 1), ((1,), (0,)), 1)
    w_out = weights(l, "w_out", z)
    ya, yb = _convpool_fwd(n + "convpool", z, p["conv_w"], p["pool_w"], p["pool_scale"], l)
    qe, kd, el, oi = _gla1_fwd(n + "gla_chunk", z, p["wdec"], p["bdec"], l)
    o, st = _gla2_fwd(n + "gla_scan", z, qe, kd, el, oi)
    yc = _gla3_fwd(n + "gla_out", o, z, p["gla_og"], l)
    fc, fr = _foxprep_fwd(n + "fox_prep", z, p["fox_bf"], l)
    yd = _fox_fwd(n + "fox_attn", z, fc, fr, p["fox_qg"], p["fox_kg"], l)
    y = jnp.concatenate([ya, yb, yc, yd], axis=1)
    w_gate, w_up, w_down = weights(l, "ffn", y)
    h1 = _mm(n + "mm_out", y, w_out,
             pl.BlockSpec((tm, G), lambda j, i, k: (i, k)),
             pl.BlockSpec((None, G, 1024), lambda j, i, k: (k, 0, j)),
             jax.ShapeDtypeStruct((s, D), f32), pl.BlockSpec((tm, 1024), lambda j, i, k: (i, j)),
             (2, s // tm, NCHIP), ((1,), (0,)), NCHIP, res=h, res_spec=pl.BlockSpec((tm, 1024), lambda j, i, k: (i, j)))
    u2 = _rmsnorm_fwd(n + "norm_ffn", h1, p["g_ffn"], l)
    ffn_in = (pl.BlockSpec((tm, D), lambda j, i, k: (i, 0)),
              pl.BlockSpec((None, D, FB), lambda j, i, k: (j, 0, 0)))
    ffn_out = (jax.ShapeDtypeStruct((s, DFF), f32), pl.BlockSpec((tm, FB), lambda j, i, k: (i, j)))
    gate = _mm(n + "mm_gate", u2, w_gate, *ffn_in, *ffn_out, (NCHIP, s // tm, 1), ((1,), (0,)), 1)
    up = _mm(n + "mm_up", u2, w_up, *ffn_in, *ffn_out, (NCHIP, s // tm, 1), ((1,), (0,)), 1)
    act = _ffn_act(n + "ffn_act", gate, up)
    h2 = _mm(n + "mm_down", act, w_down,
             pl.BlockSpec((tm, FB), lambda j, i, k: (i, k)),
             pl.BlockSpec((None, FB, 1024), lambda j, i, k: (k, 0, j)),
             jax.ShapeDtypeStruct((s, D), f32), pl.BlockSpec((tm, 1024), lambda j, i, k: (i, j)),
             (2, s // tm, NCHIP), ((1,), (0,)), NCHIP, res=h1, res_spec=pl.BlockSpec((tm, 1024), lambda j, i, k: (i, j)))
    saved = dict(h=h, u=u, z=z, qe=qe, kd=kd, el=el, st=st, o=o, fc=fc, fr=fr, y=y, h1=h1, u2=u2,
                 gate=gate, up=up, act=act, w_in=w_in, w_out=w_out, w_gate=w_gate, w_up=w_up, w_down=w_down)
    return h2, saved


def _mm_tn(name, a, b, ta, tb, out_shape, out_spec, grid):
    s = a.shape[0]
    return _mm(name, a, b, pl.BlockSpec((s, ta), lambda i, j, k: (0, i)), pl.BlockSpec((s, tb), lambda i, j, k: (0, j)),
               out_shape, out_spec, grid, ((0,), (0,)), 1)


def _layer_bwd_ffn(l, dh2, dh2b, p, sv):
    s = dh2.shape[0]
    tm = _tm(s)
    n = f"l{l}_bwd_"
    dact = _mm(n + "mm_dact", dh2b, sv["w_down"],
               pl.BlockSpec((tm, D), lambda j, i, k: (i, 0)),
               pl.BlockSpec((None, FB, D), lambda j, i, k: (j, 0, 0)),
               jax.ShapeDtypeStruct((s, DFF), f32), pl.BlockSpec((tm, FB), lambda j, i, k: (i, j)),
               (NCHIP, s // tm, 1), ((1,), (1,)), 1)
    g_wd = _mm_tn(n + "mm_dwd", sv["act"], dh2b, FB, 1024, jax.ShapeDtypeStruct((NCHIP, FB, D), bf16),
                  pl.BlockSpec((None, FB, 1024), lambda i, j, k: (i, 0, j)), (NCHIP, 2, 1))
    dgate, dup = _ffn_act_bwd(n + "ffn_act", sv["gate"], sv["up"], dact)
    nt_in = lambda: (pl.BlockSpec((tm, FB), lambda j, i, k: (i, k)),
                     pl.BlockSpec((None, 1024, FB), lambda j, i, k: (k, j, 0)))
    nt_out = lambda: (jax.ShapeDtypeStruct((s, D), f32), pl.BlockSpec((tm, 1024), lambda j, i, k: (i, j)))
    du2 = _mm(n + "mm_du2_gate", dgate, sv["w_gate"], *nt_in(), *nt_out(), (2, s // tm, NCHIP), ((1,), (1,)), NCHIP)
    du2 = _mm(n + "mm_du2_up", dup, sv["w_up"], *nt_in(), *nt_out(), (2, s // tm, NCHIP), ((1,), (1,)), NCHIP,
              res=du2, res_spec=pl.BlockSpec((tm, 1024), lambda j, i, k: (i, j)))
    wg_shape = jax.ShapeDtypeStruct((NCHIP, D, FB), bf16)
    wg_spec = lambda: pl.BlockSpec((None, 1024, FB), lambda i, j, k: (j, i, 0))
    g_wg = _mm_tn(n + "mm_dwg", sv["u2"], dgate, 1024, FB, wg_shape, wg_spec(), (2, NCHIP, 1))
    g_wu = _mm_tn(n + "mm_dwu", sv["u2"], dup, 1024, FB, wg_shape, wg_spec(), (2, NCHIP, 1))
    dh1, dh1b, dg_ffn = _rmsnorm_bwd(n + "norm_ffn", sv["h1"], p["g_ffn"], du2, dh2, l)
    dy = _mm(n + "mm_dy", dh1b, sv["w_out"],
             pl.BlockSpec((tm, D), lambda j, i, k: (i, 0)),
             pl.BlockSpec((None, G, D), lambda j, i, k: (j, 0, 0)),
             jax.ShapeDtypeStruct((s, D), f32), pl.BlockSpec((tm, G), lambda j, i, k: (i, j)),
             (NCHIP, s // tm, 1), ((1,), (1,)), 1)
    g_wo = _mm_tn(n + "mm_dwo", sv["y"], dh1b, G, 1024, jax.ShapeDtypeStruct((NCHIP, G, D), bf16),
                  pl.BlockSpec((None, G, 1024), lambda i, j, k: (i, 0, j)), (NCHIP, 2, 1))
    return dh1, dy, dict(w_out=g_wo, w_gate=g_wg, w_up=g_wu, w_down=g_wd), dg_ffn


def _layer_bwd_mix(l, dh1, dy, p, sv):
    s = dh1.shape[0]
    tm = _tm(s)
    n = f"l{l}_bwd_"
    z = sv["z"]
    dcb, dcc, dch, dpu, dconv, dpoolw, dpools = _convpool_bwd(
        n + "convpool", z, p["conv_w"], p["pool_w"], p["pool_scale"], dy, l)
    do, dgg, dog = _gla3_bwd(n + "gla_out", sv["o"], z, p["gla_og"], dy, l)
    dqe, dkd, dvi, del_ = _gla2_bwd(n + "gla_scan", z, sv["qe"], sv["kd"], sv["el"], sv["st"], do)
    dgq, dgk, dgv, dmisc4, dwd, dbd = _gla1_bwd(n + "gla_chunk", z, p["wdec"], p["bdec"], dqe, dkd, del_, do, dvi, l)
    dfq, dfk, dfv, dfc4, dfr4, dqg, dkg = _fox_bwd(n + "fox_attn", z, sv["fc"], sv["fr"], p["fox_qg"], p["fox_kg"], dy, l)
    dmisc, dbf = _foxprep_bwd(n + "fox_prep", z, p["fox_bf"], dfc4, dfr4, dmisc4, l)
    dz = jnp.concatenate([dcb, dcc, dch, dpu, dgq, dgk, dgv, dgg, dfq, dfk.astype(bf16), dfv.astype(bf16), dmisc],
                         axis=1)
    du = _mm(n + "mm_du", dz, sv["w_in"],
             pl.BlockSpec((tm, 1152), lambda j, i, k: (i, k)), pl.BlockSpec((1024, 1152), lambda j, i, k: (j, k)),
             jax.ShapeDtypeStruct((s, D), f32), pl.BlockSpec((tm, 1024), lambda j, i, k: (i, j)),
             (2, s // tm, ZC // 1152), ((1,), (1,)), ZC // 1152)
    g_wi = _mm_tn(n + "mm_dwi", sv["u"], dz, 1024, 1152, jax.ShapeDtypeStruct((D, ZC), bf16),
                  pl.BlockSpec((1024, 1152), lambda i, j, k: (i, j)), (2, ZC // 1152, 1))
    dh, dhb, dg_mix = _rmsnorm_bwd(n + "norm_mix", sv["h"], p["g_mix"], du, dh1, l)
    small = dict(
        norm_mix_g=dg_mix[0], conv_w=dconv, pool_w=dpoolw, pool_scale=dpools[0],
        gla_w_decay=jnp.concatenate([dwd[hh, GA_LANE:GA_LANE + 16, :64] for hh in range(4)], axis=1),
        gla_b_decay=jnp.concatenate([dbd[hh, 0, :64] for hh in range(4)]),
        gla_out_g=jnp.sum(dog[:, 0, :], axis=0), fox_q_g=jnp.sum(dqg[:, 0, :], axis=0),
        fox_k_g=jnp.sum(dkg[:, 0, :], axis=0), fox_b_f=dbf[0, FF_LANE:FF_LANE + 4])
    return dh, dhb, _win_to_blocks(g_wi), small


def _win_from_blocks(wb):
    def cols(a, b):
        parts = []
        for kk in range(NCHIP):
            lo, hi = max(a, kk * WINB), min(b, (kk + 1) * WINB)
            if lo < hi:
                parts.append(wb[kk, :, lo - kk * WINB:hi - kk * WINB])
        return parts

    zeros = lambda w: [jnp.zeros((wb.shape[1], w), wb.dtype)]
    segs = cols(0, 2048)
    for hh in range(4):
        segs += cols(2048 + 64 * hh, 2112 + 64 * hh) + zeros(64)
    for hh in range(4):
        segs += cols(2304 + 64 * hh, 2368 + 64 * hh) + zeros(64)
    segs += cols(2560, 3584) + cols(3600, 5136)
    segs += cols(5136, 5140) + zeros(GA_LANE - 4) + cols(3584, 3600) + zeros(LANE - GA_LANE - 16)
    return jnp.concatenate(segs, axis=-1)


def _win_to_blocks(g):
    mb = MISC * LANE
    parts = [g[:, 0:2048]]
    parts += [g[:, GQ * LANE + LANE * hh:GQ * LANE + LANE * hh + 64] for hh in range(4)]
    parts += [g[:, GK * LANE + LANE * hh:GK * LANE + LANE * hh + 64] for hh in range(4)]
    parts += [g[:, GV * LANE:FQ * LANE], g[:, mb + GA_LANE:mb + GA_LANE + 16], g[:, FQ * LANE:mb],
              g[:, mb + FF_LANE:mb + FF_LANE + 4]]
    full = jnp.concatenate(parts, axis=1)
    return full.reshape(D, NCHIP, WINB).transpose(1, 0, 2)


def _place():
    x, y, c = lax.axis_index("x"), lax.axis_index("y"), lax.axis_index("c")
    chips = [(1 - x, y), (x, 1 - y), (1 - x, 1 - y)]
    return x, y, c, chips


def _allgather_small(name, v):
    m_per, n = v.shape

    def body(x_ref, out_ref, send_sems, recv_sems, local_sem):
        x, y, c, chips = _place()
        me, sibling = (x, y, c), (x, y, 1 - c)

        def rows(px, py, pc):
            return out_ref.at[pl.ds((4 * px + 2 * py + pc) * m_per, m_per), :]

        def copy(k, block, to, src=None):
            return pltpu.make_async_remote_copy(
                src_ref=rows(*block) if src is None else src, dst_ref=rows(*block),
                send_sem=send_sems.at[k], recv_sem=recv_sems.at[k], device_id=to, device_id_type=MESH)

        mine = pltpu.make_async_copy(x_ref, rows(*me), local_sem)
        mine.start()
        first = [copy(0, me, sibling, src=x_ref)]
        first += [copy(1 + j, me, (*chip, c), src=x_ref) for j, chip in enumerate(chips)]
        for cp in first:
            cp.start()
        passed = [copy(4 + j, (*chip, c), sibling) for j, chip in enumerate(chips)]
        for j, chip in enumerate(chips):
            copy(1 + j, (*chip, c), me).wait_recv()
            passed[j].start()
        copy(0, sibling, me).wait_recv()
        for j, chip in enumerate(chips):
            copy(4 + j, (*chip, 1 - c), me).wait_recv()
        for cp in first + passed:
            cp.wait_send()
        mine.wait()

    return pl.pallas_call(
        body, name=name, out_shape=jax.ShapeDtypeStruct((8 * m_per, n), v.dtype),
        in_specs=[pl.BlockSpec(memory_space=pltpu.VMEM)], out_specs=pl.BlockSpec(memory_space=pltpu.VMEM),
        scratch_shapes=[pltpu.SemaphoreType.DMA((7,)), pltpu.SemaphoreType.DMA((7,)), pltpu.SemaphoreType.DMA],
    )(v)


def _hbm_specs(n):
    return [pl.BlockSpec(memory_space=pl.ANY)] * n


def _own_slot(shard, chip):
    return lax.dynamic_update_index_in_dim(lax.empty((NCHIP,) + shard.shape, shard.dtype), shard, chip, 0)


HBM = pl.BlockSpec(memory_space=pltpu.HBM)
SEM = pl.BlockSpec(memory_space=pltpu.SEMAPHORE)
EFFECT = pltpu.SideEffectType.DATAFLOW_SIDE_EFFECTING


def _in_hbm(v):
    return pltpu.with_memory_space_constraint(v, pltpu.HBM)


def _gather_start(name, groups):
    flat = [b for g in groups for b in g]
    nt, ng = len(flat), len(groups)

    def body(*refs):
        outs = refs[nt:]
        sems, bufs = outs[:2 * ng], outs[2 * ng:2 * ng + nt]
        x, y, c, chips = _place()
        me = 2 * x + y
        t = 0
        for gi, g in enumerate(groups):
            for k in range(len(g)):
                r = bufs[t].shape[1] // 2
                mine = bufs[t].at[me, pl.ds(c * r, r), :]
                for j, (px, py) in enumerate(chips):
                    pltpu.make_async_remote_copy(
                        src_ref=mine, dst_ref=mine, send_sem=sems[2 * gi].at[3 * k + j], recv_sem=sems[2 * gi + 1].at[3 * k + j],
                        device_id=(px, py, c), device_id_type=MESH).start()
                t += 1

    sem_shapes = []
    for g in groups:
        sem_shapes += [pltpu.SemaphoreType.DMA((3 * len(g),))] * 2
    out = pl.pallas_call(
        body, name=name,
        out_shape=tuple(sem_shapes + [pltpu.HBM(b.shape, b.dtype) for b in flat]),
        in_specs=tuple([HBM] * nt), out_specs=tuple([SEM] * (2 * ng) + [HBM] * nt),
        input_output_aliases={t: 2 * ng + t for t in range(nt)},
        compiler_params=pltpu.CompilerParams(has_side_effects=EFFECT),
    )(*[_in_hbm(b) for b in flat])
    sems = [(out[2 * gi], out[2 * gi + 1]) for gi in range(ng)]
    bufs, at = [], 2 * ng
    for g in groups:
        bufs.append(list(out[at:at + len(g)]))
        at += len(g)
    return sems, bufs


def _gather_wait(name, bufs, send_sems, recv_sems, after):
    nt = len(bufs)

    def body(*refs):
        ins, ss, rs = refs[:nt], refs[nt], refs[nt + 1]
        x, y, c, chips = _place()
        me = 2 * x + y
        for k in range(nt):
            r = ins[k].shape[1] // 2
            for j, (px, py) in enumerate(chips):
                cp = pltpu.make_async_remote_copy(
                    src_ref=ins[k].at[me, pl.ds(c * r, r), :], dst_ref=ins[k].at[2 * px + py, pl.ds(c * r, r), :],
                    send_sem=ss.at[3 * k + j], recv_sem=rs.at[3 * k + j], device_id=(px, py, c), device_id_type=MESH)
                cp.wait_send()
                cp.wait_recv()

    out = pl.pallas_call(
        body, name=name, out_shape=tuple(pltpu.HBM(b.shape, b.dtype) for b in bufs),
        in_specs=tuple([HBM] * nt + [SEM, SEM, pl.BlockSpec(memory_space=pl.ANY)]), out_specs=tuple([HBM] * nt),
        input_output_aliases={t: t for t in range(nt)},
        compiler_params=pltpu.CompilerParams(has_side_effects=EFFECT),
    )(*bufs, send_sems, recv_sems, after)
    return list(out)


def _gather_exchange(name, bufs):
    nt = len(bufs)

    def body(*refs):
        outs = refs[nt:2 * nt]
        send_sems, recv_sems = refs[2 * nt:]
        x, y, c, chips = _place()
        sibling = (x, y, 1 - c)

        def half(t, chip_idx, cc):
            r = outs[t].shape[1] // 2
            return outs[t].at[chip_idx, pl.ds(cc * r, r), :]

        sent = []
        for t in range(nt):
            for j, (px, py) in enumerate(chips):
                cp = pltpu.make_async_remote_copy(
                    src_ref=half(t, 2 * px + py, c), dst_ref=half(t, 2 * px + py, c),
                    send_sem=send_sems.at[t, j], recv_sem=recv_sems.at[t, j], device_id=sibling, device_id_type=MESH)
                cp.start()
                sent.append(cp)
        for t in range(nt):
            for j, (px, py) in enumerate(chips):
                pltpu.make_async_remote_copy(
                    src_ref=half(t, 2 * px + py, 1 - c), dst_ref=half(t, 2 * px + py, 1 - c),
                    send_sem=send_sems.at[t, j], recv_sem=recv_sems.at[t, j], device_id=sibling,
                    device_id_type=MESH).wait_recv()
        for cp in sent:
            cp.wait_send()

    return pl.pallas_call(
        body, name=name, out_shape=[jax.ShapeDtypeStruct(v.shape, v.dtype) for v in bufs],
        in_specs=_hbm_specs(nt), out_specs=_hbm_specs(nt), input_output_aliases={t: t for t in range(nt)},
        scratch_shapes=[pltpu.SemaphoreType.DMA((nt, 3)), pltpu.SemaphoreType.DMA((nt, 3))],
    )(*bufs)


def _rs_to_sibling(name, grads):
    nt = len(grads)

    def body(*refs):
        ins, outs = refs[:nt], refs[nt:2 * nt]
        send_sems, recv_sems = refs[2 * nt:]
        x, y, c, _ = _place()
        cps = []
        for t in range(nt):
            r = ins[t].shape[1] // 2
            cp = pltpu.make_async_remote_copy(
                src_ref=ins[t].at[:, pl.ds((1 - c) * r, r), :], dst_ref=outs[t],
                send_sem=send_sems.at[t], recv_sem=recv_sems.at[t], device_id=(x, y, 1 - c), device_id_type=MESH)
            cp.start()
            cps.append(cp)
        for cp in cps:
            cp.wait()

    return pl.pallas_call(
        body, name=name,
        out_shape=[jax.ShapeDtypeStruct((NCHIP, g.shape[1] // 2, g.shape[2]), g.dtype) for g in grads],
        in_specs=_hbm_specs(nt), out_specs=_hbm_specs(nt),
        scratch_shapes=[pltpu.SemaphoreType.DMA((nt,)), pltpu.SemaphoreType.DMA((nt,))],
    )(*grads)


def _rs_pair_sum(name, pos, g, other):
    r, cdim = other.shape[1], other.shape[2]
    tr = r // 4 if (r // 4) % 16 == 0 else r // 2
    nblk = r // tr

    def body(pos_ref, g_ref, o_ref, s_ref):
        s_ref[...] = (g_ref[...].astype(f32) + o_ref[...].astype(f32)).astype(bf16)

    blk = pl.BlockSpec((None, tr, cdim), lambda q, i, p: (q, i, 0))
    return pl.pallas_call(
        body, name=name, out_shape=jax.ShapeDtypeStruct(other.shape, bf16),
        grid_spec=pltpu.PrefetchScalarGridSpec(
            num_scalar_prefetch=1, grid=(NCHIP, nblk),
            in_specs=[pl.BlockSpec((None, tr, cdim), lambda q, i, p: (q, p[1] * nblk + i, 0)), blk], out_specs=blk),
        compiler_params=_cp(("parallel", "parallel")),
    )(pos, g, other)


def _scatter_start(name, sums):
    nt = len(sums)

    def body(*refs):
        outs = refs[2 * nt:]
        ss, rs, src, land = outs[0], outs[1], outs[2:2 + nt], outs[2 + nt:2 + 2 * nt]
        x, y, c, chips = _place()
        me = 2 * x + y
        for t in range(nt):
            for j, (px, py) in enumerate(chips):
                pltpu.make_async_remote_copy(
                    src_ref=src[t].at[2 * px + py], dst_ref=land[t].at[me], send_sem=ss.at[3 * t + j], recv_sem=rs.at[3 * t + j],
                    device_id=(px, py, c), device_id_type=MESH).start()

    shapes = [pltpu.HBM(v.shape, v.dtype) for v in sums]
    out = pl.pallas_call(
        body, name=name,
        out_shape=tuple([pltpu.SemaphoreType.DMA((3 * nt,))] * 2 + shapes + shapes),
        in_specs=tuple([HBM] * (2 * nt)), out_specs=tuple([SEM, SEM] + [HBM] * (2 * nt)),
        input_output_aliases={t: 2 + t for t in range(2 * nt)},
        compiler_params=pltpu.CompilerParams(has_side_effects=EFFECT),
    )(*[_in_hbm(v) for v in sums], *[_in_hbm(lax.empty(v.shape, v.dtype)) for v in sums])
    return out[0], out[1], list(out[2:2 + nt]), list(out[2 + nt:])


def _scatter_wait(name, sums, land, send_sems, recv_sems, after):
    nt = len(sums)

    def body(*refs):
        src, dst, ss, rs = refs[:nt], refs[nt:2 * nt], refs[2 * nt], refs[2 * nt + 1]
        x, y, c, chips = _place()
        for t in range(nt):
            for j, (px, py) in enumerate(chips):
                cp = pltpu.make_async_remote_copy(
                    src_ref=src[t].at[2 * px + py], dst_ref=dst[t].at[2 * px + py], send_sem=ss.at[3 * t + j],
                    recv_sem=rs.at[3 * t + j], device_id=(px, py, c), device_id_type=MESH)
                cp.wait_send()
                cp.wait_recv()

    shapes = [pltpu.HBM(v.shape, v.dtype) for v in sums]
    out = pl.pallas_call(
        body, name=name, out_shape=tuple(shapes + shapes),
        in_specs=tuple([HBM] * (2 * nt) + [SEM, SEM, pl.BlockSpec(memory_space=pl.ANY)]),
        out_specs=tuple([HBM] * (2 * nt)), input_output_aliases={t: t for t in range(2 * nt)},
        compiler_params=pltpu.CompilerParams(has_side_effects=EFFECT),
    )(*sums, *land, send_sems, recv_sems, after)
    return list(out[:nt]), list(out[nt:])


def _rs_chip_sum(name, pos, sums, parts):
    r, cdim = parts.shape[1], parts.shape[2]
    tr = r // 4 if (r // 4) % 16 == 0 else r // 2
    nblk = r // tr

    def body(pos_ref, own_ref, a_ref, b_ref, c_ref, o_ref):
        o_ref[...] = ((own_ref[...].astype(f32) + a_ref[...].astype(f32)) + b_ref[...].astype(f32)) \
            + c_ref[...].astype(f32)

    def slot(k):
        return pl.BlockSpec((None, tr, cdim), lambda i, p: ((p[0] + k) % NCHIP, i, 0))

    return pl.pallas_call(
        body, name=name, out_shape=jax.ShapeDtypeStruct((2 * r, cdim), f32),
        grid_spec=pltpu.PrefetchScalarGridSpec(
            num_scalar_prefetch=1, grid=(nblk,), in_specs=[slot(0), slot(1), slot(2), slot(3)],
            out_specs=pl.BlockSpec((tr, cdim), lambda i, p: (p[1] * nblk + i, 0))),
        compiler_params=_cp(("parallel",)),
    )(pos, sums, parts, parts, parts)


def _rs_share_halves(name, bufs):
    nt = len(bufs)

    def body(*refs):
        outs = refs[nt:2 * nt]
        send_sems, recv_sems = refs[2 * nt:]
        x, y, c, _ = _place()
        cps = []
        for t in range(nt):
            r = outs[t].shape[0] // 2
            mine = outs[t].at[pl.ds(c * r, r), :]
            theirs = outs[t].at[pl.ds((1 - c) * r, r), :]
            cp = pltpu.make_async_remote_copy(
                src_ref=mine, dst_ref=mine, send_sem=send_sems.at[t], recv_sem=recv_sems.at[t],
                device_id=(x, y, 1 - c), device_id_type=MESH)
            cp.start()
            cps.append((cp, theirs))
        for t, (cp, theirs) in enumerate(cps):
            pltpu.make_async_remote_copy(
                src_ref=theirs, dst_ref=theirs, send_sem=send_sems.at[t], recv_sem=recv_sems.at[t],
                device_id=(x, y, 1 - c), device_id_type=MESH).wait_recv()
            cp.wait_send()

    return pl.pallas_call(
        body, name=name, out_shape=[jax.ShapeDtypeStruct(v.shape, v.dtype) for v in bufs],
        in_specs=_hbm_specs(nt), out_specs=_hbm_specs(nt), input_output_aliases={t: t for t in range(nt)},
        scratch_shapes=[pltpu.SemaphoreType.DMA((nt,)), pltpu.SemaphoreType.DMA((nt,))],
    )(*bufs)


BIG = ("w_in", "w_out", "w_gate", "w_up", "w_down")


GROUPS = (("w_in",), ("w_out",), ("w_gate", "w_up", "w_down"))


def _rs_begin(tag, keys, grads, pos):
    got = _rs_to_sibling(tag + "to_sibling", grads)
    sums = [_rs_pair_sum(tag + "pair_sum_" + k, pos, g, o) for k, g, o in zip(keys, grads, got)]
    return (keys,) + _scatter_start(tag + "start", sums)


def _rs_end(tag, state, pos, after):
    keys, send_sems, recv_sems, sums, land = state
    sums, land = _scatter_wait(tag + "wait", sums, land, send_sems, recv_sems, after)
    halves = [_rs_chip_sum(tag + "chip_sum_" + k, pos, s, v) for k, s, v in zip(keys, sums, land)]
    return dict(zip(keys, _rs_share_halves(tag + "share", halves)))


def _adam_math(w, g, m, v):
    m = ADAM_B1 * m + (1.0 - ADAM_B1) * g
    v = ADAM_B2 * v + (1.0 - ADAM_B2) * (g * g)
    m_hat = m / (1.0 - ADAM_B1 ** ADAM_STEP)
    v_hat = v / (1.0 - ADAM_B2 ** ADAM_STEP)
    delta = -ADAM_LR * (m_hat / (jnp.sqrt(v_hat) + ADAM_EPS) + ADAM_WD * w)
    return delta, m, v


def _adam_big(name, g0, g1, w, m, v):
    _, r, cdim = w.shape
    tr = 128 if r % 128 == 0 else 64
    nb = r // tr

    def body(g0_ref, g1_ref, w_ref, m_ref, v_ref, go_ref, d_ref, mo_ref, vo_ref):
        l = pl.program_id(0)
        g = jnp.where(l == 0, g0_ref[...], g1_ref[...])
        delta, mn, vn = _adam_math(w_ref[...], g, m_ref[...], v_ref[...])
        go_ref[...] = g
        d_ref[...] = delta
        mo_ref[...] = mn
        vo_ref[...] = vn

    lay = pl.BlockSpec((None, tr, cdim), lambda l, i: (l, i, 0))
    return pl.pallas_call(
        body, name=name, grid=(2, nb),
        in_specs=[pl.BlockSpec((tr, cdim), lambda l, i: (i * (1 - l) + (nb - 1) * l, 0)),
                  pl.BlockSpec((tr, cdim), lambda l, i: (i * l, 0)), lay, lay, lay],
        out_specs=[lay] * 4, out_shape=[jax.ShapeDtypeStruct(w.shape, f32)] * 4,
        compiler_params=_cp(("arbitrary", "arbitrary")),
    )(g0, g1, w, m, v)


def _sum8(name, gathered):
    m_per = gathered.shape[0] // 8

    def body(g_ref, o_ref):
        tot = g_ref[pl.ds(0, m_per), :]
        for d in range(1, 8):
            tot = tot + g_ref[pl.ds(d * m_per, m_per), :]
        o_ref[...] = tot

    return pl.pallas_call(body, name=name, out_shape=jax.ShapeDtypeStruct((m_per, LANE), f32))(gathered)


def _adam_small(name, g, w, m, v):
    def body(g_ref, w_ref, m_ref, v_ref, d_ref, mo_ref, vo_ref):
        delta, mn, vn = _adam_math(w_ref[...], g_ref[...], m_ref[...], v_ref[...])
        d_ref[...] = delta
        mo_ref[...] = mn
        vo_ref[...] = vn

    return pl.pallas_call(body, name=name, out_shape=[jax.ShapeDtypeStruct(g.shape, f32)] * 3)(g, w, m, v)


def _pack(vals):
    rows, offs, at = [], [], 0
    for a in vals:
        a = a.reshape(-1)
        n = -(-a.shape[0] // (8 * LANE)) * 8
        rows.append(jnp.pad(a, (0, n * LANE - a.shape[0])).reshape(n, LANE))
        offs.append(at)
        at += n
    return jnp.concatenate(rows, axis=0), offs


def _unpack(packed, offs, shapes):
    out = []
    for o, shp in zip(offs, shapes):
        size = 1
        for d in shp:
            size *= d
        n = -(-size // LANE)
        out.append(packed[o:o + n].reshape(-1)[:size].reshape(shp))
    return out


SMALL = ("norm_mix_g", "conv_w", "pool_w", "pool_scale", "gla_w_decay", "gla_b_decay", "gla_out_g",
         "fox_q_g", "fox_k_g", "fox_b_f", "norm_ffn_g")
ALL = ("norm_mix_g", "w_in", "conv_w", "pool_w", "pool_scale", "gla_w_decay", "gla_b_decay", "gla_out_g",
       "fox_q_g", "fox_k_g", "fox_b_f", "w_out", "norm_ffn_g", "w_gate", "w_up", "w_down")


def kernel(x, norm_mix_g, w_in, conv_w, pool_w, pool_scale, gla_w_decay, gla_b_decay, gla_out_g, fox_q_g, fox_k_g, fox_b_f, w_out, norm_ffn_g, w_gate, w_up, w_down, loss_target, m_norm_mix_g, m_w_in, m_conv_w, m_pool_w, m_pool_scale, m_gla_w_decay, m_gla_b_decay, m_gla_out_g, m_fox_q_g, m_fox_k_g, m_fox_b_f, m_w_out, m_norm_ffn_g, m_w_gate, m_w_up, m_w_down, v_norm_mix_g, v_w_in, v_conv_w, v_pool_w, v_pool_scale, v_gla_w_decay, v_gla_b_decay, v_gla_out_g, v_fox_q_g, v_fox_k_g, v_fox_b_f, v_w_out, v_norm_ffn_g, v_w_gate, v_w_up, v_w_down):
    w = dict(norm_mix_g=norm_mix_g, w_in=w_in, conv_w=conv_w, pool_w=pool_w, pool_scale=pool_scale,
             gla_w_decay=gla_w_decay, gla_b_decay=gla_b_decay, gla_out_g=gla_out_g, fox_q_g=fox_q_g, fox_k_g=fox_k_g,
             fox_b_f=fox_b_f, w_out=w_out, norm_ffn_g=norm_ffn_g, w_gate=w_gate, w_up=w_up, w_down=w_down)
    m = dict(norm_mix_g=m_norm_mix_g, w_in=m_w_in, conv_w=m_conv_w, pool_w=m_pool_w, pool_scale=m_pool_scale,
             gla_w_decay=m_gla_w_decay, gla_b_decay=m_gla_b_decay, gla_out_g=m_gla_out_g, fox_q_g=m_fox_q_g,
             fox_k_g=m_fox_k_g, fox_b_f=m_fox_b_f, w_out=m_w_out, norm_ffn_g=m_norm_ffn_g, w_gate=m_w_gate,
             w_up=m_w_up, w_down=m_w_down)
    v = dict(norm_mix_g=v_norm_mix_g, w_in=v_w_in, conv_w=v_conv_w, pool_w=v_pool_w, pool_scale=v_pool_scale,
             gla_w_decay=v_gla_w_decay, gla_b_decay=v_gla_b_decay, gla_out_g=v_gla_out_g, fox_q_g=v_fox_q_g,
             fox_k_g=v_fox_k_g, fox_b_f=v_fox_b_f, w_out=v_w_out, norm_ffn_g=v_norm_ffn_g, w_gate=v_w_gate,
             w_up=v_w_up, w_down=v_w_down)
    chip = 2 * lax.axis_index("x") + lax.axis_index("y")

    mine, offs = _pack([conv_w, gla_w_decay])
    rows = mine.shape[0]
    every = _allgather_small("gather_small_params", mine).reshape(NCHIP, 2, rows, LANE)[:, 0]
    per_chip = [_unpack(every[kk], offs, [conv_w.shape, gla_w_decay.shape]) for kk in range(NCHIP)]
    conv_full = jnp.concatenate([pc[0] for pc in per_chip], axis=-1)[:, :, 0, :]
    wdec_full = jnp.concatenate([pc[1] for pc in per_chip], axis=-1)

    pos = jnp.stack([chip, lax.axis_index("c")]).astype(jnp.int32)

    order = [(l, grp) for l in range(2) for grp in GROUPS]
    sems, gbufs = _gather_start(
        "gather_start", [[_own_slot(w[k][l].astype(bf16), chip) for k in grp] for l, grp in order])

    def weights(l, group, after):
        gi = 3 * l + ("w_in", "w_out", "ffn").index(group)
        got = _gather_wait(f"l{l}_gather_wait_{group}", gbufs[gi], sems[gi][0], sems[gi][1], after)
        got = _gather_exchange(f"l{l}_gather_exchange_{group}", got)
        if group == "w_in":
            return _win_from_blocks(got[0])
        return got[0] if group == "w_out" else got

    wdec = jnp.pad(wdec_full.reshape(2, 16, 4, 64), ((0, 0), (GA_LANE, LANE - GA_LANE - 16), (0, 0), (0, 64)))
    p = dict(
        g_mix=norm_mix_g[:, None, :], g_ffn=norm_ffn_g[:, None, :],
        conv_w=conv_full, pool_w=pool_w, pool_scale=pool_scale[:, None, :],
        wdec=wdec.reshape(2, LANE, G),
        bdec=jnp.pad(gla_b_decay.reshape(2, 4, 64), ((0, 0), (0, 0), (0, 64))).reshape(2, 1, G),
        gla_og=gla_out_g[:, None, :], fox_qg=fox_q_g[:, None, :], fox_kg=fox_k_g[:, None, :],
        fox_bf=jnp.pad(fox_b_f, ((0, 0), (FF_LANE, LANE - FF_LANE - 4)))[:, None, :])

    h0 = x[0]
    h1, sv0 = _layer_fwd(0, h0, p, weights)
    h2, sv1 = _layer_fwd(1, h1, p, weights)
    sq, dh, dhb = _loss("loss", h2, loss_target[0])
    loss = lax.psum(sq[0, 0] * (0.5 / D), ("x", "y", "c"))

    late = ("w_out", "w_gate", "w_up", "w_down")
    dh1, dy, big1, dgf1 = _layer_bwd_ffn(1, dh, dhb, p, sv1)
    dh, dhb, big1["w_in"], small1 = _layer_bwd_mix(1, dh1, dy, p, sv1)
    rs1 = _rs_begin("l1_rs_", BIG, [big1[k] for k in BIG], pos)
    dh1, dy, big0, dgf0 = _layer_bwd_ffn(0, dh, dhb, p, sv0)
    rs0a = _rs_begin("l0_rs_a_", late, [big0[k] for k in late], pos)
    dh, dhb, gwi0, small0 = _layer_bwd_mix(0, dh1, dy, p, sv0)
    rs0b = _rs_begin("l0_rs_b_", ("w_in",), [gwi0], pos)
    small0["norm_ffn_g"], small1["norm_ffn_g"] = dgf0[0], dgf1[0]

    red1 = _rs_end("l1_rs_", rs1, pos, dh)
    red0 = _rs_end("l0_rs_a_", rs0a, pos, red1["w_in"])
    grads, deltas, new_m, new_v = {}, {}, {}, {}
    for k in late:
        grads[k], deltas[k], new_m[k], new_v[k] = _adam_big("adam_" + k, red0[k], red1[k], w[k], m[k], v[k])

    packed, goffs = _pack([jnp.stack([small0[k], small1[k]]) for k in SMALL])
    total = _sum8("sum_small_grads", _allgather_small("gather_small_grads", packed))
    red0.update(_rs_end("l0_rs_b_", rs0b, pos, total))
    k = "w_in"
    grads[k], deltas[k], new_m[k], new_v[k] = _adam_big("adam_" + k, red0[k], red1[k], w[k], m[k], v[k])
    full_shapes = [(2,) + small0[k].shape for k in SMALL]
    gsmall = dict(zip(SMALL, _unpack(total, goffs, full_shapes)))
    gsmall["conv_w"] = lax.dynamic_slice_in_dim(gsmall["conv_w"], chip * LANE, LANE, axis=2)[:, :, None, :]
    gsmall["gla_w_decay"] = lax.dynamic_slice_in_dim(gsmall["gla_w_decay"], chip * 64, 64, axis=2)
    gp, loffs = _pack([gsmall[k] for k in SMALL])
    wp, _ = _pack([w[k] for k in SMALL])
    mp, _ = _pack([m[k] for k in SMALL])
    vp, _ = _pack([v[k] for k in SMALL])
    dp, mnp, vnp = _adam_small("adam_small", gp, wp, mp, vp)
    shapes = [w[k].shape for k in SMALL]
    for k, a, b, c_, d_ in zip(SMALL, _unpack(gp, loffs, shapes), _unpack(dp, loffs, shapes),
                               _unpack(mnp, loffs, shapes), _unpack(vnp, loffs, shapes)):
        grads[k], deltas[k], new_m[k], new_v[k] = a, b, c_, d_

    return (loss, dh[None], *[grads[k] for k in ALL], *[deltas[k] for k in ALL],
            *[new_m[k] for k in ALL], *[new_v[k] for k in ALL])
```

```python
import functools

import jax
import jax.numpy as jnp
from jax import lax
from jax.experimental import pallas as pl
from jax.experimental.pallas import tpu as pltpu

f32 = jnp.float32
bf16 = jnp.bfloat16

D = 2048
G = 512
DFF = 5632
NCHIP = 4
FB = DFF // NCHIP
WIN = 5140
WINB = WIN // NCHIP
EPS = 1e-6
CHUNK = 64
LANE = 128

CB, CC, CH, PU, GQ, GK, GV, GG, FQ, FK, FV, MISC = 0, 4, 8, 12, 16, 20, 24, 28, 32, 36, 40, 44
ZC = 45 * LANE
FF_LANE = 0
GA_LANE = 8

ADAM_LR, ADAM_B1, ADAM_B2, ADAM_EPS, ADAM_WD, ADAM_STEP = 0.001, 0.9, 0.999, 1e-08, 0.01, 10

VMEM_LIMIT = 60 * 1024 * 1024
MESH = pl.DeviceIdType.MESH


def _cp(sem=None):
    return pltpu.CompilerParams(dimension_semantics=sem, vmem_limit_bytes=VMEM_LIMIT)


def _dot(a, b, dims=((1,), (0,))):
    return lax.dot_general(a.astype(bf16), b.astype(bf16), (dims, ((), ())), preferred_element_type=f32)


def _bdot(a, b, ca, cb):
    return lax.dot_general(a.astype(bf16), b.astype(bf16), (((ca,), (cb,)), ((0,), (0,))),
                           preferred_element_type=f32)


def _log_sigmoid(x):
    return jnp.minimum(x, 0.0) - jnp.log(1.0 + jnp.exp(-jnp.abs(x)))


@jax.custom_vjp
def _sigmoid(x):
    return 1.0 / (1.0 + jnp.exp(-x))


def _sigmoid_fwd(x):
    s = _sigmoid(x)
    return s, s


def _sigmoid_bwd(s, g):
    return (g * s * (1.0 - s),)


_sigmoid.defvjp(_sigmoid_fwd, _sigmoid_bwd)


def _rms(x, g):
    return x * lax.rsqrt(jnp.mean(x * x, axis=-1, keepdims=True) + EPS) * g


def _shift_impl(x, n, period, transpose):
    rows = x.shape[0]
    t = lax.broadcasted_iota(jnp.int32, x.shape, 0)
    if period is not None:
        t = t & (period - 1)
    keep = t >= n
    if not transpose:
        return jnp.where(keep, pltpu.roll(x, n, 0), 0.0)
    return pltpu.roll(jnp.where(keep, x, 0.0), rows - n, 0)


def _shift(x, n, period=None):
    @jax.custom_vjp
    def f(v):
        return _shift_impl(v, n, period, False)

    def fwd(v):
        return f(v), None

    def bwd(_, g):
        return (_shift_impl(g, n, period, True),)

    f.defvjp(fwd, bwd)
    return f(x)


def _cumsum_rows(x, length, period=None):
    n = 1
    while n < length:
        x = x + _shift(x, n, period)
        n *= 2
    return x


def _convpool_fn(cb, cc, ch, pu, w0, w1, w2, pw, ps, j):
    u = cc * ch
    y = w2 * u + w1 * _shift(u, 1) + w0 * _shift(u, 2)
    ya = cb * y
    s2 = pu + _shift(pu, 1)
    s4 = s2 + _shift(s2, 2)
    s8 = s4 + _shift(s4, 4)
    s16 = s8 + _shift(s8, 8)
    wsum = jnp.where(j == 0, s2, jnp.where(j == 1, s4, jnp.where(j == 2, s8, s16)))
    width = (2 << j).astype(f32)
    t = lax.broadcasted_iota(jnp.int32, pu.shape, 0).astype(f32)
    count = jnp.minimum(t + 1.0, width)
    d = wsum / count - pu
    yb = _dot(d, pw) * ps
    return ya, yb


def _foxprep_fn(misc, bf):
    lf = _log_sigmoid(misc + bf)
    fc = _cumsum_rows(lf, lf.shape[0])
    return fc, jnp.transpose(fc)


def _fox_fn(q, k, v, fcol, frow8, qg, kg, h, i):
    tq, s = q.shape[0], k.shape[0]
    qn = _rms(q, qg)
    kn = _rms(k, kg)
    lg = _dot(qn, kn, ((1,), (1,))) * (LANE ** -0.5)
    lane = lax.broadcasted_iota(jnp.int32, fcol.shape, 1)
    fq = jnp.sum(jnp.where(lane == h, fcol, 0.0), axis=1, keepdims=True)
    row = lax.broadcasted_iota(jnp.int32, frow8.shape, 0)
    fk = jnp.sum(jnp.where(row == h, frow8, 0.0), axis=0, keepdims=True)
    lg = lg + fq - fk
    qpos = i * tq + lax.broadcasted_iota(jnp.int32, (tq, s), 0)
    kpos = lax.broadcasted_iota(jnp.int32, (tq, s), 1)
    lg = jnp.where(kpos <= qpos, lg, -jnp.inf)
    m = lax.stop_gradient(jnp.max(lg, axis=1, keepdims=True))
    e = jnp.exp(lg - m)
    p = e / jnp.sum(e, axis=1, keepdims=True)
    return _dot(p, v)


def _gla1_fn(q, k, v, misc, wd, bd):
    ts = q.shape[0]
    nb = ts // CHUNK
    x = _dot(misc, wd) + bd
    la = _log_sigmoid(x) * (1.0 / 16.0)
    cc = _cumsum_rows(la, CHUNK, CHUNK)
    la3 = la.reshape(nb, CHUNK, LANE)
    last3 = jnp.sum(la3, axis=1, keepdims=True)
    last2 = jnp.sum(la3, axis=1)
    cc3 = cc.reshape(nb, CHUNK, LANE)
    q3 = (q * 0.125).reshape(nb, CHUNK, LANE)
    k3 = k.reshape(nb, CHUNK, LANE)
    v3 = v.reshape(nb, CHUNK, LANE)
    ep = jnp.exp(cc3)
    en = jnp.exp(-cc3)
    qe = q3 * ep
    a1 = _bdot(qe, k3 * en, 2, 2)
    a2 = _bdot(q3 * en, k3 * ep, 2, 2)
    ti = lax.broadcasted_iota(jnp.int32, a1.shape, 1)
    si = lax.broadcasted_iota(jnp.int32, a1.shape, 2)
    sc = jnp.where(si <= ti, a1, a2)
    oi = _bdot(sc, v3, 2, 1)
    kd = k3 * jnp.exp(last3 - cc3)
    el = jnp.exp(last2)
    return qe.reshape(ts, LANE), kd.reshape(ts, LANE), el, oi.reshape(ts, LANE)


def _gla3_fn(o, gg, og):
    return _rms(o, og) * (gg * _sigmoid(gg))


def _ffn_fn(gate, up):
    return gate * _sigmoid(gate) * up


def _mm(name, a, b, a_spec, b_spec, out_shape, out_spec, grid, dims, nk, res=None, res_spec=None, dep=None):
    has_res = res is not None
    has_dep = dep is not None
    nax = len(grid)

    def body(*refs):
        a_ref, b_ref = refs[0], refs[1]
        res_ref = refs[2] if has_res else None
        out_ref = refs[2 + has_res + has_dep]
        part = _dot(a_ref[...], b_ref[...], dims)
        if nk == 1:
            if has_res:
                part = part + res_ref[...]
            out_ref[...] = part.astype(out_ref.dtype)
            return
        acc_ref = refs[3 + has_res + has_dep]
        k = pl.program_id(nax - 1)

        @pl.when(k == 0)
        def _():
            acc_ref[...] = part

        @pl.when(k > 0)
        def _():
            acc_ref[...] += part

        @pl.when(k == nk - 1)
        def _():
            tot = acc_ref[...]
            if has_res:
                tot = tot + res_ref[...]
            out_ref[...] = tot.astype(out_ref.dtype)

    ops = [a, b] + ([res] if has_res else []) + ([dep] if has_dep else [])
    specs = [a_spec, b_spec] + ([res_spec] if has_res else [])
    if has_dep:
        specs.append(pl.BlockSpec((8, LANE), lambda *_: (0, 0)))
    blk = tuple(d for d in out_spec.block_shape if d is not None)
    scratch = [pltpu.VMEM(blk, f32)] if nk > 1 else []
    return pl.pallas_call(
        body, name=name, grid=grid, in_specs=specs, out_specs=out_spec, out_shape=out_shape,
        scratch_shapes=scratch,
        compiler_params=_cp(("parallel",) * (nax - 1) + ("arbitrary",)),
    )(*ops)


def _tm(s):
    return min(s, 512)


def _rmsnorm_fwd(name, x, g, l):
    s = x.shape[0]
    tm = min(s, 256)

    def body(x_ref, g_ref, u_ref):
        u_ref[...] = _rms(x_ref[...], g_ref[...]).astype(bf16)

    return pl.pallas_call(
        body, name=name, grid=(s // tm,),
        in_specs=[pl.BlockSpec((tm, D), lambda i: (i, 0)), pl.BlockSpec((None, 1, D), lambda i: (l, 0, 0))],
        out_specs=pl.BlockSpec((tm, D), lambda i: (i, 0)),
        out_shape=jax.ShapeDtypeStruct((s, D), bf16), compiler_params=_cp(("parallel",)),
    )(x, g)


def _rmsnorm_bwd(name, x, g, du, dres, l):
    s = x.shape[0]
    tm = min(s, 256)

    def body(x_ref, g_ref, du_ref, dres_ref, dx_ref, dxb_ref, dg_ref):
        _, vjp = jax.vjp(_rms, x_ref[...], g_ref[...])
        dx, dg = vjp(du_ref[...])
        tot = dx + dres_ref[...]
        dx_ref[...] = tot
        dxb_ref[...] = tot.astype(bf16)

        @pl.when(pl.program_id(0) == 0)
        def _():
            dg_ref[...] = dg

        @pl.when(pl.program_id(0) > 0)
        def _():
            dg_ref[...] += dg

    row = pl.BlockSpec((tm, D), lambda i: (i, 0))
    return pl.pallas_call(
        body, name=name, grid=(s // tm,),
        in_specs=[row, pl.BlockSpec((None, 1, D), lambda i: (l, 0, 0)), row, row],
        out_specs=[row, row, pl.BlockSpec((1, D), lambda i: (0, 0))],
        out_shape=[jax.ShapeDtypeStruct((s, D), f32), jax.ShapeDtypeStruct((s, D), bf16),
                   jax.ShapeDtypeStruct((1, D), f32)],
        compiler_params=_cp(("arbitrary",)),
    )(x, g, du, dres)


def _ffn_act(name, gate, up):
    s = gate.shape[0]
    tm = min(s, 256)
    spec = pl.BlockSpec((tm, FB), lambda i, j: (i, j))

    def body(g_ref, u_ref, a_ref):
        a_ref[...] = _ffn_fn(g_ref[...], u_ref[...]).astype(bf16)

    return pl.pallas_call(
        body, name=name, grid=(s // tm, NCHIP), in_specs=[spec, spec], out_specs=spec,
        out_shape=jax.ShapeDtypeStruct((s, DFF), bf16), compiler_params=_cp(("parallel", "parallel")),
    )(gate, up)


def _ffn_act_bwd(name, gate, up, dact):
    s = gate.shape[0]
    tm = min(s, 256)
    spec = pl.BlockSpec((tm, FB), lambda i, j: (i, j))

    def body(g_ref, u_ref, d_ref, dg_ref, du_ref):
        _, vjp = jax.vjp(_ffn_fn, g_ref[...], u_ref[...])
        dg, du = vjp(d_ref[...])
        dg_ref[...] = dg.astype(bf16)
        du_ref[...] = du.astype(bf16)

    return pl.pallas_call(
        body, name=name, grid=(s // tm, NCHIP), in_specs=[spec, spec, spec], out_specs=[spec, spec],
        out_shape=[jax.ShapeDtypeStruct((s, DFF), bf16)] * 2, compiler_params=_cp(("parallel", "parallel")),
    )(gate, up, dact)


def _loss(name, y, t):
    s = y.shape[0]
    tm = min(s, 256)
    row = pl.BlockSpec((tm, D), lambda i: (i, 0))

    def body(y_ref, t_ref, l_ref, d_ref, db_ref):
        e = y_ref[...] - t_ref[...]
        d = e * (1.0 / D)
        d_ref[...] = d
        db_ref[...] = d.astype(bf16)
        part = jnp.zeros((8, LANE), f32) + jnp.sum(e * e)

        @pl.when(pl.program_id(0) == 0)
        def _():
            l_ref[...] = part

        @pl.when(pl.program_id(0) > 0)
        def _():
            l_ref[...] += part

    return pl.pallas_call(
        body, name=name, grid=(s // tm,), in_specs=[row, row],
        out_specs=[pl.BlockSpec((8, LANE), lambda i: (0, 0)), row, row],
        out_shape=[jax.ShapeDtypeStruct((8, LANE), f32), jax.ShapeDtypeStruct((s, D), f32),
                   jax.ShapeDtypeStruct((s, D), bf16)],
        compiler_params=_cp(("arbitrary",)),
    )(y, t)


def _zspec(s, blk):
    return pl.BlockSpec((s, LANE), lambda j: (0, blk + j))


def _convpool_specs(s, l):
    return [_zspec(s, CB), _zspec(s, CC), _zspec(s, CH), _zspec(s, PU),
            pl.BlockSpec((None, 3, LANE), lambda j: (l, 0, j)),
            pl.BlockSpec((None, None, LANE, LANE), lambda j: (l, j, 0, 0)),
            pl.BlockSpec((None, 1, LANE), lambda j: (l, 0, j))]


def _convpool_fwd(name, z, conv_w, pool_w, pool_scale, l):
    s = z.shape[0]

    def body(cb, cc, ch, pu, cw, pw, ps, ya_ref, yb_ref):
        ya, yb = _convpool_fn(cb[...], cc[...], ch[...], pu[...], cw[0:1, :], cw[1:2, :], cw[2:3, :], pw[...], ps[...],
                              pl.program_id(0))
        ya_ref[...] = ya.astype(bf16)
        yb_ref[...] = yb.astype(bf16)

    col = pl.BlockSpec((s, LANE), lambda j: (0, j))
    return pl.pallas_call(
        body, name=name, grid=(4,), in_specs=_convpool_specs(s, l), out_specs=[col, col],
        out_shape=[jax.ShapeDtypeStruct((s, G), bf16)] * 2, compiler_params=_cp(("parallel",)),
    )(z, z, z, z, conv_w, pool_w, pool_scale)


def _convpool_bwd(name, z, conv_w, pool_w, pool_scale, dy, l):
    s = z.shape[0]

    def body(cb, cc, ch, pu, cw, pw, ps, dya, dyb, dcb, dcc, dch, dpu, dcw, dpw, dps):
        j = pl.program_id(0)
        fn = functools.partial(_convpool_fn, j=j)
        _, vjp = jax.vjp(fn, cb[...], cc[...], ch[...], pu[...], cw[0:1, :], cw[1:2, :], cw[2:3, :], pw[...], ps[...])
        g = vjp((dya[...], dyb[...]))
        dcb[...] = g[0].astype(bf16)
        dcc[...] = g[1].astype(bf16)
        dch[...] = g[2].astype(bf16)
        dpu[...] = g[3].astype(bf16)
        dcw[0:1, :] = g[4]
        dcw[1:2, :] = g[5]
        dcw[2:3, :] = g[6]
        dpw[...] = g[7]
        dps[...] = g[8]

    col = pl.BlockSpec((s, LANE), lambda j: (0, j))
    specs = _convpool_specs(s, l) + [pl.BlockSpec((s, LANE), lambda j: (0, j)),
                                     pl.BlockSpec((s, LANE), lambda j: (0, 4 + j))]
    return pl.pallas_call(
        body, name=name, grid=(4,), in_specs=specs,
        out_specs=[col, col, col, col, pl.BlockSpec((3, LANE), lambda j: (0, j)),
                   pl.BlockSpec((None, LANE, LANE), lambda j: (j, 0, 0)), pl.BlockSpec((1, LANE), lambda j: (0, j))],
        out_shape=[jax.ShapeDtypeStruct((s, G), bf16)] * 4 + [
            jax.ShapeDtypeStruct((3, G), f32), jax.ShapeDtypeStruct((4, LANE, LANE), f32),
            jax.ShapeDtypeStruct((1, G), f32)],
        compiler_params=_cp(("parallel",)),
    )(z, z, z, z, conv_w, pool_w, pool_scale, dy, dy)


def _foxprep_fwd(name, z, bf, l):
    s = z.shape[0]

    def body(m_ref, b_ref, fc_ref, fr_ref):
        fc, fr = _foxprep_fn(m_ref[...], b_ref[...])
        fc_ref[...] = fc
        fr_ref[...] = fr

    return pl.pallas_call(
        body, name=name, grid=(1,),
        in_specs=[pl.BlockSpec((s, LANE), lambda i: (0, MISC)), pl.BlockSpec((None, 1, LANE), lambda i: (l, 0, 0))],
        out_specs=[pl.BlockSpec((s, LANE), lambda i: (0, 0)), pl.BlockSpec((LANE, s), lambda i: (0, 0))],
        out_shape=[jax.ShapeDtypeStruct((s, LANE), f32), jax.ShapeDtypeStruct((LANE, s), f32)],
        compiler_params=_cp(("arbitrary",)),
    )(z, bf)


def _foxprep_bwd(name, z, bf, dfc4, dfr4, dmisc4, l):
    s = z.shape[0]

    def body(m_ref, b_ref, dfc_ref, dfr_ref, dm4_ref, dm_ref, db_ref):
        _, vjp = jax.vjp(_foxprep_fn, m_ref[...], b_ref[...])
        dfc = dfc_ref[0] + dfc_ref[1] + dfc_ref[2] + dfc_ref[3]
        dfr = dfr_ref[0] + dfr_ref[1] + dfr_ref[2] + dfr_ref[3]
        dfr = jnp.concatenate([dfr, jnp.zeros((LANE - 8, s), f32)], axis=0)
        dm, db = vjp((dfc, dfr))
        dm = dm + (dm4_ref[0] + dm4_ref[1] + dm4_ref[2] + dm4_ref[3])
        dm_ref[...] = dm.astype(bf16)
        db_ref[...] = db

    whole = lambda shape: pl.BlockSpec(shape, lambda i: (0,) * len(shape))
    return pl.pallas_call(
        body, name=name, grid=(1,),
        in_specs=[pl.BlockSpec((s, LANE), lambda i: (0, MISC)), pl.BlockSpec((None, 1, LANE), lambda i: (l, 0, 0)),
                  whole((4, s, LANE)), whole((4, 8, s)), whole((4, s, LANE))],
        out_specs=[whole((s, LANE)), whole((1, LANE))],
        out_shape=[jax.ShapeDtypeStruct((s, LANE), bf16), jax.ShapeDtypeStruct((1, LANE), f32)],
        compiler_params=_cp(("arbitrary",)),
    )(z, bf, dfc4, dfr4, dmisc4)


def _fox_specs(s, tq, l):
    return [pl.BlockSpec((tq, LANE), lambda h, i: (i, FQ + h)),
            pl.BlockSpec((s, LANE), lambda h, i: (0, FK + h)),
            pl.BlockSpec((s, LANE), lambda h, i: (0, FV + h)),
            pl.BlockSpec((tq, LANE), lambda h, i: (i, 0)),
            pl.BlockSpec((8, s), lambda h, i: (0, 0)),
            pl.BlockSpec((None, 1, LANE), lambda h, i: (l, 0, 0)),
            pl.BlockSpec((None, 1, LANE), lambda h, i: (l, 0, 0))]


def _fox_fwd(name, z, fc, fr, qg, kg, l):
    s = z.shape[0]
    tq = min(s, 256)

    def body(q, k, v, fc_ref, fr_ref, qg_ref, kg_ref, y_ref):
        y = _fox_fn(q[...], k[...], v[...], fc_ref[...], fr_ref[...], qg_ref[...], kg_ref[...],
                    pl.program_id(0), pl.program_id(1))
        y_ref[...] = y.astype(bf16)

    return pl.pallas_call(
        body, name=name, grid=(4, s // tq), in_specs=_fox_specs(s, tq, l),
        out_specs=pl.BlockSpec((tq, LANE), lambda h, i: (i, h)),
        out_shape=jax.ShapeDtypeStruct((s, G), bf16), compiler_params=_cp(("parallel", "parallel")),
    )(z, z, z, fc, fr, qg, kg)


def _fox_bwd(name, z, fc, fr, qg, kg, dy, l):
    s = z.shape[0]
    tq = min(s, 256)

    def body(q, k, v, fc_ref, fr_ref, qg_ref, kg_ref, dy_ref, dq, dk, dv, dfc, dfr, dqg, dkg):
        h, i = pl.program_id(0), pl.program_id(1)
        fn = functools.partial(_fox_fn, h=h, i=i)
        _, vjp = jax.vjp(fn, q[...], k[...], v[...], fc_ref[...], fr_ref[...], qg_ref[...], kg_ref[...])
        g = vjp(dy_ref[...])
        dq[...] = g[0].astype(bf16)
        dfc[...] = g[3]

        @pl.when(i == 0)
        def _():
            dk[...] = g[1]
            dv[...] = g[2]
            dfr[...] = g[4]
            dqg[...] = g[5]
            dkg[...] = g[6]

        @pl.when(i > 0)
        def _():
            dk[...] += g[1]
            dv[...] += g[2]
            dfr[...] += g[4]
            dqg[...] += g[5]
            dkg[...] += g[6]

    specs = _fox_specs(s, tq, l) + [pl.BlockSpec((tq, LANE), lambda h, i: (i, 12 + h))]
    head = pl.BlockSpec((s, LANE), lambda h, i: (0, h))
    gain = pl.BlockSpec((None, 1, LANE), lambda h, i: (h, 0, 0))
    return pl.pallas_call(
        body, name=name, grid=(4, s // tq), in_specs=specs,
        out_specs=[pl.BlockSpec((tq, LANE), lambda h, i: (i, h)), head, head,
                   pl.BlockSpec((None, tq, LANE), lambda h, i: (h, i, 0)),
                   pl.BlockSpec((None, 8, s), lambda h, i: (h, 0, 0)), gain, gain],
        out_shape=[jax.ShapeDtypeStruct((s, G), bf16), jax.ShapeDtypeStruct((s, G), f32),
                   jax.ShapeDtypeStruct((s, G), f32), jax.ShapeDtypeStruct((4, s, LANE), f32),
                   jax.ShapeDtypeStruct((4, 8, s), f32), jax.ShapeDtypeStruct((4, 1, LANE), f32),
                   jax.ShapeDtypeStruct((4, 1, LANE), f32)],
        compiler_params=_cp(("parallel", "arbitrary")),
    )(z, z, z, fc, fr, qg, kg, dy)


def _gla_ts(s):
    return min(s, 512)


def _gla1_specs(s, ts, l):
    return [pl.BlockSpec((ts, LANE), lambda h, i: (i, GQ + h)),
            pl.BlockSpec((ts, LANE), lambda h, i: (i, GK + h)),
            pl.BlockSpec((ts, LANE), lambda h, i: (i, GV + h)),
            pl.BlockSpec((ts, LANE), lambda h, i: (i, MISC)),
            pl.BlockSpec((None, LANE, LANE), lambda h, i: (l, 0, h)),
            pl.BlockSpec((None, 1, LANE), lambda h, i: (l, 0, h))]


def _gla1_fwd(name, z, wd, bd, l):
    s = z.shape[0]
    ts = _gla_ts(s)
    nb = ts // CHUNK

    def body(q, k, v, m, wd_ref, bd_ref, qe_ref, kd_ref, el_ref, oi_ref):
        qe, kd, el, oi = _gla1_fn(q[...], k[...], v[...], m[...], wd_ref[...], bd_ref[...])
        qe_ref[...] = qe.astype(bf16)
        kd_ref[...] = kd.astype(bf16)
        el_ref[...] = el
        oi_ref[...] = oi

    blk = pl.BlockSpec((ts, LANE), lambda h, i: (i, h))
    return pl.pallas_call(
        body, name=name, grid=(4, s // ts), in_specs=_gla1_specs(s, ts, l),
        out_specs=[blk, blk, pl.BlockSpec((nb, LANE), lambda h, i: (i, h)), blk],
        out_shape=[jax.ShapeDtypeStruct((s, G), bf16), jax.ShapeDtypeStruct((s, G), bf16),
                   jax.ShapeDtypeStruct((s // CHUNK, G), f32), jax.ShapeDtypeStruct((s, G), f32)],
        compiler_params=_cp(("parallel", "parallel")),
    )(z, z, z, z, wd, bd)


def _gla1_bwd(name, z, wd, bd, dqe, dkd, del_, do, dvi, l):
    s = z.shape[0]
    ts = _gla_ts(s)
    nb = ts // CHUNK

    def body(q, k, v, m, wd_ref, bd_ref, dqe_ref, dkd_ref, del_ref, do_ref, dvi_ref, dq, dk, dv, dm, dwd, dbd):
        i = pl.program_id(1)
        _, vjp = jax.vjp(_gla1_fn, q[...], k[...], v[...], m[...], wd_ref[...], bd_ref[...])
        g = vjp((dqe_ref[...], dkd_ref[...], del_ref[...], do_ref[...]))
        dq[...] = g[0].astype(bf16)
        dk[...] = g[1].astype(bf16)
        dv[...] = (g[2] + dvi_ref[...]).astype(bf16)
        dm[...] = g[3]

        @pl.when(i == 0)
        def _():
            dwd[...] = g[4]
            dbd[...] = g[5]

        @pl.when(i > 0)
        def _():
            dwd[...] += g[4]
            dbd[...] += g[5]

    blk = pl.BlockSpec((ts, LANE), lambda h, i: (i, h))
    specs = _gla1_specs(s, ts, l) + [blk, blk, pl.BlockSpec((nb, LANE), lambda h, i: (i, h)), blk, blk]
    return pl.pallas_call(
        body, name=name, grid=(4, s // ts), in_specs=specs,
        out_specs=[blk, blk, blk, pl.BlockSpec((None, ts, LANE), lambda h, i: (h, i, 0)),
                   pl.BlockSpec((None, LANE, LANE), lambda h, i: (h, 0, 0)),
                   pl.BlockSpec((None, 1, LANE), lambda h, i: (h, 0, 0))],
        out_shape=[jax.ShapeDtypeStruct((s, G), bf16)] * 3 + [
            jax.ShapeDtypeStruct((4, s, LANE), f32), jax.ShapeDtypeStruct((4, LANE, LANE), f32),
            jax.ShapeDtypeStruct((4, 1, LANE), f32)],
        compiler_params=_cp(("parallel", "arbitrary")),
    )(z, z, z, z, wd, bd, dqe, dkd, del_, do, dvi)


def _gla2_fwd(name, z, qe, kd, el, oi):
    s = z.shape[0]
    n = s // CHUNK

    def body(v_ref, qe_ref, kd_ref, el_ref, oi_ref, o_ref, st_ref, cur):
        cur[...] = jnp.zeros_like(cur)

        def step(c, carry):
            rows = pl.ds(pl.multiple_of(c * CHUNK, CHUNK), CHUNK)
            st = cur[...]
            st_ref[c] = st
            o_ref[rows, :] = oi_ref[rows, :] + _dot(qe_ref[rows, :], st, ((1,), (1,)))
            cur[...] = st * el_ref[pl.ds(c, 1), :] + _dot(v_ref[rows, :], kd_ref[rows, :], ((0,), (0,)))
            return carry

        lax.fori_loop(0, n, step, 0)

    head = pl.BlockSpec((s, LANE), lambda h: (0, h))
    return pl.pallas_call(
        body, name=name, grid=(4,),
        in_specs=[pl.BlockSpec((s, LANE), lambda h: (0, GV + h)), head, head,
                  pl.BlockSpec((n, LANE), lambda h: (0, h)), head],
        out_specs=[head, pl.BlockSpec((None, n, LANE, LANE), lambda h: (h, 0, 0, 0))],
        out_shape=[jax.ShapeDtypeStruct((s, G), f32), jax.ShapeDtypeStruct((4, n, LANE, LANE), f32)],
        scratch_shapes=[pltpu.VMEM((LANE, LANE), f32)],
        compiler_params=_cp(("parallel",)),
    )(z, qe, kd, el, oi)


def _gla2_bwd(name, z, qe, kd, el, st, do):
    s = z.shape[0]
    n = s // CHUNK

    def body(v_ref, qe_ref, kd_ref, el_ref, st_ref, do_ref, dqe_ref, dkd_ref, dv_ref, del_ref, dcur):
        dcur[...] = jnp.zeros_like(dcur)

        def step(t, carry):
            c = n - 1 - t
            rows = pl.ds(pl.multiple_of(c * CHUNK, CHUNK), CHUNK)
            dn = dcur[...]
            stc = st_ref[c]
            doc = do_ref[rows, :]
            dqe_ref[rows, :] = _dot(doc, stc)
            dv_ref[rows, :] = _dot(kd_ref[rows, :], dn, ((1,), (1,)))
            dkd_ref[rows, :] = _dot(v_ref[rows, :], dn)
            del_ref[pl.ds(c, 1), :] = jnp.sum(stc * dn, axis=0, keepdims=True)
            dcur[...] = dn * el_ref[pl.ds(c, 1), :] + _dot(doc, qe_ref[rows, :], ((0,), (0,)))
            return carry

        lax.fori_loop(0, n, step, 0)

    head = pl.BlockSpec((s, LANE), lambda h: (0, h))
    chunk = pl.BlockSpec((n, LANE), lambda h: (0, h))
    return pl.pallas_call(
        body, name=name, grid=(4,),
        in_specs=[pl.BlockSpec((s, LANE), lambda h: (0, GV + h)), head, head, chunk,
                  pl.BlockSpec((None, n, LANE, LANE), lambda h: (h, 0, 0, 0)), head],
        out_specs=[head, head, head, chunk],
        out_shape=[jax.ShapeDtypeStruct((s, G), f32)] * 3 + [jax.ShapeDtypeStruct((n, G), f32)],
        scratch_shapes=[pltpu.VMEM((LANE, LANE), f32)],
        compiler_params=_cp(("parallel",)),
    )(z, qe, kd, el, st, do)


def _gla3_specs(ts, l):
    return [pl.BlockSpec((ts, LANE), lambda h, i: (i, h)),
            pl.BlockSpec((ts, LANE), lambda h, i: (i, GG + h)),
            pl.BlockSpec((None, 1, LANE), lambda h, i: (l, 0, 0))]


def _gla3_fwd(name, o, z, og, l):
    s = z.shape[0]
    ts = _gla_ts(s)

    def body(o_ref, g_ref, og_ref, y_ref):
        y_ref[...] = _gla3_fn(o_ref[...], g_ref[...], og_ref[...]).astype(bf16)

    return pl.pallas_call(
        body, name=name, grid=(4, s // ts), in_specs=_gla3_specs(ts, l),
        out_specs=pl.BlockSpec((ts, LANE), lambda h, i: (i, h)),
        out_shape=jax.ShapeDtypeStruct((s, G), bf16), compiler_params=_cp(("parallel", "parallel")),
    )(o, z, og)


def _gla3_bwd(name, o, z, og, dy, l):
    s = z.shape[0]
    ts = _gla_ts(s)

    def body(o_ref, g_ref, og_ref, dy_ref, do_ref, dg_ref, dog_ref):
        i = pl.program_id(1)
        _, vjp = jax.vjp(_gla3_fn, o_ref[...], g_ref[...], og_ref[...])
        g = vjp(dy_ref[...])
        do_ref[...] = g[0]
        dg_ref[...] = g[1].astype(bf16)

        @pl.when(i == 0)
        def _():
            dog_ref[...] = g[2]

        @pl.when(i > 0)
        def _():
            dog_ref[...] += g[2]

    blk = pl.BlockSpec((ts, LANE), lambda h, i: (i, h))
    return pl.pallas_call(
        body, name=name, grid=(4, s // ts),
        in_specs=_gla3_specs(ts, l) + [pl.BlockSpec((ts, LANE), lambda h, i: (i, 8 + h))],
        out_specs=[blk, blk, pl.BlockSpec((None, 1, LANE), lambda h, i: (h, 0, 0))],
        out_shape=[jax.ShapeDtypeStruct((s, G), f32), jax.ShapeDtypeStruct((s, G), bf16),
                   jax.ShapeDtypeStruct((4, 1, LANE), f32)],
        compiler_params=_cp(("parallel", "arbitrary")),
    )(o, z, og, dy)


def _layer_fwd(l, h, p, weights):
    s = h.shape[0]
    tm = _tm(s)
    n = f"l{l}_"
    w_in = weights(l, "w_in", h)
    u = _rmsnorm_fwd(n + "norm_mix", h, p["g_mix"], l)
    z = _mm(n + "mm_in", u, w_in,
            pl.BlockSpec((tm, D), lambda j, i, k: (i, 0)), pl.BlockSpec((D, 1152), lambda j, i, k: (0, j)),
            jax.ShapeDtypeStruct((s, ZC), f32), pl.BlockSpec((tm, 1152), lambda j, i, k: (i, j)),
            (ZC // 1152, s // tm, 1), ((1,), (0,)), 1)
    w_out = weights(l, "w_out", z)
    ya, yb = _convpool_fwd(n + "convpool", z, p["conv_w"], p["pool_w"], p["pool_scale"], l)
    qe, kd, el, oi = _gla1_fwd(n + "gla_chunk", z, p["wdec"], p["bdec"], l)
    o, st = _gla2_fwd(n + "gla_scan", z, qe, kd, el, oi)
    yc = _gla3_fwd(n + "gla_out", o, z, p["gla_og"], l)
    fc, fr = _foxprep_fwd(n + "fox_prep", z, p["fox_bf"], l)
    yd = _fox_fwd(n + "fox_attn", z, fc, fr, p["fox_qg"], p["fox_kg"], l)
    y = jnp.concatenate([ya, yb, yc, yd], axis=1)
    w_gate, w_up, w_down = weights(l, "ffn", y)
    h1 = _mm(n + "mm_out", y, w_out,
             pl.BlockSpec((tm, G), lambda j, i, k: (i, k)),
             pl.BlockSpec((None, G, 1024), lambda j, i, k: (k, 0, j)),
             jax.ShapeDtypeStruct((s, D), f32), pl.BlockSpec((tm, 1024), lambda j, i, k: (i, j)),
             (2, s // tm, NCHIP), ((1,), (0,)), NCHIP, res=h, res_spec=pl.BlockSpec((tm, 1024), lambda j, i, k: (i, j)))
    u2 = _rmsnorm_fwd(n + "norm_ffn", h1, p["g_ffn"], l)
    ffn_in = (pl.BlockSpec((tm, D), lambda j, i, k: (i, 0)),
              pl.BlockSpec((None, D, FB), lambda j, i, k: (j, 0, 0)))
    ffn_out = (jax.ShapeDtypeStruct((s, DFF), f32), pl.BlockSpec((tm, FB), lambda j, i, k: (i, j)))
    gate = _mm(n + "mm_gate", u2, w_gate, *ffn_in, *ffn_out, (NCHIP, s // tm, 1), ((1,), (0,)), 1)
    up = _mm(n + "mm_up", u2, w_up, *ffn_in, *ffn_out, (NCHIP, s // tm, 1), ((1,), (0,)), 1)
    act = _ffn_act(n + "ffn_act", gate, up)
    h2 = _mm(n + "mm_down", act, w_down,
             pl.BlockSpec((tm, FB), lambda j, i, k: (i, k)),
             pl.BlockSpec((None, FB, 1024), lambda j, i, k: (k, 0, j)),
             jax.ShapeDtypeStruct((s, D), f32), pl.BlockSpec((tm, 1024), lambda j, i, k: (i, j)),
             (2, s // tm, NCHIP), ((1,), (0,)), NCHIP, res=h1, res_spec=pl.BlockSpec((tm, 1024), lambda j, i, k: (i, j)))
    saved = dict(h=h, u=u, z=z, qe=qe, kd=kd, el=el, st=st, o=o, fc=fc, fr=fr, y=y, h1=h1, u2=u2,
                 gate=gate, up=up, act=act, w_in=w_in, w_out=w_out, w_gate=w_gate, w_up=w_up, w_down=w_down)
    return h2, saved


def _mm_tn(name, a, b, ta, tb, out_shape, out_spec, grid):
    s = a.shape[0]
    return _mm(name, a, b, pl.BlockSpec((s, ta), lambda i, j, k: (0, i)), pl.BlockSpec((s, tb), lambda i, j, k: (0, j)),
               out_shape, out_spec, grid, ((0,), (0,)), 1)


def _layer_bwd_ffn(l, dh2, dh2b, p, sv, ship):
    s = dh2.shape[0]
    tm = _tm(s)
    n = f"l{l}_bwd_"
    dact = _mm(n + "mm_dact", dh2b, sv["w_down"],
               pl.BlockSpec((tm, D), lambda j, i, k: (i, 0)),
               pl.BlockSpec((None, FB, D), lambda j, i, k: (j, 0, 0)),
               jax.ShapeDtypeStruct((s, DFF), f32), pl.BlockSpec((tm, FB), lambda j, i, k: (i, j)),
               (NCHIP, s // tm, 1), ((1,), (1,)), 1)
    g_wd = _mm_tn(n + "mm_dwd", sv["act"], dh2b, FB, 1024, jax.ShapeDtypeStruct((NCHIP, FB, D), bf16),
                  pl.BlockSpec((None, FB, 1024), lambda i, j, k: (i, 0, j)), (NCHIP, 2, 1))
    dgate, dup = _ffn_act_bwd(n + "ffn_act", sv["gate"], sv["up"], dact)
    nt_in = lambda: (pl.BlockSpec((tm, FB), lambda j, i, k: (i, k)),
                     pl.BlockSpec((None, 1024, FB), lambda j, i, k: (k, j, 0)))
    nt_out = lambda: (jax.ShapeDtypeStruct((s, D), f32), pl.BlockSpec((tm, 1024), lambda j, i, k: (i, j)))
    du2 = _mm(n + "mm_du2_gate", dgate, sv["w_gate"], *nt_in(), *nt_out(), (2, s // tm, NCHIP), ((1,), (1,)), NCHIP)
    du2 = _mm(n + "mm_du2_up", dup, sv["w_up"], *nt_in(), *nt_out(), (2, s // tm, NCHIP), ((1,), (1,)), NCHIP,
              res=du2, res_spec=pl.BlockSpec((tm, 1024), lambda j, i, k: (i, j)))
    wg_shape = jax.ShapeDtypeStruct((NCHIP, D, FB), bf16)
    wg_spec = lambda: pl.BlockSpec((None, 1024, FB), lambda i, j, k: (j, i, 0))
    g_wg = _mm_tn(n + "mm_dwg", sv["u2"], dgate, 1024, FB, wg_shape, wg_spec(), (2, NCHIP, 1))
    g_wu = _mm_tn(n + "mm_dwu", sv["u2"], dup, 1024, FB, wg_shape, wg_spec(), (2, NCHIP, 1))
    dh1, dh1b, dg_ffn = _rmsnorm_bwd(n + "norm_ffn", sv["h1"], p["g_ffn"], du2, dh2, l)
    g_wo = _mm_tn(n + "mm_dwo", sv["y"], dh1b, G, 1024, jax.ShapeDtypeStruct((NCHIP, G, D), bf16),
                  pl.BlockSpec((None, G, 1024), lambda i, j, k: (i, 0, j)), (NCHIP, 2, 1))
    token = ship(dict(w_out=g_wo, w_gate=g_wg, w_up=g_wu, w_down=g_wd))
    dy = _mm(n + "mm_dy", dh1b, sv["w_out"],
             pl.BlockSpec((tm, D), lambda j, i, k: (i, 0)),
             pl.BlockSpec((None, G, D), lambda j, i, k: (j, 0, 0)),
             jax.ShapeDtypeStruct((s, D), f32), pl.BlockSpec((tm, G), lambda j, i, k: (i, j)),
             (NCHIP, s // tm, 1), ((1,), (1,)), 1, dep=token)
    return dh1, dy, dg_ffn


def _layer_bwd_mix(l, dh1, dy, p, sv, ship):
    s = dh1.shape[0]
    tm = _tm(s)
    n = f"l{l}_bwd_"
    z = sv["z"]
    dcb, dcc, dch, dpu, dconv, dpoolw, dpools = _convpool_bwd(
        n + "convpool", z, p["conv_w"], p["pool_w"], p["pool_scale"], dy, l)
    do, dgg, dog = _gla3_bwd(n + "gla_out", sv["o"], z, p["gla_og"], dy, l)
    dqe, dkd, dvi, del_ = _gla2_bwd(n + "gla_scan", z, sv["qe"], sv["kd"], sv["el"], sv["st"], do)
    dgq, dgk, dgv, dmisc4, dwd, dbd = _gla1_bwd(n + "gla_chunk", z, p["wdec"], p["bdec"], dqe, dkd, del_, do, dvi, l)
    dfq, dfk, dfv, dfc4, dfr4, dqg, dkg = _fox_bwd(n + "fox_attn", z, sv["fc"], sv["fr"], p["fox_qg"], p["fox_kg"], dy, l)
    dmisc, dbf = _foxprep_bwd(n + "fox_prep", z, p["fox_bf"], dfc4, dfr4, dmisc4, l)
    dz = jnp.concatenate([dcb, dcc, dch, dpu, dgq, dgk, dgv, dgg, dfq, dfk.astype(bf16), dfv.astype(bf16), dmisc],
                         axis=1)
    g_wi = _mm_tn(n + "mm_dwi", sv["u"], dz, 1024, 1152, jax.ShapeDtypeStruct((D, ZC), bf16),
                  pl.BlockSpec((1024, 1152), lambda i, j, k: (i, j)), (2, ZC // 1152, 1))
    token = ship(dict(w_in=_win_to_blocks(g_wi)))
    du = _mm(n + "mm_du", dz, sv["w_in"],
             pl.BlockSpec((tm, 1152), lambda j, i, k: (i, k)), pl.BlockSpec((1024, 1152), lambda j, i, k: (j, k)),
             jax.ShapeDtypeStruct((s, D), f32), pl.BlockSpec((tm, 1024), lambda j, i, k: (i, j)),
             (2, s // tm, ZC // 1152), ((1,), (1,)), ZC // 1152, dep=token)
    dh, dhb, dg_mix = _rmsnorm_bwd(n + "norm_mix", sv["h"], p["g_mix"], du, dh1, l)
    small = dict(
        norm_mix_g=dg_mix[0], conv_w=dconv, pool_w=dpoolw, pool_scale=dpools[0],
        gla_w_decay=jnp.concatenate([dwd[hh, GA_LANE:GA_LANE + 16, :64] for hh in range(4)], axis=1),
        gla_b_decay=jnp.concatenate([dbd[hh, 0, :64] for hh in range(4)]),
        gla_out_g=jnp.sum(dog[:, 0, :], axis=0), fox_q_g=jnp.sum(dqg[:, 0, :], axis=0),
        fox_k_g=jnp.sum(dkg[:, 0, :], axis=0), fox_b_f=dbf[0, FF_LANE:FF_LANE + 4])
    return dh, dhb, small


def _win_from_blocks(wb):
    def cols(a, b):
        parts = []
        for kk in range(NCHIP):
            lo, hi = max(a, kk * WINB), min(b, (kk + 1) * WINB)
            if lo < hi:
                parts.append(wb[kk, :, lo - kk * WINB:hi - kk * WINB])
        return parts

    zeros = lambda w: [jnp.zeros((wb.shape[1], w), wb.dtype)]
    segs = cols(0, 2048)
    for hh in range(4):
        segs += cols(2048 + 64 * hh, 2112 + 64 * hh) + zeros(64)
    for hh in range(4):
        segs += cols(2304 + 64 * hh, 2368 + 64 * hh) + zeros(64)
    segs += cols(2560, 3584) + cols(3600, 5136)
    segs += cols(5136, 5140) + zeros(GA_LANE - 4) + cols(3584, 3600) + zeros(LANE - GA_LANE - 16)
    return jnp.concatenate(segs, axis=-1)


def _win_to_blocks(g):
    mb = MISC * LANE
    parts = [g[:, 0:2048]]
    parts += [g[:, GQ * LANE + LANE * hh:GQ * LANE + LANE * hh + 64] for hh in range(4)]
    parts += [g[:, GK * LANE + LANE * hh:GK * LANE + LANE * hh + 64] for hh in range(4)]
    parts += [g[:, GV * LANE:FQ * LANE], g[:, mb + GA_LANE:mb + GA_LANE + 16], g[:, FQ * LANE:mb],
              g[:, mb + FF_LANE:mb + FF_LANE + 4]]
    full = jnp.concatenate(parts, axis=1)
    return full.reshape(D, NCHIP, WINB).transpose(1, 0, 2)


def _place():
    x, y, c = lax.axis_index("x"), lax.axis_index("y"), lax.axis_index("c")
    chips = [(1 - x, y), (x, 1 - y), (1 - x, 1 - y)]
    return x, y, c, chips


def _allgather_small(name, v):
    m_per, n = v.shape

    def body(x_ref, out_ref, send_sems, recv_sems, local_sem):
        x, y, c, chips = _place()
        me, sibling = (x, y, c), (x, y, 1 - c)

        def rows(px, py, pc):
            return out_ref.at[pl.ds((4 * px + 2 * py + pc) * m_per, m_per), :]

        def copy(k, block, to, src=None):
            return pltpu.make_async_remote_copy(
                src_ref=rows(*block) if src is None else src, dst_ref=rows(*block),
                send_sem=send_sems.at[k], recv_sem=recv_sems.at[k], device_id=to, device_id_type=MESH)

        mine = pltpu.make_async_copy(x_ref, rows(*me), local_sem)
        mine.start()
        first = [copy(0, me, sibling, src=x_ref)]
        first += [copy(1 + j, me, (*chip, c), src=x_ref) for j, chip in enumerate(chips)]
        for cp in first:
            cp.start()
        passed = [copy(4 + j, (*chip, c), sibling) for j, chip in enumerate(chips)]
        for j, chip in enumerate(chips):
            copy(1 + j, (*chip, c), me).wait_recv()
            passed[j].start()
        copy(0, sibling, me).wait_recv()
        for j, chip in enumerate(chips):
            copy(4 + j, (*chip, 1 - c), me).wait_recv()
        for cp in first + passed:
            cp.wait_send()
        mine.wait()

    return pl.pallas_call(
        body, name=name, out_shape=jax.ShapeDtypeStruct((8 * m_per, n), v.dtype),
        in_specs=[pl.BlockSpec(memory_space=pltpu.VMEM)], out_specs=pl.BlockSpec(memory_space=pltpu.VMEM),
        scratch_shapes=[pltpu.SemaphoreType.DMA((7,)), pltpu.SemaphoreType.DMA((7,)), pltpu.SemaphoreType.DMA],
    )(v)


def _hbm_specs(n):
    return [pl.BlockSpec(memory_space=pl.ANY)] * n


def _own_slot(shard, chip):
    return lax.dynamic_update_index_in_dim(lax.empty((NCHIP,) + shard.shape, shard.dtype), shard, chip, 0)


HBM = pl.BlockSpec(memory_space=pltpu.HBM)
SEM = pl.BlockSpec(memory_space=pltpu.SEMAPHORE)
EFFECT = pltpu.SideEffectType.DATAFLOW_SIDE_EFFECTING


def _in_hbm(v):
    return pltpu.with_memory_space_constraint(v, pltpu.HBM)


def _gather_start(name, groups):
    flat = [b for g in groups for b in g]
    nt, ng = len(flat), len(groups)

    def body(*refs):
        outs = refs[nt:]
        sems, bufs = outs[:2 * ng], outs[2 * ng:2 * ng + nt]
        x, y, c, chips = _place()
        me = 2 * x + y
        t = 0
        for gi, g in enumerate(groups):
            for k in range(len(g)):
                r = bufs[t].shape[1] // 2
                mine = bufs[t].at[me, pl.ds(c * r, r), :]
                for j, (px, py) in enumerate(chips):
                    pltpu.make_async_remote_copy(
                        src_ref=mine, dst_ref=mine, send_sem=sems[2 * gi].at[3 * k + j], recv_sem=sems[2 * gi + 1].at[3 * k + j],
                        device_id=(px, py, c), device_id_type=MESH).start()
                t += 1

    sem_shapes = []
    for g in groups:
        sem_shapes += [pltpu.SemaphoreType.DMA((3 * len(g),))] * 2
    out = pl.pallas_call(
        body, name=name,
        out_shape=tuple(sem_shapes + [pltpu.HBM(b.shape, b.dtype) for b in flat]),
        in_specs=tuple([HBM] * nt), out_specs=tuple([SEM] * (2 * ng) + [HBM] * nt),
        input_output_aliases={t: 2 * ng + t for t in range(nt)},
        compiler_params=pltpu.CompilerParams(has_side_effects=EFFECT),
    )(*[_in_hbm(b) for b in flat])
    sems = [(out[2 * gi], out[2 * gi + 1]) for gi in range(ng)]
    bufs, at = [], 2 * ng
    for g in groups:
        bufs.append(list(out[at:at + len(g)]))
        at += len(g)
    return sems, bufs


def _gather_wait(name, bufs, send_sems, recv_sems, after):
    nt = len(bufs)

    def body(*refs):
        ins, ss, rs = refs[:nt], refs[nt], refs[nt + 1]
        x, y, c, chips = _place()
        me = 2 * x + y
        for k in range(nt):
            r = ins[k].shape[1] // 2
            for j, (px, py) in enumerate(chips):
                cp = pltpu.make_async_remote_copy(
                    src_ref=ins[k].at[me, pl.ds(c * r, r), :], dst_ref=ins[k].at[2 * px + py, pl.ds(c * r, r), :],
                    send_sem=ss.at[3 * k + j], recv_sem=rs.at[3 * k + j], device_id=(px, py, c), device_id_type=MESH)
                cp.wait_send()
                cp.wait_recv()

    out = pl.pallas_call(
        body, name=name, out_shape=tuple(pltpu.HBM(b.shape, b.dtype) for b in bufs),
        in_specs=tuple([HBM] * nt + [SEM, SEM, pl.BlockSpec(memory_space=pl.ANY)]), out_specs=tuple([HBM] * nt),
        input_output_aliases={t: t for t in range(nt)},
        compiler_params=pltpu.CompilerParams(has_side_effects=EFFECT),
    )(*bufs, send_sems, recv_sems, after)
    return list(out)


def _gather_exchange(name, bufs):
    nt = len(bufs)

    def body(*refs):
        outs = refs[nt:2 * nt]
        send_sems, recv_sems = refs[2 * nt:]
        x, y, c, chips = _place()
        sibling = (x, y, 1 - c)

        def half(t, chip_idx, cc):
            r = outs[t].shape[1] // 2
            return outs[t].at[chip_idx, pl.ds(cc * r, r), :]

        sent = []
        for t in range(nt):
            for j, (px, py) in enumerate(chips):
                cp = pltpu.make_async_remote_copy(
                    src_ref=half(t, 2 * px + py, c), dst_ref=half(t, 2 * px + py, c),
                    send_sem=send_sems.at[t, j], recv_sem=recv_sems.at[t, j], device_id=sibling, device_id_type=MESH)
                cp.start()
                sent.append(cp)
        for t in range(nt):
            for j, (px, py) in enumerate(chips):
                pltpu.make_async_remote_copy(
                    src_ref=half(t, 2 * px + py, 1 - c), dst_ref=half(t, 2 * px + py, 1 - c),
                    send_sem=send_sems.at[t, j], recv_sem=recv_sems.at[t, j], device_id=sibling,
                    device_id_type=MESH).wait_recv()
        for cp in sent:
            cp.wait_send()

    return pl.pallas_call(
        body, name=name, out_shape=[jax.ShapeDtypeStruct(v.shape, v.dtype) for v in bufs],
        in_specs=_hbm_specs(nt), out_specs=_hbm_specs(nt), input_output_aliases={t: t for t in range(nt)},
        scratch_shapes=[pltpu.SemaphoreType.DMA((nt, 3)), pltpu.SemaphoreType.DMA((nt, 3))],
    )(*bufs)


def _rs_to_sibling(name, grads):
    nt = len(grads)

    def body(*refs):
        ins, outs = refs[:nt], refs[nt:2 * nt]
        send_sems, recv_sems = refs[2 * nt:]
        x, y, c, _ = _place()
        cps = []
        for t in range(nt):
            r = ins[t].shape[1] // 2
            cp = pltpu.make_async_remote_copy(
                src_ref=ins[t].at[:, pl.ds((1 - c) * r, r), :], dst_ref=outs[t],
                send_sem=send_sems.at[t], recv_sem=recv_sems.at[t], device_id=(x, y, 1 - c), device_id_type=MESH)
            cp.start()
            cps.append(cp)
        for cp in cps:
            cp.wait()

    return pl.pallas_call(
        body, name=name,
        out_shape=[jax.ShapeDtypeStruct((NCHIP, g.shape[1] // 2, g.shape[2]), g.dtype) for g in grads],
        in_specs=_hbm_specs(nt), out_specs=_hbm_specs(nt),
        scratch_shapes=[pltpu.SemaphoreType.DMA((nt,)), pltpu.SemaphoreType.DMA((nt,))],
    )(*grads)


def _rs_pair_sum(name, pos, g, other):
    r, cdim = other.shape[1], other.shape[2]
    tr = r // 4 if (r // 4) % 16 == 0 else r // 2
    nblk = r // tr

    def body(pos_ref, g_ref, o_ref, s_ref):
        s_ref[...] = (g_ref[...].astype(f32) + o_ref[...].astype(f32)).astype(bf16)

    blk = pl.BlockSpec((None, tr, cdim), lambda q, i, p: (q, i, 0))
    return pl.pallas_call(
        body, name=name, out_shape=jax.ShapeDtypeStruct(other.shape, bf16),
        grid_spec=pltpu.PrefetchScalarGridSpec(
            num_scalar_prefetch=1, grid=(NCHIP, nblk),
            in_specs=[pl.BlockSpec((None, tr, cdim), lambda q, i, p: (q, p[1] * nblk + i, 0)), blk], out_specs=blk),
        compiler_params=_cp(("parallel", "parallel")),
    )(pos, g, other)


def _scatter_start(name, sums):
    nt = len(sums)

    def body(*refs):
        outs = refs[2 * nt:]
        ss, rs, src, land, token = outs[0], outs[1], outs[2:2 + nt], outs[2 + nt:2 + 2 * nt], outs[2 + 2 * nt]
        x, y, c, chips = _place()
        me = 2 * x + y
        for t in range(nt):
            for j, (px, py) in enumerate(chips):
                pltpu.make_async_remote_copy(
                    src_ref=src[t].at[2 * px + py], dst_ref=land[t].at[me], send_sem=ss.at[3 * t + j], recv_sem=rs.at[3 * t + j],
                    device_id=(px, py, c), device_id_type=MESH).start()
        token[...] = jnp.zeros_like(token)

    shapes = [pltpu.HBM(v.shape, v.dtype) for v in sums]
    out = pl.pallas_call(
        body, name=name,
        out_shape=tuple([pltpu.SemaphoreType.DMA((3 * nt,))] * 2 + shapes + shapes + [jax.ShapeDtypeStruct((8, LANE), f32)]),
        in_specs=tuple([HBM] * (2 * nt)),
        out_specs=tuple([SEM, SEM] + [HBM] * (2 * nt) + [pl.BlockSpec(memory_space=pltpu.VMEM)]),
        input_output_aliases={t: 2 + t for t in range(2 * nt)},
        compiler_params=pltpu.CompilerParams(has_side_effects=EFFECT),
    )(*[_in_hbm(v) for v in sums], *[_in_hbm(lax.empty(v.shape, v.dtype)) for v in sums])
    return out[0], out[1], list(out[2:2 + nt]), list(out[2 + nt:2 + 2 * nt]), out[2 + 2 * nt]


def _scatter_wait(name, sums, land, send_sems, recv_sems, after):
    nt = len(sums)

    def body(*refs):
        src, dst, ss, rs = refs[:nt], refs[nt:2 * nt], refs[2 * nt], refs[2 * nt + 1]
        x, y, c, chips = _place()
        for t in range(nt):
            for j, (px, py) in enumerate(chips):
                cp = pltpu.make_async_remote_copy(
                    src_ref=src[t].at[2 * px + py], dst_ref=dst[t].at[2 * px + py], send_sem=ss.at[3 * t + j],
                    recv_sem=rs.at[3 * t + j], device_id=(px, py, c), device_id_type=MESH)
                cp.wait_send()
                cp.wait_recv()

    shapes = [pltpu.HBM(v.shape, v.dtype) for v in sums]
    out = pl.pallas_call(
        body, name=name, out_shape=tuple(shapes + shapes),
        in_specs=tuple([HBM] * (2 * nt) + [SEM, SEM, pl.BlockSpec(memory_space=pl.ANY)]),
        out_specs=tuple([HBM] * (2 * nt)), input_output_aliases={t: t for t in range(2 * nt)},
        compiler_params=pltpu.CompilerParams(has_side_effects=EFFECT),
    )(*sums, *land, send_sems, recv_sems, after)
    return list(out[:nt]), list(out[nt:])


def _rs_chip_sum(name, pos, sums, parts):
    r, cdim = parts.shape[1], parts.shape[2]
    tr = r // 4 if (r // 4) % 16 == 0 else r // 2
    nblk = r // tr

    def body(pos_ref, own_ref, a_ref, b_ref, c_ref, o_ref):
        o_ref[...] = ((own_ref[...].astype(f32) + a_ref[...].astype(f32)) + b_ref[...].astype(f32)) \
            + c_ref[...].astype(f32)

    def slot(k):
        return pl.BlockSpec((None, tr, cdim), lambda i, p: ((p[0] + k) % NCHIP, i, 0))

    return pl.pallas_call(
        body, name=name, out_shape=jax.ShapeDtypeStruct((2 * r, cdim), f32),
        grid_spec=pltpu.PrefetchScalarGridSpec(
            num_scalar_prefetch=1, grid=(nblk,), in_specs=[slot(0), slot(1), slot(2), slot(3)],
            out_specs=pl.BlockSpec((tr, cdim), lambda i, p: (p[1] * nblk + i, 0))),
        compiler_params=_cp(("parallel",)),
    )(pos, sums, parts, parts, parts)


def _rs_share_halves(name, bufs):
    nt = len(bufs)

    def body(*refs):
        outs = refs[nt:2 * nt]
        send_sems, recv_sems = refs[2 * nt:]
        x, y, c, _ = _place()
        cps = []
        for t in range(nt):
            r = outs[t].shape[0] // 2
            mine = outs[t].at[pl.ds(c * r, r), :]
            theirs = outs[t].at[pl.ds((1 - c) * r, r), :]
            cp = pltpu.make_async_remote_copy(
                src_ref=mine, dst_ref=mine, send_sem=send_sems.at[t], recv_sem=recv_sems.at[t],
                device_id=(x, y, 1 - c), device_id_type=MESH)
            cp.start()
            cps.append((cp, theirs))
        for t, (cp, theirs) in enumerate(cps):
            pltpu.make_async_remote_copy(
                src_ref=theirs, dst_ref=theirs, send_sem=send_sems.at[t], recv_sem=recv_sems.at[t],
                device_id=(x, y, 1 - c), device_id_type=MESH).wait_recv()
            cp.wait_send()

    return pl.pallas_call(
        body, name=name, out_shape=[jax.ShapeDtypeStruct(v.shape, v.dtype) for v in bufs],
        in_specs=_hbm_specs(nt), out_specs=_hbm_specs(nt), input_output_aliases={t: t for t in range(nt)},
        scratch_shapes=[pltpu.SemaphoreType.DMA((nt,)), pltpu.SemaphoreType.DMA((nt,))],
    )(*bufs)


BIG = ("w_in", "w_out", "w_gate", "w_up", "w_down")


GROUPS = (("w_in",), ("w_out",), ("w_gate", "w_up", "w_down"))


def _rs_begin(tag, keys, grads, pos):
    got = _rs_to_sibling(tag + "to_sibling", grads)
    sums = [_rs_pair_sum(tag + "pair_sum_" + k, pos, g, o) for k, g, o in zip(keys, grads, got)]
    return (keys,) + _scatter_start(tag + "start", sums)


def _rs_end(tag, state, pos, after):
    keys, send_sems, recv_sems, sums, land, _ = state
    sums, land = _scatter_wait(tag + "wait", sums, land, send_sems, recv_sems, after)
    halves = [_rs_chip_sum(tag + "chip_sum_" + k, pos, s, v) for k, s, v in zip(keys, sums, land)]
    return dict(zip(keys, _rs_share_halves(tag + "share", halves)))


def _adam_math(w, g, m, v):
    m = ADAM_B1 * m + (1.0 - ADAM_B1) * g
    v = ADAM_B2 * v + (1.0 - ADAM_B2) * (g * g)
    m_hat = m / (1.0 - ADAM_B1 ** ADAM_STEP)
    v_hat = v / (1.0 - ADAM_B2 ** ADAM_STEP)
    delta = -ADAM_LR * (m_hat / (jnp.sqrt(v_hat) + ADAM_EPS) + ADAM_WD * w)
    return delta, m, v


def _adam_big(name, g0, g1, w, m, v):
    _, r, cdim = w.shape
    tr = 128 if r % 128 == 0 else 64
    nb = r // tr

    def body(g0_ref, g1_ref, w_ref, m_ref, v_ref, go_ref, d_ref, mo_ref, vo_ref):
        l = pl.program_id(0)
        g = jnp.where(l == 0, g0_ref[...], g1_ref[...])
        delta, mn, vn = _adam_math(w_ref[...], g, m_ref[...], v_ref[...])
        go_ref[...] = g
        d_ref[...] = delta
        mo_ref[...] = mn
        vo_ref[...] = vn

    lay = pl.BlockSpec((None, tr, cdim), lambda l, i: (l, i, 0))
    return pl.pallas_call(
        body, name=name, grid=(2, nb),
        in_specs=[pl.BlockSpec((tr, cdim), lambda l, i: (i * (1 - l) + (nb - 1) * l, 0)),
                  pl.BlockSpec((tr, cdim), lambda l, i: (i * l, 0)), lay, lay, lay],
        out_specs=[lay] * 4, out_shape=[jax.ShapeDtypeStruct(w.shape, f32)] * 4,
        compiler_params=_cp(("arbitrary", "arbitrary")),
    )(g0, g1, w, m, v)


def _sum8(name, gathered):
    m_per = gathered.shape[0] // 8

    def body(g_ref, o_ref):
        tot = g_ref[pl.ds(0, m_per), :]
        for d in range(1, 8):
            tot = tot + g_ref[pl.ds(d * m_per, m_per), :]
        o_ref[...] = tot

    return pl.pallas_call(body, name=name, out_shape=jax.ShapeDtypeStruct((m_per, LANE), f32))(gathered)


def _adam_small(name, g, w, m, v):
    def body(g_ref, w_ref, m_ref, v_ref, d_ref, mo_ref, vo_ref):
        delta, mn, vn = _adam_math(w_ref[...], g_ref[...], m_ref[...], v_ref[...])
        d_ref[...] = delta
        mo_ref[...] = mn
        vo_ref[...] = vn

    return pl.pallas_call(body, name=name, out_shape=[jax.ShapeDtypeStruct(g.shape, f32)] * 3)(g, w, m, v)


def _pack(vals):
    rows, offs, at = [], [], 0
    for a in vals:
        a = a.reshape(-1)
        n = -(-a.shape[0] // (8 * LANE)) * 8
        rows.append(jnp.pad(a, (0, n * LANE - a.shape[0])).reshape(n, LANE))
        offs.append(at)
        at += n
    return jnp.concatenate(rows, axis=0), offs


def _unpack(packed, offs, shapes):
    out = []
    for o, shp in zip(offs, shapes):
        size = 1
        for d in shp:
            size *= d
        n = -(-size // LANE)
        out.append(packed[o:o + n].reshape(-1)[:size].reshape(shp))
    return out


SMALL = ("norm_mix_g", "conv_w", "pool_w", "pool_scale", "gla_w_decay", "gla_b_decay", "gla_out_g",
         "fox_q_g", "fox_k_g", "fox_b_f", "norm_ffn_g")
ALL = ("norm_mix_g", "w_in", "conv_w", "pool_w", "pool_scale", "gla_w_decay", "gla_b_decay", "gla_out_g",
       "fox_q_g", "fox_k_g", "fox_b_f", "w_out", "norm_ffn_g", "w_gate", "w_up", "w_down")


def kernel(x, norm_mix_g, w_in, conv_w, pool_w, pool_scale, gla_w_decay, gla_b_decay, gla_out_g, fox_q_g, fox_k_g, fox_b_f, w_out, norm_ffn_g, w_gate, w_up, w_down, loss_target, m_norm_mix_g, m_w_in, m_conv_w, m_pool_w, m_pool_scale, m_gla_w_decay, m_gla_b_decay, m_gla_out_g, m_fox_q_g, m_fox_k_g, m_fox_b_f, m_w_out, m_norm_ffn_g, m_w_gate, m_w_up, m_w_down, v_norm_mix_g, v_w_in, v_conv_w, v_pool_w, v_pool_scale, v_gla_w_decay, v_gla_b_decay, v_gla_out_g, v_fox_q_g, v_fox_k_g, v_fox_b_f, v_w_out, v_norm_ffn_g, v_w_gate, v_w_up, v_w_down):
    w = dict(norm_mix_g=norm_mix_g, w_in=w_in, conv_w=conv_w, pool_w=pool_w, pool_scale=pool_scale,
             gla_w_decay=gla_w_decay, gla_b_decay=gla_b_decay, gla_out_g=gla_out_g, fox_q_g=fox_q_g, fox_k_g=fox_k_g,
             fox_b_f=fox_b_f, w_out=w_out, norm_ffn_g=norm_ffn_g, w_gate=w_gate, w_up=w_up, w_down=w_down)
    m = dict(norm_mix_g=m_norm_mix_g, w_in=m_w_in, conv_w=m_conv_w, pool_w=m_pool_w, pool_scale=m_pool_scale,
             gla_w_decay=m_gla_w_decay, gla_b_decay=m_gla_b_decay, gla_out_g=m_gla_out_g, fox_q_g=m_fox_q_g,
             fox_k_g=m_fox_k_g, fox_b_f=m_fox_b_f, w_out=m_w_out, norm_ffn_g=m_norm_ffn_g, w_gate=m_w_gate,
             w_up=m_w_up, w_down=m_w_down)
    v = dict(norm_mix_g=v_norm_mix_g, w_in=v_w_in, conv_w=v_conv_w, pool_w=v_pool_w, pool_scale=v_pool_scale,
             gla_w_decay=v_gla_w_decay, gla_b_decay=v_gla_b_decay, gla_out_g=v_gla_out_g, fox_q_g=v_fox_q_g,
             fox_k_g=v_fox_k_g, fox_b_f=v_fox_b_f, w_out=v_w_out, norm_ffn_g=v_norm_ffn_g, w_gate=v_w_gate,
             w_up=v_w_up, w_down=v_w_down)
    chip = 2 * lax.axis_index("x") + lax.axis_index("y")

    pos = jnp.stack([chip, lax.axis_index("c")]).astype(jnp.int32)

    mine, offs = _pack([conv_w, gla_w_decay, jnp.zeros((8, LANE), f32)])
    order = [(l, grp) for l in range(2) for grp in GROUPS]
    sems, gbufs = _gather_start(
        "gather_start",
        [[_own_slot(mine, chip)]] + [[_own_slot(w[k][l].astype(bf16), chip) for k in grp] for l, grp in order])

    def gathered(tag, gi, after):
        got = _gather_wait(tag + "_wait", gbufs[gi], sems[gi][0], sems[gi][1], after)
        return _gather_exchange(tag + "_exchange", got)

    def weights(l, group, after):
        got = gathered(f"l{l}_gather_{group}", 1 + 3 * l + ("w_in", "w_out", "ffn").index(group), after)
        if group == "w_in":
            return _win_from_blocks(got[0])
        return got[0] if group == "w_out" else got

    every = gathered("gather_small", 0, x)[0]
    per_chip = [_unpack(every[kk], offs, [conv_w.shape, gla_w_decay.shape]) for kk in range(NCHIP)]
    conv_full = jnp.concatenate([pc[0] for pc in per_chip], axis=-1)[:, :, 0, :]
    wdec_full = jnp.concatenate([pc[1] for pc in per_chip], axis=-1)

    wdec = jnp.pad(wdec_full.reshape(2, 16, 4, 64), ((0, 0), (GA_LANE, LANE - GA_LANE - 16), (0, 0), (0, 64)))
    p = dict(
        g_mix=norm_mix_g[:, None, :], g_ffn=norm_ffn_g[:, None, :],
        conv_w=conv_full, pool_w=pool_w, pool_scale=pool_scale[:, None, :],
        wdec=wdec.reshape(2, LANE, G),
        bdec=jnp.pad(gla_b_decay.reshape(2, 4, 64), ((0, 0), (0, 0), (0, 64))).reshape(2, 1, G),
        gla_og=gla_out_g[:, None, :], fox_qg=fox_q_g[:, None, :], fox_kg=fox_k_g[:, None, :],
        fox_bf=jnp.pad(fox_b_f, ((0, 0), (FF_LANE, LANE - FF_LANE - 4)))[:, None, :])

    h0 = x[0]
    h1, sv0 = _layer_fwd(0, h0, p, weights)
    h2, sv1 = _layer_fwd(1, h1, p, weights)
    sq, dh, dhb = _loss("loss", h2, loss_target[0])
    loss = lax.psum(sq[0, 0] * (0.5 / D), ("x", "y", "c"))

    late = ("w_out", "w_gate", "w_up", "w_down")
    shipped = {}

    def ship(tag):
        def start(grads_by_name):
            keys = tuple(grads_by_name)
            shipped[tag] = _rs_begin(tag, keys, [grads_by_name[k] for k in keys], pos)
            return shipped[tag][-1]
        return start

    dh1, dy, dgf1 = _layer_bwd_ffn(1, dh, dhb, p, sv1, ship("l1_rs_a_"))
    dh, dhb, small1 = _layer_bwd_mix(1, dh1, dy, p, sv1, ship("l1_rs_b_"))
    dh1, dy, dgf0 = _layer_bwd_ffn(0, dh, dhb, p, sv0, ship("l0_rs_a_"))
    dh, dhb, small0 = _layer_bwd_mix(0, dh1, dy, p, sv0, ship("l0_rs_b_"))
    small0["norm_ffn_g"], small1["norm_ffn_g"] = dgf0[0], dgf1[0]

    red1 = _rs_end("l1_rs_a_", shipped["l1_rs_a_"], pos, dh)
    red1.update(_rs_end("l1_rs_b_", shipped["l1_rs_b_"], pos, red1["w_down"]))
    red0 = _rs_end("l0_rs_a_", shipped["l0_rs_a_"], pos, red1["w_in"])
    grads, deltas, new_m, new_v = {}, {}, {}, {}
    for k in late:
        grads[k], deltas[k], new_m[k], new_v[k] = _adam_big("adam_" + k, red0[k], red1[k], w[k], m[k], v[k])

    packed, goffs = _pack([jnp.stack([small0[k], small1[k]]) for k in SMALL])
    total = _sum8("sum_small_grads", _allgather_small("gather_small_grads", packed))
    red0.update(_rs_end("l0_rs_b_", shipped["l0_rs_b_"], pos, total))
    k = "w_in"
    grads[k], deltas[k], new_m[k], new_v[k] = _adam_big("adam_" + k, red0[k], red1[k], w[k], m[k], v[k])
    full_shapes = [(2,) + small0[k].shape for k in SMALL]
    gsmall = dict(zip(SMALL, _unpack(total, goffs, full_shapes)))
    gsmall["conv_w"] = lax.dynamic_slice_in_dim(gsmall["conv_w"], chip * LANE, LANE, axis=2)[:, :, None, :]
    gsmall["gla_w_decay"] = lax.dynamic_slice_in_dim(gsmall["gla_w_decay"], chip * 64, 64, axis=2)
    gp, loffs = _pack([gsmall[k] for k in SMALL])
    wp, _ = _pack([w[k] for k in SMALL])
    mp, _ = _pack([m[k] for k in SMALL])
    vp, _ = _pack([v[k] for k in SMALL])
    dp, mnp, vnp = _adam_small("adam_small", gp, wp, mp, vp)
    shapes = [w[k].shape for k in SMALL]
    for k, a, b, c_, d_ in zip(SMALL, _unpack(gp, loffs, shapes), _unpack(dp, loffs, shapes),
                               _unpack(mnp, loffs, shapes), _unpack(vnp, loffs, shapes)):
        grads[k], deltas[k], new_m[k], new_v[k] = a, b, c_, d_

    return (loss, dh[None], *[grads[k] for k in ALL], *[deltas[k] for k in ALL],
            *[new_m[k] for k in ALL], *[new_v[k] for k in ALL])
```

```python
import functools

import jax
import jax.numpy as jnp
from jax import lax
from jax.experimental import pallas as pl
from jax.experimental.pallas import tpu as pltpu

f32 = jnp.float32
bf16 = jnp.bfloat16

D = 2048
G = 512
DFF = 5632
NCHIP = 4
FB = DFF // NCHIP
WIN = 5140
WINB = WIN // NCHIP
EPS = 1e-6
CHUNK = 64
LANE = 128

CB, CC, CH, PU, GQ, GK, GV, GG, FQ, FK, FV, MISC = 0, 4, 8, 12, 16, 20, 24, 28, 32, 36, 40, 44
ZC = 45 * LANE
FF_LANE = 0
GA_LANE = 8

ADAM_LR, ADAM_B1, ADAM_B2, ADAM_EPS, ADAM_WD, ADAM_STEP = 0.001, 0.9, 0.999, 1e-08, 0.01, 10

VMEM_LIMIT = 60 * 1024 * 1024
MESH = pl.DeviceIdType.MESH


def _cp(sem=None):
    return pltpu.CompilerParams(dimension_semantics=sem, vmem_limit_bytes=VMEM_LIMIT)


def _dot(a, b, dims=((1,), (0,))):
    return lax.dot_general(a.astype(bf16), b.astype(bf16), (dims, ((), ())), preferred_element_type=f32)


def _bdot(a, b, ca, cb):
    return lax.dot_general(a.astype(bf16), b.astype(bf16), (((ca,), (cb,)), ((0,), (0,))),
                           preferred_element_type=f32)


def _log_sigmoid(x):
    return jnp.minimum(x, 0.0) - jnp.log(1.0 + jnp.exp(-jnp.abs(x)))


@jax.custom_vjp
def _sigmoid(x):
    return 1.0 / (1.0 + jnp.exp(-x))


def _sigmoid_fwd(x):
    s = _sigmoid(x)
    return s, s


def _sigmoid_bwd(s, g):
    return (g * s * (1.0 - s),)


_sigmoid.defvjp(_sigmoid_fwd, _sigmoid_bwd)


def _rms(x, g):
    return x * lax.rsqrt(jnp.mean(x * x, axis=-1, keepdims=True) + EPS) * g


def _shift_impl(x, n, period, transpose):
    rows = x.shape[0]
    t = lax.broadcasted_iota(jnp.int32, x.shape, 0)
    if period is not None:
        t = t & (period - 1)
    keep = t >= n
    if not transpose:
        return jnp.where(keep, pltpu.roll(x, n, 0), 0.0)
    return pltpu.roll(jnp.where(keep, x, 0.0), rows - n, 0)


def _shift(x, n, period=None):
    @jax.custom_vjp
    def f(v):
        return _shift_impl(v, n, period, False)

    def fwd(v):
        return f(v), None

    def bwd(_, g):
        return (_shift_impl(g, n, period, True),)

    f.defvjp(fwd, bwd)
    return f(x)


def _cumsum_rows(x, length, period=None):
    n = 1
    while n < length:
        x = x + _shift(x, n, period)
        n *= 2
    return x


def _convpool_fn(cb, cc, ch, pu, w0, w1, w2, pw, ps, j):
    u = cc * ch
    y = w2 * u + w1 * _shift(u, 1) + w0 * _shift(u, 2)
    ya = cb * y
    s2 = pu + _shift(pu, 1)
    s4 = s2 + _shift(s2, 2)
    s8 = s4 + _shift(s4, 4)
    s16 = s8 + _shift(s8, 8)
    wsum = jnp.where(j == 0, s2, jnp.where(j == 1, s4, jnp.where(j == 2, s8, s16)))
    width = (2 << j).astype(f32)
    t = lax.broadcasted_iota(jnp.int32, pu.shape, 0).astype(f32)
    count = jnp.minimum(t + 1.0, width)
    d = wsum / count - pu
    yb = _dot(d, pw) * ps
    return ya, yb


def _foxprep_fn(misc, bf):
    lf = _log_sigmoid(misc + bf)
    fc = _cumsum_rows(lf, lf.shape[0])
    return fc, jnp.transpose(fc)


def _fox_fn(q, k, v, fcol, frow8, qg, kg, h, i):
    tq, s = q.shape[0], k.shape[0]
    qn = _rms(q, qg)
    kn = _rms(k, kg)
    lg = _dot(qn, kn, ((1,), (1,))) * (LANE ** -0.5)
    lane = lax.broadcasted_iota(jnp.int32, fcol.shape, 1)
    fq = jnp.sum(jnp.where(lane == h, fcol, 0.0), axis=1, keepdims=True)
    row = lax.broadcasted_iota(jnp.int32, frow8.shape, 0)
    fk = jnp.sum(jnp.where(row == h, frow8, 0.0), axis=0, keepdims=True)
    lg = lg + fq - fk
    qpos = i * tq + lax.broadcasted_iota(jnp.int32, (tq, s), 0)
    kpos = lax.broadcasted_iota(jnp.int32, (tq, s), 1)
    lg = jnp.where(kpos <= qpos, lg, -jnp.inf)
    m = lax.stop_gradient(jnp.max(lg, axis=1, keepdims=True))
    e = jnp.exp(lg - m)
    p = e / jnp.sum(e, axis=1, keepdims=True)
    return _dot(p, v)


def _gla1_fn(q, k, v, misc, wd, bd):
    ts = q.shape[0]
    nb = ts // CHUNK
    x = _dot(misc, wd) + bd
    la = _log_sigmoid(x) * (1.0 / 16.0)
    cc = _cumsum_rows(la, CHUNK, CHUNK)
    la3 = la.reshape(nb, CHUNK, LANE)
    last3 = jnp.sum(la3, axis=1, keepdims=True)
    last2 = jnp.sum(la3, axis=1)
    cc3 = cc.reshape(nb, CHUNK, LANE)
    q3 = (q * 0.125).reshape(nb, CHUNK, LANE)
    k3 = k.reshape(nb, CHUNK, LANE)
    v3 = v.reshape(nb, CHUNK, LANE)
    ep = jnp.exp(cc3)
    en = jnp.exp(-cc3)
    qe = q3 * ep
    a1 = _bdot(qe, k3 * en, 2, 2)
    a2 = _bdot(q3 * en, k3 * ep, 2, 2)
    ti = lax.broadcasted_iota(jnp.int32, a1.shape, 1)
    si = lax.broadcasted_iota(jnp.int32, a1.shape, 2)
    sc = jnp.where(si <= ti, a1, a2)
    oi = _bdot(sc, v3, 2, 1)
    kd = k3 * jnp.exp(last3 - cc3)
    el = jnp.exp(last2)
    return qe.reshape(ts, LANE), kd.reshape(ts, LANE), el, oi.reshape(ts, LANE)


def _gla3_fn(o, gg, og):
    return _rms(o, og) * (gg * _sigmoid(gg))


def _ffn_fn(gate, up):
    return gate * _sigmoid(gate) * up


def _mm(name, a, b, a_spec, b_spec, out_shape, out_spec, grid, dims, nk, res=None, res_spec=None, dep=None):
    has_res = res is not None
    has_dep = dep is not None
    nax = len(grid)

    def body(*refs):
        a_ref, b_ref = refs[0], refs[1]
        res_ref = refs[2] if has_res else None
        out_ref = refs[2 + has_res + has_dep]
        part = _dot(a_ref[...], b_ref[...], dims)
        if nk == 1:
            if has_res:
                part = part + res_ref[...]
            out_ref[...] = part.astype(out_ref.dtype)
            return
        acc_ref = refs[3 + has_res + has_dep]
        k = pl.program_id(nax - 1)

        @pl.when(k == 0)
        def _():
            acc_ref[...] = part

        @pl.when(k > 0)
        def _():
            acc_ref[...] += part

        @pl.when(k == nk - 1)
        def _():
            tot = acc_ref[...]
            if has_res:
                tot = tot + res_ref[...]
            out_ref[...] = tot.astype(out_ref.dtype)

    ops = [a, b] + ([res] if has_res else []) + ([dep] if has_dep else [])
    specs = [a_spec, b_spec] + ([res_spec] if has_res else [])
    if has_dep:
        specs.append(pl.BlockSpec((8, LANE), lambda *_: (0, 0)))
    blk = tuple(d for d in out_spec.block_shape if d is not None)
    scratch = [pltpu.VMEM(blk, f32)] if nk > 1 else []
    return pl.pallas_call(
        body, name=name, grid=grid, in_specs=specs, out_specs=out_spec, out_shape=out_shape,
        scratch_shapes=scratch,
        compiler_params=_cp(("parallel",) * (nax - 1) + ("arbitrary",)),
    )(*ops)


def _tm(s):
    return min(s, 512)


def _tm_big(s):
    return min(s, 1024)


def _mm_gate_up(name, u, w_gate, w_up):
    s = u.shape[0]
    tm = _tm(s)

    def body(u_ref, wg_ref, wu_ref, g_ref, up_ref, a_ref):
        g = _dot(u_ref[...], wg_ref[...])
        up = _dot(u_ref[...], wu_ref[...])
        g_ref[...] = g
        up_ref[...] = up
        a_ref[...] = _ffn_fn(g, up).astype(bf16)

    wspec = pl.BlockSpec((None, D, FB), lambda j, i: (j, 0, 0))
    tile = pl.BlockSpec((tm, FB), lambda j, i: (i, j))
    return pl.pallas_call(
        body, name=name, grid=(NCHIP, s // tm),
        in_specs=[pl.BlockSpec((tm, D), lambda j, i: (i, 0)), wspec, wspec], out_specs=[tile, tile, tile],
        out_shape=[jax.ShapeDtypeStruct((s, DFF), f32), jax.ShapeDtypeStruct((s, DFF), f32),
                   jax.ShapeDtypeStruct((s, DFF), bf16)],
        compiler_params=_cp(("parallel", "parallel")),
    )(u, w_gate, w_up)


def _mm_dact(name, dh, w_down, gate, up):
    s = dh.shape[0]
    tm = _tm(s)

    def body(dh_ref, wd_ref, g_ref, up_ref, dg_ref, du_ref):
        dact = _dot(dh_ref[...], wd_ref[...], ((1,), (1,)))
        _, vjp = jax.vjp(_ffn_fn, g_ref[...], up_ref[...])
        dg, du = vjp(dact)
        dg_ref[...] = dg.astype(bf16)
        du_ref[...] = du.astype(bf16)

    tile = pl.BlockSpec((tm, FB), lambda j, i: (i, j))
    return pl.pallas_call(
        body, name=name, grid=(NCHIP, s // tm),
        in_specs=[pl.BlockSpec((tm, D), lambda j, i: (i, 0)), pl.BlockSpec((None, FB, D), lambda j, i: (j, 0, 0)),
                  tile, tile],
        out_specs=[tile, tile], out_shape=[jax.ShapeDtypeStruct((s, DFF), bf16)] * 2,
        compiler_params=_cp(("parallel", "parallel")),
    )(dh, w_down, gate, up)


def _rmsnorm_fwd(name, x, g, l):
    s = x.shape[0]
    tm = min(s, 256)

    def body(x_ref, g_ref, u_ref):
        u_ref[...] = _rms(x_ref[...], g_ref[...]).astype(bf16)

    return pl.pallas_call(
        body, name=name, grid=(s // tm,),
        in_specs=[pl.BlockSpec((tm, D), lambda i: (i, 0)), pl.BlockSpec((None, 1, D), lambda i: (l, 0, 0))],
        out_specs=pl.BlockSpec((tm, D), lambda i: (i, 0)),
        out_shape=jax.ShapeDtypeStruct((s, D), bf16), compiler_params=_cp(("parallel",)),
    )(x, g)


def _rmsnorm_bwd(name, x, g, du, dres, l):
    s = x.shape[0]
    tm = min(s, 256)

    def body(x_ref, g_ref, du_ref, dres_ref, dx_ref, dxb_ref, dg_ref):
        _, vjp = jax.vjp(_rms, x_ref[...], g_ref[...])
        dx, dg = vjp(du_ref[...])
        tot = dx + dres_ref[...]
        dx_ref[...] = tot
        dxb_ref[...] = tot.astype(bf16)

        @pl.when(pl.program_id(0) == 0)
        def _():
            dg_ref[...] = dg

        @pl.when(pl.program_id(0) > 0)
        def _():
            dg_ref[...] += dg

    row = pl.BlockSpec((tm, D), lambda i: (i, 0))
    return pl.pallas_call(
        body, name=name, grid=(s // tm,),
        in_specs=[row, pl.BlockSpec((None, 1, D), lambda i: (l, 0, 0)), row, row],
        out_specs=[row, row, pl.BlockSpec((1, D), lambda i: (0, 0))],
        out_shape=[jax.ShapeDtypeStruct((s, D), f32), jax.ShapeDtypeStruct((s, D), bf16),
                   jax.ShapeDtypeStruct((1, D), f32)],
        compiler_params=_cp(("arbitrary",)),
    )(x, g, du, dres)


def _loss(name, y, t):
    s = y.shape[0]
    tm = min(s, 256)
    row = pl.BlockSpec((tm, D), lambda i: (i, 0))

    def body(y_ref, t_ref, l_ref, d_ref, db_ref):
        e = y_ref[...] - t_ref[...]
        d = e * (1.0 / D)
        d_ref[...] = d
        db_ref[...] = d.astype(bf16)
        part = jnp.zeros((8, LANE), f32) + jnp.sum(e * e)

        @pl.when(pl.program_id(0) == 0)
        def _():
            l_ref[...] = part

        @pl.when(pl.program_id(0) > 0)
        def _():
            l_ref[...] += part

    return pl.pallas_call(
        body, name=name, grid=(s // tm,), in_specs=[row, row],
        out_specs=[pl.BlockSpec((8, LANE), lambda i: (0, 0)), row, row],
        out_shape=[jax.ShapeDtypeStruct((8, LANE), f32), jax.ShapeDtypeStruct((s, D), f32),
                   jax.ShapeDtypeStruct((s, D), bf16)],
        compiler_params=_cp(("arbitrary",)),
    )(y, t)


def _zspec(s, blk):
    return pl.BlockSpec((s, LANE), lambda j: (0, blk + j))


def _convpool_specs(s, l):
    return [_zspec(s, CB), _zspec(s, CC), _zspec(s, CH), _zspec(s, PU),
            pl.BlockSpec((None, 3, LANE), lambda j: (l, 0, j)),
            pl.BlockSpec((None, None, LANE, LANE), lambda j: (l, j, 0, 0)),
            pl.BlockSpec((None, 1, LANE), lambda j: (l, 0, j))]


def _convpool_fwd(name, z, conv_w, pool_w, pool_scale, l):
    s = z.shape[0]

    def body(cb, cc, ch, pu, cw, pw, ps, ya_ref, yb_ref):
        ya, yb = _convpool_fn(cb[...], cc[...], ch[...], pu[...], cw[0:1, :], cw[1:2, :], cw[2:3, :], pw[...], ps[...],
                              pl.program_id(0))
        ya_ref[...] = ya.astype(bf16)
        yb_ref[...] = yb.astype(bf16)

    col = pl.BlockSpec((s, LANE), lambda j: (0, j))
    return pl.pallas_call(
        body, name=name, grid=(4,), in_specs=_convpool_specs(s, l), out_specs=[col, col],
        out_shape=[jax.ShapeDtypeStruct((s, G), bf16)] * 2, compiler_params=_cp(("parallel",)),
    )(z, z, z, z, conv_w, pool_w, pool_scale)


def _convpool_bwd(name, z, conv_w, pool_w, pool_scale, dy, l):
    s = z.shape[0]

    def body(cb, cc, ch, pu, cw, pw, ps, dya, dyb, dcb, dcc, dch, dpu, dcw, dpw, dps):
        j = pl.program_id(0)
        fn = functools.partial(_convpool_fn, j=j)
        _, vjp = jax.vjp(fn, cb[...], cc[...], ch[...], pu[...], cw[0:1, :], cw[1:2, :], cw[2:3, :], pw[...], ps[...])
        g = vjp((dya[...], dyb[...]))
        dcb[...] = g[0].astype(bf16)
        dcc[...] = g[1].astype(bf16)
        dch[...] = g[2].astype(bf16)
        dpu[...] = g[3].astype(bf16)
        dcw[0:1, :] = g[4]
        dcw[1:2, :] = g[5]
        dcw[2:3, :] = g[6]
        dpw[...] = g[7]
        dps[...] = g[8]

    col = pl.BlockSpec((s, LANE), lambda j: (0, j))
    specs = _convpool_specs(s, l) + [pl.BlockSpec((s, LANE), lambda j: (0, j)),
                                     pl.BlockSpec((s, LANE), lambda j: (0, 4 + j))]
    return pl.pallas_call(
        body, name=name, grid=(4,), in_specs=specs,
        out_specs=[col, col, col, col, pl.BlockSpec((3, LANE), lambda j: (0, j)),
                   pl.BlockSpec((None, LANE, LANE), lambda j: (j, 0, 0)), pl.BlockSpec((1, LANE), lambda j: (0, j))],
        out_shape=[jax.ShapeDtypeStruct((s, G), bf16)] * 4 + [
            jax.ShapeDtypeStruct((3, G), f32), jax.ShapeDtypeStruct((4, LANE, LANE), f32),
            jax.ShapeDtypeStruct((1, G), f32)],
        compiler_params=_cp(("parallel",)),
    )(z, z, z, z, conv_w, pool_w, pool_scale, dy, dy)


def _foxprep_fwd(name, z, bf, l):
    s = z.shape[0]

    def body(m_ref, b_ref, fc_ref, fr_ref):
        fc, fr = _foxprep_fn(m_ref[...], b_ref[...])
        fc_ref[...] = fc
        fr_ref[...] = fr

    return pl.pallas_call(
        body, name=name, grid=(1,),
        in_specs=[pl.BlockSpec((s, LANE), lambda i: (0, MISC)), pl.BlockSpec((None, 1, LANE), lambda i: (l, 0, 0))],
        out_specs=[pl.BlockSpec((s, LANE), lambda i: (0, 0)), pl.BlockSpec((LANE, s), lambda i: (0, 0))],
        out_shape=[jax.ShapeDtypeStruct((s, LANE), f32), jax.ShapeDtypeStruct((LANE, s), f32)],
        compiler_params=_cp(("arbitrary",)),
    )(z, bf)


def _foxprep_bwd(name, z, bf, dfc4, dfr4, dmisc4, l):
    s = z.shape[0]

    def body(m_ref, b_ref, dfc_ref, dfr_ref, dm4_ref, dm_ref, db_ref):
        _, vjp = jax.vjp(_foxprep_fn, m_ref[...], b_ref[...])
        dfc = dfc_ref[0] + dfc_ref[1] + dfc_ref[2] + dfc_ref[3]
        dfr = dfr_ref[0] + dfr_ref[1] + dfr_ref[2] + dfr_ref[3]
        dfr = jnp.concatenate([dfr, jnp.zeros((LANE - 8, s), f32)], axis=0)
        dm, db = vjp((dfc, dfr))
        dm = dm + (dm4_ref[0] + dm4_ref[1] + dm4_ref[2] + dm4_ref[3])
        dm_ref[...] = dm.astype(bf16)
        db_ref[...] = db

    whole = lambda shape: pl.BlockSpec(shape, lambda i: (0,) * len(shape))
    return pl.pallas_call(
        body, name=name, grid=(1,),
        in_specs=[pl.BlockSpec((s, LANE), lambda i: (0, MISC)), pl.BlockSpec((None, 1, LANE), lambda i: (l, 0, 0)),
                  whole((4, s, LANE)), whole((4, 8, s)), whole((4, s, LANE))],
        out_specs=[whole((s, LANE)), whole((1, LANE))],
        out_shape=[jax.ShapeDtypeStruct((s, LANE), bf16), jax.ShapeDtypeStruct((1, LANE), f32)],
        compiler_params=_cp(("arbitrary",)),
    )(z, bf, dfc4, dfr4, dmisc4)


def _fox_specs(s, tq, l):
    return [pl.BlockSpec((tq, LANE), lambda h, i: (i, FQ + h)),
            pl.BlockSpec((s, LANE), lambda h, i: (0, FK + h)),
            pl.BlockSpec((s, LANE), lambda h, i: (0, FV + h)),
            pl.BlockSpec((tq, LANE), lambda h, i: (i, 0)),
            pl.BlockSpec((8, s), lambda h, i: (0, 0)),
            pl.BlockSpec((None, 1, LANE), lambda h, i: (l, 0, 0)),
            pl.BlockSpec((None, 1, LANE), lambda h, i: (l, 0, 0))]


def _fox_fwd(name, z, fc, fr, qg, kg, l):
    s = z.shape[0]
    tq = min(s, 256)

    def body(q, k, v, fc_ref, fr_ref, qg_ref, kg_ref, y_ref):
        y = _fox_fn(q[...], k[...], v[...], fc_ref[...], fr_ref[...], qg_ref[...], kg_ref[...],
                    pl.program_id(0), pl.program_id(1))
        y_ref[...] = y.astype(bf16)

    return pl.pallas_call(
        body, name=name, grid=(4, s // tq), in_specs=_fox_specs(s, tq, l),
        out_specs=pl.BlockSpec((tq, LANE), lambda h, i: (i, h)),
        out_shape=jax.ShapeDtypeStruct((s, G), bf16), compiler_params=_cp(("parallel", "parallel")),
    )(z, z, z, fc, fr, qg, kg)


def _fox_bwd(name, z, fc, fr, qg, kg, dy, l):
    s = z.shape[0]
    tq = min(s, 256)

    def body(q, k, v, fc_ref, fr_ref, qg_ref, kg_ref, dy_ref, dq, dk, dv, dfc, dfr, dqg, dkg):
        h, i = pl.program_id(0), pl.program_id(1)
        fn = functools.partial(_fox_fn, h=h, i=i)
        _, vjp = jax.vjp(fn, q[...], k[...], v[...], fc_ref[...], fr_ref[...], qg_ref[...], kg_ref[...])
        g = vjp(dy_ref[...])
        dq[...] = g[0].astype(bf16)
        dfc[...] = g[3]

        @pl.when(i == 0)
        def _():
            dk[...] = g[1]
            dv[...] = g[2]
            dfr[...] = g[4]
            dqg[...] = g[5]
            dkg[...] = g[6]

        @pl.when(i > 0)
        def _():
            dk[...] += g[1]
            dv[...] += g[2]
            dfr[...] += g[4]
            dqg[...] += g[5]
            dkg[...] += g[6]

    specs = _fox_specs(s, tq, l) + [pl.BlockSpec((tq, LANE), lambda h, i: (i, 12 + h))]
    head = pl.BlockSpec((s, LANE), lambda h, i: (0, h))
    gain = pl.BlockSpec((None, 1, LANE), lambda h, i: (h, 0, 0))
    return pl.pallas_call(
        body, name=name, grid=(4, s // tq), in_specs=specs,
        out_specs=[pl.BlockSpec((tq, LANE), lambda h, i: (i, h)), head, head,
                   pl.BlockSpec((None, tq, LANE), lambda h, i: (h, i, 0)),
                   pl.BlockSpec((None, 8, s), lambda h, i: (h, 0, 0)), gain, gain],
        out_shape=[jax.ShapeDtypeStruct((s, G), bf16), jax.ShapeDtypeStruct((s, G), f32),
                   jax.ShapeDtypeStruct((s, G), f32), jax.ShapeDtypeStruct((4, s, LANE), f32),
                   jax.ShapeDtypeStruct((4, 8, s), f32), jax.ShapeDtypeStruct((4, 1, LANE), f32),
                   jax.ShapeDtypeStruct((4, 1, LANE), f32)],
        compiler_params=_cp(("parallel", "arbitrary")),
    )(z, z, z, fc, fr, qg, kg, dy)


def _gla_ts(s):
    return min(s, 512)


def _gla1_specs(s, ts, l):
    return [pl.BlockSpec((ts, LANE), lambda h, i: (i, GQ + h)),
            pl.BlockSpec((ts, LANE), lambda h, i: (i, GK + h)),
            pl.BlockSpec((ts, LANE), lambda h, i: (i, GV + h)),
            pl.BlockSpec((ts, LANE), lambda h, i: (i, MISC)),
            pl.BlockSpec((None, LANE, LANE), lambda h, i: (l, 0, h)),
            pl.BlockSpec((None, 1, LANE), lambda h, i: (l, 0, h))]


def _gla1_fwd(name, z, wd, bd, l):
    s = z.shape[0]
    ts = _gla_ts(s)
    nb = ts // CHUNK

    def body(q, k, v, m, wd_ref, bd_ref, qe_ref, kd_ref, el_ref, oi_ref):
        qe, kd, el, oi = _gla1_fn(q[...], k[...], v[...], m[...], wd_ref[...], bd_ref[...])
        qe_ref[...] = qe.astype(bf16)
        kd_ref[...] = kd.astype(bf16)
        el_ref[...] = el
        oi_ref[...] = oi

    blk = pl.BlockSpec((ts, LANE), lambda h, i: (i, h))
    return pl.pallas_call(
        body, name=name, grid=(4, s // ts), in_specs=_gla1_specs(s, ts, l),
        out_specs=[blk, blk, pl.BlockSpec((nb, LANE), lambda h, i: (i, h)), blk],
        out_shape=[jax.ShapeDtypeStruct((s, G), bf16), jax.ShapeDtypeStruct((s, G), bf16),
                   jax.ShapeDtypeStruct((s // CHUNK, G), f32), jax.ShapeDtypeStruct((s, G), f32)],
        compiler_params=_cp(("parallel", "parallel")),
    )(z, z, z, z, wd, bd)


def _gla1_bwd(name, z, wd, bd, dqe, dkd, del_, do, dvi, l):
    s = z.shape[0]
    ts = _gla_ts(s)
    nb = ts // CHUNK

    def body(q, k, v, m, wd_ref, bd_ref, dqe_ref, dkd_ref, del_ref, do_ref, dvi_ref, dq, dk, dv, dm, dwd, dbd):
        i = pl.program_id(1)
        _, vjp = jax.vjp(_gla1_fn, q[...], k[...], v[...], m[...], wd_ref[...], bd_ref[...])
        g = vjp((dqe_ref[...], dkd_ref[...], del_ref[...], do_ref[...]))
        dq[...] = g[0].astype(bf16)
        dk[...] = g[1].astype(bf16)
        dv[...] = (g[2] + dvi_ref[...]).astype(bf16)
        dm[...] = g[3]

        @pl.when(i == 0)
        def _():
            dwd[...] = g[4]
            dbd[...] = g[5]

        @pl.when(i > 0)
        def _():
            dwd[...] += g[4]
            dbd[...] += g[5]

    blk = pl.BlockSpec((ts, LANE), lambda h, i: (i, h))
    specs = _gla1_specs(s, ts, l) + [blk, blk, pl.BlockSpec((nb, LANE), lambda h, i: (i, h)), blk, blk]
    return pl.pallas_call(
        body, name=name, grid=(4, s // ts), in_specs=specs,
        out_specs=[blk, blk, blk, pl.BlockSpec((None, ts, LANE), lambda h, i: (h, i, 0)),
                   pl.BlockSpec((None, LANE, LANE), lambda h, i: (h, 0, 0)),
                   pl.BlockSpec((None, 1, LANE), lambda h, i: (h, 0, 0))],
        out_shape=[jax.ShapeDtypeStruct((s, G), bf16)] * 3 + [
            jax.ShapeDtypeStruct((4, s, LANE), f32), jax.ShapeDtypeStruct((4, LANE, LANE), f32),
            jax.ShapeDtypeStruct((4, 1, LANE), f32)],
        compiler_params=_cp(("parallel", "arbitrary")),
    )(z, z, z, z, wd, bd, dqe, dkd, del_, do, dvi)


def _gla2_fwd(name, z, qe, kd, el, oi):
    s = z.shape[0]
    n = s // CHUNK

    def body(v_ref, qe_ref, kd_ref, el_ref, oi_ref, o_ref, st_ref, cur):
        cur[...] = jnp.zeros_like(cur)

        def step(c, carry):
            rows = pl.ds(pl.multiple_of(c * CHUNK, CHUNK), CHUNK)
            st = cur[...]
            st_ref[c] = st
            o_ref[rows, :] = oi_ref[rows, :] + _dot(qe_ref[rows, :], st, ((1,), (1,)))
            cur[...] = st * el_ref[pl.ds(c, 1), :] + _dot(v_ref[rows, :], kd_ref[rows, :], ((0,), (0,)))
            return carry

        lax.fori_loop(0, n, step, 0)

    head = pl.BlockSpec((s, LANE), lambda h: (0, h))
    return pl.pallas_call(
        body, name=name, grid=(4,),
        in_specs=[pl.BlockSpec((s, LANE), lambda h: (0, GV + h)), head, head,
                  pl.BlockSpec((n, LANE), lambda h: (0, h)), head],
        out_specs=[head, pl.BlockSpec((None, n, LANE, LANE), lambda h: (h, 0, 0, 0))],
        out_shape=[jax.ShapeDtypeStruct((s, G), f32), jax.ShapeDtypeStruct((4, n, LANE, LANE), f32)],
        scratch_shapes=[pltpu.VMEM((LANE, LANE), f32)],
        compiler_params=_cp(("parallel",)),
    )(z, qe, kd, el, oi)


def _gla2_bwd(name, z, qe, kd, el, st, do):
    s = z.shape[0]
    n = s // CHUNK

    def body(v_ref, qe_ref, kd_ref, el_ref, st_ref, do_ref, dqe_ref, dkd_ref, dv_ref, del_ref, dcur):
        dcur[...] = jnp.zeros_like(dcur)

        def step(t, carry):
            c = n - 1 - t
            rows = pl.ds(pl.multiple_of(c * CHUNK, CHUNK), CHUNK)
            dn = dcur[...]
            stc = st_ref[c]
            doc = do_ref[rows, :]
            dqe_ref[rows, :] = _dot(doc, stc)
            dv_ref[rows, :] = _dot(kd_ref[rows, :], dn, ((1,), (1,)))
            dkd_ref[rows, :] = _dot(v_ref[rows, :], dn)
            del_ref[pl.ds(c, 1), :] = jnp.sum(stc * dn, axis=0, keepdims=True)
            dcur[...] = dn * el_ref[pl.ds(c, 1), :] + _dot(doc, qe_ref[rows, :], ((0,), (0,)))
            return carry

        lax.fori_loop(0, n, step, 0)

    head = pl.BlockSpec((s, LANE), lambda h: (0, h))
    chunk = pl.BlockSpec((n, LANE), lambda h: (0, h))
    return pl.pallas_call(
        body, name=name, grid=(4,),
        in_specs=[pl.BlockSpec((s, LANE), lambda h: (0, GV + h)), head, head, chunk,
                  pl.BlockSpec((None, n, LANE, LANE), lambda h: (h, 0, 0, 0)), head],
        out_specs=[head, head, head, chunk],
        out_shape=[jax.ShapeDtypeStruct((s, G), f32)] * 3 + [jax.ShapeDtypeStruct((n, G), f32)],
        scratch_shapes=[pltpu.VMEM((LANE, LANE), f32)],
        compiler_params=_cp(("parallel",)),
    )(z, qe, kd, el, st, do)


def _gla3_specs(ts, l):
    return [pl.BlockSpec((ts, LANE), lambda h, i: (i, h)),
            pl.BlockSpec((ts, LANE), lambda h, i: (i, GG + h)),
            pl.BlockSpec((None, 1, LANE), lambda h, i: (l, 0, 0))]


def _gla3_fwd(name, o, z, og, l):
    s = z.shape[0]
    ts = _gla_ts(s)

    def body(o_ref, g_ref, og_ref, y_ref):
        y_ref[...] = _gla3_fn(o_ref[...], g_ref[...], og_ref[...]).astype(bf16)

    return pl.pallas_call(
        body, name=name, grid=(4, s // ts), in_specs=_gla3_specs(ts, l),
        out_specs=pl.BlockSpec((ts, LANE), lambda h, i: (i, h)),
        out_shape=jax.ShapeDtypeStruct((s, G), bf16), compiler_params=_cp(("parallel", "parallel")),
    )(o, z, og)


def _gla3_bwd(name, o, z, og, dy, l):
    s = z.shape[0]
    ts = _gla_ts(s)

    def body(o_ref, g_ref, og_ref, dy_ref, do_ref, dg_ref, dog_ref):
        i = pl.program_id(1)
        _, vjp = jax.vjp(_gla3_fn, o_ref[...], g_ref[...], og_ref[...])
        g = vjp(dy_ref[...])
        do_ref[...] = g[0]
        dg_ref[...] = g[1].astype(bf16)

        @pl.when(i == 0)
        def _():
            dog_ref[...] = g[2]

        @pl.when(i > 0)
        def _():
            dog_ref[...] += g[2]

    blk = pl.BlockSpec((ts, LANE), lambda h, i: (i, h))
    return pl.pallas_call(
        body, name=name, grid=(4, s // ts),
        in_specs=_gla3_specs(ts, l) + [pl.BlockSpec((ts, LANE), lambda h, i: (i, 8 + h))],
        out_specs=[blk, blk, pl.BlockSpec((None, 1, LANE), lambda h, i: (h, 0, 0))],
        out_shape=[jax.ShapeDtypeStruct((s, G), f32), jax.ShapeDtypeStruct((s, G), bf16),
                   jax.ShapeDtypeStruct((4, 1, LANE), f32)],
        compiler_params=_cp(("parallel", "arbitrary")),
    )(o, z, og, dy)


def _layer_fwd(l, h, p, weights):
    s = h.shape[0]
    tb = _tm_big(s)
    n = f"l{l}_"
    w_in = weights(l, "w_in", h)
    u = _rmsnorm_fwd(n + "norm_mix", h, p["g_mix"], l)
    z = _mm(n + "mm_in", u, w_in,
            pl.BlockSpec((tb, D), lambda j, i, k: (i, 0)), pl.BlockSpec((D, 1152), lambda j, i, k: (0, j)),
            jax.ShapeDtypeStruct((s, ZC), f32), pl.BlockSpec((tb, 1152), lambda j, i, k: (i, j)),
            (ZC // 1152, s // tb, 1), ((1,), (0,)), 1)
    w_out = weights(l, "w_out", z)
    ya, yb = _convpool_fwd(n + "convpool", z, p["conv_w"], p["pool_w"], p["pool_scale"], l)
    qe, kd, el, oi = _gla1_fwd(n + "gla_chunk", z, p["wdec"], p["bdec"], l)
    o, st = _gla2_fwd(n + "gla_scan", z, qe, kd, el, oi)
    yc = _gla3_fwd(n + "gla_out", o, z, p["gla_og"], l)
    fc, fr = _foxprep_fwd(n + "fox_prep", z, p["fox_bf"], l)
    yd = _fox_fwd(n + "fox_attn", z, fc, fr, p["fox_qg"], p["fox_kg"], l)
    y = jnp.concatenate([ya, yb, yc, yd], axis=1)
    w_gate, w_up, w_down = weights(l, "ffn", y)
    res_tile = lambda: pl.BlockSpec((tb, 1024), lambda j, i, k: (i, j))
    h1 = _mm(n + "mm_out", y, w_out.reshape(D, D),
             pl.BlockSpec((tb, D), lambda j, i, k: (i, 0)), pl.BlockSpec((D, 1024), lambda j, i, k: (0, j)),
             jax.ShapeDtypeStruct((s, D), f32), res_tile(), (2, s // tb, 1), ((1,), (0,)), 1, res=h, res_spec=res_tile())
    u2 = _rmsnorm_fwd(n + "norm_ffn", h1, p["g_ffn"], l)
    gate, up, act = _mm_gate_up(n + "mm_gate_up", u2, w_gate, w_up)
    h2 = _mm(n + "mm_down", act, w_down,
             pl.BlockSpec((tb, FB), lambda j, i, k: (i, k)),
             pl.BlockSpec((None, FB, 1024), lambda j, i, k: (k, 0, j)),
             jax.ShapeDtypeStruct((s, D), f32), res_tile(), (2, s // tb, NCHIP), ((1,), (0,)), NCHIP,
             res=h1, res_spec=res_tile())
    saved = dict(h=h, u=u, z=z, qe=qe, kd=kd, el=el, st=st, o=o, fc=fc, fr=fr, y=y, h1=h1, u2=u2,
                 gate=gate, up=up, act=act, w_in=w_in, w_out=w_out, w_gate=w_gate, w_up=w_up, w_down=w_down)
    return h2, saved


def _mm_tn(name, a, b, ta, tb, out_shape, out_spec, grid):
    s = a.shape[0]
    return _mm(name, a, b, pl.BlockSpec((s, ta), lambda i, j, k: (0, i)), pl.BlockSpec((s, tb), lambda i, j, k: (0, j)),
               out_shape, out_spec, grid, ((0,), (0,)), 1)


def _layer_bwd_ffn(l, dh2, dh2b, p, sv, ship):
    s = dh2.shape[0]
    tb = _tm_big(s)
    n = f"l{l}_bwd_"
    g_wd = _mm_tn(n + "mm_dwd", sv["act"], dh2b, FB, 1024, jax.ShapeDtypeStruct((NCHIP, FB, D), bf16),
                  pl.BlockSpec((None, FB, 1024), lambda i, j, k: (i, 0, j)), (NCHIP, 2, 1))
    dgate, dup = _mm_dact(n + "mm_dact", dh2b, sv["w_down"], sv["gate"], sv["up"])
    nt_in = lambda: (pl.BlockSpec((tb, FB), lambda j, i, k: (i, k)),
                     pl.BlockSpec((None, 1024, FB), lambda j, i, k: (k, j, 0)))
    nt_out = lambda: (jax.ShapeDtypeStruct((s, D), f32), pl.BlockSpec((tb, 1024), lambda j, i, k: (i, j)))
    du2 = _mm(n + "mm_du2_gate", dgate, sv["w_gate"], *nt_in(), *nt_out(), (2, s // tb, NCHIP), ((1,), (1,)), NCHIP)
    du2 = _mm(n + "mm_du2_up", dup, sv["w_up"], *nt_in(), *nt_out(), (2, s // tb, NCHIP), ((1,), (1,)), NCHIP,
              res=du2, res_spec=pl.BlockSpec((tb, 1024), lambda j, i, k: (i, j)))
    wg_shape = jax.ShapeDtypeStruct((NCHIP, D, FB), bf16)
    wg_spec = lambda: pl.BlockSpec((None, 1024, FB), lambda i, j, k: (j, i, 0))
    g_wg = _mm_tn(n + "mm_dwg", sv["u2"], dgate, 1024, FB, wg_shape, wg_spec(), (2, NCHIP, 1))
    g_wu = _mm_tn(n + "mm_dwu", sv["u2"], dup, 1024, FB, wg_shape, wg_spec(), (2, NCHIP, 1))
    dh1, dh1b, dg_ffn = _rmsnorm_bwd(n + "norm_ffn", sv["h1"], p["g_ffn"], du2, dh2, l)
    g_wo = _mm_tn(n + "mm_dwo", sv["y"], dh1b, G, 1024, jax.ShapeDtypeStruct((NCHIP, G, D), bf16),
                  pl.BlockSpec((None, G, 1024), lambda i, j, k: (i, 0, j)), (NCHIP, 2, 1))
    token = ship(dict(w_out=g_wo, w_gate=g_wg, w_up=g_wu, w_down=g_wd))
    dy = _mm(n + "mm_dy", dh1b, sv["w_out"].reshape(D, D),
             pl.BlockSpec((tb, D), lambda j, i, k: (i, 0)), pl.BlockSpec((1024, D), lambda j, i, k: (j, 0)),
             jax.ShapeDtypeStruct((s, D), f32), pl.BlockSpec((tb, 1024), lambda j, i, k: (i, j)),
             (2, s // tb, 1), ((1,), (1,)), 1, dep=token)
    return dh1, dy, dg_ffn


def _layer_bwd_mix(l, dh1, dy, p, sv, ship):
    s = dh1.shape[0]
    n = f"l{l}_bwd_"
    z = sv["z"]
    dcb, dcc, dch, dpu, dconv, dpoolw, dpools = _convpool_bwd(
        n + "convpool", z, p["conv_w"], p["pool_w"], p["pool_scale"], dy, l)
    do, dgg, dog = _gla3_bwd(n + "gla_out", sv["o"], z, p["gla_og"], dy, l)
    dqe, dkd, dvi, del_ = _gla2_bwd(n + "gla_scan", z, sv["qe"], sv["kd"], sv["el"], sv["st"], do)
    dgq, dgk, dgv, dmisc4, dwd, dbd = _gla1_bwd(n + "gla_chunk", z, p["wdec"], p["bdec"], dqe, dkd, del_, do, dvi, l)
    dfq, dfk, dfv, dfc4, dfr4, dqg, dkg = _fox_bwd(n + "fox_attn", z, sv["fc"], sv["fr"], p["fox_qg"], p["fox_kg"], dy, l)
    dmisc, dbf = _foxprep_bwd(n + "fox_prep", z, p["fox_bf"], dfc4, dfr4, dmisc4, l)
    dz = jnp.concatenate([dcb, dcc, dch, dpu, dgq, dgk, dgv, dgg, dfq, dfk.astype(bf16), dfv.astype(bf16), dmisc],
                         axis=1)
    g_wi = _mm_tn(n + "mm_dwi", sv["u"], dz, 1024, 1152, jax.ShapeDtypeStruct((D, ZC), bf16),
                  pl.BlockSpec((1024, 1152), lambda i, j, k: (i, j)), (2, ZC // 1152, 1))
    token = ship(dict(w_in=_win_to_blocks(g_wi)))
    tb = _tm_big(s)
    du = _mm(n + "mm_du", dz, sv["w_in"],
             pl.BlockSpec((tb, 1920), lambda j, i, k: (i, k)), pl.BlockSpec((1024, 1920), lambda j, i, k: (j, k)),
             jax.ShapeDtypeStruct((s, D), f32), pl.BlockSpec((tb, 1024), lambda j, i, k: (i, j)),
             (2, s // tb, ZC // 1920), ((1,), (1,)), ZC // 1920, dep=token)
    dh, dhb, dg_mix = _rmsnorm_bwd(n + "norm_mix", sv["h"], p["g_mix"], du, dh1, l)
    small = dict(
        norm_mix_g=dg_mix[0], conv_w=dconv, pool_w=dpoolw, pool_scale=dpools[0],
        gla_w_decay=jnp.concatenate([dwd[hh, GA_LANE:GA_LANE + 16, :64] for hh in range(4)], axis=1),
        gla_b_decay=jnp.concatenate([dbd[hh, 0, :64] for hh in range(4)]),
        gla_out_g=jnp.sum(dog[:, 0, :], axis=0), fox_q_g=jnp.sum(dqg[:, 0, :], axis=0),
        fox_k_g=jnp.sum(dkg[:, 0, :], axis=0), fox_b_f=dbf[0, FF_LANE:FF_LANE + 4])
    return dh, dhb, small


def _win_from_blocks(wb):
    def cols(a, b):
        parts = []
        for kk in range(NCHIP):
            lo, hi = max(a, kk * WINB), min(b, (kk + 1) * WINB)
            if lo < hi:
                parts.append(wb[kk, :, lo - kk * WINB:hi - kk * WINB])
        return parts

    zeros = lambda w: [jnp.zeros((wb.shape[1], w), wb.dtype)]
    segs = cols(0, 2048)
    for hh in range(4):
        segs += cols(2048 + 64 * hh, 2112 + 64 * hh) + zeros(64)
    for hh in range(4):
        segs += cols(2304 + 64 * hh, 2368 + 64 * hh) + zeros(64)
    segs += cols(2560, 3584) + cols(3600, 5136)
    segs += cols(5136, 5140) + zeros(GA_LANE - 4) + cols(3584, 3600) + zeros(LANE - GA_LANE - 16)
    return jnp.concatenate(segs, axis=-1)


def _win_to_blocks(g):
    mb = MISC * LANE
    parts = [g[:, 0:2048]]
    parts += [g[:, GQ * LANE + LANE * hh:GQ * LANE + LANE * hh + 64] for hh in range(4)]
    parts += [g[:, GK * LANE + LANE * hh:GK * LANE + LANE * hh + 64] for hh in range(4)]
    parts += [g[:, GV * LANE:FQ * LANE], g[:, mb + GA_LANE:mb + GA_LANE + 16], g[:, FQ * LANE:mb],
              g[:, mb + FF_LANE:mb + FF_LANE + 4]]
    full = jnp.concatenate(parts, axis=1)
    return full.reshape(D, NCHIP, WINB).transpose(1, 0, 2)


def _place():
    x, y, c = lax.axis_index("x"), lax.axis_index("y"), lax.axis_index("c")
    chips = [(1 - x, y), (x, 1 - y), (1 - x, 1 - y)]
    return x, y, c, chips


def _allgather_small(name, v):
    m_per, n = v.shape

    def body(x_ref, out_ref, send_sems, recv_sems, local_sem):
        x, y, c, chips = _place()
        me, sibling = (x, y, c), (x, y, 1 - c)

        def rows(px, py, pc):
            return out_ref.at[pl.ds((4 * px + 2 * py + pc) * m_per, m_per), :]

        def copy(k, block, to, src=None):
            return pltpu.make_async_remote_copy(
                src_ref=rows(*block) if src is None else src, dst_ref=rows(*block),
                send_sem=send_sems.at[k], recv_sem=recv_sems.at[k], device_id=to, device_id_type=MESH)

        mine = pltpu.make_async_copy(x_ref, rows(*me), local_sem)
        mine.start()
        first = [copy(0, me, sibling, src=x_ref)]
        first += [copy(1 + j, me, (*chip, c), src=x_ref) for j, chip in enumerate(chips)]
        for cp in first:
            cp.start()
        passed = [copy(4 + j, (*chip, c), sibling) for j, chip in enumerate(chips)]
        for j, chip in enumerate(chips):
            copy(1 + j, (*chip, c), me).wait_recv()
            passed[j].start()
        copy(0, sibling, me).wait_recv()
        for j, chip in enumerate(chips):
            copy(4 + j, (*chip, 1 - c), me).wait_recv()
        for cp in first + passed:
            cp.wait_send()
        mine.wait()

    return pl.pallas_call(
        body, name=name, out_shape=jax.ShapeDtypeStruct((8 * m_per, n), v.dtype),
        in_specs=[pl.BlockSpec(memory_space=pltpu.VMEM)], out_specs=pl.BlockSpec(memory_space=pltpu.VMEM),
        scratch_shapes=[pltpu.SemaphoreType.DMA((7,)), pltpu.SemaphoreType.DMA((7,)), pltpu.SemaphoreType.DMA],
    )(v)


def _hbm_specs(n):
    return [pl.BlockSpec(memory_space=pl.ANY)] * n


def _own_slot(shard, chip):
    return lax.dynamic_update_index_in_dim(lax.empty((NCHIP,) + shard.shape, shard.dtype), shard, chip, 0)


HBM = pl.BlockSpec(memory_space=pltpu.HBM)
SEM = pl.BlockSpec(memory_space=pltpu.SEMAPHORE)
EFFECT = pltpu.SideEffectType.DATAFLOW_SIDE_EFFECTING


def _in_hbm(v):
    return pltpu.with_memory_space_constraint(v, pltpu.HBM)


def _gather_start(name, groups):
    flat = [b for g in groups for b in g]
    nt, ng = len(flat), len(groups)

    def body(*refs):
        outs = refs[nt:]
        sems, bufs = outs[:2 * ng], outs[2 * ng:2 * ng + nt]
        x, y, c, chips = _place()
        me = 2 * x + y
        t = 0
        for gi, g in enumerate(groups):
            for k in range(len(g)):
                r = bufs[t].shape[1] // 2
                mine = bufs[t].at[me, pl.ds(c * r, r), :]
                for j, (px, py) in enumerate(chips):
                    pltpu.make_async_remote_copy(
                        src_ref=mine, dst_ref=mine, send_sem=sems[2 * gi].at[3 * k + j], recv_sem=sems[2 * gi + 1].at[3 * k + j],
                        device_id=(px, py, c), device_id_type=MESH).start()
                t += 1

    sem_shapes = []
    for g in groups:
        sem_shapes += [pltpu.SemaphoreType.DMA((3 * len(g),))] * 2
    out = pl.pallas_call(
        body, name=name,
        out_shape=tuple(sem_shapes + [pltpu.HBM(b.shape, b.dtype) for b in flat]),
        in_specs=tuple([HBM] * nt), out_specs=tuple([SEM] * (2 * ng) + [HBM] * nt),
        input_output_aliases={t: 2 * ng + t for t in range(nt)},
        compiler_params=pltpu.CompilerParams(has_side_effects=EFFECT),
    )(*[_in_hbm(b) for b in flat])
    sems = [(out[2 * gi], out[2 * gi + 1]) for gi in range(ng)]
    bufs, at = [], 2 * ng
    for g in groups:
        bufs.append(list(out[at:at + len(g)]))
        at += len(g)
    return sems, bufs


def _gather_wait(name, bufs, send_sems, recv_sems, after):
    nt = len(bufs)

    def body(*refs):
        ins, ss, rs = refs[:nt], refs[nt], refs[nt + 1]
        x, y, c, chips = _place()
        me = 2 * x + y
        for k in range(nt):
            r = ins[k].shape[1] // 2
            for j, (px, py) in enumerate(chips):
                cp = pltpu.make_async_remote_copy(
                    src_ref=ins[k].at[me, pl.ds(c * r, r), :], dst_ref=ins[k].at[2 * px + py, pl.ds(c * r, r), :],
                    send_sem=ss.at[3 * k + j], recv_sem=rs.at[3 * k + j], device_id=(px, py, c), device_id_type=MESH)
                cp.wait_send()
                cp.wait_recv()

    out = pl.pallas_call(
        body, name=name, out_shape=tuple(pltpu.HBM(b.shape, b.dtype) for b in bufs),
        in_specs=tuple([HBM] * nt + [SEM, SEM, pl.BlockSpec(memory_space=pl.ANY)]), out_specs=tuple([HBM] * nt),
        input_output_aliases={t: t for t in range(nt)},
        compiler_params=pltpu.CompilerParams(has_side_effects=EFFECT),
    )(*bufs, send_sems, recv_sems, after)
    return list(out)


def _gather_exchange(name, bufs):
    nt = len(bufs)

    def body(*refs):
        outs = refs[nt:2 * nt]
        send_sems, recv_sems = refs[2 * nt:]
        x, y, c, chips = _place()
        sibling = (x, y, 1 - c)

        def half(t, chip_idx, cc):
            r = outs[t].shape[1] // 2
            return outs[t].at[chip_idx, pl.ds(cc * r, r), :]

        sent = []
        for t in range(nt):
            for j, (px, py) in enumerate(chips):
                cp = pltpu.make_async_remote_copy(
                    src_ref=half(t, 2 * px + py, c), dst_ref=half(t, 2 * px + py, c),
                    send_sem=send_sems.at[t, j], recv_sem=recv_sems.at[t, j], device_id=sibling, device_id_type=MESH)
                cp.start()
                sent.append(cp)
        for t in range(nt):
            for j, (px, py) in enumerate(chips):
                pltpu.make_async_remote_copy(
                    src_ref=half(t, 2 * px + py, 1 - c), dst_ref=half(t, 2 * px + py, 1 - c),
                    send_sem=send_sems.at[t, j], recv_sem=recv_sems.at[t, j], device_id=sibling,
                    device_id_type=MESH).wait_recv()
        for cp in sent:
            cp.wait_send()

    return pl.pallas_call(
        body, name=name, out_shape=[jax.ShapeDtypeStruct(v.shape, v.dtype) for v in bufs],
        in_specs=_hbm_specs(nt), out_specs=_hbm_specs(nt), input_output_aliases={t: t for t in range(nt)},
        scratch_shapes=[pltpu.SemaphoreType.DMA((nt, 3)), pltpu.SemaphoreType.DMA((nt, 3))],
    )(*bufs)


def _rs_to_sibling(name, grads):
    nt = len(grads)

    def body(*refs):
        ins, outs = refs[:nt], refs[nt:2 * nt]
        send_sems, recv_sems = refs[2 * nt:]
        x, y, c, _ = _place()
        cps = []
        for t in range(nt):
            r = ins[t].shape[1] // 2
            cp = pltpu.make_async_remote_copy(
                src_ref=ins[t].at[:, pl.ds((1 - c) * r, r), :], dst_ref=outs[t],
                send_sem=send_sems.at[t], recv_sem=recv_sems.at[t], device_id=(x, y, 1 - c), device_id_type=MESH)
            cp.start()
            cps.append(cp)
        for cp in cps:
            cp.wait()

    return pl.pallas_call(
        body, name=name,
        out_shape=[jax.ShapeDtypeStruct((NCHIP, g.shape[1] // 2, g.shape[2]), g.dtype) for g in grads],
        in_specs=_hbm_specs(nt), out_specs=_hbm_specs(nt),
        scratch_shapes=[pltpu.SemaphoreType.DMA((nt,)), pltpu.SemaphoreType.DMA((nt,))],
    )(*grads)


def _rs_pair_sum(name, pos, g, other):
    r, cdim = other.shape[1], other.shape[2]
    tr = r // 4 if (r // 4) % 16 == 0 else r // 2
    nblk = r // tr

    def body(pos_ref, g_ref, o_ref, s_ref):
        s_ref[...] = (g_ref[...].astype(f32) + o_ref[...].astype(f32)).astype(bf16)

    blk = pl.BlockSpec((None, tr, cdim), lambda q, i, p: (q, i, 0))
    return pl.pallas_call(
        body, name=name, out_shape=jax.ShapeDtypeStruct(other.shape, bf16),
        grid_spec=pltpu.PrefetchScalarGridSpec(
            num_scalar_prefetch=1, grid=(NCHIP, nblk),
            in_specs=[pl.BlockSpec((None, tr, cdim), lambda q, i, p: (q, p[1] * nblk + i, 0)), blk], out_specs=blk),
        compiler_params=_cp(("parallel", "parallel")),
    )(pos, g, other)


def _scatter_start(name, sums):
    nt = len(sums)

    def body(*refs):
        outs = refs[2 * nt:]
        ss, rs, src, land, token = outs[0], outs[1], outs[2:2 + nt], outs[2 + nt:2 + 2 * nt], outs[2 + 2 * nt]
        x, y, c, chips = _place()
        me = 2 * x + y
        for t in range(nt):
            for j, (px, py) in enumerate(chips):
                pltpu.make_async_remote_copy(
                    src_ref=src[t].at[2 * px + py], dst_ref=land[t].at[me], send_sem=ss.at[3 * t + j], recv_sem=rs.at[3 * t + j],
                    device_id=(px, py, c), device_id_type=MESH).start()
        token[...] = jnp.zeros_like(token)

    shapes = [pltpu.HBM(v.shape, v.dtype) for v in sums]
    out = pl.pallas_call(
        body, name=name,
        out_shape=tuple([pltpu.SemaphoreType.DMA((3 * nt,))] * 2 + shapes + shapes + [jax.ShapeDtypeStruct((8, LANE), f32)]),
        in_specs=tuple([HBM] * (2 * nt)),
        out_specs=tuple([SEM, SEM] + [HBM] * (2 * nt) + [pl.BlockSpec(memory_space=pltpu.VMEM)]),
        input_output_aliases={t: 2 + t for t in range(2 * nt)},
        compiler_params=pltpu.CompilerParams(has_side_effects=EFFECT),
    )(*[_in_hbm(v) for v in sums], *[_in_hbm(lax.empty(v.shape, v.dtype)) for v in sums])
    return out[0], out[1], list(out[2:2 + nt]), list(out[2 + nt:2 + 2 * nt]), out[2 + 2 * nt]


def _scatter_wait(name, sums, land, send_sems, recv_sems, after):
    nt = len(sums)

    def body(*refs):
        src, dst, ss, rs = refs[:nt], refs[nt:2 * nt], refs[2 * nt], refs[2 * nt + 1]
        x, y, c, chips = _place()
        for t in range(nt):
            for j, (px, py) in enumerate(chips):
                cp = pltpu.make_async_remote_copy(
                    src_ref=src[t].at[2 * px + py], dst_ref=dst[t].at[2 * px + py], send_sem=ss.at[3 * t + j],
                    recv_sem=rs.at[3 * t + j], device_id=(px, py, c), device_id_type=MESH)
                cp.wait_send()
                cp.wait_recv()

    shapes = [pltpu.HBM(v.shape, v.dtype) for v in sums]
    out = pl.pallas_call(
        body, name=name, out_shape=tuple(shapes + shapes),
        in_specs=tuple([HBM] * (2 * nt) + [SEM, SEM, pl.BlockSpec(memory_space=pl.ANY)]),
        out_specs=tuple([HBM] * (2 * nt)), input_output_aliases={t: t for t in range(2 * nt)},
        compiler_params=pltpu.CompilerParams(has_side_effects=EFFECT),
    )(*sums, *land, send_sems, recv_sems, after)
    return list(out[:nt]), list(out[nt:])


def _rs_chip_sum(name, pos, sums, parts):
    r, cdim = parts.shape[1], parts.shape[2]
    tr = r // 4 if (r // 4) % 16 == 0 else r // 2
    nblk = r // tr

    def body(pos_ref, own_ref, a_ref, b_ref, c_ref, o_ref):
        o_ref[...] = ((own_ref[...].astype(f32) + a_ref[...].astype(f32)) + b_ref[...].astype(f32)) \
            + c_ref[...].astype(f32)

    def slot(k):
        return pl.BlockSpec((None, tr, cdim), lambda i, p: ((p[0] + k) % NCHIP, i, 0))

    return pl.pallas_call(
        body, name=name, out_shape=jax.ShapeDtypeStruct((2 * r, cdim), f32),
        grid_spec=pltpu.PrefetchScalarGridSpec(
            num_scalar_prefetch=1, grid=(nblk,), in_specs=[slot(0), slot(1), slot(2), slot(3)],
            out_specs=pl.BlockSpec((tr, cdim), lambda i, p: (p[1] * nblk + i, 0))),
        compiler_params=_cp(("parallel",)),
    )(pos, sums, parts, parts, parts)


def _rs_share_halves(name, bufs):
    nt = len(bufs)

    def body(*refs):
        outs = refs[nt:2 * nt]
        send_sems, recv_sems = refs[2 * nt:]
        x, y, c, _ = _place()
        cps = []
        for t in range(nt):
            r = outs[t].shape[0] // 2
            mine = outs[t].at[pl.ds(c * r, r), :]
            theirs = outs[t].at[pl.ds((1 - c) * r, r), :]
            cp = pltpu.make_async_remote_copy(
                src_ref=mine, dst_ref=mine, send_sem=send_sems.at[t], recv_sem=recv_sems.at[t],
                device_id=(x, y, 1 - c), device_id_type=MESH)
            cp.start()
            cps.append((cp, theirs))
        for t, (cp, theirs) in enumerate(cps):
            pltpu.make_async_remote_copy(
                src_ref=theirs, dst_ref=theirs, send_sem=send_sems.at[t], recv_sem=recv_sems.at[t],
                device_id=(x, y, 1 - c), device_id_type=MESH).wait_recv()
            cp.wait_send()

    return pl.pallas_call(
        body, name=name, out_shape=[jax.ShapeDtypeStruct(v.shape, v.dtype) for v in bufs],
        in_specs=_hbm_specs(nt), out_specs=_hbm_specs(nt), input_output_aliases={t: t for t in range(nt)},
        scratch_shapes=[pltpu.SemaphoreType.DMA((nt,)), pltpu.SemaphoreType.DMA((nt,))],
    )(*bufs)


BIG = ("w_in", "w_out", "w_gate", "w_up", "w_down")


GROUPS = (("w_in",), ("w_out",), ("w_gate", "w_up", "w_down"))


def _rs_begin(tag, keys, grads, pos):
    got = _rs_to_sibling(tag + "to_sibling", grads)
    sums = [_rs_pair_sum(tag + "pair_sum_" + k, pos, g, o) for k, g, o in zip(keys, grads, got)]
    return (keys,) + _scatter_start(tag + "start", sums)


def _rs_end(tag, state, pos, after):
    keys, send_sems, recv_sems, sums, land, _ = state
    sums, land = _scatter_wait(tag + "wait", sums, land, send_sems, recv_sems, after)
    halves = [_rs_chip_sum(tag + "chip_sum_" + k, pos, s, v) for k, s, v in zip(keys, sums, land)]
    return dict(zip(keys, _rs_share_halves(tag + "share", halves)))


def _adam_math(w, g, m, v):
    m = ADAM_B1 * m + (1.0 - ADAM_B1) * g
    v = ADAM_B2 * v + (1.0 - ADAM_B2) * (g * g)
    m_hat = m / (1.0 - ADAM_B1 ** ADAM_STEP)
    v_hat = v / (1.0 - ADAM_B2 ** ADAM_STEP)
    delta = -ADAM_LR * (m_hat / (jnp.sqrt(v_hat) + ADAM_EPS) + ADAM_WD * w)
    return delta, m, v


def _adam_big(name, g0, g1, w, m, v):
    _, r, cdim = w.shape
    tr = 128 if r % 128 == 0 else 64
    nb = r // tr

    def body(g0_ref, g1_ref, w_ref, m_ref, v_ref, go_ref, d_ref, mo_ref, vo_ref):
        l = pl.program_id(0)
        g = jnp.where(l == 0, g0_ref[...], g1_ref[...])
        delta, mn, vn = _adam_math(w_ref[...], g, m_ref[...], v_ref[...])
        go_ref[...] = g
        d_ref[...] = delta
        mo_ref[...] = mn
        vo_ref[...] = vn

    lay = pl.BlockSpec((None, tr, cdim), lambda l, i: (l, i, 0))
    return pl.pallas_call(
        body, name=name, grid=(2, nb),
        in_specs=[pl.BlockSpec((tr, cdim), lambda l, i: (i * (1 - l) + (nb - 1) * l, 0)),
                  pl.BlockSpec((tr, cdim), lambda l, i: (i * l, 0)), lay, lay, lay],
        out_specs=[lay] * 4, out_shape=[jax.ShapeDtypeStruct(w.shape, f32)] * 4,
        compiler_params=_cp(("arbitrary", "arbitrary")),
    )(g0, g1, w, m, v)


def _sum8(name, gathered):
    m_per = gathered.shape[0] // 8

    def body(g_ref, o_ref):
        tot = g_ref[pl.ds(0, m_per), :]
        for d in range(1, 8):
            tot = tot + g_ref[pl.ds(d * m_per, m_per), :]
        o_ref[...] = tot

    return pl.pallas_call(body, name=name, out_shape=jax.ShapeDtypeStruct((m_per, LANE), f32))(gathered)


def _adam_small(name, g, w, m, v):
    def body(g_ref, w_ref, m_ref, v_ref, d_ref, mo_ref, vo_ref):
        delta, mn, vn = _adam_math(w_ref[...], g_ref[...], m_ref[...], v_ref[...])
        d_ref[...] = delta
        mo_ref[...] = mn
        vo_ref[...] = vn

    return pl.pallas_call(body, name=name, out_shape=[jax.ShapeDtypeStruct(g.shape, f32)] * 3)(g, w, m, v)


def _pack(vals):
    rows, offs, at = [], [], 0
    for a in vals:
        a = a.reshape(-1)
        n = -(-a.shape[0] // (8 * LANE)) * 8
        rows.append(jnp.pad(a, (0, n * LANE - a.shape[0])).reshape(n, LANE))
        offs.append(at)
        at += n
    return jnp.concatenate(rows, axis=0), offs


def _unpack(packed, offs, shapes):
    out = []
    for o, shp in zip(offs, shapes):
        size = 1
        for d in shp:
            size *= d
        n = -(-size // LANE)
        out.append(packed[o:o + n].reshape(-1)[:size].reshape(shp))
    return out


SMALL = ("norm_mix_g", "conv_w", "pool_w", "pool_scale", "gla_w_decay", "gla_b_decay", "gla_out_g",
         "fox_q_g", "fox_k_g", "fox_b_f", "norm_ffn_g")
ALL = ("norm_mix_g", "w_in", "conv_w", "pool_w", "pool_scale", "gla_w_decay", "gla_b_decay", "gla_out_g",
       "fox_q_g", "fox_k_g", "fox_b_f", "w_out", "norm_ffn_g", "w_gate", "w_up", "w_down")


def kernel(x, norm_mix_g, w_in, conv_w, pool_w, pool_scale, gla_w_decay, gla_b_decay, gla_out_g, fox_q_g, fox_k_g, fox_b_f, w_out, norm_ffn_g, w_gate, w_up, w_down, loss_target, m_norm_mix_g, m_w_in, m_conv_w, m_pool_w, m_pool_scale, m_gla_w_decay, m_gla_b_decay, m_gla_out_g, m_fox_q_g, m_fox_k_g, m_fox_b_f, m_w_out, m_norm_ffn_g, m_w_gate, m_w_up, m_w_down, v_norm_mix_g, v_w_in, v_conv_w, v_pool_w, v_pool_scale, v_gla_w_decay, v_gla_b_decay, v_gla_out_g, v_fox_q_g, v_fox_k_g, v_fox_b_f, v_w_out, v_norm_ffn_g, v_w_gate, v_w_up, v_w_down):
    w = dict(norm_mix_g=norm_mix_g, w_in=w_in, conv_w=conv_w, pool_w=pool_w, pool_scale=pool_scale,
             gla_w_decay=gla_w_decay, gla_b_decay=gla_b_decay, gla_out_g=gla_out_g, fox_q_g=fox_q_g, fox_k_g=fox_k_g,
             fox_b_f=fox_b_f, w_out=w_out, norm_ffn_g=norm_ffn_g, w_gate=w_gate, w_up=w_up, w_down=w_down)
    m = dict(norm_mix_g=m_norm_mix_g, w_in=m_w_in, conv_w=m_conv_w, pool_w=m_pool_w, pool_scale=m_pool_scale,
             gla_w_decay=m_gla_w_decay, gla_b_decay=m_gla_b_decay, gla_out_g=m_gla_out_g, fox_q_g=m_fox_q_g,
             fox_k_g=m_fox_k_g, fox_b_f=m_fox_b_f, w_out=m_w_out, norm_ffn_g=m_norm_ffn_g, w_gate=m_w_gate,
             w_up=m_w_up, w_down=m_w_down)
    v = dict(norm_mix_g=v_norm_mix_g, w_in=v_w_in, conv_w=v_conv_w, pool_w=v_pool_w, pool_scale=v_pool_scale,
             gla_w_decay=v_gla_w_decay, gla_b_decay=v_gla_b_decay, gla_out_g=v_gla_out_g, fox_q_g=v_fox_q_g,
             fox_k_g=v_fox_k_g, fox_b_f=v_fox_b_f, w_out=v_w_out, norm_ffn_g=v_norm_ffn_g, w_gate=v_w_gate,
             w_up=v_w_up, w_down=v_w_down)
    chip = 2 * lax.axis_index("x") + lax.axis_index("y")

    pos = jnp.stack([chip, lax.axis_index("c")]).astype(jnp.int32)

    mine, offs = _pack([conv_w, gla_w_decay, jnp.zeros((8, LANE), f32)])
    order = [(l, grp) for l in range(2) for grp in GROUPS]
    sems, gbufs = _gather_start(
        "gather_start",
        [[_own_slot(mine, chip)]] + [[_own_slot(w[k][l].astype(bf16), chip) for k in grp] for l, grp in order])

    def gathered(tag, gi, after):
        got = _gather_wait(tag + "_wait", gbufs[gi], sems[gi][0], sems[gi][1], after)
        return _gather_exchange(tag + "_exchange", got)

    def weights(l, group, after):
        got = gathered(f"l{l}_gather_{group}", 1 + 3 * l + ("w_in", "w_out", "ffn").index(group), after)
        if group == "w_in":
            return _win_from_blocks(got[0])
        return got[0] if group == "w_out" else got

    every = gathered("gather_small", 0, x)[0]
    per_chip = [_unpack(every[kk], offs, [conv_w.shape, gla_w_decay.shape]) for kk in range(NCHIP)]
    conv_full = jnp.concatenate([pc[0] for pc in per_chip], axis=-1)[:, :, 0, :]
    wdec_full = jnp.concatenate([pc[1] for pc in per_chip], axis=-1)

    wdec = jnp.pad(wdec_full.reshape(2, 16, 4, 64), ((0, 0), (GA_LANE, LANE - GA_LANE - 16), (0, 0), (0, 64)))
    p = dict(
        g_mix=norm_mix_g[:, None, :], g_ffn=norm_ffn_g[:, None, :],
        conv_w=conv_full, pool_w=pool_w, pool_scale=pool_scale[:, None, :],
        wdec=wdec.reshape(2, LANE, G),
        bdec=jnp.pad(gla_b_decay.reshape(2, 4, 64), ((0, 0), (0, 0), (0, 64))).reshape(2, 1, G),
        gla_og=gla_out_g[:, None, :], fox_qg=fox_q_g[:, None, :], fox_kg=fox_k_g[:, None, :],
        fox_bf=jnp.pad(fox_b_f, ((0, 0), (FF_LANE, LANE - FF_LANE - 4)))[:, None, :])

    h0 = x[0]
    h1, sv0 = _layer_fwd(0, h0, p, weights)
    h2, sv1 = _layer_fwd(1, h1, p, weights)
    sq, dh, dhb = _loss("loss", h2, loss_target[0])
    loss = lax.psum(sq[0, 0] * (0.5 / D), ("x", "y", "c"))

    late = ("w_out", "w_gate", "w_up", "w_down")
    shipped = {}

    def ship(tag):
        def start(grads_by_name):
            keys = tuple(grads_by_name)
            shipped[tag] = _rs_begin(tag, keys, [grads_by_name[k] for k in keys], pos)
            return shipped[tag][-1]
        return start

    dh1, dy, dgf1 = _layer_bwd_ffn(1, dh, dhb, p, sv1, ship("l1_rs_a_"))
    dh, dhb, small1 = _layer_bwd_mix(1, dh1, dy, p, sv1, ship("l1_rs_b_"))
    dh1, dy, dgf0 = _layer_bwd_ffn(0, dh, dhb, p, sv0, ship("l0_rs_a_"))
    dh, dhb, small0 = _layer_bwd_mix(0, dh1, dy, p, sv0, ship("l0_rs_b_"))
    small0["norm_ffn_g"], small1["norm_ffn_g"] = dgf0[0], dgf1[0]

    red1 = _rs_end("l1_rs_a_", shipped["l1_rs_a_"], pos, dh)
    red1.update(_rs_end("l1_rs_b_", shipped["l1_rs_b_"], pos, red1["w_down"]))
    red0 = _rs_end("l0_rs_a_", shipped["l0_rs_a_"], pos, red1["w_in"])
    grads, deltas, new_m, new_v = {}, {}, {}, {}
    for k in late:
        grads[k], deltas[k], new_m[k], new_v[k] = _adam_big("adam_" + k, red0[k], red1[k], w[k], m[k], v[k])

    packed, goffs = _pack([jnp.stack([small0[k], small1[k]]) for k in SMALL])
    total = _sum8("sum_small_grads", _allgather_small("gather_small_grads", packed))
    red0.update(_rs_end("l0_rs_b_", shipped["l0_rs_b_"], pos, total))
    k = "w_in"
    grads[k], deltas[k], new_m[k], new_v[k] = _adam_big("adam_" + k, red0[k], red1[k], w[k], m[k], v[k])
    full_shapes = [(2,) + small0[k].shape for k in SMALL]
    gsmall = dict(zip(SMALL, _unpack(total, goffs, full_shapes)))
    gsmall["conv_w"] = lax.dynamic_slice_in_dim(gsmall["conv_w"], chip * LANE, LANE, axis=2)[:, :, None, :]
    gsmall["gla_w_decay"] = lax.dynamic_slice_in_dim(gsmall["gla_w_decay"], chip * 64, 64, axis=2)
    gp, loffs = _pack([gsmall[k] for k in SMALL])
    wp, _ = _pack([w[k] for k in SMALL])
    mp, _ = _pack([m[k] for k in SMALL])
    vp, _ = _pack([v[k] for k in SMALL])
    dp, mnp, vnp = _adam_small("adam_small", gp, wp, mp, vp)
    shapes = [w[k].shape for k in SMALL]
    for k, a, b, c_, d_ in zip(SMALL, _unpack(gp, loffs, shapes), _unpack(dp, loffs, shapes),
                               _unpack(mnp, loffs, shapes), _unpack(vnp, loffs, shapes)):
        grads[k], deltas[k], new_m[k], new_v[k] = a, b, c_, d_

    return (loss, dh[None], *[grads[k] for k in ALL], *[deltas[k] for k in ALL],
            *[new_m[k] for k in ALL], *[new_v[k] for k in ALL])
```

```python
import functools

import jax
import jax.numpy as jnp
from jax import lax
from jax.experimental import pallas as pl
from jax.experimental.pallas import tpu as pltpu

f32 = jnp.float32
bf16 = jnp.bfloat16

D = 2048
G = 512
DFF = 5632
NCHIP = 4
FB = DFF // NCHIP
WIN = 5140
WINB = WIN // NCHIP
EPS = 1e-6
CHUNK = 64
LANE = 128

CB, CC, CH, PU, GQ, GK, GV, GG, FQ, FK, FV, MISC = 0, 4, 8, 12, 16, 20, 24, 28, 32, 36, 40, 44
ZC = 45 * LANE
FF_LANE = 0
GA_LANE = 8

ADAM_LR, ADAM_B1, ADAM_B2, ADAM_EPS, ADAM_WD, ADAM_STEP = 0.001, 0.9, 0.999, 1e-08, 0.01, 10

VMEM_LIMIT = 60 * 1024 * 1024
MESH = pl.DeviceIdType.MESH


def _cp(sem=None):
    return pltpu.CompilerParams(dimension_semantics=sem, vmem_limit_bytes=VMEM_LIMIT)


def _dot(a, b, dims=((1,), (0,))):
    return lax.dot_general(a.astype(bf16), b.astype(bf16), (dims, ((), ())), preferred_element_type=f32)


def _bdot(a, b, ca, cb):
    return lax.dot_general(a.astype(bf16), b.astype(bf16), (((ca,), (cb,)), ((0,), (0,))),
                           preferred_element_type=f32)


def _log_sigmoid(x):
    return jnp.minimum(x, 0.0) - jnp.log(1.0 + jnp.exp(-jnp.abs(x)))


@jax.custom_vjp
def _sigmoid(x):
    return 1.0 / (1.0 + jnp.exp(-x))


def _sigmoid_fwd(x):
    s = _sigmoid(x)
    return s, s


def _sigmoid_bwd(s, g):
    return (g * s * (1.0 - s),)


_sigmoid.defvjp(_sigmoid_fwd, _sigmoid_bwd)


def _rms(x, g):
    return x * lax.rsqrt(jnp.mean(x * x, axis=-1, keepdims=True) + EPS) * g


def _shift_impl(x, n, period, transpose):
    rows = x.shape[0]
    t = lax.broadcasted_iota(jnp.int32, x.shape, 0)
    if period is not None:
        t = t & (period - 1)
    keep = t >= n
    if not transpose:
        return jnp.where(keep, pltpu.roll(x, n, 0), 0.0)
    return pltpu.roll(jnp.where(keep, x, 0.0), rows - n, 0)


def _shift(x, n, period=None):
    @jax.custom_vjp
    def f(v):
        return _shift_impl(v, n, period, False)

    def fwd(v):
        return f(v), None

    def bwd(_, g):
        return (_shift_impl(g, n, period, True),)

    f.defvjp(fwd, bwd)
    return f(x)


def _cumsum_rows(x, length, period=None):
    n = 1
    while n < length:
        x = x + _shift(x, n, period)
        n *= 2
    return x


def _convpool_fn(cb, cc, ch, pu, w0, w1, w2, pw, ps, j):
    u = cc * ch
    y = w2 * u + w1 * _shift(u, 1) + w0 * _shift(u, 2)
    ya = cb * y
    s2 = pu + _shift(pu, 1)
    s4 = s2 + _shift(s2, 2)
    s8 = s4 + _shift(s4, 4)
    s16 = s8 + _shift(s8, 8)
    wsum = jnp.where(j == 0, s2, jnp.where(j == 1, s4, jnp.where(j == 2, s8, s16)))
    width = (2 << j).astype(f32)
    t = lax.broadcasted_iota(jnp.int32, pu.shape, 0).astype(f32)
    count = jnp.minimum(t + 1.0, width)
    d = wsum / count - pu
    yb = _dot(d, pw) * ps
    return ya, yb


def _foxprep_fn(misc, bf):
    lf = _log_sigmoid(misc + bf)
    fc = _cumsum_rows(lf, lf.shape[0])
    return fc, jnp.transpose(fc)


def _fox_fn(q, k, v, fcol, frow8, qg, kg, h, i):
    tq, s = q.shape[0], k.shape[0]
    qn = _rms(q, qg)
    kn = _rms(k, kg)
    lg = _dot(qn, kn, ((1,), (1,))) * (LANE ** -0.5)
    lane = lax.broadcasted_iota(jnp.int32, fcol.shape, 1)
    fq = jnp.sum(jnp.where(lane == h, fcol, 0.0), axis=1, keepdims=True)
    row = lax.broadcasted_iota(jnp.int32, frow8.shape, 0)
    fk = jnp.sum(jnp.where(row == h, frow8, 0.0), axis=0, keepdims=True)
    lg = lg + fq - fk
    qpos = i * tq + lax.broadcasted_iota(jnp.int32, (tq, s), 0)
    kpos = lax.broadcasted_iota(jnp.int32, (tq, s), 1)
    lg = jnp.where(kpos <= qpos, lg, -jnp.inf)
    m = lax.stop_gradient(jnp.max(lg, axis=1, keepdims=True))
    e = jnp.exp(lg - m)
    p = e / jnp.sum(e, axis=1, keepdims=True)
    return _dot(p, v)


def _gla1_fn(q, k, v, misc, wd, bd):
    ts = q.shape[0]
    nb = ts // CHUNK
    x = _dot(misc, wd) + bd
    la = _log_sigmoid(x) * (1.0 / 16.0)
    cc = _cumsum_rows(la, CHUNK, CHUNK)
    la3 = la.reshape(nb, CHUNK, LANE)
    last3 = jnp.sum(la3, axis=1, keepdims=True)
    last2 = jnp.sum(la3, axis=1)
    cc3 = cc.reshape(nb, CHUNK, LANE)
    q3 = (q * 0.125).reshape(nb, CHUNK, LANE)
    k3 = k.reshape(nb, CHUNK, LANE)
    v3 = v.reshape(nb, CHUNK, LANE)
    ep = jnp.exp(cc3)
    en = jnp.exp(-cc3)
    qe = q3 * ep
    a1 = _bdot(qe, k3 * en, 2, 2)
    a2 = _bdot(q3 * en, k3 * ep, 2, 2)
    ti = lax.broadcasted_iota(jnp.int32, a1.shape, 1)
    si = lax.broadcasted_iota(jnp.int32, a1.shape, 2)
    sc = jnp.where(si <= ti, a1, a2)
    oi = _bdot(sc, v3, 2, 1)
    kd = k3 * jnp.exp(last3 - cc3)
    el = jnp.exp(last2)
    return qe.reshape(ts, LANE), kd.reshape(ts, LANE), el, oi.reshape(ts, LANE)


def _gla3_fn(o, gg, og):
    return _rms(o, og) * (gg * _sigmoid(gg))


def _ffn_fn(gate, up):
    return gate * _sigmoid(gate) * up


def _mm(name, a, b, a_spec, b_spec, out_shape, out_spec, grid, dims, nk, res=None, res_spec=None, dep=None):
    has_res = res is not None
    has_dep = dep is not None
    nax = len(grid)

    def body(*refs):
        a_ref, b_ref = refs[0], refs[1]
        res_ref = refs[2] if has_res else None
        out_ref = refs[2 + has_res + has_dep]
        part = _dot(a_ref[...], b_ref[...], dims)
        if nk == 1:
            if has_res:
                part = part + res_ref[...]
            out_ref[...] = part.astype(out_ref.dtype)
            return
        acc_ref = refs[3 + has_res + has_dep]
        k = pl.program_id(nax - 1)

        @pl.when(k == 0)
        def _():
            acc_ref[...] = part

        @pl.when(k > 0)
        def _():
            acc_ref[...] += part

        @pl.when(k == nk - 1)
        def _():
            tot = acc_ref[...]
            if has_res:
                tot = tot + res_ref[...]
            out_ref[...] = tot.astype(out_ref.dtype)

    ops = [a, b] + ([res] if has_res else []) + ([dep] if has_dep else [])
    specs = [a_spec, b_spec] + ([res_spec] if has_res else [])
    if has_dep:
        specs.append(pl.BlockSpec((8, LANE), lambda *_: (0, 0)))
    blk = tuple(d for d in out_spec.block_shape if d is not None)
    scratch = [pltpu.VMEM(blk, f32)] if nk > 1 else []
    return pl.pallas_call(
        body, name=name, grid=grid, in_specs=specs, out_specs=out_spec, out_shape=out_shape,
        scratch_shapes=scratch,
        compiler_params=_cp(("parallel",) * (nax - 1) + ("arbitrary",)),
    )(*ops)


def _tm(s):
    return min(s, 512)


def _tm_big(s):
    return min(s, 1024)


def _mm_gate_up(name, u, w_gate, w_up):
    s = u.shape[0]
    tm = _tm(s)

    def body(u_ref, wg_ref, wu_ref, g_ref, up_ref, a_ref):
        g = _dot(u_ref[...], wg_ref[...])
        up = _dot(u_ref[...], wu_ref[...])
        g_ref[...] = g
        up_ref[...] = up
        a_ref[...] = _ffn_fn(g, up).astype(bf16)

    wspec = pl.BlockSpec((None, D, FB), lambda j, i: (j, 0, 0))
    tile = pl.BlockSpec((tm, FB), lambda j, i: (i, j))
    return pl.pallas_call(
        body, name=name, grid=(NCHIP, s // tm),
        in_specs=[pl.BlockSpec((tm, D), lambda j, i: (i, 0)), wspec, wspec], out_specs=[tile, tile, tile],
        out_shape=[jax.ShapeDtypeStruct((s, DFF), f32), jax.ShapeDtypeStruct((s, DFF), f32),
                   jax.ShapeDtypeStruct((s, DFF), bf16)],
        compiler_params=_cp(("parallel", "parallel")),
    )(u, w_gate, w_up)


def _mm_dact(name, dh, w_down, gate, up):
    s = dh.shape[0]
    tm = _tm(s)

    def body(dh_ref, wd_ref, g_ref, up_ref, dg_ref, du_ref):
        dact = _dot(dh_ref[...], wd_ref[...], ((1,), (1,)))
        _, vjp = jax.vjp(_ffn_fn, g_ref[...], up_ref[...])
        dg, du = vjp(dact)
        dg_ref[...] = dg.astype(bf16)
        du_ref[...] = du.astype(bf16)

    tile = pl.BlockSpec((tm, FB), lambda j, i: (i, j))
    return pl.pallas_call(
        body, name=name, grid=(NCHIP, s // tm),
        in_specs=[pl.BlockSpec((tm, D), lambda j, i: (i, 0)), pl.BlockSpec((None, FB, D), lambda j, i: (j, 0, 0)),
                  tile, tile],
        out_specs=[tile, tile], out_shape=[jax.ShapeDtypeStruct((s, DFF), bf16)] * 2,
        compiler_params=_cp(("parallel", "parallel")),
    )(dh, w_down, gate, up)


def _rmsnorm_fwd(name, x, g, l):
    s = x.shape[0]
    tm = min(s, 256)

    def body(x_ref, g_ref, u_ref):
        u_ref[...] = _rms(x_ref[...], g_ref[...]).astype(bf16)

    return pl.pallas_call(
        body, name=name, grid=(s // tm,),
        in_specs=[pl.BlockSpec((tm, D), lambda i: (i, 0)), pl.BlockSpec((None, 1, D), lambda i: (l, 0, 0))],
        out_specs=pl.BlockSpec((tm, D), lambda i: (i, 0)),
        out_shape=jax.ShapeDtypeStruct((s, D), bf16), compiler_params=_cp(("parallel",)),
    )(x, g)


def _rmsnorm_bwd(name, x, g, du, dres, l):
    s = x.shape[0]
    tm = min(s, 256)

    def body(x_ref, g_ref, du_ref, dres_ref, dx_ref, dxb_ref, dg_ref):
        _, vjp = jax.vjp(_rms, x_ref[...], g_ref[...])
        dx, dg = vjp(du_ref[...])
        tot = dx + dres_ref[...]
        dx_ref[...] = tot
        dxb_ref[...] = tot.astype(bf16)

        @pl.when(pl.program_id(0) == 0)
        def _():
            dg_ref[...] = dg

        @pl.when(pl.program_id(0) > 0)
        def _():
            dg_ref[...] += dg

    row = pl.BlockSpec((tm, D), lambda i: (i, 0))
    return pl.pallas_call(
        body, name=name, grid=(s // tm,),
        in_specs=[row, pl.BlockSpec((None, 1, D), lambda i: (l, 0, 0)), row, row],
        out_specs=[row, row, pl.BlockSpec((1, D), lambda i: (0, 0))],
        out_shape=[jax.ShapeDtypeStruct((s, D), f32), jax.ShapeDtypeStruct((s, D), bf16),
                   jax.ShapeDtypeStruct((1, D), f32)],
        compiler_params=_cp(("arbitrary",)),
    )(x, g, du, dres)


def _loss(name, y, t):
    s = y.shape[0]
    tm = min(s, 256)
    row = pl.BlockSpec((tm, D), lambda i: (i, 0))

    def body(y_ref, t_ref, l_ref, d_ref, db_ref):
        e = y_ref[...] - t_ref[...]
        d = e * (1.0 / D)
        d_ref[...] = d
        db_ref[...] = d.astype(bf16)
        part = jnp.zeros((8, LANE), f32) + jnp.sum(e * e)

        @pl.when(pl.program_id(0) == 0)
        def _():
            l_ref[...] = part

        @pl.when(pl.program_id(0) > 0)
        def _():
            l_ref[...] += part

    return pl.pallas_call(
        body, name=name, grid=(s // tm,), in_specs=[row, row],
        out_specs=[pl.BlockSpec((8, LANE), lambda i: (0, 0)), row, row],
        out_shape=[jax.ShapeDtypeStruct((8, LANE), f32), jax.ShapeDtypeStruct((s, D), f32),
                   jax.ShapeDtypeStruct((s, D), bf16)],
        compiler_params=_cp(("arbitrary",)),
    )(y, t)


def _zspec(s, blk):
    return pl.BlockSpec((s, LANE), lambda j: (0, blk + j))


def _convpool_specs(s, l):
    return [_zspec(s, CB), _zspec(s, CC), _zspec(s, CH), _zspec(s, PU),
            pl.BlockSpec((None, 3, LANE), lambda j: (l, 0, j)),
            pl.BlockSpec((None, None, LANE, LANE), lambda j: (l, j, 0, 0)),
            pl.BlockSpec((None, 1, LANE), lambda j: (l, 0, j))]


def _convpool_fwd(name, z, conv_w, pool_w, pool_scale, l):
    s = z.shape[0]

    def body(cb, cc, ch, pu, cw, pw, ps, ya_ref, yb_ref):
        ya, yb = _convpool_fn(cb[...], cc[...], ch[...], pu[...], cw[0:1, :], cw[1:2, :], cw[2:3, :], pw[...], ps[...],
                              pl.program_id(0))
        ya_ref[...] = ya.astype(bf16)
        yb_ref[...] = yb.astype(bf16)

    col = pl.BlockSpec((s, LANE), lambda j: (0, j))
    return pl.pallas_call(
        body, name=name, grid=(4,), in_specs=_convpool_specs(s, l), out_specs=[col, col],
        out_shape=[jax.ShapeDtypeStruct((s, G), bf16)] * 2, compiler_params=_cp(("parallel",)),
    )(z, z, z, z, conv_w, pool_w, pool_scale)


def _convpool_bwd(name, z, conv_w, pool_w, pool_scale, dy, l):
    s = z.shape[0]

    def body(cb, cc, ch, pu, cw, pw, ps, dya, dyb, dcb, dcc, dch, dpu, dcw, dpw, dps):
        j = pl.program_id(0)
        fn = functools.partial(_convpool_fn, j=j)
        _, vjp = jax.vjp(fn, cb[...], cc[...], ch[...], pu[...], cw[0:1, :], cw[1:2, :], cw[2:3, :], pw[...], ps[...])
        g = vjp((dya[...], dyb[...]))
        dcb[...] = g[0].astype(bf16)
        dcc[...] = g[1].astype(bf16)
        dch[...] = g[2].astype(bf16)
        dpu[...] = g[3].astype(bf16)
        dcw[0:1, :] = g[4]
        dcw[1:2, :] = g[5]
        dcw[2:3, :] = g[6]
        dpw[...] = g[7]
        dps[...] = g[8]

    col = pl.BlockSpec((s, LANE), lambda j: (0, j))
    specs = _convpool_specs(s, l) + [pl.BlockSpec((s, LANE), lambda j: (0, j)),
                                     pl.BlockSpec((s, LANE), lambda j: (0, 4 + j))]
    return pl.pallas_call(
        body, name=name, grid=(4,), in_specs=specs,
        out_specs=[col, col, col, col, pl.BlockSpec((3, LANE), lambda j: (0, j)),
                   pl.BlockSpec((None, LANE, LANE), lambda j: (j, 0, 0)), pl.BlockSpec((1, LANE), lambda j: (0, j))],
        out_shape=[jax.ShapeDtypeStruct((s, G), bf16)] * 4 + [
            jax.ShapeDtypeStruct((3, G), f32), jax.ShapeDtypeStruct((4, LANE, LANE), f32),
            jax.ShapeDtypeStruct((1, G), f32)],
        compiler_params=_cp(("parallel",)),
    )(z, z, z, z, conv_w, pool_w, pool_scale, dy, dy)


def _foxprep_fwd(name, z, bf, l):
    s = z.shape[0]

    def body(m_ref, b_ref, fc_ref, fr_ref):
        fc, fr = _foxprep_fn(m_ref[...], b_ref[...])
        fc_ref[...] = fc
        fr_ref[...] = fr

    return pl.pallas_call(
        body, name=name, grid=(1,),
        in_specs=[pl.BlockSpec((s, LANE), lambda i: (0, MISC)), pl.BlockSpec((None, 1, LANE), lambda i: (l, 0, 0))],
        out_specs=[pl.BlockSpec((s, LANE), lambda i: (0, 0)), pl.BlockSpec((LANE, s), lambda i: (0, 0))],
        out_shape=[jax.ShapeDtypeStruct((s, LANE), f32), jax.ShapeDtypeStruct((LANE, s), f32)],
        compiler_params=_cp(("arbitrary",)),
    )(z, bf)


def _foxprep_bwd(name, z, bf, dfc4, dfr4, dmisc4, l):
    s = z.shape[0]

    def body(m_ref, b_ref, dfc_ref, dfr_ref, dm4_ref, dm_ref, db_ref):
        _, vjp = jax.vjp(_foxprep_fn, m_ref[...], b_ref[...])
        dfc = dfc_ref[0] + dfc_ref[1] + dfc_ref[2] + dfc_ref[3]
        dfr = dfr_ref[0] + dfr_ref[1] + dfr_ref[2] + dfr_ref[3]
        dfr = jnp.concatenate([dfr, jnp.zeros((LANE - 8, s), f32)], axis=0)
        dm, db = vjp((dfc, dfr))
        dm = dm + (dm4_ref[0] + dm4_ref[1] + dm4_ref[2] + dm4_ref[3])
        dm_ref[...] = dm.astype(bf16)
        db_ref[...] = db

    whole = lambda shape: pl.BlockSpec(shape, lambda i: (0,) * len(shape))
    return pl.pallas_call(
        body, name=name, grid=(1,),
        in_specs=[pl.BlockSpec((s, LANE), lambda i: (0, MISC)), pl.BlockSpec((None, 1, LANE), lambda i: (l, 0, 0)),
                  whole((4, s, LANE)), whole((4, 8, s)), whole((4, s, LANE))],
        out_specs=[whole((s, LANE)), whole((1, LANE))],
        out_shape=[jax.ShapeDtypeStruct((s, LANE), bf16), jax.ShapeDtypeStruct((1, LANE), f32)],
        compiler_params=_cp(("arbitrary",)),
    )(z, bf, dfc4, dfr4, dmisc4)


FOX_TQ = 256


def _fox_specs(s, l):
    return [pl.BlockSpec((s, LANE), lambda h: (0, FQ + h)),
            pl.BlockSpec((s, LANE), lambda h: (0, FK + h)),
            pl.BlockSpec((s, LANE), lambda h: (0, FV + h)),
            pl.BlockSpec((s, LANE), lambda h: (0, 0)),
            pl.BlockSpec((8, s), lambda h: (0, 0)),
            pl.BlockSpec((None, 1, LANE), lambda h: (l, 0, 0)),
            pl.BlockSpec((None, 1, LANE), lambda h: (l, 0, 0))]


def _fox_fwd(name, z, fc, fr, qg, kg, l):
    s = z.shape[0]
    tq = min(s, FOX_TQ)

    def body(q, k, v, fc_ref, fr_ref, qg_ref, kg_ref, y_ref):
        h = pl.program_id(0)
        for i in range(s // tq):
            rows, keys = pl.ds(i * tq, tq), pl.ds(0, (i + 1) * tq)
            y = _fox_fn(q[rows, :], k[keys, :], v[keys, :], fc_ref[rows, :], fr_ref[:, keys], qg_ref[...],
                        kg_ref[...], h, i)
            y_ref[rows, :] = y.astype(bf16)

    return pl.pallas_call(
        body, name=name, grid=(4,), in_specs=_fox_specs(s, l), out_specs=pl.BlockSpec((s, LANE), lambda h: (0, h)),
        out_shape=jax.ShapeDtypeStruct((s, G), bf16), compiler_params=_cp(("parallel",)),
    )(z, z, z, fc, fr, qg, kg)


def _fox_bwd(name, z, fc, fr, qg, kg, dy, l):
    s = z.shape[0]
    tq = min(s, FOX_TQ)

    def body(q, k, v, fc_ref, fr_ref, qg_ref, kg_ref, dy_ref, dq, dk, dv, dfc, dfr, dqg, dkg):
        h = pl.program_id(0)
        for ref in (dk, dv, dfr, dqg, dkg):
            ref[...] = jnp.zeros_like(ref)
        for i in range(s // tq):
            rows, keys = pl.ds(i * tq, tq), pl.ds(0, (i + 1) * tq)
            fn = functools.partial(_fox_fn, h=h, i=i)
            _, vjp = jax.vjp(fn, q[rows, :], k[keys, :], v[keys, :], fc_ref[rows, :], fr_ref[:, keys], qg_ref[...],
                             kg_ref[...])
            g = vjp(dy_ref[rows, :])
            dq[rows, :] = g[0].astype(bf16)
            dfc[rows, :] = g[3]
            dk[keys, :] += g[1]
            dv[keys, :] += g[2]
            dfr[:, keys] += g[4]
            dqg[...] += g[5]
            dkg[...] += g[6]

    head = pl.BlockSpec((s, LANE), lambda h: (0, h))
    gain = pl.BlockSpec((None, 1, LANE), lambda h: (h, 0, 0))
    return pl.pallas_call(
        body, name=name, grid=(4,), in_specs=_fox_specs(s, l) + [pl.BlockSpec((s, LANE), lambda h: (0, 12 + h))],
        out_specs=[head, head, head, pl.BlockSpec((None, s, LANE), lambda h: (h, 0, 0)),
                   pl.BlockSpec((None, 8, s), lambda h: (h, 0, 0)), gain, gain],
        out_shape=[jax.ShapeDtypeStruct((s, G), bf16), jax.ShapeDtypeStruct((s, G), f32),
                   jax.ShapeDtypeStruct((s, G), f32), jax.ShapeDtypeStruct((4, s, LANE), f32),
                   jax.ShapeDtypeStruct((4, 8, s), f32), jax.ShapeDtypeStruct((4, 1, LANE), f32),
                   jax.ShapeDtypeStruct((4, 1, LANE), f32)],
        compiler_params=_cp(("parallel",)),
    )(z, z, z, fc, fr, qg, kg, dy)


def _gla_ts(s):
    return min(s, 512)


def _gla1_specs(s, ts, l):
    return [pl.BlockSpec((ts, LANE), lambda h, i: (i, GQ + h)),
            pl.BlockSpec((ts, LANE), lambda h, i: (i, GK + h)),
            pl.BlockSpec((ts, LANE), lambda h, i: (i, GV + h)),
            pl.BlockSpec((ts, LANE), lambda h, i: (i, MISC)),
            pl.BlockSpec((None, LANE, LANE), lambda h, i: (l, 0, h)),
            pl.BlockSpec((None, 1, LANE), lambda h, i: (l, 0, h))]


def _gla1_fwd(name, z, wd, bd, l):
    s = z.shape[0]
    ts = _gla_ts(s)
    nb = ts // CHUNK

    def body(q, k, v, m, wd_ref, bd_ref, qe_ref, kd_ref, el_ref, oi_ref):
        qe, kd, el, oi = _gla1_fn(q[...], k[...], v[...], m[...], wd_ref[...], bd_ref[...])
        qe_ref[...] = qe.astype(bf16)
        kd_ref[...] = kd.astype(bf16)
        el_ref[...] = el
        oi_ref[...] = oi

    blk = pl.BlockSpec((ts, LANE), lambda h, i: (i, h))
    return pl.pallas_call(
        body, name=name, grid=(4, s // ts), in_specs=_gla1_specs(s, ts, l),
        out_specs=[blk, blk, pl.BlockSpec((nb, LANE), lambda h, i: (i, h)), blk],
        out_shape=[jax.ShapeDtypeStruct((s, G), bf16), jax.ShapeDtypeStruct((s, G), bf16),
                   jax.ShapeDtypeStruct((s // CHUNK, G), f32), jax.ShapeDtypeStruct((s, G), f32)],
        compiler_params=_cp(("parallel", "parallel")),
    )(z, z, z, z, wd, bd)


def _gla1_bwd(name, z, wd, bd, dqe, dkd, del_, do, dvi, l):
    s = z.shape[0]
    ts = _gla_ts(s)
    nb = ts // CHUNK

    def body(q, k, v, m, wd_ref, bd_ref, dqe_ref, dkd_ref, del_ref, do_ref, dvi_ref, dq, dk, dv, dm, dwd, dbd):
        i = pl.program_id(1)
        _, vjp = jax.vjp(_gla1_fn, q[...], k[...], v[...], m[...], wd_ref[...], bd_ref[...])
        g = vjp((dqe_ref[...], dkd_ref[...], del_ref[...], do_ref[...]))
        dq[...] = g[0].astype(bf16)
        dk[...] = g[1].astype(bf16)
        dv[...] = (g[2] + dvi_ref[...]).astype(bf16)
        dm[...] = g[3]

        @pl.when(i == 0)
        def _():
            dwd[...] = g[4]
            dbd[...] = g[5]

        @pl.when(i > 0)
        def _():
            dwd[...] += g[4]
            dbd[...] += g[5]

    blk = pl.BlockSpec((ts, LANE), lambda h, i: (i, h))
    specs = _gla1_specs(s, ts, l) + [blk, blk, pl.BlockSpec((nb, LANE), lambda h, i: (i, h)), blk, blk]
    return pl.pallas_call(
        body, name=name, grid=(4, s // ts), in_specs=specs,
        out_specs=[blk, blk, blk, pl.BlockSpec((None, ts, LANE), lambda h, i: (h, i, 0)),
                   pl.BlockSpec((None, LANE, LANE), lambda h, i: (h, 0, 0)),
                   pl.BlockSpec((None, 1, LANE), lambda h, i: (h, 0, 0))],
        out_shape=[jax.ShapeDtypeStruct((s, G), bf16)] * 3 + [
            jax.ShapeDtypeStruct((4, s, LANE), f32), jax.ShapeDtypeStruct((4, LANE, LANE), f32),
            jax.ShapeDtypeStruct((4, 1, LANE), f32)],
        compiler_params=_cp(("parallel", "arbitrary")),
    )(z, z, z, z, wd, bd, dqe, dkd, del_, do, dvi)


def _gla2_fwd(name, z, qe, kd, el, oi):
    s = z.shape[0]
    n = s // CHUNK

    def body(v_ref, qe_ref, kd_ref, el_ref, oi_ref, o_ref, st_ref, cur):
        cur[...] = jnp.zeros_like(cur)

        def step(c, carry):
            rows = pl.ds(pl.multiple_of(c * CHUNK, CHUNK), CHUNK)
            st = cur[...]
            st_ref[c] = st
            o_ref[rows, :] = oi_ref[rows, :] + _dot(qe_ref[rows, :], st, ((1,), (1,)))
            cur[...] = st * el_ref[pl.ds(c, 1), :] + _dot(v_ref[rows, :], kd_ref[rows, :], ((0,), (0,)))
            return carry

        lax.fori_loop(0, n, step, 0)

    head = pl.BlockSpec((s, LANE), lambda h: (0, h))
    return pl.pallas_call(
        body, name=name, grid=(4,),
        in_specs=[pl.BlockSpec((s, LANE), lambda h: (0, GV + h)), head, head,
                  pl.BlockSpec((n, LANE), lambda h: (0, h)), head],
        out_specs=[head, pl.BlockSpec((None, n, LANE, LANE), lambda h: (h, 0, 0, 0))],
        out_shape=[jax.ShapeDtypeStruct((s, G), f32), jax.ShapeDtypeStruct((4, n, LANE, LANE), f32)],
        scratch_shapes=[pltpu.VMEM((LANE, LANE), f32)],
        compiler_params=_cp(("parallel",)),
    )(z, qe, kd, el, oi)


def _gla2_bwd(name, z, qe, kd, el, st, do):
    s = z.shape[0]
    n = s // CHUNK

    def body(v_ref, qe_ref, kd_ref, el_ref, st_ref, do_ref, dqe_ref, dkd_ref, dv_ref, del_ref, dcur):
        dcur[...] = jnp.zeros_like(dcur)

        def step(t, carry):
            c = n - 1 - t
            rows = pl.ds(pl.multiple_of(c * CHUNK, CHUNK), CHUNK)
            dn = dcur[...]
            stc = st_ref[c]
            doc = do_ref[rows, :]
            dqe_ref[rows, :] = _dot(doc, stc)
            dv_ref[rows, :] = _dot(kd_ref[rows, :], dn, ((1,), (1,)))
            dkd_ref[rows, :] = _dot(v_ref[rows, :], dn)
            del_ref[pl.ds(c, 1), :] = jnp.sum(stc * dn, axis=0, keepdims=True)
            dcur[...] = dn * el_ref[pl.ds(c, 1), :] + _dot(doc, qe_ref[rows, :], ((0,), (0,)))
            return carry

        lax.fori_loop(0, n, step, 0)

    head = pl.BlockSpec((s, LANE), lambda h: (0, h))
    chunk = pl.BlockSpec((n, LANE), lambda h: (0, h))
    return pl.pallas_call(
        body, name=name, grid=(4,),
        in_specs=[pl.BlockSpec((s, LANE), lambda h: (0, GV + h)), head, head, chunk,
                  pl.BlockSpec((None, n, LANE, LANE), lambda h: (h, 0, 0, 0)), head],
        out_specs=[head, head, head, chunk],
        out_shape=[jax.ShapeDtypeStruct((s, G), f32)] * 3 + [jax.ShapeDtypeStruct((n, G), f32)],
        scratch_shapes=[pltpu.VMEM((LANE, LANE), f32)],
        compiler_params=_cp(("parallel",)),
    )(z, qe, kd, el, st, do)


def _gla3_specs(ts, l):
    return [pl.BlockSpec((ts, LANE), lambda h, i: (i, h)),
            pl.BlockSpec((ts, LANE), lambda h, i: (i, GG + h)),
            pl.BlockSpec((None, 1, LANE), lambda h, i: (l, 0, 0))]


def _gla3_fwd(name, o, z, og, l):
    s = z.shape[0]
    ts = _gla_ts(s)

    def body(o_ref, g_ref, og_ref, y_ref):
        y_ref[...] = _gla3_fn(o_ref[...], g_ref[...], og_ref[...]).astype(bf16)

    return pl.pallas_call(
        body, name=name, grid=(4, s // ts), in_specs=_gla3_specs(ts, l),
        out_specs=pl.BlockSpec((ts, LANE), lambda h, i: (i, h)),
        out_shape=jax.ShapeDtypeStruct((s, G), bf16), compiler_params=_cp(("parallel", "parallel")),
    )(o, z, og)


def _gla3_bwd(name, o, z, og, dy, l):
    s = z.shape[0]
    ts = _gla_ts(s)

    def body(o_ref, g_ref, og_ref, dy_ref, do_ref, dg_ref, dog_ref):
        i = pl.program_id(1)
        _, vjp = jax.vjp(_gla3_fn, o_ref[...], g_ref[...], og_ref[...])
        g = vjp(dy_ref[...])
        do_ref[...] = g[0]
        dg_ref[...] = g[1].astype(bf16)

        @pl.when(i == 0)
        def _():
            dog_ref[...] = g[2]

        @pl.when(i > 0)
        def _():
            dog_ref[...] += g[2]

    blk = pl.BlockSpec((ts, LANE), lambda h, i: (i, h))
    return pl.pallas_call(
        body, name=name, grid=(4, s // ts),
        in_specs=_gla3_specs(ts, l) + [pl.BlockSpec((ts, LANE), lambda h, i: (i, 8 + h))],
        out_specs=[blk, blk, pl.BlockSpec((None, 1, LANE), lambda h, i: (h, 0, 0))],
        out_shape=[jax.ShapeDtypeStruct((s, G), f32), jax.ShapeDtypeStruct((s, G), bf16),
                   jax.ShapeDtypeStruct((4, 1, LANE), f32)],
        compiler_params=_cp(("parallel", "arbitrary")),
    )(o, z, og, dy)


def _layer_fwd(l, h, p, weights):
    s = h.shape[0]
    tb = _tm_big(s)
    n = f"l{l}_"
    w_in = weights(l, "w_in", h)
    u = _rmsnorm_fwd(n + "norm_mix", h, p["g_mix"], l)
    z = _mm(n + "mm_in", u, w_in,
            pl.BlockSpec((tb, D), lambda j, i, k: (i, 0)), pl.BlockSpec((D, 1152), lambda j, i, k: (0, j)),
            jax.ShapeDtypeStruct((s, ZC), f32), pl.BlockSpec((tb, 1152), lambda j, i, k: (i, j)),
            (ZC // 1152, s // tb, 1), ((1,), (0,)), 1)
    w_out = weights(l, "w_out", z)
    ya, yb = _convpool_fwd(n + "convpool", z, p["conv_w"], p["pool_w"], p["pool_scale"], l)
    qe, kd, el, oi = _gla1_fwd(n + "gla_chunk", z, p["wdec"], p["bdec"], l)
    o, st = _gla2_fwd(n + "gla_scan", z, qe, kd, el, oi)
    yc = _gla3_fwd(n + "gla_out", o, z, p["gla_og"], l)
    fc, fr = _foxprep_fwd(n + "fox_prep", z, p["fox_bf"], l)
    yd = _fox_fwd(n + "fox_attn", z, fc, fr, p["fox_qg"], p["fox_kg"], l)
    y = jnp.concatenate([ya, yb, yc, yd], axis=1)
    w_gate, w_up, w_down = weights(l, "ffn", y)
    res_tile = lambda: pl.BlockSpec((tb, 1024), lambda j, i, k: (i, j))
    h1 = _mm(n + "mm_out", y, w_out.reshape(D, D),
             pl.BlockSpec((tb, D), lambda j, i, k: (i, 0)), pl.BlockSpec((D, 1024), lambda j, i, k: (0, j)),
             jax.ShapeDtypeStruct((s, D), f32), res_tile(), (2, s // tb, 1), ((1,), (0,)), 1, res=h, res_spec=res_tile())
    u2 = _rmsnorm_fwd(n + "norm_ffn", h1, p["g_ffn"], l)
    gate, up, act = _mm_gate_up(n + "mm_gate_up", u2, w_gate, w_up)
    h2 = _mm(n + "mm_down", act, w_down,
             pl.BlockSpec((tb, FB), lambda j, i, k: (i, k)),
             pl.BlockSpec((None, FB, 1024), lambda j, i, k: (k, 0, j)),
             jax.ShapeDtypeStruct((s, D), f32), res_tile(), (2, s // tb, NCHIP), ((1,), (0,)), NCHIP,
             res=h1, res_spec=res_tile())
    saved = dict(h=h, u=u, z=z, qe=qe, kd=kd, el=el, st=st, o=o, fc=fc, fr=fr, y=y, h1=h1, u2=u2,
                 gate=gate, up=up, act=act, w_in=w_in, w_out=w_out, w_gate=w_gate, w_up=w_up, w_down=w_down)
    return h2, saved


def _mm_tn(name, a, b, ta, tb, out_shape, out_spec, grid):
    s = a.shape[0]
    return _mm(name, a, b, pl.BlockSpec((s, ta), lambda i, j, k: (0, i)), pl.BlockSpec((s, tb), lambda i, j, k: (0, j)),
               out_shape, out_spec, grid, ((0,), (0,)), 1)


def _layer_bwd_ffn(l, dh2, dh2b, p, sv, ship):
    s = dh2.shape[0]
    tb = _tm_big(s)
    n = f"l{l}_bwd_"
    g_wd = _mm_tn(n + "mm_dwd", sv["act"], dh2b, FB, 1024, jax.ShapeDtypeStruct((NCHIP, FB, D), bf16),
                  pl.BlockSpec((None, FB, 1024), lambda i, j, k: (i, 0, j)), (NCHIP, 2, 1))
    dgate, dup = _mm_dact(n + "mm_dact", dh2b, sv["w_down"], sv["gate"], sv["up"])
    nt_in = lambda: (pl.BlockSpec((tb, FB), lambda j, i, k: (i, k)),
                     pl.BlockSpec((None, 1024, FB), lambda j, i, k: (k, j, 0)))
    nt_out = lambda: (jax.ShapeDtypeStruct((s, D), f32), pl.BlockSpec((tb, 1024), lambda j, i, k: (i, j)))
    du2 = _mm(n + "mm_du2_gate", dgate, sv["w_gate"], *nt_in(), *nt_out(), (2, s // tb, NCHIP), ((1,), (1,)), NCHIP)
    du2 = _mm(n + "mm_du2_up", dup, sv["w_up"], *nt_in(), *nt_out(), (2, s // tb, NCHIP), ((1,), (1,)), NCHIP,
              res=du2, res_spec=pl.BlockSpec((tb, 1024), lambda j, i, k: (i, j)))
    wg_shape = jax.ShapeDtypeStruct((NCHIP, D, FB), bf16)
    wg_spec = lambda: pl.BlockSpec((None, 1024, FB), lambda i, j, k: (j, i, 0))
    g_wg = _mm_tn(n + "mm_dwg", sv["u2"], dgate, 1024, FB, wg_shape, wg_spec(), (2, NCHIP, 1))
    g_wu = _mm_tn(n + "mm_dwu", sv["u2"], dup, 1024, FB, wg_shape, wg_spec(), (2, NCHIP, 1))
    dh1, dh1b, dg_ffn = _rmsnorm_bwd(n + "norm_ffn", sv["h1"], p["g_ffn"], du2, dh2, l)
    g_wo = _mm_tn(n + "mm_dwo", sv["y"], dh1b, G, 1024, jax.ShapeDtypeStruct((NCHIP, G, D), bf16),
                  pl.BlockSpec((None, G, 1024), lambda i, j, k: (i, 0, j)), (NCHIP, 2, 1))
    token = ship(dict(w_out=g_wo, w_gate=g_wg, w_up=g_wu, w_down=g_wd))
    dy = _mm(n + "mm_dy", dh1b, sv["w_out"].reshape(D, D),
             pl.BlockSpec((tb, D), lambda j, i, k: (i, 0)), pl.BlockSpec((1024, D), lambda j, i, k: (j, 0)),
             jax.ShapeDtypeStruct((s, D), f32), pl.BlockSpec((tb, 1024), lambda j, i, k: (i, j)),
             (2, s // tb, 1), ((1,), (1,)), 1, dep=token)
    return dh1, dy, dg_ffn


def _layer_bwd_mix(l, dh1, dy, p, sv, ship):
    s = dh1.shape[0]
    n = f"l{l}_bwd_"
    z = sv["z"]
    dcb, dcc, dch, dpu, dconv, dpoolw, dpools = _convpool_bwd(
        n + "convpool", z, p["conv_w"], p["pool_w"], p["pool_scale"], dy, l)
    do, dgg, dog = _gla3_bwd(n + "gla_out", sv["o"], z, p["gla_og"], dy, l)
    dqe, dkd, dvi, del_ = _gla2_bwd(n + "gla_scan", z, sv["qe"], sv["kd"], sv["el"], sv["st"], do)
    dgq, dgk, dgv, dmisc4, dwd, dbd = _gla1_bwd(n + "gla_chunk", z, p["wdec"], p["bdec"], dqe, dkd, del_, do, dvi, l)
    dfq, dfk, dfv, dfc4, dfr4, dqg, dkg = _fox_bwd(n + "fox_attn", z, sv["fc"], sv["fr"], p["fox_qg"], p["fox_kg"], dy, l)
    dmisc, dbf = _foxprep_bwd(n + "fox_prep", z, p["fox_bf"], dfc4, dfr4, dmisc4, l)
    dz = jnp.concatenate([dcb, dcc, dch, dpu, dgq, dgk, dgv, dgg, dfq, dfk.astype(bf16), dfv.astype(bf16), dmisc],
                         axis=1)
    g_wi = _mm_tn(n + "mm_dwi", sv["u"], dz, 1024, 1152, jax.ShapeDtypeStruct((D, ZC), bf16),
                  pl.BlockSpec((1024, 1152), lambda i, j, k: (i, j)), (2, ZC // 1152, 1))
    token = ship(dict(w_in=_win_to_blocks(g_wi)))
    tb = _tm_big(s)
    du = _mm(n + "mm_du", dz, sv["w_in"],
             pl.BlockSpec((tb, 1920), lambda j, i, k: (i, k)), pl.BlockSpec((1024, 1920), lambda j, i, k: (j, k)),
             jax.ShapeDtypeStruct((s, D), f32), pl.BlockSpec((tb, 1024), lambda j, i, k: (i, j)),
             (2, s // tb, ZC // 1920), ((1,), (1,)), ZC // 1920, dep=token)
    dh, dhb, dg_mix = _rmsnorm_bwd(n + "norm_mix", sv["h"], p["g_mix"], du, dh1, l)
    small = dict(
        norm_mix_g=dg_mix[0], conv_w=dconv, pool_w=dpoolw, pool_scale=dpools[0],
        gla_w_decay=jnp.concatenate([dwd[hh, GA_LANE:GA_LANE + 16, :64] for hh in range(4)], axis=1),
        gla_b_decay=jnp.concatenate([dbd[hh, 0, :64] for hh in range(4)]),
        gla_out_g=jnp.sum(dog[:, 0, :], axis=0), fox_q_g=jnp.sum(dqg[:, 0, :], axis=0),
        fox_k_g=jnp.sum(dkg[:, 0, :], axis=0), fox_b_f=dbf[0, FF_LANE:FF_LANE + 4])
    return dh, dhb, small


def _win_from_blocks(wb):
    def cols(a, b):
        parts = []
        for kk in range(NCHIP):
            lo, hi = max(a, kk * WINB), min(b, (kk + 1) * WINB)
            if lo < hi:
                parts.append(wb[kk, :, lo - kk * WINB:hi - kk * WINB])
        return parts

    zeros = lambda w: [jnp.zeros((wb.shape[1], w), wb.dtype)]
    segs = cols(0, 2048)
    for hh in range(4):
        segs += cols(2048 + 64 * hh, 2112 + 64 * hh) + zeros(64)
    for hh in range(4):
        segs += cols(2304 + 64 * hh, 2368 + 64 * hh) + zeros(64)
    segs += cols(2560, 3584) + cols(3600, 5136)
    segs += cols(5136, 5140) + zeros(GA_LANE - 4) + cols(3584, 3600) + zeros(LANE - GA_LANE - 16)
    return jnp.concatenate(segs, axis=-1)


def _win_to_blocks(g):
    mb = MISC * LANE
    segs = [(0, 2048)] + [(GQ * LANE + LANE * hh, 64) for hh in range(4)] + [(GK * LANE + LANE * hh, 64) for hh in range(4)]
    segs += [(GV * LANE, 1024), (mb + GA_LANE, 16), (FQ * LANE, 1536), (mb + FF_LANE, 4)]
    blocks, at = [[] for _ in range(NCHIP)], 0
    for start, width in segs:
        while width > 0:
            take = min(width, (at // WINB + 1) * WINB - at)
            blocks[at // WINB].append(g[:, start:start + take])
            start, width, at = start + take, width - take, at + take
    return jnp.stack([jnp.concatenate(b, axis=1) for b in blocks])


def _place():
    x, y, c = lax.axis_index("x"), lax.axis_index("y"), lax.axis_index("c")
    chips = [(1 - x, y), (x, 1 - y), (1 - x, 1 - y)]
    return x, y, c, chips


def _allgather_small(name, v):
    m_per, n = v.shape

    def body(x_ref, out_ref, send_sems, recv_sems, local_sem):
        x, y, c, chips = _place()
        me, sibling = (x, y, c), (x, y, 1 - c)

        def rows(px, py, pc):
            return out_ref.at[pl.ds((4 * px + 2 * py + pc) * m_per, m_per), :]

        def copy(k, block, to, src=None):
            return pltpu.make_async_remote_copy(
                src_ref=rows(*block) if src is None else src, dst_ref=rows(*block),
                send_sem=send_sems.at[k], recv_sem=recv_sems.at[k], device_id=to, device_id_type=MESH)

        mine = pltpu.make_async_copy(x_ref, rows(*me), local_sem)
        mine.start()
        first = [copy(0, me, sibling, src=x_ref)]
        first += [copy(1 + j, me, (*chip, c), src=x_ref) for j, chip in enumerate(chips)]
        for cp in first:
            cp.start()
        passed = [copy(4 + j, (*chip, c), sibling) for j, chip in enumerate(chips)]
        for j, chip in enumerate(chips):
            copy(1 + j, (*chip, c), me).wait_recv()
            passed[j].start()
        copy(0, sibling, me).wait_recv()
        for j, chip in enumerate(chips):
            copy(4 + j, (*chip, 1 - c), me).wait_recv()
        for cp in first + passed:
            cp.wait_send()
        mine.wait()

    return pl.pallas_call(
        body, name=name, out_shape=jax.ShapeDtypeStruct((8 * m_per, n), v.dtype),
        in_specs=[pl.BlockSpec(memory_space=pltpu.VMEM)], out_specs=pl.BlockSpec(memory_space=pltpu.VMEM),
        scratch_shapes=[pltpu.SemaphoreType.DMA((7,)), pltpu.SemaphoreType.DMA((7,)), pltpu.SemaphoreType.DMA],
    )(v)


def _hbm_specs(n):
    return [pl.BlockSpec(memory_space=pl.ANY)] * n


def _own_slot(shard, chip):
    return lax.dynamic_update_index_in_dim(lax.empty((NCHIP,) + shard.shape, shard.dtype), shard, chip, 0)


HBM = pl.BlockSpec(memory_space=pltpu.HBM)
SEM = pl.BlockSpec(memory_space=pltpu.SEMAPHORE)
EFFECT = pltpu.SideEffectType.DATAFLOW_SIDE_EFFECTING


def _in_hbm(v):
    return pltpu.with_memory_space_constraint(v, pltpu.HBM)


def _gather_start(name, groups):
    flat = [b for g in groups for b in g]
    nt, ng = len(flat), len(groups)

    def body(*refs):
        outs = refs[nt:]
        sems, bufs = outs[:2 * ng], outs[2 * ng:2 * ng + nt]
        x, y, c, chips = _place()
        me = 2 * x + y
        t = 0
        for gi, g in enumerate(groups):
            for k in range(len(g)):
                r = bufs[t].shape[1] // 2
                mine = bufs[t].at[me, pl.ds(c * r, r), :]
                for j, (px, py) in enumerate(chips):
                    pltpu.make_async_remote_copy(
                        src_ref=mine, dst_ref=mine, send_sem=sems[2 * gi].at[3 * k + j], recv_sem=sems[2 * gi + 1].at[3 * k + j],
                        device_id=(px, py, c), device_id_type=MESH).start()
                t += 1

    sem_shapes = []
    for g in groups:
        sem_shapes += [pltpu.SemaphoreType.DMA((3 * len(g),))] * 2
    out = pl.pallas_call(
        body, name=name,
        out_shape=tuple(sem_shapes + [pltpu.HBM(b.shape, b.dtype) for b in flat]),
        in_specs=tuple([HBM] * nt), out_specs=tuple([SEM] * (2 * ng) + [HBM] * nt),
        input_output_aliases={t: 2 * ng + t for t in range(nt)},
        compiler_params=pltpu.CompilerParams(has_side_effects=EFFECT),
    )(*[_in_hbm(b) for b in flat])
    sems = [(out[2 * gi], out[2 * gi + 1]) for gi in range(ng)]
    bufs, at = [], 2 * ng
    for g in groups:
        bufs.append(list(out[at:at + len(g)]))
        at += len(g)
    return sems, bufs


def _gather_wait(name, bufs, send_sems, recv_sems, after):
    nt = len(bufs)

    def body(*refs):
        ins, ss, rs = refs[:nt], refs[nt], refs[nt + 1]
        x, y, c, chips = _place()
        me = 2 * x + y
        for k in range(nt):
            r = ins[k].shape[1] // 2
            for j, (px, py) in enumerate(chips):
                cp = pltpu.make_async_remote_copy(
                    src_ref=ins[k].at[me, pl.ds(c * r, r), :], dst_ref=ins[k].at[2 * px + py, pl.ds(c * r, r), :],
                    send_sem=ss.at[3 * k + j], recv_sem=rs.at[3 * k + j], device_id=(px, py, c), device_id_type=MESH)
                cp.wait_send()
                cp.wait_recv()

    out = pl.pallas_call(
        body, name=name, out_shape=tuple(pltpu.HBM(b.shape, b.dtype) for b in bufs),
        in_specs=tuple([HBM] * nt + [SEM, SEM, pl.BlockSpec(memory_space=pl.ANY)]), out_specs=tuple([HBM] * nt),
        input_output_aliases={t: t for t in range(nt)},
        compiler_params=pltpu.CompilerParams(has_side_effects=EFFECT),
    )(*bufs, send_sems, recv_sems, after)
    return list(out)


def _gather_exchange(name, bufs):
    nt = len(bufs)

    def body(*refs):
        outs = refs[nt:2 * nt]
        send_sems, recv_sems = refs[2 * nt:]
        x, y, c, chips = _place()
        sibling = (x, y, 1 - c)

        def half(t, chip_idx, cc):
            r = outs[t].shape[1] // 2
            return outs[t].at[chip_idx, pl.ds(cc * r, r), :]

        sent = []
        for t in range(nt):
            for j, (px, py) in enumerate(chips):
                cp = pltpu.make_async_remote_copy(
                    src_ref=half(t, 2 * px + py, c), dst_ref=half(t, 2 * px + py, c),
                    send_sem=send_sems.at[t, j], recv_sem=recv_sems.at[t, j], device_id=sibling, device_id_type=MESH)
                cp.start()
                sent.append(cp)
        for t in range(nt):
            for j, (px, py) in enumerate(chips):
                pltpu.make_async_remote_copy(
                    src_ref=half(t, 2 * px + py, 1 - c), dst_ref=half(t, 2 * px + py, 1 - c),
                    send_sem=send_sems.at[t, j], recv_sem=recv_sems.at[t, j], device_id=sibling,
                    device_id_type=MESH).wait_recv()
        for cp in sent:
            cp.wait_send()

    return pl.pallas_call(
        body, name=name, out_shape=[jax.ShapeDtypeStruct(v.shape, v.dtype) for v in bufs],
        in_specs=_hbm_specs(nt), out_specs=_hbm_specs(nt), input_output_aliases={t: t for t in range(nt)},
        scratch_shapes=[pltpu.SemaphoreType.DMA((nt, 3)), pltpu.SemaphoreType.DMA((nt, 3))],
    )(*bufs)


def _rs_to_sibling(name, grads):
    nt = len(grads)

    def body(*refs):
        ins, outs = refs[:nt], refs[nt:2 * nt]
        send_sems, recv_sems = refs[2 * nt:]
        x, y, c, _ = _place()
        cps = []
        for t in range(nt):
            r = ins[t].shape[1] // 2
            cp = pltpu.make_async_remote_copy(
                src_ref=ins[t].at[:, pl.ds((1 - c) * r, r), :], dst_ref=outs[t],
                send_sem=send_sems.at[t], recv_sem=recv_sems.at[t], device_id=(x, y, 1 - c), device_id_type=MESH)
            cp.start()
            cps.append(cp)
        for cp in cps:
            cp.wait()

    return pl.pallas_call(
        body, name=name,
        out_shape=[jax.ShapeDtypeStruct((NCHIP, g.shape[1] // 2, g.shape[2]), g.dtype) for g in grads],
        in_specs=_hbm_specs(nt), out_specs=_hbm_specs(nt),
        scratch_shapes=[pltpu.SemaphoreType.DMA((nt,)), pltpu.SemaphoreType.DMA((nt,))],
    )(*grads)


def _rs_pair_sum(name, pos, g, other):
    r, cdim = other.shape[1], other.shape[2]
    tr = r // 4 if (r // 4) % 16 == 0 else r // 2
    nblk = r // tr

    def body(pos_ref, g_ref, o_ref, s_ref):
        s_ref[...] = (g_ref[...].astype(f32) + o_ref[...].astype(f32)).astype(bf16)

    blk = pl.BlockSpec((None, tr, cdim), lambda q, i, p: (q, i, 0))
    return pl.pallas_call(
        body, name=name, out_shape=jax.ShapeDtypeStruct(other.shape, bf16),
        grid_spec=pltpu.PrefetchScalarGridSpec(
            num_scalar_prefetch=1, grid=(NCHIP, nblk),
            in_specs=[pl.BlockSpec((None, tr, cdim), lambda q, i, p: (q, p[1] * nblk + i, 0)), blk], out_specs=blk),
        compiler_params=_cp(("parallel", "parallel")),
    )(pos, g, other)


def _scatter_start(name, sums):
    nt = len(sums)

    def body(*refs):
        outs = refs[2 * nt:]
        ss, rs, src, land, token = outs[0], outs[1], outs[2:2 + nt], outs[2 + nt:2 + 2 * nt], outs[2 + 2 * nt]
        x, y, c, chips = _place()
        me = 2 * x + y
        for t in range(nt):
            for j, (px, py) in enumerate(chips):
                pltpu.make_async_remote_copy(
                    src_ref=src[t].at[2 * px + py], dst_ref=land[t].at[me], send_sem=ss.at[3 * t + j], recv_sem=rs.at[3 * t + j],
                    device_id=(px, py, c), device_id_type=MESH).start()
        token[...] = jnp.zeros_like(token)

    shapes = [pltpu.HBM(v.shape, v.dtype) for v in sums]
    out = pl.pallas_call(
        body, name=name,
        out_shape=tuple([pltpu.SemaphoreType.DMA((3 * nt,))] * 2 + shapes + shapes + [jax.ShapeDtypeStruct((8, LANE), f32)]),
        in_specs=tuple([HBM] * (2 * nt)),
        out_specs=tuple([SEM, SEM] + [HBM] * (2 * nt) + [pl.BlockSpec(memory_space=pltpu.VMEM)]),
        input_output_aliases={t: 2 + t for t in range(2 * nt)},
        compiler_params=pltpu.CompilerParams(has_side_effects=EFFECT),
    )(*[_in_hbm(v) for v in sums], *[_in_hbm(lax.empty(v.shape, v.dtype)) for v in sums])
    return out[0], out[1], list(out[2:2 + nt]), list(out[2 + nt:2 + 2 * nt]), out[2 + 2 * nt]


def _scatter_wait(name, sums, land, send_sems, recv_sems, after):
    nt = len(sums)

    def body(*refs):
        src, dst, ss, rs = refs[:nt], refs[nt:2 * nt], refs[2 * nt], refs[2 * nt + 1]
        x, y, c, chips = _place()
        for t in range(nt):
            for j, (px, py) in enumerate(chips):
                cp = pltpu.make_async_remote_copy(
                    src_ref=src[t].at[2 * px + py], dst_ref=dst[t].at[2 * px + py], send_sem=ss.at[3 * t + j],
                    recv_sem=rs.at[3 * t + j], device_id=(px, py, c), device_id_type=MESH)
                cp.wait_send()
                cp.wait_recv()

    shapes = [pltpu.HBM(v.shape, v.dtype) for v in sums]
    out = pl.pallas_call(
        body, name=name, out_shape=tuple(shapes + shapes),
        in_specs=tuple([HBM] * (2 * nt) + [SEM, SEM, pl.BlockSpec(memory_space=pl.ANY)]),
        out_specs=tuple([HBM] * (2 * nt)), input_output_aliases={t: t for t in range(2 * nt)},
        compiler_params=pltpu.CompilerParams(has_side_effects=EFFECT),
    )(*sums, *land, send_sems, recv_sems, after)
    return list(out[:nt]), list(out[nt:])


def _rs_chip_sum(name, pos, sums, parts):
    r, cdim = parts.shape[1], parts.shape[2]
    tr = r // 4 if (r // 4) % 16 == 0 else r // 2
    nblk = r // tr

    def body(pos_ref, own_ref, a_ref, b_ref, c_ref, o_ref):
        o_ref[...] = ((own_ref[...].astype(f32) + a_ref[...].astype(f32)) + b_ref[...].astype(f32)) \
            + c_ref[...].astype(f32)

    def slot(k):
        return pl.BlockSpec((None, tr, cdim), lambda i, p: ((p[0] + k) % NCHIP, i, 0))

    return pl.pallas_call(
        body, name=name, out_shape=jax.ShapeDtypeStruct((2 * r, cdim), f32),
        grid_spec=pltpu.PrefetchScalarGridSpec(
            num_scalar_prefetch=1, grid=(nblk,), in_specs=[slot(0), slot(1), slot(2), slot(3)],
            out_specs=pl.BlockSpec((tr, cdim), lambda i, p: (p[1] * nblk + i, 0))),
        compiler_params=_cp(("parallel",)),
    )(pos, sums, parts, parts, parts)


def _rs_share_halves(name, bufs):
    nt = len(bufs)

    def body(*refs):
        outs = refs[nt:2 * nt]
        send_sems, recv_sems = refs[2 * nt:]
        x, y, c, _ = _place()
        cps = []
        for t in range(nt):
            r = outs[t].shape[0] // 2
            mine = outs[t].at[pl.ds(c * r, r), :]
            theirs = outs[t].at[pl.ds((1 - c) * r, r), :]
            cp = pltpu.make_async_remote_copy(
                src_ref=mine, dst_ref=mine, send_sem=send_sems.at[t], recv_sem=recv_sems.at[t],
                device_id=(x, y, 1 - c), device_id_type=MESH)
            cp.start()
            cps.append((cp, theirs))
        for t, (cp, theirs) in enumerate(cps):
            pltpu.make_async_remote_copy(
                src_ref=theirs, dst_ref=theirs, send_sem=send_sems.at[t], recv_sem=recv_sems.at[t],
                device_id=(x, y, 1 - c), device_id_type=MESH).wait_recv()
            cp.wait_send()

    return pl.pallas_call(
        body, name=name, out_shape=[jax.ShapeDtypeStruct(v.shape, v.dtype) for v in bufs],
        in_specs=_hbm_specs(nt), out_specs=_hbm_specs(nt), input_output_aliases={t: t for t in range(nt)},
        scratch_shapes=[pltpu.SemaphoreType.DMA((nt,)), pltpu.SemaphoreType.DMA((nt,))],
    )(*bufs)


BIG = ("w_in", "w_out", "w_gate", "w_up", "w_down")


GROUPS = (("w_in",), ("w_out",), ("w_gate", "w_up", "w_down"))


def _rs_begin(tag, keys, grads, pos):
    got = _rs_to_sibling(tag + "to_sibling", grads)
    sums = [_rs_pair_sum(tag + "pair_sum_" + k, pos, g, o) for k, g, o in zip(keys, grads, got)]
    return (keys,) + _scatter_start(tag + "start", sums)


def _rs_end(tag, state, pos, after):
    keys, send_sems, recv_sems, sums, land, _ = state
    sums, land = _scatter_wait(tag + "wait", sums, land, send_sems, recv_sems, after)
    halves = [_rs_chip_sum(tag + "chip_sum_" + k, pos, s, v) for k, s, v in zip(keys, sums, land)]
    return dict(zip(keys, _rs_share_halves(tag + "share", halves)))


def _adam_math(w, g, m, v):
    m = ADAM_B1 * m + (1.0 - ADAM_B1) * g
    v = ADAM_B2 * v + (1.0 - ADAM_B2) * (g * g)
    m_hat = m / (1.0 - ADAM_B1 ** ADAM_STEP)
    v_hat = v / (1.0 - ADAM_B2 ** ADAM_STEP)
    delta = -ADAM_LR * (m_hat / (jnp.sqrt(v_hat) + ADAM_EPS) + ADAM_WD * w)
    return delta, m, v


def _adam_big(name, g0, g1, w, m, v):
    _, r, cdim = w.shape
    tr = 128 if r % 128 == 0 else 64
    nb = r // tr

    def body(g0_ref, g1_ref, w_ref, m_ref, v_ref, go_ref, d_ref, mo_ref, vo_ref):
        l = pl.program_id(0)
        g = jnp.where(l == 0, g0_ref[...], g1_ref[...])
        delta, mn, vn = _adam_math(w_ref[...], g, m_ref[...], v_ref[...])
        go_ref[...] = g
        d_ref[...] = delta
        mo_ref[...] = mn
        vo_ref[...] = vn

    lay = pl.BlockSpec((None, tr, cdim), lambda l, i: (l, i, 0))
    return pl.pallas_call(
        body, name=name, grid=(2, nb),
        in_specs=[pl.BlockSpec((tr, cdim), lambda l, i: (i * (1 - l) + (nb - 1) * l, 0)),
                  pl.BlockSpec((tr, cdim), lambda l, i: (i * l, 0)), lay, lay, lay],
        out_specs=[lay] * 4, out_shape=[jax.ShapeDtypeStruct(w.shape, f32)] * 4,
        compiler_params=_cp(("arbitrary", "arbitrary")),
    )(g0, g1, w, m, v)


def _sum8(name, gathered):
    m_per = gathered.shape[0] // 8

    def body(g_ref, o_ref):
        tot = g_ref[pl.ds(0, m_per), :]
        for d in range(1, 8):
            tot = tot + g_ref[pl.ds(d * m_per, m_per), :]
        o_ref[...] = tot

    return pl.pallas_call(body, name=name, out_shape=jax.ShapeDtypeStruct((m_per, LANE), f32))(gathered)


def _adam_small(name, g, w, m, v):
    def body(g_ref, w_ref, m_ref, v_ref, d_ref, mo_ref, vo_ref):
        delta, mn, vn = _adam_math(w_ref[...], g_ref[...], m_ref[...], v_ref[...])
        d_ref[...] = delta
        mo_ref[...] = mn
        vo_ref[...] = vn

    return pl.pallas_call(body, name=name, out_shape=[jax.ShapeDtypeStruct(g.shape, f32)] * 3)(g, w, m, v)


def _pack(vals):
    rows, offs, at = [], [], 0
    for a in vals:
        a = a.reshape(-1)
        n = -(-a.shape[0] // (8 * LANE)) * 8
        rows.append(jnp.pad(a, (0, n * LANE - a.shape[0])).reshape(n, LANE))
        offs.append(at)
        at += n
    return jnp.concatenate(rows, axis=0), offs


def _unpack(packed, offs, shapes):
    out = []
    for o, shp in zip(offs, shapes):
        size = 1
        for d in shp:
            size *= d
        n = -(-size // LANE)
        out.append(packed[o:o + n].reshape(-1)[:size].reshape(shp))
    return out


SMALL = ("norm_mix_g", "conv_w", "pool_w", "pool_scale", "gla_w_decay", "gla_b_decay", "gla_out_g",
         "fox_q_g", "fox_k_g", "fox_b_f", "norm_ffn_g")
ALL = ("norm_mix_g", "w_in", "conv_w", "pool_w", "pool_scale", "gla_w_decay", "gla_b_decay", "gla_out_g",
       "fox_q_g", "fox_k_g", "fox_b_f", "w_out", "norm_ffn_g", "w_gate", "w_up", "w_down")


def kernel(x, norm_mix_g, w_in, conv_w, pool_w, pool_scale, gla_w_decay, gla_b_decay, gla_out_g, fox_q_g, fox_k_g, fox_b_f, w_out, norm_ffn_g, w_gate, w_up, w_down, loss_target, m_norm_mix_g, m_w_in, m_conv_w, m_pool_w, m_pool_scale, m_gla_w_decay, m_gla_b_decay, m_gla_out_g, m_fox_q_g, m_fox_k_g, m_fox_b_f, m_w_out, m_norm_ffn_g, m_w_gate, m_w_up, m_w_down, v_norm_mix_g, v_w_in, v_conv_w, v_pool_w, v_pool_scale, v_gla_w_decay, v_gla_b_decay, v_gla_out_g, v_fox_q_g, v_fox_k_g, v_fox_b_f, v_w_out, v_norm_ffn_g, v_w_gate, v_w_up, v_w_down):
    w = dict(norm_mix_g=norm_mix_g, w_in=w_in, conv_w=conv_w, pool_w=pool_w, pool_scale=pool_scale,
             gla_w_decay=gla_w_decay, gla_b_decay=gla_b_decay, gla_out_g=gla_out_g, fox_q_g=fox_q_g, fox_k_g=fox_k_g,
             fox_b_f=fox_b_f, w_out=w_out, norm_ffn_g=norm_ffn_g, w_gate=w_gate, w_up=w_up, w_down=w_down)
    m = dict(norm_mix_g=m_norm_mix_g, w_in=m_w_in, conv_w=m_conv_w, pool_w=m_pool_w, pool_scale=m_pool_scale,
             gla_w_decay=m_gla_w_decay, gla_b_decay=m_gla_b_decay, gla_out_g=m_gla_out_g, fox_q_g=m_fox_q_g,
             fox_k_g=m_fox_k_g, fox_b_f=m_fox_b_f, w_out=m_w_out, norm_ffn_g=m_norm_ffn_g, w_gate=m_w_gate,
             w_up=m_w_up, w_down=m_w_down)
    v = dict(norm_mix_g=v_norm_mix_g, w_in=v_w_in, conv_w=v_conv_w, pool_w=v_pool_w, pool_scale=v_pool_scale,
             gla_w_decay=v_gla_w_decay, gla_b_decay=v_gla_b_decay, gla_out_g=v_gla_out_g, fox_q_g=v_fox_q_g,
             fox_k_g=v_fox_k_g, fox_b_f=v_fox_b_f, w_out=v_w_out, norm_ffn_g=v_norm_ffn_g, w_gate=v_w_gate,
             w_up=v_w_up, w_down=v_w_down)
    chip = 2 * lax.axis_index("x") + lax.axis_index("y")

    pos = jnp.stack([chip, lax.axis_index("c")]).astype(jnp.int32)

    mine, offs = _pack([conv_w, gla_w_decay, jnp.zeros((8, LANE), f32)])
    order = [(l, grp) for l in range(2) for grp in GROUPS]
    sems, gbufs = _gather_start(
        "gather_start",
        [[_own_slot(mine, chip)]] + [[_own_slot(w[k][l].astype(bf16), chip) for k in grp] for l, grp in order])

    def gathered(tag, gi, after):
        got = _gather_wait(tag + "_wait", gbufs[gi], sems[gi][0], sems[gi][1], after)
        return _gather_exchange(tag + "_exchange", got)

    def weights(l, group, after):
        got = gathered(f"l{l}_gather_{group}", 1 + 3 * l + ("w_in", "w_out", "ffn").index(group), after)
        if group == "w_in":
            return _win_from_blocks(got[0])
        return got[0] if group == "w_out" else got

    every = gathered("gather_small", 0, x)[0]
    per_chip = [_unpack(every[kk], offs, [conv_w.shape, gla_w_decay.shape]) for kk in range(NCHIP)]
    conv_full = jnp.concatenate([pc[0] for pc in per_chip], axis=-1)[:, :, 0, :]
    wdec_full = jnp.concatenate([pc[1] for pc in per_chip], axis=-1)

    wdec = jnp.pad(wdec_full.reshape(2, 16, 4, 64), ((0, 0), (GA_LANE, LANE - GA_LANE - 16), (0, 0), (0, 64)))
    p = dict(
        g_mix=norm_mix_g[:, None, :], g_ffn=norm_ffn_g[:, None, :],
        conv_w=conv_full, pool_w=pool_w, pool_scale=pool_scale[:, None, :],
        wdec=wdec.reshape(2, LANE, G),
        bdec=jnp.pad(gla_b_decay.reshape(2, 4, 64), ((0, 0), (0, 0), (0, 64))).reshape(2, 1, G),
        gla_og=gla_out_g[:, None, :], fox_qg=fox_q_g[:, None, :], fox_kg=fox_k_g[:, None, :],
        fox_bf=jnp.pad(fox_b_f, ((0, 0), (FF_LANE, LANE - FF_LANE - 4)))[:, None, :])

    h0 = x[0]
    h1, sv0 = _layer_fwd(0, h0, p, weights)
    h2, sv1 = _layer_fwd(1, h1, p, weights)
    sq, dh, dhb = _loss("loss", h2, loss_target[0])
    loss = lax.psum(sq[0, 0] * (0.5 / D), ("x", "y", "c"))

    late = ("w_out", "w_gate", "w_up", "w_down")
    shipped = {}

    def ship(tag):
        def start(grads_by_name):
            keys = tuple(grads_by_name)
            shipped[tag] = _rs_begin(tag, keys, [grads_by_name[k] for k in keys], pos)
            return shipped[tag][-1]
        return start

    dh1, dy, dgf1 = _layer_bwd_ffn(1, dh, dhb, p, sv1, ship("l1_rs_a_"))
    dh, dhb, small1 = _layer_bwd_mix(1, dh1, dy, p, sv1, ship("l1_rs_b_"))
    dh1, dy, dgf0 = _layer_bwd_ffn(0, dh, dhb, p, sv0, ship("l0_rs_a_"))
    dh, dhb, small0 = _layer_bwd_mix(0, dh1, dy, p, sv0, ship("l0_rs_b_"))
    small0["norm_ffn_g"], small1["norm_ffn_g"] = dgf0[0], dgf1[0]

    red1 = _rs_end("l1_rs_a_", shipped["l1_rs_a_"], pos, dh)
    red1.update(_rs_end("l1_rs_b_", shipped["l1_rs_b_"], pos, red1["w_down"]))
    red0 = _rs_end("l0_rs_a_", shipped["l0_rs_a_"], pos, red1["w_in"])
    grads, deltas, new_m, new_v = {}, {}, {}, {}
    for k in late:
        grads[k], deltas[k], new_m[k], new_v[k] = _adam_big("adam_" + k, red0[k], red1[k], w[k], m[k], v[k])

    packed, goffs = _pack([jnp.stack([small0[k], small1[k]]) for k in SMALL])
    total = _sum8("sum_small_grads", _allgather_small("gather_small_grads", packed))
    red0.update(_rs_end("l0_rs_b_", shipped["l0_rs_b_"], pos, total))
    k = "w_in"
    grads[k], deltas[k], new_m[k], new_v[k] = _adam_big("adam_" + k, red0[k], red1[k], w[k], m[k], v[k])
    full_shapes = [(2,) + small0[k].shape for k in SMALL]
    gsmall = dict(zip(SMALL, _unpack(total, goffs, full_shapes)))
    gsmall["conv_w"] = lax.dynamic_slice_in_dim(gsmall["conv_w"], chip * LANE, LANE, axis=2)[:, :, None, :]
    gsmall["gla_w_decay"] = lax.dynamic_slice_in_dim(gsmall["gla_w_decay"], chip * 64, 64, axis=2)
    gp, loffs = _pack([gsmall[k] for k in SMALL])
    wp, _ = _pack([w[k] for k in SMALL])
    mp, _ = _pack([m[k] for k in SMALL])
    vp, _ = _pack([v[k] for k in SMALL])
    dp, mnp, vnp = _adam_small("adam_small", gp, wp, mp, vp)
    shapes = [w[k].shape for k in SMALL]
    for k, a, b, c_, d_ in zip(SMALL, _unpack(gp, loffs, shapes), _unpack(dp, loffs, shapes),
                               _unpack(mnp, loffs, shapes), _unpack(vnp, loffs, shapes)):
        grads[k], deltas[k], new_m[k], new_v[k] = a, b, c_, d_

    return (loss, dh[None], *[grads[k] for k in ALL], *[deltas[k] for k in ALL],
            *[new_m[k] for k in ALL], *[new_v[k] for k in ALL])
```

```python
import functools

import jax
import jax.numpy as jnp
from jax import lax
from jax.experimental import pallas as pl
from jax.experimental.pallas import tpu as pltpu

f32 = jnp.float32
bf16 = jnp.bfloat16

D = 2048
G = 512
DFF = 5632
NCHIP = 4
FB = DFF // NCHIP
WIN = 5140
WINB = WIN // NCHIP
EPS = 1e-6
CHUNK = 64
LANE = 128

CB, CC, CH, PU, GQ, GK, GV, GG, FQ, FK, FV, MISC = 0, 4, 8, 12, 16, 20, 24, 28, 32, 36, 40, 44
ZC = 45 * LANE
FF_LANE = 0
GA_LANE = 8

ADAM_LR, ADAM_B1, ADAM_B2, ADAM_EPS, ADAM_WD, ADAM_STEP = 0.001, 0.9, 0.999, 1e-08, 0.01, 10

VMEM_LIMIT = 60 * 1024 * 1024
MESH = pl.DeviceIdType.MESH


def _cp(sem=None):
    return pltpu.CompilerParams(dimension_semantics=sem, vmem_limit_bytes=VMEM_LIMIT)


def _dot(a, b, dims=((1,), (0,))):
    return lax.dot_general(a.astype(bf16), b.astype(bf16), (dims, ((), ())), preferred_element_type=f32)


def _bdot(a, b, ca, cb):
    return lax.dot_general(a.astype(bf16), b.astype(bf16), (((ca,), (cb,)), ((0,), (0,))),
                           preferred_element_type=f32)


def _log_sigmoid(x):
    return jnp.minimum(x, 0.0) - jnp.log(1.0 + jnp.exp(-jnp.abs(x)))


@jax.custom_vjp
def _sigmoid(x):
    return 1.0 / (1.0 + jnp.exp(-x))


def _sigmoid_fwd(x):
    s = _sigmoid(x)
    return s, s


def _sigmoid_bwd(s, g):
    return (g * s * (1.0 - s),)


_sigmoid.defvjp(_sigmoid_fwd, _sigmoid_bwd)


def _rms(x, g):
    return x * lax.rsqrt(jnp.mean(x * x, axis=-1, keepdims=True) + EPS) * g


def _shift_impl(x, n, period, transpose):
    rows = x.shape[0]
    t = lax.broadcasted_iota(jnp.int32, x.shape, 0)
    if period is not None:
        t = t & (period - 1)
    keep = t >= n
    if not transpose:
        return jnp.where(keep, pltpu.roll(x, n, 0), 0.0)
    return pltpu.roll(jnp.where(keep, x, 0.0), rows - n, 0)


def _shift(x, n, period=None):
    @jax.custom_vjp
    def f(v):
        return _shift_impl(v, n, period, False)

    def fwd(v):
        return f(v), None

    def bwd(_, g):
        return (_shift_impl(g, n, period, True),)

    f.defvjp(fwd, bwd)
    return f(x)


def _cumsum_rows(x, length, period=None):
    n = 1
    while n < length:
        x = x + _shift(x, n, period)
        n *= 2
    return x


def _convpool_fn(cb, cc, ch, pu, w0, w1, w2, pw, ps, j):
    u = cc * ch
    y = w2 * u + w1 * _shift(u, 1) + w0 * _shift(u, 2)
    ya = cb * y
    s2 = pu + _shift(pu, 1)
    s4 = s2 + _shift(s2, 2)
    s8 = s4 + _shift(s4, 4)
    s16 = s8 + _shift(s8, 8)
    wsum = jnp.where(j == 0, s2, jnp.where(j == 1, s4, jnp.where(j == 2, s8, s16)))
    width = (2 << j).astype(f32)
    t = lax.broadcasted_iota(jnp.int32, pu.shape, 0).astype(f32)
    count = jnp.minimum(t + 1.0, width)
    d = wsum / count - pu
    yb = _dot(d, pw) * ps
    return ya, yb


def _foxprep_fn(misc, bf):
    lf = _log_sigmoid(misc + bf)
    fc = _cumsum_rows(lf, lf.shape[0])
    return fc, jnp.transpose(fc)


def _fox_fn(q, k, v, fcol, frow8, qg, kg, h, i):
    tq, s = q.shape[0], k.shape[0]
    qn = _rms(q, qg)
    kn = _rms(k, kg)
    lg = _dot(qn, kn, ((1,), (1,))) * (LANE ** -0.5)
    lane = lax.broadcasted_iota(jnp.int32, fcol.shape, 1)
    fq = jnp.sum(jnp.where(lane == h, fcol, 0.0), axis=1, keepdims=True)
    row = lax.broadcasted_iota(jnp.int32, frow8.shape, 0)
    fk = jnp.sum(jnp.where(row == h, frow8, 0.0), axis=0, keepdims=True)
    lg = lg + fq - fk
    qpos = i * tq + lax.broadcasted_iota(jnp.int32, (tq, s), 0)
    kpos = lax.broadcasted_iota(jnp.int32, (tq, s), 1)
    lg = jnp.where(kpos <= qpos, lg, -jnp.inf)
    m = lax.stop_gradient(jnp.max(lg, axis=1, keepdims=True))
    e = jnp.exp(lg - m)
    p = e / jnp.sum(e, axis=1, keepdims=True)
    return _dot(p, v)


def _gla1_fn(q, k, v, misc, wd, bd):
    ts = q.shape[0]
    nb = ts // CHUNK
    x = _dot(misc, wd) + bd
    la = _log_sigmoid(x) * (1.0 / 16.0)
    cc = _cumsum_rows(la, CHUNK, CHUNK)
    la3 = la.reshape(nb, CHUNK, LANE)
    last3 = jnp.sum(la3, axis=1, keepdims=True)
    last2 = jnp.sum(la3, axis=1)
    cc3 = cc.reshape(nb, CHUNK, LANE)
    q3 = (q * 0.125).reshape(nb, CHUNK, LANE)
    k3 = k.reshape(nb, CHUNK, LANE)
    v3 = v.reshape(nb, CHUNK, LANE)
    ep = jnp.exp(cc3)
    en = jnp.exp(-cc3)
    qe = q3 * ep
    a1 = _bdot(qe, k3 * en, 2, 2)
    a2 = _bdot(q3 * en, k3 * ep, 2, 2)
    ti = lax.broadcasted_iota(jnp.int32, a1.shape, 1)
    si = lax.broadcasted_iota(jnp.int32, a1.shape, 2)
    sc = jnp.where(si <= ti, a1, a2)
    oi = _bdot(sc, v3, 2, 1)
    kd = k3 * jnp.exp(last3 - cc3)
    el = jnp.exp(last2)
    return qe.reshape(ts, LANE), kd.reshape(ts, LANE), el, oi.reshape(ts, LANE)


def _gla3_fn(o, gg, og):
    return _rms(o, og) * (gg * _sigmoid(gg))


def _ffn_fn(gate, up):
    return gate * _sigmoid(gate) * up


def _mm(name, a, b, a_spec, b_spec, out_shape, out_spec, grid, dims, nk, res=None, res_spec=None, dep=None):
    has_res = res is not None
    has_dep = dep is not None
    nax = len(grid)

    def body(*refs):
        a_ref, b_ref = refs[0], refs[1]
        res_ref = refs[2] if has_res else None
        out_ref = refs[2 + has_res + has_dep]
        part = _dot(a_ref[...], b_ref[...], dims)
        if nk == 1:
            if has_res:
                part = part + res_ref[...]
            out_ref[...] = part.astype(out_ref.dtype)
            return
        acc_ref = refs[3 + has_res + has_dep]
        k = pl.program_id(nax - 1)

        @pl.when(k == 0)
        def _():
            acc_ref[...] = part

        @pl.when(k > 0)
        def _():
            acc_ref[...] += part

        @pl.when(k == nk - 1)
        def _():
            tot = acc_ref[...]
            if has_res:
                tot = tot + res_ref[...]
            out_ref[...] = tot.astype(out_ref.dtype)

    ops = [a, b] + ([res] if has_res else []) + ([dep] if has_dep else [])
    specs = [a_spec, b_spec] + ([res_spec] if has_res else [])
    if has_dep:
        specs.append(pl.BlockSpec((8, LANE), lambda *_: (0, 0)))
    blk = tuple(d for d in out_spec.block_shape if d is not None)
    scratch = [pltpu.VMEM(blk, f32)] if nk > 1 else []
    return pl.pallas_call(
        body, name=name, grid=grid, in_specs=specs, out_specs=out_spec, out_shape=out_shape,
        scratch_shapes=scratch,
        compiler_params=_cp(("parallel",) * (nax - 1) + ("arbitrary",)),
    )(*ops)


def _tm(s):
    return min(s, 512)


def _tm_big(s):
    return min(s, 1024)


def _mm_gate_up(name, u, w_gate, w_up):
    s = u.shape[0]
    tm = _tm(s)

    def body(u_ref, wg_ref, wu_ref, g_ref, up_ref, a_ref):
        g = _dot(u_ref[...], wg_ref[...])
        up = _dot(u_ref[...], wu_ref[...])
        g_ref[...] = g
        up_ref[...] = up
        a_ref[...] = _ffn_fn(g, up).astype(bf16)

    wspec = pl.BlockSpec((None, D, FB), lambda j, i: (j, 0, 0))
    tile = pl.BlockSpec((tm, FB), lambda j, i: (i, j))
    return pl.pallas_call(
        body, name=name, grid=(NCHIP, s // tm),
        in_specs=[pl.BlockSpec((tm, D), lambda j, i: (i, 0)), wspec, wspec], out_specs=[tile, tile, tile],
        out_shape=[jax.ShapeDtypeStruct((s, DFF), f32), jax.ShapeDtypeStruct((s, DFF), f32),
                   jax.ShapeDtypeStruct((s, DFF), bf16)],
        compiler_params=_cp(("parallel", "parallel")),
    )(u, w_gate, w_up)


def _mm_dact(name, dh, w_down, gate, up):
    s = dh.shape[0]
    tm = _tm(s)

    def body(dh_ref, wd_ref, g_ref, up_ref, dg_ref, du_ref):
        dact = _dot(dh_ref[...], wd_ref[...], ((1,), (1,)))
        _, vjp = jax.vjp(_ffn_fn, g_ref[...], up_ref[...])
        dg, du = vjp(dact)
        dg_ref[...] = dg.astype(bf16)
        du_ref[...] = du.astype(bf16)

    tile = pl.BlockSpec((tm, FB), lambda j, i: (i, j))
    return pl.pallas_call(
        body, name=name, grid=(NCHIP, s // tm),
        in_specs=[pl.BlockSpec((tm, D), lambda j, i: (i, 0)), pl.BlockSpec((None, FB, D), lambda j, i: (j, 0, 0)),
                  tile, tile],
        out_specs=[tile, tile], out_shape=[jax.ShapeDtypeStruct((s, DFF), bf16)] * 2,
        compiler_params=_cp(("parallel", "parallel")),
    )(dh, w_down, gate, up)


def _rmsnorm_fwd(name, x, g, l):
    s = x.shape[0]
    tm = min(s, 256)

    def body(x_ref, g_ref, u_ref):
        u_ref[...] = _rms(x_ref[...], g_ref[...]).astype(bf16)

    return pl.pallas_call(
        body, name=name, grid=(s // tm,),
        in_specs=[pl.BlockSpec((tm, D), lambda i: (i, 0)), pl.BlockSpec((None, 1, D), lambda i: (l, 0, 0))],
        out_specs=pl.BlockSpec((tm, D), lambda i: (i, 0)),
        out_shape=jax.ShapeDtypeStruct((s, D), bf16), compiler_params=_cp(("parallel",)),
    )(x, g)


def _rmsnorm_bwd(name, x, g, du, dres, l, dep=None):
    s = x.shape[0]
    tm = min(s, 256)

    def body(x_ref, g_ref, du_ref, dres_ref, *rest):
        dx_ref, dxb_ref, dg_ref = rest[-3:]
        _, vjp = jax.vjp(_rms, x_ref[...], g_ref[...])
        dx, dg = vjp(du_ref[...])
        tot = dx + dres_ref[...]
        dx_ref[...] = tot
        dxb_ref[...] = tot.astype(bf16)

        @pl.when(pl.program_id(0) == 0)
        def _():
            dg_ref[...] = dg

        @pl.when(pl.program_id(0) > 0)
        def _():
            dg_ref[...] += dg

    row = pl.BlockSpec((tm, D), lambda i: (i, 0))
    deps = [] if dep is None else [dep]
    return pl.pallas_call(
        body, name=name, grid=(s // tm,),
        in_specs=[row, pl.BlockSpec((None, 1, D), lambda i: (l, 0, 0)), row, row]
        + [pl.BlockSpec((8, LANE), lambda i: (0, 0)) for _ in deps],
        out_specs=[row, row, pl.BlockSpec((1, D), lambda i: (0, 0))],
        out_shape=[jax.ShapeDtypeStruct((s, D), f32), jax.ShapeDtypeStruct((s, D), bf16),
                   jax.ShapeDtypeStruct((1, D), f32)],
        compiler_params=_cp(("arbitrary",)),
    )(x, g, du, dres, *deps)


def _loss(name, y, t):
    s = y.shape[0]
    tm = min(s, 256)
    row = pl.BlockSpec((tm, D), lambda i: (i, 0))

    def body(y_ref, t_ref, l_ref, d_ref, db_ref):
        e = y_ref[...] - t_ref[...]
        d = e * (1.0 / D)
        d_ref[...] = d
        db_ref[...] = d.astype(bf16)
        part = jnp.zeros((8, LANE), f32) + jnp.sum(e * e)

        @pl.when(pl.program_id(0) == 0)
        def _():
            l_ref[...] = part

        @pl.when(pl.program_id(0) > 0)
        def _():
            l_ref[...] += part

    return pl.pallas_call(
        body, name=name, grid=(s // tm,), in_specs=[row, row],
        out_specs=[pl.BlockSpec((8, LANE), lambda i: (0, 0)), row, row],
        out_shape=[jax.ShapeDtypeStruct((8, LANE), f32), jax.ShapeDtypeStruct((s, D), f32),
                   jax.ShapeDtypeStruct((s, D), bf16)],
        compiler_params=_cp(("arbitrary",)),
    )(y, t)


def _zspec(s, blk):
    return pl.BlockSpec((s, LANE), lambda j: (0, blk + j))


def _convpool_specs(s, l):
    return [_zspec(s, CB), _zspec(s, CC), _zspec(s, CH), _zspec(s, PU),
            pl.BlockSpec((None, 3, LANE), lambda j: (l, 0, j)),
            pl.BlockSpec((None, None, LANE, LANE), lambda j: (l, j, 0, 0)),
            pl.BlockSpec((None, 1, LANE), lambda j: (l, 0, j))]


def _convpool_fwd(name, z, conv_w, pool_w, pool_scale, l):
    s = z.shape[0]

    def body(cb, cc, ch, pu, cw, pw, ps, ya_ref, yb_ref):
        ya, yb = _convpool_fn(cb[...], cc[...], ch[...], pu[...], cw[0:1, :], cw[1:2, :], cw[2:3, :], pw[...], ps[...],
                              pl.program_id(0))
        ya_ref[...] = ya.astype(bf16)
        yb_ref[...] = yb.astype(bf16)

    col = pl.BlockSpec((s, LANE), lambda j: (0, j))
    return pl.pallas_call(
        body, name=name, grid=(4,), in_specs=_convpool_specs(s, l), out_specs=[col, col],
        out_shape=[jax.ShapeDtypeStruct((s, G), bf16)] * 2, compiler_params=_cp(("parallel",)),
    )(z, z, z, z, conv_w, pool_w, pool_scale)


def _convpool_bwd(name, z, conv_w, pool_w, pool_scale, dy, l):
    s = z.shape[0]

    def body(cb, cc, ch, pu, cw, pw, ps, dya, dyb, dcb, dcc, dch, dpu, dcw, dpw, dps):
        j = pl.program_id(0)
        fn = functools.partial(_convpool_fn, j=j)
        _, vjp = jax.vjp(fn, cb[...], cc[...], ch[...], pu[...], cw[0:1, :], cw[1:2, :], cw[2:3, :], pw[...], ps[...])
        g = vjp((dya[...], dyb[...]))
        dcb[...] = g[0].astype(bf16)
        dcc[...] = g[1].astype(bf16)
        dch[...] = g[2].astype(bf16)
        dpu[...] = g[3].astype(bf16)
        dcw[0:1, :] = g[4]
        dcw[1:2, :] = g[5]
        dcw[2:3, :] = g[6]
        dpw[...] = g[7]
        dps[...] = g[8]

    col = pl.BlockSpec((s, LANE), lambda j: (0, j))
    specs = _convpool_specs(s, l) + [pl.BlockSpec((s, LANE), lambda j: (0, j)),
                                     pl.BlockSpec((s, LANE), lambda j: (0, 4 + j))]
    return pl.pallas_call(
        body, name=name, grid=(4,), in_specs=specs,
        out_specs=[col, col, col, col, pl.BlockSpec((3, LANE), lambda j: (0, j)),
                   pl.BlockSpec((None, LANE, LANE), lambda j: (j, 0, 0)), pl.BlockSpec((1, LANE), lambda j: (0, j))],
        out_shape=[jax.ShapeDtypeStruct((s, G), bf16)] * 4 + [
            jax.ShapeDtypeStruct((3, G), f32), jax.ShapeDtypeStruct((4, LANE, LANE), f32),
            jax.ShapeDtypeStruct((1, G), f32)],
        compiler_params=_cp(("parallel",)),
    )(z, z, z, z, conv_w, pool_w, pool_scale, dy, dy)


def _foxprep_fwd(name, z, bf, l):
    s = z.shape[0]

    def body(m_ref, b_ref, fc_ref, fr_ref):
        fc, fr = _foxprep_fn(m_ref[...], b_ref[...])
        fc_ref[...] = fc
        fr_ref[...] = fr

    return pl.pallas_call(
        body, name=name, grid=(1,),
        in_specs=[pl.BlockSpec((s, LANE), lambda i: (0, MISC)), pl.BlockSpec((None, 1, LANE), lambda i: (l, 0, 0))],
        out_specs=[pl.BlockSpec((s, LANE), lambda i: (0, 0)), pl.BlockSpec((LANE, s), lambda i: (0, 0))],
        out_shape=[jax.ShapeDtypeStruct((s, LANE), f32), jax.ShapeDtypeStruct((LANE, s), f32)],
        compiler_params=_cp(("arbitrary",)),
    )(z, bf)


def _foxprep_bwd(name, z, bf, dfc4, dfr4, dmisc4, l):
    s = z.shape[0]

    def body(m_ref, b_ref, dfc_ref, dfr_ref, dm4_ref, dm_ref, db_ref):
        _, vjp = jax.vjp(_foxprep_fn, m_ref[...], b_ref[...])
        dfc = dfc_ref[0] + dfc_ref[1] + dfc_ref[2] + dfc_ref[3]
        dfr = dfr_ref[0] + dfr_ref[1] + dfr_ref[2] + dfr_ref[3]
        dfr = jnp.concatenate([dfr, jnp.zeros((LANE - 8, s), f32)], axis=0)
        dm, db = vjp((dfc, dfr))
        dm = dm + (dm4_ref[0] + dm4_ref[1] + dm4_ref[2] + dm4_ref[3])
        dm_ref[...] = dm.astype(bf16)
        db_ref[...] = db

    whole = lambda shape: pl.BlockSpec(shape, lambda i: (0,) * len(shape))
    return pl.pallas_call(
        body, name=name, grid=(1,),
        in_specs=[pl.BlockSpec((s, LANE), lambda i: (0, MISC)), pl.BlockSpec((None, 1, LANE), lambda i: (l, 0, 0)),
                  whole((4, s, LANE)), whole((4, 8, s)), whole((4, s, LANE))],
        out_specs=[whole((s, LANE)), whole((1, LANE))],
        out_shape=[jax.ShapeDtypeStruct((s, LANE), bf16), jax.ShapeDtypeStruct((1, LANE), f32)],
        compiler_params=_cp(("arbitrary",)),
    )(z, bf, dfc4, dfr4, dmisc4)


FOX_TQ = 256


def _fox_specs(s, l):
    return [pl.BlockSpec((s, LANE), lambda h: (0, FQ + h)),
            pl.BlockSpec((s, LANE), lambda h: (0, FK + h)),
            pl.BlockSpec((s, LANE), lambda h: (0, FV + h)),
            pl.BlockSpec((s, LANE), lambda h: (0, 0)),
            pl.BlockSpec((8, s), lambda h: (0, 0)),
            pl.BlockSpec((None, 1, LANE), lambda h: (l, 0, 0)),
            pl.BlockSpec((None, 1, LANE), lambda h: (l, 0, 0))]


def _fox_fwd(name, z, fc, fr, qg, kg, l):
    s = z.shape[0]
    tq = min(s, FOX_TQ)

    def body(q, k, v, fc_ref, fr_ref, qg_ref, kg_ref, y_ref):
        h = pl.program_id(0)
        for i in range(s // tq):
            rows, keys = pl.ds(i * tq, tq), pl.ds(0, (i + 1) * tq)
            y = _fox_fn(q[rows, :], k[keys, :], v[keys, :], fc_ref[rows, :], fr_ref[:, keys], qg_ref[...],
                        kg_ref[...], h, i)
            y_ref[rows, :] = y.astype(bf16)

    return pl.pallas_call(
        body, name=name, grid=(4,), in_specs=_fox_specs(s, l), out_specs=pl.BlockSpec((s, LANE), lambda h: (0, h)),
        out_shape=jax.ShapeDtypeStruct((s, G), bf16), compiler_params=_cp(("parallel",)),
    )(z, z, z, fc, fr, qg, kg)


def _fox_bwd(name, z, fc, fr, qg, kg, dy, l):
    s = z.shape[0]
    tq = min(s, FOX_TQ)

    def body(q, k, v, fc_ref, fr_ref, qg_ref, kg_ref, dy_ref, dq, dk, dv, dfc, dfr, dqg, dkg):
        h = pl.program_id(0)
        for ref in (dk, dv, dfr, dqg, dkg):
            ref[...] = jnp.zeros_like(ref)
        for i in range(s // tq):
            rows, keys = pl.ds(i * tq, tq), pl.ds(0, (i + 1) * tq)
            fn = functools.partial(_fox_fn, h=h, i=i)
            _, vjp = jax.vjp(fn, q[rows, :], k[keys, :], v[keys, :], fc_ref[rows, :], fr_ref[:, keys], qg_ref[...],
                             kg_ref[...])
            g = vjp(dy_ref[rows, :])
            dq[rows, :] = g[0].astype(bf16)
            dfc[rows, :] = g[3]
            dk[keys, :] += g[1]
            dv[keys, :] += g[2]
            dfr[:, keys] += g[4]
            dqg[...] += g[5]
            dkg[...] += g[6]

    head = pl.BlockSpec((s, LANE), lambda h: (0, h))
    gain = pl.BlockSpec((None, 1, LANE), lambda h: (h, 0, 0))
    return pl.pallas_call(
        body, name=name, grid=(4,), in_specs=_fox_specs(s, l) + [pl.BlockSpec((s, LANE), lambda h: (0, 12 + h))],
        out_specs=[head, head, head, pl.BlockSpec((None, s, LANE), lambda h: (h, 0, 0)),
                   pl.BlockSpec((None, 8, s), lambda h: (h, 0, 0)), gain, gain],
        out_shape=[jax.ShapeDtypeStruct((s, G), bf16), jax.ShapeDtypeStruct((s, G), f32),
                   jax.ShapeDtypeStruct((s, G), f32), jax.ShapeDtypeStruct((4, s, LANE), f32),
                   jax.ShapeDtypeStruct((4, 8, s), f32), jax.ShapeDtypeStruct((4, 1, LANE), f32),
                   jax.ShapeDtypeStruct((4, 1, LANE), f32)],
        compiler_params=_cp(("parallel",)),
    )(z, z, z, fc, fr, qg, kg, dy)


def _gla_ts(s):
    return min(s, 512)


def _gla1_specs(s, ts, l):
    return [pl.BlockSpec((ts, LANE), lambda h, i: (i, GQ + h)),
            pl.BlockSpec((ts, LANE), lambda h, i: (i, GK + h)),
            pl.BlockSpec((ts, LANE), lambda h, i: (i, GV + h)),
            pl.BlockSpec((ts, LANE), lambda h, i: (i, MISC)),
            pl.BlockSpec((None, LANE, LANE), lambda h, i: (l, 0, h)),
            pl.BlockSpec((None, 1, LANE), lambda h, i: (l, 0, h))]


def _gla1_fwd(name, z, wd, bd, l):
    s = z.shape[0]
    ts = _gla_ts(s)
    nb = ts // CHUNK

    def body(q, k, v, m, wd_ref, bd_ref, qe_ref, kd_ref, el_ref, oi_ref):
        qe, kd, el, oi = _gla1_fn(q[...], k[...], v[...], m[...], wd_ref[...], bd_ref[...])
        qe_ref[...] = qe.astype(bf16)
        kd_ref[...] = kd.astype(bf16)
        el_ref[...] = el
        oi_ref[...] = oi

    blk = pl.BlockSpec((ts, LANE), lambda h, i: (i, h))
    return pl.pallas_call(
        body, name=name, grid=(4, s // ts), in_specs=_gla1_specs(s, ts, l),
        out_specs=[blk, blk, pl.BlockSpec((nb, LANE), lambda h, i: (i, h)), blk],
        out_shape=[jax.ShapeDtypeStruct((s, G), bf16), jax.ShapeDtypeStruct((s, G), bf16),
                   jax.ShapeDtypeStruct((s // CHUNK, G), f32), jax.ShapeDtypeStruct((s, G), f32)],
        compiler_params=_cp(("parallel", "parallel")),
    )(z, z, z, z, wd, bd)


def _gla1_bwd(name, z, wd, bd, dqe, dkd, del_, do, dvi, l):
    s = z.shape[0]
    ts = _gla_ts(s)
    nb = ts // CHUNK

    def body(q, k, v, m, wd_ref, bd_ref, dqe_ref, dkd_ref, del_ref, do_ref, dvi_ref, dq, dk, dv, dm, dwd, dbd):
        i = pl.program_id(1)
        _, vjp = jax.vjp(_gla1_fn, q[...], k[...], v[...], m[...], wd_ref[...], bd_ref[...])
        g = vjp((dqe_ref[...], dkd_ref[...], del_ref[...], do_ref[...]))
        dq[...] = g[0].astype(bf16)
        dk[...] = g[1].astype(bf16)
        dv[...] = (g[2] + dvi_ref[...]).astype(bf16)
        dm[...] = g[3]

        @pl.when(i == 0)
        def _():
            dwd[...] = g[4]
            dbd[...] = g[5]

        @pl.when(i > 0)
        def _():
            dwd[...] += g[4]
            dbd[...] += g[5]

    blk = pl.BlockSpec((ts, LANE), lambda h, i: (i, h))
    specs = _gla1_specs(s, ts, l) + [blk, blk, pl.BlockSpec((nb, LANE), lambda h, i: (i, h)), blk, blk]
    return pl.pallas_call(
        body, name=name, grid=(4, s // ts), in_specs=specs,
        out_specs=[blk, blk, blk, pl.BlockSpec((None, ts, LANE), lambda h, i: (h, i, 0)),
                   pl.BlockSpec((None, LANE, LANE), lambda h, i: (h, 0, 0)),
                   pl.BlockSpec((None, 1, LANE), lambda h, i: (h, 0, 0))],
        out_shape=[jax.ShapeDtypeStruct((s, G), bf16)] * 3 + [
            jax.ShapeDtypeStruct((4, s, LANE), f32), jax.ShapeDtypeStruct((4, LANE, LANE), f32),
            jax.ShapeDtypeStruct((4, 1, LANE), f32)],
        compiler_params=_cp(("parallel", "arbitrary")),
    )(z, z, z, z, wd, bd, dqe, dkd, del_, do, dvi)


def _gla2_fwd(name, z, qe, kd, el, oi):
    s = z.shape[0]
    n = s // CHUNK

    def body(v_ref, qe_ref, kd_ref, el_ref, oi_ref, o_ref, st_ref, cur):
        cur[...] = jnp.zeros_like(cur)

        def step(c, carry):
            rows = pl.ds(pl.multiple_of(c * CHUNK, CHUNK), CHUNK)
            st = cur[...]
            st_ref[c] = st
            o_ref[rows, :] = oi_ref[rows, :] + _dot(qe_ref[rows, :], st, ((1,), (1,)))
            cur[...] = st * el_ref[pl.ds(c, 1), :] + _dot(v_ref[rows, :], kd_ref[rows, :], ((0,), (0,)))
            return carry

        lax.fori_loop(0, n, step, 0)

    head = pl.BlockSpec((s, LANE), lambda h: (0, h))
    return pl.pallas_call(
        body, name=name, grid=(4,),
        in_specs=[pl.BlockSpec((s, LANE), lambda h: (0, GV + h)), head, head,
                  pl.BlockSpec((n, LANE), lambda h: (0, h)), head],
        out_specs=[head, pl.BlockSpec((None, n, LANE, LANE), lambda h: (h, 0, 0, 0))],
        out_shape=[jax.ShapeDtypeStruct((s, G), f32), jax.ShapeDtypeStruct((4, n, LANE, LANE), f32)],
        scratch_shapes=[pltpu.VMEM((LANE, LANE), f32)],
        compiler_params=_cp(("parallel",)),
    )(z, qe, kd, el, oi)


def _gla2_bwd(name, z, qe, kd, el, st, do):
    s = z.shape[0]
    n = s // CHUNK

    def body(v_ref, qe_ref, kd_ref, el_ref, st_ref, do_ref, dqe_ref, dkd_ref, dv_ref, del_ref, dcur):
        dcur[...] = jnp.zeros_like(dcur)

        def step(t, carry):
            c = n - 1 - t
            rows = pl.ds(pl.multiple_of(c * CHUNK, CHUNK), CHUNK)
            dn = dcur[...]
            stc = st_ref[c]
            doc = do_ref[rows, :]
            dqe_ref[rows, :] = _dot(doc, stc)
            dv_ref[rows, :] = _dot(kd_ref[rows, :], dn, ((1,), (1,)))
            dkd_ref[rows, :] = _dot(v_ref[rows, :], dn)
            del_ref[pl.ds(c, 1), :] = jnp.sum(stc * dn, axis=0, keepdims=True)
            dcur[...] = dn * el_ref[pl.ds(c, 1), :] + _dot(doc, qe_ref[rows, :], ((0,), (0,)))
            return carry

        lax.fori_loop(0, n, step, 0)

    head = pl.BlockSpec((s, LANE), lambda h: (0, h))
    chunk = pl.BlockSpec((n, LANE), lambda h: (0, h))
    return pl.pallas_call(
        body, name=name, grid=(4,),
        in_specs=[pl.BlockSpec((s, LANE), lambda h: (0, GV + h)), head, head, chunk,
                  pl.BlockSpec((None, n, LANE, LANE), lambda h: (h, 0, 0, 0)), head],
        out_specs=[head, head, head, chunk],
        out_shape=[jax.ShapeDtypeStruct((s, G), f32)] * 3 + [jax.ShapeDtypeStruct((n, G), f32)],
        scratch_shapes=[pltpu.VMEM((LANE, LANE), f32)],
        compiler_params=_cp(("parallel",)),
    )(z, qe, kd, el, st, do)


def _gla3_specs(ts, l):
    return [pl.BlockSpec((ts, LANE), lambda h, i: (i, h)),
            pl.BlockSpec((ts, LANE), lambda h, i: (i, GG + h)),
            pl.BlockSpec((None, 1, LANE), lambda h, i: (l, 0, 0))]


def _gla3_fwd(name, o, z, og, l):
    s = z.shape[0]
    ts = _gla_ts(s)

    def body(o_ref, g_ref, og_ref, y_ref):
        y_ref[...] = _gla3_fn(o_ref[...], g_ref[...], og_ref[...]).astype(bf16)

    return pl.pallas_call(
        body, name=name, grid=(4, s // ts), in_specs=_gla3_specs(ts, l),
        out_specs=pl.BlockSpec((ts, LANE), lambda h, i: (i, h)),
        out_shape=jax.ShapeDtypeStruct((s, G), bf16), compiler_params=_cp(("parallel", "parallel")),
    )(o, z, og)


def _gla3_bwd(name, o, z, og, dy, l):
    s = z.shape[0]
    ts = _gla_ts(s)

    def body(o_ref, g_ref, og_ref, dy_ref, do_ref, dg_ref, dog_ref):
        i = pl.program_id(1)
        _, vjp = jax.vjp(_gla3_fn, o_ref[...], g_ref[...], og_ref[...])
        g = vjp(dy_ref[...])
        do_ref[...] = g[0]
        dg_ref[...] = g[1].astype(bf16)

        @pl.when(i == 0)
        def _():
            dog_ref[...] = g[2]

        @pl.when(i > 0)
        def _():
            dog_ref[...] += g[2]

    blk = pl.BlockSpec((ts, LANE), lambda h, i: (i, h))
    return pl.pallas_call(
        body, name=name, grid=(4, s // ts),
        in_specs=_gla3_specs(ts, l) + [pl.BlockSpec((ts, LANE), lambda h, i: (i, 8 + h))],
        out_specs=[blk, blk, pl.BlockSpec((None, 1, LANE), lambda h, i: (h, 0, 0))],
        out_shape=[jax.ShapeDtypeStruct((s, G), f32), jax.ShapeDtypeStruct((s, G), bf16),
                   jax.ShapeDtypeStruct((4, 1, LANE), f32)],
        compiler_params=_cp(("parallel", "arbitrary")),
    )(o, z, og, dy)


def _layer_fwd(l, h, p, weights):
    s = h.shape[0]
    tb = _tm_big(s)
    n = f"l{l}_"
    w_in = weights(l, "w_in", h)
    u = _rmsnorm_fwd(n + "norm_mix", h, p["g_mix"], l)
    z = _mm(n + "mm_in", u, w_in,
            pl.BlockSpec((tb, D), lambda j, i, k: (i, 0)), pl.BlockSpec((D, 1152), lambda j, i, k: (0, j)),
            jax.ShapeDtypeStruct((s, ZC), f32), pl.BlockSpec((tb, 1152), lambda j, i, k: (i, j)),
            (ZC // 1152, s // tb, 1), ((1,), (0,)), 1)
    w_out = weights(l, "w_out", z)
    ya, yb = _convpool_fwd(n + "convpool", z, p["conv_w"], p["pool_w"], p["pool_scale"], l)
    qe, kd, el, oi = _gla1_fwd(n + "gla_chunk", z, p["wdec"], p["bdec"], l)
    o, st = _gla2_fwd(n + "gla_scan", z, qe, kd, el, oi)
    yc = _gla3_fwd(n + "gla_out", o, z, p["gla_og"], l)
    fc, fr = _foxprep_fwd(n + "fox_prep", z, p["fox_bf"], l)
    yd = _fox_fwd(n + "fox_attn", z, fc, fr, p["fox_qg"], p["fox_kg"], l)
    y = jnp.concatenate([ya, yb, yc, yd], axis=1)
    w_gate, w_up, w_down = weights(l, "ffn", y)
    res_tile = lambda: pl.BlockSpec((tb, 1024), lambda j, i, k: (i, j))
    h1 = _mm(n + "mm_out", y, w_out.reshape(D, D),
             pl.BlockSpec((tb, D), lambda j, i, k: (i, 0)), pl.BlockSpec((D, 1024), lambda j, i, k: (0, j)),
             jax.ShapeDtypeStruct((s, D), f32), res_tile(), (2, s // tb, 1), ((1,), (0,)), 1, res=h, res_spec=res_tile())
    u2 = _rmsnorm_fwd(n + "norm_ffn", h1, p["g_ffn"], l)
    gate, up, act = _mm_gate_up(n + "mm_gate_up", u2, w_gate, w_up)
    h2 = _mm(n + "mm_down", act, w_down,
             pl.BlockSpec((tb, FB), lambda j, i, k: (i, k)),
             pl.BlockSpec((None, FB, 1024), lambda j, i, k: (k, 0, j)),
             jax.ShapeDtypeStruct((s, D), f32), res_tile(), (2, s // tb, NCHIP), ((1,), (0,)), NCHIP,
             res=h1, res_spec=res_tile())
    saved = dict(h=h, u=u, z=z, qe=qe, kd=kd, el=el, st=st, o=o, fc=fc, fr=fr, y=y, h1=h1, u2=u2,
                 gate=gate, up=up, act=act, w_in=w_in, w_out=w_out, w_gate=w_gate, w_up=w_up, w_down=w_down)
    return h2, saved


def _mm_tn(name, a, b, ta, tb, out_shape, out_spec, grid):
    s = a.shape[0]
    return _mm(name, a, b, pl.BlockSpec((s, ta), lambda i, j, k: (0, i)), pl.BlockSpec((s, tb), lambda i, j, k: (0, j)),
               out_shape, out_spec, grid, ((0,), (0,)), 1)


def _layer_bwd_ffn(l, dh2, dh2b, p, sv, ship):
    s = dh2.shape[0]
    tb = _tm_big(s)
    n = f"l{l}_bwd_"
    g_wd = _mm_tn(n + "mm_dwd", sv["act"], dh2b, FB, 1024, jax.ShapeDtypeStruct((NCHIP, FB, D), bf16),
                  pl.BlockSpec((None, FB, 1024), lambda i, j, k: (i, 0, j)), (NCHIP, 2, 1))
    dgate, dup = _mm_dact(n + "mm_dact", dh2b, sv["w_down"], sv["gate"], sv["up"])
    wg_shape = jax.ShapeDtypeStruct((NCHIP, D, FB), bf16)
    wg_spec = lambda: pl.BlockSpec((None, 1024, FB), lambda i, j, k: (j, i, 0))
    g_wg = _mm_tn(n + "mm_dwg", sv["u2"], dgate, 1024, FB, wg_shape, wg_spec(), (2, NCHIP, 1))
    g_wu = _mm_tn(n + "mm_dwu", sv["u2"], dup, 1024, FB, wg_shape, wg_spec(), (2, NCHIP, 1))
    token = ship.early(dict(w_gate=g_wg, w_up=g_wu, w_down=g_wd))
    nt_in = lambda: (pl.BlockSpec((tb, FB), lambda j, i, k: (i, k)),
                     pl.BlockSpec((None, 1024, FB), lambda j, i, k: (k, j, 0)))
    nt_out = lambda: (jax.ShapeDtypeStruct((s, D), f32), pl.BlockSpec((tb, 1024), lambda j, i, k: (i, j)))
    du2 = _mm(n + "mm_du2_gate", dgate, sv["w_gate"], *nt_in(), *nt_out(), (2, s // tb, NCHIP), ((1,), (1,)), NCHIP,
              dep=token)
    du2 = _mm(n + "mm_du2_up", dup, sv["w_up"], *nt_in(), *nt_out(), (2, s // tb, NCHIP), ((1,), (1,)), NCHIP,
              res=du2, res_spec=pl.BlockSpec((tb, 1024), lambda j, i, k: (i, j)))
    dh1, dh1b, dg_ffn = _rmsnorm_bwd(n + "norm_ffn", sv["h1"], p["g_ffn"], du2, dh2, l)
    g_wo = _mm_tn(n + "mm_dwo", sv["y"], dh1b, G, 1024, jax.ShapeDtypeStruct((NCHIP, G, D), bf16),
                  pl.BlockSpec((None, G, 1024), lambda i, j, k: (i, 0, j)), (NCHIP, 2, 1))
    token = ship.finish(dict(w_out=g_wo), g_wo)
    dy = _mm(n + "mm_dy", dh1b, sv["w_out"].reshape(D, D),
             pl.BlockSpec((tb, D), lambda j, i, k: (i, 0)), pl.BlockSpec((1024, D), lambda j, i, k: (j, 0)),
             jax.ShapeDtypeStruct((s, D), f32), pl.BlockSpec((tb, 1024), lambda j, i, k: (i, j)),
             (2, s // tb, 1), ((1,), (1,)), 1, dep=token)
    return dh1, dy, dg_ffn


def _layer_bwd_mix(l, dh1, dy, p, sv, ship):
    s = dh1.shape[0]
    n = f"l{l}_bwd_"
    z = sv["z"]
    dcb, dcc, dch, dpu, dconv, dpoolw, dpools = _convpool_bwd(
        n + "convpool", z, p["conv_w"], p["pool_w"], p["pool_scale"], dy, l)
    do, dgg, dog = _gla3_bwd(n + "gla_out", sv["o"], z, p["gla_og"], dy, l)
    dqe, dkd, dvi, del_ = _gla2_bwd(n + "gla_scan", z, sv["qe"], sv["kd"], sv["el"], sv["st"], do)
    dgq, dgk, dgv, dmisc4, dwd, dbd = _gla1_bwd(n + "gla_chunk", z, p["wdec"], p["bdec"], dqe, dkd, del_, do, dvi, l)
    dfq, dfk, dfv, dfc4, dfr4, dqg, dkg = _fox_bwd(n + "fox_attn", z, sv["fc"], sv["fr"], p["fox_qg"], p["fox_kg"], dy, l)
    dmisc, dbf = _foxprep_bwd(n + "fox_prep", z, p["fox_bf"], dfc4, dfr4, dmisc4, l)
    dz = jnp.concatenate([dcb, dcc, dch, dpu, dgq, dgk, dgv, dgg, dfq, dfk.astype(bf16), dfv.astype(bf16), dmisc],
                         axis=1)
    g_wi = _mm_tn(n + "mm_dwi", sv["u"], dz, 1024, 1152, jax.ShapeDtypeStruct((D, ZC), bf16),
                  pl.BlockSpec((1024, 1152), lambda i, j, k: (i, j)), (2, ZC // 1152, 1))
    token = ship.early(dict(w_in=_win_to_blocks(g_wi)))
    tb = _tm_big(s)
    du = _mm(n + "mm_du", dz, sv["w_in"],
             pl.BlockSpec((tb, 1920), lambda j, i, k: (i, k)), pl.BlockSpec((1024, 1920), lambda j, i, k: (j, k)),
             jax.ShapeDtypeStruct((s, D), f32), pl.BlockSpec((tb, 1024), lambda j, i, k: (i, j)),
             (2, s // tb, ZC // 1920), ((1,), (1,)), ZC // 1920, dep=token)
    token = ship.finish({}, du)
    dh, dhb, dg_mix = _rmsnorm_bwd(n + "norm_mix", sv["h"], p["g_mix"], du, dh1, l, dep=token)
    small = dict(
        norm_mix_g=dg_mix[0], conv_w=dconv, pool_w=dpoolw, pool_scale=dpools[0],
        gla_w_decay=jnp.concatenate([dwd[hh, GA_LANE:GA_LANE + 16, :64] for hh in range(4)], axis=1),
        gla_b_decay=jnp.concatenate([dbd[hh, 0, :64] for hh in range(4)]),
        gla_out_g=jnp.sum(dog[:, 0, :], axis=0), fox_q_g=jnp.sum(dqg[:, 0, :], axis=0),
        fox_k_g=jnp.sum(dkg[:, 0, :], axis=0), fox_b_f=dbf[0, FF_LANE:FF_LANE + 4])
    return dh, dhb, small


def _win_from_blocks(wb):
    def cols(a, b):
        parts = []
        for kk in range(NCHIP):
            lo, hi = max(a, kk * WINB), min(b, (kk + 1) * WINB)
            if lo < hi:
                parts.append(wb[kk, :, lo - kk * WINB:hi - kk * WINB])
        return parts

    zeros = lambda w: [jnp.zeros((wb.shape[1], w), wb.dtype)]
    segs = cols(0, 2048)
    for hh in range(4):
        segs += cols(2048 + 64 * hh, 2112 + 64 * hh) + zeros(64)
    for hh in range(4):
        segs += cols(2304 + 64 * hh, 2368 + 64 * hh) + zeros(64)
    segs += cols(2560, 3584) + cols(3600, 5136)
    segs += cols(5136, 5140) + zeros(GA_LANE - 4) + cols(3584, 3600) + zeros(LANE - GA_LANE - 16)
    return jnp.concatenate(segs, axis=-1)


def _win_to_blocks(g):
    mb = MISC * LANE
    segs = [(0, 2048)] + [(GQ * LANE + LANE * hh, 64) for hh in range(4)] + [(GK * LANE + LANE * hh, 64) for hh in range(4)]
    segs += [(GV * LANE, 1024), (mb + GA_LANE, 16), (FQ * LANE, 1536), (mb + FF_LANE, 4)]
    blocks, at = [[] for _ in range(NCHIP)], 0
    for start, width in segs:
        while width > 0:
            take = min(width, (at // WINB + 1) * WINB - at)
            blocks[at // WINB].append(g[:, start:start + take])
            start, width, at = start + take, width - take, at + take
    return jnp.stack([jnp.concatenate(b, axis=1) for b in blocks])


def _place():
    x, y, c = lax.axis_index("x"), lax.axis_index("y"), lax.axis_index("c")
    chips = [(1 - x, y), (x, 1 - y), (1 - x, 1 - y)]
    return x, y, c, chips


def _allgather_small(name, v):
    m_per, n = v.shape

    def body(x_ref, out_ref, send_sems, recv_sems, local_sem):
        x, y, c, chips = _place()
        me, sibling = (x, y, c), (x, y, 1 - c)

        def rows(px, py, pc):
            return out_ref.at[pl.ds((4 * px + 2 * py + pc) * m_per, m_per), :]

        def copy(k, block, to, src=None):
            return pltpu.make_async_remote_copy(
                src_ref=rows(*block) if src is None else src, dst_ref=rows(*block),
                send_sem=send_sems.at[k], recv_sem=recv_sems.at[k], device_id=to, device_id_type=MESH)

        mine = pltpu.make_async_copy(x_ref, rows(*me), local_sem)
        mine.start()
        first = [copy(0, me, sibling, src=x_ref)]
        first += [copy(1 + j, me, (*chip, c), src=x_ref) for j, chip in enumerate(chips)]
        for cp in first:
            cp.start()
        passed = [copy(4 + j, (*chip, c), sibling) for j, chip in enumerate(chips)]
        for j, chip in enumerate(chips):
            copy(1 + j, (*chip, c), me).wait_recv()
            passed[j].start()
        copy(0, sibling, me).wait_recv()
        for j, chip in enumerate(chips):
            copy(4 + j, (*chip, 1 - c), me).wait_recv()
        for cp in first + passed:
            cp.wait_send()
        mine.wait()

    return pl.pallas_call(
        body, name=name, out_shape=jax.ShapeDtypeStruct((8 * m_per, n), v.dtype),
        in_specs=[pl.BlockSpec(memory_space=pltpu.VMEM)], out_specs=pl.BlockSpec(memory_space=pltpu.VMEM),
        scratch_shapes=[pltpu.SemaphoreType.DMA((7,)), pltpu.SemaphoreType.DMA((7,)), pltpu.SemaphoreType.DMA],
    )(v)


def _hbm_specs(n):
    return [pl.BlockSpec(memory_space=pl.ANY)] * n


def _own_slot(shard, chip):
    return lax.dynamic_update_index_in_dim(lax.empty((NCHIP,) + shard.shape, shard.dtype), shard, chip, 0)


HBM = pl.BlockSpec(memory_space=pltpu.HBM)
SEM = pl.BlockSpec(memory_space=pltpu.SEMAPHORE)
EFFECT = pltpu.SideEffectType.DATAFLOW_SIDE_EFFECTING


def _in_hbm(v):
    return pltpu.with_memory_space_constraint(v, pltpu.HBM)


def _gather_start(name, groups):
    flat = [b for g in groups for b in g]
    nt, ng = len(flat), len(groups)

    def body(*refs):
        outs = refs[nt:]
        sems, bufs = outs[:2 * ng], outs[2 * ng:2 * ng + nt]
        x, y, c, chips = _place()
        me = 2 * x + y
        t = 0
        for gi, g in enumerate(groups):
            for k in range(len(g)):
                r = bufs[t].shape[1] // 2
                mine = bufs[t].at[me, pl.ds(c * r, r), :]
                for j, (px, py) in enumerate(chips):
                    pltpu.make_async_remote_copy(
                        src_ref=mine, dst_ref=mine, send_sem=sems[2 * gi].at[3 * k + j], recv_sem=sems[2 * gi + 1].at[3 * k + j],
                        device_id=(px, py, c), device_id_type=MESH).start()
                t += 1

    sem_shapes = []
    for g in groups:
        sem_shapes += [pltpu.SemaphoreType.DMA((3 * len(g),))] * 2
    out = pl.pallas_call(
        body, name=name,
        out_shape=tuple(sem_shapes + [pltpu.HBM(b.shape, b.dtype) for b in flat]),
        in_specs=tuple([HBM] * nt), out_specs=tuple([SEM] * (2 * ng) + [HBM] * nt),
        input_output_aliases={t: 2 * ng + t for t in range(nt)},
        compiler_params=pltpu.CompilerParams(has_side_effects=EFFECT),
    )(*[_in_hbm(b) for b in flat])
    sems = [(out[2 * gi], out[2 * gi + 1]) for gi in range(ng)]
    bufs, at = [], 2 * ng
    for g in groups:
        bufs.append(list(out[at:at + len(g)]))
        at += len(g)
    return sems, bufs


def _gather_wait(name, bufs, send_sems, recv_sems, after):
    nt = len(bufs)

    def body(*refs):
        ins, ss, rs = refs[:nt], refs[nt], refs[nt + 1]
        x, y, c, chips = _place()
        me = 2 * x + y
        for k in range(nt):
            r = ins[k].shape[1] // 2
            for j, (px, py) in enumerate(chips):
                cp = pltpu.make_async_remote_copy(
                    src_ref=ins[k].at[me, pl.ds(c * r, r), :], dst_ref=ins[k].at[2 * px + py, pl.ds(c * r, r), :],
                    send_sem=ss.at[3 * k + j], recv_sem=rs.at[3 * k + j], device_id=(px, py, c), device_id_type=MESH)
                cp.wait_send()
                cp.wait_recv()

    out = pl.pallas_call(
        body, name=name, out_shape=tuple(pltpu.HBM(b.shape, b.dtype) for b in bufs),
        in_specs=tuple([HBM] * nt + [SEM, SEM, pl.BlockSpec(memory_space=pl.ANY)]), out_specs=tuple([HBM] * nt),
        input_output_aliases={t: t for t in range(nt)},
        compiler_params=pltpu.CompilerParams(has_side_effects=EFFECT),
    )(*bufs, send_sems, recv_sems, after)
    return list(out)


def _gather_exchange(name, bufs):
    nt = len(bufs)

    def body(*refs):
        outs = refs[nt:2 * nt]
        send_sems, recv_sems = refs[2 * nt:]
        x, y, c, chips = _place()
        sibling = (x, y, 1 - c)

        def half(t, chip_idx, cc):
            r = outs[t].shape[1] // 2
            return outs[t].at[chip_idx, pl.ds(cc * r, r), :]

        sent = []
        for t in range(nt):
            for j, (px, py) in enumerate(chips):
                cp = pltpu.make_async_remote_copy(
                    src_ref=half(t, 2 * px + py, c), dst_ref=half(t, 2 * px + py, c),
                    send_sem=send_sems.at[t, j], recv_sem=recv_sems.at[t, j], device_id=sibling, device_id_type=MESH)
                cp.start()
                sent.append(cp)
        for t in range(nt):
            for j, (px, py) in enumerate(chips):
                pltpu.make_async_remote_copy(
                    src_ref=half(t, 2 * px + py, 1 - c), dst_ref=half(t, 2 * px + py, 1 - c),
                    send_sem=send_sems.at[t, j], recv_sem=recv_sems.at[t, j], device_id=sibling,
                    device_id_type=MESH).wait_recv()
        for cp in sent:
            cp.wait_send()

    return pl.pallas_call(
        body, name=name, out_shape=[jax.ShapeDtypeStruct(v.shape, v.dtype) for v in bufs],
        in_specs=_hbm_specs(nt), out_specs=_hbm_specs(nt), input_output_aliases={t: t for t in range(nt)},
        scratch_shapes=[pltpu.SemaphoreType.DMA((nt, 3)), pltpu.SemaphoreType.DMA((nt, 3))],
    )(*bufs)


def _rs_to_sibling(name, grads):
    nt = len(grads)

    def body(*refs):
        ins, outs = refs[:nt], refs[nt:2 * nt]
        send_sems, recv_sems = refs[2 * nt:]
        x, y, c, _ = _place()
        cps = []
        for t in range(nt):
            r = ins[t].shape[1] // 2
            cp = pltpu.make_async_remote_copy(
                src_ref=ins[t].at[:, pl.ds((1 - c) * r, r), :], dst_ref=outs[t],
                send_sem=send_sems.at[t], recv_sem=recv_sems.at[t], device_id=(x, y, 1 - c), device_id_type=MESH)
            cp.start()
            cps.append(cp)
        for cp in cps:
            cp.wait()

    return pl.pallas_call(
        body, name=name,
        out_shape=[jax.ShapeDtypeStruct((NCHIP, g.shape[1] // 2, g.shape[2]), g.dtype) for g in grads],
        in_specs=_hbm_specs(nt), out_specs=_hbm_specs(nt),
        scratch_shapes=[pltpu.SemaphoreType.DMA((nt,)), pltpu.SemaphoreType.DMA((nt,))],
    )(*grads)


def _sibling_start(name, grads):
    nt = len(grads)

    def body(*refs):
        outs = refs[2 * nt:]
        ss, rs, src, land, token = outs[0], outs[1], outs[2:2 + nt], outs[2 + nt:2 + 2 * nt], outs[2 + 2 * nt]
        x, y, c, _ = _place()
        for t in range(nt):
            r = src[t].shape[1] // 2
            pltpu.make_async_remote_copy(
                src_ref=src[t].at[:, pl.ds((1 - c) * r, r), :], dst_ref=land[t], send_sem=ss.at[t], recv_sem=rs.at[t],
                device_id=(x, y, 1 - c), device_id_type=MESH).start()
        token[...] = jnp.zeros_like(token)

    src_shapes = [pltpu.HBM(g.shape, g.dtype) for g in grads]
    land_shapes = [pltpu.HBM((NCHIP, g.shape[1] // 2, g.shape[2]), g.dtype) for g in grads]
    out = pl.pallas_call(
        body, name=name,
        out_shape=tuple([pltpu.SemaphoreType.DMA((nt,))] * 2 + src_shapes + land_shapes
                        + [jax.ShapeDtypeStruct((8, LANE), f32)]),
        in_specs=tuple([HBM] * (2 * nt)),
        out_specs=tuple([SEM, SEM] + [HBM] * (2 * nt) + [pl.BlockSpec(memory_space=pltpu.VMEM)]),
        input_output_aliases={t: 2 + t for t in range(2 * nt)},
        compiler_params=pltpu.CompilerParams(has_side_effects=EFFECT),
    )(*[_in_hbm(g) for g in grads], *[_in_hbm(lax.empty(s_.shape, s_.dtype)) for s_ in land_shapes])
    return out[0], out[1], list(out[2:2 + nt]), list(out[2 + nt:2 + 2 * nt]), out[2 + 2 * nt]


def _sibling_wait(name, grads, land, send_sems, recv_sems, after):
    nt = len(grads)

    def body(*refs):
        src, dst, ss, rs = refs[:nt], refs[nt:2 * nt], refs[2 * nt], refs[2 * nt + 1]
        x, y, c, _ = _place()
        for t in range(nt):
            r = src[t].shape[1] // 2
            cp = pltpu.make_async_remote_copy(
                src_ref=src[t].at[:, pl.ds((1 - c) * r, r), :], dst_ref=dst[t], send_sem=ss.at[t], recv_sem=rs.at[t],
                device_id=(x, y, 1 - c), device_id_type=MESH)
            cp.wait_send()
            cp.wait_recv()

    shapes = [pltpu.HBM(v.shape, v.dtype) for v in list(grads) + list(land)]
    out = pl.pallas_call(
        body, name=name, out_shape=tuple(shapes),
        in_specs=tuple([HBM] * (2 * nt) + [SEM, SEM, pl.BlockSpec(memory_space=pl.ANY)]),
        out_specs=tuple([HBM] * (2 * nt)), input_output_aliases={t: t for t in range(2 * nt)},
        compiler_params=pltpu.CompilerParams(has_side_effects=EFFECT),
    )(*grads, *land, send_sems, recv_sems, after)
    return list(out[:nt]), list(out[nt:])


def _rs_pair_sum(name, pos, g, other):
    r, cdim = other.shape[1], other.shape[2]
    tr = r // 4 if (r // 4) % 16 == 0 else r // 2
    nblk = r // tr

    def body(pos_ref, g_ref, o_ref, s_ref):
        s_ref[...] = (g_ref[...].astype(f32) + o_ref[...].astype(f32)).astype(bf16)

    blk = pl.BlockSpec((None, tr, cdim), lambda q, i, p: (q, i, 0))
    return pl.pallas_call(
        body, name=name, out_shape=jax.ShapeDtypeStruct(other.shape, bf16),
        grid_spec=pltpu.PrefetchScalarGridSpec(
            num_scalar_prefetch=1, grid=(NCHIP, nblk),
            in_specs=[pl.BlockSpec((None, tr, cdim), lambda q, i, p: (q, p[1] * nblk + i, 0)), blk], out_specs=blk),
        compiler_params=_cp(("parallel", "parallel")),
    )(pos, g, other)


def _scatter_start(name, sums):
    nt = len(sums)

    def body(*refs):
        outs = refs[2 * nt:]
        ss, rs, src, land, token = outs[0], outs[1], outs[2:2 + nt], outs[2 + nt:2 + 2 * nt], outs[2 + 2 * nt]
        x, y, c, chips = _place()
        me = 2 * x + y
        for t in range(nt):
            for j, (px, py) in enumerate(chips):
                pltpu.make_async_remote_copy(
                    src_ref=src[t].at[2 * px + py], dst_ref=land[t].at[me], send_sem=ss.at[3 * t + j], recv_sem=rs.at[3 * t + j],
                    device_id=(px, py, c), device_id_type=MESH).start()
        token[...] = jnp.zeros_like(token)

    shapes = [pltpu.HBM(v.shape, v.dtype) for v in sums]
    out = pl.pallas_call(
        body, name=name,
        out_shape=tuple([pltpu.SemaphoreType.DMA((3 * nt,))] * 2 + shapes + shapes + [jax.ShapeDtypeStruct((8, LANE), f32)]),
        in_specs=tuple([HBM] * (2 * nt)),
        out_specs=tuple([SEM, SEM] + [HBM] * (2 * nt) + [pl.BlockSpec(memory_space=pltpu.VMEM)]),
        input_output_aliases={t: 2 + t for t in range(2 * nt)},
        compiler_params=pltpu.CompilerParams(has_side_effects=EFFECT),
    )(*[_in_hbm(v) for v in sums], *[_in_hbm(lax.empty(v.shape, v.dtype)) for v in sums])
    return out[0], out[1], list(out[2:2 + nt]), list(out[2 + nt:2 + 2 * nt]), out[2 + 2 * nt]


def _scatter_wait(name, sums, land, send_sems, recv_sems, after):
    nt = len(sums)

    def body(*refs):
        src, dst, ss, rs = refs[:nt], refs[nt:2 * nt], refs[2 * nt], refs[2 * nt + 1]
        x, y, c, chips = _place()
        for t in range(nt):
            for j, (px, py) in enumerate(chips):
                cp = pltpu.make_async_remote_copy(
                    src_ref=src[t].at[2 * px + py], dst_ref=dst[t].at[2 * px + py], send_sem=ss.at[3 * t + j],
                    recv_sem=rs.at[3 * t + j], device_id=(px, py, c), device_id_type=MESH)
                cp.wait_send()
                cp.wait_recv()

    shapes = [pltpu.HBM(v.shape, v.dtype) for v in sums]
    out = pl.pallas_call(
        body, name=name, out_shape=tuple(shapes + shapes),
        in_specs=tuple([HBM] * (2 * nt) + [SEM, SEM, pl.BlockSpec(memory_space=pl.ANY)]),
        out_specs=tuple([HBM] * (2 * nt)), input_output_aliases={t: t for t in range(2 * nt)},
        compiler_params=pltpu.CompilerParams(has_side_effects=EFFECT),
    )(*sums, *land, send_sems, recv_sems, after)
    return list(out[:nt]), list(out[nt:])


def _rs_chip_sum(name, pos, sums, parts):
    r, cdim = parts.shape[1], parts.shape[2]
    tr = r // 4 if (r // 4) % 16 == 0 else r // 2
    nblk = r // tr

    def body(pos_ref, own_ref, a_ref, b_ref, c_ref, o_ref):
        o_ref[...] = ((own_ref[...].astype(f32) + a_ref[...].astype(f32)) + b_ref[...].astype(f32)) \
            + c_ref[...].astype(f32)

    def slot(k):
        return pl.BlockSpec((None, tr, cdim), lambda i, p: ((p[0] + k) % NCHIP, i, 0))

    return pl.pallas_call(
        body, name=name, out_shape=jax.ShapeDtypeStruct((2 * r, cdim), f32),
        grid_spec=pltpu.PrefetchScalarGridSpec(
            num_scalar_prefetch=1, grid=(nblk,), in_specs=[slot(0), slot(1), slot(2), slot(3)],
            out_specs=pl.BlockSpec((tr, cdim), lambda i, p: (p[1] * nblk + i, 0))),
        compiler_params=_cp(("parallel",)),
    )(pos, sums, parts, parts, parts)


def _rs_share_halves(name, bufs):
    nt = len(bufs)

    def body(*refs):
        outs = refs[nt:2 * nt]
        send_sems, recv_sems = refs[2 * nt:]
        x, y, c, _ = _place()
        cps = []
        for t in range(nt):
            r = outs[t].shape[0] // 2
            mine = outs[t].at[pl.ds(c * r, r), :]
            theirs = outs[t].at[pl.ds((1 - c) * r, r), :]
            cp = pltpu.make_async_remote_copy(
                src_ref=mine, dst_ref=mine, send_sem=send_sems.at[t], recv_sem=recv_sems.at[t],
                device_id=(x, y, 1 - c), device_id_type=MESH)
            cp.start()
            cps.append((cp, theirs))
        for t, (cp, theirs) in enumerate(cps):
            pltpu.make_async_remote_copy(
                src_ref=theirs, dst_ref=theirs, send_sem=send_sems.at[t], recv_sem=recv_sems.at[t],
                device_id=(x, y, 1 - c), device_id_type=MESH).wait_recv()
            cp.wait_send()

    return pl.pallas_call(
        body, name=name, out_shape=[jax.ShapeDtypeStruct(v.shape, v.dtype) for v in bufs],
        in_specs=_hbm_specs(nt), out_specs=_hbm_specs(nt), input_output_aliases={t: t for t in range(nt)},
        scratch_shapes=[pltpu.SemaphoreType.DMA((nt,)), pltpu.SemaphoreType.DMA((nt,))],
    )(*bufs)


BIG = ("w_in", "w_out", "w_gate", "w_up", "w_down")


GROUPS = (("w_in",), ("w_out",), ("w_gate", "w_up", "w_down"))


class _Ship:
    def __init__(self, tag, pos):
        self.tag, self.pos, self.started, self.state = tag, pos, None, None

    def early(self, grads):
        self.early_keys = tuple(grads)
        self.started = _sibling_start(self.tag + "sibling_start", [grads[k] for k in self.early_keys])
        return self.started[-1]

    def finish(self, grads, after):
        keys, mine, got = tuple(grads), [grads[k] for k in grads], []
        if keys:
            got = list(_rs_to_sibling(self.tag + "to_sibling", mine))
        if self.started is not None:
            send_sems, recv_sems, src, land, _ = self.started
            src, land = _sibling_wait(self.tag + "sibling_wait", src, land, send_sems, recv_sems, after)
            keys, mine, got = self.early_keys + keys, src + mine, land + got
        sums = [_rs_pair_sum(self.tag + "pair_sum_" + k, self.pos, g, o) for k, g, o in zip(keys, mine, got)]
        self.state = (keys,) + _scatter_start(self.tag + "start", sums)
        return self.state[-1]


def _rs_end(tag, state, pos, after):
    keys, send_sems, recv_sems, sums, land, _ = state
    sums, land = _scatter_wait(tag + "wait", sums, land, send_sems, recv_sems, after)
    halves = [_rs_chip_sum(tag + "chip_sum_" + k, pos, s, v) for k, s, v in zip(keys, sums, land)]
    return dict(zip(keys, _rs_share_halves(tag + "share", halves)))


def _adam_math(w, g, m, v):
    m = ADAM_B1 * m + (1.0 - ADAM_B1) * g
    v = ADAM_B2 * v + (1.0 - ADAM_B2) * (g * g)
    m_hat = m / (1.0 - ADAM_B1 ** ADAM_STEP)
    v_hat = v / (1.0 - ADAM_B2 ** ADAM_STEP)
    delta = -ADAM_LR * (m_hat / (jnp.sqrt(v_hat) + ADAM_EPS) + ADAM_WD * w)
    return delta, m, v


def _adam_big(name, g0, g1, w, m, v):
    _, r, cdim = w.shape
    tr = 128 if r % 128 == 0 else 64
    nb = r // tr

    def body(g0_ref, g1_ref, w_ref, m_ref, v_ref, go_ref, d_ref, mo_ref, vo_ref):
        l = pl.program_id(0)
        g = jnp.where(l == 0, g0_ref[...], g1_ref[...])
        delta, mn, vn = _adam_math(w_ref[...], g, m_ref[...], v_ref[...])
        go_ref[...] = g
        d_ref[...] = delta
        mo_ref[...] = mn
        vo_ref[...] = vn

    lay = pl.BlockSpec((None, tr, cdim), lambda l, i: (l, i, 0))
    return pl.pallas_call(
        body, name=name, grid=(2, nb),
        in_specs=[pl.BlockSpec((tr, cdim), lambda l, i: (i * (1 - l) + (nb - 1) * l, 0)),
                  pl.BlockSpec((tr, cdim), lambda l, i: (i * l, 0)), lay, lay, lay],
        out_specs=[lay] * 4, out_shape=[jax.ShapeDtypeStruct(w.shape, f32)] * 4,
        compiler_params=_cp(("arbitrary", "arbitrary")),
    )(g0, g1, w, m, v)


def _sum8(name, gathered):
    m_per = gathered.shape[0] // 8

    def body(g_ref, o_ref):
        tot = g_ref[pl.ds(0, m_per), :]
        for d in range(1, 8):
            tot = tot + g_ref[pl.ds(d * m_per, m_per), :]
        o_ref[...] = tot

    return pl.pallas_call(body, name=name, out_shape=jax.ShapeDtypeStruct((m_per, LANE), f32))(gathered)


def _adam_small(name, g, w, m, v):
    def body(g_ref, w_ref, m_ref, v_ref, d_ref, mo_ref, vo_ref):
        delta, mn, vn = _adam_math(w_ref[...], g_ref[...], m_ref[...], v_ref[...])
        d_ref[...] = delta
        mo_ref[...] = mn
        vo_ref[...] = vn

    return pl.pallas_call(body, name=name, out_shape=[jax.ShapeDtypeStruct(g.shape, f32)] * 3)(g, w, m, v)


def _pack(vals):
    rows, offs, at = [], [], 0
    for a in vals:
        a = a.reshape(-1)
        n = -(-a.shape[0] // (8 * LANE)) * 8
        rows.append(jnp.pad(a, (0, n * LANE - a.shape[0])).reshape(n, LANE))
        offs.append(at)
        at += n
    return jnp.concatenate(rows, axis=0), offs


def _unpack(packed, offs, shapes):
    out = []
    for o, shp in zip(offs, shapes):
        size = 1
        for d in shp:
            size *= d
        n = -(-size // LANE)
        out.append(packed[o:o + n].reshape(-1)[:size].reshape(shp))
    return out


SMALL = ("norm_mix_g", "conv_w", "pool_w", "pool_scale", "gla_w_decay", "gla_b_decay", "gla_out_g",
         "fox_q_g", "fox_k_g", "fox_b_f", "norm_ffn_g")
ALL = ("norm_mix_g", "w_in", "conv_w", "pool_w", "pool_scale", "gla_w_decay", "gla_b_decay", "gla_out_g",
       "fox_q_g", "fox_k_g", "fox_b_f", "w_out", "norm_ffn_g", "w_gate", "w_up", "w_down")


def kernel(x, norm_mix_g, w_in, conv_w, pool_w, pool_scale, gla_w_decay, gla_b_decay, gla_out_g, fox_q_g, fox_k_g, fox_b_f, w_out, norm_ffn_g, w_gate, w_up, w_down, loss_target, m_norm_mix_g, m_w_in, m_conv_w, m_pool_w, m_pool_scale, m_gla_w_decay, m_gla_b_decay, m_gla_out_g, m_fox_q_g, m_fox_k_g, m_fox_b_f, m_w_out, m_norm_ffn_g, m_w_gate, m_w_up, m_w_down, v_norm_mix_g, v_w_in, v_conv_w, v_pool_w, v_pool_scale, v_gla_w_decay, v_gla_b_decay, v_gla_out_g, v_fox_q_g, v_fox_k_g, v_fox_b_f, v_w_out, v_norm_ffn_g, v_w_gate, v_w_up, v_w_down):
    w = dict(norm_mix_g=norm_mix_g, w_in=w_in, conv_w=conv_w, pool_w=pool_w, pool_scale=pool_scale,
             gla_w_decay=gla_w_decay, gla_b_decay=gla_b_decay, gla_out_g=gla_out_g, fox_q_g=fox_q_g, fox_k_g=fox_k_g,
             fox_b_f=fox_b_f, w_out=w_out, norm_ffn_g=norm_ffn_g, w_gate=w_gate, w_up=w_up, w_down=w_down)
    m = dict(norm_mix_g=m_norm_mix_g, w_in=m_w_in, conv_w=m_conv_w, pool_w=m_pool_w, pool_scale=m_pool_scale,
             gla_w_decay=m_gla_w_decay, gla_b_decay=m_gla_b_decay, gla_out_g=m_gla_out_g, fox_q_g=m_fox_q_g,
             fox_k_g=m_fox_k_g, fox_b_f=m_fox_b_f, w_out=m_w_out, norm_ffn_g=m_norm_ffn_g, w_gate=m_w_gate,
             w_up=m_w_up, w_down=m_w_down)
    v = dict(norm_mix_g=v_norm_mix_g, w_in=v_w_in, conv_w=v_conv_w, pool_w=v_pool_w, pool_scale=v_pool_scale,
             gla_w_decay=v_gla_w_decay, gla_b_decay=v_gla_b_decay, gla_out_g=v_gla_out_g, fox_q_g=v_fox_q_g,
             fox_k_g=v_fox_k_g, fox_b_f=v_fox_b_f, w_out=v_w_out, norm_ffn_g=v_norm_ffn_g, w_gate=v_w_gate,
             w_up=v_w_up, w_down=v_w_down)
    chip = 2 * lax.axis_index("x") + lax.axis_index("y")

    pos = jnp.stack([chip, lax.axis_index("c")]).astype(jnp.int32)

    mine, offs = _pack([conv_w, gla_w_decay, jnp.zeros((8, LANE), f32)])
    order = [(l, grp) for l in range(2) for grp in GROUPS]
    sems, gbufs = _gather_start(
        "gather_start",
        [[_own_slot(mine, chip)]] + [[_own_slot(w[k][l].astype(bf16), chip) for k in grp] for l, grp in order])

    def gathered(tag, gi, after):
        got = _gather_wait(tag + "_wait", gbufs[gi], sems[gi][0], sems[gi][1], after)
        return _gather_exchange(tag + "_exchange", got)

    def weights(l, group, after):
        got = gathered(f"l{l}_gather_{group}", 1 + 3 * l + ("w_in", "w_out", "ffn").index(group), after)
        if group == "w_in":
            return _win_from_blocks(got[0])
        return got[0] if group == "w_out" else got

    every = gathered("gather_small", 0, x)[0]
    per_chip = [_unpack(every[kk], offs, [conv_w.shape, gla_w_decay.shape]) for kk in range(NCHIP)]
    conv_full = jnp.concatenate([pc[0] for pc in per_chip], axis=-1)[:, :, 0, :]
    wdec_full = jnp.concatenate([pc[1] for pc in per_chip], axis=-1)

    wdec = jnp.pad(wdec_full.reshape(2, 16, 4, 64), ((0, 0), (GA_LANE, LANE - GA_LANE - 16), (0, 0), (0, 64)))
    p = dict(
        g_mix=norm_mix_g[:, None, :], g_ffn=norm_ffn_g[:, None, :],
        conv_w=conv_full, pool_w=pool_w, pool_scale=pool_scale[:, None, :],
        wdec=wdec.reshape(2, LANE, G),
        bdec=jnp.pad(gla_b_decay.reshape(2, 4, 64), ((0, 0), (0, 0), (0, 64))).reshape(2, 1, G),
        gla_og=gla_out_g[:, None, :], fox_qg=fox_q_g[:, None, :], fox_kg=fox_k_g[:, None, :],
        fox_bf=jnp.pad(fox_b_f, ((0, 0), (FF_LANE, LANE - FF_LANE - 4)))[:, None, :])

    h0 = x[0]
    h1, sv0 = _layer_fwd(0, h0, p, weights)
    h2, sv1 = _layer_fwd(1, h1, p, weights)
    sq, dh, dhb = _loss("loss", h2, loss_target[0])
    loss = lax.psum(sq[0, 0] * (0.5 / D), ("x", "y", "c"))

    late = ("w_out", "w_gate", "w_up", "w_down")
    ships = {tag: _Ship(tag, pos) for tag in ("l1_rs_a_", "l1_rs_b_", "l0_rs_a_", "l0_rs_b_")}
    dh1, dy, dgf1 = _layer_bwd_ffn(1, dh, dhb, p, sv1, ships["l1_rs_a_"])
    dh, dhb, small1 = _layer_bwd_mix(1, dh1, dy, p, sv1, ships["l1_rs_b_"])
    dh1, dy, dgf0 = _layer_bwd_ffn(0, dh, dhb, p, sv0, ships["l0_rs_a_"])
    dh, dhb, small0 = _layer_bwd_mix(0, dh1, dy, p, sv0, ships["l0_rs_b_"])
    small0["norm_ffn_g"], small1["norm_ffn_g"] = dgf0[0], dgf1[0]

    red1 = _rs_end("l1_rs_a_", ships["l1_rs_a_"].state, pos, dh)
    red1.update(_rs_end("l1_rs_b_", ships["l1_rs_b_"].state, pos, red1["w_down"]))
    red0 = _rs_end("l0_rs_a_", ships["l0_rs_a_"].state, pos, red1["w_in"])
    grads, deltas, new_m, new_v = {}, {}, {}, {}
    for k in late:
        grads[k], deltas[k], new_m[k], new_v[k] = _adam_big("adam_" + k, red0[k], red1[k], w[k], m[k], v[k])

    packed, goffs = _pack([jnp.stack([small0[k], small1[k]]) for k in SMALL])
    total = _sum8("sum_small_grads", _allgather_small("gather_small_grads", packed))
    red0.update(_rs_end("l0_rs_b_", ships["l0_rs_b_"].state, pos, total))
    k = "w_in"
    grads[k], deltas[k], new_m[k], new_v[k] = _adam_big("adam_" + k, red0[k], red1[k], w[k], m[k], v[k])
    full_shapes = [(2,) + small0[k].shape for k in SMALL]
    gsmall = dict(zip(SMALL, _unpack(total, goffs, full_shapes)))
    gsmall["conv_w"] = lax.dynamic_slice_in_dim(gsmall["conv_w"], chip * LANE, LANE, axis=2)[:, :, None, :]
    gsmall["gla_w_decay"] = lax.dynamic_slice_in_dim(gsmall["gla_w_decay"], chip * 64, 64, axis=2)
    gp, loffs = _pack([gsmall[k] for k in SMALL])
    wp, _ = _pack([w[k] for k in SMALL])
    mp, _ = _pack([m[k] for k in SMALL])
    vp, _ = _pack([v[k] for k in SMALL])
    dp, mnp, vnp = _adam_small("adam_small", gp, wp, mp, vp)
    shapes = [w[k].shape for k in SMALL]
    for k, a, b, c_, d_ in zip(SMALL, _unpack(gp, loffs, shapes), _unpack(dp, loffs, shapes),
                               _unpack(mnp, loffs, shapes), _unpack(vnp, loffs, shapes)):
        grads[k], deltas[k], new_m[k], new_v[k] = a, b, c_, d_

    return (loss, dh[None], *[grads[k] for k in ALL], *[deltas[k] for k in ALL],
            *[new_m[k] for k in ALL], *[new_v[k] for k in ALL])
```

```python
import functools

import jax
import jax.numpy as jnp
from jax import lax
from jax.experimental import pallas as pl
from jax.experimental.pallas import tpu as pltpu

f32 = jnp.float32
bf16 = jnp.bfloat16

D = 2048
G = 512
DFF = 5632
NCHIP = 4
FB = DFF // NCHIP
WIN = 5140
WINB = WIN // NCHIP
EPS = 1e-6
CHUNK = 64
LANE = 128

CB, CC, CH, PU, GQ, GK, GV, GG, FQ, FK, FV, MISC = 0, 4, 8, 12, 16, 20, 24, 28, 32, 36, 40, 44
ZC = 45 * LANE
FF_LANE = 0
GA_LANE = 8

ADAM_LR, ADAM_B1, ADAM_B2, ADAM_EPS, ADAM_WD, ADAM_STEP = 0.001, 0.9, 0.999, 1e-08, 0.01, 10

VMEM_LIMIT = 60 * 1024 * 1024
MESH = pl.DeviceIdType.MESH


def _cp(sem=None):
    return pltpu.CompilerParams(dimension_semantics=sem, vmem_limit_bytes=VMEM_LIMIT)


def _dot(a, b, dims=((1,), (0,))):
    return lax.dot_general(a.astype(bf16), b.astype(bf16), (dims, ((), ())), preferred_element_type=f32)


def _bdot(a, b, ca, cb):
    return lax.dot_general(a.astype(bf16), b.astype(bf16), (((ca,), (cb,)), ((0,), (0,))),
                           preferred_element_type=f32)


def _log_sigmoid(x):
    return jnp.minimum(x, 0.0) - jnp.log(1.0 + jnp.exp(-jnp.abs(x)))


@jax.custom_vjp
def _sigmoid(x):
    return 1.0 / (1.0 + jnp.exp(-x))


def _sigmoid_fwd(x):
    s = _sigmoid(x)
    return s, s


def _sigmoid_bwd(s, g):
    return (g * s * (1.0 - s),)


_sigmoid.defvjp(_sigmoid_fwd, _sigmoid_bwd)


def _rms(x, g):
    return x * lax.rsqrt(jnp.mean(x * x, axis=-1, keepdims=True) + EPS) * g


def _shift_impl(x, n, period, transpose):
    rows = x.shape[0]
    t = lax.broadcasted_iota(jnp.int32, x.shape, 0)
    if period is not None:
        t = t & (period - 1)
    keep = t >= n
    if not transpose:
        return jnp.where(keep, pltpu.roll(x, n, 0), 0.0)
    return pltpu.roll(jnp.where(keep, x, 0.0), rows - n, 0)


def _shift(x, n, period=None):
    @jax.custom_vjp
    def f(v):
        return _shift_impl(v, n, period, False)

    def fwd(v):
        return f(v), None

    def bwd(_, g):
        return (_shift_impl(g, n, period, True),)

    f.defvjp(fwd, bwd)
    return f(x)


def _cumsum_rows(x, length, period=None):
    n = 1
    while n < length:
        x = x + _shift(x, n, period)
        n *= 2
    return x


def _convpool_fn(cb, cc, ch, pu, w0, w1, w2, pw, ps, j):
    u = cc * ch
    y = w2 * u + w1 * _shift(u, 1) + w0 * _shift(u, 2)
    ya = cb * y
    s2 = pu + _shift(pu, 1)
    s4 = s2 + _shift(s2, 2)
    s8 = s4 + _shift(s4, 4)
    s16 = s8 + _shift(s8, 8)
    wsum = jnp.where(j == 0, s2, jnp.where(j == 1, s4, jnp.where(j == 2, s8, s16)))
    width = (2 << j).astype(f32)
    t = lax.broadcasted_iota(jnp.int32, pu.shape, 0).astype(f32)
    count = jnp.minimum(t + 1.0, width)
    d = wsum / count - pu
    yb = _dot(d, pw) * ps
    return ya, yb


def _foxprep_fn(misc, bf):
    lf = _log_sigmoid(misc + bf)
    fc = _cumsum_rows(lf, lf.shape[0])
    return fc, jnp.transpose(fc)


def _fox_fn(q, k, v, fcol, frow8, qg, kg, h, i):
    tq, s = q.shape[0], k.shape[0]
    qn = _rms(q, qg)
    kn = _rms(k, kg)
    lg = _dot(qn, kn, ((1,), (1,))) * (LANE ** -0.5)
    lane = lax.broadcasted_iota(jnp.int32, fcol.shape, 1)
    fq = jnp.sum(jnp.where(lane == h, fcol, 0.0), axis=1, keepdims=True)
    row = lax.broadcasted_iota(jnp.int32, frow8.shape, 0)
    fk = jnp.sum(jnp.where(row == h, frow8, 0.0), axis=0, keepdims=True)
    lg = lg + fq - fk
    qpos = i * tq + lax.broadcasted_iota(jnp.int32, (tq, s), 0)
    kpos = lax.broadcasted_iota(jnp.int32, (tq, s), 1)
    lg = jnp.where(kpos <= qpos, lg, -jnp.inf)
    m = lax.stop_gradient(jnp.max(lg, axis=1, keepdims=True))
    e = jnp.exp(lg - m)
    p = e / jnp.sum(e, axis=1, keepdims=True)
    return _dot(p, v)


def _gla1_fn(q, k, v, misc, wd, bd):
    ts = q.shape[0]
    nb = ts // CHUNK
    x = _dot(misc, wd) + bd
    la = _log_sigmoid(x) * (1.0 / 16.0)
    cc = _cumsum_rows(la, CHUNK, CHUNK)
    la3 = la.reshape(nb, CHUNK, LANE)
    last3 = jnp.sum(la3, axis=1, keepdims=True)
    last2 = jnp.sum(la3, axis=1)
    cc3 = cc.reshape(nb, CHUNK, LANE)
    q3 = (q * 0.125).reshape(nb, CHUNK, LANE)
    k3 = k.reshape(nb, CHUNK, LANE)
    v3 = v.reshape(nb, CHUNK, LANE)
    ep = jnp.exp(cc3)
    en = jnp.exp(-cc3)
    qe = q3 * ep
    a1 = _bdot(qe, k3 * en, 2, 2)
    a2 = _bdot(q3 * en, k3 * ep, 2, 2)
    ti = lax.broadcasted_iota(jnp.int32, a1.shape, 1)
    si = lax.broadcasted_iota(jnp.int32, a1.shape, 2)
    sc = jnp.where(si <= ti, a1, a2)
    oi = _bdot(sc, v3, 2, 1)
    kd = k3 * jnp.exp(last3 - cc3)
    el = jnp.exp(last2)
    return qe.reshape(ts, LANE), kd.reshape(ts, LANE), el, oi.reshape(ts, LANE)


def _gla3_fn(o, gg, og):
    return _rms(o, og) * (gg * _sigmoid(gg))


def _ffn_fn(gate, up):
    return gate * _sigmoid(gate) * up


def _mm(name, a, b, a_spec, b_spec, out_shape, out_spec, grid, dims, nk, res=None, res_spec=None, dep=None):
    has_res = res is not None
    has_dep = dep is not None
    nax = len(grid)

    def body(*refs):
        a_ref, b_ref = refs[0], refs[1]
        res_ref = refs[2] if has_res else None
        out_ref = refs[2 + has_res + has_dep]
        part = _dot(a_ref[...], b_ref[...], dims)
        if nk == 1:
            if has_res:
                part = part + res_ref[...]
            out_ref[...] = part.astype(out_ref.dtype)
            return
        acc_ref = refs[3 + has_res + has_dep]
        k = pl.program_id(nax - 1)

        @pl.when(k == 0)
        def _():
            acc_ref[...] = part

        @pl.when(k > 0)
        def _():
            acc_ref[...] += part

        @pl.when(k == nk - 1)
        def _():
            tot = acc_ref[...]
            if has_res:
                tot = tot + res_ref[...]
            out_ref[...] = tot.astype(out_ref.dtype)

    ops = [a, b] + ([res] if has_res else []) + ([dep] if has_dep else [])
    specs = [a_spec, b_spec] + ([res_spec] if has_res else [])
    if has_dep:
        specs.append(pl.BlockSpec((8, LANE), lambda *_: (0, 0)))
    blk = tuple(d for d in out_spec.block_shape if d is not None)
    scratch = [pltpu.VMEM(blk, f32)] if nk > 1 else []
    return pl.pallas_call(
        body, name=name, grid=grid, in_specs=specs, out_specs=out_spec, out_shape=out_shape,
        scratch_shapes=scratch,
        compiler_params=_cp(("parallel",) * (nax - 1) + ("arbitrary",)),
    )(*ops)


def _tm(s):
    return min(s, 512)


def _tm_big(s):
    return min(s, 1024)


def _mm_gate_up(name, u, w_gate, w_up):
    s = u.shape[0]
    tm = _tm(s)

    def body(u_ref, wg_ref, wu_ref, g_ref, up_ref, a_ref):
        g = _dot(u_ref[...], wg_ref[...])
        up = _dot(u_ref[...], wu_ref[...])
        g_ref[...] = g
        up_ref[...] = up
        a_ref[...] = _ffn_fn(g, up).astype(bf16)

    wspec = pl.BlockSpec((None, D, FB), lambda j, i: (j, 0, 0))
    tile = pl.BlockSpec((tm, FB), lambda j, i: (i, j))
    return pl.pallas_call(
        body, name=name, grid=(NCHIP, s // tm),
        in_specs=[pl.BlockSpec((tm, D), lambda j, i: (i, 0)), wspec, wspec], out_specs=[tile, tile, tile],
        out_shape=[jax.ShapeDtypeStruct((s, DFF), f32), jax.ShapeDtypeStruct((s, DFF), f32),
                   jax.ShapeDtypeStruct((s, DFF), bf16)],
        compiler_params=_cp(("parallel", "parallel")),
    )(u, w_gate, w_up)


def _mm_dact(name, dh, w_down, gate, up):
    s = dh.shape[0]
    tm = _tm(s)

    def body(dh_ref, wd_ref, g_ref, up_ref, dg_ref, du_ref):
        dact = _dot(dh_ref[...], wd_ref[...], ((1,), (1,)))
        _, vjp = jax.vjp(_ffn_fn, g_ref[...], up_ref[...])
        dg, du = vjp(dact)
        dg_ref[...] = dg.astype(bf16)
        du_ref[...] = du.astype(bf16)

    tile = pl.BlockSpec((tm, FB), lambda j, i: (i, j))
    return pl.pallas_call(
        body, name=name, grid=(NCHIP, s // tm),
        in_specs=[pl.BlockSpec((tm, D), lambda j, i: (i, 0)), pl.BlockSpec((None, FB, D), lambda j, i: (j, 0, 0)),
                  tile, tile],
        out_specs=[tile, tile], out_shape=[jax.ShapeDtypeStruct((s, DFF), bf16)] * 2,
        compiler_params=_cp(("parallel", "parallel")),
    )(dh, w_down, gate, up)


def _rmsnorm_fwd(name, x, g, l):
    s = x.shape[0]
    tm = min(s, 256)

    def body(x_ref, g_ref, u_ref):
        u_ref[...] = _rms(x_ref[...], g_ref[...]).astype(bf16)

    return pl.pallas_call(
        body, name=name, grid=(s // tm,),
        in_specs=[pl.BlockSpec((tm, D), lambda i: (i, 0)), pl.BlockSpec((None, 1, D), lambda i: (l, 0, 0))],
        out_specs=pl.BlockSpec((tm, D), lambda i: (i, 0)),
        out_shape=jax.ShapeDtypeStruct((s, D), bf16), compiler_params=_cp(("parallel",)),
    )(x, g)


def _rmsnorm_bwd(name, x, g, du, dres, l, dep=None):
    s = x.shape[0]
    tm = min(s, 256)

    def body(x_ref, g_ref, du_ref, dres_ref, *rest):
        dx_ref, dxb_ref, dg_ref = rest[-3:]
        _, vjp = jax.vjp(_rms, x_ref[...], g_ref[...])
        dx, dg = vjp(du_ref[...])
        tot = dx + dres_ref[...]
        dx_ref[...] = tot
        dxb_ref[...] = tot.astype(bf16)

        @pl.when(pl.program_id(0) == 0)
        def _():
            dg_ref[...] = dg

        @pl.when(pl.program_id(0) > 0)
        def _():
            dg_ref[...] += dg

    row = pl.BlockSpec((tm, D), lambda i: (i, 0))
    deps = [] if dep is None else [dep]
    return pl.pallas_call(
        body, name=name, grid=(s // tm,),
        in_specs=[row, pl.BlockSpec((None, 1, D), lambda i: (l, 0, 0)), row, row]
        + [pl.BlockSpec((8, LANE), lambda i: (0, 0)) for _ in deps],
        out_specs=[row, row, pl.BlockSpec((1, D), lambda i: (0, 0))],
        out_shape=[jax.ShapeDtypeStruct((s, D), f32), jax.ShapeDtypeStruct((s, D), bf16),
                   jax.ShapeDtypeStruct((1, D), f32)],
        compiler_params=_cp(("arbitrary",)),
    )(x, g, du, dres, *deps)


def _loss(name, y, t):
    s = y.shape[0]
    tm = min(s, 256)
    row = pl.BlockSpec((tm, D), lambda i: (i, 0))

    def body(y_ref, t_ref, l_ref, d_ref, db_ref):
        e = y_ref[...] - t_ref[...]
        d = e * (1.0 / D)
        d_ref[...] = d
        db_ref[...] = d.astype(bf16)
        part = jnp.zeros((8, LANE), f32) + jnp.sum(e * e)

        @pl.when(pl.program_id(0) == 0)
        def _():
            l_ref[...] = part

        @pl.when(pl.program_id(0) > 0)
        def _():
            l_ref[...] += part

    return pl.pallas_call(
        body, name=name, grid=(s // tm,), in_specs=[row, row],
        out_specs=[pl.BlockSpec((8, LANE), lambda i: (0, 0)), row, row],
        out_shape=[jax.ShapeDtypeStruct((8, LANE), f32), jax.ShapeDtypeStruct((s, D), f32),
                   jax.ShapeDtypeStruct((s, D), bf16)],
        compiler_params=_cp(("arbitrary",)),
    )(y, t)


def _zspec(s, blk):
    return pl.BlockSpec((s, LANE), lambda j: (0, blk + j))


def _convpool_specs(s, l):
    return [_zspec(s, CB), _zspec(s, CC), _zspec(s, CH), _zspec(s, PU),
            pl.BlockSpec((None, 3, LANE), lambda j: (l, 0, j)),
            pl.BlockSpec((None, None, LANE, LANE), lambda j: (l, j, 0, 0)),
            pl.BlockSpec((None, 1, LANE), lambda j: (l, 0, j))]


def _convpool_fwd(name, z, conv_w, pool_w, pool_scale, l):
    s = z.shape[0]

    def body(cb, cc, ch, pu, cw, pw, ps, ya_ref, yb_ref):
        ya, yb = _convpool_fn(cb[...], cc[...], ch[...], pu[...], cw[0:1, :], cw[1:2, :], cw[2:3, :], pw[...], ps[...],
                              pl.program_id(0))
        ya_ref[...] = ya.astype(bf16)
        yb_ref[...] = yb.astype(bf16)

    col = pl.BlockSpec((s, LANE), lambda j: (0, j))
    return pl.pallas_call(
        body, name=name, grid=(4,), in_specs=_convpool_specs(s, l), out_specs=[col, col],
        out_shape=[jax.ShapeDtypeStruct((s, G), bf16)] * 2, compiler_params=_cp(("parallel",)),
    )(z, z, z, z, conv_w, pool_w, pool_scale)


def _convpool_bwd(name, z, conv_w, pool_w, pool_scale, dy, l):
    s = z.shape[0]

    def body(cb, cc, ch, pu, cw, pw, ps, dya, dyb, dcb, dcc, dch, dpu, dcw, dpw, dps):
        j = pl.program_id(0)
        fn = functools.partial(_convpool_fn, j=j)
        _, vjp = jax.vjp(fn, cb[...], cc[...], ch[...], pu[...], cw[0:1, :], cw[1:2, :], cw[2:3, :], pw[...], ps[...])
        g = vjp((dya[...], dyb[...]))
        dcb[...] = g[0].astype(bf16)
        dcc[...] = g[1].astype(bf16)
        dch[...] = g[2].astype(bf16)
        dpu[...] = g[3].astype(bf16)
        dcw[0:1, :] = g[4]
        dcw[1:2, :] = g[5]
        dcw[2:3, :] = g[6]
        dpw[...] = g[7]
        dps[...] = g[8]

    col = pl.BlockSpec((s, LANE), lambda j: (0, j))
    specs = _convpool_specs(s, l) + [pl.BlockSpec((s, LANE), lambda j: (0, j)),
                                     pl.BlockSpec((s, LANE), lambda j: (0, 4 + j))]
    return pl.pallas_call(
        body, name=name, grid=(4,), in_specs=specs,
        out_specs=[col, col, col, col, pl.BlockSpec((3, LANE), lambda j: (0, j)),
                   pl.BlockSpec((None, LANE, LANE), lambda j: (j, 0, 0)), pl.BlockSpec((1, LANE), lambda j: (0, j))],
        out_shape=[jax.ShapeDtypeStruct((s, G), bf16)] * 4 + [
            jax.ShapeDtypeStruct((3, G), f32), jax.ShapeDtypeStruct((4, LANE, LANE), f32),
            jax.ShapeDtypeStruct((1, G), f32)],
        compiler_params=_cp(("parallel",)),
    )(z, z, z, z, conv_w, pool_w, pool_scale, dy, dy)


def _foxprep_fwd(name, z, bf, l):
    s = z.shape[0]

    def body(m_ref, b_ref, fc_ref, fr_ref):
        fc, fr = _foxprep_fn(m_ref[...], b_ref[...])
        fc_ref[...] = fc
        fr_ref[...] = fr

    return pl.pallas_call(
        body, name=name, grid=(1,),
        in_specs=[pl.BlockSpec((s, LANE), lambda i: (0, MISC)), pl.BlockSpec((None, 1, LANE), lambda i: (l, 0, 0))],
        out_specs=[pl.BlockSpec((s, LANE), lambda i: (0, 0)), pl.BlockSpec((LANE, s), lambda i: (0, 0))],
        out_shape=[jax.ShapeDtypeStruct((s, LANE), f32), jax.ShapeDtypeStruct((LANE, s), f32)],
        compiler_params=_cp(("arbitrary",)),
    )(z, bf)


def _foxprep_bwd(name, z, bf, dfc4, dfr4, dmisc4, l):
    s = z.shape[0]

    def body(m_ref, b_ref, dfc_ref, dfr_ref, dm4_ref, dm_ref, db_ref):
        _, vjp = jax.vjp(_foxprep_fn, m_ref[...], b_ref[...])
        dfc = dfc_ref[0] + dfc_ref[1] + dfc_ref[2] + dfc_ref[3]
        dfr = dfr_ref[0] + dfr_ref[1] + dfr_ref[2] + dfr_ref[3]
        dfr = jnp.concatenate([dfr, jnp.zeros((LANE - 8, s), f32)], axis=0)
        dm, db = vjp((dfc, dfr))
        dm = dm + (dm4_ref[0] + dm4_ref[1] + dm4_ref[2] + dm4_ref[3])
        dm_ref[...] = dm.astype(bf16)
        db_ref[...] = db

    whole = lambda shape: pl.BlockSpec(shape, lambda i: (0,) * len(shape))
    return pl.pallas_call(
        body, name=name, grid=(1,),
        in_specs=[pl.BlockSpec((s, LANE), lambda i: (0, MISC)), pl.BlockSpec((None, 1, LANE), lambda i: (l, 0, 0)),
                  whole((4, s, LANE)), whole((4, 8, s)), whole((4, s, LANE))],
        out_specs=[whole((s, LANE)), whole((1, LANE))],
        out_shape=[jax.ShapeDtypeStruct((s, LANE), bf16), jax.ShapeDtypeStruct((1, LANE), f32)],
        compiler_params=_cp(("arbitrary",)),
    )(z, bf, dfc4, dfr4, dmisc4)


FOX_TQ = 256


def _fox_specs(s, l):
    return [pl.BlockSpec((s, LANE), lambda h: (0, FQ + h)),
            pl.BlockSpec((s, LANE), lambda h: (0, FK + h)),
            pl.BlockSpec((s, LANE), lambda h: (0, FV + h)),
            pl.BlockSpec((s, LANE), lambda h: (0, 0)),
            pl.BlockSpec((8, s), lambda h: (0, 0)),
            pl.BlockSpec((None, 1, LANE), lambda h: (l, 0, 0)),
            pl.BlockSpec((None, 1, LANE), lambda h: (l, 0, 0))]


def _fox_fwd(name, z, fc, fr, qg, kg, l):
    s = z.shape[0]
    tq = min(s, FOX_TQ)

    def body(q, k, v, fc_ref, fr_ref, qg_ref, kg_ref, y_ref):
        h = pl.program_id(0)
        for i in range(s // tq):
            rows, keys = pl.ds(i * tq, tq), pl.ds(0, (i + 1) * tq)
            y = _fox_fn(q[rows, :], k[keys, :], v[keys, :], fc_ref[rows, :], fr_ref[:, keys], qg_ref[...],
                        kg_ref[...], h, i)
            y_ref[rows, :] = y.astype(bf16)

    return pl.pallas_call(
        body, name=name, grid=(4,), in_specs=_fox_specs(s, l), out_specs=pl.BlockSpec((s, LANE), lambda h: (0, h)),
        out_shape=jax.ShapeDtypeStruct((s, G), bf16), compiler_params=_cp(("parallel",)),
    )(z, z, z, fc, fr, qg, kg)


def _fox_bwd(name, z, fc, fr, qg, kg, dy, l):
    s = z.shape[0]
    tq = min(s, FOX_TQ)

    def body(q, k, v, fc_ref, fr_ref, qg_ref, kg_ref, dy_ref, dq, dk, dv, dfc, dfr, dqg, dkg):
        h = pl.program_id(0)
        for ref in (dk, dv, dfr, dqg, dkg):
            ref[...] = jnp.zeros_like(ref)
        for i in range(s // tq):
            rows, keys = pl.ds(i * tq, tq), pl.ds(0, (i + 1) * tq)
            fn = functools.partial(_fox_fn, h=h, i=i)
            _, vjp = jax.vjp(fn, q[rows, :], k[keys, :], v[keys, :], fc_ref[rows, :], fr_ref[:, keys], qg_ref[...],
                             kg_ref[...])
            g = vjp(dy_ref[rows, :])
            dq[rows, :] = g[0].astype(bf16)
            dfc[rows, :] = g[3]
            dk[keys, :] += g[1]
            dv[keys, :] += g[2]
            dfr[:, keys] += g[4]
            dqg[...] += g[5]
            dkg[...] += g[6]

    head = pl.BlockSpec((s, LANE), lambda h: (0, h))
    gain = pl.BlockSpec((None, 1, LANE), lambda h: (h, 0, 0))
    return pl.pallas_call(
        body, name=name, grid=(4,), in_specs=_fox_specs(s, l) + [pl.BlockSpec((s, LANE), lambda h: (0, 12 + h))],
        out_specs=[head, head, head, pl.BlockSpec((None, s, LANE), lambda h: (h, 0, 0)),
                   pl.BlockSpec((None, 8, s), lambda h: (h, 0, 0)), gain, gain],
        out_shape=[jax.ShapeDtypeStruct((s, G), bf16), jax.ShapeDtypeStruct((s, G), f32),
                   jax.ShapeDtypeStruct((s, G), f32), jax.ShapeDtypeStruct((4, s, LANE), f32),
                   jax.ShapeDtypeStruct((4, 8, s), f32), jax.ShapeDtypeStruct((4, 1, LANE), f32),
                   jax.ShapeDtypeStruct((4, 1, LANE), f32)],
        compiler_params=_cp(("parallel",)),
    )(z, z, z, fc, fr, qg, kg, dy)


def _gla_ts(s):
    return min(s, 512)


def _gla1_specs(s, ts, l):
    return [pl.BlockSpec((ts, LANE), lambda h, i: (i, GQ + h)),
            pl.BlockSpec((ts, LANE), lambda h, i: (i, GK + h)),
            pl.BlockSpec((ts, LANE), lambda h, i: (i, GV + h)),
            pl.BlockSpec((ts, LANE), lambda h, i: (i, MISC)),
            pl.BlockSpec((None, LANE, LANE), lambda h, i: (l, 0, h)),
            pl.BlockSpec((None, 1, LANE), lambda h, i: (l, 0, h))]


def _gla1_fwd(name, z, wd, bd, l):
    s = z.shape[0]
    ts = _gla_ts(s)
    nb = ts // CHUNK

    def body(q, k, v, m, wd_ref, bd_ref, qe_ref, kd_ref, el_ref, oi_ref):
        qe, kd, el, oi = _gla1_fn(q[...], k[...], v[...], m[...], wd_ref[...], bd_ref[...])
        qe_ref[...] = qe.astype(bf16)
        kd_ref[...] = kd.astype(bf16)
        el_ref[...] = el
        oi_ref[...] = oi

    blk = pl.BlockSpec((ts, LANE), lambda h, i: (i, h))
    return pl.pallas_call(
        body, name=name, grid=(4, s // ts), in_specs=_gla1_specs(s, ts, l),
        out_specs=[blk, blk, pl.BlockSpec((nb, LANE), lambda h, i: (i, h)), blk],
        out_shape=[jax.ShapeDtypeStruct((s, G), bf16), jax.ShapeDtypeStruct((s, G), bf16),
                   jax.ShapeDtypeStruct((s // CHUNK, G), f32), jax.ShapeDtypeStruct((s, G), f32)],
        compiler_params=_cp(("parallel", "parallel")),
    )(z, z, z, z, wd, bd)


def _gla1_bwd(name, z, wd, bd, dqe, dkd, del_, do, dvi, l):
    s = z.shape[0]
    ts = _gla_ts(s)
    nb = ts // CHUNK

    def body(q, k, v, m, wd_ref, bd_ref, dqe_ref, dkd_ref, del_ref, do_ref, dvi_ref, dq, dk, dv, dm, dwd, dbd):
        i = pl.program_id(1)
        _, vjp = jax.vjp(_gla1_fn, q[...], k[...], v[...], m[...], wd_ref[...], bd_ref[...])
        g = vjp((dqe_ref[...], dkd_ref[...], del_ref[...], do_ref[...]))
        dq[...] = g[0].astype(bf16)
        dk[...] = g[1].astype(bf16)
        dv[...] = (g[2] + dvi_ref[...]).astype(bf16)
        dm[...] = g[3]

        @pl.when(i == 0)
        def _():
            dwd[...] = g[4]
            dbd[...] = g[5]

        @pl.when(i > 0)
        def _():
            dwd[...] += g[4]
            dbd[...] += g[5]

    blk = pl.BlockSpec((ts, LANE), lambda h, i: (i, h))
    specs = _gla1_specs(s, ts, l) + [blk, blk, pl.BlockSpec((nb, LANE), lambda h, i: (i, h)), blk, blk]
    return pl.pallas_call(
        body, name=name, grid=(4, s // ts), in_specs=specs,
        out_specs=[blk, blk, blk, pl.BlockSpec((None, ts, LANE), lambda h, i: (h, i, 0)),
                   pl.BlockSpec((None, LANE, LANE), lambda h, i: (h, 0, 0)),
                   pl.BlockSpec((None, 1, LANE), lambda h, i: (h, 0, 0))],
        out_shape=[jax.ShapeDtypeStruct((s, G), bf16)] * 3 + [
            jax.ShapeDtypeStruct((4, s, LANE), f32), jax.ShapeDtypeStruct((4, LANE, LANE), f32),
            jax.ShapeDtypeStruct((4, 1, LANE), f32)],
        compiler_params=_cp(("parallel", "arbitrary")),
    )(z, z, z, z, wd, bd, dqe, dkd, del_, do, dvi)


def _gla2_fwd(name, z, qe, kd, el, oi):
    s = z.shape[0]
    n = s // CHUNK

    def body(v_ref, qe_ref, kd_ref, el_ref, oi_ref, o_ref, st_ref, cur):
        cur[...] = jnp.zeros_like(cur)

        def step(c, carry):
            rows = pl.ds(pl.multiple_of(c * CHUNK, CHUNK), CHUNK)
            st = cur[...]
            st_ref[c] = st
            o_ref[rows, :] = oi_ref[rows, :] + _dot(qe_ref[rows, :], st, ((1,), (1,)))
            cur[...] = st * el_ref[pl.ds(c, 1), :] + _dot(v_ref[rows, :], kd_ref[rows, :], ((0,), (0,)))
            return carry

        lax.fori_loop(0, n, step, 0)

    head = pl.BlockSpec((s, LANE), lambda h: (0, h))
    return pl.pallas_call(
        body, name=name, grid=(4,),
        in_specs=[pl.BlockSpec((s, LANE), lambda h: (0, GV + h)), head, head,
                  pl.BlockSpec((n, LANE), lambda h: (0, h)), head],
        out_specs=[head, pl.BlockSpec((None, n, LANE, LANE), lambda h: (h, 0, 0, 0))],
        out_shape=[jax.ShapeDtypeStruct((s, G), f32), jax.ShapeDtypeStruct((4, n, LANE, LANE), f32)],
        scratch_shapes=[pltpu.VMEM((LANE, LANE), f32)],
        compiler_params=_cp(("parallel",)),
    )(z, qe, kd, el, oi)


def _gla2_bwd(name, z, qe, kd, el, st, do):
    s = z.shape[0]
    n = s // CHUNK

    def body(v_ref, qe_ref, kd_ref, el_ref, st_ref, do_ref, dqe_ref, dkd_ref, dv_ref, del_ref, dcur):
        dcur[...] = jnp.zeros_like(dcur)

        def step(t, carry):
            c = n - 1 - t
            rows = pl.ds(pl.multiple_of(c * CHUNK, CHUNK), CHUNK)
            dn = dcur[...]
            stc = st_ref[c]
            doc = do_ref[rows, :]
            dqe_ref[rows, :] = _dot(doc, stc)
            dv_ref[rows, :] = _dot(kd_ref[rows, :], dn, ((1,), (1,)))
            dkd_ref[rows, :] = _dot(v_ref[rows, :], dn)
            del_ref[pl.ds(c, 1), :] = jnp.sum(stc * dn, axis=0, keepdims=True)
            dcur[...] = dn * el_ref[pl.ds(c, 1), :] + _dot(doc, qe_ref[rows, :], ((0,), (0,)))
            return carry

        lax.fori_loop(0, n, step, 0)

    head = pl.BlockSpec((s, LANE), lambda h: (0, h))
    chunk = pl.BlockSpec((n, LANE), lambda h: (0, h))
    return pl.pallas_call(
        body, name=name, grid=(4,),
        in_specs=[pl.BlockSpec((s, LANE), lambda h: (0, GV + h)), head, head, chunk,
                  pl.BlockSpec((None, n, LANE, LANE), lambda h: (h, 0, 0, 0)), head],
        out_specs=[head, head, head, chunk],
        out_shape=[jax.ShapeDtypeStruct((s, G), f32)] * 3 + [jax.ShapeDtypeStruct((n, G), f32)],
        scratch_shapes=[pltpu.VMEM((LANE, LANE), f32)],
        compiler_params=_cp(("parallel",)),
    )(z, qe, kd, el, st, do)


def _gla3_specs(ts, l):
    return [pl.BlockSpec((ts, LANE), lambda h, i: (i, h)),
            pl.BlockSpec((ts, LANE), lambda h, i: (i, GG + h)),
            pl.BlockSpec((None, 1, LANE), lambda h, i: (l, 0, 0))]


def _gla3_fwd(name, o, z, og, l):
    s = z.shape[0]
    ts = _gla_ts(s)

    def body(o_ref, g_ref, og_ref, y_ref):
        y_ref[...] = _gla3_fn(o_ref[...], g_ref[...], og_ref[...]).astype(bf16)

    return pl.pallas_call(
        body, name=name, grid=(4, s // ts), in_specs=_gla3_specs(ts, l),
        out_specs=pl.BlockSpec((ts, LANE), lambda h, i: (i, h)),
        out_shape=jax.ShapeDtypeStruct((s, G), bf16), compiler_params=_cp(("parallel", "parallel")),
    )(o, z, og)


def _gla3_bwd(name, o, z, og, dy, l):
    s = z.shape[0]
    ts = _gla_ts(s)

    def body(o_ref, g_ref, og_ref, dy_ref, do_ref, dg_ref, dog_ref):
        i = pl.program_id(1)
        _, vjp = jax.vjp(_gla3_fn, o_ref[...], g_ref[...], og_ref[...])
        g = vjp(dy_ref[...])
        do_ref[...] = g[0]
        dg_ref[...] = g[1].astype(bf16)

        @pl.when(i == 0)
        def _():
            dog_ref[...] = g[2]

        @pl.when(i > 0)
        def _():
            dog_ref[...] += g[2]

    blk = pl.BlockSpec((ts, LANE), lambda h, i: (i, h))
    return pl.pallas_call(
        body, name=name, grid=(4, s // ts),
        in_specs=_gla3_specs(ts, l) + [pl.BlockSpec((ts, LANE), lambda h, i: (i, 8 + h))],
        out_specs=[blk, blk, pl.BlockSpec((None, 1, LANE), lambda h, i: (h, 0, 0))],
        out_shape=[jax.ShapeDtypeStruct((s, G), f32), jax.ShapeDtypeStruct((s, G), bf16),
                   jax.ShapeDtypeStruct((4, 1, LANE), f32)],
        compiler_params=_cp(("parallel", "arbitrary")),
    )(o, z, og, dy)


def _layer_fwd(l, h, p, weights):
    s = h.shape[0]
    tb = _tm_big(s)
    n = f"l{l}_"
    w_in = weights(l, "w_in", h)
    u = _rmsnorm_fwd(n + "norm_mix", h, p["g_mix"], l)
    z = _mm(n + "mm_in", u, w_in,
            pl.BlockSpec((tb, D), lambda j, i, k: (i, 0)), pl.BlockSpec((D, 1152), lambda j, i, k: (0, j)),
            jax.ShapeDtypeStruct((s, ZC), f32), pl.BlockSpec((tb, 1152), lambda j, i, k: (i, j)),
            (ZC // 1152, s // tb, 1), ((1,), (0,)), 1)
    w_out = weights(l, "w_out", z)
    ya, yb = _convpool_fwd(n + "convpool", z, p["conv_w"], p["pool_w"], p["pool_scale"], l)
    qe, kd, el, oi = _gla1_fwd(n + "gla_chunk", z, p["wdec"], p["bdec"], l)
    o, st = _gla2_fwd(n + "gla_scan", z, qe, kd, el, oi)
    yc = _gla3_fwd(n + "gla_out", o, z, p["gla_og"], l)
    fc, fr = _foxprep_fwd(n + "fox_prep", z, p["fox_bf"], l)
    yd = _fox_fwd(n + "fox_attn", z, fc, fr, p["fox_qg"], p["fox_kg"], l)
    y = jnp.concatenate([ya, yb, yc, yd], axis=1)
    w_gate, w_up, w_down = weights(l, "ffn", y)
    res_tile = lambda: pl.BlockSpec((tb, 1024), lambda j, i, k: (i, j))
    h1 = _mm(n + "mm_out", y, w_out.reshape(D, D),
             pl.BlockSpec((tb, D), lambda j, i, k: (i, 0)), pl.BlockSpec((D, 1024), lambda j, i, k: (0, j)),
             jax.ShapeDtypeStruct((s, D), f32), res_tile(), (2, s // tb, 1), ((1,), (0,)), 1, res=h, res_spec=res_tile())
    u2 = _rmsnorm_fwd(n + "norm_ffn", h1, p["g_ffn"], l)
    gate, up, act = _mm_gate_up(n + "mm_gate_up", u2, w_gate, w_up)
    h2 = _mm(n + "mm_down", act, w_down,
             pl.BlockSpec((tb, FB), lambda j, i, k: (i, k)),
             pl.BlockSpec((None, FB, 1024), lambda j, i, k: (k, 0, j)),
             jax.ShapeDtypeStruct((s, D), f32), res_tile(), (2, s // tb, NCHIP), ((1,), (0,)), NCHIP,
             res=h1, res_spec=res_tile())
    saved = dict(h=h, u=u, z=z, qe=qe, kd=kd, el=el, st=st, o=o, fc=fc, fr=fr, y=y, h1=h1, u2=u2,
                 gate=gate, up=up, act=act, w_in=w_in, w_out=w_out, w_gate=w_gate, w_up=w_up, w_down=w_down)
    return h2, saved


def _mm_tn(name, a, b, ta, tb, out_shape, out_spec, grid):
    s = a.shape[0]
    return _mm(name, a, b, pl.BlockSpec((s, ta), lambda i, j, k: (0, i)), pl.BlockSpec((s, tb), lambda i, j, k: (0, j)),
               out_shape, out_spec, grid, ((0,), (0,)), 1)


def _layer_bwd_ffn(l, dh2, dh2b, p, sv, ship):
    s = dh2.shape[0]
    tb = _tm_big(s)
    n = f"l{l}_bwd_"
    g_wd = _mm_tn(n + "mm_dwd", sv["act"], dh2b, FB, 1024, jax.ShapeDtypeStruct((NCHIP, FB, D), bf16),
                  pl.BlockSpec((None, FB, 1024), lambda i, j, k: (i, 0, j)), (NCHIP, 2, 1))
    dgate, dup = _mm_dact(n + "mm_dact", dh2b, sv["w_down"], sv["gate"], sv["up"])
    wg_shape = jax.ShapeDtypeStruct((NCHIP, D, FB), bf16)
    wg_spec = lambda: pl.BlockSpec((None, 1024, FB), lambda i, j, k: (j, i, 0))
    g_wg = _mm_tn(n + "mm_dwg", sv["u2"], dgate, 1024, FB, wg_shape, wg_spec(), (2, NCHIP, 1))
    g_wu = _mm_tn(n + "mm_dwu", sv["u2"], dup, 1024, FB, wg_shape, wg_spec(), (2, NCHIP, 1))
    token = ship.early(dict(w_gate=g_wg, w_up=g_wu, w_down=g_wd))
    nt_in = lambda: (pl.BlockSpec((tb, FB), lambda j, i, k: (i, k)),
                     pl.BlockSpec((None, 1024, FB), lambda j, i, k: (k, j, 0)))
    nt_out = lambda: (jax.ShapeDtypeStruct((s, D), f32), pl.BlockSpec((tb, 1024), lambda j, i, k: (i, j)))
    du2 = _mm(n + "mm_du2_gate", dgate, sv["w_gate"], *nt_in(), *nt_out(), (2, s // tb, NCHIP), ((1,), (1,)), NCHIP,
              dep=token)
    du2 = _mm(n + "mm_du2_up", dup, sv["w_up"], *nt_in(), *nt_out(), (2, s // tb, NCHIP), ((1,), (1,)), NCHIP,
              res=du2, res_spec=pl.BlockSpec((tb, 1024), lambda j, i, k: (i, j)))
    dh1, dh1b, dg_ffn = _rmsnorm_bwd(n + "norm_ffn", sv["h1"], p["g_ffn"], du2, dh2, l)
    g_wo = _mm_tn(n + "mm_dwo", sv["y"], dh1b, G, 1024, jax.ShapeDtypeStruct((NCHIP, G, D), bf16),
                  pl.BlockSpec((None, G, 1024), lambda i, j, k: (i, 0, j)), (NCHIP, 2, 1))
    token = ship.finish(dict(w_out=g_wo), g_wo)
    dy = _mm(n + "mm_dy", dh1b, sv["w_out"].reshape(D, D),
             pl.BlockSpec((tb, D), lambda j, i, k: (i, 0)), pl.BlockSpec((1024, D), lambda j, i, k: (j, 0)),
             jax.ShapeDtypeStruct((s, D), f32), pl.BlockSpec((tb, 1024), lambda j, i, k: (i, j)),
             (2, s // tb, 1), ((1,), (1,)), 1, dep=token)
    return dh1, dy, dg_ffn


def _layer_bwd_mix(l, dh1, dy, p, sv, ship):
    s = dh1.shape[0]
    n = f"l{l}_bwd_"
    z = sv["z"]
    dcb, dcc, dch, dpu, dconv, dpoolw, dpools = _convpool_bwd(
        n + "convpool", z, p["conv_w"], p["pool_w"], p["pool_scale"], dy, l)
    do, dgg, dog = _gla3_bwd(n + "gla_out", sv["o"], z, p["gla_og"], dy, l)
    dqe, dkd, dvi, del_ = _gla2_bwd(n + "gla_scan", z, sv["qe"], sv["kd"], sv["el"], sv["st"], do)
    dgq, dgk, dgv, dmisc4, dwd, dbd = _gla1_bwd(n + "gla_chunk", z, p["wdec"], p["bdec"], dqe, dkd, del_, do, dvi, l)
    dfq, dfk, dfv, dfc4, dfr4, dqg, dkg = _fox_bwd(n + "fox_attn", z, sv["fc"], sv["fr"], p["fox_qg"], p["fox_kg"], dy, l)
    dmisc, dbf = _foxprep_bwd(n + "fox_prep", z, p["fox_bf"], dfc4, dfr4, dmisc4, l)
    dz = jnp.concatenate([dcb, dcc, dch, dpu, dgq, dgk, dgv, dgg, dfq, dfk.astype(bf16), dfv.astype(bf16), dmisc],
                         axis=1)
    g_wi = _mm_tn(n + "mm_dwi", sv["u"], dz, 1024, 1152, jax.ShapeDtypeStruct((D, ZC), bf16),
                  pl.BlockSpec((1024, 1152), lambda i, j, k: (i, j)), (2, ZC // 1152, 1))
    token = ship.early(dict(w_in=_win_to_blocks(g_wi)))
    tb = _tm_big(s)
    du = _mm(n + "mm_du", dz, sv["w_in"],
             pl.BlockSpec((tb, 1920), lambda j, i, k: (i, k)), pl.BlockSpec((1024, 1920), lambda j, i, k: (j, k)),
             jax.ShapeDtypeStruct((s, D), f32), pl.BlockSpec((tb, 1024), lambda j, i, k: (i, j)),
             (2, s // tb, ZC // 1920), ((1,), (1,)), ZC // 1920, dep=token)
    token = ship.finish({}, du)
    dh, dhb, dg_mix = _rmsnorm_bwd(n + "norm_mix", sv["h"], p["g_mix"], du, dh1, l, dep=token)
    small = dict(
        norm_mix_g=dg_mix[0], conv_w=dconv, pool_w=dpoolw, pool_scale=dpools[0],
        gla_w_decay=jnp.concatenate([dwd[hh, GA_LANE:GA_LANE + 16, :64] for hh in range(4)], axis=1),
        gla_b_decay=jnp.concatenate([dbd[hh, 0, :64] for hh in range(4)]),
        gla_out_g=jnp.sum(dog[:, 0, :], axis=0), fox_q_g=jnp.sum(dqg[:, 0, :], axis=0),
        fox_k_g=jnp.sum(dkg[:, 0, :], axis=0), fox_b_f=dbf[0, FF_LANE:FF_LANE + 4])
    return dh, dhb, small


def _win_from_blocks(wb):
    def cols(a, b):
        parts = []
        for kk in range(NCHIP):
            lo, hi = max(a, kk * WINB), min(b, (kk + 1) * WINB)
            if lo < hi:
                parts.append(wb[kk, :, lo - kk * WINB:hi - kk * WINB])
        return parts

    zeros = lambda w: [jnp.zeros((wb.shape[1], w), wb.dtype)]
    segs = cols(0, 2048)
    for hh in range(4):
        segs += cols(2048 + 64 * hh, 2112 + 64 * hh) + zeros(64)
    for hh in range(4):
        segs += cols(2304 + 64 * hh, 2368 + 64 * hh) + zeros(64)
    segs += cols(2560, 3584) + cols(3600, 5136)
    segs += cols(5136, 5140) + zeros(GA_LANE - 4) + cols(3584, 3600) + zeros(LANE - GA_LANE - 16)
    return jnp.concatenate(segs, axis=-1)


def _win_to_blocks(g):
    mb = MISC * LANE
    segs = [(0, 2048)] + [(GQ * LANE + LANE * hh, 64) for hh in range(4)] + [(GK * LANE + LANE * hh, 64) for hh in range(4)]
    segs += [(GV * LANE, 1024), (mb + GA_LANE, 16), (FQ * LANE, 1536), (mb + FF_LANE, 4)]
    blocks, at = [[] for _ in range(NCHIP)], 0
    for start, width in segs:
        while width > 0:
            take = min(width, (at // WINB + 1) * WINB - at)
            blocks[at // WINB].append(g[:, start:start + take])
            start, width, at = start + take, width - take, at + take
    return jnp.stack([jnp.concatenate(b, axis=1) for b in blocks])


def _place():
    x, y, c = lax.axis_index("x"), lax.axis_index("y"), lax.axis_index("c")
    chips = [(1 - x, y), (x, 1 - y), (1 - x, 1 - y)]
    return x, y, c, chips


def _allgather_small(name, v, dep):
    m_per, n = v.shape

    def body(x_ref, dep_ref, out_ref, send_sems, recv_sems, local_sem):
        x, y, c, chips = _place()
        me, sibling = (x, y, c), (x, y, 1 - c)

        def rows(px, py, pc):
            return out_ref.at[pl.ds((4 * px + 2 * py + pc) * m_per, m_per), :]

        def copy(k, block, to, src=None):
            return pltpu.make_async_remote_copy(
                src_ref=rows(*block) if src is None else src, dst_ref=rows(*block),
                send_sem=send_sems.at[k], recv_sem=recv_sems.at[k], device_id=to, device_id_type=MESH)

        mine = pltpu.make_async_copy(x_ref, rows(*me), local_sem)
        mine.start()
        first = [copy(0, me, sibling, src=x_ref)]
        first += [copy(1 + j, me, (*chip, c), src=x_ref) for j, chip in enumerate(chips)]
        for cp in first:
            cp.start()
        passed = [copy(4 + j, (*chip, c), sibling) for j, chip in enumerate(chips)]
        for j, chip in enumerate(chips):
            copy(1 + j, (*chip, c), me).wait_recv()
            passed[j].start()
        copy(0, sibling, me).wait_recv()
        for j, chip in enumerate(chips):
            copy(4 + j, (*chip, 1 - c), me).wait_recv()
        for cp in first + passed:
            cp.wait_send()
        mine.wait()

    return pl.pallas_call(
        body, name=name, out_shape=jax.ShapeDtypeStruct((8 * m_per, n), v.dtype),
        in_specs=[pl.BlockSpec(memory_space=pltpu.VMEM), pl.BlockSpec(memory_space=pl.ANY)],
        out_specs=pl.BlockSpec(memory_space=pltpu.VMEM),
        scratch_shapes=[pltpu.SemaphoreType.DMA((7,)), pltpu.SemaphoreType.DMA((7,)), pltpu.SemaphoreType.DMA],
    )(v, dep)


def _hbm_specs(n):
    return [pl.BlockSpec(memory_space=pl.ANY)] * n


def _own_slot(shard, chip):
    return lax.dynamic_update_index_in_dim(lax.empty((NCHIP,) + shard.shape, shard.dtype), shard, chip, 0)


HBM = pl.BlockSpec(memory_space=pltpu.HBM)
SEM = pl.BlockSpec(memory_space=pltpu.SEMAPHORE)
EFFECT = pltpu.SideEffectType.DATAFLOW_SIDE_EFFECTING


def _in_hbm(v):
    return pltpu.with_memory_space_constraint(v, pltpu.HBM)


def _gather_start(name, groups):
    flat = [b for g in groups for b in g]
    nt, ng = len(flat), len(groups)

    def body(*refs):
        outs = refs[nt:]
        sems, bufs = outs[:2 * ng], outs[2 * ng:2 * ng + nt]
        x, y, c, chips = _place()
        me = 2 * x + y
        t = 0
        for gi, g in enumerate(groups):
            for k in range(len(g)):
                r = bufs[t].shape[1] // 2
                mine = bufs[t].at[me, pl.ds(c * r, r), :]
                for j, (px, py) in enumerate(chips):
                    pltpu.make_async_remote_copy(
                        src_ref=mine, dst_ref=mine, send_sem=sems[2 * gi].at[3 * k + j], recv_sem=sems[2 * gi + 1].at[3 * k + j],
                        device_id=(px, py, c), device_id_type=MESH).start()
                t += 1

    sem_shapes = []
    for g in groups:
        sem_shapes += [pltpu.SemaphoreType.DMA((3 * len(g),))] * 2
    out = pl.pallas_call(
        body, name=name,
        out_shape=tuple(sem_shapes + [pltpu.HBM(b.shape, b.dtype) for b in flat]),
        in_specs=tuple([HBM] * nt), out_specs=tuple([SEM] * (2 * ng) + [HBM] * nt),
        input_output_aliases={t: 2 * ng + t for t in range(nt)},
        compiler_params=pltpu.CompilerParams(has_side_effects=EFFECT),
    )(*[_in_hbm(b) for b in flat])
    sems = [(out[2 * gi], out[2 * gi + 1]) for gi in range(ng)]
    bufs, at = [], 2 * ng
    for g in groups:
        bufs.append(list(out[at:at + len(g)]))
        at += len(g)
    return sems, bufs


def _gather_wait(name, bufs, send_sems, recv_sems, after):
    nt = len(bufs)

    def body(*refs):
        ins, ss, rs = refs[:nt], refs[nt], refs[nt + 1]
        x, y, c, chips = _place()
        me = 2 * x + y
        for k in range(nt):
            r = ins[k].shape[1] // 2
            for j, (px, py) in enumerate(chips):
                cp = pltpu.make_async_remote_copy(
                    src_ref=ins[k].at[me, pl.ds(c * r, r), :], dst_ref=ins[k].at[2 * px + py, pl.ds(c * r, r), :],
                    send_sem=ss.at[3 * k + j], recv_sem=rs.at[3 * k + j], device_id=(px, py, c), device_id_type=MESH)
                cp.wait_send()
                cp.wait_recv()

    out = pl.pallas_call(
        body, name=name, out_shape=tuple(pltpu.HBM(b.shape, b.dtype) for b in bufs),
        in_specs=tuple([HBM] * nt + [SEM, SEM, pl.BlockSpec(memory_space=pl.ANY)]), out_specs=tuple([HBM] * nt),
        input_output_aliases={t: t for t in range(nt)},
        compiler_params=pltpu.CompilerParams(has_side_effects=EFFECT),
    )(*bufs, send_sems, recv_sems, after)
    return list(out)


def _gather_exchange(name, bufs):
    nt = len(bufs)

    def body(*refs):
        outs = refs[nt:2 * nt]
        send_sems, recv_sems = refs[2 * nt:]
        x, y, c, chips = _place()
        sibling = (x, y, 1 - c)

        def half(t, chip_idx, cc):
            r = outs[t].shape[1] // 2
            return outs[t].at[chip_idx, pl.ds(cc * r, r), :]

        sent = []
        for t in range(nt):
            for j, (px, py) in enumerate(chips):
                cp = pltpu.make_async_remote_copy(
                    src_ref=half(t, 2 * px + py, c), dst_ref=half(t, 2 * px + py, c),
                    send_sem=send_sems.at[t, j], recv_sem=recv_sems.at[t, j], device_id=sibling, device_id_type=MESH)
                cp.start()
                sent.append(cp)
        for t in range(nt):
            for j, (px, py) in enumerate(chips):
                pltpu.make_async_remote_copy(
                    src_ref=half(t, 2 * px + py, 1 - c), dst_ref=half(t, 2 * px + py, 1 - c),
                    send_sem=send_sems.at[t, j], recv_sem=recv_sems.at[t, j], device_id=sibling,
                    device_id_type=MESH).wait_recv()
        for cp in sent:
            cp.wait_send()

    return pl.pallas_call(
        body, name=name, out_shape=[jax.ShapeDtypeStruct(v.shape, v.dtype) for v in bufs],
        in_specs=_hbm_specs(nt), out_specs=_hbm_specs(nt), input_output_aliases={t: t for t in range(nt)},
        scratch_shapes=[pltpu.SemaphoreType.DMA((nt, 3)), pltpu.SemaphoreType.DMA((nt, 3))],
    )(*bufs)


def _rs_to_sibling(name, grads):
    nt = len(grads)

    def body(*refs):
        ins, outs = refs[:nt], refs[nt:2 * nt]
        send_sems, recv_sems = refs[2 * nt:]
        x, y, c, _ = _place()
        cps = []
        for t in range(nt):
            r = ins[t].shape[1] // 2
            cp = pltpu.make_async_remote_copy(
                src_ref=ins[t].at[:, pl.ds((1 - c) * r, r), :], dst_ref=outs[t],
                send_sem=send_sems.at[t], recv_sem=recv_sems.at[t], device_id=(x, y, 1 - c), device_id_type=MESH)
            cp.start()
            cps.append(cp)
        for cp in cps:
            cp.wait()

    return pl.pallas_call(
        body, name=name,
        out_shape=[jax.ShapeDtypeStruct((NCHIP, g.shape[1] // 2, g.shape[2]), g.dtype) for g in grads],
        in_specs=_hbm_specs(nt), out_specs=_hbm_specs(nt),
        scratch_shapes=[pltpu.SemaphoreType.DMA((nt,)), pltpu.SemaphoreType.DMA((nt,))],
    )(*grads)


def _sibling_start(name, grads):
    nt = len(grads)

    def body(*refs):
        outs = refs[2 * nt:]
        ss, rs, src, land, token = outs[0], outs[1], outs[2:2 + nt], outs[2 + nt:2 + 2 * nt], outs[2 + 2 * nt]
        x, y, c, _ = _place()
        for t in range(nt):
            r = src[t].shape[1] // 2
            pltpu.make_async_remote_copy(
                src_ref=src[t].at[:, pl.ds((1 - c) * r, r), :], dst_ref=land[t], send_sem=ss.at[t], recv_sem=rs.at[t],
                device_id=(x, y, 1 - c), device_id_type=MESH).start()
        token[...] = jnp.zeros_like(token)

    src_shapes = [pltpu.HBM(g.shape, g.dtype) for g in grads]
    land_shapes = [pltpu.HBM((NCHIP, g.shape[1] // 2, g.shape[2]), g.dtype) for g in grads]
    out = pl.pallas_call(
        body, name=name,
        out_shape=tuple([pltpu.SemaphoreType.DMA((nt,))] * 2 + src_shapes + land_shapes
                        + [jax.ShapeDtypeStruct((8, LANE), f32)]),
        in_specs=tuple([HBM] * (2 * nt)),
        out_specs=tuple([SEM, SEM] + [HBM] * (2 * nt) + [pl.BlockSpec(memory_space=pltpu.VMEM)]),
        input_output_aliases={t: 2 + t for t in range(2 * nt)},
        compiler_params=pltpu.CompilerParams(has_side_effects=EFFECT),
    )(*[_in_hbm(g) for g in grads], *[_in_hbm(lax.empty(s_.shape, s_.dtype)) for s_ in land_shapes])
    return out[0], out[1], list(out[2:2 + nt]), list(out[2 + nt:2 + 2 * nt]), out[2 + 2 * nt]


def _sibling_wait(name, grads, land, send_sems, recv_sems, after):
    nt = len(grads)

    def body(*refs):
        src, dst, ss, rs = refs[:nt], refs[nt:2 * nt], refs[2 * nt], refs[2 * nt + 1]
        x, y, c, _ = _place()
        for t in range(nt):
            r = src[t].shape[1] // 2
            cp = pltpu.make_async_remote_copy(
                src_ref=src[t].at[:, pl.ds((1 - c) * r, r), :], dst_ref=dst[t], send_sem=ss.at[t], recv_sem=rs.at[t],
                device_id=(x, y, 1 - c), device_id_type=MESH)
            cp.wait_send()
            cp.wait_recv()

    shapes = [pltpu.HBM(v.shape, v.dtype) for v in list(grads) + list(land)]
    out = pl.pallas_call(
        body, name=name, out_shape=tuple(shapes),
        in_specs=tuple([HBM] * (2 * nt) + [SEM, SEM, pl.BlockSpec(memory_space=pl.ANY)]),
        out_specs=tuple([HBM] * (2 * nt)), input_output_aliases={t: t for t in range(2 * nt)},
        compiler_params=pltpu.CompilerParams(has_side_effects=EFFECT),
    )(*grads, *land, send_sems, recv_sems, after)
    return list(out[:nt]), list(out[nt:])


def _rs_pair_sum(name, pos, g, other):
    r, cdim = other.shape[1], other.shape[2]
    tr = r // 4 if (r // 4) % 16 == 0 else r // 2
    nblk = r // tr

    def body(pos_ref, g_ref, o_ref, s_ref):
        s_ref[...] = (g_ref[...].astype(f32) + o_ref[...].astype(f32)).astype(bf16)

    blk = pl.BlockSpec((None, tr, cdim), lambda q, i, p: (q, i, 0))
    return pl.pallas_call(
        body, name=name, out_shape=jax.ShapeDtypeStruct(other.shape, bf16),
        grid_spec=pltpu.PrefetchScalarGridSpec(
            num_scalar_prefetch=1, grid=(NCHIP, nblk),
            in_specs=[pl.BlockSpec((None, tr, cdim), lambda q, i, p: (q, p[1] * nblk + i, 0)), blk], out_specs=blk),
        compiler_params=_cp(("parallel", "parallel")),
    )(pos, g, other)


def _scatter_start(name, sums):
    nt = len(sums)

    def body(*refs):
        outs = refs[2 * nt:]
        ss, rs, src, land, token = outs[0], outs[1], outs[2:2 + nt], outs[2 + nt:2 + 2 * nt], outs[2 + 2 * nt]
        x, y, c, chips = _place()
        me = 2 * x + y
        for t in range(nt):
            for j, (px, py) in enumerate(chips):
                pltpu.make_async_remote_copy(
                    src_ref=src[t].at[2 * px + py], dst_ref=land[t].at[me], send_sem=ss.at[3 * t + j], recv_sem=rs.at[3 * t + j],
                    device_id=(px, py, c), device_id_type=MESH).start()
        token[...] = jnp.zeros_like(token)

    shapes = [pltpu.HBM(v.shape, v.dtype) for v in sums]
    out = pl.pallas_call(
        body, name=name,
        out_shape=tuple([pltpu.SemaphoreType.DMA((3 * nt,))] * 2 + shapes + shapes + [jax.ShapeDtypeStruct((8, LANE), f32)]),
        in_specs=tuple([HBM] * (2 * nt)),
        out_specs=tuple([SEM, SEM] + [HBM] * (2 * nt) + [pl.BlockSpec(memory_space=pltpu.VMEM)]),
        input_output_aliases={t: 2 + t for t in range(2 * nt)},
        compiler_params=pltpu.CompilerParams(has_side_effects=EFFECT),
    )(*[_in_hbm(v) for v in sums], *[_in_hbm(lax.empty(v.shape, v.dtype)) for v in sums])
    return out[0], out[1], list(out[2:2 + nt]), list(out[2 + nt:2 + 2 * nt]), out[2 + 2 * nt]


def _scatter_wait(name, sums, land, send_sems, recv_sems, after):
    nt = len(sums)

    def body(*refs):
        src, dst, ss, rs = refs[:nt], refs[nt:2 * nt], refs[2 * nt], refs[2 * nt + 1]
        x, y, c, chips = _place()
        for t in range(nt):
            for j, (px, py) in enumerate(chips):
                cp = pltpu.make_async_remote_copy(
                    src_ref=src[t].at[2 * px + py], dst_ref=dst[t].at[2 * px + py], send_sem=ss.at[3 * t + j],
                    recv_sem=rs.at[3 * t + j], device_id=(px, py, c), device_id_type=MESH)
                cp.wait_send()
                cp.wait_recv()

    shapes = [pltpu.HBM(v.shape, v.dtype) for v in sums]
    out = pl.pallas_call(
        body, name=name, out_shape=tuple(shapes + shapes),
        in_specs=tuple([HBM] * (2 * nt) + [SEM, SEM, pl.BlockSpec(memory_space=pl.ANY)]),
        out_specs=tuple([HBM] * (2 * nt)), input_output_aliases={t: t for t in range(2 * nt)},
        compiler_params=pltpu.CompilerParams(has_side_effects=EFFECT),
    )(*sums, *land, send_sems, recv_sems, after)
    return list(out[:nt]), list(out[nt:])


def _rs_chip_sum(name, pos, sums, parts):
    r, cdim = parts.shape[1], parts.shape[2]
    tr = r // 4 if (r // 4) % 16 == 0 else r // 2
    nblk = r // tr

    def body(pos_ref, own_ref, a_ref, b_ref, c_ref, o_ref):
        o_ref[...] = ((own_ref[...].astype(f32) + a_ref[...].astype(f32)) + b_ref[...].astype(f32)) \
            + c_ref[...].astype(f32)

    def slot(k):
        return pl.BlockSpec((None, tr, cdim), lambda i, p: ((p[0] + k) % NCHIP, i, 0))

    return pl.pallas_call(
        body, name=name, out_shape=jax.ShapeDtypeStruct((2 * r, cdim), f32),
        grid_spec=pltpu.PrefetchScalarGridSpec(
            num_scalar_prefetch=1, grid=(nblk,), in_specs=[slot(0), slot(1), slot(2), slot(3)],
            out_specs=pl.BlockSpec((tr, cdim), lambda i, p: (p[1] * nblk + i, 0))),
        compiler_params=_cp(("parallel",)),
    )(pos, sums, parts, parts, parts)


def _rs_share_halves(name, bufs):
    nt = len(bufs)

    def body(*refs):
        outs = refs[nt:2 * nt]
        send_sems, recv_sems = refs[2 * nt:]
        x, y, c, _ = _place()
        cps = []
        for t in range(nt):
            r = outs[t].shape[0] // 2
            mine = outs[t].at[pl.ds(c * r, r), :]
            theirs = outs[t].at[pl.ds((1 - c) * r, r), :]
            cp = pltpu.make_async_remote_copy(
                src_ref=mine, dst_ref=mine, send_sem=send_sems.at[t], recv_sem=recv_sems.at[t],
                device_id=(x, y, 1 - c), device_id_type=MESH)
            cp.start()
            cps.append((cp, theirs))
        for t, (cp, theirs) in enumerate(cps):
            pltpu.make_async_remote_copy(
                src_ref=theirs, dst_ref=theirs, send_sem=send_sems.at[t], recv_sem=recv_sems.at[t],
                device_id=(x, y, 1 - c), device_id_type=MESH).wait_recv()
            cp.wait_send()

    return pl.pallas_call(
        body, name=name, out_shape=[jax.ShapeDtypeStruct(v.shape, v.dtype) for v in bufs],
        in_specs=_hbm_specs(nt), out_specs=_hbm_specs(nt), input_output_aliases={t: t for t in range(nt)},
        scratch_shapes=[pltpu.SemaphoreType.DMA((nt,)), pltpu.SemaphoreType.DMA((nt,))],
    )(*bufs)


BIG = ("w_in", "w_out", "w_gate", "w_up", "w_down")


GROUPS = (("w_in",), ("w_out",), ("w_gate", "w_up", "w_down"))


class _Ship:
    def __init__(self, tag, pos):
        self.tag, self.pos, self.started, self.state = tag, pos, None, None

    def early(self, grads):
        self.early_keys = tuple(grads)
        self.started = _sibling_start(self.tag + "sibling_start", [grads[k] for k in self.early_keys])
        return self.started[-1]

    def finish(self, grads, after):
        keys, mine, got = tuple(grads), [grads[k] for k in grads], []
        if keys:
            got = list(_rs_to_sibling(self.tag + "to_sibling", mine))
        if self.started is not None:
            send_sems, recv_sems, src, land, _ = self.started
            src, land = _sibling_wait(self.tag + "sibling_wait", src, land, send_sems, recv_sems, after)
            keys, mine, got = self.early_keys + keys, src + mine, land + got
        sums = [_rs_pair_sum(self.tag + "pair_sum_" + k, self.pos, g, o) for k, g, o in zip(keys, mine, got)]
        self.state = (keys,) + _scatter_start(self.tag + "start", sums)
        return self.state[-1]


def _rs_end(tag, state, pos, after):
    keys, send_sems, recv_sems, sums, land, _ = state
    sums, land = _scatter_wait(tag + "wait", sums, land, send_sems, recv_sems, after)
    halves = [_rs_chip_sum(tag + "chip_sum_" + k, pos, s, v) for k, s, v in zip(keys, sums, land)]
    return dict(zip(keys, _rs_share_halves(tag + "share", halves)))


def _adam_math(w, g, m, v):
    m = ADAM_B1 * m + (1.0 - ADAM_B1) * g
    v = ADAM_B2 * v + (1.0 - ADAM_B2) * (g * g)
    m_hat = m / (1.0 - ADAM_B1 ** ADAM_STEP)
    v_hat = v / (1.0 - ADAM_B2 ** ADAM_STEP)
    delta = -ADAM_LR * (m_hat / (jnp.sqrt(v_hat) + ADAM_EPS) + ADAM_WD * w)
    return delta, m, v


def _adam_big(name, g0, g1, w, m, v):
    _, r, cdim = w.shape
    tr = 128 if r % 128 == 0 else 64
    nb = r // tr

    def body(g0_ref, g1_ref, w_ref, m_ref, v_ref, go_ref, d_ref, mo_ref, vo_ref):
        l = pl.program_id(0)
        g = jnp.where(l == 0, g0_ref[...], g1_ref[...])
        delta, mn, vn = _adam_math(w_ref[...], g, m_ref[...], v_ref[...])
        go_ref[...] = g
        d_ref[...] = delta
        mo_ref[...] = mn
        vo_ref[...] = vn

    lay = pl.BlockSpec((None, tr, cdim), lambda l, i: (l, i, 0))
    return pl.pallas_call(
        body, name=name, grid=(2, nb),
        in_specs=[pl.BlockSpec((tr, cdim), lambda l, i: (i * (1 - l) + (nb - 1) * l, 0)),
                  pl.BlockSpec((tr, cdim), lambda l, i: (i * l, 0)), lay, lay, lay],
        out_specs=[lay] * 4, out_shape=[jax.ShapeDtypeStruct(w.shape, f32)] * 4,
        compiler_params=_cp(("arbitrary", "arbitrary")),
    )(g0, g1, w, m, v)


def _adam_w_in(name, g0, g1, w, m, v):
    wt, mt, vt = (jnp.transpose(a, (2, 0, 1)) for a in (w, m, v))
    cols, _, rows = wt.shape

    def body(g0_ref, g1_ref, w_ref, m_ref, v_ref, go_ref, d_ref, mo_ref, vo_ref):
        for l, g_ref in enumerate((g0_ref, g1_ref)):
            g = jnp.transpose(g_ref[...])
            delta, mn, vn = _adam_math(w_ref[:, l, :], g, m_ref[:, l, :], v_ref[:, l, :])
            go_ref[:, l, :] = g
            d_ref[:, l, :] = delta
            mo_ref[:, l, :] = mn
            vo_ref[:, l, :] = vn

    blk = pl.BlockSpec((LANE, 2, rows), lambda i: (i, 0, 0))
    gblk = pl.BlockSpec((rows, LANE), lambda i: (0, i))
    outs = pl.pallas_call(
        body, name=name, grid=(pl.cdiv(cols, LANE),), in_specs=[gblk, gblk, blk, blk, blk], out_specs=[blk] * 4,
        out_shape=[jax.ShapeDtypeStruct(wt.shape, f32)] * 4, compiler_params=_cp(("parallel",)),
    )(g0, g1, wt, mt, vt)
    return tuple(jnp.transpose(o, (1, 2, 0)) for o in outs)


def _sum8(name, gathered):
    m_per = gathered.shape[0] // 8

    def body(g_ref, o_ref):
        tot = g_ref[pl.ds(0, m_per), :]
        for d in range(1, 8):
            tot = tot + g_ref[pl.ds(d * m_per, m_per), :]
        o_ref[...] = tot

    return pl.pallas_call(body, name=name, out_shape=jax.ShapeDtypeStruct((m_per, LANE), f32))(gathered)


def _adam_small(name, g, w, m, v):
    def body(g_ref, w_ref, m_ref, v_ref, d_ref, mo_ref, vo_ref):
        delta, mn, vn = _adam_math(w_ref[...], g_ref[...], m_ref[...], v_ref[...])
        d_ref[...] = delta
        mo_ref[...] = mn
        vo_ref[...] = vn

    return pl.pallas_call(body, name=name, out_shape=[jax.ShapeDtypeStruct(g.shape, f32)] * 3)(g, w, m, v)


def _pack(vals):
    rows, offs, at = [], [], 0
    for a in vals:
        a = a.reshape(-1)
        n = -(-a.shape[0] // (8 * LANE)) * 8
        rows.append(jnp.pad(a, (0, n * LANE - a.shape[0])).reshape(n, LANE))
        offs.append(at)
        at += n
    return jnp.concatenate(rows, axis=0), offs


def _unpack(packed, offs, shapes):
    out = []
    for o, shp in zip(offs, shapes):
        size = 1
        for d in shp:
            size *= d
        n = -(-size // LANE)
        out.append(packed[o:o + n].reshape(-1)[:size].reshape(shp))
    return out


SMALL = ("norm_mix_g", "conv_w", "pool_w", "pool_scale", "gla_w_decay", "gla_b_decay", "gla_out_g",
         "fox_q_g", "fox_k_g", "fox_b_f", "norm_ffn_g")
ALL = ("norm_mix_g", "w_in", "conv_w", "pool_w", "pool_scale", "gla_w_decay", "gla_b_decay", "gla_out_g",
       "fox_q_g", "fox_k_g", "fox_b_f", "w_out", "norm_ffn_g", "w_gate", "w_up", "w_down")


def kernel(x, norm_mix_g, w_in, conv_w, pool_w, pool_scale, gla_w_decay, gla_b_decay, gla_out_g, fox_q_g, fox_k_g, fox_b_f, w_out, norm_ffn_g, w_gate, w_up, w_down, loss_target, m_norm_mix_g, m_w_in, m_conv_w, m_pool_w, m_pool_scale, m_gla_w_decay, m_gla_b_decay, m_gla_out_g, m_fox_q_g, m_fox_k_g, m_fox_b_f, m_w_out, m_norm_ffn_g, m_w_gate, m_w_up, m_w_down, v_norm_mix_g, v_w_in, v_conv_w, v_pool_w, v_pool_scale, v_gla_w_decay, v_gla_b_decay, v_gla_out_g, v_fox_q_g, v_fox_k_g, v_fox_b_f, v_w_out, v_norm_ffn_g, v_w_gate, v_w_up, v_w_down):
    w = dict(norm_mix_g=norm_mix_g, w_in=w_in, conv_w=conv_w, pool_w=pool_w, pool_scale=pool_scale,
             gla_w_decay=gla_w_decay, gla_b_decay=gla_b_decay, gla_out_g=gla_out_g, fox_q_g=fox_q_g, fox_k_g=fox_k_g,
             fox_b_f=fox_b_f, w_out=w_out, norm_ffn_g=norm_ffn_g, w_gate=w_gate, w_up=w_up, w_down=w_down)
    m = dict(norm_mix_g=m_norm_mix_g, w_in=m_w_in, conv_w=m_conv_w, pool_w=m_pool_w, pool_scale=m_pool_scale,
             gla_w_decay=m_gla_w_decay, gla_b_decay=m_gla_b_decay, gla_out_g=m_gla_out_g, fox_q_g=m_fox_q_g,
             fox_k_g=m_fox_k_g, fox_b_f=m_fox_b_f, w_out=m_w_out, norm_ffn_g=m_norm_ffn_g, w_gate=m_w_gate,
             w_up=m_w_up, w_down=m_w_down)
    v = dict(norm_mix_g=v_norm_mix_g, w_in=v_w_in, conv_w=v_conv_w, pool_w=v_pool_w, pool_scale=v_pool_scale,
             gla_w_decay=v_gla_w_decay, gla_b_decay=v_gla_b_decay, gla_out_g=v_gla_out_g, fox_q_g=v_fox_q_g,
             fox_k_g=v_fox_k_g, fox_b_f=v_fox_b_f, w_out=v_w_out, norm_ffn_g=v_norm_ffn_g, w_gate=v_w_gate,
             w_up=v_w_up, w_down=v_w_down)
    chip = 2 * lax.axis_index("x") + lax.axis_index("y")

    pos = jnp.stack([chip, lax.axis_index("c")]).astype(jnp.int32)

    mine, offs = _pack([conv_w, gla_w_decay, jnp.zeros((8, LANE), f32)])
    order = [(l, grp) for l in range(2) for grp in GROUPS]
    sems, gbufs = _gather_start(
        "gather_start",
        [[_own_slot(mine, chip)]] + [[_own_slot(w[k][l].astype(bf16), chip) for k in grp] for l, grp in order])

    def gathered(tag, gi, after):
        got = _gather_wait(tag + "_wait", gbufs[gi], sems[gi][0], sems[gi][1], after)
        return _gather_exchange(tag + "_exchange", got)

    def weights(l, group, after):
        got = gathered(f"l{l}_gather_{group}", 1 + 3 * l + ("w_in", "w_out", "ffn").index(group), after)
        if group == "w_in":
            return _win_from_blocks(got[0])
        return got[0] if group == "w_out" else got

    every = gathered("gather_small", 0, x)[0]
    per_chip = [_unpack(every[kk], offs, [conv_w.shape, gla_w_decay.shape]) for kk in range(NCHIP)]
    conv_full = jnp.concatenate([pc[0] for pc in per_chip], axis=-1)[:, :, 0, :]
    wdec_full = jnp.concatenate([pc[1] for pc in per_chip], axis=-1)

    wdec = jnp.pad(wdec_full.reshape(2, 16, 4, 64), ((0, 0), (GA_LANE, LANE - GA_LANE - 16), (0, 0), (0, 64)))
    p = dict(
        g_mix=norm_mix_g[:, None, :], g_ffn=norm_ffn_g[:, None, :],
        conv_w=conv_full, pool_w=pool_w, pool_scale=pool_scale[:, None, :],
        wdec=wdec.reshape(2, LANE, G),
        bdec=jnp.pad(gla_b_decay.reshape(2, 4, 64), ((0, 0), (0, 0), (0, 64))).reshape(2, 1, G),
        gla_og=gla_out_g[:, None, :], fox_qg=fox_q_g[:, None, :], fox_kg=fox_k_g[:, None, :],
        fox_bf=jnp.pad(fox_b_f, ((0, 0), (FF_LANE, LANE - FF_LANE - 4)))[:, None, :])

    h0 = x[0]
    h1, sv0 = _layer_fwd(0, h0, p, weights)
    h2, sv1 = _layer_fwd(1, h1, p, weights)
    sq, dh, dhb = _loss("loss", h2, loss_target[0])
    loss = lax.psum(sq[0, 0] * (0.5 / D), ("x", "y", "c"))

    late = ("w_out", "w_gate", "w_up", "w_down")
    ships = {tag: _Ship(tag, pos) for tag in ("l1_rs_a_", "l1_rs_b_", "l0_rs_a_", "l0_rs_b_")}
    dh1, dy, dgf1 = _layer_bwd_ffn(1, dh, dhb, p, sv1, ships["l1_rs_a_"])
    dh, dhb, small1 = _layer_bwd_mix(1, dh1, dy, p, sv1, ships["l1_rs_b_"])
    dh1, dy, dgf0 = _layer_bwd_ffn(0, dh, dhb, p, sv0, ships["l0_rs_a_"])
    dh, dhb, small0 = _layer_bwd_mix(0, dh1, dy, p, sv0, ships["l0_rs_b_"])
    small0["norm_ffn_g"], small1["norm_ffn_g"] = dgf0[0], dgf1[0]

    red1 = _rs_end("l1_rs_a_", ships["l1_rs_a_"].state, pos, dh)
    red1.update(_rs_end("l1_rs_b_", ships["l1_rs_b_"].state, pos, red1["w_down"]))
    red0 = _rs_end("l0_rs_a_", ships["l0_rs_a_"].state, pos, red1["w_in"])
    grads, deltas, new_m, new_v = {}, {}, {}, {}
    for k in late:
        grads[k], deltas[k], new_m[k], new_v[k] = _adam_big("adam_" + k, red0[k], red1[k], w[k], m[k], v[k])

    packed, goffs = _pack([jnp.stack([small0[k], small1[k]]) for k in SMALL])
    total = _sum8("sum_small_grads", _allgather_small("gather_small_grads", packed, new_v["w_down"]))
    red0.update(_rs_end("l0_rs_b_", ships["l0_rs_b_"].state, pos, total))
    k = "w_in"
    grads[k], deltas[k], new_m[k], new_v[k] = _adam_w_in("adam_" + k, red0[k], red1[k], w[k], m[k], v[k])
    full_shapes = [(2,) + small0[k].shape for k in SMALL]
    gsmall = dict(zip(SMALL, _unpack(total, goffs, full_shapes)))
    gsmall["conv_w"] = lax.dynamic_slice_in_dim(gsmall["conv_w"], chip * LANE, LANE, axis=2)[:, :, None, :]
    gsmall["gla_w_decay"] = lax.dynamic_slice_in_dim(gsmall["gla_w_decay"], chip * 64, 64, axis=2)
    gp, loffs = _pack([gsmall[k] for k in SMALL])
    wp, _ = _pack([w[k] for k in SMALL])
    mp, _ = _pack([m[k] for k in SMALL])
    vp, _ = _pack([v[k] for k in SMALL])
    dp, mnp, vnp = _adam_small("adam_small", gp, wp, mp, vp)
    shapes = [w[k].shape for k in SMALL]
    for k, a, b, c_, d_ in zip(SMALL, _unpack(gp, loffs, shapes), _unpack(dp, loffs, shapes),
                               _unpack(mnp, loffs, shapes), _unpack(vnp, loffs, shapes)):
        grads[k], deltas[k], new_m[k], new_v[k] = a, b, c_, d_

    return (loss, dh[None], *[grads[k] for k in ALL], *[deltas[k] for k in ALL],
            *[new_m[k] for k in ALL], *[new_v[k] for k in ALL])
```

```python
import functools

import jax
import jax.numpy as jnp
from jax import lax
from jax.experimental import pallas as pl
from jax.experimental.pallas import tpu as pltpu

f32 = jnp.float32
bf16 = jnp.bfloat16

D = 2048
G = 512
DFF = 5632
NCHIP = 4
FB = DFF // NCHIP
WIN = 5140
WINB = WIN // NCHIP
EPS = 1e-6
CHUNK = 64
LANE = 128

CB, CC, CH, PU, GQ, GK, GV, GG, FQ, FK, FV, MISC = 0, 4, 8, 12, 16, 20, 24, 28, 32, 36, 40, 44
ZC = 45 * LANE
FF_LANE = 0
GA_LANE = 8

ADAM_LR, ADAM_B1, ADAM_B2, ADAM_EPS, ADAM_WD, ADAM_STEP = 0.001, 0.9, 0.999, 1e-08, 0.01, 10

VMEM_LIMIT = 60 * 1024 * 1024
MESH = pl.DeviceIdType.MESH


def _cp(sem=None):
    return pltpu.CompilerParams(dimension_semantics=sem, vmem_limit_bytes=VMEM_LIMIT)


def _dot(a, b, dims=((1,), (0,))):
    return lax.dot_general(a.astype(bf16), b.astype(bf16), (dims, ((), ())), preferred_element_type=f32)


def _bdot(a, b, ca, cb):
    return lax.dot_general(a.astype(bf16), b.astype(bf16), (((ca,), (cb,)), ((0,), (0,))),
                           preferred_element_type=f32)


def _log_sigmoid(x):
    return jnp.minimum(x, 0.0) - jnp.log(1.0 + jnp.exp(-jnp.abs(x)))


@jax.custom_vjp
def _sigmoid(x):
    return 1.0 / (1.0 + jnp.exp(-x))


def _sigmoid_fwd(x):
    s = _sigmoid(x)
    return s, s


def _sigmoid_bwd(s, g):
    return (g * s * (1.0 - s),)


_sigmoid.defvjp(_sigmoid_fwd, _sigmoid_bwd)


def _rms(x, g):
    return x * lax.rsqrt(jnp.mean(x * x, axis=-1, keepdims=True) + EPS) * g


def _shift_impl(x, n, period, transpose):
    rows = x.shape[0]
    t = lax.broadcasted_iota(jnp.int32, x.shape, 0)
    if period is not None:
        t = t & (period - 1)
    keep = t >= n
    if not transpose:
        return jnp.where(keep, pltpu.roll(x, n, 0), 0.0)
    return pltpu.roll(jnp.where(keep, x, 0.0), rows - n, 0)


def _shift(x, n, period=None):
    @jax.custom_vjp
    def f(v):
        return _shift_impl(v, n, period, False)

    def fwd(v):
        return f(v), None

    def bwd(_, g):
        return (_shift_impl(g, n, period, True),)

    f.defvjp(fwd, bwd)
    return f(x)


def _cumsum_rows(x, length, period=None):
    n = 1
    while n < length:
        x = x + _shift(x, n, period)
        n *= 2
    return x


def _convpool_fn(cb, cc, ch, pu, w0, w1, w2, pw, ps, j):
    u = cc * ch
    y = w2 * u + w1 * _shift(u, 1) + w0 * _shift(u, 2)
    ya = cb * y
    s2 = pu + _shift(pu, 1)
    s4 = s2 + _shift(s2, 2)
    s8 = s4 + _shift(s4, 4)
    s16 = s8 + _shift(s8, 8)
    wsum = jnp.where(j == 0, s2, jnp.where(j == 1, s4, jnp.where(j == 2, s8, s16)))
    width = (2 << j).astype(f32)
    t = lax.broadcasted_iota(jnp.int32, pu.shape, 0).astype(f32)
    count = jnp.minimum(t + 1.0, width)
    d = wsum / count - pu
    yb = _dot(d, pw) * ps
    return ya, yb


def _foxprep_fn(misc, bf):
    lf = _log_sigmoid(misc + bf)
    fc = _cumsum_rows(lf, lf.shape[0])
    return fc, jnp.transpose(fc)


def _fox_fn(q, k, v, fcol, frow8, qg, kg, h, i):
    tq, s = q.shape[0], k.shape[0]
    qn = _rms(q, qg)
    kn = _rms(k, kg)
    lg = _dot(qn, kn, ((1,), (1,))) * (LANE ** -0.5)
    lane = lax.broadcasted_iota(jnp.int32, fcol.shape, 1)
    fq = jnp.sum(jnp.where(lane == h, fcol, 0.0), axis=1, keepdims=True)
    row = lax.broadcasted_iota(jnp.int32, frow8.shape, 0)
    fk = jnp.sum(jnp.where(row == h, frow8, 0.0), axis=0, keepdims=True)
    lg = lg + fq - fk
    qpos = i * tq + lax.broadcasted_iota(jnp.int32, (tq, s), 0)
    kpos = lax.broadcasted_iota(jnp.int32, (tq, s), 1)
    lg = jnp.where(kpos <= qpos, lg, -jnp.inf)
    m = lax.stop_gradient(jnp.max(lg, axis=1, keepdims=True))
    e = jnp.exp(lg - m)
    p = e / jnp.sum(e, axis=1, keepdims=True)
    return _dot(p, v)


def _gla1_fn(q, k, v, misc, wd, bd):
    ts = q.shape[0]
    nb = ts // CHUNK
    x = _dot(misc, wd) + bd
    la = _log_sigmoid(x) * (1.0 / 16.0)
    cc = _cumsum_rows(la, CHUNK, CHUNK)
    la3 = la.reshape(nb, CHUNK, LANE)
    last3 = jnp.sum(la3, axis=1, keepdims=True)
    last2 = jnp.sum(la3, axis=1)
    cc3 = cc.reshape(nb, CHUNK, LANE)
    q3 = (q * 0.125).reshape(nb, CHUNK, LANE)
    k3 = k.reshape(nb, CHUNK, LANE)
    v3 = v.reshape(nb, CHUNK, LANE)
    ep = jnp.exp(cc3)
    en = jnp.exp(-cc3)
    qe = q3 * ep
    a1 = _bdot(qe, k3 * en, 2, 2)
    a2 = _bdot(q3 * en, k3 * ep, 2, 2)
    ti = lax.broadcasted_iota(jnp.int32, a1.shape, 1)
    si = lax.broadcasted_iota(jnp.int32, a1.shape, 2)
    sc = jnp.where(si <= ti, a1, a2)
    oi = _bdot(sc, v3, 2, 1)
    kd = k3 * jnp.exp(last3 - cc3)
    el = jnp.exp(last2)
    return qe.reshape(ts, LANE), kd.reshape(ts, LANE), el, oi.reshape(ts, LANE)


def _gla3_fn(o, gg, og):
    return _rms(o, og) * (gg * _sigmoid(gg))


def _ffn_fn(gate, up):
    return gate * _sigmoid(gate) * up


def _mm(name, a, b, a_spec, b_spec, out_shape, out_spec, grid, dims, nk, res=None, res_spec=None, dep=None):
    has_res = res is not None
    has_dep = dep is not None
    nax = len(grid)

    def body(*refs):
        a_ref, b_ref = refs[0], refs[1]
        res_ref = refs[2] if has_res else None
        out_ref = refs[2 + has_res + has_dep]
        part = _dot(a_ref[...], b_ref[...], dims)
        if nk == 1:
            if has_res:
                part = part + res_ref[...]
            out_ref[...] = part.astype(out_ref.dtype)
            return
        acc_ref = refs[3 + has_res + has_dep]
        k = pl.program_id(nax - 1)

        @pl.when(k == 0)
        def _():
            acc_ref[...] = part

        @pl.when(k > 0)
        def _():
            acc_ref[...] += part

        @pl.when(k == nk - 1)
        def _():
            tot = acc_ref[...]
            if has_res:
                tot = tot + res_ref[...]
            out_ref[...] = tot.astype(out_ref.dtype)

    ops = [a, b] + ([res] if has_res else []) + ([dep] if has_dep else [])
    specs = [a_spec, b_spec] + ([res_spec] if has_res else [])
    if has_dep:
        specs.append(pl.BlockSpec((8, LANE), lambda *_: (0, 0)))
    blk = tuple(d for d in out_spec.block_shape if d is not None)
    scratch = [pltpu.VMEM(blk, f32)] if nk > 1 else []
    return pl.pallas_call(
        body, name=name, grid=grid, in_specs=specs, out_specs=out_spec, out_shape=out_shape,
        scratch_shapes=scratch,
        compiler_params=_cp(("parallel",) * (nax - 1) + ("arbitrary",)),
    )(*ops)


def _tm(s):
    return min(s, 512)


def _tm_big(s):
    return min(s, 1024)


def _mm_gate_up(name, u, w_gate, w_up):
    s = u.shape[0]
    tm = _tm(s)

    def body(u_ref, wg_ref, wu_ref, g_ref, up_ref, a_ref):
        g = _dot(u_ref[...], wg_ref[...])
        up = _dot(u_ref[...], wu_ref[...])
        g_ref[...] = g
        up_ref[...] = up
        a_ref[...] = _ffn_fn(g, up).astype(bf16)

    wspec = pl.BlockSpec((None, D, FB), lambda j, i: (j, 0, 0))
    tile = pl.BlockSpec((tm, FB), lambda j, i: (i, j))
    return pl.pallas_call(
        body, name=name, grid=(NCHIP, s // tm),
        in_specs=[pl.BlockSpec((tm, D), lambda j, i: (i, 0)), wspec, wspec], out_specs=[tile, tile, tile],
        out_shape=[jax.ShapeDtypeStruct((s, DFF), f32), jax.ShapeDtypeStruct((s, DFF), f32),
                   jax.ShapeDtypeStruct((s, DFF), bf16)],
        compiler_params=_cp(("parallel", "parallel")),
    )(u, w_gate, w_up)


def _mm_dact(name, dh, w_down, gate, up):
    s = dh.shape[0]
    tm = _tm(s)

    def body(dh_ref, wd_ref, g_ref, up_ref, dg_ref, du_ref):
        dact = _dot(dh_ref[...], wd_ref[...], ((1,), (1,)))
        _, vjp = jax.vjp(_ffn_fn, g_ref[...], up_ref[...])
        dg, du = vjp(dact)
        dg_ref[...] = dg.astype(bf16)
        du_ref[...] = du.astype(bf16)

    tile = pl.BlockSpec((tm, FB), lambda j, i: (i, j))
    return pl.pallas_call(
        body, name=name, grid=(NCHIP, s // tm),
        in_specs=[pl.BlockSpec((tm, D), lambda j, i: (i, 0)), pl.BlockSpec((None, FB, D), lambda j, i: (j, 0, 0)),
                  tile, tile],
        out_specs=[tile, tile], out_shape=[jax.ShapeDtypeStruct((s, DFF), bf16)] * 2,
        compiler_params=_cp(("parallel", "parallel")),
    )(dh, w_down, gate, up)


def _rmsnorm_fwd(name, x, g, l):
    s = x.shape[0]
    tm = min(s, 256)

    def body(x_ref, g_ref, u_ref):
        u_ref[...] = _rms(x_ref[...], g_ref[...]).astype(bf16)

    return pl.pallas_call(
        body, name=name, grid=(s // tm,),
        in_specs=[pl.BlockSpec((tm, D), lambda i: (i, 0)), pl.BlockSpec((None, 1, D), lambda i: (l, 0, 0))],
        out_specs=pl.BlockSpec((tm, D), lambda i: (i, 0)),
        out_shape=jax.ShapeDtypeStruct((s, D), bf16), compiler_params=_cp(("parallel",)),
    )(x, g)


def _rmsnorm_bwd(name, x, g, du, dres, l, dep=None):
    s = x.shape[0]
    tm = min(s, 256)

    def body(x_ref, g_ref, du_ref, dres_ref, *rest):
        dx_ref, dxb_ref, dg_ref = rest[-3:]
        _, vjp = jax.vjp(_rms, x_ref[...], g_ref[...])
        dx, dg = vjp(du_ref[...])
        tot = dx + dres_ref[...]
        dx_ref[...] = tot
        dxb_ref[...] = tot.astype(bf16)

        @pl.when(pl.program_id(0) == 0)
        def _():
            dg_ref[...] = dg

        @pl.when(pl.program_id(0) > 0)
        def _():
            dg_ref[...] += dg

    row = pl.BlockSpec((tm, D), lambda i: (i, 0))
    deps = [] if dep is None else [dep]
    return pl.pallas_call(
        body, name=name, grid=(s // tm,),
        in_specs=[row, pl.BlockSpec((None, 1, D), lambda i: (l, 0, 0)), row, row]
        + [pl.BlockSpec((8, LANE), lambda i: (0, 0)) for _ in deps],
        out_specs=[row, row, pl.BlockSpec((1, D), lambda i: (0, 0))],
        out_shape=[jax.ShapeDtypeStruct((s, D), f32), jax.ShapeDtypeStruct((s, D), bf16),
                   jax.ShapeDtypeStruct((1, D), f32)],
        compiler_params=_cp(("arbitrary",)),
    )(x, g, du, dres, *deps)


def _loss(name, y, t):
    s = y.shape[0]
    tm = min(s, 256)
    row = pl.BlockSpec((tm, D), lambda i: (i, 0))

    def body(y_ref, t_ref, l_ref, d_ref, db_ref):
        e = y_ref[...] - t_ref[...]
        d = e * (1.0 / D)
        d_ref[...] = d
        db_ref[...] = d.astype(bf16)
        part = jnp.zeros((8, LANE), f32) + jnp.sum(e * e)

        @pl.when(pl.program_id(0) == 0)
        def _():
            l_ref[...] = part

        @pl.when(pl.program_id(0) > 0)
        def _():
            l_ref[...] += part

    return pl.pallas_call(
        body, name=name, grid=(s // tm,), in_specs=[row, row],
        out_specs=[pl.BlockSpec((8, LANE), lambda i: (0, 0)), row, row],
        out_shape=[jax.ShapeDtypeStruct((8, LANE), f32), jax.ShapeDtypeStruct((s, D), f32),
                   jax.ShapeDtypeStruct((s, D), bf16)],
        compiler_params=_cp(("arbitrary",)),
    )(y, t)


def _zspec(s, blk):
    return pl.BlockSpec((s, LANE), lambda j: (0, blk + j))


def _convpool_specs(s, l):
    return [_zspec(s, CB), _zspec(s, CC), _zspec(s, CH), _zspec(s, PU),
            pl.BlockSpec((None, 3, LANE), lambda j: (l, 0, j)),
            pl.BlockSpec((None, None, LANE, LANE), lambda j: (l, j, 0, 0)),
            pl.BlockSpec((None, 1, LANE), lambda j: (l, 0, j))]


def _convpool_fwd(name, z, conv_w, pool_w, pool_scale, l):
    s = z.shape[0]

    def body(cb, cc, ch, pu, cw, pw, ps, ya_ref, yb_ref):
        ya, yb = _convpool_fn(cb[...], cc[...], ch[...], pu[...], cw[0:1, :], cw[1:2, :], cw[2:3, :], pw[...], ps[...],
                              pl.program_id(0))
        ya_ref[...] = ya.astype(bf16)
        yb_ref[...] = yb.astype(bf16)

    col = pl.BlockSpec((s, LANE), lambda j: (0, j))
    return pl.pallas_call(
        body, name=name, grid=(4,), in_specs=_convpool_specs(s, l), out_specs=[col, col],
        out_shape=[jax.ShapeDtypeStruct((s, G), bf16)] * 2, compiler_params=_cp(("parallel",)),
    )(z, z, z, z, conv_w, pool_w, pool_scale)


def _convpool_bwd(name, z, conv_w, pool_w, pool_scale, dy, l):
    s = z.shape[0]

    def body(cb, cc, ch, pu, cw, pw, ps, dya, dyb, dcb, dcc, dch, dpu, dcw, dpw, dps):
        j = pl.program_id(0)
        fn = functools.partial(_convpool_fn, j=j)
        _, vjp = jax.vjp(fn, cb[...], cc[...], ch[...], pu[...], cw[0:1, :], cw[1:2, :], cw[2:3, :], pw[...], ps[...])
        g = vjp((dya[...], dyb[...]))
        dcb[...] = g[0].astype(bf16)
        dcc[...] = g[1].astype(bf16)
        dch[...] = g[2].astype(bf16)
        dpu[...] = g[3].astype(bf16)
        dcw[0:1, :] = g[4]
        dcw[1:2, :] = g[5]
        dcw[2:3, :] = g[6]
        dpw[...] = g[7]
        dps[...] = g[8]

    col = pl.BlockSpec((s, LANE), lambda j: (0, j))
    specs = _convpool_specs(s, l) + [pl.BlockSpec((s, LANE), lambda j: (0, j)),
                                     pl.BlockSpec((s, LANE), lambda j: (0, 4 + j))]
    return pl.pallas_call(
        body, name=name, grid=(4,), in_specs=specs,
        out_specs=[col, col, col, col, pl.BlockSpec((3, LANE), lambda j: (0, j)),
                   pl.BlockSpec((None, LANE, LANE), lambda j: (j, 0, 0)), pl.BlockSpec((1, LANE), lambda j: (0, j))],
        out_shape=[jax.ShapeDtypeStruct((s, G), bf16)] * 4 + [
            jax.ShapeDtypeStruct((3, G), f32), jax.ShapeDtypeStruct((4, LANE, LANE), f32),
            jax.ShapeDtypeStruct((1, G), f32)],
        compiler_params=_cp(("parallel",)),
    )(z, z, z, z, conv_w, pool_w, pool_scale, dy, dy)


def _foxprep_fwd(name, z, bf, l):
    s = z.shape[0]

    def body(m_ref, b_ref, fc_ref, fr_ref):
        fc, fr = _foxprep_fn(m_ref[...], b_ref[...])
        fc_ref[...] = fc
        fr_ref[...] = fr

    return pl.pallas_call(
        body, name=name, grid=(1,),
        in_specs=[pl.BlockSpec((s, LANE), lambda i: (0, MISC)), pl.BlockSpec((None, 1, LANE), lambda i: (l, 0, 0))],
        out_specs=[pl.BlockSpec((s, LANE), lambda i: (0, 0)), pl.BlockSpec((LANE, s), lambda i: (0, 0))],
        out_shape=[jax.ShapeDtypeStruct((s, LANE), f32), jax.ShapeDtypeStruct((LANE, s), f32)],
        compiler_params=_cp(("arbitrary",)),
    )(z, bf)


def _foxprep_bwd(name, z, bf, dfc4, dfr4, dmisc4, l):
    s = z.shape[0]

    def body(m_ref, b_ref, dfc_ref, dfr_ref, dm4_ref, dm_ref, db_ref):
        _, vjp = jax.vjp(_foxprep_fn, m_ref[...], b_ref[...])
        dfc = dfc_ref[0] + dfc_ref[1] + dfc_ref[2] + dfc_ref[3]
        dfr = dfr_ref[0] + dfr_ref[1] + dfr_ref[2] + dfr_ref[3]
        dfr = jnp.concatenate([dfr, jnp.zeros((LANE - 8, s), f32)], axis=0)
        dm, db = vjp((dfc, dfr))
        dm = dm + (dm4_ref[0] + dm4_ref[1] + dm4_ref[2] + dm4_ref[3])
        dm_ref[...] = dm.astype(bf16)
        db_ref[...] = db

    whole = lambda shape: pl.BlockSpec(shape, lambda i: (0,) * len(shape))
    return pl.pallas_call(
        body, name=name, grid=(1,),
        in_specs=[pl.BlockSpec((s, LANE), lambda i: (0, MISC)), pl.BlockSpec((None, 1, LANE), lambda i: (l, 0, 0)),
                  whole((4, s, LANE)), whole((4, 8, s)), whole((4, s, LANE))],
        out_specs=[whole((s, LANE)), whole((1, LANE))],
        out_shape=[jax.ShapeDtypeStruct((s, LANE), bf16), jax.ShapeDtypeStruct((1, LANE), f32)],
        compiler_params=_cp(("arbitrary",)),
    )(z, bf, dfc4, dfr4, dmisc4)


FOX_TQ = 256


def _fox_specs(s, l):
    return [pl.BlockSpec((s, LANE), lambda h: (0, FQ + h)),
            pl.BlockSpec((s, LANE), lambda h: (0, FK + h)),
            pl.BlockSpec((s, LANE), lambda h: (0, FV + h)),
            pl.BlockSpec((s, LANE), lambda h: (0, 0)),
            pl.BlockSpec((8, s), lambda h: (0, 0)),
            pl.BlockSpec((None, 1, LANE), lambda h: (l, 0, 0)),
            pl.BlockSpec((None, 1, LANE), lambda h: (l, 0, 0))]


def _fox_fwd(name, z, fc, fr, qg, kg, l):
    s = z.shape[0]
    tq = min(s, FOX_TQ)

    def body(q, k, v, fc_ref, fr_ref, qg_ref, kg_ref, y_ref):
        h = pl.program_id(0)
        for i in range(s // tq):
            rows, keys = pl.ds(i * tq, tq), pl.ds(0, (i + 1) * tq)
            y = _fox_fn(q[rows, :], k[keys, :], v[keys, :], fc_ref[rows, :], fr_ref[:, keys], qg_ref[...],
                        kg_ref[...], h, i)
            y_ref[rows, :] = y.astype(bf16)

    return pl.pallas_call(
        body, name=name, grid=(4,), in_specs=_fox_specs(s, l), out_specs=pl.BlockSpec((s, LANE), lambda h: (0, h)),
        out_shape=jax.ShapeDtypeStruct((s, G), bf16), compiler_params=_cp(("parallel",)),
    )(z, z, z, fc, fr, qg, kg)


def _fox_bwd(name, z, fc, fr, qg, kg, dy, l):
    s = z.shape[0]
    tq = min(s, FOX_TQ)

    def body(q, k, v, fc_ref, fr_ref, qg_ref, kg_ref, dy_ref, dq, dk_out, dv_out, dfc, dfr, dqg, dkg, dk, dv):
        h = pl.program_id(0)
        for ref in (dk, dv, dfr, dqg, dkg):
            ref[...] = jnp.zeros_like(ref)
        for i in range(s // tq):
            rows, keys = pl.ds(i * tq, tq), pl.ds(0, (i + 1) * tq)
            fn = functools.partial(_fox_fn, h=h, i=i)
            _, vjp = jax.vjp(fn, q[rows, :], k[keys, :], v[keys, :], fc_ref[rows, :], fr_ref[:, keys], qg_ref[...],
                             kg_ref[...])
            g = vjp(dy_ref[rows, :])
            dq[rows, :] = g[0].astype(bf16)
            dfc[rows, :] = g[3]
            dk[keys, :] += g[1]
            dv[keys, :] += g[2]
            dfr[:, keys] += g[4]
            dqg[...] += g[5]
            dkg[...] += g[6]
        dk_out[...] = dk[...].astype(bf16)
        dv_out[...] = dv[...].astype(bf16)

    head = pl.BlockSpec((s, LANE), lambda h: (0, h))
    gain = pl.BlockSpec((None, 1, LANE), lambda h: (h, 0, 0))
    return pl.pallas_call(
        body, name=name, grid=(4,), in_specs=_fox_specs(s, l) + [pl.BlockSpec((s, LANE), lambda h: (0, 12 + h))],
        out_specs=[head, head, head, pl.BlockSpec((None, s, LANE), lambda h: (h, 0, 0)),
                   pl.BlockSpec((None, 8, s), lambda h: (h, 0, 0)), gain, gain],
        out_shape=[jax.ShapeDtypeStruct((s, G), bf16)] * 3 + [
            jax.ShapeDtypeStruct((4, s, LANE), f32), jax.ShapeDtypeStruct((4, 8, s), f32),
            jax.ShapeDtypeStruct((4, 1, LANE), f32), jax.ShapeDtypeStruct((4, 1, LANE), f32)],
        scratch_shapes=[pltpu.VMEM((s, LANE), f32), pltpu.VMEM((s, LANE), f32)],
        compiler_params=_cp(("parallel",)),
    )(z, z, z, fc, fr, qg, kg, dy)


def _gla_ts(s):
    return min(s, 512)


def _gla1_specs(s, ts, l):
    return [pl.BlockSpec((ts, LANE), lambda h, i: (i, GQ + h)),
            pl.BlockSpec((ts, LANE), lambda h, i: (i, GK + h)),
            pl.BlockSpec((ts, LANE), lambda h, i: (i, GV + h)),
            pl.BlockSpec((ts, LANE), lambda h, i: (i, MISC)),
            pl.BlockSpec((None, LANE, LANE), lambda h, i: (l, 0, h)),
            pl.BlockSpec((None, 1, LANE), lambda h, i: (l, 0, h))]


def _gla1_fwd(name, z, wd, bd, l):
    s = z.shape[0]
    ts = _gla_ts(s)
    nb = ts // CHUNK

    def body(q, k, v, m, wd_ref, bd_ref, qe_ref, kd_ref, el_ref, oi_ref):
        qe, kd, el, oi = _gla1_fn(q[...], k[...], v[...], m[...], wd_ref[...], bd_ref[...])
        qe_ref[...] = qe.astype(bf16)
        kd_ref[...] = kd.astype(bf16)
        el_ref[...] = el
        oi_ref[...] = oi

    blk = pl.BlockSpec((ts, LANE), lambda h, i: (i, h))
    return pl.pallas_call(
        body, name=name, grid=(4, s // ts), in_specs=_gla1_specs(s, ts, l),
        out_specs=[blk, blk, pl.BlockSpec((nb, LANE), lambda h, i: (i, h)), blk],
        out_shape=[jax.ShapeDtypeStruct((s, G), bf16), jax.ShapeDtypeStruct((s, G), bf16),
                   jax.ShapeDtypeStruct((s // CHUNK, G), f32), jax.ShapeDtypeStruct((s, G), f32)],
        compiler_params=_cp(("parallel", "parallel")),
    )(z, z, z, z, wd, bd)


def _gla1_bwd(name, z, wd, bd, dqe, dkd, del_, do, dvi, l):
    s = z.shape[0]
    ts = _gla_ts(s)
    nb = ts // CHUNK

    def body(q, k, v, m, wd_ref, bd_ref, dqe_ref, dkd_ref, del_ref, do_ref, dvi_ref, dq, dk, dv, dm, dwd, dbd):
        i = pl.program_id(1)
        _, vjp = jax.vjp(_gla1_fn, q[...], k[...], v[...], m[...], wd_ref[...], bd_ref[...])
        g = vjp((dqe_ref[...], dkd_ref[...], del_ref[...], do_ref[...]))
        dq[...] = g[0].astype(bf16)
        dk[...] = g[1].astype(bf16)
        dv[...] = (g[2] + dvi_ref[...]).astype(bf16)
        dm[...] = g[3]

        @pl.when(i == 0)
        def _():
            dwd[...] = g[4]
            dbd[...] = g[5]

        @pl.when(i > 0)
        def _():
            dwd[...] += g[4]
            dbd[...] += g[5]

    blk = pl.BlockSpec((ts, LANE), lambda h, i: (i, h))
    specs = _gla1_specs(s, ts, l) + [blk, blk, pl.BlockSpec((nb, LANE), lambda h, i: (i, h)), blk, blk]
    return pl.pallas_call(
        body, name=name, grid=(4, s // ts), in_specs=specs,
        out_specs=[blk, blk, blk, pl.BlockSpec((None, ts, LANE), lambda h, i: (h, i, 0)),
                   pl.BlockSpec((None, LANE, LANE), lambda h, i: (h, 0, 0)),
                   pl.BlockSpec((None, 1, LANE), lambda h, i: (h, 0, 0))],
        out_shape=[jax.ShapeDtypeStruct((s, G), bf16)] * 3 + [
            jax.ShapeDtypeStruct((4, s, LANE), f32), jax.ShapeDtypeStruct((4, LANE, LANE), f32),
            jax.ShapeDtypeStruct((4, 1, LANE), f32)],
        compiler_params=_cp(("parallel", "arbitrary")),
    )(z, z, z, z, wd, bd, dqe, dkd, del_, do, dvi)


def _gla2_fwd(name, z, qe, kd, el, oi):
    s = z.shape[0]
    n = s // CHUNK

    def body(v_ref, qe_ref, kd_ref, el_ref, oi_ref, o_ref, st_ref, cur):
        cur[...] = jnp.zeros_like(cur)

        def step(c, carry):
            rows = pl.ds(pl.multiple_of(c * CHUNK, CHUNK), CHUNK)
            st = cur[...]
            st_ref[c] = st
            o_ref[rows, :] = oi_ref[rows, :] + _dot(qe_ref[rows, :], st, ((1,), (1,)))
            cur[...] = st * el_ref[pl.ds(c, 1), :] + _dot(v_ref[rows, :], kd_ref[rows, :], ((0,), (0,)))
            return carry

        lax.fori_loop(0, n, step, 0)

    head = pl.BlockSpec((s, LANE), lambda h: (0, h))
    return pl.pallas_call(
        body, name=name, grid=(4,),
        in_specs=[pl.BlockSpec((s, LANE), lambda h: (0, GV + h)), head, head,
                  pl.BlockSpec((n, LANE), lambda h: (0, h)), head],
        out_specs=[head, pl.BlockSpec((None, n, LANE, LANE), lambda h: (h, 0, 0, 0))],
        out_shape=[jax.ShapeDtypeStruct((s, G), f32), jax.ShapeDtypeStruct((4, n, LANE, LANE), f32)],
        scratch_shapes=[pltpu.VMEM((LANE, LANE), f32)],
        compiler_params=_cp(("parallel",)),
    )(z, qe, kd, el, oi)


def _gla2_bwd(name, z, qe, kd, el, st, do):
    s = z.shape[0]
    n = s // CHUNK

    def body(v_ref, qe_ref, kd_ref, el_ref, st_ref, do_ref, dqe_ref, dkd_ref, dv_ref, del_ref, dcur):
        dcur[...] = jnp.zeros_like(dcur)

        def step(t, carry):
            c = n - 1 - t
            rows = pl.ds(pl.multiple_of(c * CHUNK, CHUNK), CHUNK)
            dn = dcur[...]
            stc = st_ref[c]
            doc = do_ref[rows, :]
            dqe_ref[rows, :] = _dot(doc, stc)
            dv_ref[rows, :] = _dot(kd_ref[rows, :], dn, ((1,), (1,)))
            dkd_ref[rows, :] = _dot(v_ref[rows, :], dn)
            del_ref[pl.ds(c, 1), :] = jnp.sum(stc * dn, axis=0, keepdims=True)
            dcur[...] = dn * el_ref[pl.ds(c, 1), :] + _dot(doc, qe_ref[rows, :], ((0,), (0,)))
            return carry

        lax.fori_loop(0, n, step, 0)

    head = pl.BlockSpec((s, LANE), lambda h: (0, h))
    chunk = pl.BlockSpec((n, LANE), lambda h: (0, h))
    return pl.pallas_call(
        body, name=name, grid=(4,),
        in_specs=[pl.BlockSpec((s, LANE), lambda h: (0, GV + h)), head, head, chunk,
                  pl.BlockSpec((None, n, LANE, LANE), lambda h: (h, 0, 0, 0)), head],
        out_specs=[head, head, head, chunk],
        out_shape=[jax.ShapeDtypeStruct((s, G), f32)] * 3 + [jax.ShapeDtypeStruct((n, G), f32)],
        scratch_shapes=[pltpu.VMEM((LANE, LANE), f32)],
        compiler_params=_cp(("parallel",)),
    )(z, qe, kd, el, st, do)


def _gla3_specs(ts, l):
    return [pl.BlockSpec((ts, LANE), lambda h, i: (i, h)),
            pl.BlockSpec((ts, LANE), lambda h, i: (i, GG + h)),
            pl.BlockSpec((None, 1, LANE), lambda h, i: (l, 0, 0))]


def _gla3_fwd(name, o, z, og, l):
    s = z.shape[0]
    ts = _gla_ts(s)

    def body(o_ref, g_ref, og_ref, y_ref):
        y_ref[...] = _gla3_fn(o_ref[...], g_ref[...], og_ref[...]).astype(bf16)

    return pl.pallas_call(
        body, name=name, grid=(4, s // ts), in_specs=_gla3_specs(ts, l),
        out_specs=pl.BlockSpec((ts, LANE), lambda h, i: (i, h)),
        out_shape=jax.ShapeDtypeStruct((s, G), bf16), compiler_params=_cp(("parallel", "parallel")),
    )(o, z, og)


def _gla3_bwd(name, o, z, og, dy, l):
    s = z.shape[0]
    ts = _gla_ts(s)

    def body(o_ref, g_ref, og_ref, dy_ref, do_ref, dg_ref, dog_ref):
        i = pl.program_id(1)
        _, vjp = jax.vjp(_gla3_fn, o_ref[...], g_ref[...], og_ref[...])
        g = vjp(dy_ref[...])
        do_ref[...] = g[0]
        dg_ref[...] = g[1].astype(bf16)

        @pl.when(i == 0)
        def _():
            dog_ref[...] = g[2]

        @pl.when(i > 0)
        def _():
            dog_ref[...] += g[2]

    blk = pl.BlockSpec((ts, LANE), lambda h, i: (i, h))
    return pl.pallas_call(
        body, name=name, grid=(4, s // ts),
        in_specs=_gla3_specs(ts, l) + [pl.BlockSpec((ts, LANE), lambda h, i: (i, 8 + h))],
        out_specs=[blk, blk, pl.BlockSpec((None, 1, LANE), lambda h, i: (h, 0, 0))],
        out_shape=[jax.ShapeDtypeStruct((s, G), f32), jax.ShapeDtypeStruct((s, G), bf16),
                   jax.ShapeDtypeStruct((4, 1, LANE), f32)],
        compiler_params=_cp(("parallel", "arbitrary")),
    )(o, z, og, dy)


def _layer_fwd(l, h, p, weights):
    s = h.shape[0]
    tb = _tm_big(s)
    n = f"l{l}_"
    w_in = weights(l, "w_in", h)
    u = _rmsnorm_fwd(n + "norm_mix", h, p["g_mix"], l)
    z = _mm(n + "mm_in", u, w_in,
            pl.BlockSpec((tb, D), lambda j, i, k: (i, 0)), pl.BlockSpec((D, 1152), lambda j, i, k: (0, j)),
            jax.ShapeDtypeStruct((s, ZC), f32), pl.BlockSpec((tb, 1152), lambda j, i, k: (i, j)),
            (ZC // 1152, s // tb, 1), ((1,), (0,)), 1)
    w_out = weights(l, "w_out", z)
    ya, yb = _convpool_fwd(n + "convpool", z, p["conv_w"], p["pool_w"], p["pool_scale"], l)
    qe, kd, el, oi = _gla1_fwd(n + "gla_chunk", z, p["wdec"], p["bdec"], l)
    o, st = _gla2_fwd(n + "gla_scan", z, qe, kd, el, oi)
    yc = _gla3_fwd(n + "gla_out", o, z, p["gla_og"], l)
    fc, fr = _foxprep_fwd(n + "fox_prep", z, p["fox_bf"], l)
    yd = _fox_fwd(n + "fox_attn", z, fc, fr, p["fox_qg"], p["fox_kg"], l)
    y = jnp.concatenate([ya, yb, yc, yd], axis=1)
    w_gate, w_up, w_down = weights(l, "ffn", y)
    res_tile = lambda: pl.BlockSpec((tb, 1024), lambda j, i, k: (i, j))
    h1 = _mm(n + "mm_out", y, w_out.reshape(D, D),
             pl.BlockSpec((tb, D), lambda j, i, k: (i, 0)), pl.BlockSpec((D, 1024), lambda j, i, k: (0, j)),
             jax.ShapeDtypeStruct((s, D), f32), res_tile(), (2, s // tb, 1), ((1,), (0,)), 1, res=h, res_spec=res_tile())
    u2 = _rmsnorm_fwd(n + "norm_ffn", h1, p["g_ffn"], l)
    gate, up, act = _mm_gate_up(n + "mm_gate_up", u2, w_gate, w_up)
    h2 = _mm(n + "mm_down", act, w_down,
             pl.BlockSpec((tb, FB), lambda j, i, k: (i, k)),
             pl.BlockSpec((None, FB, 1024), lambda j, i, k: (k, 0, j)),
             jax.ShapeDtypeStruct((s, D), f32), res_tile(), (2, s // tb, NCHIP), ((1,), (0,)), NCHIP,
             res=h1, res_spec=res_tile())
    saved = dict(h=h, u=u, z=z, qe=qe, kd=kd, el=el, st=st, o=o, fc=fc, fr=fr, y=y, h1=h1, u2=u2,
                 gate=gate, up=up, act=act, w_in=w_in, w_out=w_out, w_gate=w_gate, w_up=w_up, w_down=w_down)
    return h2, saved


def _mm_tn(name, a, b, ta, tb, out_shape, out_spec, grid):
    s = a.shape[0]
    return _mm(name, a, b, pl.BlockSpec((s, ta), lambda i, j, k: (0, i)), pl.BlockSpec((s, tb), lambda i, j, k: (0, j)),
               out_shape, out_spec, grid, ((0,), (0,)), 1)


def _layer_bwd_ffn(l, dh2, dh2b, p, sv, ship):
    s = dh2.shape[0]
    tb = _tm_big(s)
    n = f"l{l}_bwd_"
    g_wd = _mm_tn(n + "mm_dwd", sv["act"], dh2b, FB, 1024, jax.ShapeDtypeStruct((NCHIP, FB, D), bf16),
                  pl.BlockSpec((None, FB, 1024), lambda i, j, k: (i, 0, j)), (NCHIP, 2, 1))
    dgate, dup = _mm_dact(n + "mm_dact", dh2b, sv["w_down"], sv["gate"], sv["up"])
    wg_shape = jax.ShapeDtypeStruct((NCHIP, D, FB), bf16)
    wg_spec = lambda: pl.BlockSpec((None, 1024, FB), lambda i, j, k: (j, i, 0))
    g_wg = _mm_tn(n + "mm_dwg", sv["u2"], dgate, 1024, FB, wg_shape, wg_spec(), (2, NCHIP, 1))
    g_wu = _mm_tn(n + "mm_dwu", sv["u2"], dup, 1024, FB, wg_shape, wg_spec(), (2, NCHIP, 1))
    token = ship.early(dict(w_gate=g_wg, w_up=g_wu, w_down=g_wd))
    nt_in = lambda: (pl.BlockSpec((tb, FB), lambda j, i, k: (i, k)),
                     pl.BlockSpec((None, 1024, FB), lambda j, i, k: (k, j, 0)))
    nt_out = lambda: (jax.ShapeDtypeStruct((s, D), f32), pl.BlockSpec((tb, 1024), lambda j, i, k: (i, j)))
    du2 = _mm(n + "mm_du2_gate", dgate, sv["w_gate"], *nt_in(), *nt_out(), (2, s // tb, NCHIP), ((1,), (1,)), NCHIP,
              dep=token)
    du2 = _mm(n + "mm_du2_up", dup, sv["w_up"], *nt_in(), *nt_out(), (2, s // tb, NCHIP), ((1,), (1,)), NCHIP,
              res=du2, res_spec=pl.BlockSpec((tb, 1024), lambda j, i, k: (i, j)))
    dh1, dh1b, dg_ffn = _rmsnorm_bwd(n + "norm_ffn", sv["h1"], p["g_ffn"], du2, dh2, l)
    g_wo = _mm_tn(n + "mm_dwo", sv["y"], dh1b, G, 1024, jax.ShapeDtypeStruct((NCHIP, G, D), bf16),
                  pl.BlockSpec((None, G, 1024), lambda i, j, k: (i, 0, j)), (NCHIP, 2, 1))
    token = ship.finish(dict(w_out=g_wo), g_wo)
    dy = _mm(n + "mm_dy", dh1b, sv["w_out"].reshape(D, D),
             pl.BlockSpec((tb, D), lambda j, i, k: (i, 0)), pl.BlockSpec((1024, D), lambda j, i, k: (j, 0)),
             jax.ShapeDtypeStruct((s, D), f32), pl.BlockSpec((tb, 1024), lambda j, i, k: (i, j)),
             (2, s // tb, 1), ((1,), (1,)), 1, dep=token)
    return dh1, dy, dg_ffn


def _layer_bwd_mix(l, dh1, dy, p, sv, ship):
    s = dh1.shape[0]
    n = f"l{l}_bwd_"
    z = sv["z"]
    dcb, dcc, dch, dpu, dconv, dpoolw, dpools = _convpool_bwd(
        n + "convpool", z, p["conv_w"], p["pool_w"], p["pool_scale"], dy, l)
    do, dgg, dog = _gla3_bwd(n + "gla_out", sv["o"], z, p["gla_og"], dy, l)
    dqe, dkd, dvi, del_ = _gla2_bwd(n + "gla_scan", z, sv["qe"], sv["kd"], sv["el"], sv["st"], do)
    dgq, dgk, dgv, dmisc4, dwd, dbd = _gla1_bwd(n + "gla_chunk", z, p["wdec"], p["bdec"], dqe, dkd, del_, do, dvi, l)
    dfq, dfk, dfv, dfc4, dfr4, dqg, dkg = _fox_bwd(n + "fox_attn", z, sv["fc"], sv["fr"], p["fox_qg"], p["fox_kg"], dy, l)
    dmisc, dbf = _foxprep_bwd(n + "fox_prep", z, p["fox_bf"], dfc4, dfr4, dmisc4, l)
    dz = jnp.concatenate([dcb, dcc, dch, dpu, dgq, dgk, dgv, dgg, dfq, dfk, dfv, dmisc], axis=1)
    g_wi = _mm_tn(n + "mm_dwi", sv["u"], dz, 1024, 1152, jax.ShapeDtypeStruct((D, ZC), bf16),
                  pl.BlockSpec((1024, 1152), lambda i, j, k: (i, j)), (2, ZC // 1152, 1))
    token = ship.early(dict(w_in=_win_to_blocks(g_wi)))
    tb = _tm_big(s)
    du = _mm(n + "mm_du", dz, sv["w_in"],
             pl.BlockSpec((tb, 1920), lambda j, i, k: (i, k)), pl.BlockSpec((1024, 1920), lambda j, i, k: (j, k)),
             jax.ShapeDtypeStruct((s, D), f32), pl.BlockSpec((tb, 1024), lambda j, i, k: (i, j)),
             (2, s // tb, ZC // 1920), ((1,), (1,)), ZC // 1920, dep=token)
    token = ship.finish({}, du)
    dh, dhb, dg_mix = _rmsnorm_bwd(n + "norm_mix", sv["h"], p["g_mix"], du, dh1, l, dep=token)
    small = dict(
        norm_mix_g=dg_mix[0], conv_w=dconv, pool_w=dpoolw, pool_scale=dpools[0],
        gla_w_decay=jnp.concatenate([dwd[hh, GA_LANE:GA_LANE + 16, :64] for hh in range(4)], axis=1),
        gla_b_decay=jnp.concatenate([dbd[hh, 0, :64] for hh in range(4)]),
        gla_out_g=jnp.sum(dog[:, 0, :], axis=0), fox_q_g=jnp.sum(dqg[:, 0, :], axis=0),
        fox_k_g=jnp.sum(dkg[:, 0, :], axis=0), fox_b_f=dbf[0, FF_LANE:FF_LANE + 4])
    return dh, dhb, small


def _win_from_blocks(wb):
    def cols(a, b):
        parts = []
        for kk in range(NCHIP):
            lo, hi = max(a, kk * WINB), min(b, (kk + 1) * WINB)
            if lo < hi:
                parts.append(wb[kk, :, lo - kk * WINB:hi - kk * WINB])
        return parts

    zeros = lambda w: [jnp.zeros((wb.shape[1], w), wb.dtype)]
    segs = cols(0, 2048)
    for hh in range(4):
        segs += cols(2048 + 64 * hh, 2112 + 64 * hh) + zeros(64)
    for hh in range(4):
        segs += cols(2304 + 64 * hh, 2368 + 64 * hh) + zeros(64)
    segs += cols(2560, 3584) + cols(3600, 5136)
    segs += cols(5136, 5140) + zeros(GA_LANE - 4) + cols(3584, 3600) + zeros(LANE - GA_LANE - 16)
    return jnp.concatenate(segs, axis=-1)


def _win_to_blocks(g):
    mb = MISC * LANE
    segs = [(0, 2048)] + [(GQ * LANE + LANE * hh, 64) for hh in range(4)] + [(GK * LANE + LANE * hh, 64) for hh in range(4)]
    segs += [(GV * LANE, 1024), (mb + GA_LANE, 16), (FQ * LANE, 1536), (mb + FF_LANE, 4)]
    blocks, at = [[] for _ in range(NCHIP)], 0
    for start, width in segs:
        while width > 0:
            take = min(width, (at // WINB + 1) * WINB - at)
            blocks[at // WINB].append(g[:, start:start + take])
            start, width, at = start + take, width - take, at + take
    return jnp.stack([jnp.concatenate(b, axis=1) for b in blocks])


def _place():
    x, y, c = lax.axis_index("x"), lax.axis_index("y"), lax.axis_index("c")
    chips = [(1 - x, y), (x, 1 - y), (1 - x, 1 - y)]
    return x, y, c, chips


def _allgather_small(name, v, dep):
    m_per, n = v.shape

    def body(x_ref, dep_ref, out_ref, send_sems, recv_sems, local_sem):
        x, y, c, chips = _place()
        me, sibling = (x, y, c), (x, y, 1 - c)

        def rows(px, py, pc):
            return out_ref.at[pl.ds((4 * px + 2 * py + pc) * m_per, m_per), :]

        def copy(k, block, to, src=None):
            return pltpu.make_async_remote_copy(
                src_ref=rows(*block) if src is None else src, dst_ref=rows(*block),
                send_sem=send_sems.at[k], recv_sem=recv_sems.at[k], device_id=to, device_id_type=MESH)

        mine = pltpu.make_async_copy(x_ref, rows(*me), local_sem)
        mine.start()
        first = [copy(0, me, sibling, src=x_ref)]
        first += [copy(1 + j, me, (*chip, c), src=x_ref) for j, chip in enumerate(chips)]
        for cp in first:
            cp.start()
        passed = [copy(4 + j, (*chip, c), sibling) for j, chip in enumerate(chips)]
        for j, chip in enumerate(chips):
            copy(1 + j, (*chip, c), me).wait_recv()
            passed[j].start()
        copy(0, sibling, me).wait_recv()
        for j, chip in enumerate(chips):
            copy(4 + j, (*chip, 1 - c), me).wait_recv()
        for cp in first + passed:
            cp.wait_send()
        mine.wait()

    return pl.pallas_call(
        body, name=name, out_shape=jax.ShapeDtypeStruct((8 * m_per, n), v.dtype),
        in_specs=[pl.BlockSpec(memory_space=pltpu.VMEM), pl.BlockSpec(memory_space=pl.ANY)],
        out_specs=pl.BlockSpec(memory_space=pltpu.VMEM),
        scratch_shapes=[pltpu.SemaphoreType.DMA((7,)), pltpu.SemaphoreType.DMA((7,)), pltpu.SemaphoreType.DMA],
    )(v, dep)


def _hbm_specs(n):
    return [pl.BlockSpec(memory_space=pl.ANY)] * n


def _own_slot(shard, chip):
    return lax.dynamic_update_index_in_dim(lax.empty((NCHIP,) + shard.shape, shard.dtype), shard, chip, 0)


HBM = pl.BlockSpec(memory_space=pltpu.HBM)
SEM = pl.BlockSpec(memory_space=pltpu.SEMAPHORE)
EFFECT = pltpu.SideEffectType.DATAFLOW_SIDE_EFFECTING


def _in_hbm(v):
    return pltpu.with_memory_space_constraint(v, pltpu.HBM)


def _gather_start(name, groups):
    flat = [b for g in groups for b in g]
    nt, ng = len(flat), len(groups)

    def body(*refs):
        outs = refs[nt:]
        sems, bufs, token = outs[:2 * ng], outs[2 * ng:2 * ng + nt], outs[2 * ng + nt]
        token[...] = jnp.zeros_like(token)
        x, y, c, chips = _place()
        me = 2 * x + y
        t = 0
        for gi, g in enumerate(groups):
            for k in range(len(g)):
                r = bufs[t].shape[1] // 2
                mine = bufs[t].at[me, pl.ds(c * r, r), :]
                for j, (px, py) in enumerate(chips):
                    pltpu.make_async_remote_copy(
                        src_ref=mine, dst_ref=mine, send_sem=sems[2 * gi].at[3 * k + j], recv_sem=sems[2 * gi + 1].at[3 * k + j],
                        device_id=(px, py, c), device_id_type=MESH).start()
                t += 1

    sem_shapes = []
    for g in groups:
        sem_shapes += [pltpu.SemaphoreType.DMA((3 * len(g),))] * 2
    out = pl.pallas_call(
        body, name=name,
        out_shape=tuple(sem_shapes + [pltpu.HBM(b.shape, b.dtype) for b in flat] + [jax.ShapeDtypeStruct((8, LANE), f32)]),
        in_specs=tuple([HBM] * nt),
        out_specs=tuple([SEM] * (2 * ng) + [HBM] * nt + [pl.BlockSpec(memory_space=pltpu.VMEM)]),
        input_output_aliases={t: 2 * ng + t for t in range(nt)},
        compiler_params=pltpu.CompilerParams(has_side_effects=EFFECT),
    )(*[_in_hbm(b) for b in flat])
    sems = [(out[2 * gi], out[2 * gi + 1]) for gi in range(ng)]
    bufs, at = [], 2 * ng
    for g in groups:
        bufs.append(list(out[at:at + len(g)]))
        at += len(g)
    return sems, bufs, out[at]


def _gather_wait(name, bufs, send_sems, recv_sems, after):
    nt = len(bufs)

    def body(*refs):
        ins, ss, rs = refs[:nt], refs[nt], refs[nt + 1]
        x, y, c, chips = _place()
        me = 2 * x + y
        for k in range(nt):
            r = ins[k].shape[1] // 2
            for j, (px, py) in enumerate(chips):
                cp = pltpu.make_async_remote_copy(
                    src_ref=ins[k].at[me, pl.ds(c * r, r), :], dst_ref=ins[k].at[2 * px + py, pl.ds(c * r, r), :],
                    send_sem=ss.at[3 * k + j], recv_sem=rs.at[3 * k + j], device_id=(px, py, c), device_id_type=MESH)
                cp.wait_send()
                cp.wait_recv()

    out = pl.pallas_call(
        body, name=name, out_shape=tuple(pltpu.HBM(b.shape, b.dtype) for b in bufs),
        in_specs=tuple([HBM] * nt + [SEM, SEM, pl.BlockSpec(memory_space=pl.ANY)]), out_specs=tuple([HBM] * nt),
        input_output_aliases={t: t for t in range(nt)},
        compiler_params=pltpu.CompilerParams(has_side_effects=EFFECT),
    )(*bufs, send_sems, recv_sems, after)
    return list(out)


def _gather_exchange(name, bufs):
    nt = len(bufs)

    def body(*refs):
        outs = refs[nt:2 * nt]
        send_sems, recv_sems = refs[2 * nt:]
        x, y, c, chips = _place()
        sibling = (x, y, 1 - c)

        def half(t, chip_idx, cc):
            r = outs[t].shape[1] // 2
            return outs[t].at[chip_idx, pl.ds(cc * r, r), :]

        sent = []
        for t in range(nt):
            for j, (px, py) in enumerate(chips):
                cp = pltpu.make_async_remote_copy(
                    src_ref=half(t, 2 * px + py, c), dst_ref=half(t, 2 * px + py, c),
                    send_sem=send_sems.at[t, j], recv_sem=recv_sems.at[t, j], device_id=sibling, device_id_type=MESH)
                cp.start()
                sent.append(cp)
        for t in range(nt):
            for j, (px, py) in enumerate(chips):
                pltpu.make_async_remote_copy(
                    src_ref=half(t, 2 * px + py, 1 - c), dst_ref=half(t, 2 * px + py, 1 - c),
                    send_sem=send_sems.at[t, j], recv_sem=recv_sems.at[t, j], device_id=sibling,
                    device_id_type=MESH).wait_recv()
        for cp in sent:
            cp.wait_send()

    return pl.pallas_call(
        body, name=name, out_shape=[jax.ShapeDtypeStruct(v.shape, v.dtype) for v in bufs],
        in_specs=_hbm_specs(nt), out_specs=_hbm_specs(nt), input_output_aliases={t: t for t in range(nt)},
        scratch_shapes=[pltpu.SemaphoreType.DMA((nt, 3)), pltpu.SemaphoreType.DMA((nt, 3))],
    )(*bufs)


def _rs_to_sibling(name, grads):
    nt = len(grads)

    def body(*refs):
        ins, outs = refs[:nt], refs[nt:2 * nt]
        send_sems, recv_sems = refs[2 * nt:]
        x, y, c, _ = _place()
        cps = []
        for t in range(nt):
            r = ins[t].shape[1] // 2
            cp = pltpu.make_async_remote_copy(
                src_ref=ins[t].at[:, pl.ds((1 - c) * r, r), :], dst_ref=outs[t],
                send_sem=send_sems.at[t], recv_sem=recv_sems.at[t], device_id=(x, y, 1 - c), device_id_type=MESH)
            cp.start()
            cps.append(cp)
        for cp in cps:
            cp.wait()

    return pl.pallas_call(
        body, name=name,
        out_shape=[jax.ShapeDtypeStruct((NCHIP, g.shape[1] // 2, g.shape[2]), g.dtype) for g in grads],
        in_specs=_hbm_specs(nt), out_specs=_hbm_specs(nt),
        scratch_shapes=[pltpu.SemaphoreType.DMA((nt,)), pltpu.SemaphoreType.DMA((nt,))],
    )(*grads)


def _sibling_start(name, grads):
    nt = len(grads)

    def body(*refs):
        outs = refs[2 * nt:]
        ss, rs, src, land, token = outs[0], outs[1], outs[2:2 + nt], outs[2 + nt:2 + 2 * nt], outs[2 + 2 * nt]
        x, y, c, _ = _place()
        for t in range(nt):
            r = src[t].shape[1] // 2
            pltpu.make_async_remote_copy(
                src_ref=src[t].at[:, pl.ds((1 - c) * r, r), :], dst_ref=land[t], send_sem=ss.at[t], recv_sem=rs.at[t],
                device_id=(x, y, 1 - c), device_id_type=MESH).start()
        token[...] = jnp.zeros_like(token)

    src_shapes = [pltpu.HBM(g.shape, g.dtype) for g in grads]
    land_shapes = [pltpu.HBM((NCHIP, g.shape[1] // 2, g.shape[2]), g.dtype) for g in grads]
    out = pl.pallas_call(
        body, name=name,
        out_shape=tuple([pltpu.SemaphoreType.DMA((nt,))] * 2 + src_shapes + land_shapes
                        + [jax.ShapeDtypeStruct((8, LANE), f32)]),
        in_specs=tuple([HBM] * (2 * nt)),
        out_specs=tuple([SEM, SEM] + [HBM] * (2 * nt) + [pl.BlockSpec(memory_space=pltpu.VMEM)]),
        input_output_aliases={t: 2 + t for t in range(2 * nt)},
        compiler_params=pltpu.CompilerParams(has_side_effects=EFFECT),
    )(*[_in_hbm(g) for g in grads], *[_in_hbm(lax.empty(s_.shape, s_.dtype)) for s_ in land_shapes])
    return out[0], out[1], list(out[2:2 + nt]), list(out[2 + nt:2 + 2 * nt]), out[2 + 2 * nt]


def _sibling_wait(name, grads, land, send_sems, recv_sems, after):
    nt = len(grads)

    def body(*refs):
        src, dst, ss, rs = refs[:nt], refs[nt:2 * nt], refs[2 * nt], refs[2 * nt + 1]
        x, y, c, _ = _place()
        for t in range(nt):
            r = src[t].shape[1] // 2
            cp = pltpu.make_async_remote_copy(
                src_ref=src[t].at[:, pl.ds((1 - c) * r, r), :], dst_ref=dst[t], send_sem=ss.at[t], recv_sem=rs.at[t],
                device_id=(x, y, 1 - c), device_id_type=MESH)
            cp.wait_send()
            cp.wait_recv()

    shapes = [pltpu.HBM(v.shape, v.dtype) for v in list(grads) + list(land)]
    out = pl.pallas_call(
        body, name=name, out_shape=tuple(shapes),
        in_specs=tuple([HBM] * (2 * nt) + [SEM, SEM, pl.BlockSpec(memory_space=pl.ANY)]),
        out_specs=tuple([HBM] * (2 * nt)), input_output_aliases={t: t for t in range(2 * nt)},
        compiler_params=pltpu.CompilerParams(has_side_effects=EFFECT),
    )(*grads, *land, send_sems, recv_sems, after)
    return list(out[:nt]), list(out[nt:])


def _rs_pair_sum(name, pos, g, other):
    r, cdim = other.shape[1], other.shape[2]
    tr = r // 4 if (r // 4) % 16 == 0 else r // 2
    nblk = r // tr

    def body(pos_ref, g_ref, o_ref, s_ref):
        s_ref[...] = (g_ref[...].astype(f32) + o_ref[...].astype(f32)).astype(bf16)

    blk = pl.BlockSpec((None, tr, cdim), lambda q, i, p: (q, i, 0))
    return pl.pallas_call(
        body, name=name, out_shape=jax.ShapeDtypeStruct(other.shape, bf16),
        grid_spec=pltpu.PrefetchScalarGridSpec(
            num_scalar_prefetch=1, grid=(NCHIP, nblk),
            in_specs=[pl.BlockSpec((None, tr, cdim), lambda q, i, p: (q, p[1] * nblk + i, 0)), blk], out_specs=blk),
        compiler_params=_cp(("parallel", "parallel")),
    )(pos, g, other)


def _scatter_start(name, sums):
    nt = len(sums)

    def body(*refs):
        outs = refs[2 * nt:]
        ss, rs, src, land, token = outs[0], outs[1], outs[2:2 + nt], outs[2 + nt:2 + 2 * nt], outs[2 + 2 * nt]
        x, y, c, chips = _place()
        me = 2 * x + y
        for t in range(nt):
            for j, (px, py) in enumerate(chips):
                pltpu.make_async_remote_copy(
                    src_ref=src[t].at[2 * px + py], dst_ref=land[t].at[me], send_sem=ss.at[3 * t + j], recv_sem=rs.at[3 * t + j],
                    device_id=(px, py, c), device_id_type=MESH).start()
        token[...] = jnp.zeros_like(token)

    shapes = [pltpu.HBM(v.shape, v.dtype) for v in sums]
    out = pl.pallas_call(
        body, name=name,
        out_shape=tuple([pltpu.SemaphoreType.DMA((3 * nt,))] * 2 + shapes + shapes + [jax.ShapeDtypeStruct((8, LANE), f32)]),
        in_specs=tuple([HBM] * (2 * nt)),
        out_specs=tuple([SEM, SEM] + [HBM] * (2 * nt) + [pl.BlockSpec(memory_space=pltpu.VMEM)]),
        input_output_aliases={t: 2 + t for t in range(2 * nt)},
        compiler_params=pltpu.CompilerParams(has_side_effects=EFFECT),
    )(*[_in_hbm(v) for v in sums], *[_in_hbm(lax.empty(v.shape, v.dtype)) for v in sums])
    return out[0], out[1], list(out[2:2 + nt]), list(out[2 + nt:2 + 2 * nt]), out[2 + 2 * nt]


def _scatter_wait(name, sums, land, send_sems, recv_sems, after):
    nt = len(sums)

    def body(*refs):
        src, dst, ss, rs = refs[:nt], refs[nt:2 * nt], refs[2 * nt], refs[2 * nt + 1]
        x, y, c, chips = _place()
        for t in range(nt):
            for j, (px, py) in enumerate(chips):
                cp = pltpu.make_async_remote_copy(
                    src_ref=src[t].at[2 * px + py], dst_ref=dst[t].at[2 * px + py], send_sem=ss.at[3 * t + j],
                    recv_sem=rs.at[3 * t + j], device_id=(px, py, c), device_id_type=MESH)
                cp.wait_send()
                cp.wait_recv()

    shapes = [pltpu.HBM(v.shape, v.dtype) for v in sums]
    out = pl.pallas_call(
        body, name=name, out_shape=tuple(shapes + shapes),
        in_specs=tuple([HBM] * (2 * nt) + [SEM, SEM, pl.BlockSpec(memory_space=pl.ANY)]),
        out_specs=tuple([HBM] * (2 * nt)), input_output_aliases={t: t for t in range(2 * nt)},
        compiler_params=pltpu.CompilerParams(has_side_effects=EFFECT),
    )(*sums, *land, send_sems, recv_sems, after)
    return list(out[:nt]), list(out[nt:])


def _rs_chip_sum(name, pos, sums, parts):
    r, cdim = parts.shape[1], parts.shape[2]
    tr = r // 4 if (r // 4) % 16 == 0 else r // 2
    nblk = r // tr

    def body(pos_ref, own_ref, a_ref, b_ref, c_ref, o_ref):
        o_ref[...] = ((own_ref[...].astype(f32) + a_ref[...].astype(f32)) + b_ref[...].astype(f32)) \
            + c_ref[...].astype(f32)

    def slot(k):
        return pl.BlockSpec((None, tr, cdim), lambda i, p: ((p[0] + k) % NCHIP, i, 0))

    return pl.pallas_call(
        body, name=name, out_shape=jax.ShapeDtypeStruct((2 * r, cdim), f32),
        grid_spec=pltpu.PrefetchScalarGridSpec(
            num_scalar_prefetch=1, grid=(nblk,), in_specs=[slot(0), slot(1), slot(2), slot(3)],
            out_specs=pl.BlockSpec((tr, cdim), lambda i, p: (p[1] * nblk + i, 0))),
        compiler_params=_cp(("parallel",)),
    )(pos, sums, parts, parts, parts)


def _rs_share_halves(name, bufs):
    nt = len(bufs)

    def body(*refs):
        outs = refs[nt:2 * nt]
        send_sems, recv_sems = refs[2 * nt:]
        x, y, c, _ = _place()
        cps = []
        for t in range(nt):
            r = outs[t].shape[0] // 2
            mine = outs[t].at[pl.ds(c * r, r), :]
            theirs = outs[t].at[pl.ds((1 - c) * r, r), :]
            cp = pltpu.make_async_remote_copy(
                src_ref=mine, dst_ref=mine, send_sem=send_sems.at[t], recv_sem=recv_sems.at[t],
                device_id=(x, y, 1 - c), device_id_type=MESH)
            cp.start()
            cps.append((cp, theirs))
        for t, (cp, theirs) in enumerate(cps):
            pltpu.make_async_remote_copy(
                src_ref=theirs, dst_ref=theirs, send_sem=send_sems.at[t], recv_sem=recv_sems.at[t],
                device_id=(x, y, 1 - c), device_id_type=MESH).wait_recv()
            cp.wait_send()

    return pl.pallas_call(
        body, name=name, out_shape=[jax.ShapeDtypeStruct(v.shape, v.dtype) for v in bufs],
        in_specs=_hbm_specs(nt), out_specs=_hbm_specs(nt), input_output_aliases={t: t for t in range(nt)},
        scratch_shapes=[pltpu.SemaphoreType.DMA((nt,)), pltpu.SemaphoreType.DMA((nt,))],
    )(*bufs)


BIG = ("w_in", "w_out", "w_gate", "w_up", "w_down")


GROUPS = (("w_in",), ("w_out",), ("w_gate", "w_up", "w_down"))


class _Ship:
    def __init__(self, tag, pos):
        self.tag, self.pos, self.started, self.state = tag, pos, None, None

    def early(self, grads):
        self.early_keys = tuple(grads)
        self.started = _sibling_start(self.tag + "sibling_start", [grads[k] for k in self.early_keys])
        return self.started[-1]

    def finish(self, grads, after):
        keys, mine, got = tuple(grads), [grads[k] for k in grads], []
        if keys:
            got = list(_rs_to_sibling(self.tag + "to_sibling", mine))
        if self.started is not None:
            send_sems, recv_sems, src, land, _ = self.started
            src, land = _sibling_wait(self.tag + "sibling_wait", src, land, send_sems, recv_sems, after)
            keys, mine, got = self.early_keys + keys, src + mine, land + got
        sums = [_rs_pair_sum(self.tag + "pair_sum_" + k, self.pos, g, o) for k, g, o in zip(keys, mine, got)]
        self.state = (keys,) + _scatter_start(self.tag + "start", sums)
        return self.state[-1]


def _rs_end(tag, state, pos, after):
    keys, send_sems, recv_sems, sums, land, _ = state
    sums, land = _scatter_wait(tag + "wait", sums, land, send_sems, recv_sems, after)
    halves = [_rs_chip_sum(tag + "chip_sum_" + k, pos, s, v) for k, s, v in zip(keys, sums, land)]
    return dict(zip(keys, _rs_share_halves(tag + "share", halves)))


def _adam_math(w, g, m, v):
    m = ADAM_B1 * m + (1.0 - ADAM_B1) * g
    v = ADAM_B2 * v + (1.0 - ADAM_B2) * (g * g)
    m_hat = m / (1.0 - ADAM_B1 ** ADAM_STEP)
    v_hat = v / (1.0 - ADAM_B2 ** ADAM_STEP)
    delta = -ADAM_LR * (m_hat / (jnp.sqrt(v_hat) + ADAM_EPS) + ADAM_WD * w)
    return delta, m, v


def _adam_big(name, g0, g1, w, m, v):
    _, r, cdim = w.shape
    tr = 128 if r % 128 == 0 else 64
    nb = r // tr

    def body(g0_ref, g1_ref, w_ref, m_ref, v_ref, go_ref, d_ref, mo_ref, vo_ref):
        l = pl.program_id(0)
        g = jnp.where(l == 0, g0_ref[...], g1_ref[...])
        delta, mn, vn = _adam_math(w_ref[...], g, m_ref[...], v_ref[...])
        go_ref[...] = g
        d_ref[...] = delta
        mo_ref[...] = mn
        vo_ref[...] = vn

    lay = pl.BlockSpec((None, tr, cdim), lambda l, i: (l, i, 0))
    return pl.pallas_call(
        body, name=name, grid=(2, nb),
        in_specs=[pl.BlockSpec((tr, cdim), lambda l, i: (i * (1 - l) + (nb - 1) * l, 0)),
                  pl.BlockSpec((tr, cdim), lambda l, i: (i * l, 0)), lay, lay, lay],
        out_specs=[lay] * 4, out_shape=[jax.ShapeDtypeStruct(w.shape, f32)] * 4,
        compiler_params=_cp(("arbitrary", "arbitrary")),
    )(g0, g1, w, m, v)


def _adam_w_in(name, g0, g1, w, m, v):
    wt, mt, vt = (jnp.transpose(a, (2, 0, 1)) for a in (w, m, v))
    cols, _, rows = wt.shape

    def body(g0_ref, g1_ref, w_ref, m_ref, v_ref, go_ref, d_ref, mo_ref, vo_ref):
        for l, g_ref in enumerate((g0_ref, g1_ref)):
            g = jnp.transpose(g_ref[...])
            delta, mn, vn = _adam_math(w_ref[:, l, :], g, m_ref[:, l, :], v_ref[:, l, :])
            go_ref[:, l, :] = g
            d_ref[:, l, :] = delta
            mo_ref[:, l, :] = mn
            vo_ref[:, l, :] = vn

    blk = pl.BlockSpec((LANE, 2, rows), lambda i: (i, 0, 0))
    gblk = pl.BlockSpec((rows, LANE), lambda i: (0, i))
    outs = pl.pallas_call(
        body, name=name, grid=(pl.cdiv(cols, LANE),), in_specs=[gblk, gblk, blk, blk, blk], out_specs=[blk] * 4,
        out_shape=[jax.ShapeDtypeStruct(wt.shape, f32)] * 4, compiler_params=_cp(("parallel",)),
    )(g0, g1, wt, mt, vt)
    return tuple(jnp.transpose(o, (1, 2, 0)) for o in outs)


def _sum8(name, gathered):
    m_per = gathered.shape[0] // 8

    def body(g_ref, o_ref):
        tot = g_ref[pl.ds(0, m_per), :]
        for d in range(1, 8):
            tot = tot + g_ref[pl.ds(d * m_per, m_per), :]
        o_ref[...] = tot

    return pl.pallas_call(body, name=name, out_shape=jax.ShapeDtypeStruct((m_per, LANE), f32))(gathered)


def _adam_small(name, g, w, m, v):
    def body(g_ref, w_ref, m_ref, v_ref, d_ref, mo_ref, vo_ref):
        delta, mn, vn = _adam_math(w_ref[...], g_ref[...], m_ref[...], v_ref[...])
        d_ref[...] = delta
        mo_ref[...] = mn
        vo_ref[...] = vn

    return pl.pallas_call(body, name=name, out_shape=[jax.ShapeDtypeStruct(g.shape, f32)] * 3)(g, w, m, v)


def _pack(vals):
    rows, offs, at = [], [], 0
    for a in vals:
        a = a.reshape(-1)
        n = -(-a.shape[0] // (8 * LANE)) * 8
        rows.append(jnp.pad(a, (0, n * LANE - a.shape[0])).reshape(n, LANE))
        offs.append(at)
        at += n
    return jnp.concatenate(rows, axis=0), offs


def _unpack(packed, offs, shapes):
    out = []
    for o, shp in zip(offs, shapes):
        size = 1
        for d in shp:
            size *= d
        n = -(-size // LANE)
        out.append(packed[o:o + n].reshape(-1)[:size].reshape(shp))
    return out


SMALL = ("norm_mix_g", "conv_w", "pool_w", "pool_scale", "gla_w_decay", "gla_b_decay", "gla_out_g",
         "fox_q_g", "fox_k_g", "fox_b_f", "norm_ffn_g")
ALL = ("norm_mix_g", "w_in", "conv_w", "pool_w", "pool_scale", "gla_w_decay", "gla_b_decay", "gla_out_g",
       "fox_q_g", "fox_k_g", "fox_b_f", "w_out", "norm_ffn_g", "w_gate", "w_up", "w_down")


def kernel(x, norm_mix_g, w_in, conv_w, pool_w, pool_scale, gla_w_decay, gla_b_decay, gla_out_g, fox_q_g, fox_k_g, fox_b_f, w_out, norm_ffn_g, w_gate, w_up, w_down, loss_target, m_norm_mix_g, m_w_in, m_conv_w, m_pool_w, m_pool_scale, m_gla_w_decay, m_gla_b_decay, m_gla_out_g, m_fox_q_g, m_fox_k_g, m_fox_b_f, m_w_out, m_norm_ffn_g, m_w_gate, m_w_up, m_w_down, v_norm_mix_g, v_w_in, v_conv_w, v_pool_w, v_pool_scale, v_gla_w_decay, v_gla_b_decay, v_gla_out_g, v_fox_q_g, v_fox_k_g, v_fox_b_f, v_w_out, v_norm_ffn_g, v_w_gate, v_w_up, v_w_down):
    w = dict(norm_mix_g=norm_mix_g, w_in=w_in, conv_w=conv_w, pool_w=pool_w, pool_scale=pool_scale,
             gla_w_decay=gla_w_decay, gla_b_decay=gla_b_decay, gla_out_g=gla_out_g, fox_q_g=fox_q_g, fox_k_g=fox_k_g,
             fox_b_f=fox_b_f, w_out=w_out, norm_ffn_g=norm_ffn_g, w_gate=w_gate, w_up=w_up, w_down=w_down)
    m = dict(norm_mix_g=m_norm_mix_g, w_in=m_w_in, conv_w=m_conv_w, pool_w=m_pool_w, pool_scale=m_pool_scale,
             gla_w_decay=m_gla_w_decay, gla_b_decay=m_gla_b_decay, gla_out_g=m_gla_out_g, fox_q_g=m_fox_q_g,
             fox_k_g=m_fox_k_g, fox_b_f=m_fox_b_f, w_out=m_w_out, norm_ffn_g=m_norm_ffn_g, w_gate=m_w_gate,
             w_up=m_w_up, w_down=m_w_down)
    v = dict(norm_mix_g=v_norm_mix_g, w_in=v_w_in, conv_w=v_conv_w, pool_w=v_pool_w, pool_scale=v_pool_scale,
             gla_w_decay=v_gla_w_decay, gla_b_decay=v_gla_b_decay, gla_out_g=v_gla_out_g, fox_q_g=v_fox_q_g,
             fox_k_g=v_fox_k_g, fox_b_f=v_fox_b_f, w_out=v_w_out, norm_ffn_g=v_norm_ffn_g, w_gate=v_w_gate,
             w_up=v_w_up, w_down=v_w_down)
    chip = 2 * lax.axis_index("x") + lax.axis_index("y")

    pos = jnp.stack([chip, lax.axis_index("c")]).astype(jnp.int32)

    mine, offs = _pack([conv_w, gla_w_decay, jnp.zeros((8, LANE), f32)])
    slots = lambda l: [[_own_slot(w[k][l].astype(bf16), chip) for k in grp] for grp in GROUPS]
    sems, gbufs, _ = _gather_start("gather_start", [[_own_slot(mine, chip)]] + slots(0))
    sems1, gbufs1, started1 = _gather_start("l1_gather_start", slots(1))
    sems, gbufs = sems + sems1, gbufs + gbufs1

    def gathered(tag, gi, after):
        got = _gather_wait(tag + "_wait", gbufs[gi], sems[gi][0], sems[gi][1], after)
        return _gather_exchange(tag + "_exchange", got)

    def weights(l, group, after):
        if (l, group) == (0, "w_in"):
            after = started1
        got = gathered(f"l{l}_gather_{group}", 1 + 3 * l + ("w_in", "w_out", "ffn").index(group), after)
        if group == "w_in":
            return _win_from_blocks(got[0])
        return got[0] if group == "w_out" else got

    every = gathered("gather_small", 0, x)[0]
    per_chip = [_unpack(every[kk], offs, [conv_w.shape, gla_w_decay.shape]) for kk in range(NCHIP)]
    conv_full = jnp.concatenate([pc[0] for pc in per_chip], axis=-1)[:, :, 0, :]
    wdec_full = jnp.concatenate([pc[1] for pc in per_chip], axis=-1)

    wdec = jnp.pad(wdec_full.reshape(2, 16, 4, 64), ((0, 0), (GA_LANE, LANE - GA_LANE - 16), (0, 0), (0, 64)))
    p = dict(
        g_mix=norm_mix_g[:, None, :], g_ffn=norm_ffn_g[:, None, :],
        conv_w=conv_full, pool_w=pool_w, pool_scale=pool_scale[:, None, :],
        wdec=wdec.reshape(2, LANE, G),
        bdec=jnp.pad(gla_b_decay.reshape(2, 4, 64), ((0, 0), (0, 0), (0, 64))).reshape(2, 1, G),
        gla_og=gla_out_g[:, None, :], fox_qg=fox_q_g[:, None, :], fox_kg=fox_k_g[:, None, :],
        fox_bf=jnp.pad(fox_b_f, ((0, 0), (FF_LANE, LANE - FF_LANE - 4)))[:, None, :])

    h0 = x[0]
    h1, sv0 = _layer_fwd(0, h0, p, weights)
    h2, sv1 = _layer_fwd(1, h1, p, weights)
    sq, dh, dhb = _loss("loss", h2, loss_target[0])
    loss = lax.psum(sq[0, 0] * (0.5 / D), ("x", "y", "c"))

    late = ("w_out", "w_gate", "w_up", "w_down")
    ships = {tag: _Ship(tag, pos) for tag in ("l1_rs_a_", "l1_rs_b_", "l0_rs_a_", "l0_rs_b_")}
    dh1, dy, dgf1 = _layer_bwd_ffn(1, dh, dhb, p, sv1, ships["l1_rs_a_"])
    dh, dhb, small1 = _layer_bwd_mix(1, dh1, dy, p, sv1, ships["l1_rs_b_"])
    dh1, dy, dgf0 = _layer_bwd_ffn(0, dh, dhb, p, sv0, ships["l0_rs_a_"])
    dh, dhb, small0 = _layer_bwd_mix(0, dh1, dy, p, sv0, ships["l0_rs_b_"])
    small0["norm_ffn_g"], small1["norm_ffn_g"] = dgf0[0], dgf1[0]

    red1 = _rs_end("l1_rs_a_", ships["l1_rs_a_"].state, pos, dh)
    red1.update(_rs_end("l1_rs_b_", ships["l1_rs_b_"].state, pos, red1["w_down"]))
    red0 = _rs_end("l0_rs_a_", ships["l0_rs_a_"].state, pos, red1["w_in"])
    grads, deltas, new_m, new_v = {}, {}, {}, {}
    for k in late:
        grads[k], deltas[k], new_m[k], new_v[k] = _adam_big("adam_" + k, red0[k], red1[k], w[k], m[k], v[k])

    packed, goffs = _pack([jnp.stack([small0[k], small1[k]]) for k in SMALL])
    total = _sum8("sum_small_grads", _allgather_small("gather_small_grads", packed, new_v["w_down"]))
    red0.update(_rs_end("l0_rs_b_", ships["l0_rs_b_"].state, pos, total))
    k = "w_in"
    grads[k], deltas[k], new_m[k], new_v[k] = _adam_w_in("adam_" + k, red0[k], red1[k], w[k], m[k], v[k])
    full_shapes = [(2,) + small0[k].shape for k in SMALL]
    gsmall = dict(zip(SMALL, _unpack(total, goffs, full_shapes)))
    gsmall["conv_w"] = lax.dynamic_slice_in_dim(gsmall["conv_w"], chip * LANE, LANE, axis=2)[:, :, None, :]
    gsmall["gla_w_decay"] = lax.dynamic_slice_in_dim(gsmall["gla_w_decay"], chip * 64, 64, axis=2)
    gp, loffs = _pack([gsmall[k] for k in SMALL])
    wp, _ = _pack([w[k] for k in SMALL])
    mp, _ = _pack([m[k] for k in SMALL])
    vp, _ = _pack([v[k] for k in SMALL])
    dp, mnp, vnp = _adam_small("adam_small", gp, wp, mp, vp)
    shapes = [w[k].shape for k in SMALL]
    for k, a, b, c_, d_ in zip(SMALL, _unpack(gp, loffs, shapes), _unpack(dp, loffs, shapes),
                               _unpack(mnp, loffs, shapes), _unpack(vnp, loffs, shapes)):
        grads[k], deltas[k], new_m[k], new_v[k] = a, b, c_, d_

    return (loss, dh[None], *[grads[k] for k in ALL], *[deltas[k] for k in ALL],
            *[new_m[k] for k in ALL], *[new_v[k] for k in ALL])
```

```python
import functools

import jax
import jax.numpy as jnp
from jax import lax
from jax.experimental import pallas as pl
from jax.experimental.pallas import tpu as pltpu

f32 = jnp.float32
bf16 = jnp.bfloat16

D = 2048
G = 512
DFF = 5632
NCHIP = 4
FB = DFF // NCHIP
WIN = 5140
WINB = WIN // NCHIP
EPS = 1e-6
CHUNK = 64
LANE = 128

CB, CC, CH, PU, GQ, GK, GV, GG, FQ, FK, FV, MISC = 0, 4, 8, 12, 16, 20, 24, 28, 32, 36, 40, 44
ZC = 45 * LANE
FF_LANE = 0
GA_LANE = 8

ADAM_LR, ADAM_B1, ADAM_B2, ADAM_EPS, ADAM_WD, ADAM_STEP = 0.001, 0.9, 0.999, 1e-08, 0.01, 10

VMEM_LIMIT = 60 * 1024 * 1024
MESH = pl.DeviceIdType.MESH


def _cp(sem=None):
    return pltpu.CompilerParams(dimension_semantics=sem, vmem_limit_bytes=VMEM_LIMIT)


def _dot(a, b, dims=((1,), (0,))):
    return lax.dot_general(a.astype(bf16), b.astype(bf16), (dims, ((), ())), preferred_element_type=f32)


def _bdot(a, b, ca, cb):
    return lax.dot_general(a.astype(bf16), b.astype(bf16), (((ca,), (cb,)), ((0,), (0,))),
                           preferred_element_type=f32)


def _log_sigmoid(x):
    return jnp.minimum(x, 0.0) - jnp.log(1.0 + jnp.exp(-jnp.abs(x)))


@jax.custom_vjp
def _sigmoid(x):
    return 1.0 / (1.0 + jnp.exp(-x))


def _sigmoid_fwd(x):
    s = _sigmoid(x)
    return s, s


def _sigmoid_bwd(s, g):
    return (g * s * (1.0 - s),)


_sigmoid.defvjp(_sigmoid_fwd, _sigmoid_bwd)


def _rms(x, g):
    return x * lax.rsqrt(jnp.mean(x * x, axis=-1, keepdims=True) + EPS) * g


def _shift_impl(x, n, period, transpose):
    rows = x.shape[0]
    t = lax.broadcasted_iota(jnp.int32, x.shape, 0)
    if period is not None:
        t = t & (period - 1)
    keep = t >= n
    if not transpose:
        return jnp.where(keep, pltpu.roll(x, n, 0), 0.0)
    return pltpu.roll(jnp.where(keep, x, 0.0), rows - n, 0)


def _shift(x, n, period=None):
    @jax.custom_vjp
    def f(v):
        return _shift_impl(v, n, period, False)

    def fwd(v):
        return f(v), None

    def bwd(_, g):
        return (_shift_impl(g, n, period, True),)

    f.defvjp(fwd, bwd)
    return f(x)


def _cumsum_rows(x, length, period=None):
    n = 1
    while n < length:
        x = x + _shift(x, n, period)
        n *= 2
    return x


def _convpool_fn(cb, cc, ch, pu, w0, w1, w2, pw, ps, j):
    u = cc * ch
    y = w2 * u + w1 * _shift(u, 1) + w0 * _shift(u, 2)
    ya = cb * y
    s2 = pu + _shift(pu, 1)
    s4 = s2 + _shift(s2, 2)
    s8 = s4 + _shift(s4, 4)
    s16 = s8 + _shift(s8, 8)
    wsum = jnp.where(j == 0, s2, jnp.where(j == 1, s4, jnp.where(j == 2, s8, s16)))
    width = (2 << j).astype(f32)
    t = lax.broadcasted_iota(jnp.int32, pu.shape, 0).astype(f32)
    count = jnp.minimum(t + 1.0, width)
    d = wsum / count - pu
    yb = _dot(d, pw) * ps
    return ya, yb


def _foxprep_fn(misc, bf):
    lf = _log_sigmoid(misc + bf)
    fc = _cumsum_rows(lf, lf.shape[0])
    return fc, jnp.transpose(fc)


def _fox_fn(q, k, v, fcol, frow8, qg, kg, h, i):
    tq, s = q.shape[0], k.shape[0]
    qn = _rms(q, qg)
    kn = _rms(k, kg)
    lg = _dot(qn, kn, ((1,), (1,))) * (LANE ** -0.5)
    lane = lax.broadcasted_iota(jnp.int32, fcol.shape, 1)
    fq = jnp.sum(jnp.where(lane == h, fcol, 0.0), axis=1, keepdims=True)
    row = lax.broadcasted_iota(jnp.int32, frow8.shape, 0)
    fk = jnp.sum(jnp.where(row == h, frow8, 0.0), axis=0, keepdims=True)
    lg = lg + fq - fk
    qpos = i * tq + lax.broadcasted_iota(jnp.int32, (tq, s), 0)
    kpos = lax.broadcasted_iota(jnp.int32, (tq, s), 1)
    lg = jnp.where(kpos <= qpos, lg, -jnp.inf)
    m = lax.stop_gradient(jnp.max(lg, axis=1, keepdims=True))
    e = jnp.exp(lg - m)
    p = e / jnp.sum(e, axis=1, keepdims=True)
    return _dot(p, v)


def _gla1_fn(q, k, v, misc, wd, bd):
    ts = q.shape[0]
    nb = ts // CHUNK
    x = _dot(misc, wd) + bd
    la = _log_sigmoid(x) * (1.0 / 16.0)
    cc = _cumsum_rows(la, CHUNK, CHUNK)
    la3 = la.reshape(nb, CHUNK, LANE)
    last3 = jnp.sum(la3, axis=1, keepdims=True)
    last2 = jnp.sum(la3, axis=1)
    cc3 = cc.reshape(nb, CHUNK, LANE)
    q3 = (q * 0.125).reshape(nb, CHUNK, LANE)
    k3 = k.reshape(nb, CHUNK, LANE)
    v3 = v.reshape(nb, CHUNK, LANE)
    ep = jnp.exp(cc3)
    en = jnp.exp(-cc3)
    qe = q3 * ep
    a1 = _bdot(qe, k3 * en, 2, 2)
    a2 = _bdot(q3 * en, k3 * ep, 2, 2)
    ti = lax.broadcasted_iota(jnp.int32, a1.shape, 1)
    si = lax.broadcasted_iota(jnp.int32, a1.shape, 2)
    sc = jnp.where(si <= ti, a1, a2)
    oi = _bdot(sc, v3, 2, 1)
    kd = k3 * jnp.exp(last3 - cc3)
    el = jnp.exp(last2)
    return qe.reshape(ts, LANE), kd.reshape(ts, LANE), el, oi.reshape(ts, LANE)


def _gla3_fn(o, gg, og):
    return _rms(o, og) * (gg * _sigmoid(gg))


def _ffn_fn(gate, up):
    return gate * _sigmoid(gate) * up


def _mm(name, a, b, a_spec, b_spec, out_shape, out_spec, grid, dims, nk, res=None, res_spec=None, dep=None):
    has_res = res is not None
    has_dep = dep is not None
    nax = len(grid)

    def body(*refs):
        a_ref, b_ref = refs[0], refs[1]
        res_ref = refs[2] if has_res else None
        out_ref = refs[2 + has_res + has_dep]
        part = _dot(a_ref[...], b_ref[...], dims)
        if nk == 1:
            if has_res:
                part = part + res_ref[...]
            out_ref[...] = part.astype(out_ref.dtype)
            return
        acc_ref = refs[3 + has_res + has_dep]
        k = pl.program_id(nax - 1)

        @pl.when(k == 0)
        def _():
            acc_ref[...] = part

        @pl.when(k > 0)
        def _():
            acc_ref[...] += part

        @pl.when(k == nk - 1)
        def _():
            tot = acc_ref[...]
            if has_res:
                tot = tot + res_ref[...]
            out_ref[...] = tot.astype(out_ref.dtype)

    ops = [a, b] + ([res] if has_res else []) + ([dep] if has_dep else [])
    specs = [a_spec, b_spec] + ([res_spec] if has_res else [])
    if has_dep:
        specs.append(pl.BlockSpec((8, LANE), lambda *_: (0, 0)))
    blk = tuple(d for d in out_spec.block_shape if d is not None)
    scratch = [pltpu.VMEM(blk, f32)] if nk > 1 else []
    return pl.pallas_call(
        body, name=name, grid=grid, in_specs=specs, out_specs=out_spec, out_shape=out_shape,
        scratch_shapes=scratch,
        compiler_params=_cp(("parallel",) * (nax - 1) + ("arbitrary",)),
    )(*ops)


def _tm(s):
    return min(s, 512)


def _tm_big(s):
    return min(s, 1024)


def _mm_gate_up(name, u, w_gate, w_up):
    s = u.shape[0]
    tm = _tm(s)

    def body(u_ref, wg_ref, wu_ref, g_ref, up_ref, a_ref):
        g = _dot(u_ref[...], wg_ref[...])
        up = _dot(u_ref[...], wu_ref[...])
        g_ref[...] = g
        up_ref[...] = up
        a_ref[...] = _ffn_fn(g, up).astype(bf16)

    wspec = pl.BlockSpec((None, D, FB), lambda j, i: (j, 0, 0))
    tile = pl.BlockSpec((tm, FB), lambda j, i: (i, j))
    return pl.pallas_call(
        body, name=name, grid=(NCHIP, s // tm),
        in_specs=[pl.BlockSpec((tm, D), lambda j, i: (i, 0)), wspec, wspec], out_specs=[tile, tile, tile],
        out_shape=[jax.ShapeDtypeStruct((s, DFF), f32), jax.ShapeDtypeStruct((s, DFF), f32),
                   jax.ShapeDtypeStruct((s, DFF), bf16)],
        compiler_params=_cp(("parallel", "parallel")),
    )(u, w_gate, w_up)


def _mm_dact(name, dh, w_down, gate, up):
    s = dh.shape[0]
    tm = _tm(s)

    def body(dh_ref, wd_ref, g_ref, up_ref, dg_ref, du_ref):
        dact = _dot(dh_ref[...], wd_ref[...], ((1,), (1,)))
        _, vjp = jax.vjp(_ffn_fn, g_ref[...], up_ref[...])
        dg, du = vjp(dact)
        dg_ref[...] = dg.astype(bf16)
        du_ref[...] = du.astype(bf16)

    tile = pl.BlockSpec((tm, FB), lambda j, i: (i, j))
    return pl.pallas_call(
        body, name=name, grid=(NCHIP, s // tm),
        in_specs=[pl.BlockSpec((tm, D), lambda j, i: (i, 0)), pl.BlockSpec((None, FB, D), lambda j, i: (j, 0, 0)),
                  tile, tile],
        out_specs=[tile, tile], out_shape=[jax.ShapeDtypeStruct((s, DFF), bf16)] * 2,
        compiler_params=_cp(("parallel", "parallel")),
    )(dh, w_down, gate, up)


def _rmsnorm_fwd(name, x, g, l):
    s = x.shape[0]
    tm = min(s, 256)

    def body(x_ref, g_ref, u_ref):
        u_ref[...] = _rms(x_ref[...], g_ref[...]).astype(bf16)

    return pl.pallas_call(
        body, name=name, grid=(s // tm,),
        in_specs=[pl.BlockSpec((tm, D), lambda i: (i, 0)), pl.BlockSpec((None, 1, D), lambda i: (l, 0, 0))],
        out_specs=pl.BlockSpec((tm, D), lambda i: (i, 0)),
        out_shape=jax.ShapeDtypeStruct((s, D), bf16), compiler_params=_cp(("parallel",)),
    )(x, g)


def _rmsnorm_bwd(name, x, g, du, dres, l, dep=None):
    s = x.shape[0]
    tm = min(s, 256)

    def body(x_ref, g_ref, du_ref, dres_ref, *rest):
        dx_ref, dxb_ref, dg_ref = rest[-3:]
        _, vjp = jax.vjp(_rms, x_ref[...], g_ref[...])
        dx, dg = vjp(du_ref[...])
        tot = dx + dres_ref[...]
        dx_ref[...] = tot
        dxb_ref[...] = tot.astype(bf16)

        @pl.when(pl.program_id(0) == 0)
        def _():
            dg_ref[...] = dg

        @pl.when(pl.program_id(0) > 0)
        def _():
            dg_ref[...] += dg

    row = pl.BlockSpec((tm, D), lambda i: (i, 0))
    deps = [] if dep is None else [dep]
    return pl.pallas_call(
        body, name=name, grid=(s // tm,),
        in_specs=[row, pl.BlockSpec((None, 1, D), lambda i: (l, 0, 0)), row, row]
        + [pl.BlockSpec((8, LANE), lambda i: (0, 0)) for _ in deps],
        out_specs=[row, row, pl.BlockSpec((1, D), lambda i: (0, 0))],
        out_shape=[jax.ShapeDtypeStruct((s, D), f32), jax.ShapeDtypeStruct((s, D), bf16),
                   jax.ShapeDtypeStruct((1, D), f32)],
        compiler_params=_cp(("arbitrary",)),
    )(x, g, du, dres, *deps)


def _loss(name, y, t):
    s = y.shape[0]
    tm = min(s, 256)
    row = pl.BlockSpec((tm, D), lambda i: (i, 0))

    def body(y_ref, t_ref, l_ref, d_ref, db_ref):
        e = y_ref[...] - t_ref[...]
        d = e * (1.0 / D)
        d_ref[...] = d
        db_ref[...] = d.astype(bf16)
        part = jnp.zeros((8, LANE), f32) + jnp.sum(e * e)

        @pl.when(pl.program_id(0) == 0)
        def _():
            l_ref[...] = part

        @pl.when(pl.program_id(0) > 0)
        def _():
            l_ref[...] += part

    return pl.pallas_call(
        body, name=name, grid=(s // tm,), in_specs=[row, row],
        out_specs=[pl.BlockSpec((8, LANE), lambda i: (0, 0)), row, row],
        out_shape=[jax.ShapeDtypeStruct((8, LANE), f32), jax.ShapeDtypeStruct((s, D), f32),
                   jax.ShapeDtypeStruct((s, D), bf16)],
        compiler_params=_cp(("arbitrary",)),
    )(y, t)


def _zspec(s, blk):
    return pl.BlockSpec((s, LANE), lambda j: (0, blk + j))


def _convpool_specs(s, l):
    return [_zspec(s, CB), _zspec(s, CC), _zspec(s, CH), _zspec(s, PU),
            pl.BlockSpec((None, 3, LANE), lambda j: (l, 0, j)),
            pl.BlockSpec((None, None, LANE, LANE), lambda j: (l, j, 0, 0)),
            pl.BlockSpec((None, 1, LANE), lambda j: (l, 0, j))]


def _convpool_fwd(name, z, conv_w, pool_w, pool_scale, l):
    s = z.shape[0]

    def body(cb, cc, ch, pu, cw, pw, ps, ya_ref, yb_ref):
        ya, yb = _convpool_fn(cb[...], cc[...], ch[...], pu[...], cw[0:1, :], cw[1:2, :], cw[2:3, :], pw[...], ps[...],
                              pl.program_id(0))
        ya_ref[...] = ya.astype(bf16)
        yb_ref[...] = yb.astype(bf16)

    col = pl.BlockSpec((s, LANE), lambda j: (0, j))
    return pl.pallas_call(
        body, name=name, grid=(4,), in_specs=_convpool_specs(s, l), out_specs=[col, col],
        out_shape=[jax.ShapeDtypeStruct((s, G), bf16)] * 2, compiler_params=_cp(("parallel",)),
    )(z, z, z, z, conv_w, pool_w, pool_scale)


def _convpool_bwd(name, z, conv_w, pool_w, pool_scale, dy, l):
    s = z.shape[0]

    def body(cb, cc, ch, pu, cw, pw, ps, dya, dyb, dcb, dcc, dch, dpu, dcw, dpw, dps):
        j = pl.program_id(0)
        fn = functools.partial(_convpool_fn, j=j)
        _, vjp = jax.vjp(fn, cb[...], cc[...], ch[...], pu[...], cw[0:1, :], cw[1:2, :], cw[2:3, :], pw[...], ps[...])
        g = vjp((dya[...], dyb[...]))
        dcb[...] = g[0].astype(bf16)
        dcc[...] = g[1].astype(bf16)
        dch[...] = g[2].astype(bf16)
        dpu[...] = g[3].astype(bf16)
        dcw[0:1, :] = g[4]
        dcw[1:2, :] = g[5]
        dcw[2:3, :] = g[6]
        dpw[...] = g[7]
        dps[...] = g[8]

    col = pl.BlockSpec((s, LANE), lambda j: (0, j))
    specs = _convpool_specs(s, l) + [pl.BlockSpec((s, LANE), lambda j: (0, j)),
                                     pl.BlockSpec((s, LANE), lambda j: (0, 4 + j))]
    return pl.pallas_call(
        body, name=name, grid=(4,), in_specs=specs,
        out_specs=[col, col, col, col, pl.BlockSpec((3, LANE), lambda j: (0, j)),
                   pl.BlockSpec((None, LANE, LANE), lambda j: (j, 0, 0)), pl.BlockSpec((1, LANE), lambda j: (0, j))],
        out_shape=[jax.ShapeDtypeStruct((s, G), bf16)] * 4 + [
            jax.ShapeDtypeStruct((3, G), f32), jax.ShapeDtypeStruct((4, LANE, LANE), f32),
            jax.ShapeDtypeStruct((1, G), f32)],
        compiler_params=_cp(("parallel",)),
    )(z, z, z, z, conv_w, pool_w, pool_scale, dy, dy)


def _foxprep_fwd(name, z, bf, l):
    s = z.shape[0]

    def body(m_ref, b_ref, fc_ref, fr_ref):
        fc, fr = _foxprep_fn(m_ref[...], b_ref[...])
        fc_ref[...] = fc
        fr_ref[...] = fr

    return pl.pallas_call(
        body, name=name, grid=(1,),
        in_specs=[pl.BlockSpec((s, LANE), lambda i: (0, MISC)), pl.BlockSpec((None, 1, LANE), lambda i: (l, 0, 0))],
        out_specs=[pl.BlockSpec((s, LANE), lambda i: (0, 0)), pl.BlockSpec((LANE, s), lambda i: (0, 0))],
        out_shape=[jax.ShapeDtypeStruct((s, LANE), f32), jax.ShapeDtypeStruct((LANE, s), f32)],
        compiler_params=_cp(("arbitrary",)),
    )(z, bf)


def _foxprep_bwd(name, z, bf, dfc4, dfr4, dmisc4, l):
    s = z.shape[0]

    def body(m_ref, b_ref, dfc_ref, dfr_ref, dm4_ref, dm_ref, db_ref):
        _, vjp = jax.vjp(_foxprep_fn, m_ref[...], b_ref[...])
        dfc = dfc_ref[0] + dfc_ref[1] + dfc_ref[2] + dfc_ref[3]
        dfr = dfr_ref[0] + dfr_ref[1] + dfr_ref[2] + dfr_ref[3]
        dfr = jnp.concatenate([dfr, jnp.zeros((LANE - 8, s), f32)], axis=0)
        dm, db = vjp((dfc, dfr))
        dm = dm + (dm4_ref[0] + dm4_ref[1] + dm4_ref[2] + dm4_ref[3])
        dm_ref[...] = dm.astype(bf16)
        db_ref[...] = db

    whole = lambda shape: pl.BlockSpec(shape, lambda i: (0,) * len(shape))
    return pl.pallas_call(
        body, name=name, grid=(1,),
        in_specs=[pl.BlockSpec((s, LANE), lambda i: (0, MISC)), pl.BlockSpec((None, 1, LANE), lambda i: (l, 0, 0)),
                  whole((4, s, LANE)), whole((4, 8, s)), whole((4, s, LANE))],
        out_specs=[whole((s, LANE)), whole((1, LANE))],
        out_shape=[jax.ShapeDtypeStruct((s, LANE), bf16), jax.ShapeDtypeStruct((1, LANE), f32)],
        compiler_params=_cp(("arbitrary",)),
    )(z, bf, dfc4, dfr4, dmisc4)


FOX_TQ = 256


def _fox_specs(s, l):
    return [pl.BlockSpec((s, LANE), lambda h: (0, FQ + h)),
            pl.BlockSpec((s, LANE), lambda h: (0, FK + h)),
            pl.BlockSpec((s, LANE), lambda h: (0, FV + h)),
            pl.BlockSpec((s, LANE), lambda h: (0, 0)),
            pl.BlockSpec((8, s), lambda h: (0, 0)),
            pl.BlockSpec((None, 1, LANE), lambda h: (l, 0, 0)),
            pl.BlockSpec((None, 1, LANE), lambda h: (l, 0, 0))]


def _fox_fwd(name, z, fc, fr, qg, kg, l):
    s = z.shape[0]
    tq = min(s, FOX_TQ)

    def body(q, k, v, fc_ref, fr_ref, qg_ref, kg_ref, y_ref):
        h = pl.program_id(0)
        for i in range(s // tq):
            rows, keys = pl.ds(i * tq, tq), pl.ds(0, (i + 1) * tq)
            y = _fox_fn(q[rows, :], k[keys, :], v[keys, :], fc_ref[rows, :], fr_ref[:, keys], qg_ref[...],
                        kg_ref[...], h, i)
            y_ref[rows, :] = y.astype(bf16)

    return pl.pallas_call(
        body, name=name, grid=(4,), in_specs=_fox_specs(s, l), out_specs=pl.BlockSpec((s, LANE), lambda h: (0, h)),
        out_shape=jax.ShapeDtypeStruct((s, G), bf16), compiler_params=_cp(("parallel",)),
    )(z, z, z, fc, fr, qg, kg)


def _fox_bwd(name, z, fc, fr, qg, kg, dy, l):
    s = z.shape[0]
    tq = min(s, FOX_TQ)

    def body(q, k, v, fc_ref, fr_ref, qg_ref, kg_ref, dy_ref, dq, dk_out, dv_out, dfc, dfr, dqg, dkg, dk, dv):
        h = pl.program_id(0)
        for ref in (dk, dv, dfr, dqg, dkg):
            ref[...] = jnp.zeros_like(ref)
        for i in range(s // tq):
            rows, keys = pl.ds(i * tq, tq), pl.ds(0, (i + 1) * tq)
            fn = functools.partial(_fox_fn, h=h, i=i)
            _, vjp = jax.vjp(fn, q[rows, :], k[keys, :], v[keys, :], fc_ref[rows, :], fr_ref[:, keys], qg_ref[...],
                             kg_ref[...])
            g = vjp(dy_ref[rows, :])
            dq[rows, :] = g[0].astype(bf16)
            dfc[rows, :] = g[3]
            dk[keys, :] += g[1]
            dv[keys, :] += g[2]
            dfr[:, keys] += g[4]
            dqg[...] += g[5]
            dkg[...] += g[6]
        dk_out[...] = dk[...].astype(bf16)
        dv_out[...] = dv[...].astype(bf16)

    head = pl.BlockSpec((s, LANE), lambda h: (0, h))
    gain = pl.BlockSpec((None, 1, LANE), lambda h: (h, 0, 0))
    return pl.pallas_call(
        body, name=name, grid=(4,), in_specs=_fox_specs(s, l) + [pl.BlockSpec((s, LANE), lambda h: (0, 12 + h))],
        out_specs=[head, head, head, pl.BlockSpec((None, s, LANE), lambda h: (h, 0, 0)),
                   pl.BlockSpec((None, 8, s), lambda h: (h, 0, 0)), gain, gain],
        out_shape=[jax.ShapeDtypeStruct((s, G), bf16)] * 3 + [
            jax.ShapeDtypeStruct((4, s, LANE), f32), jax.ShapeDtypeStruct((4, 8, s), f32),
            jax.ShapeDtypeStruct((4, 1, LANE), f32), jax.ShapeDtypeStruct((4, 1, LANE), f32)],
        scratch_shapes=[pltpu.VMEM((s, LANE), f32), pltpu.VMEM((s, LANE), f32)],
        compiler_params=_cp(("parallel",)),
    )(z, z, z, fc, fr, qg, kg, dy)


def _gla_ts(s):
    return min(s, 512)


def _gla1_specs(s, ts, l):
    return [pl.BlockSpec((ts, LANE), lambda h, i: (i, GQ + h)),
            pl.BlockSpec((ts, LANE), lambda h, i: (i, GK + h)),
            pl.BlockSpec((ts, LANE), lambda h, i: (i, GV + h)),
            pl.BlockSpec((ts, LANE), lambda h, i: (i, MISC)),
            pl.BlockSpec((None, LANE, LANE), lambda h, i: (l, 0, h)),
            pl.BlockSpec((None, 1, LANE), lambda h, i: (l, 0, h))]


def _gla1_fwd(name, z, wd, bd, l):
    s = z.shape[0]
    ts = _gla_ts(s)
    nb = ts // CHUNK

    def body(q, k, v, m, wd_ref, bd_ref, qe_ref, kd_ref, el_ref, oi_ref):
        qe, kd, el, oi = _gla1_fn(q[...], k[...], v[...], m[...], wd_ref[...], bd_ref[...])
        qe_ref[...] = qe.astype(bf16)
        kd_ref[...] = kd.astype(bf16)
        el_ref[...] = el
        oi_ref[...] = oi

    blk = pl.BlockSpec((ts, LANE), lambda h, i: (i, h))
    return pl.pallas_call(
        body, name=name, grid=(4, s // ts), in_specs=_gla1_specs(s, ts, l),
        out_specs=[blk, blk, pl.BlockSpec((nb, LANE), lambda h, i: (i, h)), blk],
        out_shape=[jax.ShapeDtypeStruct((s, G), bf16), jax.ShapeDtypeStruct((s, G), bf16),
                   jax.ShapeDtypeStruct((s // CHUNK, G), f32), jax.ShapeDtypeStruct((s, G), f32)],
        compiler_params=_cp(("parallel", "parallel")),
    )(z, z, z, z, wd, bd)


def _gla1_bwd(name, z, wd, bd, dqe, dkd, del_, do, dvi, l):
    s = z.shape[0]
    ts = _gla_ts(s)
    nb = ts // CHUNK

    def body(q, k, v, m, wd_ref, bd_ref, dqe_ref, dkd_ref, del_ref, do_ref, dvi_ref, dq, dk, dv, dm, dwd, dbd):
        i = pl.program_id(1)
        _, vjp = jax.vjp(_gla1_fn, q[...], k[...], v[...], m[...], wd_ref[...], bd_ref[...])
        g = vjp((dqe_ref[...], dkd_ref[...], del_ref[...], do_ref[...]))
        dq[...] = g[0].astype(bf16)
        dk[...] = g[1].astype(bf16)
        dv[...] = (g[2] + dvi_ref[...]).astype(bf16)
        dm[...] = g[3]

        @pl.when(i == 0)
        def _():
            dwd[...] = g[4]
            dbd[...] = g[5]

        @pl.when(i > 0)
        def _():
            dwd[...] += g[4]
            dbd[...] += g[5]

    blk = pl.BlockSpec((ts, LANE), lambda h, i: (i, h))
    specs = _gla1_specs(s, ts, l) + [blk, blk, pl.BlockSpec((nb, LANE), lambda h, i: (i, h)), blk, blk]
    return pl.pallas_call(
        body, name=name, grid=(4, s // ts), in_specs=specs,
        out_specs=[blk, blk, blk, pl.BlockSpec((None, ts, LANE), lambda h, i: (h, i, 0)),
                   pl.BlockSpec((None, LANE, LANE), lambda h, i: (h, 0, 0)),
                   pl.BlockSpec((None, 1, LANE), lambda h, i: (h, 0, 0))],
        out_shape=[jax.ShapeDtypeStruct((s, G), bf16)] * 3 + [
            jax.ShapeDtypeStruct((4, s, LANE), f32), jax.ShapeDtypeStruct((4, LANE, LANE), f32),
            jax.ShapeDtypeStruct((4, 1, LANE), f32)],
        compiler_params=_cp(("parallel", "arbitrary")),
    )(z, z, z, z, wd, bd, dqe, dkd, del_, do, dvi)


def _gla2_fwd(name, z, qe, kd, el, oi):
    s = z.shape[0]
    n = s // CHUNK

    def body(v_ref, qe_ref, kd_ref, el_ref, oi_ref, o_ref, st_ref, cur):
        cur[...] = jnp.zeros_like(cur)

        def step(c, carry):
            rows = pl.ds(pl.multiple_of(c * CHUNK, CHUNK), CHUNK)
            st = cur[...]
            st_ref[c] = st
            o_ref[rows, :] = oi_ref[rows, :] + _dot(qe_ref[rows, :], st, ((1,), (1,)))
            cur[...] = st * el_ref[pl.ds(c, 1), :] + _dot(v_ref[rows, :], kd_ref[rows, :], ((0,), (0,)))
            return carry

        lax.fori_loop(0, n, step, 0)

    head = pl.BlockSpec((s, LANE), lambda h: (0, h))
    return pl.pallas_call(
        body, name=name, grid=(4,),
        in_specs=[pl.BlockSpec((s, LANE), lambda h: (0, GV + h)), head, head,
                  pl.BlockSpec((n, LANE), lambda h: (0, h)), head],
        out_specs=[head, pl.BlockSpec((None, n, LANE, LANE), lambda h: (h, 0, 0, 0))],
        out_shape=[jax.ShapeDtypeStruct((s, G), f32), jax.ShapeDtypeStruct((4, n, LANE, LANE), f32)],
        scratch_shapes=[pltpu.VMEM((LANE, LANE), f32)],
        compiler_params=_cp(("parallel",)),
    )(z, qe, kd, el, oi)


def _gla2_bwd(name, z, qe, kd, el, st, do):
    s = z.shape[0]
    n = s // CHUNK

    def body(v_ref, qe_ref, kd_ref, el_ref, st_ref, do_ref, dqe_ref, dkd_ref, dv_ref, del_ref, dcur):
        dcur[...] = jnp.zeros_like(dcur)

        def step(t, carry):
            c = n - 1 - t
            rows = pl.ds(pl.multiple_of(c * CHUNK, CHUNK), CHUNK)
            dn = dcur[...]
            stc = st_ref[c]
            doc = do_ref[rows, :]
            dqe_ref[rows, :] = _dot(doc, stc)
            dv_ref[rows, :] = _dot(kd_ref[rows, :], dn, ((1,), (1,)))
            dkd_ref[rows, :] = _dot(v_ref[rows, :], dn)
            del_ref[pl.ds(c, 1), :] = jnp.sum(stc * dn, axis=0, keepdims=True)
            dcur[...] = dn * el_ref[pl.ds(c, 1), :] + _dot(doc, qe_ref[rows, :], ((0,), (0,)))
            return carry

        lax.fori_loop(0, n, step, 0)

    head = pl.BlockSpec((s, LANE), lambda h: (0, h))
    chunk = pl.BlockSpec((n, LANE), lambda h: (0, h))
    return pl.pallas_call(
        body, name=name, grid=(4,),
        in_specs=[pl.BlockSpec((s, LANE), lambda h: (0, GV + h)), head, head, chunk,
                  pl.BlockSpec((None, n, LANE, LANE), lambda h: (h, 0, 0, 0)), head],
        out_specs=[head, head, head, chunk],
        out_shape=[jax.ShapeDtypeStruct((s, G), f32)] * 3 + [jax.ShapeDtypeStruct((n, G), f32)],
        scratch_shapes=[pltpu.VMEM((LANE, LANE), f32)],
        compiler_params=_cp(("parallel",)),
    )(z, qe, kd, el, st, do)


def _gla3_specs(ts, l):
    return [pl.BlockSpec((ts, LANE), lambda h, i: (i, h)),
            pl.BlockSpec((ts, LANE), lambda h, i: (i, GG + h)),
            pl.BlockSpec((None, 1, LANE), lambda h, i: (l, 0, 0))]


def _gla3_fwd(name, o, z, og, l):
    s = z.shape[0]
    ts = _gla_ts(s)

    def body(o_ref, g_ref, og_ref, y_ref):
        y_ref[...] = _gla3_fn(o_ref[...], g_ref[...], og_ref[...]).astype(bf16)

    return pl.pallas_call(
        body, name=name, grid=(4, s // ts), in_specs=_gla3_specs(ts, l),
        out_specs=pl.BlockSpec((ts, LANE), lambda h, i: (i, h)),
        out_shape=jax.ShapeDtypeStruct((s, G), bf16), compiler_params=_cp(("parallel", "parallel")),
    )(o, z, og)


def _gla3_bwd(name, o, z, og, dy, l):
    s = z.shape[0]
    ts = _gla_ts(s)

    def body(o_ref, g_ref, og_ref, dy_ref, do_ref, dg_ref, dog_ref):
        i = pl.program_id(1)
        _, vjp = jax.vjp(_gla3_fn, o_ref[...], g_ref[...], og_ref[...])
        g = vjp(dy_ref[...])
        do_ref[...] = g[0]
        dg_ref[...] = g[1].astype(bf16)

        @pl.when(i == 0)
        def _():
            dog_ref[...] = g[2]

        @pl.when(i > 0)
        def _():
            dog_ref[...] += g[2]

    blk = pl.BlockSpec((ts, LANE), lambda h, i: (i, h))
    return pl.pallas_call(
        body, name=name, grid=(4, s // ts),
        in_specs=_gla3_specs(ts, l) + [pl.BlockSpec((ts, LANE), lambda h, i: (i, 8 + h))],
        out_specs=[blk, blk, pl.BlockSpec((None, 1, LANE), lambda h, i: (h, 0, 0))],
        out_shape=[jax.ShapeDtypeStruct((s, G), f32), jax.ShapeDtypeStruct((s, G), bf16),
                   jax.ShapeDtypeStruct((4, 1, LANE), f32)],
        compiler_params=_cp(("parallel", "arbitrary")),
    )(o, z, og, dy)


def _layer_fwd(l, h, p, weights):
    s = h.shape[0]
    tb = _tm_big(s)
    n = f"l{l}_"
    w_in = weights(l, "w_in", h)
    u = _rmsnorm_fwd(n + "norm_mix", h, p["g_mix"], l)
    z = _mm(n + "mm_in", u, w_in,
            pl.BlockSpec((tb, D), lambda j, i, k: (i, 0)), pl.BlockSpec((D, 1152), lambda j, i, k: (0, j)),
            jax.ShapeDtypeStruct((s, ZC), f32), pl.BlockSpec((tb, 1152), lambda j, i, k: (i, j)),
            (ZC // 1152, s // tb, 1), ((1,), (0,)), 1)
    w_out = weights(l, "w_out", z)
    ya, yb = _convpool_fwd(n + "convpool", z, p["conv_w"], p["pool_w"], p["pool_scale"], l)
    qe, kd, el, oi = _gla1_fwd(n + "gla_chunk", z, p["wdec"], p["bdec"], l)
    o, st = _gla2_fwd(n + "gla_scan", z, qe, kd, el, oi)
    yc = _gla3_fwd(n + "gla_out", o, z, p["gla_og"], l)
    fc, fr = _foxprep_fwd(n + "fox_prep", z, p["fox_bf"], l)
    yd = _fox_fwd(n + "fox_attn", z, fc, fr, p["fox_qg"], p["fox_kg"], l)
    y = jnp.concatenate([ya, yb, yc, yd], axis=1)
    w_gate, w_up, w_down = weights(l, "ffn", y)
    res_tile = lambda: pl.BlockSpec((tb, 1024), lambda j, i, k: (i, j))
    h1 = _mm(n + "mm_out", y, w_out.reshape(D, D),
             pl.BlockSpec((tb, D), lambda j, i, k: (i, 0)), pl.BlockSpec((D, 1024), lambda j, i, k: (0, j)),
             jax.ShapeDtypeStruct((s, D), f32), res_tile(), (2, s // tb, 1), ((1,), (0,)), 1, res=h, res_spec=res_tile())
    u2 = _rmsnorm_fwd(n + "norm_ffn", h1, p["g_ffn"], l)
    gate, up, act = _mm_gate_up(n + "mm_gate_up", u2, w_gate, w_up)
    h2 = _mm(n + "mm_down", act, w_down,
             pl.BlockSpec((tb, FB), lambda j, i, k: (i, k)),
             pl.BlockSpec((None, FB, 1024), lambda j, i, k: (k, 0, j)),
             jax.ShapeDtypeStruct((s, D), f32), res_tile(), (2, s // tb, NCHIP), ((1,), (0,)), NCHIP,
             res=h1, res_spec=res_tile())
    saved = dict(h=h, u=u, z=z, qe=qe, kd=kd, el=el, st=st, o=o, fc=fc, fr=fr, y=y, h1=h1, u2=u2,
                 gate=gate, up=up, act=act, w_in=w_in, w_out=w_out, w_gate=w_gate, w_up=w_up, w_down=w_down)
    return h2, saved


def _mm_tn(name, a, b, ta, tb, out_shape, out_spec, grid):
    s = a.shape[0]
    return _mm(name, a, b, pl.BlockSpec((s, ta), lambda i, j, k: (0, i)), pl.BlockSpec((s, tb), lambda i, j, k: (0, j)),
               out_shape, out_spec, grid, ((0,), (0,)), 1)


def _layer_bwd_ffn(l, dh2, dh2b, p, sv, ship):
    s = dh2.shape[0]
    tb = _tm_big(s)
    n = f"l{l}_bwd_"
    g_wd = _mm_tn(n + "mm_dwd", sv["act"], dh2b, FB, 1024, jax.ShapeDtypeStruct((NCHIP, FB, D), bf16),
                  pl.BlockSpec((None, FB, 1024), lambda i, j, k: (i, 0, j)), (NCHIP, 2, 1))
    dgate, dup = _mm_dact(n + "mm_dact", dh2b, sv["w_down"], sv["gate"], sv["up"])
    wg_shape = jax.ShapeDtypeStruct((NCHIP, D, FB), bf16)
    wg_spec = lambda: pl.BlockSpec((None, 1024, FB), lambda i, j, k: (j, i, 0))
    g_wg = _mm_tn(n + "mm_dwg", sv["u2"], dgate, 1024, FB, wg_shape, wg_spec(), (2, NCHIP, 1))
    g_wu = _mm_tn(n + "mm_dwu", sv["u2"], dup, 1024, FB, wg_shape, wg_spec(), (2, NCHIP, 1))
    token = ship.early(dict(w_gate=g_wg, w_up=g_wu, w_down=g_wd))
    nt_in = lambda: (pl.BlockSpec((tb, FB), lambda j, i, k: (i, k)),
                     pl.BlockSpec((None, 1024, FB), lambda j, i, k: (k, j, 0)))
    nt_out = lambda: (jax.ShapeDtypeStruct((s, D), f32), pl.BlockSpec((tb, 1024), lambda j, i, k: (i, j)))
    du2 = _mm(n + "mm_du2_gate", dgate, sv["w_gate"], *nt_in(), *nt_out(), (2, s // tb, NCHIP), ((1,), (1,)), NCHIP,
              dep=token)
    du2 = _mm(n + "mm_du2_up", dup, sv["w_up"], *nt_in(), *nt_out(), (2, s // tb, NCHIP), ((1,), (1,)), NCHIP,
              res=du2, res_spec=pl.BlockSpec((tb, 1024), lambda j, i, k: (i, j)))
    dh1, dh1b, dg_ffn = _rmsnorm_bwd(n + "norm_ffn", sv["h1"], p["g_ffn"], du2, dh2, l)
    g_wo = _mm_tn(n + "mm_dwo", sv["y"], dh1b, G, 1024, jax.ShapeDtypeStruct((NCHIP, G, D), bf16),
                  pl.BlockSpec((None, G, 1024), lambda i, j, k: (i, 0, j)), (NCHIP, 2, 1))
    token = ship.finish(dict(w_out=g_wo), g_wo)
    dy = _mm(n + "mm_dy", dh1b, sv["w_out"].reshape(D, D),
             pl.BlockSpec((tb, D), lambda j, i, k: (i, 0)), pl.BlockSpec((1024, D), lambda j, i, k: (j, 0)),
             jax.ShapeDtypeStruct((s, D), f32), pl.BlockSpec((tb, 1024), lambda j, i, k: (i, j)),
             (2, s // tb, 1), ((1,), (1,)), 1, dep=token)
    return dh1, dy, dg_ffn


def _layer_bwd_mix(l, dh1, dy, p, sv, ship):
    s = dh1.shape[0]
    n = f"l{l}_bwd_"
    z = sv["z"]
    dcb, dcc, dch, dpu, dconv, dpoolw, dpools = _convpool_bwd(
        n + "convpool", z, p["conv_w"], p["pool_w"], p["pool_scale"], dy, l)
    do, dgg, dog = _gla3_bwd(n + "gla_out", sv["o"], z, p["gla_og"], dy, l)
    dqe, dkd, dvi, del_ = _gla2_bwd(n + "gla_scan", z, sv["qe"], sv["kd"], sv["el"], sv["st"], do)
    dgq, dgk, dgv, dmisc4, dwd, dbd = _gla1_bwd(n + "gla_chunk", z, p["wdec"], p["bdec"], dqe, dkd, del_, do, dvi, l)
    dfq, dfk, dfv, dfc4, dfr4, dqg, dkg = _fox_bwd(n + "fox_attn", z, sv["fc"], sv["fr"], p["fox_qg"], p["fox_kg"], dy, l)
    dmisc, dbf = _foxprep_bwd(n + "fox_prep", z, p["fox_bf"], dfc4, dfr4, dmisc4, l)
    dz = jnp.concatenate([dcb, dcc, dch, dpu, dgq, dgk, dgv, dgg, dfq, dfk, dfv, dmisc], axis=1)
    g_wi = _mm_tn(n + "mm_dwi", sv["u"], dz, 1024, 1152, jax.ShapeDtypeStruct((D, ZC), bf16),
                  pl.BlockSpec((1024, 1152), lambda i, j, k: (i, j)), (2, ZC // 1152, 1))
    token = ship.early(dict(w_in=_win_to_blocks(g_wi)))
    tb = _tm_big(s)
    du = _mm(n + "mm_du", dz, sv["w_in"],
             pl.BlockSpec((tb, 1920), lambda j, i, k: (i, k)), pl.BlockSpec((1024, 1920), lambda j, i, k: (j, k)),
             jax.ShapeDtypeStruct((s, D), f32), pl.BlockSpec((tb, 1024), lambda j, i, k: (i, j)),
             (2, s // tb, ZC // 1920), ((1,), (1,)), ZC // 1920, dep=token)
    token = ship.finish({}, du)
    dh, dhb, dg_mix = _rmsnorm_bwd(n + "norm_mix", sv["h"], p["g_mix"], du, dh1, l, dep=token)
    small = dict(
        norm_mix_g=dg_mix[0], conv_w=dconv, pool_w=dpoolw, pool_scale=dpools[0],
        gla_w_decay=jnp.concatenate([dwd[hh, GA_LANE:GA_LANE + 16, :64] for hh in range(4)], axis=1),
        gla_b_decay=jnp.concatenate([dbd[hh, 0, :64] for hh in range(4)]),
        gla_out_g=jnp.sum(dog[:, 0, :], axis=0), fox_q_g=jnp.sum(dqg[:, 0, :], axis=0),
        fox_k_g=jnp.sum(dkg[:, 0, :], axis=0), fox_b_f=dbf[0, FF_LANE:FF_LANE + 4])
    return dh, dhb, small


def _win_from_blocks(wb):
    def cols(a, b):
        parts = []
        for kk in range(NCHIP):
            lo, hi = max(a, kk * WINB), min(b, (kk + 1) * WINB)
            if lo < hi:
                parts.append(wb[kk, :, lo - kk * WINB:hi - kk * WINB])
        return parts

    zeros = lambda w: [jnp.zeros((wb.shape[1], w), wb.dtype)]
    segs = cols(0, 2048)
    for hh in range(4):
        segs += cols(2048 + 64 * hh, 2112 + 64 * hh) + zeros(64)
    for hh in range(4):
        segs += cols(2304 + 64 * hh, 2368 + 64 * hh) + zeros(64)
    segs += cols(2560, 3584) + cols(3600, 5136)
    segs += cols(5136, 5140) + zeros(GA_LANE - 4) + cols(3584, 3600) + zeros(LANE - GA_LANE - 16)
    return jnp.concatenate(segs, axis=-1)


def _win_to_blocks(g):
    mb = MISC * LANE
    segs = [(0, 2048)] + [(GQ * LANE + LANE * hh, 64) for hh in range(4)] + [(GK * LANE + LANE * hh, 64) for hh in range(4)]
    segs += [(GV * LANE, 1024), (mb + GA_LANE, 16), (FQ * LANE, 1536), (mb + FF_LANE, 4)]
    blocks, at = [[] for _ in range(NCHIP)], 0
    for start, width in segs:
        while width > 0:
            take = min(width, (at // WINB + 1) * WINB - at)
            blocks[at // WINB].append(g[:, start:start + take])
            start, width, at = start + take, width - take, at + take
    return jnp.stack([jnp.concatenate(b, axis=1) for b in blocks])


def _place():
    x, y, c = lax.axis_index("x"), lax.axis_index("y"), lax.axis_index("c")
    chips = [(1 - x, y), (x, 1 - y), (1 - x, 1 - y)]
    return x, y, c, chips


def _allgather_small(name, v, dep):
    m_per, n = v.shape

    def body(x_ref, dep_ref, out_ref, send_sems, recv_sems, local_sem):
        x, y, c, chips = _place()
        me, sibling = (x, y, c), (x, y, 1 - c)

        def rows(px, py, pc):
            return out_ref.at[pl.ds((4 * px + 2 * py + pc) * m_per, m_per), :]

        def copy(k, block, to, src=None):
            return pltpu.make_async_remote_copy(
                src_ref=rows(*block) if src is None else src, dst_ref=rows(*block),
                send_sem=send_sems.at[k], recv_sem=recv_sems.at[k], device_id=to, device_id_type=MESH)

        mine = pltpu.make_async_copy(x_ref, rows(*me), local_sem)
        mine.start()
        first = [copy(0, me, sibling, src=x_ref)]
        first += [copy(1 + j, me, (*chip, c), src=x_ref) for j, chip in enumerate(chips)]
        for cp in first:
            cp.start()
        passed = [copy(4 + j, (*chip, c), sibling) for j, chip in enumerate(chips)]
        for j, chip in enumerate(chips):
            copy(1 + j, (*chip, c), me).wait_recv()
            passed[j].start()
        copy(0, sibling, me).wait_recv()
        for j, chip in enumerate(chips):
            copy(4 + j, (*chip, 1 - c), me).wait_recv()
        for cp in first + passed:
            cp.wait_send()
        mine.wait()

    return pl.pallas_call(
        body, name=name, out_shape=jax.ShapeDtypeStruct((8 * m_per, n), v.dtype),
        in_specs=[pl.BlockSpec(memory_space=pltpu.VMEM), pl.BlockSpec(memory_space=pl.ANY)],
        out_specs=pl.BlockSpec(memory_space=pltpu.VMEM),
        scratch_shapes=[pltpu.SemaphoreType.DMA((7,)), pltpu.SemaphoreType.DMA((7,)), pltpu.SemaphoreType.DMA],
    )(v, dep)


def _hbm_specs(n):
    return [pl.BlockSpec(memory_space=pl.ANY)] * n


def _own_slot(shard, chip):
    return lax.dynamic_update_index_in_dim(lax.empty((NCHIP,) + shard.shape, shard.dtype), shard, chip, 0)


def _cast_slot(name, w, l, pos, dep):
    _, r, cdim = w.shape
    tr = r // 4
    deps = [] if dep is None else [dep]

    def body(pos_ref, w_ref, *rest):
        rest[-1][...] = w_ref[...].astype(bf16)

    return pl.pallas_call(
        body, name=name, out_shape=jax.ShapeDtypeStruct((NCHIP, r, cdim), bf16),
        grid_spec=pltpu.PrefetchScalarGridSpec(
            num_scalar_prefetch=1, grid=(r // tr,),
            in_specs=[pl.BlockSpec((None, tr, cdim), lambda i, p: (l, i, 0))]
            + [pl.BlockSpec((8, LANE), lambda i, p: (0, 0)) for _ in deps],
            out_specs=pl.BlockSpec((None, tr, cdim), lambda i, p: (p[0], i, 0))),
        compiler_params=_cp(("parallel",)),
    )(pos, w, *deps)


def _cast_slot_w_in(name, w, l, pos, dep):
    wt = jnp.transpose(w, (2, 0, 1))
    cols, _, rows = wt.shape
    deps = [] if dep is None else [dep]

    def body(pos_ref, w_ref, *rest):
        rest[-1][...] = jnp.transpose(w_ref[:, l, :]).astype(bf16)

    return pl.pallas_call(
        body, name=name, out_shape=jax.ShapeDtypeStruct((NCHIP, rows, cols), bf16),
        grid_spec=pltpu.PrefetchScalarGridSpec(
            num_scalar_prefetch=1, grid=(pl.cdiv(cols, LANE),),
            in_specs=[pl.BlockSpec((LANE, 2, rows), lambda i, p: (i, 0, 0))]
            + [pl.BlockSpec((8, LANE), lambda i, p: (0, 0)) for _ in deps],
            out_specs=pl.BlockSpec((None, rows, LANE), lambda i, p: (p[0], 0, i))),
        compiler_params=_cp(("parallel",)),
    )(pos, wt, *deps)


HBM = pl.BlockSpec(memory_space=pltpu.HBM)
SEM = pl.BlockSpec(memory_space=pltpu.SEMAPHORE)
EFFECT = pltpu.SideEffectType.DATAFLOW_SIDE_EFFECTING


def _in_hbm(v):
    return pltpu.with_memory_space_constraint(v, pltpu.HBM)


def _gather_start(name, groups):
    flat = [b for g in groups for b in g]
    nt, ng = len(flat), len(groups)

    def body(*refs):
        outs = refs[nt:]
        sems, bufs, token = outs[:2 * ng], outs[2 * ng:2 * ng + nt], outs[2 * ng + nt]
        token[...] = jnp.zeros_like(token)
        x, y, c, chips = _place()
        me = 2 * x + y
        t = 0
        for gi, g in enumerate(groups):
            for k in range(len(g)):
                r = bufs[t].shape[1] // 2
                mine = bufs[t].at[me, pl.ds(c * r, r), :]
                for j, (px, py) in enumerate(chips):
                    pltpu.make_async_remote_copy(
                        src_ref=mine, dst_ref=mine, send_sem=sems[2 * gi].at[3 * k + j], recv_sem=sems[2 * gi + 1].at[3 * k + j],
                        device_id=(px, py, c), device_id_type=MESH).start()
                t += 1

    sem_shapes = []
    for g in groups:
        sem_shapes += [pltpu.SemaphoreType.DMA((3 * len(g),))] * 2
    out = pl.pallas_call(
        body, name=name,
        out_shape=tuple(sem_shapes + [pltpu.HBM(b.shape, b.dtype) for b in flat] + [jax.ShapeDtypeStruct((8, LANE), f32)]),
        in_specs=tuple([HBM] * nt),
        out_specs=tuple([SEM] * (2 * ng) + [HBM] * nt + [pl.BlockSpec(memory_space=pltpu.VMEM)]),
        input_output_aliases={t: 2 * ng + t for t in range(nt)},
        compiler_params=pltpu.CompilerParams(has_side_effects=EFFECT),
    )(*[_in_hbm(b) for b in flat])
    sems = [(out[2 * gi], out[2 * gi + 1]) for gi in range(ng)]
    bufs, at = [], 2 * ng
    for g in groups:
        bufs.append(list(out[at:at + len(g)]))
        at += len(g)
    return sems, bufs, out[at]


def _gather_wait(name, bufs, send_sems, recv_sems, after):
    nt = len(bufs)

    def body(*refs):
        ins, ss, rs = refs[:nt], refs[nt], refs[nt + 1]
        x, y, c, chips = _place()
        me = 2 * x + y
        for k in range(nt):
            r = ins[k].shape[1] // 2
            for j, (px, py) in enumerate(chips):
                cp = pltpu.make_async_remote_copy(
                    src_ref=ins[k].at[me, pl.ds(c * r, r), :], dst_ref=ins[k].at[2 * px + py, pl.ds(c * r, r), :],
                    send_sem=ss.at[3 * k + j], recv_sem=rs.at[3 * k + j], device_id=(px, py, c), device_id_type=MESH)
                cp.wait_send()
                cp.wait_recv()

    out = pl.pallas_call(
        body, name=name, out_shape=tuple(pltpu.HBM(b.shape, b.dtype) for b in bufs),
        in_specs=tuple([HBM] * nt + [SEM, SEM, pl.BlockSpec(memory_space=pl.ANY)]), out_specs=tuple([HBM] * nt),
        input_output_aliases={t: t for t in range(nt)},
        compiler_params=pltpu.CompilerParams(has_side_effects=EFFECT),
    )(*bufs, send_sems, recv_sems, after)
    return list(out)


def _gather_exchange(name, bufs):
    nt = len(bufs)

    def body(*refs):
        outs = refs[nt:2 * nt]
        send_sems, recv_sems = refs[2 * nt:]
        x, y, c, chips = _place()
        sibling = (x, y, 1 - c)

        def half(t, chip_idx, cc):
            r = outs[t].shape[1] // 2
            return outs[t].at[chip_idx, pl.ds(cc * r, r), :]

        sent = []
        for t in range(nt):
            for j, (px, py) in enumerate(chips):
                cp = pltpu.make_async_remote_copy(
                    src_ref=half(t, 2 * px + py, c), dst_ref=half(t, 2 * px + py, c),
                    send_sem=send_sems.at[t, j], recv_sem=recv_sems.at[t, j], device_id=sibling, device_id_type=MESH)
                cp.start()
                sent.append(cp)
        for t in range(nt):
            for j, (px, py) in enumerate(chips):
                pltpu.make_async_remote_copy(
                    src_ref=half(t, 2 * px + py, 1 - c), dst_ref=half(t, 2 * px + py, 1 - c),
                    send_sem=send_sems.at[t, j], recv_sem=recv_sems.at[t, j], device_id=sibling,
                    device_id_type=MESH).wait_recv()
        for cp in sent:
            cp.wait_send()

    return pl.pallas_call(
        body, name=name, out_shape=[jax.ShapeDtypeStruct(v.shape, v.dtype) for v in bufs],
        in_specs=_hbm_specs(nt), out_specs=_hbm_specs(nt), input_output_aliases={t: t for t in range(nt)},
        scratch_shapes=[pltpu.SemaphoreType.DMA((nt, 3)), pltpu.SemaphoreType.DMA((nt, 3))],
    )(*bufs)


def _rs_to_sibling(name, grads):
    nt = len(grads)

    def body(*refs):
        ins, outs = refs[:nt], refs[nt:2 * nt]
        send_sems, recv_sems = refs[2 * nt:]
        x, y, c, _ = _place()
        cps = []
        for t in range(nt):
            r = ins[t].shape[1] // 2
            cp = pltpu.make_async_remote_copy(
                src_ref=ins[t].at[:, pl.ds((1 - c) * r, r), :], dst_ref=outs[t],
                send_sem=send_sems.at[t], recv_sem=recv_sems.at[t], device_id=(x, y, 1 - c), device_id_type=MESH)
            cp.start()
            cps.append(cp)
        for cp in cps:
            cp.wait()

    return pl.pallas_call(
        body, name=name,
        out_shape=[jax.ShapeDtypeStruct((NCHIP, g.shape[1] // 2, g.shape[2]), g.dtype) for g in grads],
        in_specs=_hbm_specs(nt), out_specs=_hbm_specs(nt),
        scratch_shapes=[pltpu.SemaphoreType.DMA((nt,)), pltpu.SemaphoreType.DMA((nt,))],
    )(*grads)


def _sibling_start(name, grads):
    nt = len(grads)

    def body(*refs):
        outs = refs[2 * nt:]
        ss, rs, src, land, token = outs[0], outs[1], outs[2:2 + nt], outs[2 + nt:2 + 2 * nt], outs[2 + 2 * nt]
        x, y, c, _ = _place()
        for t in range(nt):
            r = src[t].shape[1] // 2
            pltpu.make_async_remote_copy(
                src_ref=src[t].at[:, pl.ds((1 - c) * r, r), :], dst_ref=land[t], send_sem=ss.at[t], recv_sem=rs.at[t],
                device_id=(x, y, 1 - c), device_id_type=MESH).start()
        token[...] = jnp.zeros_like(token)

    src_shapes = [pltpu.HBM(g.shape, g.dtype) for g in grads]
    land_shapes = [pltpu.HBM((NCHIP, g.shape[1] // 2, g.shape[2]), g.dtype) for g in grads]
    out = pl.pallas_call(
        body, name=name,
        out_shape=tuple([pltpu.SemaphoreType.DMA((nt,))] * 2 + src_shapes + land_shapes
                        + [jax.ShapeDtypeStruct((8, LANE), f32)]),
        in_specs=tuple([HBM] * (2 * nt)),
        out_specs=tuple([SEM, SEM] + [HBM] * (2 * nt) + [pl.BlockSpec(memory_space=pltpu.VMEM)]),
        input_output_aliases={t: 2 + t for t in range(2 * nt)},
        compiler_params=pltpu.CompilerParams(has_side_effects=EFFECT),
    )(*[_in_hbm(g) for g in grads], *[_in_hbm(lax.empty(s_.shape, s_.dtype)) for s_ in land_shapes])
    return out[0], out[1], list(out[2:2 + nt]), list(out[2 + nt:2 + 2 * nt]), out[2 + 2 * nt]


def _sibling_wait(name, grads, land, send_sems, recv_sems, after):
    nt = len(grads)

    def body(*refs):
        src, dst, ss, rs = refs[:nt], refs[nt:2 * nt], refs[2 * nt], refs[2 * nt + 1]
        x, y, c, _ = _place()
        for t in range(nt):
            r = src[t].shape[1] // 2
            cp = pltpu.make_async_remote_copy(
                src_ref=src[t].at[:, pl.ds((1 - c) * r, r), :], dst_ref=dst[t], send_sem=ss.at[t], recv_sem=rs.at[t],
                device_id=(x, y, 1 - c), device_id_type=MESH)
            cp.wait_send()
            cp.wait_recv()

    shapes = [pltpu.HBM(v.shape, v.dtype) for v in list(grads) + list(land)]
    out = pl.pallas_call(
        body, name=name, out_shape=tuple(shapes),
        in_specs=tuple([HBM] * (2 * nt) + [SEM, SEM, pl.BlockSpec(memory_space=pl.ANY)]),
        out_specs=tuple([HBM] * (2 * nt)), input_output_aliases={t: t for t in range(2 * nt)},
        compiler_params=pltpu.CompilerParams(has_side_effects=EFFECT),
    )(*grads, *land, send_sems, recv_sems, after)
    return list(out[:nt]), list(out[nt:])


def _rs_pair_sum(name, pos, g, other):
    r, cdim = other.shape[1], other.shape[2]
    tr = r // 4 if (r // 4) % 16 == 0 else r // 2
    nblk = r // tr

    def body(pos_ref, g_ref, o_ref, s_ref):
        s_ref[...] = (g_ref[...].astype(f32) + o_ref[...].astype(f32)).astype(bf16)

    blk = pl.BlockSpec((None, tr, cdim), lambda q, i, p: (q, i, 0))
    return pl.pallas_call(
        body, name=name, out_shape=jax.ShapeDtypeStruct(other.shape, bf16),
        grid_spec=pltpu.PrefetchScalarGridSpec(
            num_scalar_prefetch=1, grid=(NCHIP, nblk),
            in_specs=[pl.BlockSpec((None, tr, cdim), lambda q, i, p: (q, p[1] * nblk + i, 0)), blk], out_specs=blk),
        compiler_params=_cp(("parallel", "parallel")),
    )(pos, g, other)


def _scatter_start(name, sums):
    nt = len(sums)

    def body(*refs):
        outs = refs[2 * nt:]
        ss, rs, src, land, token = outs[0], outs[1], outs[2:2 + nt], outs[2 + nt:2 + 2 * nt], outs[2 + 2 * nt]
        x, y, c, chips = _place()
        me = 2 * x + y
        for t in range(nt):
            for j, (px, py) in enumerate(chips):
                pltpu.make_async_remote_copy(
                    src_ref=src[t].at[2 * px + py], dst_ref=land[t].at[me], send_sem=ss.at[3 * t + j], recv_sem=rs.at[3 * t + j],
                    device_id=(px, py, c), device_id_type=MESH).start()
        token[...] = jnp.zeros_like(token)

    shapes = [pltpu.HBM(v.shape, v.dtype) for v in sums]
    out = pl.pallas_call(
        body, name=name,
        out_shape=tuple([pltpu.SemaphoreType.DMA((3 * nt,))] * 2 + shapes + shapes + [jax.ShapeDtypeStruct((8, LANE), f32)]),
        in_specs=tuple([HBM] * (2 * nt)),
        out_specs=tuple([SEM, SEM] + [HBM] * (2 * nt) + [pl.BlockSpec(memory_space=pltpu.VMEM)]),
        input_output_aliases={t: 2 + t for t in range(2 * nt)},
        compiler_params=pltpu.CompilerParams(has_side_effects=EFFECT),
    )(*[_in_hbm(v) for v in sums], *[_in_hbm(lax.empty(v.shape, v.dtype)) for v in sums])
    return out[0], out[1], list(out[2:2 + nt]), list(out[2 + nt:2 + 2 * nt]), out[2 + 2 * nt]


def _scatter_wait(name, sums, land, send_sems, recv_sems, after):
    nt = len(sums)

    def body(*refs):
        src, dst, ss, rs = refs[:nt], refs[nt:2 * nt], refs[2 * nt], refs[2 * nt + 1]
        x, y, c, chips = _place()
        for t in range(nt):
            for j, (px, py) in enumerate(chips):
                cp = pltpu.make_async_remote_copy(
                    src_ref=src[t].at[2 * px + py], dst_ref=dst[t].at[2 * px + py], send_sem=ss.at[3 * t + j],
                    recv_sem=rs.at[3 * t + j], device_id=(px, py, c), device_id_type=MESH)
                cp.wait_send()
                cp.wait_recv()

    shapes = [pltpu.HBM(v.shape, v.dtype) for v in sums]
    out = pl.pallas_call(
        body, name=name, out_shape=tuple(shapes + shapes),
        in_specs=tuple([HBM] * (2 * nt) + [SEM, SEM, pl.BlockSpec(memory_space=pl.ANY)]),
        out_specs=tuple([HBM] * (2 * nt)), input_output_aliases={t: t for t in range(2 * nt)},
        compiler_params=pltpu.CompilerParams(has_side_effects=EFFECT),
    )(*sums, *land, send_sems, recv_sems, after)
    return list(out[:nt]), list(out[nt:])


def _rs_chip_sum(name, pos, sums, parts):
    r, cdim = parts.shape[1], parts.shape[2]
    tr = r // 4 if (r // 4) % 16 == 0 else r // 2
    nblk = r // tr

    def body(pos_ref, own_ref, a_ref, b_ref, c_ref, o_ref):
        o_ref[...] = ((own_ref[...].astype(f32) + a_ref[...].astype(f32)) + b_ref[...].astype(f32)) \
            + c_ref[...].astype(f32)

    def slot(k):
        return pl.BlockSpec((None, tr, cdim), lambda i, p: ((p[0] + k) % NCHIP, i, 0))

    return pl.pallas_call(
        body, name=name, out_shape=jax.ShapeDtypeStruct((2 * r, cdim), f32),
        grid_spec=pltpu.PrefetchScalarGridSpec(
            num_scalar_prefetch=1, grid=(nblk,), in_specs=[slot(0), slot(1), slot(2), slot(3)],
            out_specs=pl.BlockSpec((tr, cdim), lambda i, p: (p[1] * nblk + i, 0))),
        compiler_params=_cp(("parallel",)),
    )(pos, sums, parts, parts, parts)


def _rs_share_halves(name, bufs):
    nt = len(bufs)

    def body(*refs):
        outs = refs[nt:2 * nt]
        send_sems, recv_sems = refs[2 * nt:]
        x, y, c, _ = _place()
        cps = []
        for t in range(nt):
            r = outs[t].shape[0] // 2
            mine = outs[t].at[pl.ds(c * r, r), :]
            theirs = outs[t].at[pl.ds((1 - c) * r, r), :]
            cp = pltpu.make_async_remote_copy(
                src_ref=mine, dst_ref=mine, send_sem=send_sems.at[t], recv_sem=recv_sems.at[t],
                device_id=(x, y, 1 - c), device_id_type=MESH)
            cp.start()
            cps.append((cp, theirs))
        for t, (cp, theirs) in enumerate(cps):
            pltpu.make_async_remote_copy(
                src_ref=theirs, dst_ref=theirs, send_sem=send_sems.at[t], recv_sem=recv_sems.at[t],
                device_id=(x, y, 1 - c), device_id_type=MESH).wait_recv()
            cp.wait_send()

    return pl.pallas_call(
        body, name=name, out_shape=[jax.ShapeDtypeStruct(v.shape, v.dtype) for v in bufs],
        in_specs=_hbm_specs(nt), out_specs=_hbm_specs(nt), input_output_aliases={t: t for t in range(nt)},
        scratch_shapes=[pltpu.SemaphoreType.DMA((nt,)), pltpu.SemaphoreType.DMA((nt,))],
    )(*bufs)


BIG = ("w_in", "w_out", "w_gate", "w_up", "w_down")


GROUPS = (("w_in",), ("w_out",), ("w_gate", "w_up", "w_down"))


class _Ship:
    def __init__(self, tag, pos):
        self.tag, self.pos, self.started, self.state = tag, pos, None, None

    def early(self, grads):
        self.early_keys = tuple(grads)
        self.started = _sibling_start(self.tag + "sibling_start", [grads[k] for k in self.early_keys])
        return self.started[-1]

    def finish(self, grads, after):
        keys, mine, got = tuple(grads), [grads[k] for k in grads], []
        if keys:
            got = list(_rs_to_sibling(self.tag + "to_sibling", mine))
        if self.started is not None:
            send_sems, recv_sems, src, land, _ = self.started
            src, land = _sibling_wait(self.tag + "sibling_wait", src, land, send_sems, recv_sems, after)
            keys, mine, got = self.early_keys + keys, src + mine, land + got
        sums = [_rs_pair_sum(self.tag + "pair_sum_" + k, self.pos, g, o) for k, g, o in zip(keys, mine, got)]
        self.state = (keys,) + _scatter_start(self.tag + "start", sums)
        return self.state[-1]


def _rs_end(tag, state, pos, after):
    keys, send_sems, recv_sems, sums, land, _ = state
    sums, land = _scatter_wait(tag + "wait", sums, land, send_sems, recv_sems, after)
    halves = [_rs_chip_sum(tag + "chip_sum_" + k, pos, s, v) for k, s, v in zip(keys, sums, land)]
    return dict(zip(keys, _rs_share_halves(tag + "share", halves)))


def _adam_math(w, g, m, v):
    m = ADAM_B1 * m + (1.0 - ADAM_B1) * g
    v = ADAM_B2 * v + (1.0 - ADAM_B2) * (g * g)
    m_hat = m / (1.0 - ADAM_B1 ** ADAM_STEP)
    v_hat = v / (1.0 - ADAM_B2 ** ADAM_STEP)
    delta = -ADAM_LR * (m_hat / (jnp.sqrt(v_hat) + ADAM_EPS) + ADAM_WD * w)
    return delta, m, v


def _adam_big(name, g0, g1, w, m, v):
    _, r, cdim = w.shape
    tr = 128 if r % 128 == 0 else 64
    nb = r // tr

    def body(g0_ref, g1_ref, w_ref, m_ref, v_ref, go_ref, d_ref, mo_ref, vo_ref):
        l = pl.program_id(0)
        g = jnp.where(l == 0, g0_ref[...], g1_ref[...])
        delta, mn, vn = _adam_math(w_ref[...], g, m_ref[...], v_ref[...])
        go_ref[...] = g
        d_ref[...] = delta
        mo_ref[...] = mn
        vo_ref[...] = vn

    lay = pl.BlockSpec((None, tr, cdim), lambda l, i: (l, i, 0))
    return pl.pallas_call(
        body, name=name, grid=(2, nb),
        in_specs=[pl.BlockSpec((tr, cdim), lambda l, i: (i * (1 - l) + (nb - 1) * l, 0)),
                  pl.BlockSpec((tr, cdim), lambda l, i: (i * l, 0)), lay, lay, lay],
        out_specs=[lay] * 4, out_shape=[jax.ShapeDtypeStruct(w.shape, f32)] * 4,
        compiler_params=_cp(("arbitrary", "arbitrary")),
    )(g0, g1, w, m, v)


def _adam_w_in(name, g0, g1, w, m, v):
    wt, mt, vt = (jnp.transpose(a, (2, 0, 1)) for a in (w, m, v))
    cols, _, rows = wt.shape

    def body(g0_ref, g1_ref, w_ref, m_ref, v_ref, go_ref, d_ref, mo_ref, vo_ref):
        for l, g_ref in enumerate((g0_ref, g1_ref)):
            g = jnp.transpose(g_ref[...])
            delta, mn, vn = _adam_math(w_ref[:, l, :], g, m_ref[:, l, :], v_ref[:, l, :])
            go_ref[:, l, :] = g
            d_ref[:, l, :] = delta
            mo_ref[:, l, :] = mn
            vo_ref[:, l, :] = vn

    blk = pl.BlockSpec((LANE, 2, rows), lambda i: (i, 0, 0))
    gblk = pl.BlockSpec((rows, LANE), lambda i: (0, i))
    outs = pl.pallas_call(
        body, name=name, grid=(pl.cdiv(cols, LANE),), in_specs=[gblk, gblk, blk, blk, blk], out_specs=[blk] * 4,
        out_shape=[jax.ShapeDtypeStruct(wt.shape, f32)] * 4, compiler_params=_cp(("parallel",)),
    )(g0, g1, wt, mt, vt)
    return tuple(jnp.transpose(o, (1, 2, 0)) for o in outs)


def _sum8(name, gathered):
    m_per = gathered.shape[0] // 8

    def body(g_ref, o_ref):
        tot = g_ref[pl.ds(0, m_per), :]
        for d in range(1, 8):
            tot = tot + g_ref[pl.ds(d * m_per, m_per), :]
        o_ref[...] = tot

    return pl.pallas_call(body, name=name, out_shape=jax.ShapeDtypeStruct((m_per, LANE), f32))(gathered)


def _adam_small(name, g, w, m, v):
    def body(g_ref, w_ref, m_ref, v_ref, d_ref, mo_ref, vo_ref):
        delta, mn, vn = _adam_math(w_ref[...], g_ref[...], m_ref[...], v_ref[...])
        d_ref[...] = delta
        mo_ref[...] = mn
        vo_ref[...] = vn

    return pl.pallas_call(body, name=name, out_shape=[jax.ShapeDtypeStruct(g.shape, f32)] * 3)(g, w, m, v)


def _pack(vals):
    rows, offs, at = [], [], 0
    for a in vals:
        a = a.reshape(-1)
        n = -(-a.shape[0] // (8 * LANE)) * 8
        rows.append(jnp.pad(a, (0, n * LANE - a.shape[0])).reshape(n, LANE))
        offs.append(at)
        at += n
    return jnp.concatenate(rows, axis=0), offs


def _unpack(packed, offs, shapes):
    out = []
    for o, shp in zip(offs, shapes):
        size = 1
        for d in shp:
            size *= d
        n = -(-size // LANE)
        out.append(packed[o:o + n].reshape(-1)[:size].reshape(shp))
    return out


SMALL = ("norm_mix_g", "conv_w", "pool_w", "pool_scale", "gla_w_decay", "gla_b_decay", "gla_out_g",
         "fox_q_g", "fox_k_g", "fox_b_f", "norm_ffn_g")
ALL = ("norm_mix_g", "w_in", "conv_w", "pool_w", "pool_scale", "gla_w_decay", "gla_b_decay", "gla_out_g",
       "fox_q_g", "fox_k_g", "fox_b_f", "w_out", "norm_ffn_g", "w_gate", "w_up", "w_down")


def kernel(x, norm_mix_g, w_in, conv_w, pool_w, pool_scale, gla_w_decay, gla_b_decay, gla_out_g, fox_q_g, fox_k_g, fox_b_f, w_out, norm_ffn_g, w_gate, w_up, w_down, loss_target, m_norm_mix_g, m_w_in, m_conv_w, m_pool_w, m_pool_scale, m_gla_w_decay, m_gla_b_decay, m_gla_out_g, m_fox_q_g, m_fox_k_g, m_fox_b_f, m_w_out, m_norm_ffn_g, m_w_gate, m_w_up, m_w_down, v_norm_mix_g, v_w_in, v_conv_w, v_pool_w, v_pool_scale, v_gla_w_decay, v_gla_b_decay, v_gla_out_g, v_fox_q_g, v_fox_k_g, v_fox_b_f, v_w_out, v_norm_ffn_g, v_w_gate, v_w_up, v_w_down):
    w = dict(norm_mix_g=norm_mix_g, w_in=w_in, conv_w=conv_w, pool_w=pool_w, pool_scale=pool_scale,
             gla_w_decay=gla_w_decay, gla_b_decay=gla_b_decay, gla_out_g=gla_out_g, fox_q_g=fox_q_g, fox_k_g=fox_k_g,
             fox_b_f=fox_b_f, w_out=w_out, norm_ffn_g=norm_ffn_g, w_gate=w_gate, w_up=w_up, w_down=w_down)
    m = dict(norm_mix_g=m_norm_mix_g, w_in=m_w_in, conv_w=m_conv_w, pool_w=m_pool_w, pool_scale=m_pool_scale,
             gla_w_decay=m_gla_w_decay, gla_b_decay=m_gla_b_decay, gla_out_g=m_gla_out_g, fox_q_g=m_fox_q_g,
             fox_k_g=m_fox_k_g, fox_b_f=m_fox_b_f, w_out=m_w_out, norm_ffn_g=m_norm_ffn_g, w_gate=m_w_gate,
             w_up=m_w_up, w_down=m_w_down)
    v = dict(norm_mix_g=v_norm_mix_g, w_in=v_w_in, conv_w=v_conv_w, pool_w=v_pool_w, pool_scale=v_pool_scale,
             gla_w_decay=v_gla_w_decay, gla_b_decay=v_gla_b_decay, gla_out_g=v_gla_out_g, fox_q_g=v_fox_q_g,
             fox_k_g=v_fox_k_g, fox_b_f=v_fox_b_f, w_out=v_w_out, norm_ffn_g=v_norm_ffn_g, w_gate=v_w_gate,
             w_up=v_w_up, w_down=v_w_down)
    chip = 2 * lax.axis_index("x") + lax.axis_index("y")

    pos = jnp.stack([chip, lax.axis_index("c")]).astype(jnp.int32)

    mine, offs = _pack([conv_w, gla_w_decay, jnp.zeros((8, LANE), f32)])

    def casts(l, keys, dep):
        return [(_cast_slot_w_in if k == "w_in" else _cast_slot)(f"l{l}_cast_{k}", w[k], l, pos, dep) for k in keys]

    sems, gbufs, token = _gather_start("gather_start", [[_own_slot(mine, chip)], casts(0, ("w_in",), None)])
    more = [casts(0, grp, token) for grp in GROUPS[1:]]
    sems_b, gbufs_b, token = _gather_start("l0_gather_start_rest", more)
    more = [casts(1, grp, token) for grp in GROUPS]
    sems_c, gbufs_c, started1 = _gather_start("l1_gather_start", more)
    sems, gbufs = sems + sems_b + sems_c, gbufs + gbufs_b + gbufs_c

    def gathered(tag, gi, after):
        got = _gather_wait(tag + "_wait", gbufs[gi], sems[gi][0], sems[gi][1], after)
        return _gather_exchange(tag + "_exchange", got)

    def weights(l, group, after):
        if (l, group) == (0, "w_in"):
            after = started1
        got = gathered(f"l{l}_gather_{group}", 1 + 3 * l + ("w_in", "w_out", "ffn").index(group), after)
        if group == "w_in":
            return _win_from_blocks(got[0])
        return got[0] if group == "w_out" else got

    every = gathered("gather_small", 0, x)[0]
    per_chip = [_unpack(every[kk], offs, [conv_w.shape, gla_w_decay.shape]) for kk in range(NCHIP)]
    conv_full = jnp.concatenate([pc[0] for pc in per_chip], axis=-1)[:, :, 0, :]
    wdec_full = jnp.concatenate([pc[1] for pc in per_chip], axis=-1)

    wdec = jnp.pad(wdec_full.reshape(2, 16, 4, 64), ((0, 0), (GA_LANE, LANE - GA_LANE - 16), (0, 0), (0, 64)))
    p = dict(
        g_mix=norm_mix_g[:, None, :], g_ffn=norm_ffn_g[:, None, :],
        conv_w=conv_full, pool_w=pool_w, pool_scale=pool_scale[:, None, :],
        wdec=wdec.reshape(2, LANE, G),
        bdec=jnp.pad(gla_b_decay.reshape(2, 4, 64), ((0, 0), (0, 0), (0, 64))).reshape(2, 1, G),
        gla_og=gla_out_g[:, None, :], fox_qg=fox_q_g[:, None, :], fox_kg=fox_k_g[:, None, :],
        fox_bf=jnp.pad(fox_b_f, ((0, 0), (FF_LANE, LANE - FF_LANE - 4)))[:, None, :])

    h0 = x[0]
    h1, sv0 = _layer_fwd(0, h0, p, weights)
    h2, sv1 = _layer_fwd(1, h1, p, weights)
    sq, dh, dhb = _loss("loss", h2, loss_target[0])
    loss = lax.psum(sq[0, 0] * (0.5 / D), ("x", "y", "c"))

    late = ("w_out", "w_gate", "w_up", "w_down")
    ships = {tag: _Ship(tag, pos) for tag in ("l1_rs_a_", "l1_rs_b_", "l0_rs_a_", "l0_rs_b_")}
    dh1, dy, dgf1 = _layer_bwd_ffn(1, dh, dhb, p, sv1, ships["l1_rs_a_"])
    dh, dhb, small1 = _layer_bwd_mix(1, dh1, dy, p, sv1, ships["l1_rs_b_"])
    dh1, dy, dgf0 = _layer_bwd_ffn(0, dh, dhb, p, sv0, ships["l0_rs_a_"])
    dh, dhb, small0 = _layer_bwd_mix(0, dh1, dy, p, sv0, ships["l0_rs_b_"])
    small0["norm_ffn_g"], small1["norm_ffn_g"] = dgf0[0], dgf1[0]

    red1 = _rs_end("l1_rs_a_", ships["l1_rs_a_"].state, pos, dh)
    red1.update(_rs_end("l1_rs_b_", ships["l1_rs_b_"].state, pos, red1["w_down"]))
    red0 = _rs_end("l0_rs_a_", ships["l0_rs_a_"].state, pos, red1["w_in"])
    grads, deltas, new_m, new_v = {}, {}, {}, {}
    for k in late:
        grads[k], deltas[k], new_m[k], new_v[k] = _adam_big("adam_" + k, red0[k], red1[k], w[k], m[k], v[k])

    packed, goffs = _pack([jnp.stack([small0[k], small1[k]]) for k in SMALL])
    total = _sum8("sum_small_grads", _allgather_small("gather_small_grads", packed, new_v["w_down"]))
    red0.update(_rs_end("l0_rs_b_", ships["l0_rs_b_"].state, pos, total))
    k = "w_in"
    grads[k], deltas[k], new_m[k], new_v[k] = _adam_w_in("adam_" + k, red0[k], red1[k], w[k], m[k], v[k])
    full_shapes = [(2,) + small0[k].shape for k in SMALL]
    gsmall = dict(zip(SMALL, _unpack(total, goffs, full_shapes)))
    gsmall["conv_w"] = lax.dynamic_slice_in_dim(gsmall["conv_w"], chip * LANE, LANE, axis=2)[:, :, None, :]
    gsmall["gla_w_decay"] = lax.dynamic_slice_in_dim(gsmall["gla_w_decay"], chip * 64, 64, axis=2)
    gp, loffs = _pack([gsmall[k] for k in SMALL])
    wp, _ = _pack([w[k] for k in SMALL])
    mp, _ = _pack([m[k] for k in SMALL])
    vp, _ = _pack([v[k] for k in SMALL])
    dp, mnp, vnp = _adam_small("adam_small", gp, wp, mp, vp)
    shapes = [w[k].shape for k in SMALL]
    for k, a, b, c_, d_ in zip(SMALL, _unpack(gp, loffs, shapes), _unpack(dp, loffs, shapes),
                               _unpack(mnp, loffs, shapes), _unpack(vnp, loffs, shapes)):
        grads[k], deltas[k], new_m[k], new_v[k] = a, b, c_, d_

    return (loss, dh[None], *[grads[k] for k in ALL], *[deltas[k] for k in ALL],
            *[new_m[k] for k in ALL], *[new_v[k] for k in ALL])
```

```python
import functools

import jax
import jax.numpy as jnp
from jax import lax
from jax.experimental import pallas as pl
from jax.experimental.pallas import tpu as pltpu

f32 = jnp.float32
bf16 = jnp.bfloat16

D = 2048
G = 512
DFF = 5632
NCHIP = 4
FB = DFF // NCHIP
WIN = 5140
WINB = WIN // NCHIP
EPS = 1e-6
CHUNK = 64
LANE = 128

CB, CC, CH, PU, GQ, GK, GV, GG, FQ, FK, FV, MISC = 0, 4, 8, 12, 16, 20, 24, 28, 32, 36, 40, 44
ZC = 45 * LANE
FF_LANE = 0
GA_LANE = 8

ADAM_LR, ADAM_B1, ADAM_B2, ADAM_EPS, ADAM_WD, ADAM_STEP = 0.001, 0.9, 0.999, 1e-08, 0.01, 10

VMEM_LIMIT = 60 * 1024 * 1024
MESH = pl.DeviceIdType.MESH


def _cp(sem=None):
    return pltpu.CompilerParams(dimension_semantics=sem, vmem_limit_bytes=VMEM_LIMIT)


def _dot(a, b, dims=((1,), (0,))):
    return lax.dot_general(a.astype(bf16), b.astype(bf16), (dims, ((), ())), preferred_element_type=f32)


def _bdot(a, b, ca, cb):
    return lax.dot_general(a.astype(bf16), b.astype(bf16), (((ca,), (cb,)), ((0,), (0,))),
                           preferred_element_type=f32)


def _log_sigmoid(x):
    return jnp.minimum(x, 0.0) - jnp.log(1.0 + jnp.exp(-jnp.abs(x)))


@jax.custom_vjp
def _sigmoid(x):
    return 1.0 / (1.0 + jnp.exp(-x))


def _sigmoid_fwd(x):
    s = _sigmoid(x)
    return s, s


def _sigmoid_bwd(s, g):
    return (g * s * (1.0 - s),)


_sigmoid.defvjp(_sigmoid_fwd, _sigmoid_bwd)


def _rms(x, g):
    return x * lax.rsqrt(jnp.mean(x * x, axis=-1, keepdims=True) + EPS) * g


def _shift_impl(x, n, period, transpose):
    rows = x.shape[0]
    t = lax.broadcasted_iota(jnp.int32, x.shape, 0)
    if period is not None:
        t = t & (period - 1)
    keep = t >= n
    if not transpose:
        return jnp.where(keep, pltpu.roll(x, n, 0), 0.0)
    return pltpu.roll(jnp.where(keep, x, 0.0), rows - n, 0)


def _shift(x, n, period=None):
    @jax.custom_vjp
    def f(v):
        return _shift_impl(v, n, period, False)

    def fwd(v):
        return f(v), None

    def bwd(_, g):
        return (_shift_impl(g, n, period, True),)

    f.defvjp(fwd, bwd)
    return f(x)


def _cumsum_rows(x, length, period=None):
    n = 1
    while n < length:
        x = x + _shift(x, n, period)
        n *= 2
    return x


def _convpool_fn(cb, cc, ch, pu, w0, w1, w2, pw, ps, j):
    u = cc * ch
    y = w2 * u + w1 * _shift(u, 1) + w0 * _shift(u, 2)
    ya = cb * y
    s2 = pu + _shift(pu, 1)
    s4 = s2 + _shift(s2, 2)
    s8 = s4 + _shift(s4, 4)
    s16 = s8 + _shift(s8, 8)
    wsum = jnp.where(j == 0, s2, jnp.where(j == 1, s4, jnp.where(j == 2, s8, s16)))
    width = (2 << j).astype(f32)
    t = lax.broadcasted_iota(jnp.int32, pu.shape, 0).astype(f32)
    count = jnp.minimum(t + 1.0, width)
    d = wsum / count - pu
    yb = _dot(d, pw) * ps
    return ya, yb


def _foxprep_fn(misc, bf):
    lf = _log_sigmoid(misc + bf)
    fc = _cumsum_rows(lf, lf.shape[0])
    return fc, jnp.transpose(fc)


def _fox_fn(q, k, v, fcol, frow8, qg, kg, h, i):
    tq, s = q.shape[0], k.shape[0]
    qn = _rms(q, qg)
    kn = _rms(k, kg)
    lg = _dot(qn, kn, ((1,), (1,))) * (LANE ** -0.5)
    lane = lax.broadcasted_iota(jnp.int32, fcol.shape, 1)
    fq = jnp.sum(jnp.where(lane == h, fcol, 0.0), axis=1, keepdims=True)
    row = lax.broadcasted_iota(jnp.int32, frow8.shape, 0)
    fk = jnp.sum(jnp.where(row == h, frow8, 0.0), axis=0, keepdims=True)
    lg = lg + fq - fk
    qpos = i * tq + lax.broadcasted_iota(jnp.int32, (tq, s), 0)
    kpos = lax.broadcasted_iota(jnp.int32, (tq, s), 1)
    lg = jnp.where(kpos <= qpos, lg, -jnp.inf)
    m = lax.stop_gradient(jnp.max(lg, axis=1, keepdims=True))
    e = jnp.exp(lg - m)
    p = e / jnp.sum(e, axis=1, keepdims=True)
    return _dot(p, v)


def _gla1_fn(q, k, v, misc, wd, bd):
    ts = q.shape[0]
    nb = ts // CHUNK
    x = _dot(misc, wd) + bd
    la = _log_sigmoid(x) * (1.0 / 16.0)
    cc = _cumsum_rows(la, CHUNK, CHUNK)
    la3 = la.reshape(nb, CHUNK, LANE)
    last3 = jnp.sum(la3, axis=1, keepdims=True)
    last2 = jnp.sum(la3, axis=1)
    cc3 = cc.reshape(nb, CHUNK, LANE)
    q3 = (q * 0.125).reshape(nb, CHUNK, LANE)
    k3 = k.reshape(nb, CHUNK, LANE)
    v3 = v.reshape(nb, CHUNK, LANE)
    ep = jnp.exp(cc3)
    en = jnp.exp(-cc3)
    qe = q3 * ep
    a1 = _bdot(qe, k3 * en, 2, 2)
    a2 = _bdot(q3 * en, k3 * ep, 2, 2)
    ti = lax.broadcasted_iota(jnp.int32, a1.shape, 1)
    si = lax.broadcasted_iota(jnp.int32, a1.shape, 2)
    sc = jnp.where(si <= ti, a1, a2)
    oi = _bdot(sc, v3, 2, 1)
    kd = k3 * jnp.exp(last3 - cc3)
    el = jnp.exp(last2)
    return qe.reshape(ts, LANE), kd.reshape(ts, LANE), el, oi.reshape(ts, LANE)


def _gla3_fn(o, gg, og):
    return _rms(o, og) * (gg * _sigmoid(gg))


def _ffn_fn(gate, up):
    return gate * _sigmoid(gate) * up


def _mm(name, a, b, a_spec, b_spec, out_shape, out_spec, grid, dims, nk, res=None, res_spec=None, dep=None,
        second=None):
    has_res = res is not None
    has_dep = dep is not None
    has_two = second is not None
    nax = len(grid)
    first_out = 2 + 2 * has_two + has_res + has_dep

    def body(*refs):
        a_ref, b_ref = refs[0], refs[1]
        res_ref = refs[2 + 2 * has_two] if has_res else None
        out_ref = refs[first_out]
        part = _dot(a_ref[...], b_ref[...], dims)
        if has_two:
            part = part + _dot(refs[2][...], refs[3][...], dims)
        if nk == 1:
            if has_res:
                part = part + res_ref[...]
            out_ref[...] = part.astype(out_ref.dtype)
            return
        acc_ref = refs[first_out + 1]
        k = pl.program_id(nax - 1)

        @pl.when(k == 0)
        def _():
            acc_ref[...] = part

        @pl.when(k > 0)
        def _():
            acc_ref[...] += part

        @pl.when(k == nk - 1)
        def _():
            tot = acc_ref[...]
            if has_res:
                tot = tot + res_ref[...]
            out_ref[...] = tot.astype(out_ref.dtype)

    ops = [a, b] + (list(second) if has_two else []) + ([res] if has_res else []) + ([dep] if has_dep else [])
    specs = [a_spec, b_spec] + ([a_spec, b_spec] if has_two else []) + ([res_spec] if has_res else [])
    if has_dep:
        specs.append(pl.BlockSpec((8, LANE), lambda *_: (0, 0)))
    blk = tuple(d for d in out_spec.block_shape if d is not None)
    scratch = [pltpu.VMEM(blk, f32)] if nk > 1 else []
    return pl.pallas_call(
        body, name=name, grid=grid, in_specs=specs, out_specs=out_spec, out_shape=out_shape,
        scratch_shapes=scratch,
        compiler_params=_cp(("parallel",) * (nax - 1) + ("arbitrary",)),
    )(*ops)


def _tm(s):
    return min(s, 512)


def _tm_big(s):
    return min(s, 1024)


def _mm_gate_up(name, u, w_gate, w_up):
    s = u.shape[0]
    tm = _tm(s)

    def body(u_ref, wg_ref, wu_ref, g_ref, up_ref, a_ref):
        g = _dot(u_ref[...], wg_ref[...])
        up = _dot(u_ref[...], wu_ref[...])
        g_ref[...] = g
        up_ref[...] = up
        a_ref[...] = _ffn_fn(g, up).astype(bf16)

    wspec = pl.BlockSpec((None, D, FB), lambda j, i: (j, 0, 0))
    tile = pl.BlockSpec((tm, FB), lambda j, i: (i, j))
    return pl.pallas_call(
        body, name=name, grid=(NCHIP, s // tm),
        in_specs=[pl.BlockSpec((tm, D), lambda j, i: (i, 0)), wspec, wspec], out_specs=[tile, tile, tile],
        out_shape=[jax.ShapeDtypeStruct((s, DFF), f32), jax.ShapeDtypeStruct((s, DFF), f32),
                   jax.ShapeDtypeStruct((s, DFF), bf16)],
        compiler_params=_cp(("parallel", "parallel")),
    )(u, w_gate, w_up)


def _mm_dact(name, dh, w_down, gate, up):
    s = dh.shape[0]
    tm = _tm(s)

    def body(dh_ref, wd_ref, g_ref, up_ref, dg_ref, du_ref):
        dact = _dot(dh_ref[...], wd_ref[...], ((1,), (1,)))
        _, vjp = jax.vjp(_ffn_fn, g_ref[...], up_ref[...])
        dg, du = vjp(dact)
        dg_ref[...] = dg.astype(bf16)
        du_ref[...] = du.astype(bf16)

    tile = pl.BlockSpec((tm, FB), lambda j, i: (i, j))
    return pl.pallas_call(
        body, name=name, grid=(NCHIP, s // tm),
        in_specs=[pl.BlockSpec((tm, D), lambda j, i: (i, 0)), pl.BlockSpec((None, FB, D), lambda j, i: (j, 0, 0)),
                  tile, tile],
        out_specs=[tile, tile], out_shape=[jax.ShapeDtypeStruct((s, DFF), bf16)] * 2,
        compiler_params=_cp(("parallel", "parallel")),
    )(dh, w_down, gate, up)


def _rmsnorm_fwd(name, x, g, l):
    s = x.shape[0]
    tm = min(s, 256)

    def body(x_ref, g_ref, u_ref):
        u_ref[...] = _rms(x_ref[...], g_ref[...]).astype(bf16)

    return pl.pallas_call(
        body, name=name, grid=(s // tm,),
        in_specs=[pl.BlockSpec((tm, D), lambda i: (i, 0)), pl.BlockSpec((None, 1, D), lambda i: (l, 0, 0))],
        out_specs=pl.BlockSpec((tm, D), lambda i: (i, 0)),
        out_shape=jax.ShapeDtypeStruct((s, D), bf16), compiler_params=_cp(("parallel",)),
    )(x, g)


def _rmsnorm_bwd(name, x, g, du, dres, l, dep=None):
    s = x.shape[0]
    tm = min(s, 256)

    def body(x_ref, g_ref, du_ref, dres_ref, *rest):
        dx_ref, dxb_ref, dg_ref = rest[-3:]
        _, vjp = jax.vjp(_rms, x_ref[...], g_ref[...])
        dx, dg = vjp(du_ref[...])
        tot = dx + dres_ref[...]
        dx_ref[...] = tot
        dxb_ref[...] = tot.astype(bf16)

        @pl.when(pl.program_id(0) == 0)
        def _():
            dg_ref[...] = dg

        @pl.when(pl.program_id(0) > 0)
        def _():
            dg_ref[...] += dg

    row = pl.BlockSpec((tm, D), lambda i: (i, 0))
    deps = [] if dep is None else [dep]
    return pl.pallas_call(
        body, name=name, grid=(s // tm,),
        in_specs=[row, pl.BlockSpec((None, 1, D), lambda i: (l, 0, 0)), row, row]
        + [pl.BlockSpec((8, LANE), lambda i: (0, 0)) for _ in deps],
        out_specs=[row, row, pl.BlockSpec((1, D), lambda i: (0, 0))],
        out_shape=[jax.ShapeDtypeStruct((s, D), f32), jax.ShapeDtypeStruct((s, D), bf16),
                   jax.ShapeDtypeStruct((1, D), f32)],
        compiler_params=_cp(("arbitrary",)),
    )(x, g, du, dres, *deps)


def _loss(name, y, t):
    s = y.shape[0]
    tm = min(s, 256)
    row = pl.BlockSpec((tm, D), lambda i: (i, 0))

    def body(y_ref, t_ref, l_ref, d_ref, db_ref):
        e = y_ref[...] - t_ref[...]
        d = e * (1.0 / D)
        d_ref[...] = d
        db_ref[...] = d.astype(bf16)
        part = jnp.zeros((8, LANE), f32) + jnp.sum(e * e)

        @pl.when(pl.program_id(0) == 0)
        def _():
            l_ref[...] = part

        @pl.when(pl.program_id(0) > 0)
        def _():
            l_ref[...] += part

    return pl.pallas_call(
        body, name=name, grid=(s // tm,), in_specs=[row, row],
        out_specs=[pl.BlockSpec((8, LANE), lambda i: (0, 0)), row, row],
        out_shape=[jax.ShapeDtypeStruct((8, LANE), f32), jax.ShapeDtypeStruct((s, D), f32),
                   jax.ShapeDtypeStruct((s, D), bf16)],
        compiler_params=_cp(("arbitrary",)),
    )(y, t)


def _zspec(s, blk):
    return pl.BlockSpec((s, LANE), lambda j: (0, blk + j))


def _convpool_specs(s, l):
    return [_zspec(s, CB), _zspec(s, CC), _zspec(s, CH), _zspec(s, PU),
            pl.BlockSpec((None, 3, LANE), lambda j: (l, 0, j)),
            pl.BlockSpec((None, None, LANE, LANE), lambda j: (l, j, 0, 0)),
            pl.BlockSpec((None, 1, LANE), lambda j: (l, 0, j))]


def _convpool_fwd(name, z, conv_w, pool_w, pool_scale, l):
    s = z.shape[0]

    def body(cb, cc, ch, pu, cw, pw, ps, ya_ref, yb_ref):
        ya, yb = _convpool_fn(cb[...], cc[...], ch[...], pu[...], cw[0:1, :], cw[1:2, :], cw[2:3, :], pw[...], ps[...],
                              pl.program_id(0))
        ya_ref[...] = ya.astype(bf16)
        yb_ref[...] = yb.astype(bf16)

    col = pl.BlockSpec((s, LANE), lambda j: (0, j))
    return pl.pallas_call(
        body, name=name, grid=(4,), in_specs=_convpool_specs(s, l), out_specs=[col, col],
        out_shape=[jax.ShapeDtypeStruct((s, G), bf16)] * 2, compiler_params=_cp(("parallel",)),
    )(z, z, z, z, conv_w, pool_w, pool_scale)


def _convpool_bwd(name, z, conv_w, pool_w, pool_scale, dy, l):
    s = z.shape[0]

    def body(cb, cc, ch, pu, cw, pw, ps, dya, dyb, dcb, dcc, dch, dpu, dcw, dpw, dps):
        j = pl.program_id(0)
        fn = functools.partial(_convpool_fn, j=j)
        _, vjp = jax.vjp(fn, cb[...], cc[...], ch[...], pu[...], cw[0:1, :], cw[1:2, :], cw[2:3, :], pw[...], ps[...])
        g = vjp((dya[...], dyb[...]))
        dcb[...] = g[0].astype(bf16)
        dcc[...] = g[1].astype(bf16)
        dch[...] = g[2].astype(bf16)
        dpu[...] = g[3].astype(bf16)
        dcw[0:1, :] = g[4]
        dcw[1:2, :] = g[5]
        dcw[2:3, :] = g[6]
        dpw[...] = g[7]
        dps[...] = g[8]

    col = pl.BlockSpec((s, LANE), lambda j: (0, j))
    specs = _convpool_specs(s, l) + [pl.BlockSpec((s, LANE), lambda j: (0, j)),
                                     pl.BlockSpec((s, LANE), lambda j: (0, 4 + j))]
    return pl.pallas_call(
        body, name=name, grid=(4,), in_specs=specs,
        out_specs=[col, col, col, col, pl.BlockSpec((3, LANE), lambda j: (0, j)),
                   pl.BlockSpec((None, LANE, LANE), lambda j: (j, 0, 0)), pl.BlockSpec((1, LANE), lambda j: (0, j))],
        out_shape=[jax.ShapeDtypeStruct((s, G), bf16)] * 4 + [
            jax.ShapeDtypeStruct((3, G), f32), jax.ShapeDtypeStruct((4, LANE, LANE), f32),
            jax.ShapeDtypeStruct((1, G), f32)],
        compiler_params=_cp(("parallel",)),
    )(z, z, z, z, conv_w, pool_w, pool_scale, dy, dy)


def _foxprep_fwd(name, z, bf, l):
    s = z.shape[0]

    def body(m_ref, b_ref, fc_ref, fr_ref):
        fc, fr = _foxprep_fn(m_ref[...], b_ref[...])
        fc_ref[...] = fc
        fr_ref[...] = fr

    return pl.pallas_call(
        body, name=name, grid=(1,),
        in_specs=[pl.BlockSpec((s, LANE), lambda i: (0, MISC)), pl.BlockSpec((None, 1, LANE), lambda i: (l, 0, 0))],
        out_specs=[pl.BlockSpec((s, LANE), lambda i: (0, 0)), pl.BlockSpec((LANE, s), lambda i: (0, 0))],
        out_shape=[jax.ShapeDtypeStruct((s, LANE), f32), jax.ShapeDtypeStruct((LANE, s), f32)],
        compiler_params=_cp(("arbitrary",)),
    )(z, bf)


def _foxprep_bwd(name, z, bf, dfc4, dfr4, dmisc4, l):
    s = z.shape[0]

    def body(m_ref, b_ref, dfc_ref, dfr_ref, dm4_ref, dm_ref, db_ref):
        _, vjp = jax.vjp(_foxprep_fn, m_ref[...], b_ref[...])
        dfc = dfc_ref[0] + dfc_ref[1] + dfc_ref[2] + dfc_ref[3]
        dfr = dfr_ref[0] + dfr_ref[1] + dfr_ref[2] + dfr_ref[3]
        dfr = jnp.concatenate([dfr, jnp.zeros((LANE - 8, s), f32)], axis=0)
        dm, db = vjp((dfc, dfr))
        dm = dm + (dm4_ref[0] + dm4_ref[1] + dm4_ref[2] + dm4_ref[3])
        dm_ref[...] = dm.astype(bf16)
        db_ref[...] = db

    whole = lambda shape: pl.BlockSpec(shape, lambda i: (0,) * len(shape))
    return pl.pallas_call(
        body, name=name, grid=(1,),
        in_specs=[pl.BlockSpec((s, LANE), lambda i: (0, MISC)), pl.BlockSpec((None, 1, LANE), lambda i: (l, 0, 0)),
                  whole((4, s, LANE)), whole((4, 8, s)), whole((4, s, LANE))],
        out_specs=[whole((s, LANE)), whole((1, LANE))],
        out_shape=[jax.ShapeDtypeStruct((s, LANE), bf16), jax.ShapeDtypeStruct((1, LANE), f32)],
        compiler_params=_cp(("arbitrary",)),
    )(z, bf, dfc4, dfr4, dmisc4)


FOX_TQ = 256


def _fox_specs(s, l):
    return [pl.BlockSpec((s, LANE), lambda h: (0, FQ + h)),
            pl.BlockSpec((s, LANE), lambda h: (0, FK + h)),
            pl.BlockSpec((s, LANE), lambda h: (0, FV + h)),
            pl.BlockSpec((s, LANE), lambda h: (0, 0)),
            pl.BlockSpec((8, s), lambda h: (0, 0)),
            pl.BlockSpec((None, 1, LANE), lambda h: (l, 0, 0)),
            pl.BlockSpec((None, 1, LANE), lambda h: (l, 0, 0))]


def _fox_fwd(name, z, fc, fr, qg, kg, l):
    s = z.shape[0]
    tq = min(s, FOX_TQ)

    def body(q, k, v, fc_ref, fr_ref, qg_ref, kg_ref, y_ref):
        h = pl.program_id(0)
        for i in range(s // tq):
            rows, keys = pl.ds(i * tq, tq), pl.ds(0, (i + 1) * tq)
            y = _fox_fn(q[rows, :], k[keys, :], v[keys, :], fc_ref[rows, :], fr_ref[:, keys], qg_ref[...],
                        kg_ref[...], h, i)
            y_ref[rows, :] = y.astype(bf16)

    return pl.pallas_call(
        body, name=name, grid=(4,), in_specs=_fox_specs(s, l), out_specs=pl.BlockSpec((s, LANE), lambda h: (0, h)),
        out_shape=jax.ShapeDtypeStruct((s, G), bf16), compiler_params=_cp(("parallel",)),
    )(z, z, z, fc, fr, qg, kg)


def _fox_bwd(name, z, fc, fr, qg, kg, dy, l):
    s = z.shape[0]
    tq = min(s, FOX_TQ)

    def body(q, k, v, fc_ref, fr_ref, qg_ref, kg_ref, dy_ref, dq, dk_out, dv_out, dfc, dfr, dqg, dkg, dk, dv):
        h = pl.program_id(0)
        for ref in (dk, dv, dfr, dqg, dkg):
            ref[...] = jnp.zeros_like(ref)
        for i in range(s // tq):
            rows, keys = pl.ds(i * tq, tq), pl.ds(0, (i + 1) * tq)
            fn = functools.partial(_fox_fn, h=h, i=i)
            _, vjp = jax.vjp(fn, q[rows, :], k[keys, :], v[keys, :], fc_ref[rows, :], fr_ref[:, keys], qg_ref[...],
                             kg_ref[...])
            g = vjp(dy_ref[rows, :])
            dq[rows, :] = g[0].astype(bf16)
            dfc[rows, :] = g[3]
            dk[keys, :] += g[1]
            dv[keys, :] += g[2]
            dfr[:, keys] += g[4]
            dqg[...] += g[5]
            dkg[...] += g[6]
        dk_out[...] = dk[...].astype(bf16)
        dv_out[...] = dv[...].astype(bf16)

    head = pl.BlockSpec((s, LANE), lambda h: (0, h))
    gain = pl.BlockSpec((None, 1, LANE), lambda h: (h, 0, 0))
    return pl.pallas_call(
        body, name=name, grid=(4,), in_specs=_fox_specs(s, l) + [pl.BlockSpec((s, LANE), lambda h: (0, 12 + h))],
        out_specs=[head, head, head, pl.BlockSpec((None, s, LANE), lambda h: (h, 0, 0)),
                   pl.BlockSpec((None, 8, s), lambda h: (h, 0, 0)), gain, gain],
        out_shape=[jax.ShapeDtypeStruct((s, G), bf16)] * 3 + [
            jax.ShapeDtypeStruct((4, s, LANE), f32), jax.ShapeDtypeStruct((4, 8, s), f32),
            jax.ShapeDtypeStruct((4, 1, LANE), f32), jax.ShapeDtypeStruct((4, 1, LANE), f32)],
        scratch_shapes=[pltpu.VMEM((s, LANE), f32), pltpu.VMEM((s, LANE), f32)],
        compiler_params=_cp(("parallel",)),
    )(z, z, z, fc, fr, qg, kg, dy)


def _gla_ts(s):
    return min(s, 512)


def _gla1_specs(s, ts, l):
    return [pl.BlockSpec((ts, LANE), lambda h, i: (i, GQ + h)),
            pl.BlockSpec((ts, LANE), lambda h, i: (i, GK + h)),
            pl.BlockSpec((ts, LANE), lambda h, i: (i, GV + h)),
            pl.BlockSpec((ts, LANE), lambda h, i: (i, MISC)),
            pl.BlockSpec((None, LANE, LANE), lambda h, i: (l, 0, h)),
            pl.BlockSpec((None, 1, LANE), lambda h, i: (l, 0, h))]


def _gla1_fwd(name, z, wd, bd, l):
    s = z.shape[0]
    ts = _gla_ts(s)
    nb = ts // CHUNK

    def body(q, k, v, m, wd_ref, bd_ref, qe_ref, kd_ref, el_ref, oi_ref):
        qe, kd, el, oi = _gla1_fn(q[...], k[...], v[...], m[...], wd_ref[...], bd_ref[...])
        qe_ref[...] = qe.astype(bf16)
        kd_ref[...] = kd.astype(bf16)
        el_ref[...] = el
        oi_ref[...] = oi

    blk = pl.BlockSpec((ts, LANE), lambda h, i: (i, h))
    return pl.pallas_call(
        body, name=name, grid=(4, s // ts), in_specs=_gla1_specs(s, ts, l),
        out_specs=[blk, blk, pl.BlockSpec((nb, LANE), lambda h, i: (i, h)), blk],
        out_shape=[jax.ShapeDtypeStruct((s, G), bf16), jax.ShapeDtypeStruct((s, G), bf16),
                   jax.ShapeDtypeStruct((s // CHUNK, G), f32), jax.ShapeDtypeStruct((s, G), f32)],
        compiler_params=_cp(("parallel", "parallel")),
    )(z, z, z, z, wd, bd)


def _gla1_bwd(name, z, wd, bd, dqe, dkd, del_, do, dvi, l):
    s = z.shape[0]
    ts = _gla_ts(s)
    nb = ts // CHUNK

    def body(q, k, v, m, wd_ref, bd_ref, dqe_ref, dkd_ref, del_ref, do_ref, dvi_ref, dq, dk, dv, dm, dwd, dbd):
        i = pl.program_id(1)
        _, vjp = jax.vjp(_gla1_fn, q[...], k[...], v[...], m[...], wd_ref[...], bd_ref[...])
        g = vjp((dqe_ref[...], dkd_ref[...], del_ref[...], do_ref[...]))
        dq[...] = g[0].astype(bf16)
        dk[...] = g[1].astype(bf16)
        dv[...] = (g[2] + dvi_ref[...]).astype(bf16)
        dm[...] = g[3]

        @pl.when(i == 0)
        def _():
            dwd[...] = g[4]
            dbd[...] = g[5]

        @pl.when(i > 0)
        def _():
            dwd[...] += g[4]
            dbd[...] += g[5]

    blk = pl.BlockSpec((ts, LANE), lambda h, i: (i, h))
    specs = _gla1_specs(s, ts, l) + [blk, blk, pl.BlockSpec((nb, LANE), lambda h, i: (i, h)), blk, blk]
    return pl.pallas_call(
        body, name=name, grid=(4, s // ts), in_specs=specs,
        out_specs=[blk, blk, blk, pl.BlockSpec((None, ts, LANE), lambda h, i: (h, i, 0)),
                   pl.BlockSpec((None, LANE, LANE), lambda h, i: (h, 0, 0)),
                   pl.BlockSpec((None, 1, LANE), lambda h, i: (h, 0, 0))],
        out_shape=[jax.ShapeDtypeStruct((s, G), bf16)] * 3 + [
            jax.ShapeDtypeStruct((4, s, LANE), f32), jax.ShapeDtypeStruct((4, LANE, LANE), f32),
            jax.ShapeDtypeStruct((4, 1, LANE), f32)],
        compiler_params=_cp(("parallel", "arbitrary")),
    )(z, z, z, z, wd, bd, dqe, dkd, del_, do, dvi)


def _gla2_fwd(name, z, qe, kd, el, oi):
    s = z.shape[0]
    n = s // CHUNK

    def body(v_ref, qe_ref, kd_ref, el_ref, oi_ref, o_ref, st_ref, cur):
        cur[...] = jnp.zeros_like(cur)

        def step(c, carry):
            rows = pl.ds(pl.multiple_of(c * CHUNK, CHUNK), CHUNK)
            el = el_ref[pl.ds(c, 1), :]
            for hh in range(2):
                cols = pl.ds(hh * LANE, LANE)
                st = cur[hh]
                st_ref[hh, c] = st
                o_ref[rows, cols] = oi_ref[rows, cols] + _dot(qe_ref[rows, cols], st, ((1,), (1,)))
                cur[hh] = st * el[:, hh * LANE:(hh + 1) * LANE] \
                    + _dot(v_ref[rows, cols], kd_ref[rows, cols], ((0,), (0,)))
            return carry

        lax.fori_loop(0, n, step, 0)

    pair = pl.BlockSpec((s, 2 * LANE), lambda h: (0, h))
    return pl.pallas_call(
        body, name=name, grid=(2,),
        in_specs=[pl.BlockSpec((s, 2 * LANE), lambda h: (0, GV // 2 + h)), pair, pair,
                  pl.BlockSpec((n, 2 * LANE), lambda h: (0, h)), pair],
        out_specs=[pair, pl.BlockSpec((2, n, LANE, LANE), lambda h: (h, 0, 0, 0))],
        out_shape=[jax.ShapeDtypeStruct((s, G), f32), jax.ShapeDtypeStruct((4, n, LANE, LANE), f32)],
        scratch_shapes=[pltpu.VMEM((2, LANE, LANE), f32)],
        compiler_params=_cp(("parallel",)),
    )(z, qe, kd, el, oi)


def _gla2_bwd(name, z, qe, kd, el, st, do):
    s = z.shape[0]
    n = s // CHUNK

    def body(v_ref, qe_ref, kd_ref, el_ref, st_ref, do_ref, dqe_ref, dkd_ref, dv_ref, del_ref, dcur):
        dcur[...] = jnp.zeros_like(dcur)

        def step(t, carry):
            c = n - 1 - t
            rows = pl.ds(pl.multiple_of(c * CHUNK, CHUNK), CHUNK)
            el, dels = el_ref[pl.ds(c, 1), :], []
            for hh in range(2):
                cols = pl.ds(hh * LANE, LANE)
                dn = dcur[hh]
                stc = st_ref[hh, c]
                doc = do_ref[rows, cols]
                dqe_ref[rows, cols] = _dot(doc, stc)
                dv_ref[rows, cols] = _dot(kd_ref[rows, cols], dn, ((1,), (1,)))
                dkd_ref[rows, cols] = _dot(v_ref[rows, cols], dn)
                dels.append(jnp.sum(stc * dn, axis=0, keepdims=True))
                dcur[hh] = dn * el[:, hh * LANE:(hh + 1) * LANE] + _dot(doc, qe_ref[rows, cols], ((0,), (0,)))
            del_ref[pl.ds(c, 1), :] = jnp.concatenate(dels, axis=1)
            return carry

        lax.fori_loop(0, n, step, 0)

    pair = pl.BlockSpec((s, 2 * LANE), lambda h: (0, h))
    chunk = pl.BlockSpec((n, 2 * LANE), lambda h: (0, h))
    return pl.pallas_call(
        body, name=name, grid=(2,),
        in_specs=[pl.BlockSpec((s, 2 * LANE), lambda h: (0, GV // 2 + h)), pair, pair, chunk,
                  pl.BlockSpec((2, n, LANE, LANE), lambda h: (h, 0, 0, 0)), pair],
        out_specs=[pair, pair, pair, chunk],
        out_shape=[jax.ShapeDtypeStruct((s, G), f32)] * 3 + [jax.ShapeDtypeStruct((n, G), f32)],
        scratch_shapes=[pltpu.VMEM((2, LANE, LANE), f32)],
        compiler_params=_cp(("parallel",)),
    )(z, qe, kd, el, st, do)


def _gla3_specs(ts, l):
    return [pl.BlockSpec((ts, LANE), lambda h, i: (i, h)),
            pl.BlockSpec((ts, LANE), lambda h, i: (i, GG + h)),
            pl.BlockSpec((None, 1, LANE), lambda h, i: (l, 0, 0))]


def _gla3_fwd(name, o, z, og, l):
    s = z.shape[0]
    ts = _gla_ts(s)

    def body(o_ref, g_ref, og_ref, y_ref):
        y_ref[...] = _gla3_fn(o_ref[...], g_ref[...], og_ref[...]).astype(bf16)

    return pl.pallas_call(
        body, name=name, grid=(4, s // ts), in_specs=_gla3_specs(ts, l),
        out_specs=pl.BlockSpec((ts, LANE), lambda h, i: (i, h)),
        out_shape=jax.ShapeDtypeStruct((s, G), bf16), compiler_params=_cp(("parallel", "parallel")),
    )(o, z, og)


def _gla3_bwd(name, o, z, og, dy, l):
    s = z.shape[0]
    ts = _gla_ts(s)

    def body(o_ref, g_ref, og_ref, dy_ref, do_ref, dg_ref, dog_ref):
        i = pl.program_id(1)
        _, vjp = jax.vjp(_gla3_fn, o_ref[...], g_ref[...], og_ref[...])
        g = vjp(dy_ref[...])
        do_ref[...] = g[0]
        dg_ref[...] = g[1].astype(bf16)

        @pl.when(i == 0)
        def _():
            dog_ref[...] = g[2]

        @pl.when(i > 0)
        def _():
            dog_ref[...] += g[2]

    blk = pl.BlockSpec((ts, LANE), lambda h, i: (i, h))
    return pl.pallas_call(
        body, name=name, grid=(4, s // ts),
        in_specs=_gla3_specs(ts, l) + [pl.BlockSpec((ts, LANE), lambda h, i: (i, 8 + h))],
        out_specs=[blk, blk, pl.BlockSpec((None, 1, LANE), lambda h, i: (h, 0, 0))],
        out_shape=[jax.ShapeDtypeStruct((s, G), f32), jax.ShapeDtypeStruct((s, G), bf16),
                   jax.ShapeDtypeStruct((4, 1, LANE), f32)],
        compiler_params=_cp(("parallel", "arbitrary")),
    )(o, z, og, dy)


def _layer_fwd(l, h, p, weights):
    s = h.shape[0]
    tb = _tm_big(s)
    n = f"l{l}_"
    w_in = weights(l, "w_in", h)
    u = _rmsnorm_fwd(n + "norm_mix", h, p["g_mix"], l)
    z = _mm(n + "mm_in", u, w_in,
            pl.BlockSpec((tb, D), lambda j, i, k: (i, 0)), pl.BlockSpec((D, 1152), lambda j, i, k: (0, j)),
            jax.ShapeDtypeStruct((s, ZC), f32), pl.BlockSpec((tb, 1152), lambda j, i, k: (i, j)),
            (ZC // 1152, s // tb, 1), ((1,), (0,)), 1)
    w_out = weights(l, "w_out", z)
    ya, yb = _convpool_fwd(n + "convpool", z, p["conv_w"], p["pool_w"], p["pool_scale"], l)
    qe, kd, el, oi = _gla1_fwd(n + "gla_chunk", z, p["wdec"], p["bdec"], l)
    o, st = _gla2_fwd(n + "gla_scan", z, qe, kd, el, oi)
    yc = _gla3_fwd(n + "gla_out", o, z, p["gla_og"], l)
    fc, fr = _foxprep_fwd(n + "fox_prep", z, p["fox_bf"], l)
    yd = _fox_fwd(n + "fox_attn", z, fc, fr, p["fox_qg"], p["fox_kg"], l)
    y = jnp.concatenate([ya, yb, yc, yd], axis=1)
    w_gate, w_up, w_down = weights(l, "ffn", y)
    res_tile = lambda: pl.BlockSpec((tb, 1024), lambda j, i, k: (i, j))
    h1 = _mm(n + "mm_out", y, w_out.reshape(D, D),
             pl.BlockSpec((tb, D), lambda j, i, k: (i, 0)), pl.BlockSpec((D, 1024), lambda j, i, k: (0, j)),
             jax.ShapeDtypeStruct((s, D), f32), res_tile(), (2, s // tb, 1), ((1,), (0,)), 1, res=h, res_spec=res_tile())
    u2 = _rmsnorm_fwd(n + "norm_ffn", h1, p["g_ffn"], l)
    gate, up, act = _mm_gate_up(n + "mm_gate_up", u2, w_gate, w_up)
    h2 = _mm(n + "mm_down", act, w_down,
             pl.BlockSpec((tb, FB), lambda j, i, k: (i, k)),
             pl.BlockSpec((None, FB, 1024), lambda j, i, k: (k, 0, j)),
             jax.ShapeDtypeStruct((s, D), f32), res_tile(), (2, s // tb, NCHIP), ((1,), (0,)), NCHIP,
             res=h1, res_spec=res_tile())
    saved = dict(h=h, u=u, z=z, qe=qe, kd=kd, el=el, st=st, o=o, fc=fc, fr=fr, y=y, h1=h1, u2=u2,
                 gate=gate, up=up, act=act, w_in=w_in, w_out=w_out, w_gate=w_gate, w_up=w_up, w_down=w_down)
    return h2, saved


def _mm_tn(name, a, b, ta, tb, out_shape, out_spec, grid):
    s = a.shape[0]
    return _mm(name, a, b, pl.BlockSpec((s, ta), lambda i, j, k: (0, i)), pl.BlockSpec((s, tb), lambda i, j, k: (0, j)),
               out_shape, out_spec, grid, ((0,), (0,)), 1)


def _layer_bwd_ffn(l, dh2, dh2b, p, sv, ship):
    s = dh2.shape[0]
    tb = _tm_big(s)
    n = f"l{l}_bwd_"
    g_wd = _mm_tn(n + "mm_dwd", sv["act"], dh2b, FB, 1024, jax.ShapeDtypeStruct((NCHIP, FB, D), bf16),
                  pl.BlockSpec((None, FB, 1024), lambda i, j, k: (i, 0, j)), (NCHIP, 2, 1))
    dgate, dup = _mm_dact(n + "mm_dact", dh2b, sv["w_down"], sv["gate"], sv["up"])
    wg_shape = jax.ShapeDtypeStruct((NCHIP, D, FB), bf16)
    wg_spec = lambda: pl.BlockSpec((None, 1024, FB), lambda i, j, k: (j, i, 0))
    g_wg = _mm_tn(n + "mm_dwg", sv["u2"], dgate, 1024, FB, wg_shape, wg_spec(), (2, NCHIP, 1))
    g_wu = _mm_tn(n + "mm_dwu", sv["u2"], dup, 1024, FB, wg_shape, wg_spec(), (2, NCHIP, 1))
    token = ship.early(dict(w_gate=g_wg, w_up=g_wu, w_down=g_wd))
    nt_in = lambda: (pl.BlockSpec((tb, FB), lambda j, i, k: (i, k)),
                     pl.BlockSpec((None, 1024, FB), lambda j, i, k: (k, j, 0)))
    nt_out = lambda: (jax.ShapeDtypeStruct((s, D), f32), pl.BlockSpec((tb, 1024), lambda j, i, k: (i, j)))
    du2 = _mm(n + "mm_du2", dgate, sv["w_gate"], *nt_in(), *nt_out(), (2, s // tb, NCHIP), ((1,), (1,)), NCHIP,
              dep=token, second=(dup, sv["w_up"]))
    dh1, dh1b, dg_ffn = _rmsnorm_bwd(n + "norm_ffn", sv["h1"], p["g_ffn"], du2, dh2, l)
    g_wo = _mm_tn(n + "mm_dwo", sv["y"], dh1b, G, 1024, jax.ShapeDtypeStruct((NCHIP, G, D), bf16),
                  pl.BlockSpec((None, G, 1024), lambda i, j, k: (i, 0, j)), (NCHIP, 2, 1))
    token = ship.finish(dict(w_out=g_wo), g_wo)
    dy = _mm(n + "mm_dy", dh1b, sv["w_out"].reshape(D, D),
             pl.BlockSpec((tb, D), lambda j, i, k: (i, 0)), pl.BlockSpec((1024, D), lambda j, i, k: (j, 0)),
             jax.ShapeDtypeStruct((s, D), f32), pl.BlockSpec((tb, 1024), lambda j, i, k: (i, j)),
             (2, s // tb, 1), ((1,), (1,)), 1, dep=token)
    return dh1, dy, dg_ffn


def _layer_bwd_mix(l, dh1, dy, p, sv, ship):
    s = dh1.shape[0]
    n = f"l{l}_bwd_"
    z = sv["z"]
    dcb, dcc, dch, dpu, dconv, dpoolw, dpools = _convpool_bwd(
        n + "convpool", z, p["conv_w"], p["pool_w"], p["pool_scale"], dy, l)
    do, dgg, dog = _gla3_bwd(n + "gla_out", sv["o"], z, p["gla_og"], dy, l)
    dqe, dkd, dvi, del_ = _gla2_bwd(n + "gla_scan", z, sv["qe"], sv["kd"], sv["el"], sv["st"], do)
    dgq, dgk, dgv, dmisc4, dwd, dbd = _gla1_bwd(n + "gla_chunk", z, p["wdec"], p["bdec"], dqe, dkd, del_, do, dvi, l)
    dfq, dfk, dfv, dfc4, dfr4, dqg, dkg = _fox_bwd(n + "fox_attn", z, sv["fc"], sv["fr"], p["fox_qg"], p["fox_kg"], dy, l)
    dmisc, dbf = _foxprep_bwd(n + "fox_prep", z, p["fox_bf"], dfc4, dfr4, dmisc4, l)
    dz = jnp.concatenate([dcb, dcc, dch, dpu, dgq, dgk, dgv, dgg, dfq, dfk, dfv, dmisc], axis=1)
    g_wi = _mm_tn(n + "mm_dwi", sv["u"], dz, 1024, 1152, jax.ShapeDtypeStruct((D, ZC), bf16),
                  pl.BlockSpec((1024, 1152), lambda i, j, k: (i, j)), (2, ZC // 1152, 1))
    token = ship.early(dict(w_in=_win_to_blocks(g_wi)))
    tb = _tm_big(s)
    du = _mm(n + "mm_du", dz, sv["w_in"],
             pl.BlockSpec((tb, 1920), lambda j, i, k: (i, k)), pl.BlockSpec((1024, 1920), lambda j, i, k: (j, k)),
             jax.ShapeDtypeStruct((s, D), f32), pl.BlockSpec((tb, 1024), lambda j, i, k: (i, j)),
             (2, s // tb, ZC // 1920), ((1,), (1,)), ZC // 1920, dep=token)
    token = ship.finish({}, du)
    dh, dhb, dg_mix = _rmsnorm_bwd(n + "norm_mix", sv["h"], p["g_mix"], du, dh1, l, dep=token)
    small = dict(
        norm_mix_g=dg_mix[0], conv_w=dconv, pool_w=dpoolw, pool_scale=dpools[0],
        gla_w_decay=jnp.concatenate([dwd[hh, GA_LANE:GA_LANE + 16, :64] for hh in range(4)], axis=1),
        gla_b_decay=jnp.concatenate([dbd[hh, 0, :64] for hh in range(4)]),
        gla_out_g=jnp.sum(dog[:, 0, :], axis=0), fox_q_g=jnp.sum(dqg[:, 0, :], axis=0),
        fox_k_g=jnp.sum(dkg[:, 0, :], axis=0), fox_b_f=dbf[0, FF_LANE:FF_LANE + 4])
    return dh, dhb, small


def _win_from_blocks(wb):
    def cols(a, b):
        parts = []
        for kk in range(NCHIP):
            lo, hi = max(a, kk * WINB), min(b, (kk + 1) * WINB)
            if lo < hi:
                parts.append(wb[kk, :, lo - kk * WINB:hi - kk * WINB])
        return parts

    zeros = lambda w: [jnp.zeros((wb.shape[1], w), wb.dtype)]
    segs = cols(0, 2048)
    for hh in range(4):
        segs += cols(2048 + 64 * hh, 2112 + 64 * hh) + zeros(64)
    for hh in range(4):
        segs += cols(2304 + 64 * hh, 2368 + 64 * hh) + zeros(64)
    segs += cols(2560, 3584) + cols(3600, 5136)
    segs += cols(5136, 5140) + zeros(GA_LANE - 4) + cols(3584, 3600) + zeros(LANE - GA_LANE - 16)
    return jnp.concatenate(segs, axis=-1)


def _win_to_blocks(g):
    mb = MISC * LANE
    segs = [(0, 2048)] + [(GQ * LANE + LANE * hh, 64) for hh in range(4)] + [(GK * LANE + LANE * hh, 64) for hh in range(4)]
    segs += [(GV * LANE, 1024), (mb + GA_LANE, 16), (FQ * LANE, 1536), (mb + FF_LANE, 4)]
    blocks, at = [[] for _ in range(NCHIP)], 0
    for start, width in segs:
        while width > 0:
            take = min(width, (at // WINB + 1) * WINB - at)
            blocks[at // WINB].append(g[:, start:start + take])
            start, width, at = start + take, width - take, at + take
    return jnp.stack([jnp.concatenate(b, axis=1) for b in blocks])


def _place():
    x, y, c = lax.axis_index("x"), lax.axis_index("y"), lax.axis_index("c")
    chips = [(1 - x, y), (x, 1 - y), (1 - x, 1 - y)]
    return x, y, c, chips


def _allgather_small(name, v, dep):
    m_per, n = v.shape

    def body(x_ref, dep_ref, out_ref, send_sems, recv_sems, local_sem):
        x, y, c, chips = _place()
        me, sibling = (x, y, c), (x, y, 1 - c)

        def rows(px, py, pc):
            return out_ref.at[pl.ds((4 * px + 2 * py + pc) * m_per, m_per), :]

        def copy(k, block, to, src=None):
            return pltpu.make_async_remote_copy(
                src_ref=rows(*block) if src is None else src, dst_ref=rows(*block),
                send_sem=send_sems.at[k], recv_sem=recv_sems.at[k], device_id=to, device_id_type=MESH)

        mine = pltpu.make_async_copy(x_ref, rows(*me), local_sem)
        mine.start()
        first = [copy(0, me, sibling, src=x_ref)]
        first += [copy(1 + j, me, (*chip, c), src=x_ref) for j, chip in enumerate(chips)]
        for cp in first:
            cp.start()
        passed = [copy(4 + j, (*chip, c), sibling) for j, chip in enumerate(chips)]
        for j, chip in enumerate(chips):
            copy(1 + j, (*chip, c), me).wait_recv()
            passed[j].start()
        copy(0, sibling, me).wait_recv()
        for j, chip in enumerate(chips):
            copy(4 + j, (*chip, 1 - c), me).wait_recv()
        for cp in first + passed:
            cp.wait_send()
        mine.wait()

    return pl.pallas_call(
        body, name=name, out_shape=jax.ShapeDtypeStruct((8 * m_per, n), v.dtype),
        in_specs=[pl.BlockSpec(memory_space=pltpu.VMEM), pl.BlockSpec(memory_space=pl.ANY)],
        out_specs=pl.BlockSpec(memory_space=pltpu.VMEM),
        scratch_shapes=[pltpu.SemaphoreType.DMA((7,)), pltpu.SemaphoreType.DMA((7,)), pltpu.SemaphoreType.DMA],
    )(v, dep)


def _hbm_specs(n):
    return [pl.BlockSpec(memory_space=pl.ANY)] * n


def _own_slot(shard, chip):
    return lax.dynamic_update_index_in_dim(lax.empty((NCHIP,) + shard.shape, shard.dtype), shard, chip, 0)


def _cast_slot(name, w, l, pos, dep):
    _, r, cdim = w.shape
    tr = r // 4
    deps = [] if dep is None else [dep]

    def body(pos_ref, w_ref, *rest):
        rest[-1][...] = w_ref[...].astype(bf16)

    return pl.pallas_call(
        body, name=name, out_shape=jax.ShapeDtypeStruct((NCHIP, r, cdim), bf16),
        grid_spec=pltpu.PrefetchScalarGridSpec(
            num_scalar_prefetch=1, grid=(r // tr,),
            in_specs=[pl.BlockSpec((None, tr, cdim), lambda i, p: (l, i, 0))]
            + [pl.BlockSpec((8, LANE), lambda i, p: (0, 0)) for _ in deps],
            out_specs=pl.BlockSpec((None, tr, cdim), lambda i, p: (p[0], i, 0))),
        compiler_params=_cp(("parallel",)),
    )(pos, w, *deps)


def _cast_slot_w_in(name, w, l, pos, dep):
    wt = jnp.transpose(w, (2, 0, 1))
    cols, _, rows = wt.shape
    deps = [] if dep is None else [dep]

    def body(pos_ref, w_ref, *rest):
        rest[-1][...] = jnp.transpose(w_ref[:, l, :]).astype(bf16)

    return pl.pallas_call(
        body, name=name, out_shape=jax.ShapeDtypeStruct((NCHIP, rows, cols), bf16),
        grid_spec=pltpu.PrefetchScalarGridSpec(
            num_scalar_prefetch=1, grid=(pl.cdiv(cols, LANE),),
            in_specs=[pl.BlockSpec((LANE, 2, rows), lambda i, p: (i, 0, 0))]
            + [pl.BlockSpec((8, LANE), lambda i, p: (0, 0)) for _ in deps],
            out_specs=pl.BlockSpec((None, rows, LANE), lambda i, p: (p[0], 0, i))),
        compiler_params=_cp(("parallel",)),
    )(pos, wt, *deps)


HBM = pl.BlockSpec(memory_space=pltpu.HBM)
SEM = pl.BlockSpec(memory_space=pltpu.SEMAPHORE)
EFFECT = pltpu.SideEffectType.DATAFLOW_SIDE_EFFECTING


def _in_hbm(v):
    return pltpu.with_memory_space_constraint(v, pltpu.HBM)


def _gather_start(name, groups):
    flat = [b for g in groups for b in g]
    nt, ng = len(flat), len(groups)

    def body(*refs):
        outs = refs[nt:]
        sems, bufs, token = outs[:2 * ng], outs[2 * ng:2 * ng + nt], outs[2 * ng + nt]
        token[...] = jnp.zeros_like(token)
        x, y, c, chips = _place()
        me = 2 * x + y
        t = 0
        for gi, g in enumerate(groups):
            for k in range(len(g)):
                r = bufs[t].shape[1] // 2
                mine = bufs[t].at[me, pl.ds(c * r, r), :]
                for j, (px, py) in enumerate(chips):
                    pltpu.make_async_remote_copy(
                        src_ref=mine, dst_ref=mine, send_sem=sems[2 * gi].at[3 * k + j], recv_sem=sems[2 * gi + 1].at[3 * k + j],
                        device_id=(px, py, c), device_id_type=MESH).start()
                t += 1

    sem_shapes = []
    for g in groups:
        sem_shapes += [pltpu.SemaphoreType.DMA((3 * len(g),))] * 2
    out = pl.pallas_call(
        body, name=name,
        out_shape=tuple(sem_shapes + [pltpu.HBM(b.shape, b.dtype) for b in flat] + [jax.ShapeDtypeStruct((8, LANE), f32)]),
        in_specs=tuple([HBM] * nt),
        out_specs=tuple([SEM] * (2 * ng) + [HBM] * nt + [pl.BlockSpec(memory_space=pltpu.VMEM)]),
        input_output_aliases={t: 2 * ng + t for t in range(nt)},
        compiler_params=pltpu.CompilerParams(has_side_effects=EFFECT),
    )(*[_in_hbm(b) for b in flat])
    sems = [(out[2 * gi], out[2 * gi + 1]) for gi in range(ng)]
    bufs, at = [], 2 * ng
    for g in groups:
        bufs.append(list(out[at:at + len(g)]))
        at += len(g)
    return sems, bufs, out[at]


def _gather_wait(name, bufs, send_sems, recv_sems, after):
    nt = len(bufs)

    def body(*refs):
        ins, ss, rs = refs[:nt], refs[nt], refs[nt + 1]
        x, y, c, chips = _place()
        me = 2 * x + y
        for k in range(nt):
            r = ins[k].shape[1] // 2
            for j, (px, py) in enumerate(chips):
                cp = pltpu.make_async_remote_copy(
                    src_ref=ins[k].at[me, pl.ds(c * r, r), :], dst_ref=ins[k].at[2 * px + py, pl.ds(c * r, r), :],
                    send_sem=ss.at[3 * k + j], recv_sem=rs.at[3 * k + j], device_id=(px, py, c), device_id_type=MESH)
                cp.wait_send()
                cp.wait_recv()

    out = pl.pallas_call(
        body, name=name, out_shape=tuple(pltpu.HBM(b.shape, b.dtype) for b in bufs),
        in_specs=tuple([HBM] * nt + [SEM, SEM, pl.BlockSpec(memory_space=pl.ANY)]), out_specs=tuple([HBM] * nt),
        input_output_aliases={t: t for t in range(nt)},
        compiler_params=pltpu.CompilerParams(has_side_effects=EFFECT),
    )(*bufs, send_sems, recv_sems, after)
    return list(out)


def _gather_exchange(name, bufs):
    nt = len(bufs)

    def body(*refs):
        outs = refs[nt:2 * nt]
        send_sems, recv_sems = refs[2 * nt:]
        x, y, c, chips = _place()
        sibling = (x, y, 1 - c)

        def half(t, chip_idx, cc):
            r = outs[t].shape[1] // 2
            return outs[t].at[chip_idx, pl.ds(cc * r, r), :]

        sent = []
        for t in range(nt):
            for j, (px, py) in enumerate(chips):
                cp = pltpu.make_async_remote_copy(
                    src_ref=half(t, 2 * px + py, c), dst_ref=half(t, 2 * px + py, c),
                    send_sem=send_sems.at[t, j], recv_sem=recv_sems.at[t, j], device_id=sibling, device_id_type=MESH)
                cp.start()
                sent.append(cp)
        for t in range(nt):
            for j, (px, py) in enumerate(chips):
                pltpu.make_async_remote_copy(
                    src_ref=half(t, 2 * px + py, 1 - c), dst_ref=half(t, 2 * px + py, 1 - c),
                    send_sem=send_sems.at[t, j], recv_sem=recv_sems.at[t, j], device_id=sibling,
                    device_id_type=MESH).wait_recv()
        for cp in sent:
            cp.wait_send()

    return pl.pallas_call(
        body, name=name, out_shape=[jax.ShapeDtypeStruct(v.shape, v.dtype) for v in bufs],
        in_specs=_hbm_specs(nt), out_specs=_hbm_specs(nt), input_output_aliases={t: t for t in range(nt)},
        scratch_shapes=[pltpu.SemaphoreType.DMA((nt, 3)), pltpu.SemaphoreType.DMA((nt, 3))],
    )(*bufs)


def _rs_to_sibling(name, grads):
    nt = len(grads)

    def body(*refs):
        ins, outs = refs[:nt], refs[nt:2 * nt]
        send_sems, recv_sems = refs[2 * nt:]
        x, y, c, _ = _place()
        cps = []
        for t in range(nt):
            r = ins[t].shape[1] // 2
            cp = pltpu.make_async_remote_copy(
                src_ref=ins[t].at[:, pl.ds((1 - c) * r, r), :], dst_ref=outs[t],
                send_sem=send_sems.at[t], recv_sem=recv_sems.at[t], device_id=(x, y, 1 - c), device_id_type=MESH)
            cp.start()
            cps.append(cp)
        for cp in cps:
            cp.wait()

    return pl.pallas_call(
        body, name=name,
        out_shape=[jax.ShapeDtypeStruct((NCHIP, g.shape[1] // 2, g.shape[2]), g.dtype) for g in grads],
        in_specs=_hbm_specs(nt), out_specs=_hbm_specs(nt),
        scratch_shapes=[pltpu.SemaphoreType.DMA((nt,)), pltpu.SemaphoreType.DMA((nt,))],
    )(*grads)


def _sibling_start(name, grads):
    nt = len(grads)

    def body(*refs):
        outs = refs[2 * nt:]
        ss, rs, src, land, token = outs[0], outs[1], outs[2:2 + nt], outs[2 + nt:2 + 2 * nt], outs[2 + 2 * nt]
        x, y, c, _ = _place()
        for t in range(nt):
            r = src[t].shape[1] // 2
            pltpu.make_async_remote_copy(
                src_ref=src[t].at[:, pl.ds((1 - c) * r, r), :], dst_ref=land[t], send_sem=ss.at[t], recv_sem=rs.at[t],
                device_id=(x, y, 1 - c), device_id_type=MESH).start()
        token[...] = jnp.zeros_like(token)

    src_shapes = [pltpu.HBM(g.shape, g.dtype) for g in grads]
    land_shapes = [pltpu.HBM((NCHIP, g.shape[1] // 2, g.shape[2]), g.dtype) for g in grads]
    out = pl.pallas_call(
        body, name=name,
        out_shape=tuple([pltpu.SemaphoreType.DMA((nt,))] * 2 + src_shapes + land_shapes
                        + [jax.ShapeDtypeStruct((8, LANE), f32)]),
        in_specs=tuple([HBM] * (2 * nt)),
        out_specs=tuple([SEM, SEM] + [HBM] * (2 * nt) + [pl.BlockSpec(memory_space=pltpu.VMEM)]),
        input_output_aliases={t: 2 + t for t in range(2 * nt)},
        compiler_params=pltpu.CompilerParams(has_side_effects=EFFECT),
    )(*[_in_hbm(g) for g in grads], *[_in_hbm(lax.empty(s_.shape, s_.dtype)) for s_ in land_shapes])
    return out[0], out[1], list(out[2:2 + nt]), list(out[2 + nt:2 + 2 * nt]), out[2 + 2 * nt]


def _sibling_wait(name, grads, land, send_sems, recv_sems, after):
    nt = len(grads)

    def body(*refs):
        src, dst, ss, rs = refs[:nt], refs[nt:2 * nt], refs[2 * nt], refs[2 * nt + 1]
        x, y, c, _ = _place()
        for t in range(nt):
            r = src[t].shape[1] // 2
            cp = pltpu.make_async_remote_copy(
                src_ref=src[t].at[:, pl.ds((1 - c) * r, r), :], dst_ref=dst[t], send_sem=ss.at[t], recv_sem=rs.at[t],
                device_id=(x, y, 1 - c), device_id_type=MESH)
            cp.wait_send()
            cp.wait_recv()

    shapes = [pltpu.HBM(v.shape, v.dtype) for v in list(grads) + list(land)]
    out = pl.pallas_call(
        body, name=name, out_shape=tuple(shapes),
        in_specs=tuple([HBM] * (2 * nt) + [SEM, SEM, pl.BlockSpec(memory_space=pl.ANY)]),
        out_specs=tuple([HBM] * (2 * nt)), input_output_aliases={t: t for t in range(2 * nt)},
        compiler_params=pltpu.CompilerParams(has_side_effects=EFFECT),
    )(*grads, *land, send_sems, recv_sems, after)
    return list(out[:nt]), list(out[nt:])


def _rs_pair_sum(name, pos, g, other):
    r, cdim = other.shape[1], other.shape[2]
    tr = r // 4 if (r // 4) % 16 == 0 else r // 2
    nblk = r // tr

    def body(pos_ref, g_ref, o_ref, s_ref):
        s_ref[...] = (g_ref[...].astype(f32) + o_ref[...].astype(f32)).astype(bf16)

    blk = pl.BlockSpec((None, tr, cdim), lambda q, i, p: (q, i, 0))
    return pl.pallas_call(
        body, name=name, out_shape=jax.ShapeDtypeStruct(other.shape, bf16),
        grid_spec=pltpu.PrefetchScalarGridSpec(
            num_scalar_prefetch=1, grid=(NCHIP, nblk),
            in_specs=[pl.BlockSpec((None, tr, cdim), lambda q, i, p: (q, p[1] * nblk + i, 0)), blk], out_specs=blk),
        compiler_params=_cp(("parallel", "parallel")),
    )(pos, g, other)


def _scatter_start(name, sums):
    nt = len(sums)

    def body(*refs):
        outs = refs[2 * nt:]
        ss, rs, src, land, token = outs[0], outs[1], outs[2:2 + nt], outs[2 + nt:2 + 2 * nt], outs[2 + 2 * nt]
        x, y, c, chips = _place()
        me = 2 * x + y
        for t in range(nt):
            for j, (px, py) in enumerate(chips):
                pltpu.make_async_remote_copy(
                    src_ref=src[t].at[2 * px + py], dst_ref=land[t].at[me], send_sem=ss.at[3 * t + j], recv_sem=rs.at[3 * t + j],
                    device_id=(px, py, c), device_id_type=MESH).start()
        token[...] = jnp.zeros_like(token)

    shapes = [pltpu.HBM(v.shape, v.dtype) for v in sums]
    out = pl.pallas_call(
        body, name=name,
        out_shape=tuple([pltpu.SemaphoreType.DMA((3 * nt,))] * 2 + shapes + shapes + [jax.ShapeDtypeStruct((8, LANE), f32)]),
        in_specs=tuple([HBM] * (2 * nt)),
        out_specs=tuple([SEM, SEM] + [HBM] * (2 * nt) + [pl.BlockSpec(memory_space=pltpu.VMEM)]),
        input_output_aliases={t: 2 + t for t in range(2 * nt)},
        compiler_params=pltpu.CompilerParams(has_side_effects=EFFECT),
    )(*[_in_hbm(v) for v in sums], *[_in_hbm(lax.empty(v.shape, v.dtype)) for v in sums])
    return out[0], out[1], list(out[2:2 + nt]), list(out[2 + nt:2 + 2 * nt]), out[2 + 2 * nt]


def _scatter_wait(name, sums, land, send_sems, recv_sems, after):
    nt = len(sums)

    def body(*refs):
        src, dst, ss, rs = refs[:nt], refs[nt:2 * nt], refs[2 * nt], refs[2 * nt + 1]
        x, y, c, chips = _place()
        for t in range(nt):
            for j, (px, py) in enumerate(chips):
                cp = pltpu.make_async_remote_copy(
                    src_ref=src[t].at[2 * px + py], dst_ref=dst[t].at[2 * px + py], send_sem=ss.at[3 * t + j],
                    recv_sem=rs.at[3 * t + j], device_id=(px, py, c), device_id_type=MESH)
                cp.wait_send()
                cp.wait_recv()

    shapes = [pltpu.HBM(v.shape, v.dtype) for v in sums]
    out = pl.pallas_call(
        body, name=name, out_shape=tuple(shapes + shapes),
        in_specs=tuple([HBM] * (2 * nt) + [SEM, SEM, pl.BlockSpec(memory_space=pl.ANY)]),
        out_specs=tuple([HBM] * (2 * nt)), input_output_aliases={t: t for t in range(2 * nt)},
        compiler_params=pltpu.CompilerParams(has_side_effects=EFFECT),
    )(*sums, *land, send_sems, recv_sems, after)
    return list(out[:nt]), list(out[nt:])


def _rs_chip_sum(name, pos, sums, parts):
    r, cdim = parts.shape[1], parts.shape[2]
    tr = r // 4 if (r // 4) % 16 == 0 else r // 2
    nblk = r // tr

    def body(pos_ref, own_ref, a_ref, b_ref, c_ref, o_ref):
        o_ref[...] = ((own_ref[...].astype(f32) + a_ref[...].astype(f32)) + b_ref[...].astype(f32)) \
            + c_ref[...].astype(f32)

    def slot(k):
        return pl.BlockSpec((None, tr, cdim), lambda i, p: ((p[0] + k) % NCHIP, i, 0))

    return pl.pallas_call(
        body, name=name, out_shape=jax.ShapeDtypeStruct((2 * r, cdim), f32),
        grid_spec=pltpu.PrefetchScalarGridSpec(
            num_scalar_prefetch=1, grid=(nblk,), in_specs=[slot(0), slot(1), slot(2), slot(3)],
            out_specs=pl.BlockSpec((tr, cdim), lambda i, p: (p[1] * nblk + i, 0))),
        compiler_params=_cp(("parallel",)),
    )(pos, sums, parts, parts, parts)


def _rs_share_halves(name, bufs):
    nt = len(bufs)

    def body(*refs):
        outs = refs[nt:2 * nt]
        send_sems, recv_sems = refs[2 * nt:]
        x, y, c, _ = _place()
        cps = []
        for t in range(nt):
            r = outs[t].shape[0] // 2
            mine = outs[t].at[pl.ds(c * r, r), :]
            theirs = outs[t].at[pl.ds((1 - c) * r, r), :]
            cp = pltpu.make_async_remote_copy(
                src_ref=mine, dst_ref=mine, send_sem=send_sems.at[t], recv_sem=recv_sems.at[t],
                device_id=(x, y, 1 - c), device_id_type=MESH)
            cp.start()
            cps.append((cp, theirs))
        for t, (cp, theirs) in enumerate(cps):
            pltpu.make_async_remote_copy(
                src_ref=theirs, dst_ref=theirs, send_sem=send_sems.at[t], recv_sem=recv_sems.at[t],
                device_id=(x, y, 1 - c), device_id_type=MESH).wait_recv()
            cp.wait_send()

    return pl.pallas_call(
        body, name=name, out_shape=[jax.ShapeDtypeStruct(v.shape, v.dtype) for v in bufs],
        in_specs=_hbm_specs(nt), out_specs=_hbm_specs(nt), input_output_aliases={t: t for t in range(nt)},
        scratch_shapes=[pltpu.SemaphoreType.DMA((nt,)), pltpu.SemaphoreType.DMA((nt,))],
    )(*bufs)


BIG = ("w_in", "w_out", "w_gate", "w_up", "w_down")


GROUPS = (("w_in",), ("w_out",), ("w_gate", "w_up", "w_down"))


class _Ship:
    def __init__(self, tag, pos):
        self.tag, self.pos, self.started, self.state = tag, pos, None, None

    def early(self, grads):
        self.early_keys = tuple(grads)
        self.started = _sibling_start(self.tag + "sibling_start", [grads[k] for k in self.early_keys])
        return self.started[-1]

    def finish(self, grads, after):
        keys, mine, got = tuple(grads), [grads[k] for k in grads], []
        if keys:
            got = list(_rs_to_sibling(self.tag + "to_sibling", mine))
        if self.started is not None:
            send_sems, recv_sems, src, land, _ = self.started
            src, land = _sibling_wait(self.tag + "sibling_wait", src, land, send_sems, recv_sems, after)
            keys, mine, got = self.early_keys + keys, src + mine, land + got
        sums = [_rs_pair_sum(self.tag + "pair_sum_" + k, self.pos, g, o) for k, g, o in zip(keys, mine, got)]
        self.state = (keys,) + _scatter_start(self.tag + "start", sums)
        return self.state[-1]


def _rs_end(tag, state, pos, after):
    keys, send_sems, recv_sems, sums, land, _ = state
    sums, land = _scatter_wait(tag + "wait", sums, land, send_sems, recv_sems, after)
    halves = [_rs_chip_sum(tag + "chip_sum_" + k, pos, s, v) for k, s, v in zip(keys, sums, land)]
    return dict(zip(keys, _rs_share_halves(tag + "share", halves)))


def _adam_math(w, g, m, v):
    m = ADAM_B1 * m + (1.0 - ADAM_B1) * g
    v = ADAM_B2 * v + (1.0 - ADAM_B2) * (g * g)
    m_hat = m / (1.0 - ADAM_B1 ** ADAM_STEP)
    v_hat = v / (1.0 - ADAM_B2 ** ADAM_STEP)
    delta = -ADAM_LR * (m_hat / (jnp.sqrt(v_hat) + ADAM_EPS) + ADAM_WD * w)
    return delta, m, v


def _adam_big(name, g0, g1, w, m, v):
    _, r, cdim = w.shape
    tr = 128 if r % 128 == 0 else 64
    nb = r // tr

    def body(g0_ref, g1_ref, w_ref, m_ref, v_ref, go_ref, d_ref, mo_ref, vo_ref):
        l = pl.program_id(0)
        g = jnp.where(l == 0, g0_ref[...], g1_ref[...])
        delta, mn, vn = _adam_math(w_ref[...], g, m_ref[...], v_ref[...])
        go_ref[...] = g
        d_ref[...] = delta
        mo_ref[...] = mn
        vo_ref[...] = vn

    lay = pl.BlockSpec((None, tr, cdim), lambda l, i: (l, i, 0))
    return pl.pallas_call(
        body, name=name, grid=(2, nb),
        in_specs=[pl.BlockSpec((tr, cdim), lambda l, i: (i * (1 - l) + (nb - 1) * l, 0)),
                  pl.BlockSpec((tr, cdim), lambda l, i: (i * l, 0)), lay, lay, lay],
        out_specs=[lay] * 4, out_shape=[jax.ShapeDtypeStruct(w.shape, f32)] * 4,
        compiler_params=_cp(("arbitrary", "arbitrary")),
    )(g0, g1, w, m, v)


def _adam_w_in(name, g0, g1, w, m, v):
    wt, mt, vt = (jnp.transpose(a, (2, 0, 1)) for a in (w, m, v))
    cols, _, rows = wt.shape

    def body(g0_ref, g1_ref, w_ref, m_ref, v_ref, go_ref, d_ref, mo_ref, vo_ref):
        for l, g_ref in enumerate((g0_ref, g1_ref)):
            g = jnp.transpose(g_ref[...])
            delta, mn, vn = _adam_math(w_ref[:, l, :], g, m_ref[:, l, :], v_ref[:, l, :])
            go_ref[:, l, :] = g
            d_ref[:, l, :] = delta
            mo_ref[:, l, :] = mn
            vo_ref[:, l, :] = vn

    blk = pl.BlockSpec((LANE, 2, rows), lambda i: (i, 0, 0))
    gblk = pl.BlockSpec((rows, LANE), lambda i: (0, i))
    outs = pl.pallas_call(
        body, name=name, grid=(pl.cdiv(cols, LANE),), in_specs=[gblk, gblk, blk, blk, blk], out_specs=[blk] * 4,
        out_shape=[jax.ShapeDtypeStruct(wt.shape, f32)] * 4, compiler_params=_cp(("parallel",)),
    )(g0, g1, wt, mt, vt)
    return tuple(jnp.transpose(o, (1, 2, 0)) for o in outs)


def _sum8(name, gathered):
    m_per = gathered.shape[0] // 8

    def body(g_ref, o_ref):
        tot = g_ref[pl.ds(0, m_per), :]
        for d in range(1, 8):
            tot = tot + g_ref[pl.ds(d * m_per, m_per), :]
        o_ref[...] = tot

    return pl.pallas_call(body, name=name, out_shape=jax.ShapeDtypeStruct((m_per, LANE), f32))(gathered)


def _adam_small(name, g, w, m, v):
    def body(g_ref, w_ref, m_ref, v_ref, d_ref, mo_ref, vo_ref):
        delta, mn, vn = _adam_math(w_ref[...], g_ref[...], m_ref[...], v_ref[...])
        d_ref[...] = delta
        mo_ref[...] = mn
        vo_ref[...] = vn

    return pl.pallas_call(body, name=name, out_shape=[jax.ShapeDtypeStruct(g.shape, f32)] * 3)(g, w, m, v)


def _pack(vals):
    rows, offs, at = [], [], 0
    for a in vals:
        a = a.reshape(-1)
        n = -(-a.shape[0] // (8 * LANE)) * 8
        rows.append(jnp.pad(a, (0, n * LANE - a.shape[0])).reshape(n, LANE))
        offs.append(at)
        at += n
    return jnp.concatenate(rows, axis=0), offs


def _unpack(packed, offs, shapes):
    out = []
    for o, shp in zip(offs, shapes):
        size = 1
        for d in shp:
            size *= d
        n = -(-size // LANE)
        out.append(packed[o:o + n].reshape(-1)[:size].reshape(shp))
    return out


SMALL = ("norm_mix_g", "conv_w", "pool_w", "pool_scale", "gla_w_decay", "gla_b_decay", "gla_out_g",
         "fox_q_g", "fox_k_g", "fox_b_f", "norm_ffn_g")
ALL = ("norm_mix_g", "w_in", "conv_w", "pool_w", "pool_scale", "gla_w_decay", "gla_b_decay", "gla_out_g",
       "fox_q_g", "fox_k_g", "fox_b_f", "w_out", "norm_ffn_g", "w_gate", "w_up", "w_down")


def kernel(x, norm_mix_g, w_in, conv_w, pool_w, pool_scale, gla_w_decay, gla_b_decay, gla_out_g, fox_q_g, fox_k_g, fox_b_f, w_out, norm_ffn_g, w_gate, w_up, w_down, loss_target, m_norm_mix_g, m_w_in, m_conv_w, m_pool_w, m_pool_scale, m_gla_w_decay, m_gla_b_decay, m_gla_out_g, m_fox_q_g, m_fox_k_g, m_fox_b_f, m_w_out, m_norm_ffn_g, m_w_gate, m_w_up, m_w_down, v_norm_mix_g, v_w_in, v_conv_w, v_pool_w, v_pool_scale, v_gla_w_decay, v_gla_b_decay, v_gla_out_g, v_fox_q_g, v_fox_k_g, v_fox_b_f, v_w_out, v_norm_ffn_g, v_w_gate, v_w_up, v_w_down):
    w = dict(norm_mix_g=norm_mix_g, w_in=w_in, conv_w=conv_w, pool_w=pool_w, pool_scale=pool_scale,
             gla_w_decay=gla_w_decay, gla_b_decay=gla_b_decay, gla_out_g=gla_out_g, fox_q_g=fox_q_g, fox_k_g=fox_k_g,
             fox_b_f=fox_b_f, w_out=w_out, norm_ffn_g=norm_ffn_g, w_gate=w_gate, w_up=w_up, w_down=w_down)
    m = dict(norm_mix_g=m_norm_mix_g, w_in=m_w_in, conv_w=m_conv_w, pool_w=m_pool_w, pool_scale=m_pool_scale,
             gla_w_decay=m_gla_w_decay, gla_b_decay=m_gla_b_decay, gla_out_g=m_gla_out_g, fox_q_g=m_fox_q_g,
             fox_k_g=m_fox_k_g, fox_b_f=m_fox_b_f, w_out=m_w_out, norm_ffn_g=m_norm_ffn_g, w_gate=m_w_gate,
             w_up=m_w_up, w_down=m_w_down)
    v = dict(norm_mix_g=v_norm_mix_g, w_in=v_w_in, conv_w=v_conv_w, pool_w=v_pool_w, pool_scale=v_pool_scale,
             gla_w_decay=v_gla_w_decay, gla_b_decay=v_gla_b_decay, gla_out_g=v_gla_out_g, fox_q_g=v_fox_q_g,
             fox_k_g=v_fox_k_g, fox_b_f=v_fox_b_f, w_out=v_w_out, norm_ffn_g=v_norm_ffn_g, w_gate=v_w_gate,
             w_up=v_w_up, w_down=v_w_down)
    chip = 2 * lax.axis_index("x") + lax.axis_index("y")

    pos = jnp.stack([chip, lax.axis_index("c")]).astype(jnp.int32)

    mine, offs = _pack([conv_w, gla_w_decay, jnp.zeros((8, LANE), f32)])

    def casts(l, keys, dep):
        return [(_cast_slot_w_in if k == "w_in" else _cast_slot)(f"l{l}_cast_{k}", w[k], l, pos, dep) for k in keys]

    sems, gbufs, token = _gather_start("gather_start", [[_own_slot(mine, chip)], casts(0, ("w_in",), None)])
    more = [casts(0, grp, token) for grp in GROUPS[1:]]
    sems_b, gbufs_b, token = _gather_start("l0_gather_start_rest", more)
    more = [casts(1, grp, token) for grp in GROUPS]
    sems_c, gbufs_c, started1 = _gather_start("l1_gather_start", more)
    sems, gbufs = sems + sems_b + sems_c, gbufs + gbufs_b + gbufs_c

    def gathered(tag, gi, after):
        got = _gather_wait(tag + "_wait", gbufs[gi], sems[gi][0], sems[gi][1], after)
        return _gather_exchange(tag + "_exchange", got)

    def weights(l, group, after):
        if (l, group) == (0, "w_in"):
            after = started1
        got = gathered(f"l{l}_gather_{group}", 1 + 3 * l + ("w_in", "w_out", "ffn").index(group), after)
        if group == "w_in":
            return _win_from_blocks(got[0])
        return got[0] if group == "w_out" else got

    every = gathered("gather_small", 0, x)[0]
    per_chip = [_unpack(every[kk], offs, [conv_w.shape, gla_w_decay.shape]) for kk in range(NCHIP)]
    conv_full = jnp.concatenate([pc[0] for pc in per_chip], axis=-1)[:, :, 0, :]
    wdec_full = jnp.concatenate([pc[1] for pc in per_chip], axis=-1)

    wdec = jnp.pad(wdec_full.reshape(2, 16, 4, 64), ((0, 0), (GA_LANE, LANE - GA_LANE - 16), (0, 0), (0, 64)))
    p = dict(
        g_mix=norm_mix_g[:, None, :], g_ffn=norm_ffn_g[:, None, :],
        conv_w=conv_full, pool_w=pool_w, pool_scale=pool_scale[:, None, :],
        wdec=wdec.reshape(2, LANE, G),
        bdec=jnp.pad(gla_b_decay.reshape(2, 4, 64), ((0, 0), (0, 0), (0, 64))).reshape(2, 1, G),
        gla_og=gla_out_g[:, None, :], fox_qg=fox_q_g[:, None, :], fox_kg=fox_k_g[:, None, :],
        fox_bf=jnp.pad(fox_b_f, ((0, 0), (FF_LANE, LANE - FF_LANE - 4)))[:, None, :])

    h0 = x[0]
    h1, sv0 = _layer_fwd(0, h0, p, weights)
    h2, sv1 = _layer_fwd(1, h1, p, weights)
    sq, dh, dhb = _loss("loss", h2, loss_target[0])
    loss = lax.psum(sq[0, 0] * (0.5 / D), ("x", "y", "c"))

    late = ("w_out", "w_gate", "w_up", "w_down")
    ships = {tag: _Ship(tag, pos) for tag in ("l1_rs_a_", "l1_rs_b_", "l0_rs_a_", "l0_rs_b_")}
    dh1, dy, dgf1 = _layer_bwd_ffn(1, dh, dhb, p, sv1, ships["l1_rs_a_"])
    dh, dhb, small1 = _layer_bwd_mix(1, dh1, dy, p, sv1, ships["l1_rs_b_"])
    dh1, dy, dgf0 = _layer_bwd_ffn(0, dh, dhb, p, sv0, ships["l0_rs_a_"])
    dh, dhb, small0 = _layer_bwd_mix(0, dh1, dy, p, sv0, ships["l0_rs_b_"])
    small0["norm_ffn_g"], small1["norm_ffn_g"] = dgf0[0], dgf1[0]

    red1 = _rs_end("l1_rs_a_", ships["l1_rs_a_"].state, pos, dh)
    red1.update(_rs_end("l1_rs_b_", ships["l1_rs_b_"].state, pos, red1["w_down"]))
    red0 = _rs_end("l0_rs_a_", ships["l0_rs_a_"].state, pos, red1["w_in"])
    grads, deltas, new_m, new_v = {}, {}, {}, {}
    for k in late:
        grads[k], deltas[k], new_m[k], new_v[k] = _adam_big("adam_" + k, red0[k], red1[k], w[k], m[k], v[k])

    packed, goffs = _pack([jnp.stack([small0[k], small1[k]]) for k in SMALL])
    total = _sum8("sum_small_grads", _allgather_small("gather_small_grads", packed, new_v["w_down"]))
    red0.update(_rs_end("l0_rs_b_", ships["l0_rs_b_"].state, pos, total))
    k = "w_in"
    grads[k], deltas[k], new_m[k], new_v[k] = _adam_w_in("adam_" + k, red0[k], red1[k], w[k], m[k], v[k])
    full_shapes = [(2,) + small0[k].shape for k in SMALL]
    gsmall = dict(zip(SMALL, _unpack(total, goffs, full_shapes)))
    gsmall["conv_w"] = lax.dynamic_slice_in_dim(gsmall["conv_w"], chip * LANE, LANE, axis=2)[:, :, None, :]
    gsmall["gla_w_decay"] = lax.dynamic_slice_in_dim(gsmall["gla_w_decay"], chip * 64, 64, axis=2)
    gp, loffs = _pack([gsmall[k] for k in SMALL])
    wp, _ = _pack([w[k] for k in SMALL])
    mp, _ = _pack([m[k] for k in SMALL])
    vp, _ = _pack([v[k] for k in SMALL])
    dp, mnp, vnp = _adam_small("adam_small", gp, wp, mp, vp)
    shapes = [w[k].shape for k in SMALL]
    for k, a, b, c_, d_ in zip(SMALL, _unpack(gp, loffs, shapes), _unpack(dp, loffs, shapes),
                               _unpack(mnp, loffs, shapes), _unpack(vnp, loffs, shapes)):
        grads[k], deltas[k], new_m[k], new_v[k] = a, b, c_, d_

    return (loss, dh[None], *[grads[k] for k in ALL], *[deltas[k] for k in ALL],
            *[new_m[k] for k in ALL], *[new_v[k] for k in ALL])
```

```python
import functools

import jax
import jax.numpy as jnp
from jax import lax
from jax.experimental import pallas as pl
from jax.experimental.pallas import tpu as pltpu

f32 = jnp.float32
bf16 = jnp.bfloat16

D = 2048
G = 512
DFF = 5632
NCHIP = 4
FB = DFF // NCHIP
WIN = 5140
WINB = WIN // NCHIP
EPS = 1e-6
CHUNK = 64
LANE = 128

CB, CC, CH, PU, GQ, GK, GV, GG, FQ, FK, FV, MISC = 0, 4, 8, 12, 16, 20, 24, 28, 32, 36, 40, 44
ZC = 45 * LANE
TZ = ZC // 5
TZK = ZC // 3
FF_LANE = 0
GA_LANE = 8

ADAM_LR, ADAM_B1, ADAM_B2, ADAM_EPS, ADAM_WD, ADAM_STEP = 0.001, 0.9, 0.999, 1e-08, 0.01, 10

VMEM_LIMIT = 60 * 1024 * 1024
MESH = pl.DeviceIdType.MESH


def _cp(sem=None):
    return pltpu.CompilerParams(dimension_semantics=sem, vmem_limit_bytes=VMEM_LIMIT)


def _dot(a, b, dims=((1,), (0,))):
    return lax.dot_general(a.astype(bf16), b.astype(bf16), (dims, ((), ())), preferred_element_type=f32)


def _bdot(a, b, ca, cb):
    return lax.dot_general(a.astype(bf16), b.astype(bf16), (((ca,), (cb,)), ((0,), (0,))),
                           preferred_element_type=f32)


def _log_sigmoid(x):
    return jnp.minimum(x, 0.0) - jnp.log(1.0 + jnp.exp(-jnp.abs(x)))


@jax.custom_vjp
def _sigmoid(x):
    return 1.0 / (1.0 + jnp.exp(-x))


def _sigmoid_fwd(x):
    s = _sigmoid(x)
    return s, s


def _sigmoid_bwd(s, g):
    return (g * s * (1.0 - s),)


_sigmoid.defvjp(_sigmoid_fwd, _sigmoid_bwd)


def _rms(x, g):
    return x * lax.rsqrt(jnp.mean(x * x, axis=-1, keepdims=True) + EPS) * g


def _shift_impl(x, n, period, transpose):
    rows = x.shape[0]
    t = lax.broadcasted_iota(jnp.int32, x.shape, 0)
    if period is not None:
        t = t & (period - 1)
    keep = t >= n
    if not transpose:
        return jnp.where(keep, pltpu.roll(x, n, 0), 0.0)
    return pltpu.roll(jnp.where(keep, x, 0.0), rows - n, 0)


def _shift(x, n, period=None):
    @jax.custom_vjp
    def f(v):
        return _shift_impl(v, n, period, False)

    def fwd(v):
        return f(v), None

    def bwd(_, g):
        return (_shift_impl(g, n, period, True),)

    f.defvjp(fwd, bwd)
    return f(x)


def _cumsum_rows(x, length, period=None):
    n = 1
    while n < length:
        x = x + _shift(x, n, period)
        n *= 2
    return x


def _convpool_fn(cb, cc, ch, pu, w0, w1, w2, pw, ps, j):
    u = cc * ch
    y = w2 * u + w1 * _shift(u, 1) + w0 * _shift(u, 2)
    ya = cb * y
    s2 = pu + _shift(pu, 1)
    s4 = s2 + _shift(s2, 2)
    s8 = s4 + _shift(s4, 4)
    s16 = s8 + _shift(s8, 8)
    wsum = jnp.where(j == 0, s2, jnp.where(j == 1, s4, jnp.where(j == 2, s8, s16)))
    width = (2 << j).astype(f32)
    t = lax.broadcasted_iota(jnp.int32, pu.shape, 0).astype(f32)
    count = jnp.minimum(t + 1.0, width)
    d = wsum / count - pu
    yb = _dot(d, pw) * ps
    return ya, yb


def _foxprep_fn(misc, bf):
    lf = _log_sigmoid(misc + bf)
    fc = _cumsum_rows(lf, lf.shape[0])
    return fc, jnp.transpose(fc)


def _fox_fn(q, k, v, fcol, frow8, qg, kg, h, i):
    tq, s = q.shape[0], k.shape[0]
    qn = _rms(q, qg)
    kn = _rms(k, kg)
    lg = _dot(qn, kn, ((1,), (1,))) * (LANE ** -0.5)
    lane = lax.broadcasted_iota(jnp.int32, fcol.shape, 1)
    fq = jnp.sum(jnp.where(lane == h, fcol, 0.0), axis=1, keepdims=True)
    row = lax.broadcasted_iota(jnp.int32, frow8.shape, 0)
    fk = jnp.sum(jnp.where(row == h, frow8, 0.0), axis=0, keepdims=True)
    lg = lg + fq - fk
    qpos = i * tq + lax.broadcasted_iota(jnp.int32, (tq, s), 0)
    kpos = lax.broadcasted_iota(jnp.int32, (tq, s), 1)
    lg = jnp.where(kpos <= qpos, lg, -jnp.inf)
    m = lax.stop_gradient(jnp.max(lg, axis=1, keepdims=True))
    e = jnp.exp(lg - m)
    p = e / jnp.sum(e, axis=1, keepdims=True)
    return _dot(p, v)


def _gla1_fn(q, k, v, misc, wd, bd):
    ts = q.shape[0]
    nb = ts // CHUNK
    x = _dot(misc, wd) + bd
    la = _log_sigmoid(x) * (1.0 / 16.0)
    cc = _cumsum_rows(la, CHUNK, CHUNK)
    la3 = la.reshape(nb, CHUNK, LANE)
    last3 = jnp.sum(la3, axis=1, keepdims=True)
    last2 = jnp.sum(la3, axis=1)
    cc3 = cc.reshape(nb, CHUNK, LANE)
    q3 = (q * 0.125).reshape(nb, CHUNK, LANE)
    k3 = k.reshape(nb, CHUNK, LANE)
    v3 = v.reshape(nb, CHUNK, LANE)
    ep = jnp.exp(cc3)
    en = jnp.exp(-cc3)
    qe = q3 * ep
    a1 = _bdot(qe, k3 * en, 2, 2)
    a2 = _bdot(q3 * en, k3 * ep, 2, 2)
    ti = lax.broadcasted_iota(jnp.int32, a1.shape, 1)
    si = lax.broadcasted_iota(jnp.int32, a1.shape, 2)
    sc = jnp.where(si <= ti, a1, a2)
    oi = _bdot(sc, v3, 2, 1)
    kd = k3 * jnp.exp(last3 - cc3)
    el = jnp.exp(last2)
    return qe.reshape(ts, LANE), kd.reshape(ts, LANE), el, oi.reshape(ts, LANE)


def _gla3_fn(o, gg, og):
    return _rms(o, og) * (gg * _sigmoid(gg))


def _ffn_fn(gate, up):
    return gate * _sigmoid(gate) * up


def _mm(name, a, b, a_spec, b_spec, out_shape, out_spec, grid, dims, nk, res=None, res_spec=None, dep=None,
        second=None):
    has_res = res is not None
    has_dep = dep is not None
    has_two = second is not None
    nax = len(grid)
    first_out = 2 + 2 * has_two + has_res + has_dep

    def body(*refs):
        a_ref, b_ref = refs[0], refs[1]
        res_ref = refs[2 + 2 * has_two] if has_res else None
        out_ref = refs[first_out]
        part = _dot(a_ref[...], b_ref[...], dims)
        if has_two:
            part = part + _dot(refs[2][...], refs[3][...], dims)
        if nk == 1:
            if has_res:
                part = part + res_ref[...]
            out_ref[...] = part.astype(out_ref.dtype)
            return
        acc_ref = refs[first_out + 1]
        k = pl.program_id(nax - 1)

        @pl.when(k == 0)
        def _():
            acc_ref[...] = part

        @pl.when(k > 0)
        def _():
            acc_ref[...] += part

        @pl.when(k == nk - 1)
        def _():
            tot = acc_ref[...]
            if has_res:
                tot = tot + res_ref[...]
            out_ref[...] = tot.astype(out_ref.dtype)

    ops = [a, b] + (list(second) if has_two else []) + ([res] if has_res else []) + ([dep] if has_dep else [])
    specs = [a_spec, b_spec] + ([a_spec, b_spec] if has_two else []) + ([res_spec] if has_res else [])
    if has_dep:
        specs.append(pl.BlockSpec((8, LANE), lambda *_: (0, 0)))
    blk = tuple(d for d in out_spec.block_shape if d is not None)
    scratch = [pltpu.VMEM(blk, f32)] if nk > 1 else []
    return pl.pallas_call(
        body, name=name, grid=grid, in_specs=specs, out_specs=out_spec, out_shape=out_shape,
        scratch_shapes=scratch,
        compiler_params=_cp(("parallel",) * (nax - 1) + ("arbitrary",)),
    )(*ops)


def _tm(s):
    return min(s, 512)


def _tm_big(s):
    return min(s, 1024)


def _mm_gate_up(name, u, w_gate, w_up):
    s = u.shape[0]
    tm = _tm(s)

    def body(u_ref, wg_ref, wu_ref, g_ref, up_ref, a_ref):
        g = _dot(u_ref[...], wg_ref[...])
        up = _dot(u_ref[...], wu_ref[...])
        g_ref[...] = g
        up_ref[...] = up
        a_ref[...] = _ffn_fn(g, up).astype(bf16)

    wspec = pl.BlockSpec((None, D, FB), lambda j, i: (j, 0, 0))
    tile = pl.BlockSpec((tm, FB), lambda j, i: (i, j))
    return pl.pallas_call(
        body, name=name, grid=(NCHIP, s // tm),
        in_specs=[pl.BlockSpec((tm, D), lambda j, i: (i, 0)), wspec, wspec], out_specs=[tile, tile, tile],
        out_shape=[jax.ShapeDtypeStruct((s, DFF), f32), jax.ShapeDtypeStruct((s, DFF), f32),
                   jax.ShapeDtypeStruct((s, DFF), bf16)],
        compiler_params=_cp(("parallel", "parallel")),
    )(u, w_gate, w_up)


def _mm_dact(name, dh, w_down, gate, up):
    s = dh.shape[0]
    tm = _tm(s)

    def body(dh_ref, wd_ref, g_ref, up_ref, dg_ref, du_ref):
        dact = _dot(dh_ref[...], wd_ref[...], ((1,), (1,)))
        _, vjp = jax.vjp(_ffn_fn, g_ref[...], up_ref[...])
        dg, du = vjp(dact)
        dg_ref[...] = dg.astype(bf16)
        du_ref[...] = du.astype(bf16)

    tile = pl.BlockSpec((tm, FB), lambda j, i: (i, j))
    return pl.pallas_call(
        body, name=name, grid=(NCHIP, s // tm),
        in_specs=[pl.BlockSpec((tm, D), lambda j, i: (i, 0)), pl.BlockSpec((None, FB, D), lambda j, i: (j, 0, 0)),
                  tile, tile],
        out_specs=[tile, tile], out_shape=[jax.ShapeDtypeStruct((s, DFF), bf16)] * 2,
        compiler_params=_cp(("parallel", "parallel")),
    )(dh, w_down, gate, up)


def _rmsnorm_fwd(name, x, g, l):
    s = x.shape[0]
    tm = min(s, 256)

    def body(x_ref, g_ref, u_ref):
        u_ref[...] = _rms(x_ref[...], g_ref[...]).astype(bf16)

    return pl.pallas_call(
        body, name=name, grid=(s // tm,),
        in_specs=[pl.BlockSpec((tm, D), lambda i: (i, 0)), pl.BlockSpec((None, 1, D), lambda i: (l, 0, 0))],
        out_specs=pl.BlockSpec((tm, D), lambda i: (i, 0)),
        out_shape=jax.ShapeDtypeStruct((s, D), bf16), compiler_params=_cp(("parallel",)),
    )(x, g)


def _rmsnorm_bwd(name, x, g, du, dres, l, dep=None):
    s = x.shape[0]
    tm = min(s, 256)

    def body(x_ref, g_ref, du_ref, dres_ref, *rest):
        dx_ref, dxb_ref, dg_ref = rest[-3:]
        _, vjp = jax.vjp(_rms, x_ref[...], g_ref[...])
        dx, dg = vjp(du_ref[...])
        tot = dx + dres_ref[...]
        dx_ref[...] = tot
        dxb_ref[...] = tot.astype(bf16)

        @pl.when(pl.program_id(0) == 0)
        def _():
            dg_ref[...] = dg

        @pl.when(pl.program_id(0) > 0)
        def _():
            dg_ref[...] += dg

    row = pl.BlockSpec((tm, D), lambda i: (i, 0))
    deps = [] if dep is None else [dep]
    return pl.pallas_call(
        body, name=name, grid=(s // tm,),
        in_specs=[row, pl.BlockSpec((None, 1, D), lambda i: (l, 0, 0)), row, row]
        + [pl.BlockSpec((8, LANE), lambda i: (0, 0)) for _ in deps],
        out_specs=[row, row, pl.BlockSpec((1, D), lambda i: (0, 0))],
        out_shape=[jax.ShapeDtypeStruct((s, D), f32), jax.ShapeDtypeStruct((s, D), bf16),
                   jax.ShapeDtypeStruct((1, D), f32)],
        compiler_params=_cp(("arbitrary",)),
    )(x, g, du, dres, *deps)


def _loss(name, y, t):
    s = y.shape[0]
    tm = min(s, 256)
    row = pl.BlockSpec((tm, D), lambda i: (i, 0))

    def body(y_ref, t_ref, l_ref, d_ref, db_ref):
        e = y_ref[...] - t_ref[...]
        d = e * (1.0 / D)
        d_ref[...] = d
        db_ref[...] = d.astype(bf16)
        part = jnp.zeros((8, LANE), f32) + jnp.sum(e * e)

        @pl.when(pl.program_id(0) == 0)
        def _():
            l_ref[...] = part

        @pl.when(pl.program_id(0) > 0)
        def _():
            l_ref[...] += part

    return pl.pallas_call(
        body, name=name, grid=(s // tm,), in_specs=[row, row],
        out_specs=[pl.BlockSpec((8, LANE), lambda i: (0, 0)), row, row],
        out_shape=[jax.ShapeDtypeStruct((8, LANE), f32), jax.ShapeDtypeStruct((s, D), f32),
                   jax.ShapeDtypeStruct((s, D), bf16)],
        compiler_params=_cp(("arbitrary",)),
    )(y, t)


def _zspec(s, blk):
    return pl.BlockSpec((s, LANE), lambda j: (0, blk + j))


def _convpool_specs(s, l):
    return [_zspec(s, CB), _zspec(s, CC), _zspec(s, CH), _zspec(s, PU),
            pl.BlockSpec((None, 3, LANE), lambda j: (l, 0, j)),
            pl.BlockSpec((None, None, LANE, LANE), lambda j: (l, j, 0, 0)),
            pl.BlockSpec((None, 1, LANE), lambda j: (l, 0, j))]


def _convpool_fwd(name, z, conv_w, pool_w, pool_scale, l):
    s = z.shape[0]

    def body(cb, cc, ch, pu, cw, pw, ps, ya_ref, yb_ref):
        ya, yb = _convpool_fn(cb[...], cc[...], ch[...], pu[...], cw[0:1, :], cw[1:2, :], cw[2:3, :], pw[...], ps[...],
                              pl.program_id(0))
        ya_ref[...] = ya.astype(bf16)
        yb_ref[...] = yb.astype(bf16)

    col = pl.BlockSpec((s, LANE), lambda j: (0, j))
    return pl.pallas_call(
        body, name=name, grid=(4,), in_specs=_convpool_specs(s, l), out_specs=[col, col],
        out_shape=[jax.ShapeDtypeStruct((s, G), bf16)] * 2, compiler_params=_cp(("parallel",)),
    )(z, z, z, z, conv_w, pool_w, pool_scale)


def _convpool_bwd(name, z, conv_w, pool_w, pool_scale, dy, l):
    s = z.shape[0]

    def body(cb, cc, ch, pu, cw, pw, ps, dya, dyb, dcb, dcc, dch, dpu, dcw, dpw, dps):
        j = pl.program_id(0)
        fn = functools.partial(_convpool_fn, j=j)
        _, vjp = jax.vjp(fn, cb[...], cc[...], ch[...], pu[...], cw[0:1, :], cw[1:2, :], cw[2:3, :], pw[...], ps[...])
        g = vjp((dya[...], dyb[...]))
        dcb[...] = g[0].astype(bf16)
        dcc[...] = g[1].astype(bf16)
        dch[...] = g[2].astype(bf16)
        dpu[...] = g[3].astype(bf16)
        dcw[0:1, :] = g[4]
        dcw[1:2, :] = g[5]
        dcw[2:3, :] = g[6]
        dpw[...] = g[7]
        dps[...] = g[8]

    col = pl.BlockSpec((s, LANE), lambda j: (0, j))
    specs = _convpool_specs(s, l) + [pl.BlockSpec((s, LANE), lambda j: (0, j)),
                                     pl.BlockSpec((s, LANE), lambda j: (0, 4 + j))]
    return pl.pallas_call(
        body, name=name, grid=(4,), in_specs=specs,
        out_specs=[col, col, col, col, pl.BlockSpec((3, LANE), lambda j: (0, j)),
                   pl.BlockSpec((None, LANE, LANE), lambda j: (j, 0, 0)), pl.BlockSpec((1, LANE), lambda j: (0, j))],
        out_shape=[jax.ShapeDtypeStruct((s, G), bf16)] * 4 + [
            jax.ShapeDtypeStruct((3, G), f32), jax.ShapeDtypeStruct((4, LANE, LANE), f32),
            jax.ShapeDtypeStruct((1, G), f32)],
        compiler_params=_cp(("parallel",)),
    )(z, z, z, z, conv_w, pool_w, pool_scale, dy, dy)


def _foxprep_fwd(name, z, bf, l):
    s = z.shape[0]

    def body(m_ref, b_ref, fc_ref, fr_ref):
        fc, fr = _foxprep_fn(m_ref[...], b_ref[...])
        fc_ref[...] = fc
        fr_ref[...] = fr

    return pl.pallas_call(
        body, name=name, grid=(1,),
        in_specs=[pl.BlockSpec((s, LANE), lambda i: (0, MISC)), pl.BlockSpec((None, 1, LANE), lambda i: (l, 0, 0))],
        out_specs=[pl.BlockSpec((s, LANE), lambda i: (0, 0)), pl.BlockSpec((LANE, s), lambda i: (0, 0))],
        out_shape=[jax.ShapeDtypeStruct((s, LANE), f32), jax.ShapeDtypeStruct((LANE, s), f32)],
        compiler_params=_cp(("arbitrary",)),
    )(z, bf)


def _foxprep_bwd(name, z, bf, dfc4, dfr4, dmisc4, l):
    s = z.shape[0]

    def body(m_ref, b_ref, dfc_ref, dfr_ref, dm4_ref, dm_ref, db_ref):
        _, vjp = jax.vjp(_foxprep_fn, m_ref[...], b_ref[...])
        dfc = dfc_ref[0] + dfc_ref[1] + dfc_ref[2] + dfc_ref[3]
        dfr = dfr_ref[0] + dfr_ref[1] + dfr_ref[2] + dfr_ref[3]
        dfr = jnp.concatenate([dfr, jnp.zeros((LANE - 8, s), f32)], axis=0)
        dm, db = vjp((dfc, dfr))
        dm = dm + (dm4_ref[0] + dm4_ref[1] + dm4_ref[2] + dm4_ref[3])
        dm_ref[...] = dm.astype(bf16)
        db_ref[...] = db

    whole = lambda shape: pl.BlockSpec(shape, lambda i: (0,) * len(shape))
    return pl.pallas_call(
        body, name=name, grid=(1,),
        in_specs=[pl.BlockSpec((s, LANE), lambda i: (0, MISC)), pl.BlockSpec((None, 1, LANE), lambda i: (l, 0, 0)),
                  whole((4, s, LANE)), whole((4, 8, s)), whole((4, s, LANE))],
        out_specs=[whole((s, LANE)), whole((1, LANE))],
        out_shape=[jax.ShapeDtypeStruct((s, LANE), bf16), jax.ShapeDtypeStruct((1, LANE), f32)],
        compiler_params=_cp(("arbitrary",)),
    )(z, bf, dfc4, dfr4, dmisc4)


FOX_TQ = 256


def _fox_specs(s, l):
    return [pl.BlockSpec((s, LANE), lambda h: (0, FQ + h)),
            pl.BlockSpec((s, LANE), lambda h: (0, FK + h)),
            pl.BlockSpec((s, LANE), lambda h: (0, FV + h)),
            pl.BlockSpec((s, LANE), lambda h: (0, 0)),
            pl.BlockSpec((8, s), lambda h: (0, 0)),
            pl.BlockSpec((None, 1, LANE), lambda h: (l, 0, 0)),
            pl.BlockSpec((None, 1, LANE), lambda h: (l, 0, 0))]


def _fox_fwd(name, z, fc, fr, qg, kg, l):
    s = z.shape[0]
    tq = min(s, FOX_TQ)

    def body(q, k, v, fc_ref, fr_ref, qg_ref, kg_ref, y_ref):
        h = pl.program_id(0)
        for i in range(s // tq):
            rows, keys = pl.ds(i * tq, tq), pl.ds(0, (i + 1) * tq)
            y = _fox_fn(q[rows, :], k[keys, :], v[keys, :], fc_ref[rows, :], fr_ref[:, keys], qg_ref[...],
                        kg_ref[...], h, i)
            y_ref[rows, :] = y.astype(bf16)

    return pl.pallas_call(
        body, name=name, grid=(4,), in_specs=_fox_specs(s, l), out_specs=pl.BlockSpec((s, LANE), lambda h: (0, h)),
        out_shape=jax.ShapeDtypeStruct((s, G), bf16), compiler_params=_cp(("parallel",)),
    )(z, z, z, fc, fr, qg, kg)


def _fox_bwd(name, z, fc, fr, qg, kg, dy, l):
    s = z.shape[0]
    tq = min(s, FOX_TQ)

    def body(q, k, v, fc_ref, fr_ref, qg_ref, kg_ref, dy_ref, dq, dk_out, dv_out, dfc, dfr, dqg, dkg, dk, dv):
        h = pl.program_id(0)
        for ref in (dk, dv, dfr, dqg, dkg):
            ref[...] = jnp.zeros_like(ref)
        for i in range(s // tq):
            rows, keys = pl.ds(i * tq, tq), pl.ds(0, (i + 1) * tq)
            fn = functools.partial(_fox_fn, h=h, i=i)
            _, vjp = jax.vjp(fn, q[rows, :], k[keys, :], v[keys, :], fc_ref[rows, :], fr_ref[:, keys], qg_ref[...],
                             kg_ref[...])
            g = vjp(dy_ref[rows, :])
            dq[rows, :] = g[0].astype(bf16)
            dfc[rows, :] = g[3]
            dk[keys, :] += g[1]
            dv[keys, :] += g[2]
            dfr[:, keys] += g[4]
            dqg[...] += g[5]
            dkg[...] += g[6]
        dk_out[...] = dk[...].astype(bf16)
        dv_out[...] = dv[...].astype(bf16)

    head = pl.BlockSpec((s, LANE), lambda h: (0, h))
    gain = pl.BlockSpec((None, 1, LANE), lambda h: (h, 0, 0))
    return pl.pallas_call(
        body, name=name, grid=(4,), in_specs=_fox_specs(s, l) + [pl.BlockSpec((s, LANE), lambda h: (0, 12 + h))],
        out_specs=[head, head, head, pl.BlockSpec((None, s, LANE), lambda h: (h, 0, 0)),
                   pl.BlockSpec((None, 8, s), lambda h: (h, 0, 0)), gain, gain],
        out_shape=[jax.ShapeDtypeStruct((s, G), bf16)] * 3 + [
            jax.ShapeDtypeStruct((4, s, LANE), f32), jax.ShapeDtypeStruct((4, 8, s), f32),
            jax.ShapeDtypeStruct((4, 1, LANE), f32), jax.ShapeDtypeStruct((4, 1, LANE), f32)],
        scratch_shapes=[pltpu.VMEM((s, LANE), f32), pltpu.VMEM((s, LANE), f32)],
        compiler_params=_cp(("parallel",)),
    )(z, z, z, fc, fr, qg, kg, dy)


def _gla_ts(s):
    return min(s, 512)


def _gla1_specs(s, ts, l):
    return [pl.BlockSpec((ts, LANE), lambda h, i: (i, GQ + h)),
            pl.BlockSpec((ts, LANE), lambda h, i: (i, GK + h)),
            pl.BlockSpec((ts, LANE), lambda h, i: (i, GV + h)),
            pl.BlockSpec((ts, LANE), lambda h, i: (i, MISC)),
            pl.BlockSpec((None, LANE, LANE), lambda h, i: (l, 0, h)),
            pl.BlockSpec((None, 1, LANE), lambda h, i: (l, 0, h))]


def _gla1_fwd(name, z, wd, bd, l):
    s = z.shape[0]
    ts = _gla_ts(s)
    nb = ts // CHUNK

    def body(q, k, v, m, wd_ref, bd_ref, qe_ref, kd_ref, el_ref, oi_ref):
        qe, kd, el, oi = _gla1_fn(q[...], k[...], v[...], m[...], wd_ref[...], bd_ref[...])
        qe_ref[...] = qe.astype(bf16)
        kd_ref[...] = kd.astype(bf16)
        el_ref[...] = el
        oi_ref[...] = oi

    blk = pl.BlockSpec((ts, LANE), lambda h, i: (i, h))
    return pl.pallas_call(
        body, name=name, grid=(4, s // ts), in_specs=_gla1_specs(s, ts, l),
        out_specs=[blk, blk, pl.BlockSpec((nb, LANE), lambda h, i: (i, h)), blk],
        out_shape=[jax.ShapeDtypeStruct((s, G), bf16), jax.ShapeDtypeStruct((s, G), bf16),
                   jax.ShapeDtypeStruct((s // CHUNK, G), f32), jax.ShapeDtypeStruct((s, G), f32)],
        compiler_params=_cp(("parallel", "parallel")),
    )(z, z, z, z, wd, bd)


def _gla1_bwd(name, z, wd, bd, dqe, dkd, del_, do, dvi, l):
    s = z.shape[0]
    ts = _gla_ts(s)
    nb = ts // CHUNK

    def body(q, k, v, m, wd_ref, bd_ref, dqe_ref, dkd_ref, del_ref, do_ref, dvi_ref, dq, dk, dv, dm, dwd, dbd):
        i = pl.program_id(1)
        _, vjp = jax.vjp(_gla1_fn, q[...], k[...], v[...], m[...], wd_ref[...], bd_ref[...])
        g = vjp((dqe_ref[...], dkd_ref[...], del_ref[...], do_ref[...]))
        dq[...] = g[0].astype(bf16)
        dk[...] = g[1].astype(bf16)
        dv[...] = (g[2] + dvi_ref[...]).astype(bf16)
        dm[...] = g[3]

        @pl.when(i == 0)
        def _():
            dwd[...] = g[4]
            dbd[...] = g[5]

        @pl.when(i > 0)
        def _():
            dwd[...] += g[4]
            dbd[...] += g[5]

    blk = pl.BlockSpec((ts, LANE), lambda h, i: (i, h))
    specs = _gla1_specs(s, ts, l) + [blk, blk, pl.BlockSpec((nb, LANE), lambda h, i: (i, h)), blk, blk]
    return pl.pallas_call(
        body, name=name, grid=(4, s // ts), in_specs=specs,
        out_specs=[blk, blk, blk, pl.BlockSpec((None, ts, LANE), lambda h, i: (h, i, 0)),
                   pl.BlockSpec((None, LANE, LANE), lambda h, i: (h, 0, 0)),
                   pl.BlockSpec((None, 1, LANE), lambda h, i: (h, 0, 0))],
        out_shape=[jax.ShapeDtypeStruct((s, G), bf16)] * 3 + [
            jax.ShapeDtypeStruct((4, s, LANE), f32), jax.ShapeDtypeStruct((4, LANE, LANE), f32),
            jax.ShapeDtypeStruct((4, 1, LANE), f32)],
        compiler_params=_cp(("parallel", "arbitrary")),
    )(z, z, z, z, wd, bd, dqe, dkd, del_, do, dvi)


def _gla2_fwd(name, z, qe, kd, el, oi):
    s = z.shape[0]
    n = s // CHUNK

    def body(v_ref, qe_ref, kd_ref, el_ref, oi_ref, o_ref, st_ref, cur):
        cur[...] = jnp.zeros_like(cur)

        def step(c, carry):
            rows = pl.ds(pl.multiple_of(c * CHUNK, CHUNK), CHUNK)
            el = el_ref[pl.ds(c, 1), :]
            for hh in range(2):
                cols = pl.ds(hh * LANE, LANE)
                st = cur[hh]
                st_ref[hh, c] = st
                o_ref[rows, cols] = oi_ref[rows, cols] + _dot(qe_ref[rows, cols], st, ((1,), (1,)))
                cur[hh] = st * el[:, hh * LANE:(hh + 1) * LANE] \
                    + _dot(v_ref[rows, cols], kd_ref[rows, cols], ((0,), (0,)))
            return carry

        lax.fori_loop(0, n, step, 0)

    pair = pl.BlockSpec((s, 2 * LANE), lambda h: (0, h))
    return pl.pallas_call(
        body, name=name, grid=(2,),
        in_specs=[pl.BlockSpec((s, 2 * LANE), lambda h: (0, GV // 2 + h)), pair, pair,
                  pl.BlockSpec((n, 2 * LANE), lambda h: (0, h)), pair],
        out_specs=[pair, pl.BlockSpec((2, n, LANE, LANE), lambda h: (h, 0, 0, 0))],
        out_shape=[jax.ShapeDtypeStruct((s, G), f32), jax.ShapeDtypeStruct((4, n, LANE, LANE), f32)],
        scratch_shapes=[pltpu.VMEM((2, LANE, LANE), f32)],
        compiler_params=_cp(("parallel",)),
    )(z, qe, kd, el, oi)


def _gla2_bwd(name, z, qe, kd, el, st, do):
    s = z.shape[0]
    n = s // CHUNK

    def body(v_ref, qe_ref, kd_ref, el_ref, st_ref, do_ref, dqe_ref, dkd_ref, dv_ref, del_ref, dcur):
        dcur[...] = jnp.zeros_like(dcur)

        def step(t, carry):
            c = n - 1 - t
            rows = pl.ds(pl.multiple_of(c * CHUNK, CHUNK), CHUNK)
            el, dels = el_ref[pl.ds(c, 1), :], []
            for hh in range(2):
                cols = pl.ds(hh * LANE, LANE)
                dn = dcur[hh]
                stc = st_ref[hh, c]
                doc = do_ref[rows, cols]
                dqe_ref[rows, cols] = _dot(doc, stc)
                dv_ref[rows, cols] = _dot(kd_ref[rows, cols], dn, ((1,), (1,)))
                dkd_ref[rows, cols] = _dot(v_ref[rows, cols], dn)
                dels.append(jnp.sum(stc * dn, axis=0, keepdims=True))
                dcur[hh] = dn * el[:, hh * LANE:(hh + 1) * LANE] + _dot(doc, qe_ref[rows, cols], ((0,), (0,)))
            del_ref[pl.ds(c, 1), :] = jnp.concatenate(dels, axis=1)
            return carry

        lax.fori_loop(0, n, step, 0)

    pair = pl.BlockSpec((s, 2 * LANE), lambda h: (0, h))
    chunk = pl.BlockSpec((n, 2 * LANE), lambda h: (0, h))
    return pl.pallas_call(
        body, name=name, grid=(2,),
        in_specs=[pl.BlockSpec((s, 2 * LANE), lambda h: (0, GV // 2 + h)), pair, pair, chunk,
                  pl.BlockSpec((2, n, LANE, LANE), lambda h: (h, 0, 0, 0)), pair],
        out_specs=[pair, pair, pair, chunk],
        out_shape=[jax.ShapeDtypeStruct((s, G), f32)] * 3 + [jax.ShapeDtypeStruct((n, G), f32)],
        scratch_shapes=[pltpu.VMEM((2, LANE, LANE), f32)],
        compiler_params=_cp(("parallel",)),
    )(z, qe, kd, el, st, do)


def _gla3_specs(ts, l):
    return [pl.BlockSpec((ts, LANE), lambda h, i: (i, h)),
            pl.BlockSpec((ts, LANE), lambda h, i: (i, GG + h)),
            pl.BlockSpec((None, 1, LANE), lambda h, i: (l, 0, 0))]


def _gla3_fwd(name, o, z, og, l):
    s = z.shape[0]
    ts = _gla_ts(s)

    def body(o_ref, g_ref, og_ref, y_ref):
        y_ref[...] = _gla3_fn(o_ref[...], g_ref[...], og_ref[...]).astype(bf16)

    return pl.pallas_call(
        body, name=name, grid=(4, s // ts), in_specs=_gla3_specs(ts, l),
        out_specs=pl.BlockSpec((ts, LANE), lambda h, i: (i, h)),
        out_shape=jax.ShapeDtypeStruct((s, G), bf16), compiler_params=_cp(("parallel", "parallel")),
    )(o, z, og)


def _gla3_bwd(name, o, z, og, dy, l):
    s = z.shape[0]
    ts = _gla_ts(s)

    def body(o_ref, g_ref, og_ref, dy_ref, do_ref, dg_ref, dog_ref):
        i = pl.program_id(1)
        _, vjp = jax.vjp(_gla3_fn, o_ref[...], g_ref[...], og_ref[...])
        g = vjp(dy_ref[...])
        do_ref[...] = g[0]
        dg_ref[...] = g[1].astype(bf16)

        @pl.when(i == 0)
        def _():
            dog_ref[...] = g[2]

        @pl.when(i > 0)
        def _():
            dog_ref[...] += g[2]

    blk = pl.BlockSpec((ts, LANE), lambda h, i: (i, h))
    return pl.pallas_call(
        body, name=name, grid=(4, s // ts),
        in_specs=_gla3_specs(ts, l) + [pl.BlockSpec((ts, LANE), lambda h, i: (i, 8 + h))],
        out_specs=[blk, blk, pl.BlockSpec((None, 1, LANE), lambda h, i: (h, 0, 0))],
        out_shape=[jax.ShapeDtypeStruct((s, G), f32), jax.ShapeDtypeStruct((s, G), bf16),
                   jax.ShapeDtypeStruct((4, 1, LANE), f32)],
        compiler_params=_cp(("parallel", "arbitrary")),
    )(o, z, og, dy)


def _layer_fwd(l, h, p, weights):
    s = h.shape[0]
    tb = _tm_big(s)
    n = f"l{l}_"
    w_in = weights(l, "w_in", h)
    u = _rmsnorm_fwd(n + "norm_mix", h, p["g_mix"], l)
    z = _mm(n + "mm_in", u, w_in,
            pl.BlockSpec((tb, D), lambda j, i, k: (i, 0)), pl.BlockSpec((D, TZ), lambda j, i, k: (0, j)),
            jax.ShapeDtypeStruct((s, ZC), f32), pl.BlockSpec((tb, TZ), lambda j, i, k: (i, j)),
            (ZC // TZ, s // tb, 1), ((1,), (0,)), 1)
    w_out = weights(l, "w_out", z)
    ya, yb = _convpool_fwd(n + "convpool", z, p["conv_w"], p["pool_w"], p["pool_scale"], l)
    qe, kd, el, oi = _gla1_fwd(n + "gla_chunk", z, p["wdec"], p["bdec"], l)
    o, st = _gla2_fwd(n + "gla_scan", z, qe, kd, el, oi)
    yc = _gla3_fwd(n + "gla_out", o, z, p["gla_og"], l)
    fc, fr = _foxprep_fwd(n + "fox_prep", z, p["fox_bf"], l)
    yd = _fox_fwd(n + "fox_attn", z, fc, fr, p["fox_qg"], p["fox_kg"], l)
    y = jnp.concatenate([ya, yb, yc, yd], axis=1)
    w_gate, w_up, w_down = weights(l, "ffn", y)
    res_tile = lambda: pl.BlockSpec((tb, 1024), lambda j, i, k: (i, j))
    h1 = _mm(n + "mm_out", y, w_out.reshape(D, D),
             pl.BlockSpec((tb, D), lambda j, i, k: (i, 0)), pl.BlockSpec((D, 1024), lambda j, i, k: (0, j)),
             jax.ShapeDtypeStruct((s, D), f32), res_tile(), (2, s // tb, 1), ((1,), (0,)), 1, res=h, res_spec=res_tile())
    u2 = _rmsnorm_fwd(n + "norm_ffn", h1, p["g_ffn"], l)
    gate, up, act = _mm_gate_up(n + "mm_gate_up", u2, w_gate, w_up)
    h2 = _mm(n + "mm_down", act, w_down,
             pl.BlockSpec((tb, FB), lambda j, i, k: (i, k)),
             pl.BlockSpec((None, FB, 1024), lambda j, i, k: (k, 0, j)),
             jax.ShapeDtypeStruct((s, D), f32), res_tile(), (2, s // tb, NCHIP), ((1,), (0,)), NCHIP,
             res=h1, res_spec=res_tile())
    saved = dict(h=h, u=u, z=z, qe=qe, kd=kd, el=el, st=st, o=o, fc=fc, fr=fr, y=y, h1=h1, u2=u2,
                 gate=gate, up=up, act=act, w_in=w_in, w_out=w_out, w_gate=w_gate, w_up=w_up, w_down=w_down)
    return h2, saved


def _mm_tn(name, a, b, ta, tb, out_shape, out_spec, grid):
    s = a.shape[0]
    return _mm(name, a, b, pl.BlockSpec((s, ta), lambda i, j, k: (0, i)), pl.BlockSpec((s, tb), lambda i, j, k: (0, j)),
               out_shape, out_spec, grid, ((0,), (0,)), 1)


def _layer_bwd_ffn(l, dh2, dh2b, p, sv, ship):
    s = dh2.shape[0]
    tb = _tm_big(s)
    n = f"l{l}_bwd_"
    g_wd = _mm_tn(n + "mm_dwd", sv["act"], dh2b, FB, 1024, jax.ShapeDtypeStruct((NCHIP, FB, D), bf16),
                  pl.BlockSpec((None, FB, 1024), lambda i, j, k: (i, 0, j)), (NCHIP, 2, 1))
    dgate, dup = _mm_dact(n + "mm_dact", dh2b, sv["w_down"], sv["gate"], sv["up"])
    wg_shape = jax.ShapeDtypeStruct((NCHIP, D, FB), bf16)
    wg_spec = lambda: pl.BlockSpec((None, 1024, FB), lambda i, j, k: (j, i, 0))
    g_wg = _mm_tn(n + "mm_dwg", sv["u2"], dgate, 1024, FB, wg_shape, wg_spec(), (2, NCHIP, 1))
    g_wu = _mm_tn(n + "mm_dwu", sv["u2"], dup, 1024, FB, wg_shape, wg_spec(), (2, NCHIP, 1))
    token = ship.early(dict(w_gate=g_wg, w_up=g_wu, w_down=g_wd))
    nt_in = lambda: (pl.BlockSpec((tb, FB), lambda j, i, k: (i, k)),
                     pl.BlockSpec((None, 1024, FB), lambda j, i, k: (k, j, 0)))
    nt_out = lambda: (jax.ShapeDtypeStruct((s, D), f32), pl.BlockSpec((tb, 1024), lambda j, i, k: (i, j)))
    du2 = _mm(n + "mm_du2", dgate, sv["w_gate"], *nt_in(), *nt_out(), (2, s // tb, NCHIP), ((1,), (1,)), NCHIP,
              dep=token, second=(dup, sv["w_up"]))
    dh1, dh1b, dg_ffn = _rmsnorm_bwd(n + "norm_ffn", sv["h1"], p["g_ffn"], du2, dh2, l)
    g_wo = _mm_tn(n + "mm_dwo", sv["y"], dh1b, G, 1024, jax.ShapeDtypeStruct((NCHIP, G, D), bf16),
                  pl.BlockSpec((None, G, 1024), lambda i, j, k: (i, 0, j)), (NCHIP, 2, 1))
    token = ship.finish(dict(w_out=g_wo), g_wo)
    dy = _mm(n + "mm_dy", dh1b, sv["w_out"].reshape(D, D),
             pl.BlockSpec((tb, D), lambda j, i, k: (i, 0)), pl.BlockSpec((1024, D), lambda j, i, k: (j, 0)),
             jax.ShapeDtypeStruct((s, D), f32), pl.BlockSpec((tb, 1024), lambda j, i, k: (i, j)),
             (2, s // tb, 1), ((1,), (1,)), 1, dep=token)
    return dh1, dy, dg_ffn


def _layer_bwd_mix(l, dh1, dy, p, sv, ship):
    s = dh1.shape[0]
    n = f"l{l}_bwd_"
    z = sv["z"]
    dcb, dcc, dch, dpu, dconv, dpoolw, dpools = _convpool_bwd(
        n + "convpool", z, p["conv_w"], p["pool_w"], p["pool_scale"], dy, l)
    do, dgg, dog = _gla3_bwd(n + "gla_out", sv["o"], z, p["gla_og"], dy, l)
    dqe, dkd, dvi, del_ = _gla2_bwd(n + "gla_scan", z, sv["qe"], sv["kd"], sv["el"], sv["st"], do)
    dgq, dgk, dgv, dmisc4, dwd, dbd = _gla1_bwd(n + "gla_chunk", z, p["wdec"], p["bdec"], dqe, dkd, del_, do, dvi, l)
    dfq, dfk, dfv, dfc4, dfr4, dqg, dkg = _fox_bwd(n + "fox_attn", z, sv["fc"], sv["fr"], p["fox_qg"], p["fox_kg"], dy, l)
    dmisc, dbf = _foxprep_bwd(n + "fox_prep", z, p["fox_bf"], dfc4, dfr4, dmisc4, l)
    dz = jnp.concatenate([dcb, dcc, dch, dpu, dgq, dgk, dgv, dgg, dfq, dfk, dfv, dmisc], axis=1)
    g_wi = _mm_tn(n + "mm_dwi", sv["u"], dz, 1024, TZ, jax.ShapeDtypeStruct((D, ZC), bf16),
                  pl.BlockSpec((1024, TZ), lambda i, j, k: (i, j)), (2, ZC // TZ, 1))
    token = ship.early(dict(w_in=_win_to_blocks(g_wi)))
    tb = _tm_big(s)
    du = _mm(n + "mm_du", dz, sv["w_in"],
             pl.BlockSpec((tb, TZK), lambda j, i, k: (i, k)), pl.BlockSpec((1024, TZK), lambda j, i, k: (j, k)),
             jax.ShapeDtypeStruct((s, D), f32), pl.BlockSpec((tb, 1024), lambda j, i, k: (i, j)),
             (2, s // tb, ZC // TZK), ((1,), (1,)), ZC // TZK, dep=token)
    token = ship.finish({}, du)
    dh, dhb, dg_mix = _rmsnorm_bwd(n + "norm_mix", sv["h"], p["g_mix"], du, dh1, l, dep=token)
    small = dict(
        norm_mix_g=dg_mix[0], conv_w=dconv, pool_w=dpoolw, pool_scale=dpools[0],
        gla_w_decay=jnp.concatenate([dwd[hh, GA_LANE:GA_LANE + 16, :64] for hh in range(4)], axis=1),
        gla_b_decay=jnp.concatenate([dbd[hh, 0, :64] for hh in range(4)]),
        gla_out_g=jnp.sum(dog[:, 0, :], axis=0), fox_q_g=jnp.sum(dqg[:, 0, :], axis=0),
        fox_k_g=jnp.sum(dkg[:, 0, :], axis=0), fox_b_f=dbf[0, FF_LANE:FF_LANE + 4])
    return dh, dhb, small


def _win_from_blocks(wb):
    def cols(a, b):
        parts = []
        for kk in range(NCHIP):
            lo, hi = max(a, kk * WINB), min(b, (kk + 1) * WINB)
            if lo < hi:
                parts.append(wb[kk, :, lo - kk * WINB:hi - kk * WINB])
        return parts

    zeros = lambda w: [jnp.zeros((wb.shape[1], w), wb.dtype)]
    segs = cols(0, 2048)
    for hh in range(4):
        segs += cols(2048 + 64 * hh, 2112 + 64 * hh) + zeros(64)
    for hh in range(4):
        segs += cols(2304 + 64 * hh, 2368 + 64 * hh) + zeros(64)
    segs += cols(2560, 3584) + cols(3600, 5136)
    segs += cols(5136, 5140) + zeros(GA_LANE - 4) + cols(3584, 3600) + zeros(LANE - GA_LANE - 16)
    return jnp.concatenate(segs, axis=-1)


def _win_to_blocks(g):
    mb = MISC * LANE
    segs = [(0, 2048)] + [(GQ * LANE + LANE * hh, 64) for hh in range(4)] + [(GK * LANE + LANE * hh, 64) for hh in range(4)]
    segs += [(GV * LANE, 1024), (mb + GA_LANE, 16), (FQ * LANE, 1536), (mb + FF_LANE, 4)]
    blocks, at = [[] for _ in range(NCHIP)], 0
    for start, width in segs:
        while width > 0:
            take = min(width, (at // WINB + 1) * WINB - at)
            blocks[at // WINB].append(g[:, start:start + take])
            start, width, at = start + take, width - take, at + take
    return jnp.stack([jnp.concatenate(b, axis=1) for b in blocks])


def _place():
    x, y, c = lax.axis_index("x"), lax.axis_index("y"), lax.axis_index("c")
    chips = [(1 - x, y), (x, 1 - y), (1 - x, 1 - y)]
    return x, y, c, chips


def _allgather_small(name, v, dep):
    m_per, n = v.shape

    def body(x_ref, dep_ref, out_ref, send_sems, recv_sems, local_sem):
        x, y, c, chips = _place()
        me, sibling = (x, y, c), (x, y, 1 - c)

        def rows(px, py, pc):
            return out_ref.at[pl.ds((4 * px + 2 * py + pc) * m_per, m_per), :]

        def copy(k, block, to, src=None):
            return pltpu.make_async_remote_copy(
                src_ref=rows(*block) if src is None else src, dst_ref=rows(*block),
                send_sem=send_sems.at[k], recv_sem=recv_sems.at[k], device_id=to, device_id_type=MESH)

        mine = pltpu.make_async_copy(x_ref, rows(*me), local_sem)
        mine.start()
        first = [copy(0, me, sibling, src=x_ref)]
        first += [copy(1 + j, me, (*chip, c), src=x_ref) for j, chip in enumerate(chips)]
        for cp in first:
            cp.start()
        passed = [copy(4 + j, (*chip, c), sibling) for j, chip in enumerate(chips)]
        for j, chip in enumerate(chips):
            copy(1 + j, (*chip, c), me).wait_recv()
            passed[j].start()
        copy(0, sibling, me).wait_recv()
        for j, chip in enumerate(chips):
            copy(4 + j, (*chip, 1 - c), me).wait_recv()
        for cp in first + passed:
            cp.wait_send()
        mine.wait()

    return pl.pallas_call(
        body, name=name, out_shape=jax.ShapeDtypeStruct((8 * m_per, n), v.dtype),
        in_specs=[pl.BlockSpec(memory_space=pltpu.VMEM), pl.BlockSpec(memory_space=pl.ANY)],
        out_specs=pl.BlockSpec(memory_space=pltpu.VMEM),
        scratch_shapes=[pltpu.SemaphoreType.DMA((7,)), pltpu.SemaphoreType.DMA((7,)), pltpu.SemaphoreType.DMA],
    )(v, dep)


def _hbm_specs(n):
    return [pl.BlockSpec(memory_space=pl.ANY)] * n


def _own_slot(shard, chip):
    return lax.dynamic_update_index_in_dim(lax.empty((NCHIP,) + shard.shape, shard.dtype), shard, chip, 0)


def _cast_slot(name, w, l, pos, dep):
    _, r, cdim = w.shape
    tr = r // 4
    deps = [] if dep is None else [dep]

    def body(pos_ref, w_ref, *rest):
        rest[-1][...] = w_ref[...].astype(bf16)

    return pl.pallas_call(
        body, name=name, out_shape=jax.ShapeDtypeStruct((NCHIP, r, cdim), bf16),
        grid_spec=pltpu.PrefetchScalarGridSpec(
            num_scalar_prefetch=1, grid=(r // tr,),
            in_specs=[pl.BlockSpec((None, tr, cdim), lambda i, p: (l, i, 0))]
            + [pl.BlockSpec((8, LANE), lambda i, p: (0, 0)) for _ in deps],
            out_specs=pl.BlockSpec((None, tr, cdim), lambda i, p: (p[0], i, 0))),
        compiler_params=_cp(("parallel",)),
    )(pos, w, *deps)


def _cast_slot_w_in(name, w, l, pos, dep):
    wt = jnp.transpose(w, (2, 0, 1))
    cols, _, rows = wt.shape
    deps = [] if dep is None else [dep]

    def body(pos_ref, w_ref, *rest):
        rest[-1][...] = jnp.transpose(w_ref[:, l, :]).astype(bf16)

    return pl.pallas_call(
        body, name=name, out_shape=jax.ShapeDtypeStruct((NCHIP, rows, cols), bf16),
        grid_spec=pltpu.PrefetchScalarGridSpec(
            num_scalar_prefetch=1, grid=(pl.cdiv(cols, LANE),),
            in_specs=[pl.BlockSpec((LANE, 2, rows), lambda i, p: (i, 0, 0))]
            + [pl.BlockSpec((8, LANE), lambda i, p: (0, 0)) for _ in deps],
            out_specs=pl.BlockSpec((None, rows, LANE), lambda i, p: (p[0], 0, i))),
        compiler_params=_cp(("parallel",)),
    )(pos, wt, *deps)


HBM = pl.BlockSpec(memory_space=pltpu.HBM)
SEM = pl.BlockSpec(memory_space=pltpu.SEMAPHORE)
EFFECT = pltpu.SideEffectType.DATAFLOW_SIDE_EFFECTING


def _in_hbm(v):
    return pltpu.with_memory_space_constraint(v, pltpu.HBM)


def _gather_start(name, groups):
    flat = [b for g in groups for b in g]
    nt, ng = len(flat), len(groups)

    def body(*refs):
        outs = refs[nt:]
        sems, bufs, token = outs[:2 * ng], outs[2 * ng:2 * ng + nt], outs[2 * ng + nt]
        token[...] = jnp.zeros_like(token)
        x, y, c, chips = _place()
        me = 2 * x + y
        t = 0
        for gi, g in enumerate(groups):
            for k in range(len(g)):
                r = bufs[t].shape[1] // 2
                mine = bufs[t].at[me, pl.ds(c * r, r), :]
                for j, (px, py) in enumerate(chips):
                    pltpu.make_async_remote_copy(
                        src_ref=mine, dst_ref=mine, send_sem=sems[2 * gi].at[3 * k + j], recv_sem=sems[2 * gi + 1].at[3 * k + j],
                        device_id=(px, py, c), device_id_type=MESH).start()
                t += 1

    sem_shapes = []
    for g in groups:
        sem_shapes += [pltpu.SemaphoreType.DMA((3 * len(g),))] * 2
    out = pl.pallas_call(
        body, name=name,
        out_shape=tuple(sem_shapes + [pltpu.HBM(b.shape, b.dtype) for b in flat] + [jax.ShapeDtypeStruct((8, LANE), f32)]),
        in_specs=tuple([HBM] * nt),
        out_specs=tuple([SEM] * (2 * ng) + [HBM] * nt + [pl.BlockSpec(memory_space=pltpu.VMEM)]),
        input_output_aliases={t: 2 * ng + t for t in range(nt)},
        compiler_params=pltpu.CompilerParams(has_side_effects=EFFECT),
    )(*[_in_hbm(b) for b in flat])
    sems = [(out[2 * gi], out[2 * gi + 1]) for gi in range(ng)]
    bufs, at = [], 2 * ng
    for g in groups:
        bufs.append(list(out[at:at + len(g)]))
        at += len(g)
    return sems, bufs, out[at]


def _gather_wait(name, bufs, send_sems, recv_sems, after):
    nt = len(bufs)

    def body(*refs):
        ins, ss, rs = refs[:nt], refs[nt], refs[nt + 1]
        x, y, c, chips = _place()
        me = 2 * x + y
        for k in range(nt):
            r = ins[k].shape[1] // 2
            for j, (px, py) in enumerate(chips):
                cp = pltpu.make_async_remote_copy(
                    src_ref=ins[k].at[me, pl.ds(c * r, r), :], dst_ref=ins[k].at[2 * px + py, pl.ds(c * r, r), :],
                    send_sem=ss.at[3 * k + j], recv_sem=rs.at[3 * k + j], device_id=(px, py, c), device_id_type=MESH)
                cp.wait_send()
                cp.wait_recv()

    out = pl.pallas_call(
        body, name=name, out_shape=tuple(pltpu.HBM(b.shape, b.dtype) for b in bufs),
        in_specs=tuple([HBM] * nt + [SEM, SEM, pl.BlockSpec(memory_space=pl.ANY)]), out_specs=tuple([HBM] * nt),
        input_output_aliases={t: t for t in range(nt)},
        compiler_params=pltpu.CompilerParams(has_side_effects=EFFECT),
    )(*bufs, send_sems, recv_sems, after)
    return list(out)


def _gather_exchange(name, bufs):
    nt = len(bufs)

    def body(*refs):
        outs = refs[nt:2 * nt]
        send_sems, recv_sems = refs[2 * nt:]
        x, y, c, chips = _place()
        sibling = (x, y, 1 - c)

        def half(t, chip_idx, cc):
            r = outs[t].shape[1] // 2
            return outs[t].at[chip_idx, pl.ds(cc * r, r), :]

        sent = []
        for t in range(nt):
            for j, (px, py) in enumerate(chips):
                cp = pltpu.make_async_remote_copy(
                    src_ref=half(t, 2 * px + py, c), dst_ref=half(t, 2 * px + py, c),
                    send_sem=send_sems.at[t, j], recv_sem=recv_sems.at[t, j], device_id=sibling, device_id_type=MESH)
                cp.start()
                sent.append(cp)
        for t in range(nt):
            for j, (px, py) in enumerate(chips):
                pltpu.make_async_remote_copy(
                    src_ref=half(t, 2 * px + py, 1 - c), dst_ref=half(t, 2 * px + py, 1 - c),
                    send_sem=send_sems.at[t, j], recv_sem=recv_sems.at[t, j], device_id=sibling,
                    device_id_type=MESH).wait_recv()
        for cp in sent:
            cp.wait_send()

    return pl.pallas_call(
        body, name=name, out_shape=[jax.ShapeDtypeStruct(v.shape, v.dtype) for v in bufs],
        in_specs=_hbm_specs(nt), out_specs=_hbm_specs(nt), input_output_aliases={t: t for t in range(nt)},
        scratch_shapes=[pltpu.SemaphoreType.DMA((nt, 3)), pltpu.SemaphoreType.DMA((nt, 3))],
    )(*bufs)


def _rs_to_sibling(name, grads):
    nt = len(grads)

    def body(*refs):
        ins, outs = refs[:nt], refs[nt:2 * nt]
        send_sems, recv_sems = refs[2 * nt:]
        x, y, c, _ = _place()
        cps = []
        for t in range(nt):
            r = ins[t].shape[1] // 2
            cp = pltpu.make_async_remote_copy(
                src_ref=ins[t].at[:, pl.ds((1 - c) * r, r), :], dst_ref=outs[t],
                send_sem=send_sems.at[t], recv_sem=recv_sems.at[t], device_id=(x, y, 1 - c), device_id_type=MESH)
            cp.start()
            cps.append(cp)
        for cp in cps:
            cp.wait()

    return pl.pallas_call(
        body, name=name,
        out_shape=[jax.ShapeDtypeStruct((NCHIP, g.shape[1] // 2, g.shape[2]), g.dtype) for g in grads],
        in_specs=_hbm_specs(nt), out_specs=_hbm_specs(nt),
        scratch_shapes=[pltpu.SemaphoreType.DMA((nt,)), pltpu.SemaphoreType.DMA((nt,))],
    )(*grads)


def _sibling_start(name, grads):
    nt = len(grads)

    def body(*refs):
        outs = refs[2 * nt:]
        ss, rs, src, land, token = outs[0], outs[1], outs[2:2 + nt], outs[2 + nt:2 + 2 * nt], outs[2 + 2 * nt]
        x, y, c, _ = _place()
        for t in range(nt):
            r = src[t].shape[1] // 2
            pltpu.make_async_remote_copy(
                src_ref=src[t].at[:, pl.ds((1 - c) * r, r), :], dst_ref=land[t], send_sem=ss.at[t], recv_sem=rs.at[t],
                device_id=(x, y, 1 - c), device_id_type=MESH).start()
        token[...] = jnp.zeros_like(token)

    src_shapes = [pltpu.HBM(g.shape, g.dtype) for g in grads]
    land_shapes = [pltpu.HBM((NCHIP, g.shape[1] // 2, g.shape[2]), g.dtype) for g in grads]
    out = pl.pallas_call(
        body, name=name,
        out_shape=tuple([pltpu.SemaphoreType.DMA((nt,))] * 2 + src_shapes + land_shapes
                        + [jax.ShapeDtypeStruct((8, LANE), f32)]),
        in_specs=tuple([HBM] * (2 * nt)),
        out_specs=tuple([SEM, SEM] + [HBM] * (2 * nt) + [pl.BlockSpec(memory_space=pltpu.VMEM)]),
        input_output_aliases={t: 2 + t for t in range(2 * nt)},
        compiler_params=pltpu.CompilerParams(has_side_effects=EFFECT),
    )(*[_in_hbm(g) for g in grads], *[_in_hbm(lax.empty(s_.shape, s_.dtype)) for s_ in land_shapes])
    return out[0], out[1], list(out[2:2 + nt]), list(out[2 + nt:2 + 2 * nt]), out[2 + 2 * nt]


def _sibling_wait(name, grads, land, send_sems, recv_sems, after):
    nt = len(grads)

    def body(*refs):
        src, dst, ss, rs = refs[:nt], refs[nt:2 * nt], refs[2 * nt], refs[2 * nt + 1]
        x, y, c, _ = _place()
        for t in range(nt):
            r = src[t].shape[1] // 2
            cp = pltpu.make_async_remote_copy(
                src_ref=src[t].at[:, pl.ds((1 - c) * r, r), :], dst_ref=dst[t], send_sem=ss.at[t], recv_sem=rs.at[t],
                device_id=(x, y, 1 - c), device_id_type=MESH)
            cp.wait_send()
            cp.wait_recv()

    shapes = [pltpu.HBM(v.shape, v.dtype) for v in list(grads) + list(land)]
    out = pl.pallas_call(
        body, name=name, out_shape=tuple(shapes),
        in_specs=tuple([HBM] * (2 * nt) + [SEM, SEM, pl.BlockSpec(memory_space=pl.ANY)]),
        out_specs=tuple([HBM] * (2 * nt)), input_output_aliases={t: t for t in range(2 * nt)},
        compiler_params=pltpu.CompilerParams(has_side_effects=EFFECT),
    )(*grads, *land, send_sems, recv_sems, after)
    return list(out[:nt]), list(out[nt:])


def _rs_pair_sum(name, pos, g, other):
    r, cdim = other.shape[1], other.shape[2]
    tr = r // 4 if (r // 4) % 16 == 0 else r // 2
    nblk = r // tr

    def body(pos_ref, g_ref, o_ref, s_ref):
        s_ref[...] = (g_ref[...].astype(f32) + o_ref[...].astype(f32)).astype(bf16)

    blk = pl.BlockSpec((None, tr, cdim), lambda q, i, p: (q, i, 0))
    return pl.pallas_call(
        body, name=name, out_shape=jax.ShapeDtypeStruct(other.shape, bf16),
        grid_spec=pltpu.PrefetchScalarGridSpec(
            num_scalar_prefetch=1, grid=(NCHIP, nblk),
            in_specs=[pl.BlockSpec((None, tr, cdim), lambda q, i, p: (q, p[1] * nblk + i, 0)), blk], out_specs=blk),
        compiler_params=_cp(("parallel", "parallel")),
    )(pos, g, other)


def _scatter_start(name, sums):
    nt = len(sums)

    def body(*refs):
        outs = refs[2 * nt:]
        ss, rs, src, land, token = outs[0], outs[1], outs[2:2 + nt], outs[2 + nt:2 + 2 * nt], outs[2 + 2 * nt]
        x, y, c, chips = _place()
        me = 2 * x + y
        for t in range(nt):
            for j, (px, py) in enumerate(chips):
                pltpu.make_async_remote_copy(
                    src_ref=src[t].at[2 * px + py], dst_ref=land[t].at[me], send_sem=ss.at[3 * t + j], recv_sem=rs.at[3 * t + j],
                    device_id=(px, py, c), device_id_type=MESH).start()
        token[...] = jnp.zeros_like(token)

    shapes = [pltpu.HBM(v.shape, v.dtype) for v in sums]
    out = pl.pallas_call(
        body, name=name,
        out_shape=tuple([pltpu.SemaphoreType.DMA((3 * nt,))] * 2 + shapes + shapes + [jax.ShapeDtypeStruct((8, LANE), f32)]),
        in_specs=tuple([HBM] * (2 * nt)),
        out_specs=tuple([SEM, SEM] + [HBM] * (2 * nt) + [pl.BlockSpec(memory_space=pltpu.VMEM)]),
        input_output_aliases={t: 2 + t for t in range(2 * nt)},
        compiler_params=pltpu.CompilerParams(has_side_effects=EFFECT),
    )(*[_in_hbm(v) for v in sums], *[_in_hbm(lax.empty(v.shape, v.dtype)) for v in sums])
    return out[0], out[1], list(out[2:2 + nt]), list(out[2 + nt:2 + 2 * nt]), out[2 + 2 * nt]


def _scatter_wait(name, sums, land, send_sems, recv_sems, after):
    nt = len(sums)

    def body(*refs):
        src, dst, ss, rs = refs[:nt], refs[nt:2 * nt], refs[2 * nt], refs[2 * nt + 1]
        x, y, c, chips = _place()
        for t in range(nt):
            for j, (px, py) in enumerate(chips):
                cp = pltpu.make_async_remote_copy(
                    src_ref=src[t].at[2 * px + py], dst_ref=dst[t].at[2 * px + py], send_sem=ss.at[3 * t + j],
                    recv_sem=rs.at[3 * t + j], device_id=(px, py, c), device_id_type=MESH)
                cp.wait_send()
                cp.wait_recv()

    shapes = [pltpu.HBM(v.shape, v.dtype) for v in sums]
    out = pl.pallas_call(
        body, name=name, out_shape=tuple(shapes + shapes),
        in_specs=tuple([HBM] * (2 * nt) + [SEM, SEM, pl.BlockSpec(memory_space=pl.ANY)]),
        out_specs=tuple([HBM] * (2 * nt)), input_output_aliases={t: t for t in range(2 * nt)},
        compiler_params=pltpu.CompilerParams(has_side_effects=EFFECT),
    )(*sums, *land, send_sems, recv_sems, after)
    return list(out[:nt]), list(out[nt:])


def _rs_chip_sum(name, pos, sums, parts):
    r, cdim = parts.shape[1], parts.shape[2]
    tr = r // 4 if (r // 4) % 16 == 0 else r // 2
    nblk = r // tr

    def body(pos_ref, own_ref, a_ref, b_ref, c_ref, o_ref):
        o_ref[...] = ((own_ref[...].astype(f32) + a_ref[...].astype(f32)) + b_ref[...].astype(f32)) \
            + c_ref[...].astype(f32)

    def slot(k):
        return pl.BlockSpec((None, tr, cdim), lambda i, p: ((p[0] + k) % NCHIP, i, 0))

    return pl.pallas_call(
        body, name=name, out_shape=jax.ShapeDtypeStruct((2 * r, cdim), f32),
        grid_spec=pltpu.PrefetchScalarGridSpec(
            num_scalar_prefetch=1, grid=(nblk,), in_specs=[slot(0), slot(1), slot(2), slot(3)],
            out_specs=pl.BlockSpec((tr, cdim), lambda i, p: (p[1] * nblk + i, 0))),
        compiler_params=_cp(("parallel",)),
    )(pos, sums, parts, parts, parts)


def _rs_share_halves(name, bufs):
    nt = len(bufs)

    def body(*refs):
        outs = refs[nt:2 * nt]
        send_sems, recv_sems = refs[2 * nt:]
        x, y, c, _ = _place()
        cps = []
        for t in range(nt):
            r = outs[t].shape[0] // 2
            mine = outs[t].at[pl.ds(c * r, r), :]
            theirs = outs[t].at[pl.ds((1 - c) * r, r), :]
            cp = pltpu.make_async_remote_copy(
                src_ref=mine, dst_ref=mine, send_sem=send_sems.at[t], recv_sem=recv_sems.at[t],
                device_id=(x, y, 1 - c), device_id_type=MESH)
            cp.start()
            cps.append((cp, theirs))
        for t, (cp, theirs) in enumerate(cps):
            pltpu.make_async_remote_copy(
                src_ref=theirs, dst_ref=theirs, send_sem=send_sems.at[t], recv_sem=recv_sems.at[t],
                device_id=(x, y, 1 - c), device_id_type=MESH).wait_recv()
            cp.wait_send()

    return pl.pallas_call(
        body, name=name, out_shape=[jax.ShapeDtypeStruct(v.shape, v.dtype) for v in bufs],
        in_specs=_hbm_specs(nt), out_specs=_hbm_specs(nt), input_output_aliases={t: t for t in range(nt)},
        scratch_shapes=[pltpu.SemaphoreType.DMA((nt,)), pltpu.SemaphoreType.DMA((nt,))],
    )(*bufs)


GROUPS = (("w_in",), ("w_out",), ("w_gate", "w_up", "w_down"))


class _Ship:
    def __init__(self, tag, pos):
        self.tag, self.pos, self.started, self.state = tag, pos, None, None

    def early(self, grads):
        self.early_keys = tuple(grads)
        self.started = _sibling_start(self.tag + "sibling_start", [grads[k] for k in self.early_keys])
        return self.started[-1]

    def finish(self, grads, after):
        keys, mine, got = tuple(grads), [grads[k] for k in grads], []
        if keys:
            got = list(_rs_to_sibling(self.tag + "to_sibling", mine))
        if self.started is not None:
            send_sems, recv_sems, src, land, _ = self.started
            src, land = _sibling_wait(self.tag + "sibling_wait", src, land, send_sems, recv_sems, after)
            keys, mine, got = self.early_keys + keys, src + mine, land + got
        sums = [_rs_pair_sum(self.tag + "pair_sum_" + k, self.pos, g, o) for k, g, o in zip(keys, mine, got)]
        self.state = (keys,) + _scatter_start(self.tag + "start", sums)
        return self.state[-1]


def _rs_end(tag, state, pos, after):
    keys, send_sems, recv_sems, sums, land, _ = state
    sums, land = _scatter_wait(tag + "wait", sums, land, send_sems, recv_sems, after)
    halves = [_rs_chip_sum(tag + "chip_sum_" + k, pos, s, v) for k, s, v in zip(keys, sums, land)]
    return dict(zip(keys, _rs_share_halves(tag + "share", halves)))


def _adam_math(w, g, m, v):
    m = ADAM_B1 * m + (1.0 - ADAM_B1) * g
    v = ADAM_B2 * v + (1.0 - ADAM_B2) * (g * g)
    m_hat = m / (1.0 - ADAM_B1 ** ADAM_STEP)
    v_hat = v / (1.0 - ADAM_B2 ** ADAM_STEP)
    delta = -ADAM_LR * (m_hat / (jnp.sqrt(v_hat) + ADAM_EPS) + ADAM_WD * w)
    return delta, m, v


def _adam_big(name, g0, g1, w, m, v):
    _, r, cdim = w.shape
    tr = 256 if r % 256 == 0 else 128
    nb = r // tr

    def body(g0_ref, g1_ref, w_ref, m_ref, v_ref, go_ref, d_ref, mo_ref, vo_ref):
        l = pl.program_id(0)
        g = jnp.where(l == 0, g0_ref[...], g1_ref[...])
        delta, mn, vn = _adam_math(w_ref[...], g, m_ref[...], v_ref[...])
        go_ref[...] = g
        d_ref[...] = delta
        mo_ref[...] = mn
        vo_ref[...] = vn

    lay = pl.BlockSpec((None, tr, cdim), lambda l, i: (l, i, 0))
    return pl.pallas_call(
        body, name=name, grid=(2, nb),
        in_specs=[pl.BlockSpec((tr, cdim), lambda l, i: (i * (1 - l) + (nb - 1) * l, 0)),
                  pl.BlockSpec((tr, cdim), lambda l, i: (i * l, 0)), lay, lay, lay],
        out_specs=[lay] * 4, out_shape=[jax.ShapeDtypeStruct(w.shape, f32)] * 4,
        compiler_params=_cp(("arbitrary", "arbitrary")),
    )(g0, g1, w, m, v)


def _adam_w_in(name, g0, g1, w, m, v):
    wt, mt, vt = (jnp.transpose(a, (2, 0, 1)) for a in (w, m, v))
    cols, _, rows = wt.shape

    def body(g0_ref, g1_ref, w_ref, m_ref, v_ref, go_ref, d_ref, mo_ref, vo_ref):
        for l, g_ref in enumerate((g0_ref, g1_ref)):
            g = jnp.transpose(g_ref[...])
            delta, mn, vn = _adam_math(w_ref[:, l, :], g, m_ref[:, l, :], v_ref[:, l, :])
            go_ref[:, l, :] = g
            d_ref[:, l, :] = delta
            mo_ref[:, l, :] = mn
            vo_ref[:, l, :] = vn

    blk = pl.BlockSpec((LANE, 2, rows), lambda i: (i, 0, 0))
    gblk = pl.BlockSpec((rows, LANE), lambda i: (0, i))
    outs = pl.pallas_call(
        body, name=name, grid=(pl.cdiv(cols, LANE),), in_specs=[gblk, gblk, blk, blk, blk], out_specs=[blk] * 4,
        out_shape=[jax.ShapeDtypeStruct(wt.shape, f32)] * 4, compiler_params=_cp(("parallel",)),
    )(g0, g1, wt, mt, vt)
    return tuple(jnp.transpose(o, (1, 2, 0)) for o in outs)


def _sum8(name, gathered):
    m_per = gathered.shape[0] // 8

    def body(g_ref, o_ref):
        tot = g_ref[pl.ds(0, m_per), :]
        for d in range(1, 8):
            tot = tot + g_ref[pl.ds(d * m_per, m_per), :]
        o_ref[...] = tot

    return pl.pallas_call(body, name=name, out_shape=jax.ShapeDtypeStruct((m_per, LANE), f32))(gathered)


def _adam_small(name, g, w, m, v):
    def body(g_ref, w_ref, m_ref, v_ref, d_ref, mo_ref, vo_ref):
        delta, mn, vn = _adam_math(w_ref[...], g_ref[...], m_ref[...], v_ref[...])
        d_ref[...] = delta
        mo_ref[...] = mn
        vo_ref[...] = vn

    return pl.pallas_call(body, name=name, out_shape=[jax.ShapeDtypeStruct(g.shape, f32)] * 3)(g, w, m, v)


def _pack(vals):
    rows, offs, at = [], [], 0
    for a in vals:
        a = a.reshape(-1)
        n = -(-a.shape[0] // (8 * LANE)) * 8
        rows.append(jnp.pad(a, (0, n * LANE - a.shape[0])).reshape(n, LANE))
        offs.append(at)
        at += n
    return jnp.concatenate(rows, axis=0), offs


def _unpack(packed, offs, shapes):
    out = []
    for o, shp in zip(offs, shapes):
        size = 1
        for d in shp:
            size *= d
        n = -(-size // LANE)
        out.append(packed[o:o + n].reshape(-1)[:size].reshape(shp))
    return out


SMALL = ("norm_mix_g", "conv_w", "pool_w", "pool_scale", "gla_w_decay", "gla_b_decay", "gla_out_g",
         "fox_q_g", "fox_k_g", "fox_b_f", "norm_ffn_g")
ALL = ("norm_mix_g", "w_in", "conv_w", "pool_w", "pool_scale", "gla_w_decay", "gla_b_decay", "gla_out_g",
       "fox_q_g", "fox_k_g", "fox_b_f", "w_out", "norm_ffn_g", "w_gate", "w_up", "w_down")


def kernel(x, norm_mix_g, w_in, conv_w, pool_w, pool_scale, gla_w_decay, gla_b_decay, gla_out_g, fox_q_g, fox_k_g, fox_b_f, w_out, norm_ffn_g, w_gate, w_up, w_down, loss_target, m_norm_mix_g, m_w_in, m_conv_w, m_pool_w, m_pool_scale, m_gla_w_decay, m_gla_b_decay, m_gla_out_g, m_fox_q_g, m_fox_k_g, m_fox_b_f, m_w_out, m_norm_ffn_g, m_w_gate, m_w_up, m_w_down, v_norm_mix_g, v_w_in, v_conv_w, v_pool_w, v_pool_scale, v_gla_w_decay, v_gla_b_decay, v_gla_out_g, v_fox_q_g, v_fox_k_g, v_fox_b_f, v_w_out, v_norm_ffn_g, v_w_gate, v_w_up, v_w_down):
    w = dict(norm_mix_g=norm_mix_g, w_in=w_in, conv_w=conv_w, pool_w=pool_w, pool_scale=pool_scale,
             gla_w_decay=gla_w_decay, gla_b_decay=gla_b_decay, gla_out_g=gla_out_g, fox_q_g=fox_q_g, fox_k_g=fox_k_g,
             fox_b_f=fox_b_f, w_out=w_out, norm_ffn_g=norm_ffn_g, w_gate=w_gate, w_up=w_up, w_down=w_down)
    m = dict(norm_mix_g=m_norm_mix_g, w_in=m_w_in, conv_w=m_conv_w, pool_w=m_pool_w, pool_scale=m_pool_scale,
             gla_w_decay=m_gla_w_decay, gla_b_decay=m_gla_b_decay, gla_out_g=m_gla_out_g, fox_q_g=m_fox_q_g,
             fox_k_g=m_fox_k_g, fox_b_f=m_fox_b_f, w_out=m_w_out, norm_ffn_g=m_norm_ffn_g, w_gate=m_w_gate,
             w_up=m_w_up, w_down=m_w_down)
    v = dict(norm_mix_g=v_norm_mix_g, w_in=v_w_in, conv_w=v_conv_w, pool_w=v_pool_w, pool_scale=v_pool_scale,
             gla_w_decay=v_gla_w_decay, gla_b_decay=v_gla_b_decay, gla_out_g=v_gla_out_g, fox_q_g=v_fox_q_g,
             fox_k_g=v_fox_k_g, fox_b_f=v_fox_b_f, w_out=v_w_out, norm_ffn_g=v_norm_ffn_g, w_gate=v_w_gate,
             w_up=v_w_up, w_down=v_w_down)
    chip = 2 * lax.axis_index("x") + lax.axis_index("y")

    pos = jnp.stack([chip, lax.axis_index("c")]).astype(jnp.int32)

    mine, offs = _pack([conv_w, gla_w_decay, jnp.zeros((8, LANE), f32)])

    def casts(l, keys, dep):
        return [(_cast_slot_w_in if k == "w_in" else _cast_slot)(f"l{l}_cast_{k}", w[k], l, pos, dep) for k in keys]

    sems, gbufs, token = _gather_start("gather_start", [[_own_slot(mine, chip)], casts(0, ("w_in",), None)])
    more = [casts(0, grp, token) for grp in GROUPS[1:]]
    sems_b, gbufs_b, token = _gather_start("l0_gather_start_rest", more)
    more = [casts(1, grp, token) for grp in GROUPS]
    sems_c, gbufs_c, started1 = _gather_start("l1_gather_start", more)
    sems, gbufs = sems + sems_b + sems_c, gbufs + gbufs_b + gbufs_c

    def gathered(tag, gi, after):
        got = _gather_wait(tag + "_wait", gbufs[gi], sems[gi][0], sems[gi][1], after)
        return _gather_exchange(tag + "_exchange", got)

    def weights(l, group, after):
        if (l, group) == (0, "w_in"):
            after = started1
        got = gathered(f"l{l}_gather_{group}", 1 + 3 * l + ("w_in", "w_out", "ffn").index(group), after)
        if group == "w_in":
            return _win_from_blocks(got[0])
        return got[0] if group == "w_out" else got

    every = gathered("gather_small", 0, x)[0]
    per_chip = [_unpack(every[kk], offs, [conv_w.shape, gla_w_decay.shape]) for kk in range(NCHIP)]
    conv_full = jnp.concatenate([pc[0] for pc in per_chip], axis=-1)[:, :, 0, :]
    wdec_full = jnp.concatenate([pc[1] for pc in per_chip], axis=-1)

    wdec = jnp.pad(wdec_full.reshape(2, 16, 4, 64), ((0, 0), (GA_LANE, LANE - GA_LANE - 16), (0, 0), (0, 64)))
    p = dict(
        g_mix=norm_mix_g[:, None, :], g_ffn=norm_ffn_g[:, None, :],
        conv_w=conv_full, pool_w=pool_w, pool_scale=pool_scale[:, None, :],
        wdec=wdec.reshape(2, LANE, G),
        bdec=jnp.pad(gla_b_decay.reshape(2, 4, 64), ((0, 0), (0, 0), (0, 64))).reshape(2, 1, G),
        gla_og=gla_out_g[:, None, :], fox_qg=fox_q_g[:, None, :], fox_kg=fox_k_g[:, None, :],
        fox_bf=jnp.pad(fox_b_f, ((0, 0), (FF_LANE, LANE - FF_LANE - 4)))[:, None, :])

    h0 = x[0]
    h1, sv0 = _layer_fwd(0, h0, p, weights)
    h2, sv1 = _layer_fwd(1, h1, p, weights)
    sq, dh, dhb = _loss("loss", h2, loss_target[0])
    loss = lax.psum(sq[0, 0] * (0.5 / D), ("x", "y", "c"))

    late = ("w_out", "w_gate", "w_up", "w_down")
    ships = {tag: _Ship(tag, pos) for tag in ("l1_rs_a_", "l1_rs_b_", "l0_rs_a_", "l0_rs_b_")}
    dh1, dy, dgf1 = _layer_bwd_ffn(1, dh, dhb, p, sv1, ships["l1_rs_a_"])
    dh, dhb, small1 = _layer_bwd_mix(1, dh1, dy, p, sv1, ships["l1_rs_b_"])
    dh1, dy, dgf0 = _layer_bwd_ffn(0, dh, dhb, p, sv0, ships["l0_rs_a_"])
    dh, dhb, small0 = _layer_bwd_mix(0, dh1, dy, p, sv0, ships["l0_rs_b_"])
    small0["norm_ffn_g"], small1["norm_ffn_g"] = dgf0[0], dgf1[0]

    red1 = _rs_end("l1_rs_a_", ships["l1_rs_a_"].state, pos, dh)
    red1.update(_rs_end("l1_rs_b_", ships["l1_rs_b_"].state, pos, red1["w_down"]))
    red0 = _rs_end("l0_rs_a_", ships["l0_rs_a_"].state, pos, red1["w_in"])
    grads, deltas, new_m, new_v = {}, {}, {}, {}
    for k in late:
        grads[k], deltas[k], new_m[k], new_v[k] = _adam_big("adam_" + k, red0[k], red1[k], w[k], m[k], v[k])

    packed, goffs = _pack([jnp.stack([small0[k], small1[k]]) for k in SMALL])
    total = _sum8("sum_small_grads", _allgather_small("gather_small_grads", packed, new_v["w_down"]))
    red0.update(_rs_end("l0_rs_b_", ships["l0_rs_b_"].state, pos, total))
    k = "w_in"
    grads[k], deltas[k], new_m[k], new_v[k] = _adam_w_in("adam_" + k, red0[k], red1[k], w[k], m[k], v[k])
    full_shapes = [(2,) + small0[k].shape for k in SMALL]
    gsmall = dict(zip(SMALL, _unpack(total, goffs, full_shapes)))
    gsmall["conv_w"] = lax.dynamic_slice_in_dim(gsmall["conv_w"], chip * LANE, LANE, axis=2)[:, :, None, :]
    gsmall["gla_w_decay"] = lax.dynamic_slice_in_dim(gsmall["gla_w_decay"], chip * 64, 64, axis=2)
    gp, loffs = _pack([gsmall[k] for k in SMALL])
    wp, _ = _pack([w[k] for k in SMALL])
    mp, _ = _pack([m[k] for k in SMALL])
    vp, _ = _pack([v[k] for k in SMALL])
    dp, mnp, vnp = _adam_small("adam_small", gp, wp, mp, vp)
    shapes = [w[k].shape for k in SMALL]
    for k, a, b, c_, d_ in zip(SMALL, _unpack(gp, loffs, shapes), _unpack(dp, loffs, shapes),
                               _unpack(mnp, loffs, shapes), _unpack(vnp, loffs, shapes)):
        grads[k], deltas[k], new_m[k], new_v[k] = a, b, c_, d_

    return (loss, dh[None], *[grads[k] for k in ALL], *[deltas[k] for k in ALL],
            *[new_m[k] for k in ALL], *[new_v[k] for k in ALL])
```
